```python
import math
import jax, jax.numpy as jnp
from jax import lax
import numpy as np


D_MODEL = 1024
BATCH = 4
SEQ = 4096
DEPTH = 4

GRID_W = 64
CTX_LEN = 256
N_BRANCH = 4
BRANCH_W = D_MODEL // 2

GDN_HEADS = 4
GDN_DK = BRANCH_W // GDN_HEADS
GDN_DV = BRANCH_W // GDN_HEADS
GDN_CONV = 4
GDN_CHUNK = 64

HY_W = BRANCH_W
HY_ORDER = 2
HY_CONV = 3
HY_EMB = 33
HY_BANDS = (HY_EMB - 1) // 2
HY_FH = 64
HY_DECAY_TARGET = 1e-2
HY_FAST_PCT = 0.3
HY_SLOW_PCT = 1.5
HY_MIN_DECAY = math.log(HY_DECAY_TARGET) / HY_SLOW_PCT
HY_MAX_DECAY = math.log(HY_DECAY_TARGET) / HY_FAST_PCT

GQA_HEADS = 4
GQA_KV = 2
GQA_HD = BRANCH_W // GQA_HEADS

DIFF_HEADS = 4
DIFF_HD = BRANCH_W // DIFF_HEADS
DIFF_QK = DIFF_HD // 2

Q_BLOCK = 128
ROPE_THETA = 10000.0
EPS = 1e-6
ALPHA = (2.0 * DEPTH) ** 0.25
BETA = (8.0 * DEPTH) ** -0.25

IN_SPLITS = (
    ('gdn_qkv', 3 * GDN_HEADS * GDN_DK), ('gdn_a', 2 * GDN_HEADS), ('gdn_b', 2 * GDN_HEADS), ('gdn_gate', BRANCH_W),
    ('hy_xv', 3 * HY_W), ('hy_gate', HY_W),
    ('gqa_q', GQA_HEADS * GQA_HD), ('gqa_k', GQA_KV * GQA_HD), ('gqa_v', GQA_KV * GQA_HD), ('gqa_gate', BRANCH_W),
    ('diff_q', DIFF_HEADS * 2 * DIFF_QK), ('diff_k', DIFF_HEADS * 2 * DIFF_QK), ('diff_v', DIFF_HEADS * DIFF_HD), ('diff_gate', BRANCH_W),
    ('merge', N_BRANCH * D_MODEL),
)
N_IN = (3 * GDN_HEADS * GDN_DK + 4 * GDN_HEADS + BRANCH_W + 4 * HY_W + 2 * GQA_HEADS * GQA_HD + 2 * GQA_KV * GQA_HD
        + 4 * DIFF_HEADS * DIFF_QK + DIFF_HEADS * DIFF_HD + BRANCH_W + N_BRANCH * D_MODEL)

kernel_name = 'hybrid_gated_merge_diffusion_trunk'

F32 = jnp.float32


def layer_norm(x):
    xf = x.astype(F32)
    mu = jnp.mean(xf, -1, keepdims=True)
    var = jnp.mean(jnp.square(xf - mu), -1, keepdims=True)
    return ((xf - mu) * lax.rsqrt(var + EPS)).astype(x.dtype)


def rms_norm(x, w):
    xf = x.astype(F32)
    y = xf * lax.rsqrt(jnp.mean(xf * xf, -1, keepdims=True) + EPS)
    return (y * w.astype(F32)).astype(x.dtype)


def l2norm(x):
    xf = x.astype(F32)
    return xf * lax.rsqrt(jnp.sum(xf * xf, -1, keepdims=True) + EPS)


def depthwise_conv(x, w):
    k = w.shape[0]
    pad_l = (k - 1) // 2
    return lax.conv_general_dilated(x, w.astype(x.dtype)[:, None, :], window_strides=(1,),
                                    padding=[(pad_l, k - 1 - pad_l)],
                                    dimension_numbers=('NWC', 'WIO', 'NWC'),
                                    feature_group_count=x.shape[-1])


def split_proj(p):
    out = {}
    off = 0
    for name, size in IN_SPLITS:
        out[name] = p[..., off:off + size]
        off += size
    return out


def axial_rope_tables(row, col, dim):
    half = dim // 2
    inv = ROPE_THETA ** (-jnp.arange(0, half, 2, dtype=F32) / half)
    ang = jnp.concatenate([row[:, None] * inv, col[:, None] * inv], -1)
    return jnp.cos(ang), jnp.sin(ang)


def apply_rope(x, cos, sin):
    xf = x.astype(F32)
    x1, x2 = xf[..., 0::2], xf[..., 1::2]
    cs, sn = cos[None, :, None, :], sin[None, :, None, :]
    return jnp.stack([x1 * cs - x2 * sn, x1 * sn + x2 * cs], -1).reshape(x.shape).astype(x.dtype)


def adaln_modulate(h, mod):
    shift, scale, gate = jnp.split(mod, 3, axis=-1)
    return layer_norm(h) * (1 + scale) + shift, gate


def post_norm(h, f, g, b):
    return layer_norm(ALPHA * h + f) * g + b


def gdn_prep(p, conv_w, a_log, dt_bias):
    qkv = jax.nn.silu(depthwise_conv(p['gdn_qkv'], conv_w))
    bsz, n = qkv.shape[:2]
    q, k, v = jnp.split(qkv, 3, axis=-1)
    q = l2norm(q.reshape(bsz, n, GDN_HEADS, GDN_DK)) * GDN_DK ** -0.5
    k = l2norm(k.reshape(bsz, n, GDN_HEADS, GDN_DK))
    v = v.reshape(bsz, n, GDN_HEADS, GDN_DV).astype(F32)
    a = p['gdn_a'].astype(F32).reshape(bsz, n, 2, GDN_HEADS)
    b = p['gdn_b'].astype(F32).reshape(bsz, n, 2, GDN_HEADS)
    g = -jnp.exp(a_log.astype(F32)) * jax.nn.softplus(a + dt_bias.astype(F32))
    return q, k, v, g, jax.nn.sigmoid(b)


def gdn_chunked(q, k, v, g, beta, s0, want_out):
    bsz, n_tok, nh, dk = q.shape
    dv = v.shape[-1]
    c = GDN_CHUNK
    nc = n_tok // c

    def chunks(t):
        return jnp.transpose(t.astype(F32).reshape(bsz, nc, c, nh, -1), (1, 0, 3, 2, 4))

    q, k, v = chunks(q), chunks(k), chunks(v)
    g = chunks(g[..., None])[..., 0]
    beta = chunks(beta[..., None])[..., 0]
    cum = jnp.cumsum(g, axis=-1)
    incl = jnp.tril(jnp.ones((c, c), dtype=bool))
    strict = jnp.tril(jnp.ones((c, c), dtype=bool), -1)
    decay = jnp.exp(jnp.where(incl, cum[..., :, None] - cum[..., None, :], -jnp.inf))
    kk = jnp.einsum('nbhid,nbhjd->nbhij', k, k)
    t_mat = jnp.where(strict, beta[..., :, None] * kk * decay, 0.0) + jnp.eye(c, dtype=F32)
    rhs = jnp.concatenate([k * (beta * jnp.exp(cum))[..., None], v * beta[..., None]], -1)
    sol = lax.linalg.triangular_solve(t_mat, rhs, left_side=True, lower=True, unit_diagonal=True)
    w_c, u_c = sol[..., :dk], sol[..., dk:]
    k_tail = k * jnp.exp(cum[..., -1:] - cum)[..., None]
    g_last = jnp.exp(cum[..., -1])
    xs = (w_c, u_c, k_tail, g_last)
    if want_out:
        q_dec = q * jnp.exp(cum)[..., None]
        qk = jnp.einsum('nbhid,nbhjd->nbhij', q, k) * decay
        xs = xs + (q_dec, qk)

    def step(s, inp):
        wc, uc, kt, gl = inp[:4]
        v_new = uc - jnp.einsum('bhck,bhkv->bhcv', wc, s)
        s_next = s * gl[..., None, None] + jnp.einsum('bhck,bhcv->bhkv', kt, v_new)
        if want_out:
            qd, qkc = inp[4], inp[5]
            o = jnp.einsum('bhck,bhkv->bhcv', qd, s) + jnp.einsum('bhcj,bhjv->bhcv', qkc, v_new)
            return s_next, o
        return s_next, None

    s_fin, o = lax.scan(step, s0, xs)
    if want_out:
        o = jnp.transpose(o, (1, 0, 3, 2, 4)).reshape(bsz, n_tok, nh, dv)
    return o, s_fin


def gdn_bidir(q, k, v, g, beta, s0_f, s0_b, want_out):
    flip = lambda t: t[:, ::-1]
    o_f, s_f = gdn_chunked(q, k, v, g[:, :, 0], beta[:, :, 0], s0_f, want_out)
    o_b, s_b = gdn_chunked(flip(q), flip(k), flip(v), flip(g[:, :, 1]), flip(beta[:, :, 1]), s0_b, want_out)
    o = o_f + flip(o_b) if want_out else None
    return o, s_f, s_b


def gdn_output(o, gate, w):
    y = rms_norm(o, w)
    return y.reshape(*gate.shape).astype(gate.dtype) * jax.nn.silu(gate)


def hyena_filter_spectrum(n, w1, b1, w2, b2, w3, b3, w4, freq):
    pos = jnp.arange(n, dtype=F32)
    t = pos / max(n - 1, 1)
    ang = (2.0 * math.pi / n) * pos[:, None] * jnp.linspace(1e-4, HY_BANDS - 1, HY_BANDS, dtype=F32)
    z = jnp.concatenate([t[:, None], jnp.cos(ang), -jnp.sin(ang)], -1)
    fr = freq.astype(F32)
    h = jnp.sin(fr * (z @ w1.astype(F32) + b1.astype(F32)))
    h = jnp.sin(fr * (h @ w2.astype(F32) + b2.astype(F32)))
    h = jnp.sin(fr * (h @ w3.astype(F32) + b3.astype(F32)))
    h = (h @ w4.astype(F32)).reshape(n, HY_ORDER, 2, HY_W)
    deltas = jnp.abs(jnp.linspace(HY_MIN_DECAY, HY_MAX_DECAY, HY_W, dtype=F32))
    h = h * jnp.exp(-t[:, None] * deltas)[:, None, None, :]
    full = jnp.concatenate([h[:, :, 0], jnp.zeros((1, HY_ORDER, HY_W), F32), h[:0:-1, :, 1]], 0)
    return jnp.fft.rfft(full, axis=0)


def hyena_mix(xv, gate, conv_w, spec, bias):
    x1, x2, v = jnp.split(depthwise_conv(xv, conv_w), 3, axis=-1)
    n = v.shape[1]

    def long_conv(z, o):
        zf = z.astype(F32)
        y = jnp.fft.irfft(jnp.fft.rfft(zf, n=2 * n, axis=1) * spec[None, :, o, :], n=2 * n, axis=1)[:, :n]
        return (y + zf * bias[o].astype(F32)).astype(z.dtype)

    z = x1 * long_conv(v, 0)
    z = x2 * long_conv(z, 1)
    return z * jax.nn.silu(gate)


def sweep_blocks(fn, q):
    bsz, n = q.shape[:2]
    nb = n // Q_BLOCK
    qb = jnp.moveaxis(q.reshape(bsz, nb, Q_BLOCK, *q.shape[2:]), 1, 0)
    out = lax.map(fn, qb)
    return jnp.moveaxis(out, 0, 1).reshape(bsz, n, *out.shape[3:])


def gqa_q(p, qn, rope):
    bsz, n = p['gqa_q'].shape[:2]
    q = rms_norm(p['gqa_q'].reshape(bsz, n, GQA_HEADS, GQA_HD), qn)
    return apply_rope(q, *rope) if rope is not None else q


def gqa_kv(p, kn, rope):
    bsz, n = p['gqa_k'].shape[:2]
    k = rms_norm(p['gqa_k'].reshape(bsz, n, GQA_KV, GQA_HD), kn)
    if rope is not None:
        k = apply_rope(k, *rope)
    return k, p['gqa_v'].reshape(bsz, n, GQA_KV, GQA_HD)


def gqa_attend(q, k, v):
    bsz, nq = q.shape[:2]
    qg = q.reshape(bsz, nq, GQA_KV, GQA_HEADS // GQA_KV, GQA_HD)
    s = jnp.einsum('bqkgd,bnkd->bkgqn', qg, k, preferred_element_type=F32) * GQA_HD ** -0.5
    pr = jax.nn.softmax(s, axis=-1).astype(v.dtype)
    return jnp.einsum('bkgqn,bnkd->bqkgd', pr, v).reshape(bsz, nq, GQA_HEADS * GQA_HD)


def diff_q(p, rope):
    bsz, n = p['diff_q'].shape[:2]
    q = p['diff_q'].reshape(bsz, n, DIFF_HEADS * 2, DIFF_QK)
    if rope is not None:
        q = apply_rope(q, *rope)
    return q.reshape(bsz, n, DIFF_HEADS, 2, DIFF_QK)


def diff_kv(p, rope):
    bsz, n = p['diff_k'].shape[:2]
    k = p['diff_k'].reshape(bsz, n, DIFF_HEADS * 2, DIFF_QK)
    if rope is not None:
        k = apply_rope(k, *rope)
    return k.reshape(bsz, n, DIFF_HEADS, 2, DIFF_QK), p['diff_v'].reshape(bsz, n, DIFF_HEADS, DIFF_HD)


def diff_attend(q, k, v, lam):
    s = jnp.einsum('bqhcd,bnhcd->bhcqn', q, k, preferred_element_type=F32) * DIFF_QK ** -0.5
    pr = jax.nn.softmax(s, axis=-1)
    w = (pr[:, :, 0] - lam * pr[:, :, 1]).astype(v.dtype)
    return jnp.einsum('bhqn,bnhd->bqhd', w, v)


def diff_output(o, gate, w, lam_init):
    y = rms_norm(o, w) * (1.0 - lam_init)
    return y.reshape(*gate.shape) * jax.nn.silu(gate)


def merge_branches(ys, merge_logits, w_br, w_out):
    y = jnp.stack(ys, axis=-2)
    proj = jnp.einsum('blnw,nwd->blnd', y, w_br)
    g = jax.nn.sigmoid(merge_logits.reshape(*merge_logits.shape[:-1], N_BRANCH, D_MODEL))
    return jnp.sum(g * proj, axis=-2) @ w_out


def setup_inputs(seed: int = 0) -> dict:
    key = jax.random.key(seed)
    ks = iter(jax.random.split(key, 40))
    nrm = lambda shape, std: std * jax.random.normal(next(ks), shape, F32)
    x = nrm((BATCH, SEQ, D_MODEL), 1.0)
    c = nrm((BATCH, D_MODEL), 1.0)
    ctx = nrm((BATCH, CTX_LEN, D_MODEL), 1.0)
    c_ctx = nrm((D_MODEL,), 1.0)
    w_ada = nrm((DEPTH, D_MODEL, 3 * D_MODEL), 0.5 * D_MODEL ** -0.5)
    b_ada = nrm((DEPTH, 3 * D_MODEL), 0.02)
    w_in = nrm((DEPTH, D_MODEL, N_IN), D_MODEL ** -0.5)
    gdn_conv = nrm((DEPTH, GDN_CONV, 3 * GDN_HEADS * GDN_DK), GDN_CONV ** -0.5)
    gdn_a_log = jnp.log(jax.random.uniform(next(ks), (DEPTH, 2, GDN_HEADS), F32, 1.0, 16.0))
    dt = jnp.exp(jax.random.uniform(next(ks), (DEPTH, 2, GDN_HEADS), F32, math.log(1e-3), math.log(1e-1)))
    gdn_dt_bias = dt + jnp.log(-jnp.expm1(-dt))
    gdn_norm = 1.0 + nrm((DEPTH, GDN_DV), 0.02)
    hy_conv = nrm((DEPTH, HY_CONV, 3 * HY_W), HY_CONV ** -0.5)
    hy_w1 = nrm((DEPTH, HY_EMB, HY_FH), HY_EMB ** -0.5)
    hy_b1 = nrm((DEPTH, HY_FH), 0.1)
    hy_w2 = nrm((DEPTH, HY_FH, HY_FH), HY_FH ** -0.5)
    hy_b2 = nrm((DEPTH, HY_FH), 0.1)
    hy_w3 = nrm((DEPTH, HY_FH, HY_FH), HY_FH ** -0.5)
    hy_b3 = nrm((DEPTH, HY_FH), 0.1)
    hy_w4 = nrm((DEPTH, HY_FH, HY_ORDER * 2 * HY_W), 0.1 * HY_FH ** -0.5)
    hy_freq = 1.0 + nrm((DEPTH, HY_FH), 0.1)
    hy_bias = nrm((DEPTH, HY_ORDER, HY_W), 0.1)
    gqa_qn = 1.0 + nrm((DEPTH, GQA_HD), 0.02)
    gqa_kn = 1.0 + nrm((DEPTH, GQA_HD), 0.02)
    diff_lam = nrm((DEPTH, 4, DIFF_QK), 0.1)
    diff_norm = 1.0 + nrm((DEPTH, DIFF_HD), 0.02)
    w_br = nrm((DEPTH, N_BRANCH, BRANCH_W, D_MODEL), BETA * BRANCH_W ** -0.5)
    w_out = nrm((DEPTH, D_MODEL, D_MODEL), BETA * D_MODEL ** -0.5)
    ln_g = 1.0 + nrm((DEPTH, D_MODEL), 0.02)
    ln_b = nrm((DEPTH, D_MODEL), 0.02)
    return {'x': x, 'c': c, 'ctx': ctx, 'c_ctx': c_ctx, 'w_ada': w_ada, 'b_ada': b_ada, 'w_in': w_in,
            'gdn_conv': gdn_conv, 'gdn_a_log': gdn_a_log, 'gdn_dt_bias': gdn_dt_bias, 'gdn_norm': gdn_norm,
            'hy_conv': hy_conv, 'hy_w1': hy_w1, 'hy_b1': hy_b1, 'hy_w2': hy_w2, 'hy_b2': hy_b2,
            'hy_w3': hy_w3, 'hy_b3': hy_b3, 'hy_w4': hy_w4, 'hy_freq': hy_freq, 'hy_bias': hy_bias,
            'gqa_qn': gqa_qn, 'gqa_kn': gqa_kn, 'diff_lam': diff_lam, 'diff_norm': diff_norm,
            'w_br': w_br, 'w_out': w_out, 'ln_g': ln_g, 'ln_b': ln_b}


def reference(x, c, ctx, c_ctx, w_ada, b_ada, w_in, gdn_conv, gdn_a_log, gdn_dt_bias, gdn_norm,
              hy_conv, hy_w1, hy_b1, hy_w2, hy_b2, hy_w3, hy_b3, hy_w4, hy_freq, hy_bias,
              gqa_qn, gqa_kn, diff_lam, diff_norm, w_br, w_out, ln_g, ln_b):
    n_lat = x.shape[1]
    n_ctx = ctx.shape[1]
    rows = n_lat // GRID_W
    row = jnp.repeat(jnp.arange(rows, dtype=F32), GRID_W)
    col = jnp.tile(jnp.arange(GRID_W, dtype=F32), rows)
    rope_gqa = axial_rope_tables(row, col, GQA_HD)
    rope_diff = axial_rope_tables(row, col, DIFF_QK)
    h_ctx, h_lat = ctx, x
    for l in range(DEPTH):
        want_ctx = l < DEPTH - 1
        lam_init = 0.8 - 0.6 * math.exp(-0.3 * l)
        mod_lat = (jax.nn.silu(c) @ w_ada[l] + b_ada[l])[:, None, :]
        mod_ctx = jax.nn.silu(c_ctx) @ w_ada[l] + b_ada[l]
        u_c, gate_c = adaln_modulate(h_ctx, mod_ctx)
        u_x, gate_x = adaln_modulate(h_lat, mod_lat)
        pc = split_proj(u_c @ w_in[l])
        px = split_proj(u_x @ w_in[l])

        qc, kc, vc, gc, bc = gdn_prep(pc, gdn_conv[l], gdn_a_log[l], gdn_dt_bias[l])
        qx, kx, vx, gx, bx = gdn_prep(px, gdn_conv[l], gdn_a_log[l], gdn_dt_bias[l])
        s0 = jnp.zeros((h_ctx.shape[0], GDN_HEADS, GDN_DK, GDN_DV), F32)
        oc, s_f, s_b = gdn_bidir(qc, kc, vc, gc, bc, s0, s0, want_ctx)
        ox, _, _ = gdn_bidir(qx, kx, vx, gx, bx, s_f, s_b, True)
        ya_x = gdn_output(ox, px['gdn_gate'], gdn_norm[l])

        spec_x = hyena_filter_spectrum(n_lat, hy_w1[l], hy_b1[l], hy_w2[l], hy_b2[l], hy_w3[l], hy_b3[l], hy_w4[l], hy_freq[l])
        yb_x = hyena_mix(px['hy_xv'], px['hy_gate'], hy_conv[l], spec_x, hy_bias[l])

        kgc, vgc = gqa_kv(pc, gqa_kn[l], None)
        kgx, vgx = gqa_kv(px, gqa_kn[l], rope_gqa)
        k_all = jnp.concatenate([kgc, kgx], axis=1)
        v_all = jnp.concatenate([vgc, vgx], axis=1)
        q_lat = gqa_q(px, gqa_qn[l], rope_gqa)
        yc_x = sweep_blocks(lambda qb: gqa_attend(qb, k_all, v_all), q_lat) * jax.nn.silu(px['gqa_gate'])

        lq1, lk1, lq2, lk2 = (diff_lam[l, i].astype(F32) for i in range(4))
        lam = jnp.exp(jnp.sum(lq1 * lk1)) - jnp.exp(jnp.sum(lq2 * lk2)) + lam_init
        kdc, vdc = diff_kv(pc, None)
        kdx, vdx = diff_kv(px, rope_diff)
        kd_all = jnp.concatenate([kdc, kdx], axis=1)
        vd_all = jnp.concatenate([vdc, vdx], axis=1)
        od_x = sweep_blocks(lambda qb: diff_attend(qb, kd_all, vd_all, lam), diff_q(px, rope_diff))
        yd_x = diff_output(od_x, px['diff_gate'], diff_norm[l], lam_init)

        out_x = merge_branches((ya_x, yb_x, yc_x, yd_x), px['merge'], w_br[l], w_out[l])
        new_lat = post_norm(h_lat, gate_x * out_x, ln_g[l], ln_b[l])

        if want_ctx:
            ya_c = gdn_output(oc, pc['gdn_gate'], gdn_norm[l])
            spec_c = hyena_filter_spectrum(n_ctx, hy_w1[l], hy_b1[l], hy_w2[l], hy_b2[l], hy_w3[l], hy_b3[l], hy_w4[l], hy_freq[l])
            yb_c = hyena_mix(pc['hy_xv'], pc['hy_gate'], hy_conv[l], spec_c, hy_bias[l])
            yc_c = gqa_attend(gqa_q(pc, gqa_qn[l], None), kgc, vgc) * jax.nn.silu(pc['gqa_gate'])
            yd_c = diff_output(diff_attend(diff_q(pc, None), kdc, vdc, lam), pc['diff_gate'], diff_norm[l], lam_init)
            out_c = merge_branches((ya_c, yb_c, yc_c, yd_c), pc['merge'], w_br[l], w_out[l])
            h_ctx = post_norm(h_ctx, gate_c * out_c, ln_g[l], ln_b[l])
        h_lat = new_lat
    return h_lat
```

```python
import functools
import math

import numpy as np
import jax
import jax.numpy as jnp
from jax import lax
from jax.experimental import pallas as pl
from jax.experimental.pallas import tpu as pltpu

F32 = jnp.float32
BF16 = jnp.bfloat16
HI = lax.Precision.HIGHEST

D_MODEL = 1024
DEPTH = 4
GRID_W = 64
BRANCH_W = D_MODEL // 2
N_BRANCH = 4
HEADS = 4
HEAD_D = BRANCH_W // HEADS
GDN_CONV = 4
GDN_CHUNK = 64
HY_CONV = 3
HY_EMB = 33
HY_BANDS = (HY_EMB - 1) // 2
HY_FH = 64
HY_ORDER = 2
HY_MIN_DECAY = math.log(1e-2) / 1.5
HY_MAX_DECAY = math.log(1e-2) / 0.3
GQA_KV = 2
DIFF_QK = HEAD_D // 2
ROPE_THETA = 10000.0
EPS = 1e-6
ALPHA = (2.0 * DEPTH) ** 0.25

LANE = 128
SUB = 8
FFT_N2 = 128
VMEM_LIMIT = 60 * 1024 * 1024

C_MERGE = 0
C_GDN_QKV = 4096
C_GDN_GATE = 5632
C_HY_XV = 6144
C_HY_GATE = 7680
C_GQA_QKV = 8192
C_GQA_GATE = 9216
C_DIFF_Q = 9728
C_DIFF_K = 10240
C_DIFF_V = 10752
C_DIFF_GATE = 11264
N_MAIN = 11776
O_GDN_AB = 1536
O_MERGE = 7696


def _cparams(sem):
    return pltpu.CompilerParams(dimension_semantics=sem, vmem_limit_bytes=VMEM_LIMIT)


def _dot(a, b, hi=False):
    if hi:
        return jnp.dot(a, b, precision=HI, preferred_element_type=F32)
    return jnp.dot(a.astype(BF16), b.astype(BF16), preferred_element_type=F32)


def _dot_nt(a, b, hi=False):
    dn = (((1,), (1,)), ((), ()))
    if hi:
        return lax.dot_general(a, b, dn, precision=HI, preferred_element_type=F32)
    return lax.dot_general(a.astype(BF16), b.astype(BF16), dn, preferred_element_type=F32)


def _dot_tn(a, b):
    return lax.dot_general(a.astype(BF16), b.astype(BF16), (((0,), (0,)), ((), ())), preferred_element_type=F32)


def _sigmoid(x):
    return 1.0 / (1.0 + jnp.exp(-x))


def _silu(x):
    return x * _sigmoid(x)


def _softplus(x):
    return jnp.maximum(x, 0.0) + jnp.log1p(jnp.exp(-jnp.abs(x)))


def _ada_body(c_ref, w_ref, b_ref, o_ref):
    o_ref[...] = _dot(_silu(c_ref[...]), w_ref[...], hi=True) + b_ref[...]


def ada_mod(cvec, w_ada, b_ada3, layer):
    d = cvec.shape[1]
    tn = 512
    return pl.pallas_call(
        _ada_body,
        grid=(3 * d // tn,),
        in_specs=[pl.BlockSpec((SUB, d), lambda j: (0, 0)),
                  pl.BlockSpec((None, d, tn), lambda j: (layer, 0, j)),
                  pl.BlockSpec((None, 1, tn), lambda j: (layer, 0, j))],
        out_specs=pl.BlockSpec((SUB, tn), lambda j: (0, j)),
        out_shape=jax.ShapeDtypeStruct((SUB, 3 * d), F32),
        compiler_params=_cparams(("arbitrary",)),
        name="ada_mod",
    )(cvec, w_ada, b_ada3)


def _inproj_body(h_ref, mod_ref, w_ref, wab_ref, o_ref, ab_ref, u_ref):
    @pl.when(pl.program_id(1) == 0)
    def _():
        x = h_ref[...]
        mu = jnp.mean(x, axis=-1, keepdims=True)
        xc = x - mu
        var = jnp.mean(xc * xc, axis=-1, keepdims=True)
        u = xc * lax.rsqrt(var + EPS) * (1.0 + mod_ref[1:2, :]) + mod_ref[0:1, :]
        u_ref[...] = u.astype(BF16)
        ab_ref[...] = _dot(u, wab_ref[...], hi=True)

    o_ref[...] = jnp.dot(u_ref[...], w_ref[...], preferred_element_type=F32)


def in_proj(h_all, mod3, w_main, w_ab, layer, tm, lat_blocks_per_batch, n_batch):
    t, d = h_all.shape
    tn = 512
    n_main = w_main.shape[2]
    row = lambda i: jnp.minimum(i // lat_blocks_per_batch, n_batch)
    return pl.pallas_call(
        _inproj_body,
        grid=(t // tm, n_main // tn),
        in_specs=[pl.BlockSpec((tm, d), lambda i, j: (i, 0)),
                  pl.BlockSpec((None, 3, d), lambda i, j: (row(i), 0, 0)),
                  pl.BlockSpec((None, d, tn), lambda i, j: (layer, 0, j)),
                  pl.BlockSpec((None, d, LANE), lambda i, j: (layer, 0, 0))],
        out_specs=[pl.BlockSpec((tm, tn), lambda i, j: (i, j)),
                   pl.BlockSpec((tm, LANE), lambda i, j: (i, 0))],
        out_shape=[jax.ShapeDtypeStruct((t, n_main), F32), jax.ShapeDtypeStruct((t, LANE), F32)],
        scratch_shapes=[pltpu.VMEM((tm, d), BF16)],
        compiler_params=_cparams(("arbitrary", "arbitrary")),
        name="in_proj",
    )(h_all, mod3, w_main, w_ab)


def _dwconv_body(xp_ref, x_ref, xn_ref, w_ref, o_ref, pad_ref, *, taps, pad_l, lat_blocks, bl, bc, act):
    i = pl.program_id(0)
    r = x_ref.shape[0]
    is_lat = i < lat_blocks
    pos = jnp.where(is_lat, i % bl, (i - lat_blocks) % bc)
    last = jnp.where(is_lat, bl - 1, bc - 1)
    pad_ref[0:SUB, :] = jnp.where(pos == 0, 0.0, xp_ref[...])
    pad_ref[SUB:SUB + r, :] = x_ref[...]
    pad_ref[SUB + r:2 * SUB + r, :] = jnp.where(pos == last, 0.0, xn_ref[...])
    acc = None
    for j in range(taps):
        off = SUB + j - pad_l
        term = w_ref[j:j + 1, :] * pad_ref[off:off + r, :]
        acc = term if acc is None else acc + term
    if act:
        acc = _silu(acc)
    o_ref[...] = acc


def dwconv(p, w_conv, layer, col0, width, n_lat, n_ctx, n_batch, act):
    t = p.shape[0]
    taps = w_conv.shape[1]
    r = 256 if n_ctx % 256 == 0 else n_ctx
    lw = 256
    cb = col0 // lw
    rs = r // SUB
    body = functools.partial(_dwconv_body, taps=taps, pad_l=(taps - 1) // 2, lat_blocks=n_batch * n_lat // r,
                             bl=n_lat // r, bc=n_ctx // r, act=act)
    return pl.pallas_call(
        body,
        grid=(t // r, width // lw),
        in_specs=[pl.BlockSpec((SUB, lw), lambda i, j: (jnp.maximum(i * rs - 1, 0), cb + j)),
                  pl.BlockSpec((r, lw), lambda i, j: (i, cb + j)),
                  pl.BlockSpec((SUB, lw), lambda i, j: (jnp.minimum((i + 1) * rs, t // SUB - 1), cb + j)),
                  pl.BlockSpec((None, taps, lw), lambda i, j: (layer, 0, j))],
        out_specs=pl.BlockSpec((r, lw), lambda i, j: (i, j)),
        out_shape=jax.ShapeDtypeStruct((t, width), F32),
        scratch_shapes=[pltpu.VMEM((r + 2 * SUB, lw), F32)],
        compiler_params=_cparams(("arbitrary", "arbitrary")),
        name="dwconv",
    )(p, p, p, w_conv)


def _neumann_inverse(a):
    c = a.shape[0]
    eye = (lax.broadcasted_iota(jnp.int32, (c, c), 0) == lax.broadcasted_iota(jnp.int32, (c, c), 1)).astype(F32)
    p = -a
    inv = eye + p
    steps = int(math.log2(c)) - 1
    for _ in range(steps):
        p = _dot(p, p, hi=True)
        inv = inv + _dot(inv, p, hi=True)
    return inv


def _gdn_body(qf_ref, qb_ref, abcf_ref, abcb_ref, abrf_ref, abrb_ref, pr_ref, pc_ref, of_ref, ob_ref, s_ref):
    c = GDN_CHUNK

    @pl.when(pl.program_id(1) == 0)
    def _():
        s_ref[...] = jnp.zeros_like(s_ref)

    ii = lax.broadcasted_iota(jnp.int32, (c, c), 0)
    jj = lax.broadcasted_iota(jnp.int32, (c, c), 1)
    lmat = (jj <= ii).astype(F32)
    alr, dtr = pr_ref[0:1, :], pr_ref[1:2, :]
    alc, dtc = pc_ref[:, 0:1], pc_ref[:, 1:2]
    for d in range(2):
        qkv_ref = (qf_ref, qb_ref)[d]
        abc = (abcf_ref, abcb_ref)[d][...]
        abr = (abrf_ref, abrb_ref)[d][...]
        out_ref = (of_ref, ob_ref)[d]
        g_c = -jnp.exp(alr) * _softplus(abc + dtr)
        g_r = -jnp.exp(alc) * _softplus(abr + dtc)
        cum_c = _dot(lmat, g_c, hi=True)
        cum_r = _dot_nt(g_r, lmat, hi=True)
        if d == 1:
            cum_c = cum_c[c - 1:c, :] - cum_c + g_c
            cum_r = cum_r[:, c - 1:c] - cum_r + g_r
        beta_all = _sigmoid(abc)
        incl = (jj <= ii) if d == 0 else (jj >= ii)
        strict = (jj < ii) if d == 0 else (jj > ii)
        for h in range(HEADS):
            idx = HEADS * d + h
            q = qkv_ref[:, h * HEAD_D:(h + 1) * HEAD_D]
            k = qkv_ref[:, BRANCH_W + h * HEAD_D:BRANCH_W + (h + 1) * HEAD_D]
            v = qkv_ref[:, 2 * BRANCH_W + h * HEAD_D:2 * BRANCH_W + (h + 1) * HEAD_D]
            q = q * lax.rsqrt(jnp.sum(q * q, axis=-1, keepdims=True) + EPS) * (HEAD_D ** -0.5)
            k = k * lax.rsqrt(jnp.sum(k * k, axis=-1, keepdims=True) + EPS)
            cc = cum_c[:, idx:idx + 1]
            cr = cum_r[idx:idx + 1, :]
            dec = jnp.exp(jnp.where(incl, cc - cr, -1e30))
            beta = beta_all[:, 2 * HEADS + idx:2 * HEADS + idx + 1]
            kk = _dot_nt(k, k)
            qk = _dot_nt(q, k)
            tinv = _neumann_inverse(jnp.where(strict, beta * kk * dec, 0.0))
            ecum = jnp.exp(cc)
            w_c = _dot(tinv, k * (beta * ecum), hi=True)
            u_c = _dot(tinv, v * beta, hi=True)
            tot = cc[c - 1:c, :] if d == 0 else cc[0:1, :]
            k_tail = k * jnp.exp(tot - cc)
            s = s_ref[d, h]
            v_new = u_c - _dot(w_c, s)
            out_ref[:, h * HEAD_D:(h + 1) * HEAD_D] = _dot(q * ecum, s) + _dot(qk * dec, v_new)
            s_ref[d, h] = s * jnp.exp(tot) + _dot_tn(k_tail, v_new)


def gdn_scan(qkv, ab, ab_rows, par_r, par_c, n_lat, n_ctx, n_batch):
    t = qkv.shape[0]
    c = GDN_CHUNK
    nlc, ncc = n_lat // c, n_ctx // c
    base = n_batch * nlc

    def fwd(b, s):
        return jnp.where(s < ncc, base + b * ncc + s, b * nlc + (s - ncc))

    def bwd(b, s):
        return jnp.where(s < ncc, base + b * ncc + (ncc - 1 - s), b * nlc + (nlc - 1 - (s - ncc)))

    w3 = 3 * BRANCH_W
    return pl.pallas_call(
        _gdn_body,
        grid=(n_batch, ncc + nlc),
        in_specs=[pl.BlockSpec((c, w3), lambda b, s: (fwd(b, s), 0)),
                  pl.BlockSpec((c, w3), lambda b, s: (bwd(b, s), 0)),
                  pl.BlockSpec((c, LANE), lambda b, s: (fwd(b, s), 0)),
                  pl.BlockSpec((c, LANE), lambda b, s: (bwd(b, s), 0)),
                  pl.BlockSpec((None, 4 * HEADS, c), lambda b, s: (fwd(b, s), 0, 0)),
                  pl.BlockSpec((None, 4 * HEADS, c), lambda b, s: (bwd(b, s), 0, 0)),
                  pl.BlockSpec((SUB, LANE), lambda b, s: (0, 0)),
                  pl.BlockSpec((4 * HEADS, LANE), lambda b, s: (0, 0))],
        out_specs=[pl.BlockSpec((c, BRANCH_W), lambda b, s: (fwd(b, s), 0)),
                   pl.BlockSpec((c, BRANCH_W), lambda b, s: (bwd(b, s), 0))],
        out_shape=[jax.ShapeDtypeStruct((t, BRANCH_W), F32), jax.ShapeDtypeStruct((t, BRANCH_W), F32)],
        scratch_shapes=[pltpu.VMEM((2, HEADS, HEAD_D, HEAD_D), F32)],
        compiler_params=_cparams(("arbitrary", "arbitrary")),
        name="gdn_scan",
    )(qkv, qkv, ab, ab, ab_rows, ab_rows, par_r, par_c)


def _hyfilt_body(z_ref, t_ref, w1_ref, b1_ref, w2_ref, b2_ref, w3_ref, b3_ref, w4_ref, fr_ref, dl_ref, o_ref):
    fr = fr_ref[...]
    h = jnp.sin(fr * (_dot(z_ref[...], w1_ref[...], hi=True) + b1_ref[...]))
    h = jnp.sin(fr * (_dot(h, w2_ref[...], hi=True) + b2_ref[...]))
    h = jnp.sin(fr * (_dot(h, w3_ref[...], hi=True) + b3_ref[...]))
    o_ref[...] = _dot(h, w4_ref[...], hi=True) * jnp.exp(-t_ref[...] * dl_ref[...])


def hyena_filter(n, w1p, b1, w2, b2, w3, b3, w4, fr, layer):
    pos = jnp.arange(n, dtype=F32)
    tt = pos / max(n - 1, 1)
    ang = (2.0 * math.pi / n) * pos[:, None] * jnp.linspace(1e-4, HY_BANDS - 1, HY_BANDS, dtype=F32)
    z = jnp.concatenate([tt[:, None], jnp.cos(ang), -jnp.sin(ang), jnp.zeros((n, LANE - HY_EMB), F32)], -1)
    deltas = jnp.abs(jnp.linspace(HY_MIN_DECAY, HY_MAX_DECAY, BRANCH_W, dtype=F32))
    dl = jnp.tile(deltas, 2 * HY_ORDER)[None, :]
    r = min(n, 512)
    wo = 2 * HY_ORDER * BRANCH_W
    full = lambda shape: pl.BlockSpec((None,) + shape, lambda i: (layer,) + (0,) * len(shape))
    return pl.pallas_call(
        _hyfilt_body,
        grid=(n // r,),
        in_specs=[pl.BlockSpec((r, LANE), lambda i: (i, 0)),
                  pl.BlockSpec((r, 1), lambda i: (i, 0)),
                  full((LANE, HY_FH)), full((1, HY_FH)), full((HY_FH, HY_FH)), full((1, HY_FH)),
                  full((HY_FH, HY_FH)), full((1, HY_FH)), full((HY_FH, wo)), full((1, HY_FH)),
                  pl.BlockSpec((1, wo), lambda i: (0, 0))],
        out_specs=pl.BlockSpec((r, wo), lambda i: (i, 0)),
        out_shape=jax.ShapeDtypeStruct((n, wo), F32),
        compiler_params=_cparams(("arbitrary",)),
        name="hyena_filter",
    )(z, tt[:, None], w1p, b1, w2, b2, w3, b3, w4, fr, dl)


def _two_sided(hf, n):
    w = BRANCH_W
    cols = []
    for o in range(HY_ORDER):
        h0 = hf[:, o * 2 * w:o * 2 * w + w]
        h1 = hf[:, o * 2 * w + w:(o + 1) * 2 * w]
        cols.append(jnp.concatenate([h0, jnp.zeros((1, w), F32), h1[:0:-1]], 0))
    return jnp.concatenate(cols, 1)


@functools.lru_cache(maxsize=None)
def _dense_dft_tables(n):
    nn = 2 * n
    k = np.arange(nn)[:, None].astype(np.float64)
    m = np.arange(nn)[None, :].astype(np.float64)
    ang = -2.0 * np.pi * k * m / nn
    wr, wi = np.cos(ang), np.sin(ang)
    f_real = np.concatenate([wr, wi], 0)
    wr_h, wi_h = wr[:, :n], wi[:, :n]
    f_fwd = np.block([[wr_h, -wi_h], [wi_h, wr_h]])
    cr, ci = wr.T[:n] / nn, -wi.T[:n] / nn
    f_inv = np.block([[cr, -ci], [ci, cr]])
    return (np.asarray(f_real, np.float32), np.asarray(f_fwd, np.float32), np.asarray(f_inv, np.float32))


@functools.lru_cache(maxsize=None)
def _two_stage_dft_tables(n):
    nn = 2 * n
    n2c = FFT_N2
    n1c = nn // n2c
    n1h = n1c // 2
    k1 = np.arange(n1c).astype(np.float64)
    n1 = np.arange(n1c).astype(np.float64)
    n2 = np.arange(n2c).astype(np.float64)
    ang = -2.0 * np.pi * (k1[None, :, None] * n1[None, None, :] / n1c + n2[:, None, None] * k1[None, :, None] / nn)
    mr, mi = np.cos(ang), np.sin(ang)
    f1_real = np.concatenate([mr, mi], 1)
    mrh, mih = mr[:, :, :n1h], mi[:, :, :n1h]
    f1_cplx = np.concatenate([np.concatenate([mrh, -mih], 2), np.concatenate([mih, mrh], 2)], 1)
    gr = np.transpose(mr, (0, 2, 1))[:, :n1h, :] / nn
    gi = -np.transpose(mi, (0, 2, 1))[:, :n1h, :] / nn
    g1 = np.concatenate([np.concatenate([gr, -gi], 2), np.concatenate([gi, gr], 2)], 1)
    k2 = np.arange(n2c).astype(np.float64)
    a2 = -2.0 * np.pi * k2[:, None] * n2[None, :] / n2c
    fr, fi = np.cos(a2), np.sin(a2)
    f2 = np.block([[fr, -fi], [fi, fr]])
    f2i = np.block([[fr.T, fi.T], [-fi.T, fr.T]])
    f32 = lambda a: np.asarray(a, np.float32)
    return f32(f1_real), f32(f1_cplx), f32(g1), f32(f2), f32(f2i)


def _spec_dense_body(f_ref, x_ref, o_ref):
    o_ref[...] = _dot(f_ref[...], x_ref[...], hi=True)


def hyena_spec_dense(full, n):
    f_real, _, _ = _dense_dft_tables(n)
    nn, cols = full.shape
    return pl.pallas_call(
        _spec_dense_body,
        grid=(cols // LANE,),
        in_specs=[pl.BlockSpec((2 * nn, nn), lambda j: (0, 0)),
                  pl.BlockSpec((nn, LANE), lambda j: (0, j))],
        out_specs=pl.BlockSpec((2 * nn, LANE), lambda j: (0, j)),
        out_shape=jax.ShapeDtypeStruct((2 * nn, cols), F32),
        compiler_params=_cparams(("arbitrary",)),
        name="hyena_spec_dense",
    )(jnp.asarray(f_real), full)


def _conv_dense_body(zr_ref, zi_ref, h_ref, ff_ref, fi_ref, bias_ref, or_ref, oi_ref):
    zr, zi = zr_ref[...], zi_ref[...]
    n = zr.shape[0]
    nn = 2 * n
    x = _dot(ff_ref[...], jnp.concatenate([zr, zi], 0), hi=True)
    xr, xi = x[:nn], x[nn:]
    hr, hi_ = h_ref[0:nn, :], h_ref[nn:2 * nn, :]
    y = _dot(fi_ref[...], jnp.concatenate([xr * hr - xi * hi_, xr * hi_ + xi * hr], 0), hi=True)
    bias = bias_ref[...]
    or_ref[...] = y[:n] + zr * bias
    oi_ref[...] = y[n:] + zi * bias


def hyena_conv_dense(zsrc, zcol, row0, n, n_batch, spec, bias3, layer, order):
    _, f_fwd, f_inv = _dense_dft_tables(n)
    nn = 2 * n
    rb, cb = row0 // n, zcol // LANE
    wb = BRANCH_W // LANE
    outs = pl.pallas_call(
        _conv_dense_body,
        grid=(n_batch // 2, wb),
        in_specs=[pl.BlockSpec((n, LANE), lambda p, j: (rb + 2 * p, cb + j)),
                  pl.BlockSpec((n, LANE), lambda p, j: (rb + 2 * p + 1, cb + j)),
                  pl.BlockSpec((2 * nn, LANE), lambda p, j: (0, order * wb + j)),
                  pl.BlockSpec((2 * nn, 2 * n), lambda p, j: (0, 0)),
                  pl.BlockSpec((2 * n, 2 * nn), lambda p, j: (0, 0)),
                  pl.BlockSpec((None, 1, LANE), lambda p, j: (layer * HY_ORDER + order, 0, j))],
        out_specs=[pl.BlockSpec((None, n, LANE), lambda p, j: (p, 0, j)),
                   pl.BlockSpec((None, n, LANE), lambda p, j: (p, 0, j))],
        out_shape=[jax.ShapeDtypeStruct((n_batch // 2, n, BRANCH_W), F32)] * 2,
        compiler_params=_cparams(("arbitrary", "arbitrary")),
        name="hyena_conv_dense",
    )(zsrc, zsrc, spec, jnp.asarray(f_fwd), jnp.asarray(f_inv), bias3)
    return jnp.stack(outs, 1).reshape(n_batch * n, BRANCH_W)


def _spec_fft_body(x_ref, f1_ref, f2_ref, o_ref, a_ref):
    n1c = o_ref.shape[0]

    def stage1(n2, carry):
        xs = x_ref[pl.ds(n2, n1c, stride=FFT_N2), :]
        a_ref[n2] = _dot(f1_ref[n2], xs, hi=True)
        return carry

    lax.fori_loop(0, FFT_N2, stage1, 0)

    def stage2(k1, carry):
        blk = jnp.concatenate([a_ref[:, k1, :], a_ref[:, n1c + k1, :]], 0)
        o_ref[k1] = _dot(f2_ref[...], blk, hi=True)
        return carry

    lax.fori_loop(0, n1c, stage2, 0)


def hyena_spec_fft(full, n):
    f1_real, _, _, f2, _ = _two_stage_dft_tables(n)
    nn, cols = full.shape
    n1c = nn // FFT_N2
    const = lambda shape: pl.BlockSpec(shape, lambda j: (0,) * len(shape), pipeline_mode=pl.Buffered(1))
    return pl.pallas_call(
        _spec_fft_body,
        grid=(cols // LANE,),
        in_specs=[pl.BlockSpec((nn, LANE), lambda j: (0, j)),
                  const((FFT_N2, 2 * n1c, n1c)), const((2 * FFT_N2, 2 * FFT_N2))],
        out_specs=pl.BlockSpec((n1c, 2 * FFT_N2, LANE), lambda j: (0, 0, j)),
        out_shape=jax.ShapeDtypeStruct((n1c, 2 * FFT_N2, cols), F32),
        scratch_shapes=[pltpu.VMEM((FFT_N2, 2 * n1c, LANE), F32)],
        compiler_params=_cparams(("arbitrary",)),
        name="hyena_spec_fft",
    )(full, jnp.asarray(f1_real), jnp.asarray(f2))


def _conv_fft_body(zr_ref, zi_ref, h_ref, f1_ref, f2_ref, f2i_ref, g1_ref, bias_ref, or_ref, oi_ref, a_ref):
    n1c = h_ref.shape[0]
    n1h = n1c // 2
    n2c = FFT_N2

    def stage1(n2, carry):
        xs = jnp.concatenate([zr_ref[pl.ds(n2, n1h, stride=n2c), :], zi_ref[pl.ds(n2, n1h, stride=n2c), :]], 0)
        a_ref[n2] = _dot(f1_ref[n2], xs, hi=True)
        return carry

    lax.fori_loop(0, n2c, stage1, 0)

    def stage2(k1, carry):
        x = _dot(f2_ref[...], jnp.concatenate([a_ref[:, k1, :], a_ref[:, n1c + k1, :]], 0), hi=True)
        xr, xi = x[:n2c], x[n2c:]
        hr, hi_ = h_ref[k1, 0:n2c, :], h_ref[k1, n2c:2 * n2c, :]
        b = _dot(f2i_ref[...], jnp.concatenate([xr * hr - xi * hi_, xr * hi_ + xi * hr], 0), hi=True)
        a_ref[:, k1, :] = b[:n2c]
        a_ref[:, n1c + k1, :] = b[n2c:]
        return carry

    lax.fori_loop(0, n1c, stage2, 0)
    bias = bias_ref[...]

    def stage3(n2, carry):
        y = _dot(g1_ref[n2], a_ref[n2], hi=True)
        rows = pl.ds(n2, n1h, stride=n2c)
        or_ref[rows, :] = y[:n1h] + zr_ref[rows, :] * bias
        oi_ref[rows, :] = y[n1h:] + zi_ref[rows, :] * bias
        return carry

    lax.fori_loop(0, n2c, stage3, 0)


def hyena_conv_fft(zsrc, zcol, n, n_batch, spec, bias3, layer, order):
    _, f1_cplx, g1, f2, f2i = _two_stage_dft_tables(n)
    n1c = 2 * n // FFT_N2
    cb = zcol // LANE
    wb = BRANCH_W // LANE
    const = lambda shape: pl.BlockSpec(shape, lambda j, p: (0,) * len(shape), pipeline_mode=pl.Buffered(1))
    outs = pl.pallas_call(
        _conv_fft_body,
        grid=(wb, n_batch // 2),
        in_specs=[pl.BlockSpec((n, LANE), lambda j, p: (2 * p, cb + j)),
                  pl.BlockSpec((n, LANE), lambda j, p: (2 * p + 1, cb + j)),
                  pl.BlockSpec((n1c, 2 * FFT_N2, LANE), lambda j, p: (0, 0, order * wb + j),
                               pipeline_mode=pl.Buffered(1)),
                  const((FFT_N2, 2 * n1c, n1c)), const((2 * FFT_N2, 2 * FFT_N2)), const((2 * FFT_N2, 2 * FFT_N2)),
                  const((FFT_N2, n1c, 2 * n1c)),
                  pl.BlockSpec((None, 1, LANE), lambda j, p: (layer * HY_ORDER + order, 0, j))],
        out_specs=[pl.BlockSpec((None, n, LANE), lambda j, p: (p, 0, j)),
                   pl.BlockSpec((None, n, LANE), lambda j, p: (p, 0, j))],
        out_shape=[jax.ShapeDtypeStruct((n_batch // 2, n, BRANCH_W), F32)] * 2,
        scratch_shapes=[pltpu.VMEM((FFT_N2, 2 * n1c, LANE), F32)],
        compiler_params=_cparams(("arbitrary", "arbitrary")),
        name="hyena_conv_fft",
    )(zsrc, zsrc, spec, jnp.asarray(f1_cplx), jnp.asarray(f2), jnp.asarray(f2i), jnp.asarray(g1), bias3)
    return jnp.stack(outs, 1).reshape(n_batch * n, BRANCH_W)


def _mul_body(a_ref, b_ref, o_ref):
    o_ref[...] = a_ref[...] * b_ref[...]


def mul_cols(a, b, bcol):
    t, w = a.shape
    r = 256
    return pl.pallas_call(
        _mul_body,
        grid=(t // r,),
        in_specs=[pl.BlockSpec((r, w), lambda i: (i, 0)), pl.BlockSpec((r, w), lambda i: (i, bcol // w))],
        out_specs=pl.BlockSpec((r, w), lambda i: (i, 0)),
        out_shape=jax.ShapeDtypeStruct((t, w), F32),
        compiler_params=_cparams(("arbitrary",)),
        name="mul_cols",
    )(a, b)


def _swap_pairs(x):
    w = x.shape[-1]
    lane = lax.broadcasted_iota(jnp.int32, x.shape, x.ndim - 1)
    return jnp.where(lane % 2 == 0, pltpu.roll(x, w - 1, x.ndim - 1), pltpu.roll(x, 1, x.ndim - 1))


def _attn_prep_body(g_ref, dq_ref, dk_ref, dv_ref, cg_ref, sg_ref, cd_ref, sd_ref, qn_ref, kn_ref,
                    qg_ref, kg_ref, vg_ref, qd_ref, kd_ref, vd_ref, *, lat_blocks):
    is_lat = pl.program_id(0) < lat_blocks
    cg = jnp.where(is_lat, cg_ref[...], 1.0)
    sg = jnp.where(is_lat, sg_ref[...], 0.0)
    cd = jnp.where(is_lat, cd_ref[...], 1.0)
    sd = jnp.where(is_lat, sd_ref[...], 0.0)

    def rope(x, cs, sn):
        return x * cs + _swap_pairs(x) * sn

    def rms(x, w):
        return x * lax.rsqrt(jnp.mean(x * x, axis=-1, keepdims=True) + EPS) * w

    for h in range(HEADS):
        sl = slice(h * HEAD_D, (h + 1) * HEAD_D)
        q = rope(rms(g_ref[:, sl], qn_ref[...]), cg, sg)
        qg_ref[:, sl] = (q * HEAD_D ** -0.5).astype(BF16)
        qd_ref[:, sl] = (rope(dq_ref[:, sl], cd, sd) * DIFF_QK ** -0.5).astype(BF16)
        kd_ref[:, sl] = rope(dk_ref[:, sl], cd, sd).astype(BF16)
    for h in range(GQA_KV):
        sl = slice(h * HEAD_D, (h + 1) * HEAD_D)
        kin = g_ref[:, BRANCH_W + h * HEAD_D:BRANCH_W + (h + 1) * HEAD_D]
        kg_ref[:, sl] = rope(rms(kin, kn_ref[...]), cg, sg).astype(BF16)
    vg_ref[...] = g_ref[:, BRANCH_W + GQA_KV * HEAD_D:BRANCH_W + 2 * GQA_KV * HEAD_D].astype(BF16)
    vd_ref[...] = dv_ref[...].astype(BF16)


def attn_prep(p, ropes, qn3, kn3, layer, n_lat, n_ctx, n_batch):
    t = p.shape[0]
    r = 256 if n_ctx % 256 == 0 else n_ctx
    nlb, ncb = n_lat // r, n_ctx // r
    lat_blocks = n_batch * nlb
    kvw = GQA_KV * HEAD_D
    w = BRANCH_W

    def kv_row(i):
        lat = (i // nlb) * (nlb + ncb) + ncb + i % nlb
        j = i - lat_blocks
        ctx = (j // ncb) * (nlb + ncb) + j % ncb
        return jnp.where(i < lat_blocks, lat, ctx)

    rope_spec = pl.BlockSpec((r, LANE), lambda i: (jnp.where(i < lat_blocks, i % nlb, 0), 0))
    nkv = n_batch * (n_lat + n_ctx)
    return pl.pallas_call(
        functools.partial(_attn_prep_body, lat_blocks=lat_blocks),
        grid=(t // r,),
        in_specs=[pl.BlockSpec((r, 2 * w), lambda i: (i, C_GQA_QKV // (2 * w))),
                  pl.BlockSpec((r, w), lambda i: (i, C_DIFF_Q // w)),
                  pl.BlockSpec((r, w), lambda i: (i, C_DIFF_K // w)),
                  pl.BlockSpec((r, w), lambda i: (i, C_DIFF_V // w)),
                  rope_spec, rope_spec, rope_spec, rope_spec,
                  pl.BlockSpec((None, 1, LANE), lambda i: (layer, 0, 0)),
                  pl.BlockSpec((None, 1, LANE), lambda i: (layer, 0, 0))],
        out_specs=[pl.BlockSpec((r, w), lambda i: (i, 0)),
                   pl.BlockSpec((r, kvw), lambda i: (kv_row(i), 0)),
                   pl.BlockSpec((r, kvw), lambda i: (kv_row(i), 0)),
                   pl.BlockSpec((r, w), lambda i: (i, 0)),
                   pl.BlockSpec((r, w), lambda i: (kv_row(i), 0)),
                   pl.BlockSpec((r, w), lambda i: (kv_row(i), 0))],
        out_shape=[jax.ShapeDtypeStruct((t, w), BF16), jax.ShapeDtypeStruct((nkv, kvw), BF16),
                   jax.ShapeDtypeStruct((nkv, kvw), BF16), jax.ShapeDtypeStruct((t, w), BF16),
                   jax.ShapeDtypeStruct((nkv, w), BF16), jax.ShapeDtypeStruct((nkv, w), BF16)],
        compiler_params=_cparams(("arbitrary",)),
        name="attn_prep",
    )(p, p, p, p, *ropes, qn3, kn3)


def _softmax_parts(s):
    e = jnp.exp(s - jnp.max(s, axis=-1, keepdims=True))
    return e, jnp.sum(e, axis=-1, keepdims=True)


def _gqa_body(q_ref, k_ref, v_ref, o_ref):
    group = HEADS // GQA_KV
    for kvh in range(GQA_KV):
        k = k_ref[:, kvh * HEAD_D:(kvh + 1) * HEAD_D]
        v = v_ref[:, kvh * HEAD_D:(kvh + 1) * HEAD_D]
        for g in range(group):
            sl = slice((kvh * group + g) * HEAD_D, (kvh * group + g + 1) * HEAD_D)
            s = lax.dot_general(q_ref[:, sl], k, (((1,), (1,)), ((), ())), preferred_element_type=F32)
            e, l = _softmax_parts(s)
            o_ref[:, sl] = jnp.dot(e.astype(BF16), v, preferred_element_type=F32) / l


def _diff_body(q_ref, k_ref, v_ref, lam_ref, o_ref, *, lam_init):
    lam4 = lam_ref[...]
    lam = (jnp.exp(jnp.sum(lam4[0:1] * lam4[1:2], axis=-1, keepdims=True))
           - jnp.exp(jnp.sum(lam4[2:3] * lam4[3:4], axis=-1, keepdims=True)) + lam_init)
    dn = (((1,), (1,)), ((), ()))
    for h in range(HEADS):
        sl = slice(h * HEAD_D, (h + 1) * HEAD_D)
        q = q_ref[:, sl]
        k = k_ref[:, sl]
        v = v_ref[:, sl]
        first = lax.broadcasted_iota(jnp.int32, q.shape, 1) < DIFF_QK
        zero = jnp.zeros_like(q)
        e1, l1 = _softmax_parts(lax.dot_general(jnp.where(first, q, zero), k, dn, preferred_element_type=F32))
        e2, l2 = _softmax_parts(lax.dot_general(jnp.where(first, zero, q), k, dn, preferred_element_type=F32))
        o1 = jnp.dot(e1.astype(BF16), v, preferred_element_type=F32) / l1
        o2 = jnp.dot(e2.astype(BF16), v, preferred_element_type=F32) / l2
        o_ref[:, sl] = o1 - lam * o2


def attention(body, q, k, v, extra, extra_specs, q_row0, nq, kv_per_batch, kv_len, n_batch, tq, name):
    w = q.shape[1]
    qb0 = q_row0 // tq
    nqb = nq // tq
    kvb = kv_per_batch // kv_len
    return pl.pallas_call(
        body,
        grid=(n_batch, nqb),
        in_specs=[pl.BlockSpec((tq, w), lambda b, i: (qb0 + b * nqb + i, 0)),
                  pl.BlockSpec((kv_len, k.shape[1]), lambda b, i: (b * kvb, 0)),
                  pl.BlockSpec((kv_len, v.shape[1]), lambda b, i: (b * kvb, 0))] + extra_specs,
        out_specs=pl.BlockSpec((tq, w), lambda b, i: (b * nqb + i, 0)),
        out_shape=jax.ShapeDtypeStruct((n_batch * nq, w), F32),
        compiler_params=_cparams(("arbitrary", "arbitrary")),
        name=name,
    )(q, k, v, *extra)


def _merge_body(h_ref, mod_ref, mg_ref, of_ref, ob_ref, ggate_ref, y1_ref, x2_ref, hgate_ref, oc_ref, cgate_ref,
                od_ref, dgate_ref, gnorm_ref, dnorm_ref, wbr_ref, wout_ref, lng_ref, lnb_ref, o_ref, *, diff_scale):
    def rms_heads(x, w):
        parts = []
        for h in range(HEADS):
            xh = x[:, h * HEAD_D:(h + 1) * HEAD_D]
            parts.append(xh * lax.rsqrt(jnp.mean(xh * xh, axis=-1, keepdims=True) + EPS) * w)
        return jnp.concatenate(parts, -1)

    ys = (rms_heads(of_ref[...] + ob_ref[...], gnorm_ref[...]) * _silu(ggate_ref[...]),
          x2_ref[...] * y1_ref[...] * _silu(hgate_ref[...]),
          oc_ref[...] * _silu(cgate_ref[...]),
          rms_heads(od_ref[...], dnorm_ref[...]) * diff_scale * _silu(dgate_ref[...]))
    acc = None
    for n in range(N_BRANCH):
        proj = jnp.dot(ys[n].astype(BF16), wbr_ref[n], preferred_element_type=F32)
        term = _sigmoid(mg_ref[:, n * D_MODEL:(n + 1) * D_MODEL]) * proj
        acc = term if acc is None else acc + term
    out = jnp.dot(acc.astype(BF16), wout_ref[...], preferred_element_type=F32)
    x = ALPHA * h_ref[...] + mod_ref[2:3, :] * out
    mu = jnp.mean(x, axis=-1, keepdims=True)
    xc = x - mu
    var = jnp.mean(xc * xc, axis=-1, keepdims=True)
    o_ref[...] = xc * lax.rsqrt(var + EPS) * lng_ref[...] + lnb_ref[...]


def merge_postnorm(h_all, mod3, p, o_f, o_b, y1, xv, oc, od, gnorm3, dnorm3, wbr, wout, lng3, lnb3, layer, lam_init,
                   n_lat, n_batch):
    t, d = h_all.shape
    r = 256 if n_lat % 256 == 0 else 64
    w = BRANCH_W
    lbb = n_lat // r
    row = lambda i: jnp.minimum(i // lbb, n_batch)
    tok = lambda cb: pl.BlockSpec((r, w), lambda i: (i, cb))
    vec = lambda width: pl.BlockSpec((None, 1, width), lambda i: (layer, 0, 0))
    return pl.pallas_call(
        functools.partial(_merge_body, diff_scale=1.0 - lam_init),
        grid=(t // r,),
        in_specs=[pl.BlockSpec((r, d), lambda i: (i, 0)),
                  pl.BlockSpec((None, 3, d), lambda i: (row(i), 0, 0)),
                  pl.BlockSpec((r, N_BRANCH * d), lambda i: (i, C_MERGE // (N_BRANCH * d))),
                  tok(0), tok(0), tok(C_GDN_GATE // w), tok(0), tok(1), tok(C_HY_GATE // w), tok(0),
                  tok(C_GQA_GATE // w), tok(0), tok(C_DIFF_GATE // w),
                  vec(LANE), vec(LANE),
                  pl.BlockSpec((None, N_BRANCH, w, d), lambda i: (layer, 0, 0, 0)),
                  pl.BlockSpec((None, d, d), lambda i: (layer, 0, 0)),
                  vec(d), vec(d)],
        out_specs=pl.BlockSpec((r, d), lambda i: (i, 0)),
        out_shape=jax.ShapeDtypeStruct((t, d), F32),
        compiler_params=_cparams(("arbitrary",)),
        name="merge_postnorm",
    )(h_all, mod3, p, o_f, o_b, p, y1, xv, p, oc, p, od, p, gnorm3, dnorm3, wbr, wout, lng3, lnb3)


def _rope_tables(n_lat, dim):
    rows = n_lat // GRID_W
    row = jnp.repeat(jnp.arange(rows, dtype=F32), GRID_W)
    col = jnp.tile(jnp.arange(GRID_W, dtype=F32), rows)
    half = dim // 2
    inv = ROPE_THETA ** (-jnp.arange(0, half, 2, dtype=F32) / half)
    ang = jnp.concatenate([row[:, None] * inv, col[:, None] * inv], -1)
    cos = jnp.repeat(jnp.cos(ang), 2, axis=-1)
    sin = jnp.repeat(jnp.sin(ang), 2, axis=-1)
    sign = jnp.tile(jnp.array([-1.0, 1.0], F32), dim // 2)
    reps = LANE // dim
    return jnp.tile(cos, (1, reps)), jnp.tile(sin * sign, (1, reps))


def kernel(x, c, ctx, c_ctx, w_ada, b_ada, w_in, gdn_conv, gdn_a_log, gdn_dt_bias, gdn_norm, hy_conv, hy_w1, hy_b1,
           hy_w2, hy_b2, hy_w3, hy_b3, hy_w4, hy_freq, hy_bias, gqa_qn, gqa_kn, diff_lam, diff_norm, w_br, w_out,
           ln_g, ln_b):
    nb, n_lat, d = x.shape
    n_ctx = ctx.shape[1]
    t_lat, t_ctx = nb * n_lat, nb * n_ctx
    depth = w_in.shape[0]
    w = BRANCH_W

    w_main = jnp.concatenate([w_in[:, :, O_MERGE:], w_in[:, :, :O_GDN_AB], w_in[:, :, O_GDN_AB + 4 * HEADS:O_MERGE]],
                             axis=2).astype(BF16)
    w_ab = jnp.pad(w_in[:, :, O_GDN_AB:O_GDN_AB + 4 * HEADS], ((0, 0), (0, 0), (0, LANE - 4 * HEADS)))
    wbr_bf = w_br.astype(BF16)
    wout_bf = w_out.astype(BF16)
    b_ada3 = b_ada[:, None, :]
    cvec = jnp.concatenate([c, c_ctx[None, :], jnp.zeros((SUB - nb - 1, d), F32)], 0)
    as3 = lambda a: a[:, None, :]
    gdn_par_r = jnp.pad(jnp.stack([gdn_a_log.reshape(depth, -1), gdn_dt_bias.reshape(depth, -1)], 1),
                        ((0, 0), (0, SUB - 2), (0, LANE - 2 * HEADS)))
    gdn_par_c = jnp.pad(jnp.stack([gdn_a_log.reshape(depth, -1), gdn_dt_bias.reshape(depth, -1)], 2),
                        ((0, 0), (0, 2 * HEADS), (0, LANE - 2)))
    hy_w1p = jnp.pad(hy_w1, ((0, 0), (0, LANE - HY_EMB), (0, 0)))
    hy_bias3 = hy_bias.reshape(depth * HY_ORDER, 1, w)
    ropes = _rope_tables(n_lat, HEAD_D) + _rope_tables(n_lat, DIFF_QK)

    tm = 1024 if (n_lat % 1024 == 0 and t_ctx % 1024 == 0) else n_ctx
    h_all = jnp.concatenate([x.reshape(t_lat, d), ctx.reshape(t_ctx, d)], 0)
    for l in range(depth):
        lam_init = 0.8 - 0.6 * math.exp(-0.3 * l)
        mod3 = ada_mod(cvec, w_ada, b_ada3, l).reshape(SUB, 3, d)
        p, ab = in_proj(h_all, mod3, w_main, w_ab, l, tm, n_lat // tm, nb)

        qkv = dwconv(p, gdn_conv, l, C_GDN_QKV, 3 * w, n_lat, n_ctx, nb, act=True)
        ab_rows = jnp.transpose(ab[:, :4 * HEADS].reshape(-1, GDN_CHUNK, 4 * HEADS), (0, 2, 1))
        o_f, o_b = gdn_scan(qkv, ab, ab_rows, gdn_par_r[l], gdn_par_c[l], n_lat, n_ctx, nb)

        xv = dwconv(p, hy_conv, l, C_HY_XV, 3 * w, n_lat, n_ctx, nb, act=False)
        filt = lambda n: _two_sided(hyena_filter(n, hy_w1p, as3(hy_b1), hy_w2, as3(hy_b2), hy_w3, as3(hy_b3), hy_w4,
                                                 as3(hy_freq), l), n)
        spec_lat = hyena_spec_fft(filt(n_lat), n_lat)
        spec_ctx = hyena_spec_dense(filt(n_ctx), n_ctx)
        y0 = jnp.concatenate([hyena_conv_fft(xv, 2 * w, n_lat, nb, spec_lat, hy_bias3, l, 0),
                              hyena_conv_dense(xv, 2 * w, t_lat, n_ctx, nb, spec_ctx, hy_bias3, l, 0)], 0)
        z1 = mul_cols(y0, xv, 0)
        y1 = jnp.concatenate([hyena_conv_fft(z1, 0, n_lat, nb, spec_lat, hy_bias3, l, 1),
                              hyena_conv_dense(z1, 0, t_lat, n_ctx, nb, spec_ctx, hy_bias3, l, 1)], 0)

        qg, kg, vg, qd, kd, vd = attn_prep(p, ropes, as3(gqa_qn), as3(gqa_kn), l, n_lat, n_ctx, nb)
        kv_all = n_lat + n_ctx
        tq = min(256, n_ctx)
        lam_spec = [pl.BlockSpec((None, 4, DIFF_QK), lambda b, i: (l, 0, 0))]
        diff_body = functools.partial(_diff_body, lam_init=lam_init)
        oc = jnp.concatenate([
            attention(_gqa_body, qg, kg, vg, (), [], 0, n_lat, kv_all, kv_all, nb, tq, "gqa_lat"),
            attention(_gqa_body, qg, kg, vg, (), [], t_lat, n_ctx, kv_all, n_ctx, nb, tq, "gqa_ctx")], 0)
        od = jnp.concatenate([
            attention(diff_body, qd, kd, vd, (diff_lam,), lam_spec, 0, n_lat, kv_all, kv_all, nb, tq, "diff_lat"),
            attention(diff_body, qd, kd, vd, (diff_lam,), lam_spec, t_lat, n_ctx, kv_all, n_ctx, nb, tq, "diff_ctx")], 0)

        h_all = merge_postnorm(h_all, mod3, p, o_f, o_b, y1, xv, oc, od, as3(gdn_norm), as3(diff_norm), wbr_bf, wout_bf,
                               as3(ln_g), as3(ln_b), l, lam_init, n_lat, nb)
    return h_all[:t_lat].reshape(nb, n_lat, d)
```

```python
import functools
import math

import numpy as np
import jax
import jax.numpy as jnp
from jax import lax
from jax.experimental import pallas as pl
from jax.experimental.pallas import tpu as pltpu

F32 = jnp.float32
BF16 = jnp.bfloat16
HI = lax.Precision.HIGHEST

D_MODEL = 1024
DEPTH = 4
GRID_W = 64
BRANCH_W = D_MODEL // 2
N_BRANCH = 4
HEADS = 4
HEAD_D = BRANCH_W // HEADS
GDN_CONV = 4
GDN_CHUNK = 64
HY_CONV = 3
HY_EMB = 33
HY_BANDS = (HY_EMB - 1) // 2
HY_FH = 64
HY_ORDER = 2
HY_MIN_DECAY = math.log(1e-2) / 1.5
HY_MAX_DECAY = math.log(1e-2) / 0.3
GQA_KV = 2
DIFF_QK = HEAD_D // 2
ROPE_THETA = 10000.0
EPS = 1e-6
ALPHA = (2.0 * DEPTH) ** 0.25

LANE = 128
SUB = 8
FFT_N2 = 128
VMEM_LIMIT = 60 * 1024 * 1024

C_MERGE = 0
C_GDN_QKV = 4096
C_GDN_GATE = 5632
C_HY_XV = 6144
C_HY_GATE = 7680
C_GQA_QKV = 8192
C_GQA_GATE = 9216
C_DIFF_Q = 9728
C_DIFF_K = 10240
C_DIFF_V = 10752
C_DIFF_GATE = 11264
N_MAIN = 11776
O_GDN_AB = 1536
O_MERGE = 7696


def _cparams(sem):
    return pltpu.CompilerParams(dimension_semantics=sem, vmem_limit_bytes=VMEM_LIMIT)


def _dot(a, b, hi=False):
    if hi:
        return jnp.dot(a, b, precision=HI, preferred_element_type=F32)
    return jnp.dot(a.astype(BF16), b.astype(BF16), preferred_element_type=F32)


def _dot_nt(a, b, hi=False):
    dn = (((1,), (1,)), ((), ()))
    if hi:
        return lax.dot_general(a, b, dn, precision=HI, preferred_element_type=F32)
    return lax.dot_general(a.astype(BF16), b.astype(BF16), dn, preferred_element_type=F32)


def _dot_tn(a, b):
    return lax.dot_general(a.astype(BF16), b.astype(BF16), (((0,), (0,)), ((), ())), preferred_element_type=F32)


def _sigmoid(x):
    return 1.0 / (1.0 + jnp.exp(-x))


def _silu(x):
    return x * _sigmoid(x)


def _softplus(x):
    return jnp.maximum(x, 0.0) + jnp.log1p(jnp.exp(-jnp.abs(x)))


def _ada_body(c_ref, w_ref, b_ref, o_ref):
    o_ref[...] = _dot(_silu(c_ref[...]), w_ref[...], hi=True) + b_ref[...]


def ada_mod(cvec, w_ada, b_ada3, layer):
    d = cvec.shape[1]
    tn = 512
    return pl.pallas_call(
        _ada_body,
        grid=(3 * d // tn,),
        in_specs=[pl.BlockSpec((SUB, d), lambda j: (0, 0)),
                  pl.BlockSpec((None, d, tn), lambda j: (layer, 0, j)),
                  pl.BlockSpec((None, 1, tn), lambda j: (layer, 0, j))],
        out_specs=pl.BlockSpec((SUB, tn), lambda j: (0, j)),
        out_shape=jax.ShapeDtypeStruct((SUB, 3 * d), F32),
        compiler_params=_cparams(("arbitrary",)),
        name="ada_mod",
    )(cvec, w_ada, b_ada3)


def _inproj_body(h_ref, mod_ref, w_ref, wab_ref, o_ref, ab_ref, u_ref):
    @pl.when(pl.program_id(1) == 0)
    def _():
        x = h_ref[...]
        mu = jnp.mean(x, axis=-1, keepdims=True)
        xc = x - mu
        var = jnp.mean(xc * xc, axis=-1, keepdims=True)
        u = xc * lax.rsqrt(var + EPS) * (1.0 + mod_ref[1:2, :]) + mod_ref[0:1, :]
        u_ref[...] = u.astype(BF16)
        ab_ref[...] = _dot(u, wab_ref[...], hi=True)

    o_ref[...] = jnp.dot(u_ref[...], w_ref[...], preferred_element_type=F32)


def in_proj(h_all, mod3, w_main, w_ab, layer, tm, lat_blocks_per_batch, n_batch):
    t, d = h_all.shape
    tn = 512
    n_main = w_main.shape[2]
    row = lambda i: jnp.minimum(i // lat_blocks_per_batch, n_batch)
    return pl.pallas_call(
        _inproj_body,
        grid=(t // tm, n_main // tn),
        in_specs=[pl.BlockSpec((tm, d), lambda i, j: (i, 0)),
                  pl.BlockSpec((None, 3, d), lambda i, j: (row(i), 0, 0)),
                  pl.BlockSpec((None, d, tn), lambda i, j: (layer, 0, j)),
                  pl.BlockSpec((None, d, LANE), lambda i, j: (layer, 0, 0))],
        out_specs=[pl.BlockSpec((tm, tn), lambda i, j: (i, j)),
                   pl.BlockSpec((tm, LANE), lambda i, j: (i, 0))],
        out_shape=[jax.ShapeDtypeStruct((t, n_main), F32), jax.ShapeDtypeStruct((t, LANE), F32)],
        scratch_shapes=[pltpu.VMEM((tm, d), BF16)],
        compiler_params=_cparams(("arbitrary", "arbitrary")),
        name="in_proj",
    )(h_all, mod3, w_main, w_ab)


def _dwconv_body(xp_ref, x_ref, xn_ref, w_ref, o_ref, pad_ref, *, taps, pad_l, lat_blocks, bl, bc, act):
    i = pl.program_id(0)
    r = x_ref.shape[0]
    is_lat = i < lat_blocks
    pos = jnp.where(is_lat, i % bl, (i - lat_blocks) % bc)
    last = jnp.where(is_lat, bl - 1, bc - 1)
    pad_ref[0:SUB, :] = jnp.where(pos == 0, 0.0, xp_ref[...])
    pad_ref[SUB:SUB + r, :] = x_ref[...]
    pad_ref[SUB + r:2 * SUB + r, :] = jnp.where(pos == last, 0.0, xn_ref[...])
    acc = None
    for j in range(taps):
        off = SUB + j - pad_l
        term = w_ref[j:j + 1, :] * pad_ref[off:off + r, :]
        acc = term if acc is None else acc + term
    if act:
        acc = _silu(acc)
    o_ref[...] = acc


def dwconv(p, w_conv, layer, col0, width, n_lat, n_ctx, n_batch, act):
    t = p.shape[0]
    taps = w_conv.shape[1]
    r = 256 if n_ctx % 256 == 0 else n_ctx
    lw = 256
    cb = col0 // lw
    rs = r // SUB
    body = functools.partial(_dwconv_body, taps=taps, pad_l=(taps - 1) // 2, lat_blocks=n_batch * n_lat // r,
                             bl=n_lat // r, bc=n_ctx // r, act=act)
    return pl.pallas_call(
        body,
        grid=(t // r, width // lw),
        in_specs=[pl.BlockSpec((SUB, lw), lambda i, j: (jnp.maximum(i * rs - 1, 0), cb + j)),
                  pl.BlockSpec((r, lw), lambda i, j: (i, cb + j)),
                  pl.BlockSpec((SUB, lw), lambda i, j: (jnp.minimum((i + 1) * rs, t // SUB - 1), cb + j)),
                  pl.BlockSpec((None, taps, lw), lambda i, j: (layer, 0, j))],
        out_specs=pl.BlockSpec((r, lw), lambda i, j: (i, j)),
        out_shape=jax.ShapeDtypeStruct((t, width), F32),
        scratch_shapes=[pltpu.VMEM((r + 2 * SUB, lw), F32)],
        compiler_params=_cparams(("arbitrary", "arbitrary")),
        name="dwconv",
    )(p, p, p, w_conv)


def _neumann_inverse(a):
    c = a.shape[0]
    eye = (lax.broadcasted_iota(jnp.int32, (c, c), 0) == lax.broadcasted_iota(jnp.int32, (c, c), 1)).astype(F32)
    p = -a
    inv = eye + p
    steps = int(math.log2(c)) - 1
    for _ in range(steps):
        p = _dot(p, p)
        inv = inv + _dot(inv, p)
    return inv


def _gdn_body(qf_ref, qb_ref, abcf_ref, abcb_ref, abrf_ref, abrb_ref, pr_ref, pc_ref, of_ref, ob_ref, s_ref):
    c = GDN_CHUNK

    @pl.when(pl.program_id(1) == 0)
    def _():
        s_ref[...] = jnp.zeros_like(s_ref)

    ii = lax.broadcasted_iota(jnp.int32, (c, c), 0)
    jj = lax.broadcasted_iota(jnp.int32, (c, c), 1)
    lmat = (jj <= ii).astype(F32)
    alr, dtr = pr_ref[0:1, :], pr_ref[1:2, :]
    alc, dtc = pc_ref[:, 0:1], pc_ref[:, 1:2]
    for d in range(2):
        qkv_ref = (qf_ref, qb_ref)[d]
        abc = (abcf_ref, abcb_ref)[d][...]
        abr = (abrf_ref, abrb_ref)[d][...]
        out_ref = (of_ref, ob_ref)[d]
        g_c = -jnp.exp(alr) * _softplus(abc + dtr)
        g_r = -jnp.exp(alc) * _softplus(abr + dtc)
        cum_c = _dot(lmat, g_c, hi=True)
        cum_r = _dot_nt(g_r, lmat, hi=True)
        if d == 1:
            cum_c = cum_c[c - 1:c, :] - cum_c + g_c
            cum_r = cum_r[:, c - 1:c] - cum_r + g_r
        beta_all = _sigmoid(abc)
        incl = (jj <= ii) if d == 0 else (jj >= ii)
        strict = (jj < ii) if d == 0 else (jj > ii)
        for h in range(HEADS):
            idx = HEADS * d + h
            q = qkv_ref[:, h * HEAD_D:(h + 1) * HEAD_D]
            k = qkv_ref[:, BRANCH_W + h * HEAD_D:BRANCH_W + (h + 1) * HEAD_D]
            v = qkv_ref[:, 2 * BRANCH_W + h * HEAD_D:2 * BRANCH_W + (h + 1) * HEAD_D]
            q = q * lax.rsqrt(jnp.sum(q * q, axis=-1, keepdims=True) + EPS) * (HEAD_D ** -0.5)
            k = k * lax.rsqrt(jnp.sum(k * k, axis=-1, keepdims=True) + EPS)
            cc = cum_c[:, idx:idx + 1]
            cr = cum_r[idx:idx + 1, :]
            dec = jnp.exp(jnp.where(incl, cc - cr, -1e30))
            beta = beta_all[:, 2 * HEADS + idx:2 * HEADS + idx + 1]
            kk = _dot_nt(k, k)
            qk = _dot_nt(q, k)
            tinv = _neumann_inverse(jnp.where(strict, beta * kk * dec, 0.0))
            ecum = jnp.exp(cc)
            w_c = _dot(tinv, k * (beta * ecum))
            u_c = _dot(tinv, v * beta)
            tot = cc[c - 1:c, :] if d == 0 else cc[0:1, :]
            k_tail = k * jnp.exp(tot - cc)
            s = s_ref[d, h]
            v_new = u_c - _dot(w_c, s)
            out_ref[:, h * HEAD_D:(h + 1) * HEAD_D] = _dot(q * ecum, s) + _dot(qk * dec, v_new)
            s_ref[d, h] = s * jnp.exp(tot) + _dot_tn(k_tail, v_new)


def gdn_scan(qkv, ab, ab_rows, par_r, par_c, n_lat, n_ctx, n_batch):
    t = qkv.shape[0]
    c = GDN_CHUNK
    nlc, ncc = n_lat // c, n_ctx // c
    base = n_batch * nlc

    def fwd(b, s):
        return jnp.where(s < ncc, base + b * ncc + s, b * nlc + (s - ncc))

    def bwd(b, s):
        return jnp.where(s < ncc, base + b * ncc + (ncc - 1 - s), b * nlc + (nlc - 1 - (s - ncc)))

    w3 = 3 * BRANCH_W
    return pl.pallas_call(
        _gdn_body,
        grid=(n_batch, ncc + nlc),
        in_specs=[pl.BlockSpec((c, w3), lambda b, s: (fwd(b, s), 0)),
                  pl.BlockSpec((c, w3), lambda b, s: (bwd(b, s), 0)),
                  pl.BlockSpec((c, LANE), lambda b, s: (fwd(b, s), 0)),
                  pl.BlockSpec((c, LANE), lambda b, s: (bwd(b, s), 0)),
                  pl.BlockSpec((None, 4 * HEADS, c), lambda b, s: (fwd(b, s), 0, 0)),
                  pl.BlockSpec((None, 4 * HEADS, c), lambda b, s: (bwd(b, s), 0, 0)),
                  pl.BlockSpec((SUB, LANE), lambda b, s: (0, 0)),
                  pl.BlockSpec((4 * HEADS, LANE), lambda b, s: (0, 0))],
        out_specs=[pl.BlockSpec((c, BRANCH_W), lambda b, s: (fwd(b, s), 0)),
                   pl.BlockSpec((c, BRANCH_W), lambda b, s: (bwd(b, s), 0))],
        out_shape=[jax.ShapeDtypeStruct((t, BRANCH_W), F32), jax.ShapeDtypeStruct((t, BRANCH_W), F32)],
        scratch_shapes=[pltpu.VMEM((2, HEADS, HEAD_D, HEAD_D), F32)],
        compiler_params=_cparams(("arbitrary", "arbitrary")),
        name="gdn_scan",
    )(qkv, qkv, ab, ab, ab_rows, ab_rows, par_r, par_c)


def _hyfilt_body(z_ref, t_ref, w1_ref, b1_ref, w2_ref, b2_ref, w3_ref, b3_ref, w4_ref, fr_ref, dl_ref, o_ref):
    fr = fr_ref[...]
    h = jnp.sin(fr * (_dot(z_ref[...], w1_ref[...], hi=True) + b1_ref[...]))
    h = jnp.sin(fr * (_dot(h, w2_ref[...], hi=True) + b2_ref[...]))
    h = jnp.sin(fr * (_dot(h, w3_ref[...], hi=True) + b3_ref[...]))
    o_ref[...] = _dot(h, w4_ref[...], hi=True) * jnp.exp(-t_ref[...] * dl_ref[...])


def hyena_filter(n, w1p, b1, w2, b2, w3, b3, w4, fr, layer):
    pos = jnp.arange(n, dtype=F32)
    tt = pos / max(n - 1, 1)
    ang = (2.0 * math.pi / n) * pos[:, None] * jnp.linspace(1e-4, HY_BANDS - 1, HY_BANDS, dtype=F32)
    z = jnp.concatenate([tt[:, None], jnp.cos(ang), -jnp.sin(ang), jnp.zeros((n, LANE - HY_EMB), F32)], -1)
    deltas = jnp.abs(jnp.linspace(HY_MIN_DECAY, HY_MAX_DECAY, BRANCH_W, dtype=F32))
    dl = jnp.tile(deltas, 2 * HY_ORDER)[None, :]
    r = min(n, 512)
    wo = 2 * HY_ORDER * BRANCH_W
    full = lambda shape: pl.BlockSpec((None,) + shape, lambda i: (layer,) + (0,) * len(shape))
    return pl.pallas_call(
        _hyfilt_body,
        grid=(n // r,),
        in_specs=[pl.BlockSpec((r, LANE), lambda i: (i, 0)),
                  pl.BlockSpec((r, 1), lambda i: (i, 0)),
                  full((LANE, HY_FH)), full((1, HY_FH)), full((HY_FH, HY_FH)), full((1, HY_FH)),
                  full((HY_FH, HY_FH)), full((1, HY_FH)), full((HY_FH, wo)), full((1, HY_FH)),
                  pl.BlockSpec((1, wo), lambda i: (0, 0))],
        out_specs=pl.BlockSpec((r, wo), lambda i: (i, 0)),
        out_shape=jax.ShapeDtypeStruct((n, wo), F32),
        compiler_params=_cparams(("arbitrary",)),
        name="hyena_filter",
    )(z, tt[:, None], w1p, b1, w2, b2, w3, b3, w4, fr, dl)


def _two_sided(hf, n):
    w = BRANCH_W
    cols = []
    for o in range(HY_ORDER):
        h0 = hf[:, o * 2 * w:o * 2 * w + w]
        h1 = hf[:, o * 2 * w + w:(o + 1) * 2 * w]
        cols.append(jnp.concatenate([h0, jnp.zeros((1, w), F32), h1[:0:-1]], 0))
    return jnp.concatenate(cols, 1)


@functools.lru_cache(maxsize=None)
def _dense_dft_tables(n):
    nn = 2 * n
    k = np.arange(nn)[:, None].astype(np.float64)
    m = np.arange(nn)[None, :].astype(np.float64)
    ang = -2.0 * np.pi * k * m / nn
    wr, wi = np.cos(ang), np.sin(ang)
    f_real = np.concatenate([wr, wi], 0)
    wr_h, wi_h = wr[:, :n], wi[:, :n]
    f_fwd = np.block([[wr_h, -wi_h], [wi_h, wr_h]])
    cr, ci = wr.T[:n] / nn, -wi.T[:n] / nn
    f_inv = np.block([[cr, -ci], [ci, cr]])
    return (np.asarray(f_real, np.float32), np.asarray(f_fwd, np.float32), np.asarray(f_inv, np.float32))


@functools.lru_cache(maxsize=None)
def _two_stage_dft_tables(n):
    nn = 2 * n
    n2c = FFT_N2
    n1c = nn // n2c
    n1h = n1c // 2
    k1 = np.arange(n1c).astype(np.float64)
    n1 = np.arange(n1c).astype(np.float64)
    n2 = np.arange(n2c).astype(np.float64)
    ang = -2.0 * np.pi * (k1[None, :, None] * n1[None, None, :] / n1c + n2[:, None, None] * k1[None, :, None] / nn)
    mr, mi = np.cos(ang), np.sin(ang)
    f1_real = np.concatenate([mr, mi], 1)
    mrh, mih = mr[:, :, :n1h], mi[:, :, :n1h]
    f1_cplx = np.concatenate([np.concatenate([mrh, -mih], 2), np.concatenate([mih, mrh], 2)], 1)
    gr = np.transpose(mr, (0, 2, 1))[:, :n1h, :] / nn
    gi = -np.transpose(mi, (0, 2, 1))[:, :n1h, :] / nn
    g1 = np.concatenate([np.concatenate([gr, -gi], 2), np.concatenate([gi, gr], 2)], 1)
    k2 = np.arange(n2c).astype(np.float64)
    a2 = -2.0 * np.pi * k2[:, None] * n2[None, :] / n2c
    fr, fi = np.cos(a2), np.sin(a2)
    f2 = np.block([[fr, -fi], [fi, fr]])
    f2i = np.block([[fr.T, fi.T], [-fi.T, fr.T]])
    f32 = lambda a: np.asarray(a, np.float32)
    return f32(f1_real), f32(f1_cplx), f32(g1), f32(f2), f32(f2i)


def _spec_dense_body(f_ref, x_ref, o_ref):
    o_ref[...] = _dot(f_ref[...], x_ref[...], hi=True)


def hyena_spec_dense(full, n):
    f_real, _, _ = _dense_dft_tables(n)
    nn, cols = full.shape
    return pl.pallas_call(
        _spec_dense_body,
        grid=(cols // LANE,),
        in_specs=[pl.BlockSpec((2 * nn, nn), lambda j: (0, 0)),
                  pl.BlockSpec((nn, LANE), lambda j: (0, j))],
        out_specs=pl.BlockSpec((2 * nn, LANE), lambda j: (0, j)),
        out_shape=jax.ShapeDtypeStruct((2 * nn, cols), F32),
        compiler_params=_cparams(("arbitrary",)),
        name="hyena_spec_dense",
    )(jnp.asarray(f_real), full)


def _conv_dense_body(zr_ref, zi_ref, h_ref, ff_ref, fi_ref, bias_ref, or_ref, oi_ref):
    zr, zi = zr_ref[...], zi_ref[...]
    n = zr.shape[0]
    nn = 2 * n
    x = _dot(ff_ref[...], jnp.concatenate([zr, zi], 0), hi=True)
    xr, xi = x[:nn], x[nn:]
    hr, hi_ = h_ref[0:nn, :], h_ref[nn:2 * nn, :]
    y = _dot(fi_ref[...], jnp.concatenate([xr * hr - xi * hi_, xr * hi_ + xi * hr], 0), hi=True)
    bias = bias_ref[...]
    or_ref[...] = y[:n] + zr * bias
    oi_ref[...] = y[n:] + zi * bias


def hyena_conv_dense(zsrc, zcol, row0, n, n_batch, spec, bias3, layer, order):
    _, f_fwd, f_inv = _dense_dft_tables(n)
    nn = 2 * n
    rb, cb = row0 // n, zcol // LANE
    wb = BRANCH_W // LANE
    outs = pl.pallas_call(
        _conv_dense_body,
        grid=(n_batch // 2, wb),
        in_specs=[pl.BlockSpec((n, LANE), lambda p, j: (rb + 2 * p, cb + j)),
                  pl.BlockSpec((n, LANE), lambda p, j: (rb + 2 * p + 1, cb + j)),
                  pl.BlockSpec((2 * nn, LANE), lambda p, j: (0, order * wb + j)),
                  pl.BlockSpec((2 * nn, 2 * n), lambda p, j: (0, 0)),
                  pl.BlockSpec((2 * n, 2 * nn), lambda p, j: (0, 0)),
                  pl.BlockSpec((None, 1, LANE), lambda p, j: (layer * HY_ORDER + order, 0, j))],
        out_specs=[pl.BlockSpec((None, n, LANE), lambda p, j: (p, 0, j)),
                   pl.BlockSpec((None, n, LANE), lambda p, j: (p, 0, j))],
        out_shape=[jax.ShapeDtypeStruct((n_batch // 2, n, BRANCH_W), F32)] * 2,
        compiler_params=_cparams(("arbitrary", "arbitrary")),
        name="hyena_conv_dense",
    )(zsrc, zsrc, spec, jnp.asarray(f_fwd), jnp.asarray(f_inv), bias3)
    return jnp.stack(outs, 1).reshape(n_batch * n, BRANCH_W)


def _spec_fft_body(x_ref, f1_ref, f2_ref, o_ref, a_ref):
    n1c = o_ref.shape[0]

    def stage1(n2, carry):
        xs = x_ref[pl.ds(n2, n1c, stride=FFT_N2), :]
        a_ref[n2] = _dot(f1_ref[n2], xs)
        return carry

    lax.fori_loop(0, FFT_N2, stage1, 0)

    def stage2(k1, carry):
        blk = jnp.concatenate([a_ref[:, k1, :], a_ref[:, n1c + k1, :]], 0)
        o_ref[k1] = _dot(f2_ref[...], blk)
        return carry

    lax.fori_loop(0, n1c, stage2, 0)


def hyena_spec_fft(full, n):
    f1_real, _, _, f2, _ = _two_stage_dft_tables(n)
    nn, cols = full.shape
    n1c = nn // FFT_N2
    const = lambda shape: pl.BlockSpec(shape, lambda j: (0,) * len(shape), pipeline_mode=pl.Buffered(1))
    return pl.pallas_call(
        _spec_fft_body,
        grid=(cols // LANE,),
        in_specs=[pl.BlockSpec((nn, LANE), lambda j: (0, j)),
                  const((FFT_N2, 2 * n1c, n1c)), const((2 * FFT_N2, 2 * FFT_N2))],
        out_specs=pl.BlockSpec((n1c, 2 * FFT_N2, LANE), lambda j: (0, 0, j)),
        out_shape=jax.ShapeDtypeStruct((n1c, 2 * FFT_N2, cols), F32),
        scratch_shapes=[pltpu.VMEM((FFT_N2, 2 * n1c, LANE), F32)],
        compiler_params=_cparams(("arbitrary",)),
        name="hyena_spec_fft",
    )(full, jnp.asarray(f1_real, BF16), jnp.asarray(f2, BF16))


def _conv_fft_body(zr_ref, zi_ref, h_ref, f1_ref, f2_ref, f2i_ref, g1_ref, bias_ref, or_ref, oi_ref, a_ref):
    n1c = h_ref.shape[0]
    n1h = n1c // 2
    n2c = FFT_N2

    def stage1(n2, carry):
        xs = jnp.concatenate([zr_ref[pl.ds(n2, n1h, stride=n2c), :], zi_ref[pl.ds(n2, n1h, stride=n2c), :]], 0)
        a_ref[n2] = _dot(f1_ref[n2], xs)
        return carry

    lax.fori_loop(0, n2c, stage1, 0)

    def stage2(k1, carry):
        x = _dot(f2_ref[...], jnp.concatenate([a_ref[:, k1, :], a_ref[:, n1c + k1, :]], 0))
        xr, xi = x[:n2c], x[n2c:]
        hr, hi_ = h_ref[k1, 0:n2c, :], h_ref[k1, n2c:2 * n2c, :]
        b = _dot(f2i_ref[...], jnp.concatenate([xr * hr - xi * hi_, xr * hi_ + xi * hr], 0))
        a_ref[:, k1, :] = b[:n2c]
        a_ref[:, n1c + k1, :] = b[n2c:]
        return carry

    lax.fori_loop(0, n1c, stage2, 0)
    bias = bias_ref[...]

    def stage3(n2, carry):
        y = _dot(g1_ref[n2], a_ref[n2])
        rows = pl.ds(n2, n1h, stride=n2c)
        or_ref[rows, :] = y[:n1h] + zr_ref[rows, :] * bias
        oi_ref[rows, :] = y[n1h:] + zi_ref[rows, :] * bias
        return carry

    lax.fori_loop(0, n2c, stage3, 0)


def hyena_conv_fft(zsrc, zcol, n, n_batch, spec, bias3, layer, order):
    _, f1_cplx, g1, f2, f2i = _two_stage_dft_tables(n)
    n1c = 2 * n // FFT_N2
    cb = zcol // LANE
    wb = BRANCH_W // LANE
    const = lambda shape: pl.BlockSpec(shape, lambda j, p: (0,) * len(shape), pipeline_mode=pl.Buffered(1))
    outs = pl.pallas_call(
        _conv_fft_body,
        grid=(wb, n_batch // 2),
        in_specs=[pl.BlockSpec((n, LANE), lambda j, p: (2 * p, cb + j)),
                  pl.BlockSpec((n, LANE), lambda j, p: (2 * p + 1, cb + j)),
                  pl.BlockSpec((n1c, 2 * FFT_N2, LANE), lambda j, p: (0, 0, order * wb + j),
                               pipeline_mode=pl.Buffered(1)),
                  const((FFT_N2, 2 * n1c, n1c)), const((2 * FFT_N2, 2 * FFT_N2)), const((2 * FFT_N2, 2 * FFT_N2)),
                  const((FFT_N2, n1c, 2 * n1c)),
                  pl.BlockSpec((None, 1, LANE), lambda j, p: (layer * HY_ORDER + order, 0, j))],
        out_specs=[pl.BlockSpec((None, n, LANE), lambda j, p: (p, 0, j)),
                   pl.BlockSpec((None, n, LANE), lambda j, p: (p, 0, j))],
        out_shape=[jax.ShapeDtypeStruct((n_batch // 2, n, BRANCH_W), F32)] * 2,
        scratch_shapes=[pltpu.VMEM((FFT_N2, 2 * n1c, LANE), F32)],
        compiler_params=_cparams(("arbitrary", "arbitrary")),
        name="hyena_conv_fft",
    )(zsrc, zsrc, spec, jnp.asarray(f1_cplx, BF16), jnp.asarray(f2, BF16), jnp.asarray(f2i, BF16),
      jnp.asarray(g1, BF16), bias3)
    return jnp.stack(outs, 1).reshape(n_batch * n, BRANCH_W)


def _mul_body(a_ref, b_ref, o_ref):
    o_ref[...] = a_ref[...] * b_ref[...]


def mul_cols(a, b, bcol):
    t, w = a.shape
    r = 256
    return pl.pallas_call(
        _mul_body,
        grid=(t // r,),
        in_specs=[pl.BlockSpec((r, w), lambda i: (i, 0)), pl.BlockSpec((r, w), lambda i: (i, bcol // w))],
        out_specs=pl.BlockSpec((r, w), lambda i: (i, 0)),
        out_shape=jax.ShapeDtypeStruct((t, w), F32),
        compiler_params=_cparams(("arbitrary",)),
        name="mul_cols",
    )(a, b)


def _swap_pairs(x):
    w = x.shape[-1]
    lane = lax.broadcasted_iota(jnp.int32, x.shape, x.ndim - 1)
    return jnp.where(lane % 2 == 0, pltpu.roll(x, w - 1, x.ndim - 1), pltpu.roll(x, 1, x.ndim - 1))


def _attn_prep_body(g_ref, dq_ref, dk_ref, dv_ref, cg_ref, sg_ref, cd_ref, sd_ref, qn_ref, kn_ref,
                    qg_ref, kg_ref, vg_ref, qd_ref, kd_ref, vd_ref, *, lat_blocks):
    is_lat = pl.program_id(0) < lat_blocks
    cg = jnp.where(is_lat, cg_ref[...], 1.0)
    sg = jnp.where(is_lat, sg_ref[...], 0.0)
    cd = jnp.where(is_lat, cd_ref[...], 1.0)
    sd = jnp.where(is_lat, sd_ref[...], 0.0)

    def rope(x, cs, sn):
        return x * cs + _swap_pairs(x) * sn

    def rms(x, w):
        return x * lax.rsqrt(jnp.mean(x * x, axis=-1, keepdims=True) + EPS) * w

    for h in range(HEADS):
        sl = slice(h * HEAD_D, (h + 1) * HEAD_D)
        q = rope(rms(g_ref[:, sl], qn_ref[...]), cg, sg)
        qg_ref[:, sl] = (q * HEAD_D ** -0.5).astype(BF16)
        qd_ref[:, sl] = (rope(dq_ref[:, sl], cd, sd) * DIFF_QK ** -0.5).astype(BF16)
        kd_ref[:, sl] = rope(dk_ref[:, sl], cd, sd).astype(BF16)
    for h in range(GQA_KV):
        sl = slice(h * HEAD_D, (h + 1) * HEAD_D)
        kin = g_ref[:, BRANCH_W + h * HEAD_D:BRANCH_W + (h + 1) * HEAD_D]
        kg_ref[:, sl] = rope(rms(kin, kn_ref[...]), cg, sg).astype(BF16)
    vg_ref[...] = g_ref[:, BRANCH_W + GQA_KV * HEAD_D:BRANCH_W + 2 * GQA_KV * HEAD_D].astype(BF16)
    vd_ref[...] = dv_ref[...].astype(BF16)


def attn_prep(p, ropes, qn3, kn3, layer, n_lat, n_ctx, n_batch):
    t = p.shape[0]
    r = 256 if n_ctx % 256 == 0 else n_ctx
    nlb, ncb = n_lat // r, n_ctx // r
    lat_blocks = n_batch * nlb
    kvw = GQA_KV * HEAD_D
    w = BRANCH_W

    def kv_row(i):
        lat = (i // nlb) * (nlb + ncb) + ncb + i % nlb
        j = i - lat_blocks
        ctx = (j // ncb) * (nlb + ncb) + j % ncb
        return jnp.where(i < lat_blocks, lat, ctx)

    rope_spec = pl.BlockSpec((r, LANE), lambda i: (jnp.where(i < lat_blocks, i % nlb, 0), 0))
    nkv = n_batch * (n_lat + n_ctx)
    return pl.pallas_call(
        functools.partial(_attn_prep_body, lat_blocks=lat_blocks),
        grid=(t // r,),
        in_specs=[pl.BlockSpec((r, 2 * w), lambda i: (i, C_GQA_QKV // (2 * w))),
                  pl.BlockSpec((r, w), lambda i: (i, C_DIFF_Q // w)),
                  pl.BlockSpec((r, w), lambda i: (i, C_DIFF_K // w)),
                  pl.BlockSpec((r, w), lambda i: (i, C_DIFF_V // w)),
                  rope_spec, rope_spec, rope_spec, rope_spec,
                  pl.BlockSpec((None, 1, LANE), lambda i: (layer, 0, 0)),
                  pl.BlockSpec((None, 1, LANE), lambda i: (layer, 0, 0))],
        out_specs=[pl.BlockSpec((r, w), lambda i: (i, 0)),
                   pl.BlockSpec((r, kvw), lambda i: (kv_row(i), 0)),
                   pl.BlockSpec((r, kvw), lambda i: (kv_row(i), 0)),
                   pl.BlockSpec((r, w), lambda i: (i, 0)),
                   pl.BlockSpec((r, w), lambda i: (kv_row(i), 0)),
                   pl.BlockSpec((r, w), lambda i: (kv_row(i), 0))],
        out_shape=[jax.ShapeDtypeStruct((t, w), BF16), jax.ShapeDtypeStruct((nkv, kvw), BF16),
                   jax.ShapeDtypeStruct((nkv, kvw), BF16), jax.ShapeDtypeStruct((t, w), BF16),
                   jax.ShapeDtypeStruct((nkv, w), BF16), jax.ShapeDtypeStruct((nkv, w), BF16)],
        compiler_params=_cparams(("arbitrary",)),
        name="attn_prep",
    )(p, p, p, p, *ropes, qn3, kn3)


def _softmax_parts(s):
    e = jnp.exp(s - jnp.max(s, axis=-1, keepdims=True))
    return e, jnp.sum(e, axis=-1, keepdims=True)


def _gqa_body(q_ref, k_ref, v_ref, o_ref):
    group = HEADS // GQA_KV
    for kvh in range(GQA_KV):
        k = k_ref[:, kvh * HEAD_D:(kvh + 1) * HEAD_D]
        v = v_ref[:, kvh * HEAD_D:(kvh + 1) * HEAD_D]
        for g in range(group):
            sl = slice((kvh * group + g) * HEAD_D, (kvh * group + g + 1) * HEAD_D)
            s = lax.dot_general(q_ref[:, sl], k, (((1,), (1,)), ((), ())), preferred_element_type=F32)
            e, l = _softmax_parts(s)
            o_ref[:, sl] = jnp.dot(e.astype(BF16), v, preferred_element_type=F32) / l


def _diff_body(q_ref, k_ref, v_ref, lam_ref, o_ref, *, lam_init):
    lam4 = lam_ref[...]
    lam = (jnp.exp(jnp.sum(lam4[0:1] * lam4[1:2], axis=-1, keepdims=True))
           - jnp.exp(jnp.sum(lam4[2:3] * lam4[3:4], axis=-1, keepdims=True)) + lam_init)
    dn = (((1,), (1,)), ((), ()))
    for h in range(HEADS):
        sl = slice(h * HEAD_D, (h + 1) * HEAD_D)
        q = q_ref[:, sl]
        k = k_ref[:, sl]
        v = v_ref[:, sl]
        first = lax.broadcasted_iota(jnp.int32, q.shape, 1) < DIFF_QK
        zero = jnp.zeros_like(q)
        e1, l1 = _softmax_parts(lax.dot_general(jnp.where(first, q, zero), k, dn, preferred_element_type=F32))
        e2, l2 = _softmax_parts(lax.dot_general(jnp.where(first, zero, q), k, dn, preferred_element_type=F32))
        o1 = jnp.dot(e1.astype(BF16), v, preferred_element_type=F32) / l1
        o2 = jnp.dot(e2.astype(BF16), v, preferred_element_type=F32) / l2
        o_ref[:, sl] = o1 - lam * o2


def attention(body, q, k, v, extra, extra_specs, q_row0, nq, kv_per_batch, kv_len, n_batch, tq, name):
    w = q.shape[1]
    qb0 = q_row0 // tq
    nqb = nq // tq
    kvb = kv_per_batch // kv_len
    return pl.pallas_call(
        body,
        grid=(n_batch, nqb),
        in_specs=[pl.BlockSpec((tq, w), lambda b, i: (qb0 + b * nqb + i, 0)),
                  pl.BlockSpec((kv_len, k.shape[1]), lambda b, i: (b * kvb, 0)),
                  pl.BlockSpec((kv_len, v.shape[1]), lambda b, i: (b * kvb, 0))] + extra_specs,
        out_specs=pl.BlockSpec((tq, w), lambda b, i: (b * nqb + i, 0)),
        out_shape=jax.ShapeDtypeStruct((n_batch * nq, w), F32),
        compiler_params=_cparams(("arbitrary", "arbitrary")),
        name=name,
    )(q, k, v, *extra)


def _merge_body(h_ref, mod_ref, mg_ref, of_ref, ob_ref, ggate_ref, y1_ref, x2_ref, hgate_ref, oc_ref, cgate_ref,
                od_ref, dgate_ref, gnorm_ref, dnorm_ref, wbr_ref, wout_ref, lng_ref, lnb_ref, o_ref, *, diff_scale):
    def rms_heads(x, w):
        parts = []
        for h in range(HEADS):
            xh = x[:, h * HEAD_D:(h + 1) * HEAD_D]
            parts.append(xh * lax.rsqrt(jnp.mean(xh * xh, axis=-1, keepdims=True) + EPS) * w)
        return jnp.concatenate(parts, -1)

    ys = (rms_heads(of_ref[...] + ob_ref[...], gnorm_ref[...]) * _silu(ggate_ref[...]),
          x2_ref[...] * y1_ref[...] * _silu(hgate_ref[...]),
          oc_ref[...] * _silu(cgate_ref[...]),
          rms_heads(od_ref[...], dnorm_ref[...]) * diff_scale * _silu(dgate_ref[...]))
    acc = None
    for n in range(N_BRANCH):
        proj = jnp.dot(ys[n].astype(BF16), wbr_ref[n], preferred_element_type=F32)
        term = _sigmoid(mg_ref[:, n * D_MODEL:(n + 1) * D_MODEL]) * proj
        acc = term if acc is None else acc + term
    out = jnp.dot(acc.astype(BF16), wout_ref[...], preferred_element_type=F32)
    x = ALPHA * h_ref[...] + mod_ref[2:3, :] * out
    mu = jnp.mean(x, axis=-1, keepdims=True)
    xc = x - mu
    var = jnp.mean(xc * xc, axis=-1, keepdims=True)
    o_ref[...] = xc * lax.rsqrt(var + EPS) * lng_ref[...] + lnb_ref[...]


def merge_postnorm(h_all, mod3, p, o_f, o_b, y1, xv, oc, od, gnorm3, dnorm3, wbr, wout, lng3, lnb3, layer, lam_init,
                   n_lat, n_batch):
    t, d = h_all.shape
    r = 256 if n_lat % 256 == 0 else 64
    w = BRANCH_W
    lbb = n_lat // r
    row = lambda i: jnp.minimum(i // lbb, n_batch)
    tok = lambda cb: pl.BlockSpec((r, w), lambda i: (i, cb))
    vec = lambda width: pl.BlockSpec((None, 1, width), lambda i: (layer, 0, 0))
    return pl.pallas_call(
        functools.partial(_merge_body, diff_scale=1.0 - lam_init),
        grid=(t // r,),
        in_specs=[pl.BlockSpec((r, d), lambda i: (i, 0)),
                  pl.BlockSpec((None, 3, d), lambda i: (row(i), 0, 0)),
                  pl.BlockSpec((r, N_BRANCH * d), lambda i: (i, C_MERGE // (N_BRANCH * d))),
                  tok(0), tok(0), tok(C_GDN_GATE // w), tok(0), tok(1), tok(C_HY_GATE // w), tok(0),
                  tok(C_GQA_GATE // w), tok(0), tok(C_DIFF_GATE // w),
                  vec(LANE), vec(LANE),
                  pl.BlockSpec((None, N_BRANCH, w, d), lambda i: (layer, 0, 0, 0)),
                  pl.BlockSpec((None, d, d), lambda i: (layer, 0, 0)),
                  vec(d), vec(d)],
        out_specs=pl.BlockSpec((r, d), lambda i: (i, 0)),
        out_shape=jax.ShapeDtypeStruct((t, d), F32),
        compiler_params=_cparams(("arbitrary",)),
        name="merge_postnorm",
    )(h_all, mod3, p, o_f, o_b, p, y1, xv, p, oc, p, od, p, gnorm3, dnorm3, wbr, wout, lng3, lnb3)


def _rope_tables(n_lat, dim):
    rows = n_lat // GRID_W
    row = jnp.repeat(jnp.arange(rows, dtype=F32), GRID_W)
    col = jnp.tile(jnp.arange(GRID_W, dtype=F32), rows)
    half = dim // 2
    inv = ROPE_THETA ** (-jnp.arange(0, half, 2, dtype=F32) / half)
    ang = jnp.concatenate([row[:, None] * inv, col[:, None] * inv], -1)
    cos = jnp.repeat(jnp.cos(ang), 2, axis=-1)
    sin = jnp.repeat(jnp.sin(ang), 2, axis=-1)
    sign = jnp.tile(jnp.array([-1.0, 1.0], F32), dim // 2)
    reps = LANE // dim
    return jnp.tile(cos, (1, reps)), jnp.tile(sin * sign, (1, reps))


def kernel(x, c, ctx, c_ctx, w_ada, b_ada, w_in, gdn_conv, gdn_a_log, gdn_dt_bias, gdn_norm, hy_conv, hy_w1, hy_b1,
           hy_w2, hy_b2, hy_w3, hy_b3, hy_w4, hy_freq, hy_bias, gqa_qn, gqa_kn, diff_lam, diff_norm, w_br, w_out,
           ln_g, ln_b):
    nb, n_lat, d = x.shape
    n_ctx = ctx.shape[1]
    t_lat, t_ctx = nb * n_lat, nb * n_ctx
    depth = w_in.shape[0]
    w = BRANCH_W

    w_main = jnp.concatenate([w_in[:, :, O_MERGE:], w_in[:, :, :O_GDN_AB], w_in[:, :, O_GDN_AB + 4 * HEADS:O_MERGE]],
                             axis=2).astype(BF16)
    w_ab = jnp.pad(w_in[:, :, O_GDN_AB:O_GDN_AB + 4 * HEADS], ((0, 0), (0, 0), (0, LANE - 4 * HEADS)))
    wbr_bf = w_br.astype(BF16)
    wout_bf = w_out.astype(BF16)
    b_ada3 = b_ada[:, None, :]
    cvec = jnp.concatenate([c, c_ctx[None, :], jnp.zeros((SUB - nb - 1, d), F32)], 0)
    as3 = lambda a: a[:, None, :]
    gdn_par_r = jnp.pad(jnp.stack([gdn_a_log.reshape(depth, -1), gdn_dt_bias.reshape(depth, -1)], 1),
                        ((0, 0), (0, SUB - 2), (0, LANE - 2 * HEADS)))
    gdn_par_c = jnp.pad(jnp.stack([gdn_a_log.reshape(depth, -1), gdn_dt_bias.reshape(depth, -1)], 2),
                        ((0, 0), (0, 2 * HEADS), (0, LANE - 2)))
    hy_w1p = jnp.pad(hy_w1, ((0, 0), (0, LANE - HY_EMB), (0, 0)))
    hy_bias3 = hy_bias.reshape(depth * HY_ORDER, 1, w)
    ropes = _rope_tables(n_lat, HEAD_D) + _rope_tables(n_lat, DIFF_QK)

    tm = 1024 if (n_lat % 1024 == 0 and t_ctx % 1024 == 0) else n_ctx
    h_all = jnp.concatenate([x.reshape(t_lat, d), ctx.reshape(t_ctx, d)], 0)
    for l in range(depth):
        lam_init = 0.8 - 0.6 * math.exp(-0.3 * l)
        mod3 = ada_mod(cvec, w_ada, b_ada3, l).reshape(SUB, 3, d)
        p, ab = in_proj(h_all, mod3, w_main, w_ab, l, tm, n_lat // tm, nb)

        qkv = dwconv(p, gdn_conv, l, C_GDN_QKV, 3 * w, n_lat, n_ctx, nb, act=True)
        ab_rows = jnp.transpose(ab[:, :4 * HEADS].reshape(-1, GDN_CHUNK, 4 * HEADS), (0, 2, 1))
        o_f, o_b = gdn_scan(qkv, ab, ab_rows, gdn_par_r[l], gdn_par_c[l], n_lat, n_ctx, nb)

        xv = dwconv(p, hy_conv, l, C_HY_XV, 3 * w, n_lat, n_ctx, nb, act=False)
        filt = lambda n: _two_sided(hyena_filter(n, hy_w1p, as3(hy_b1), hy_w2, as3(hy_b2), hy_w3, as3(hy_b3), hy_w4,
                                                 as3(hy_freq), l), n)
        spec_lat = hyena_spec_fft(filt(n_lat), n_lat)
        spec_ctx = hyena_spec_dense(filt(n_ctx), n_ctx)
        y0 = jnp.concatenate([hyena_conv_fft(xv, 2 * w, n_lat, nb, spec_lat, hy_bias3, l, 0),
                              hyena_conv_dense(xv, 2 * w, t_lat, n_ctx, nb, spec_ctx, hy_bias3, l, 0)], 0)
        z1 = mul_cols(y0, xv, 0)
        y1 = jnp.concatenate([hyena_conv_fft(z1, 0, n_lat, nb, spec_lat, hy_bias3, l, 1),
                              hyena_conv_dense(z1, 0, t_lat, n_ctx, nb, spec_ctx, hy_bias3, l, 1)], 0)

        qg, kg, vg, qd, kd, vd = attn_prep(p, ropes, as3(gqa_qn), as3(gqa_kn), l, n_lat, n_ctx, nb)
        kv_all = n_lat + n_ctx
        tq = min(256, n_ctx)
        lam_spec = [pl.BlockSpec((None, 4, DIFF_QK), lambda b, i: (l, 0, 0))]
        diff_body = functools.partial(_diff_body, lam_init=lam_init)
        oc = jnp.concatenate([
            attention(_gqa_body, qg, kg, vg, (), [], 0, n_lat, kv_all, kv_all, nb, tq, "gqa_lat"),
            attention(_gqa_body, qg, kg, vg, (), [], t_lat, n_ctx, kv_all, n_ctx, nb, tq, "gqa_ctx")], 0)
        od = jnp.concatenate([
            attention(diff_body, qd, kd, vd, (diff_lam,), lam_spec, 0, n_lat, kv_all, kv_all, nb, tq, "diff_lat"),
            attention(diff_body, qd, kd, vd, (diff_lam,), lam_spec, t_lat, n_ctx, kv_all, n_ctx, nb, tq, "diff_ctx")], 0)

        h_all = merge_postnorm(h_all, mod3, p, o_f, o_b, y1, xv, oc, od, as3(gdn_norm), as3(diff_norm), wbr_bf, wout_bf,
                               as3(ln_g), as3(ln_b), l, lam_init, n_lat, nb)
    return h_all[:t_lat].reshape(nb, n_lat, d)
```

```python
import functools
import math

import numpy as np
import jax
import jax.numpy as jnp
from jax import lax
from jax.experimental import pallas as pl
from jax.experimental.pallas import tpu as pltpu

F32 = jnp.float32
BF16 = jnp.bfloat16
HI = lax.Precision.HIGHEST

D_MODEL = 1024
DEPTH = 4
GRID_W = 64
BRANCH_W = D_MODEL // 2
N_BRANCH = 4
HEADS = 4
HEAD_D = BRANCH_W // HEADS
GDN_CONV = 4
GDN_CHUNK = 64
HY_CONV = 3
HY_EMB = 33
HY_BANDS = (HY_EMB - 1) // 2
HY_FH = 64
HY_ORDER = 2
HY_MIN_DECAY = math.log(1e-2) / 1.5
HY_MAX_DECAY = math.log(1e-2) / 0.3
GQA_KV = 2
DIFF_QK = HEAD_D // 2
ROPE_THETA = 10000.0
EPS = 1e-6
ALPHA = (2.0 * DEPTH) ** 0.25

LANE = 128
SUB = 8
FFT_N2 = 128
VMEM_LIMIT = 60 * 1024 * 1024

C_MERGE = 0
C_GDN_QKV = 4096
C_GDN_GATE = 5632
C_HY_XV = 6144
C_HY_GATE = 7680
C_GQA_QKV = 8192
C_GQA_GATE = 9216
C_DIFF_Q = 9728
C_DIFF_K = 10240
C_DIFF_V = 10752
C_DIFF_GATE = 11264
N_MAIN = 11776
O_GDN_AB = 1536
O_MERGE = 7696


def _cparams(sem):
    return pltpu.CompilerParams(dimension_semantics=sem, vmem_limit_bytes=VMEM_LIMIT)


def _dot(a, b, hi=False):
    if hi:
        return jnp.dot(a, b, precision=HI, preferred_element_type=F32)
    return jnp.dot(a.astype(BF16), b.astype(BF16), preferred_element_type=F32)


def _dot_nt(a, b, hi=False):
    dn = (((1,), (1,)), ((), ()))
    if hi:
        return lax.dot_general(a, b, dn, precision=HI, preferred_element_type=F32)
    return lax.dot_general(a.astype(BF16), b.astype(BF16), dn, preferred_element_type=F32)


def _dot_tn(a, b):
    return lax.dot_general(a.astype(BF16), b.astype(BF16), (((0,), (0,)), ((), ())), preferred_element_type=F32)


def _sigmoid(x):
    return 1.0 / (1.0 + jnp.exp(-x))


def _silu(x):
    return x * _sigmoid(x)


def _softplus(x):
    return jnp.maximum(x, 0.0) + jnp.log1p(jnp.exp(-jnp.abs(x)))


def _ada_body(c_ref, w_ref, b_ref, o_ref):
    o_ref[...] = _dot(_silu(c_ref[...]), w_ref[...], hi=True) + b_ref[...]


def ada_mod(cvec, w_ada, b_ada3, layer):
    d = cvec.shape[1]
    tn = 512
    return pl.pallas_call(
        _ada_body,
        grid=(3 * d // tn,),
        in_specs=[pl.BlockSpec((SUB, d), lambda j: (0, 0)),
                  pl.BlockSpec((None, d, tn), lambda j: (layer, 0, j)),
                  pl.BlockSpec((None, 1, tn), lambda j: (layer, 0, j))],
        out_specs=pl.BlockSpec((SUB, tn), lambda j: (0, j)),
        out_shape=jax.ShapeDtypeStruct((SUB, 3 * d), F32),
        compiler_params=_cparams(("arbitrary",)),
        name="ada_mod",
    )(cvec, w_ada, b_ada3)


def _inproj_body(h_ref, mod_ref, w_ref, wab_ref, o_ref, ab_ref, u_ref):
    @pl.when(pl.program_id(1) == 0)
    def _():
        x = h_ref[...]
        mu = jnp.mean(x, axis=-1, keepdims=True)
        xc = x - mu
        var = jnp.mean(xc * xc, axis=-1, keepdims=True)
        u = xc * lax.rsqrt(var + EPS) * (1.0 + mod_ref[1:2, :]) + mod_ref[0:1, :]
        u_ref[...] = u.astype(BF16)
        ab_ref[...] = _dot(u, wab_ref[...], hi=True)

    o_ref[...] = jnp.dot(u_ref[...], w_ref[...], preferred_element_type=F32)


def in_proj(h_all, mod3, w_main, w_ab, layer, tm, lat_blocks_per_batch, n_batch):
    t, d = h_all.shape
    tn = 512
    n_main = w_main.shape[2]
    row = lambda i: jnp.minimum(i // lat_blocks_per_batch, n_batch)
    return pl.pallas_call(
        _inproj_body,
        grid=(t // tm, n_main // tn),
        in_specs=[pl.BlockSpec((tm, d), lambda i, j: (i, 0)),
                  pl.BlockSpec((None, 3, d), lambda i, j: (row(i), 0, 0)),
                  pl.BlockSpec((None, d, tn), lambda i, j: (layer, 0, j)),
                  pl.BlockSpec((None, d, LANE), lambda i, j: (layer, 0, 0))],
        out_specs=[pl.BlockSpec((tm, tn), lambda i, j: (i, j)),
                   pl.BlockSpec((tm, LANE), lambda i, j: (i, 0))],
        out_shape=[jax.ShapeDtypeStruct((t, n_main), F32), jax.ShapeDtypeStruct((t, LANE), F32)],
        scratch_shapes=[pltpu.VMEM((tm, d), BF16)],
        compiler_params=_cparams(("arbitrary", "arbitrary")),
        name="in_proj",
    )(h_all, mod3, w_main, w_ab)


def _dwconv_body(xp_ref, x_ref, xn_ref, w_ref, o_ref, pad_ref, *, taps, pad_l, lat_blocks, bl, bc, act):
    i = pl.program_id(0)
    r = x_ref.shape[0]
    is_lat = i < lat_blocks
    pos = jnp.where(is_lat, i % bl, (i - lat_blocks) % bc)
    last = jnp.where(is_lat, bl - 1, bc - 1)
    pad_ref[0:SUB, :] = jnp.where(pos == 0, 0.0, xp_ref[...])
    pad_ref[SUB:SUB + r, :] = x_ref[...]
    pad_ref[SUB + r:2 * SUB + r, :] = jnp.where(pos == last, 0.0, xn_ref[...])
    acc = None
    for j in range(taps):
        off = SUB + j - pad_l
        term = w_ref[j:j + 1, :] * pad_ref[off:off + r, :]
        acc = term if acc is None else acc + term
    if act:
        acc = _silu(acc)
    o_ref[...] = acc


def dwconv(p, w_conv, layer, col0, width, n_lat, n_ctx, n_batch, act):
    t = p.shape[0]
    taps = w_conv.shape[1]
    r = 256 if n_ctx % 256 == 0 else n_ctx
    lw = 256
    cb = col0 // lw
    rs = r // SUB
    body = functools.partial(_dwconv_body, taps=taps, pad_l=(taps - 1) // 2, lat_blocks=n_batch * n_lat // r,
                             bl=n_lat // r, bc=n_ctx // r, act=act)
    return pl.pallas_call(
        body,
        grid=(t // r, width // lw),
        in_specs=[pl.BlockSpec((SUB, lw), lambda i, j: (jnp.maximum(i * rs - 1, 0), cb + j)),
                  pl.BlockSpec((r, lw), lambda i, j: (i, cb + j)),
                  pl.BlockSpec((SUB, lw), lambda i, j: (jnp.minimum((i + 1) * rs, t // SUB - 1), cb + j)),
                  pl.BlockSpec((None, taps, lw), lambda i, j: (layer, 0, j))],
        out_specs=pl.BlockSpec((r, lw), lambda i, j: (i, j)),
        out_shape=jax.ShapeDtypeStruct((t, width), F32),
        scratch_shapes=[pltpu.VMEM((r + 2 * SUB, lw), F32)],
        compiler_params=_cparams(("arbitrary", "arbitrary")),
        name="dwconv",
    )(p, p, p, w_conv)


def _gdn_body(qf_ref, qb_ref, abcf_ref, abcb_ref, abrf_ref, abrb_ref, pr_ref, pc_ref, of_ref, ob_ref, s_ref):
    c = GDN_CHUNK

    @pl.when(pl.program_id(1) == 0)
    def _():
        s_ref[...] = jnp.zeros_like(s_ref)

    ii = lax.broadcasted_iota(jnp.int32, (c, c), 0)
    jj = lax.broadcasted_iota(jnp.int32, (c, c), 1)
    lmat = (jj <= ii).astype(F32)
    eye = (jj == ii).astype(F32)
    alr, dtr = pr_ref[0:1, :], pr_ref[1:2, :]
    alc, dtc = pc_ref[:, 0:1], pc_ref[:, 1:2]
    chains = []
    for d in range(2):
        qkv_ref = (qf_ref, qb_ref)[d]
        abc = (abcf_ref, abcb_ref)[d][...]
        abr = (abrf_ref, abrb_ref)[d][...]
        g_c = -jnp.exp(alr) * _softplus(abc + dtr)
        g_r = -jnp.exp(alc) * _softplus(abr + dtc)
        cum_c = _dot(lmat, g_c, hi=True)
        cum_r = _dot_nt(g_r, lmat, hi=True)
        if d == 1:
            cum_c = cum_c[c - 1:c, :] - cum_c + g_c
            cum_r = cum_r[:, c - 1:c] - cum_r + g_r
        beta_all = _sigmoid(abc)
        incl = (jj <= ii) if d == 0 else (jj >= ii)
        strict = (jj < ii) if d == 0 else (jj > ii)
        for h in range(HEADS):
            idx = HEADS * d + h
            q = qkv_ref[:, h * HEAD_D:(h + 1) * HEAD_D]
            k = qkv_ref[:, BRANCH_W + h * HEAD_D:BRANCH_W + (h + 1) * HEAD_D]
            v = qkv_ref[:, 2 * BRANCH_W + h * HEAD_D:2 * BRANCH_W + (h + 1) * HEAD_D]
            q = q * lax.rsqrt(jnp.sum(q * q, axis=-1, keepdims=True) + EPS) * (HEAD_D ** -0.5)
            k = k * lax.rsqrt(jnp.sum(k * k, axis=-1, keepdims=True) + EPS)
            cc = cum_c[:, idx:idx + 1]
            cr = cum_r[idx:idx + 1, :]
            dec = jnp.exp(jnp.where(incl, cc - cr, -1e30))
            beta = beta_all[:, 2 * HEADS + idx:2 * HEADS + idx + 1]
            ecum = jnp.exp(cc)
            tot = cc[c - 1:c, :] if d == 0 else cc[0:1, :]
            chains.append(dict(d=d, h=h, q=q, k=k, dec=dec, strict=strict, beta=beta, ecum=ecum, tot=tot,
                               rhs=jnp.concatenate([k * (beta * ecum), v * beta], 1),
                               k_tail=k * jnp.exp(tot - cc)))
    for ch in chains:
        ch["kk"] = _dot_nt(ch["k"], ch["k"])
        ch["qk"] = _dot_nt(ch["q"], ch["k"])
    for ch in chains:
        ch["p"] = -jnp.where(ch["strict"], ch["beta"] * ch["kk"] * ch["dec"], 0.0)
        ch["inv"] = eye + ch["p"]
    for _ in range(int(math.log2(c)) - 1):
        for ch in chains:
            ch["p"] = _dot(ch["p"], ch["p"])
        for ch in chains:
            ch["inv"] = ch["inv"] + _dot(ch["inv"], ch["p"])
    for ch in chains:
        ch["wu"] = _dot(ch["inv"], ch["rhs"])
    for ch in chains:
        ch["s"] = s_ref[ch["d"], ch["h"]]
        ch["ws"] = _dot(jnp.concatenate([ch["wu"][:, :HEAD_D], ch["q"] * ch["ecum"]], 0), ch["s"])
    for ch in chains:
        ch["v_new"] = ch["wu"][:, HEAD_D:] - ch["ws"][:c]
    for ch in chains:
        out_ref = (of_ref, ob_ref)[ch["d"]]
        h = ch["h"]
        out_ref[:, h * HEAD_D:(h + 1) * HEAD_D] = ch["ws"][c:] + _dot(ch["qk"] * ch["dec"], ch["v_new"])
        s_ref[ch["d"], h] = ch["s"] * jnp.exp(ch["tot"]) + _dot_tn(ch["k_tail"], ch["v_new"])


def gdn_scan(qkv, ab, ab_rows, par_r, par_c, n_lat, n_ctx, n_batch):
    t = qkv.shape[0]
    c = GDN_CHUNK
    nlc, ncc = n_lat // c, n_ctx // c
    base = n_batch * nlc

    def fwd(b, s):
        return jnp.where(s < ncc, base + b * ncc + s, b * nlc + (s - ncc))

    def bwd(b, s):
        return jnp.where(s < ncc, base + b * ncc + (ncc - 1 - s), b * nlc + (nlc - 1 - (s - ncc)))

    w3 = 3 * BRANCH_W
    return pl.pallas_call(
        _gdn_body,
        grid=(n_batch, ncc + nlc),
        in_specs=[pl.BlockSpec((c, w3), lambda b, s: (fwd(b, s), 0)),
                  pl.BlockSpec((c, w3), lambda b, s: (bwd(b, s), 0)),
                  pl.BlockSpec((c, LANE), lambda b, s: (fwd(b, s), 0)),
                  pl.BlockSpec((c, LANE), lambda b, s: (bwd(b, s), 0)),
                  pl.BlockSpec((None, 4 * HEADS, c), lambda b, s: (fwd(b, s), 0, 0)),
                  pl.BlockSpec((None, 4 * HEADS, c), lambda b, s: (bwd(b, s), 0, 0)),
                  pl.BlockSpec((SUB, LANE), lambda b, s: (0, 0)),
                  pl.BlockSpec((4 * HEADS, LANE), lambda b, s: (0, 0))],
        out_specs=[pl.BlockSpec((c, BRANCH_W), lambda b, s: (fwd(b, s), 0)),
                   pl.BlockSpec((c, BRANCH_W), lambda b, s: (bwd(b, s), 0))],
        out_shape=[jax.ShapeDtypeStruct((t, BRANCH_W), F32), jax.ShapeDtypeStruct((t, BRANCH_W), F32)],
        scratch_shapes=[pltpu.VMEM((2, HEADS, HEAD_D, HEAD_D), F32)],
        compiler_params=_cparams(("arbitrary", "arbitrary")),
        name="gdn_scan",
    )(qkv, qkv, ab, ab, ab_rows, ab_rows, par_r, par_c)


def _hyfilt_body(z_ref, t_ref, w1_ref, b1_ref, w2_ref, b2_ref, w3_ref, b3_ref, w4_ref, fr_ref, dl_ref, o_ref):
    fr = fr_ref[...]
    h = jnp.sin(fr * (_dot(z_ref[...], w1_ref[...], hi=True) + b1_ref[...]))
    h = jnp.sin(fr * (_dot(h, w2_ref[...], hi=True) + b2_ref[...]))
    h = jnp.sin(fr * (_dot(h, w3_ref[...], hi=True) + b3_ref[...]))
    o_ref[...] = _dot(h, w4_ref[...], hi=True) * jnp.exp(-t_ref[...] * dl_ref[...])


def hyena_filter(n, w1p, b1, w2, b2, w3, b3, w4, fr, layer):
    pos = jnp.arange(n, dtype=F32)
    tt = pos / max(n - 1, 1)
    ang = (2.0 * math.pi / n) * pos[:, None] * jnp.linspace(1e-4, HY_BANDS - 1, HY_BANDS, dtype=F32)
    z = jnp.concatenate([tt[:, None], jnp.cos(ang), -jnp.sin(ang), jnp.zeros((n, LANE - HY_EMB), F32)], -1)
    deltas = jnp.abs(jnp.linspace(HY_MIN_DECAY, HY_MAX_DECAY, BRANCH_W, dtype=F32))
    dl = jnp.tile(deltas, 2 * HY_ORDER)[None, :]
    r = min(n, 512)
    wo = 2 * HY_ORDER * BRANCH_W
    full = lambda shape: pl.BlockSpec((None,) + shape, lambda i: (layer,) + (0,) * len(shape))
    return pl.pallas_call(
        _hyfilt_body,
        grid=(n // r,),
        in_specs=[pl.BlockSpec((r, LANE), lambda i: (i, 0)),
                  pl.BlockSpec((r, 1), lambda i: (i, 0)),
                  full((LANE, HY_FH)), full((1, HY_FH)), full((HY_FH, HY_FH)), full((1, HY_FH)),
                  full((HY_FH, HY_FH)), full((1, HY_FH)), full((HY_FH, wo)), full((1, HY_FH)),
                  pl.BlockSpec((1, wo), lambda i: (0, 0))],
        out_specs=pl.BlockSpec((r, wo), lambda i: (i, 0)),
        out_shape=jax.ShapeDtypeStruct((n, wo), F32),
        compiler_params=_cparams(("arbitrary",)),
        name="hyena_filter",
    )(z, tt[:, None], w1p, b1, w2, b2, w3, b3, w4, fr, dl)


def _two_sided(hf, n):
    w = BRANCH_W
    cols = []
    for o in range(HY_ORDER):
        h0 = hf[:, o * 2 * w:o * 2 * w + w]
        h1 = hf[:, o * 2 * w + w:(o + 1) * 2 * w]
        cols.append(jnp.concatenate([h0, jnp.zeros((1, w), F32), h1[:0:-1]], 0))
    return jnp.concatenate(cols, 1)


@functools.lru_cache(maxsize=None)
def _dense_dft_tables(n):
    nn = 2 * n
    k = np.arange(nn)[:, None].astype(np.float64)
    m = np.arange(nn)[None, :].astype(np.float64)
    ang = -2.0 * np.pi * k * m / nn
    wr, wi = np.cos(ang), np.sin(ang)
    f_real = np.concatenate([wr, wi], 0)
    wr_h, wi_h = wr[:, :n], wi[:, :n]
    f_fwd = np.block([[wr_h, -wi_h], [wi_h, wr_h]])
    cr, ci = wr.T[:n] / nn, -wi.T[:n] / nn
    f_inv = np.block([[cr, -ci], [ci, cr]])
    return (np.asarray(f_real, np.float32), np.asarray(f_fwd, np.float32), np.asarray(f_inv, np.float32))


@functools.lru_cache(maxsize=None)
def _two_stage_dft_tables(n):
    nn = 2 * n
    n2c = FFT_N2
    n1c = nn // n2c
    n1h = n1c // 2
    k1 = np.arange(n1c).astype(np.float64)
    n1 = np.arange(n1c).astype(np.float64)
    n2 = np.arange(n2c).astype(np.float64)
    ang = -2.0 * np.pi * (k1[None, :, None] * n1[None, None, :] / n1c + n2[:, None, None] * k1[None, :, None] / nn)
    mr, mi = np.cos(ang), np.sin(ang)
    f1_real = np.concatenate([mr, mi], 1)
    mrh, mih = mr[:, :, :n1h], mi[:, :, :n1h]
    f1_cplx = np.concatenate([np.concatenate([mrh, -mih], 2), np.concatenate([mih, mrh], 2)], 1)
    gr = np.transpose(mr, (0, 2, 1))[:, :n1h, :] / nn
    gi = -np.transpose(mi, (0, 2, 1))[:, :n1h, :] / nn
    g1 = np.concatenate([np.concatenate([gr, -gi], 2), np.concatenate([gi, gr], 2)], 1)
    k2 = np.arange(n2c).astype(np.float64)
    a2 = -2.0 * np.pi * k2[:, None] * n2[None, :] / n2c
    fr, fi = np.cos(a2), np.sin(a2)
    f2 = np.block([[fr, -fi], [fi, fr]])
    f2i = np.block([[fr.T, fi.T], [-fi.T, fr.T]])
    f32 = lambda a: np.asarray(a, np.float32)
    return f32(f1_real), f32(f1_cplx), f32(g1), f32(f2), f32(f2i)


def _spec_dense_body(f_ref, x_ref, o_ref):
    o_ref[...] = _dot(f_ref[...], x_ref[...], hi=True)


def hyena_spec_dense(full, n):
    f_real, _, _ = _dense_dft_tables(n)
    nn, cols = full.shape
    return pl.pallas_call(
        _spec_dense_body,
        grid=(cols // LANE,),
        in_specs=[pl.BlockSpec((2 * nn, nn), lambda j: (0, 0)),
                  pl.BlockSpec((nn, LANE), lambda j: (0, j))],
        out_specs=pl.BlockSpec((2 * nn, LANE), lambda j: (0, j)),
        out_shape=jax.ShapeDtypeStruct((2 * nn, cols), F32),
        compiler_params=_cparams(("arbitrary",)),
        name="hyena_spec_dense",
    )(jnp.asarray(f_real), full)


def _conv_dense_body(zr_ref, zi_ref, h_ref, ff_ref, fi_ref, bias_ref, or_ref, oi_ref):
    zr, zi = zr_ref[...], zi_ref[...]
    n = zr.shape[0]
    nn = 2 * n
    x = _dot(ff_ref[...], jnp.concatenate([zr, zi], 0), hi=True)
    xr, xi = x[:nn], x[nn:]
    hr, hi_ = h_ref[0:nn, :], h_ref[nn:2 * nn, :]
    y = _dot(fi_ref[...], jnp.concatenate([xr * hr - xi * hi_, xr * hi_ + xi * hr], 0), hi=True)
    bias = bias_ref[...]
    or_ref[...] = y[:n] + zr * bias
    oi_ref[...] = y[n:] + zi * bias


def hyena_conv_dense(zsrc, zcol, row0, n, n_batch, spec, bias3, layer, order):
    _, f_fwd, f_inv = _dense_dft_tables(n)
    nn = 2 * n
    rb, cb = row0 // n, zcol // LANE
    wb = BRANCH_W // LANE
    outs = pl.pallas_call(
        _conv_dense_body,
        grid=(n_batch // 2, wb),
        in_specs=[pl.BlockSpec((n, LANE), lambda p, j: (rb + 2 * p, cb + j)),
                  pl.BlockSpec((n, LANE), lambda p, j: (rb + 2 * p + 1, cb + j)),
                  pl.BlockSpec((2 * nn, LANE), lambda p, j: (0, order * wb + j)),
                  pl.BlockSpec((2 * nn, 2 * n), lambda p, j: (0, 0)),
                  pl.BlockSpec((2 * n, 2 * nn), lambda p, j: (0, 0)),
                  pl.BlockSpec((None, 1, LANE), lambda p, j: (layer * HY_ORDER + order, 0, j))],
        out_specs=[pl.BlockSpec((None, n, LANE), lambda p, j: (p, 0, j)),
                   pl.BlockSpec((None, n, LANE), lambda p, j: (p, 0, j))],
        out_shape=[jax.ShapeDtypeStruct((n_batch // 2, n, BRANCH_W), F32)] * 2,
        compiler_params=_cparams(("arbitrary", "arbitrary")),
        name="hyena_conv_dense",
    )(zsrc, zsrc, spec, jnp.asarray(f_fwd), jnp.asarray(f_inv), bias3)
    return jnp.stack(outs, 1).reshape(n_batch * n, BRANCH_W)


def _spec_fft_body(x_ref, f1_ref, f2_ref, o_ref, a_ref):
    n1c = o_ref.shape[0]

    def stage1(n2, carry):
        xs = x_ref[pl.ds(n2, n1c, stride=FFT_N2), :]
        a_ref[n2] = _dot(f1_ref[n2], xs)
        return carry

    lax.fori_loop(0, FFT_N2, stage1, 0)

    def stage2(k1, carry):
        blk = jnp.concatenate([a_ref[:, k1, :], a_ref[:, n1c + k1, :]], 0)
        o_ref[k1] = _dot(f2_ref[...], blk)
        return carry

    lax.fori_loop(0, n1c, stage2, 0)


def hyena_spec_fft(full, n):
    f1_real, _, _, f2, _ = _two_stage_dft_tables(n)
    nn, cols = full.shape
    n1c = nn // FFT_N2
    const = lambda shape: pl.BlockSpec(shape, lambda j: (0,) * len(shape), pipeline_mode=pl.Buffered(1))
    return pl.pallas_call(
        _spec_fft_body,
        grid=(cols // LANE,),
        in_specs=[pl.BlockSpec((nn, LANE), lambda j: (0, j)),
                  const((FFT_N2, 2 * n1c, n1c)), const((2 * FFT_N2, 2 * FFT_N2))],
        out_specs=pl.BlockSpec((n1c, 2 * FFT_N2, LANE), lambda j: (0, 0, j)),
        out_shape=jax.ShapeDtypeStruct((n1c, 2 * FFT_N2, cols), F32),
        scratch_shapes=[pltpu.VMEM((FFT_N2, 2 * n1c, LANE), F32)],
        compiler_params=_cparams(("arbitrary",)),
        name="hyena_spec_fft",
    )(full, jnp.asarray(f1_real, BF16), jnp.asarray(f2, BF16))


def _conv_fft_body(zr_ref, zi_ref, h_ref, f1_ref, f2_ref, f2i_ref, g1_ref, bias_ref, or_ref, oi_ref, a_ref):
    n1c = h_ref.shape[0]
    n1h = n1c // 2
    n2c = FFT_N2

    def stage1(n2, carry):
        xs = jnp.concatenate([zr_ref[pl.ds(n2, n1h, stride=n2c), :], zi_ref[pl.ds(n2, n1h, stride=n2c), :]], 0)
        a_ref[n2] = _dot(f1_ref[n2], xs)
        return carry

    lax.fori_loop(0, n2c, stage1, 0)

    def stage2(k1, carry):
        x = _dot(f2_ref[...], jnp.concatenate([a_ref[:, k1, :], a_ref[:, n1c + k1, :]], 0))
        xr, xi = x[:n2c], x[n2c:]
        hr, hi_ = h_ref[k1, 0:n2c, :], h_ref[k1, n2c:2 * n2c, :]
        b = _dot(f2i_ref[...], jnp.concatenate([xr * hr - xi * hi_, xr * hi_ + xi * hr], 0))
        a_ref[:, k1, :] = b[:n2c]
        a_ref[:, n1c + k1, :] = b[n2c:]
        return carry

    lax.fori_loop(0, n1c, stage2, 0)
    bias = bias_ref[...]

    def stage3(n2, carry):
        y = _dot(g1_ref[n2], a_ref[n2])
        rows = pl.ds(n2, n1h, stride=n2c)
        or_ref[rows, :] = y[:n1h] + zr_ref[rows, :] * bias
        oi_ref[rows, :] = y[n1h:] + zi_ref[rows, :] * bias
        return carry

    lax.fori_loop(0, n2c, stage3, 0)


def hyena_conv_fft(zsrc, zcol, n, n_batch, spec, bias3, layer, order):
    _, f1_cplx, g1, f2, f2i = _two_stage_dft_tables(n)
    n1c = 2 * n // FFT_N2
    cb = zcol // LANE
    wb = BRANCH_W // LANE
    const = lambda shape: pl.BlockSpec(shape, lambda j, p: (0,) * len(shape), pipeline_mode=pl.Buffered(1))
    outs = pl.pallas_call(
        _conv_fft_body,
        grid=(wb, n_batch // 2),
        in_specs=[pl.BlockSpec((n, LANE), lambda j, p: (2 * p, cb + j)),
                  pl.BlockSpec((n, LANE), lambda j, p: (2 * p + 1, cb + j)),
                  pl.BlockSpec((n1c, 2 * FFT_N2, LANE), lambda j, p: (0, 0, order * wb + j),
                               pipeline_mode=pl.Buffered(1)),
                  const((FFT_N2, 2 * n1c, n1c)), const((2 * FFT_N2, 2 * FFT_N2)), const((2 * FFT_N2, 2 * FFT_N2)),
                  const((FFT_N2, n1c, 2 * n1c)),
                  pl.BlockSpec((None, 1, LANE), lambda j, p: (layer * HY_ORDER + order, 0, j))],
        out_specs=[pl.BlockSpec((None, n, LANE), lambda j, p: (p, 0, j)),
                   pl.BlockSpec((None, n, LANE), lambda j, p: (p, 0, j))],
        out_shape=[jax.ShapeDtypeStruct((n_batch // 2, n, BRANCH_W), F32)] * 2,
        scratch_shapes=[pltpu.VMEM((FFT_N2, 2 * n1c, LANE), F32)],
        compiler_params=_cparams(("arbitrary", "arbitrary")),
        name="hyena_conv_fft",
    )(zsrc, zsrc, spec, jnp.asarray(f1_cplx, BF16), jnp.asarray(f2, BF16), jnp.asarray(f2i, BF16),
      jnp.asarray(g1, BF16), bias3)
    return jnp.stack(outs, 1).reshape(n_batch * n, BRANCH_W)


def _mul_body(a_ref, b_ref, o_ref):
    o_ref[...] = a_ref[...] * b_ref[...]


def mul_cols(a, b, bcol):
    t, w = a.shape
    r = 256
    return pl.pallas_call(
        _mul_body,
        grid=(t // r,),
        in_specs=[pl.BlockSpec((r, w), lambda i: (i, 0)), pl.BlockSpec((r, w), lambda i: (i, bcol // w))],
        out_specs=pl.BlockSpec((r, w), lambda i: (i, 0)),
        out_shape=jax.ShapeDtypeStruct((t, w), F32),
        compiler_params=_cparams(("arbitrary",)),
        name="mul_cols",
    )(a, b)


def _swap_pairs(x):
    w = x.shape[-1]
    lane = lax.broadcasted_iota(jnp.int32, x.shape, x.ndim - 1)
    return jnp.where(lane % 2 == 0, pltpu.roll(x, w - 1, x.ndim - 1), pltpu.roll(x, 1, x.ndim - 1))


def _attn_prep_body(g_ref, dq_ref, dk_ref, dv_ref, cg_ref, sg_ref, cd_ref, sd_ref, qn_ref, kn_ref,
                    qg_ref, kg_ref, vg_ref, qd_ref, kd_ref, vd_ref, *, lat_blocks):
    is_lat = pl.program_id(0) < lat_blocks
    cg = jnp.where(is_lat, cg_ref[...], 1.0)
    sg = jnp.where(is_lat, sg_ref[...], 0.0)
    cd = jnp.where(is_lat, cd_ref[...], 1.0)
    sd = jnp.where(is_lat, sd_ref[...], 0.0)

    def rope(x, cs, sn):
        return x * cs + _swap_pairs(x) * sn

    def rms(x, w):
        return x * lax.rsqrt(jnp.mean(x * x, axis=-1, keepdims=True) + EPS) * w

    for h in range(HEADS):
        sl = slice(h * HEAD_D, (h + 1) * HEAD_D)
        q = rope(rms(g_ref[:, sl], qn_ref[...]), cg, sg)
        qg_ref[:, sl] = (q * HEAD_D ** -0.5).astype(BF16)
        qd_ref[:, sl] = (rope(dq_ref[:, sl], cd, sd) * DIFF_QK ** -0.5).astype(BF16)
        kd_ref[:, sl] = rope(dk_ref[:, sl], cd, sd).astype(BF16)
    for h in range(GQA_KV):
        sl = slice(h * HEAD_D, (h + 1) * HEAD_D)
        kin = g_ref[:, BRANCH_W + h * HEAD_D:BRANCH_W + (h + 1) * HEAD_D]
        kg_ref[:, sl] = rope(rms(kin, kn_ref[...]), cg, sg).astype(BF16)
    vg_ref[...] = g_ref[:, BRANCH_W + GQA_KV * HEAD_D:BRANCH_W + 2 * GQA_KV * HEAD_D].astype(BF16)
    vd_ref[...] = dv_ref[...].astype(BF16)


def attn_prep(p, ropes, qn3, kn3, layer, n_lat, n_ctx, n_batch):
    t = p.shape[0]
    r = 256 if n_ctx % 256 == 0 else n_ctx
    nlb, ncb = n_lat // r, n_ctx // r
    lat_blocks = n_batch * nlb
    kvw = GQA_KV * HEAD_D
    w = BRANCH_W

    def kv_row(i):
        lat = (i // nlb) * (nlb + ncb) + ncb + i % nlb
        j = i - lat_blocks
        ctx = (j // ncb) * (nlb + ncb) + j % ncb
        return jnp.where(i < lat_blocks, lat, ctx)

    rope_spec = pl.BlockSpec((r, LANE), lambda i: (jnp.where(i < lat_blocks, i % nlb, 0), 0))
    nkv = n_batch * (n_lat + n_ctx)
    return pl.pallas_call(
        functools.partial(_attn_prep_body, lat_blocks=lat_blocks),
        grid=(t // r,),
        in_specs=[pl.BlockSpec((r, 2 * w), lambda i: (i, C_GQA_QKV // (2 * w))),
                  pl.BlockSpec((r, w), lambda i: (i, C_DIFF_Q // w)),
                  pl.BlockSpec((r, w), lambda i: (i, C_DIFF_K // w)),
                  pl.BlockSpec((r, w), lambda i: (i, C_DIFF_V // w)),
                  rope_spec, rope_spec, rope_spec, rope_spec,
                  pl.BlockSpec((None, 1, LANE), lambda i: (layer, 0, 0)),
                  pl.BlockSpec((None, 1, LANE), lambda i: (layer, 0, 0))],
        out_specs=[pl.BlockSpec((r, w), lambda i: (i, 0)),
                   pl.BlockSpec((r, kvw), lambda i: (kv_row(i), 0)),
                   pl.BlockSpec((r, kvw), lambda i: (kv_row(i), 0)),
                   pl.BlockSpec((r, w), lambda i: (i, 0)),
                   pl.BlockSpec((r, w), lambda i: (kv_row(i), 0)),
                   pl.BlockSpec((r, w), lambda i: (kv_row(i), 0))],
        out_shape=[jax.ShapeDtypeStruct((t, w), BF16), jax.ShapeDtypeStruct((nkv, kvw), BF16),
                   jax.ShapeDtypeStruct((nkv, kvw), BF16), jax.ShapeDtypeStruct((t, w), BF16),
                   jax.ShapeDtypeStruct((nkv, w), BF16), jax.ShapeDtypeStruct((nkv, w), BF16)],
        compiler_params=_cparams(("arbitrary",)),
        name="attn_prep",
    )(p, p, p, p, *ropes, qn3, kn3)


def _softmax_parts(s):
    e = jnp.exp(s - jnp.max(s, axis=-1, keepdims=True))
    return e, jnp.sum(e, axis=-1, keepdims=True)


def _gqa_body(q_ref, k_ref, v_ref, o_ref):
    group = HEADS // GQA_KV
    for kvh in range(GQA_KV):
        k = k_ref[:, kvh * HEAD_D:(kvh + 1) * HEAD_D]
        v = v_ref[:, kvh * HEAD_D:(kvh + 1) * HEAD_D]
        for g in range(group):
            sl = slice((kvh * group + g) * HEAD_D, (kvh * group + g + 1) * HEAD_D)
            s = lax.dot_general(q_ref[:, sl], k, (((1,), (1,)), ((), ())), preferred_element_type=F32)
            e, l = _softmax_parts(s)
            o_ref[:, sl] = jnp.dot(e.astype(BF16), v, preferred_element_type=F32) / l


def _diff_body(q_ref, k_ref, v_ref, lam_ref, o_ref, *, lam_init):
    lam4 = lam_ref[...]
    lam = (jnp.exp(jnp.sum(lam4[0:1] * lam4[1:2], axis=-1, keepdims=True))
           - jnp.exp(jnp.sum(lam4[2:3] * lam4[3:4], axis=-1, keepdims=True)) + lam_init)
    dn = (((1,), (1,)), ((), ()))
    for h in range(HEADS):
        sl = slice(h * HEAD_D, (h + 1) * HEAD_D)
        q = q_ref[:, sl]
        k = k_ref[:, sl]
        v = v_ref[:, sl]
        first = lax.broadcasted_iota(jnp.int32, q.shape, 1) < DIFF_QK
        zero = jnp.zeros_like(q)
        e1, l1 = _softmax_parts(lax.dot_general(jnp.where(first, q, zero), k, dn, preferred_element_type=F32))
        e2, l2 = _softmax_parts(lax.dot_general(jnp.where(first, zero, q), k, dn, preferred_element_type=F32))
        o1 = jnp.dot(e1.astype(BF16), v, preferred_element_type=F32) / l1
        o2 = jnp.dot(e2.astype(BF16), v, preferred_element_type=F32) / l2
        o_ref[:, sl] = o1 - lam * o2


def attention(body, q, k, v, extra, extra_specs, q_row0, nq, kv_per_batch, kv_len, n_batch, tq, name):
    w = q.shape[1]
    qb0 = q_row0 // tq
    nqb = nq // tq
    kvb = kv_per_batch // kv_len
    return pl.pallas_call(
        body,
        grid=(n_batch, nqb),
        in_specs=[pl.BlockSpec((tq, w), lambda b, i: (qb0 + b * nqb + i, 0)),
                  pl.BlockSpec((kv_len, k.shape[1]), lambda b, i: (b * kvb, 0)),
                  pl.BlockSpec((kv_len, v.shape[1]), lambda b, i: (b * kvb, 0))] + extra_specs,
        out_specs=pl.BlockSpec((tq, w), lambda b, i: (b * nqb + i, 0)),
        out_shape=jax.ShapeDtypeStruct((n_batch * nq, w), F32),
        compiler_params=_cparams(("arbitrary", "arbitrary")),
        name=name,
    )(q, k, v, *extra)


def _merge_body(h_ref, mod_ref, mg_ref, of_ref, ob_ref, ggate_ref, y1_ref, x2_ref, hgate_ref, oc_ref, cgate_ref,
                od_ref, dgate_ref, gnorm_ref, dnorm_ref, wbr_ref, wout_ref, lng_ref, lnb_ref, o_ref, *, diff_scale):
    def rms_heads(x, w):
        parts = []
        for h in range(HEADS):
            xh = x[:, h * HEAD_D:(h + 1) * HEAD_D]
            parts.append(xh * lax.rsqrt(jnp.mean(xh * xh, axis=-1, keepdims=True) + EPS) * w)
        return jnp.concatenate(parts, -1)

    ys = (rms_heads(of_ref[...] + ob_ref[...], gnorm_ref[...]) * _silu(ggate_ref[...]),
          x2_ref[...] * y1_ref[...] * _silu(hgate_ref[...]),
          oc_ref[...] * _silu(cgate_ref[...]),
          rms_heads(od_ref[...], dnorm_ref[...]) * diff_scale * _silu(dgate_ref[...]))
    acc = None
    for n in range(N_BRANCH):
        proj = jnp.dot(ys[n].astype(BF16), wbr_ref[n], preferred_element_type=F32)
        term = _sigmoid(mg_ref[:, n * D_MODEL:(n + 1) * D_MODEL]) * proj
        acc = term if acc is None else acc + term
    out = jnp.dot(acc.astype(BF16), wout_ref[...], preferred_element_type=F32)
    x = ALPHA * h_ref[...] + mod_ref[2:3, :] * out
    mu = jnp.mean(x, axis=-1, keepdims=True)
    xc = x - mu
    var = jnp.mean(xc * xc, axis=-1, keepdims=True)
    o_ref[...] = xc * lax.rsqrt(var + EPS) * lng_ref[...] + lnb_ref[...]


def merge_postnorm(h_all, mod3, p, o_f, o_b, y1, xv, oc, od, gnorm3, dnorm3, wbr, wout, lng3, lnb3, layer, lam_init,
                   n_lat, n_batch):
    t, d = h_all.shape
    r = 256 if n_lat % 256 == 0 else 64
    w = BRANCH_W
    lbb = n_lat // r
    row = lambda i: jnp.minimum(i // lbb, n_batch)
    tok = lambda cb: pl.BlockSpec((r, w), lambda i: (i, cb))
    vec = lambda width: pl.BlockSpec((None, 1, width), lambda i: (layer, 0, 0))
    return pl.pallas_call(
        functools.partial(_merge_body, diff_scale=1.0 - lam_init),
        grid=(t // r,),
        in_specs=[pl.BlockSpec((r, d), lambda i: (i, 0)),
                  pl.BlockSpec((None, 3, d), lambda i: (row(i), 0, 0)),
                  pl.BlockSpec((r, N_BRANCH * d), lambda i: (i, C_MERGE // (N_BRANCH * d))),
                  tok(0), tok(0), tok(C_GDN_GATE // w), tok(0), tok(1), tok(C_HY_GATE // w), tok(0),
                  tok(C_GQA_GATE // w), tok(0), tok(C_DIFF_GATE // w),
                  vec(LANE), vec(LANE),
                  pl.BlockSpec((None, N_BRANCH, w, d), lambda i: (layer, 0, 0, 0)),
                  pl.BlockSpec((None, d, d), lambda i: (layer, 0, 0)),
                  vec(d), vec(d)],
        out_specs=pl.BlockSpec((r, d), lambda i: (i, 0)),
        out_shape=jax.ShapeDtypeStruct((t, d), F32),
        compiler_params=_cparams(("arbitrary",)),
        name="merge_postnorm",
    )(h_all, mod3, p, o_f, o_b, p, y1, xv, p, oc, p, od, p, gnorm3, dnorm3, wbr, wout, lng3, lnb3)


def _rope_tables(n_lat, dim):
    rows = n_lat // GRID_W
    row = jnp.repeat(jnp.arange(rows, dtype=F32), GRID_W)
    col = jnp.tile(jnp.arange(GRID_W, dtype=F32), rows)
    half = dim // 2
    inv = ROPE_THETA ** (-jnp.arange(0, half, 2, dtype=F32) / half)
    ang = jnp.concatenate([row[:, None] * inv, col[:, None] * inv], -1)
    cos = jnp.repeat(jnp.cos(ang), 2, axis=-1)
    sin = jnp.repeat(jnp.sin(ang), 2, axis=-1)
    sign = jnp.tile(jnp.array([-1.0, 1.0], F32), dim // 2)
    reps = LANE // dim
    return jnp.tile(cos, (1, reps)), jnp.tile(sin * sign, (1, reps))


def kernel(x, c, ctx, c_ctx, w_ada, b_ada, w_in, gdn_conv, gdn_a_log, gdn_dt_bias, gdn_norm, hy_conv, hy_w1, hy_b1,
           hy_w2, hy_b2, hy_w3, hy_b3, hy_w4, hy_freq, hy_bias, gqa_qn, gqa_kn, diff_lam, diff_norm, w_br, w_out,
           ln_g, ln_b):
    nb, n_lat, d = x.shape
    n_ctx = ctx.shape[1]
    t_lat, t_ctx = nb * n_lat, nb * n_ctx
    depth = w_in.shape[0]
    w = BRANCH_W

    w_main = jnp.concatenate([w_in[:, :, O_MERGE:], w_in[:, :, :O_GDN_AB], w_in[:, :, O_GDN_AB + 4 * HEADS:O_MERGE]],
                             axis=2).astype(BF16)
    w_ab = jnp.pad(w_in[:, :, O_GDN_AB:O_GDN_AB + 4 * HEADS], ((0, 0), (0, 0), (0, LANE - 4 * HEADS)))
    wbr_bf = w_br.astype(BF16)
    wout_bf = w_out.astype(BF16)
    b_ada3 = b_ada[:, None, :]
    cvec = jnp.concatenate([c, c_ctx[None, :], jnp.zeros((SUB - nb - 1, d), F32)], 0)
    as3 = lambda a: a[:, None, :]
    gdn_par_r = jnp.pad(jnp.stack([gdn_a_log.reshape(depth, -1), gdn_dt_bias.reshape(depth, -1)], 1),
                        ((0, 0), (0, SUB - 2), (0, LANE - 2 * HEADS)))
    gdn_par_c = jnp.pad(jnp.stack([gdn_a_log.reshape(depth, -1), gdn_dt_bias.reshape(depth, -1)], 2),
                        ((0, 0), (0, 2 * HEADS), (0, LANE - 2)))
    hy_w1p = jnp.pad(hy_w1, ((0, 0), (0, LANE - HY_EMB), (0, 0)))
    hy_bias3 = hy_bias.reshape(depth * HY_ORDER, 1, w)
    ropes = _rope_tables(n_lat, HEAD_D) + _rope_tables(n_lat, DIFF_QK)

    tm = 1024 if (n_lat % 1024 == 0 and t_ctx % 1024 == 0) else n_ctx
    h_all = jnp.concatenate([x.reshape(t_lat, d), ctx.reshape(t_ctx, d)], 0)
    for l in range(depth):
        lam_init = 0.8 - 0.6 * math.exp(-0.3 * l)
        mod3 = ada_mod(cvec, w_ada, b_ada3, l).reshape(SUB, 3, d)
        p, ab = in_proj(h_all, mod3, w_main, w_ab, l, tm, n_lat // tm, nb)

        qkv = dwconv(p, gdn_conv, l, C_GDN_QKV, 3 * w, n_lat, n_ctx, nb, act=True)
        ab_rows = jnp.transpose(ab[:, :4 * HEADS].reshape(-1, GDN_CHUNK, 4 * HEADS), (0, 2, 1))
        o_f, o_b = gdn_scan(qkv, ab, ab_rows, gdn_par_r[l], gdn_par_c[l], n_lat, n_ctx, nb)

        xv = dwconv(p, hy_conv, l, C_HY_XV, 3 * w, n_lat, n_ctx, nb, act=False)
        filt = lambda n: _two_sided(hyena_filter(n, hy_w1p, as3(hy_b1), hy_w2, as3(hy_b2), hy_w3, as3(hy_b3), hy_w4,
                                                 as3(hy_freq), l), n)
        spec_lat = hyena_spec_fft(filt(n_lat), n_lat)
        spec_ctx = hyena_spec_dense(filt(n_ctx), n_ctx)
        y0 = jnp.concatenate([hyena_conv_fft(xv, 2 * w, n_lat, nb, spec_lat, hy_bias3, l, 0),
                              hyena_conv_dense(xv, 2 * w, t_lat, n_ctx, nb, spec_ctx, hy_bias3, l, 0)], 0)
        z1 = mul_cols(y0, xv, 0)
        y1 = jnp.concatenate([hyena_conv_fft(z1, 0, n_lat, nb, spec_lat, hy_bias3, l, 1),
                              hyena_conv_dense(z1, 0, t_lat, n_ctx, nb, spec_ctx, hy_bias3, l, 1)], 0)

        qg, kg, vg, qd, kd, vd = attn_prep(p, ropes, as3(gqa_qn), as3(gqa_kn), l, n_lat, n_ctx, nb)
        kv_all = n_lat + n_ctx
        tq = min(256, n_ctx)
        lam_spec = [pl.BlockSpec((None, 4, DIFF_QK), lambda b, i: (l, 0, 0))]
        diff_body = functools.partial(_diff_body, lam_init=lam_init)
        oc = jnp.concatenate([
            attention(_gqa_body, qg, kg, vg, (), [], 0, n_lat, kv_all, kv_all, nb, tq, "gqa_lat"),
            attention(_gqa_body, qg, kg, vg, (), [], t_lat, n_ctx, kv_all, n_ctx, nb, tq, "gqa_ctx")], 0)
        od = jnp.concatenate([
            attention(diff_body, qd, kd, vd, (diff_lam,), lam_spec, 0, n_lat, kv_all, kv_all, nb, tq, "diff_lat"),
            attention(diff_body, qd, kd, vd, (diff_lam,), lam_spec, t_lat, n_ctx, kv_all, n_ctx, nb, tq, "diff_ctx")], 0)

        h_all = merge_postnorm(h_all, mod3, p, o_f, o_b, y1, xv, oc, od, as3(gdn_norm), as3(diff_norm), wbr_bf, wout_bf,
                               as3(ln_g), as3(ln_b), l, lam_init, n_lat, nb)
    return h_all[:t_lat].reshape(nb, n_lat, d)
```

```python
import functools
import math

import numpy as np
import jax
import jax.numpy as jnp
from jax import lax
from jax.experimental import pallas as pl
from jax.experimental.pallas import tpu as pltpu

F32 = jnp.float32
BF16 = jnp.bfloat16
HI = lax.Precision.HIGHEST

D_MODEL = 1024
DEPTH = 4
GRID_W = 64
BRANCH_W = D_MODEL // 2
N_BRANCH = 4
HEADS = 4
HEAD_D = BRANCH_W // HEADS
GDN_CONV = 4
GDN_CHUNK = 64
HY_CONV = 3
HY_EMB = 33
HY_BANDS = (HY_EMB - 1) // 2
HY_FH = 64
HY_ORDER = 2
HY_MIN_DECAY = math.log(1e-2) / 1.5
HY_MAX_DECAY = math.log(1e-2) / 0.3
GQA_KV = 2
DIFF_QK = HEAD_D // 2
ROPE_THETA = 10000.0
EPS = 1e-6
ALPHA = (2.0 * DEPTH) ** 0.25

LANE = 128
SUB = 8
FFT_N2 = 128
FFT_UNROLL = 8
VMEM_LIMIT = 60 * 1024 * 1024

C_MERGE = 0
C_GDN_QKV = 4096
C_GDN_GATE = 5632
C_HY_XV = 6144
C_HY_GATE = 7680
C_GQA_QKV = 8192
C_GQA_GATE = 9216
C_DIFF_Q = 9728
C_DIFF_K = 10240
C_DIFF_V = 10752
C_DIFF_GATE = 11264
N_MAIN = 11776
O_GDN_AB = 1536
O_MERGE = 7696


def _cparams(sem):
    return pltpu.CompilerParams(dimension_semantics=sem, vmem_limit_bytes=VMEM_LIMIT)


def _dot(a, b, hi=False):
    if hi:
        return jnp.dot(a, b, precision=HI, preferred_element_type=F32)
    return jnp.dot(a.astype(BF16), b.astype(BF16), preferred_element_type=F32)


def _dot_nt(a, b, hi=False):
    dn = (((1,), (1,)), ((), ()))
    if hi:
        return lax.dot_general(a, b, dn, precision=HI, preferred_element_type=F32)
    return lax.dot_general(a.astype(BF16), b.astype(BF16), dn, preferred_element_type=F32)


def _dot_tn(a, b):
    return lax.dot_general(a.astype(BF16), b.astype(BF16), (((0,), (0,)), ((), ())), preferred_element_type=F32)


def _sigmoid(x):
    return 1.0 / (1.0 + jnp.exp(-x))


def _silu(x):
    return x * _sigmoid(x)


def _softplus(x):
    return jnp.maximum(x, 0.0) + jnp.log1p(jnp.exp(-jnp.abs(x)))


def _ada_body(c_ref, w_ref, b_ref, o_ref):
    o_ref[...] = _dot(_silu(c_ref[...]), w_ref[...], hi=True) + b_ref[...]


def ada_mod(cvec, w_ada, b_ada3, layer):
    d = cvec.shape[1]
    tn = 512
    return pl.pallas_call(
        _ada_body,
        grid=(3 * d // tn,),
        in_specs=[pl.BlockSpec((SUB, d), lambda j: (0, 0)),
                  pl.BlockSpec((None, d, tn), lambda j: (layer, 0, j)),
                  pl.BlockSpec((None, 1, tn), lambda j: (layer, 0, j))],
        out_specs=pl.BlockSpec((SUB, tn), lambda j: (0, j)),
        out_shape=jax.ShapeDtypeStruct((SUB, 3 * d), F32),
        compiler_params=_cparams(("arbitrary",)),
        name="ada_mod",
    )(cvec, w_ada, b_ada3)


def _inproj_body(h_ref, mod_ref, w_ref, wab_ref, o_ref, ab_ref, u_ref):
    @pl.when(pl.program_id(1) == 0)
    def _():
        x = h_ref[...]
        mu = jnp.mean(x, axis=-1, keepdims=True)
        xc = x - mu
        var = jnp.mean(xc * xc, axis=-1, keepdims=True)
        u = xc * lax.rsqrt(var + EPS) * (1.0 + mod_ref[1:2, :]) + mod_ref[0:1, :]
        u_ref[...] = u.astype(BF16)
        ab_ref[...] = _dot(u, wab_ref[...], hi=True)

    o_ref[...] = jnp.dot(u_ref[...], w_ref[...], preferred_element_type=F32)


def in_proj(h_all, mod3, w_main, w_ab, layer, tm, lat_blocks_per_batch, n_batch):
    t, d = h_all.shape
    tn = 512
    n_main = w_main.shape[2]
    row = lambda i: jnp.minimum(i // lat_blocks_per_batch, n_batch)
    return pl.pallas_call(
        _inproj_body,
        grid=(t // tm, n_main // tn),
        in_specs=[pl.BlockSpec((tm, d), lambda i, j: (i, 0)),
                  pl.BlockSpec((None, 3, d), lambda i, j: (row(i), 0, 0)),
                  pl.BlockSpec((None, d, tn), lambda i, j: (layer, 0, j)),
                  pl.BlockSpec((None, d, LANE), lambda i, j: (layer, 0, 0))],
        out_specs=[pl.BlockSpec((tm, tn), lambda i, j: (i, j)),
                   pl.BlockSpec((tm, LANE), lambda i, j: (i, 0))],
        out_shape=[jax.ShapeDtypeStruct((t, n_main), F32), jax.ShapeDtypeStruct((t, LANE), F32)],
        scratch_shapes=[pltpu.VMEM((tm, d), BF16)],
        compiler_params=_cparams(("arbitrary", "arbitrary")),
        name="in_proj",
    )(h_all, mod3, w_main, w_ab)


def _dwconv_body(xp_ref, x_ref, xn_ref, w_ref, o_ref, pad_ref, *, taps, pad_l, lat_blocks, bl, bc, act):
    i = pl.program_id(0)
    r = x_ref.shape[0]
    is_lat = i < lat_blocks
    pos = jnp.where(is_lat, i % bl, (i - lat_blocks) % bc)
    last = jnp.where(is_lat, bl - 1, bc - 1)
    pad_ref[0:SUB, :] = jnp.where(pos == 0, 0.0, xp_ref[...])
    pad_ref[SUB:SUB + r, :] = x_ref[...]
    pad_ref[SUB + r:2 * SUB + r, :] = jnp.where(pos == last, 0.0, xn_ref[...])
    acc = None
    for j in range(taps):
        off = SUB + j - pad_l
        term = w_ref[j:j + 1, :] * pad_ref[off:off + r, :]
        acc = term if acc is None else acc + term
    if act:
        acc = _silu(acc)
    o_ref[...] = acc


def dwconv(p, w_conv, layer, col0, width, n_lat, n_ctx, n_batch, act):
    t = p.shape[0]
    taps = w_conv.shape[1]
    r = 256 if n_ctx % 256 == 0 else n_ctx
    lw = 512
    cb = col0 // lw
    rs = r // SUB
    body = functools.partial(_dwconv_body, taps=taps, pad_l=(taps - 1) // 2, lat_blocks=n_batch * n_lat // r,
                             bl=n_lat // r, bc=n_ctx // r, act=act)
    return pl.pallas_call(
        body,
        grid=(t // r, width // lw),
        in_specs=[pl.BlockSpec((SUB, lw), lambda i, j: (jnp.maximum(i * rs - 1, 0), cb + j)),
                  pl.BlockSpec((r, lw), lambda i, j: (i, cb + j)),
                  pl.BlockSpec((SUB, lw), lambda i, j: (jnp.minimum((i + 1) * rs, t // SUB - 1), cb + j)),
                  pl.BlockSpec((None, taps, lw), lambda i, j: (layer, 0, j))],
        out_specs=pl.BlockSpec((r, lw), lambda i, j: (i, j)),
        out_shape=jax.ShapeDtypeStruct((t, width), F32),
        scratch_shapes=[pltpu.VMEM((r + 2 * SUB, lw), F32)],
        compiler_params=_cparams(("arbitrary", "arbitrary")),
        name="dwconv",
    )(p, p, p, w_conv)


def _gdn_body(qf_ref, qb_ref, abcf_ref, abcb_ref, abrf_ref, abrb_ref, pr_ref, pc_ref, of_ref, ob_ref, s_ref):
    c = GDN_CHUNK

    @pl.when(pl.program_id(1) == 0)
    def _():
        s_ref[...] = jnp.zeros_like(s_ref)

    ii = lax.broadcasted_iota(jnp.int32, (c, c), 0)
    jj = lax.broadcasted_iota(jnp.int32, (c, c), 1)
    lmat = (jj <= ii).astype(F32)
    eye = (jj == ii).astype(F32)
    alr, dtr = pr_ref[0:1, :], pr_ref[1:2, :]
    alc, dtc = pc_ref[:, 0:1], pc_ref[:, 1:2]
    chains = []
    for d in range(2):
        qkv_ref = (qf_ref, qb_ref)[d]
        abc = (abcf_ref, abcb_ref)[d][...]
        abr = (abrf_ref, abrb_ref)[d][...]
        g_c = -jnp.exp(alr) * _softplus(abc + dtr)
        g_r = -jnp.exp(alc) * _softplus(abr + dtc)
        cum_c = _dot(lmat, g_c, hi=True)
        cum_r = _dot_nt(g_r, lmat, hi=True)
        if d == 1:
            cum_c = cum_c[c - 1:c, :] - cum_c + g_c
            cum_r = cum_r[:, c - 1:c] - cum_r + g_r
        beta_all = _sigmoid(abc)
        incl = (jj <= ii) if d == 0 else (jj >= ii)
        strict = (jj < ii) if d == 0 else (jj > ii)
        for h in range(HEADS):
            idx = HEADS * d + h
            q = qkv_ref[:, h * HEAD_D:(h + 1) * HEAD_D]
            k = qkv_ref[:, BRANCH_W + h * HEAD_D:BRANCH_W + (h + 1) * HEAD_D]
            v = qkv_ref[:, 2 * BRANCH_W + h * HEAD_D:2 * BRANCH_W + (h + 1) * HEAD_D]
            q = q * lax.rsqrt(jnp.sum(q * q, axis=-1, keepdims=True) + EPS) * (HEAD_D ** -0.5)
            k = k * lax.rsqrt(jnp.sum(k * k, axis=-1, keepdims=True) + EPS)
            cc = cum_c[:, idx:idx + 1]
            cr = cum_r[idx:idx + 1, :]
            dec = jnp.exp(jnp.where(incl, cc - cr, -1e30))
            beta = beta_all[:, 2 * HEADS + idx:2 * HEADS + idx + 1]
            ecum = jnp.exp(cc)
            tot = cc[c - 1:c, :] if d == 0 else cc[0:1, :]
            chains.append(dict(d=d, h=h, q=q, k=k, dec=dec, strict=strict, beta=beta, ecum=ecum, tot=tot,
                               rhs=jnp.concatenate([k * (beta * ecum), v * beta], 1),
                               k_tail=k * jnp.exp(tot - cc)))
    for ch in chains:
        ch["kk"] = _dot_nt(ch["k"], ch["k"])
        ch["qk"] = _dot_nt(ch["q"], ch["k"])
    for ch in chains:
        ch["p"] = -jnp.where(ch["strict"], ch["beta"] * ch["kk"] * ch["dec"], 0.0)
        ch["inv"] = eye + ch["p"]
    for _ in range(int(math.log2(c)) - 1):
        for ch in chains:
            ch["p"] = _dot(ch["p"], ch["p"])
        for ch in chains:
            ch["inv"] = ch["inv"] + _dot(ch["inv"], ch["p"])
    for ch in chains:
        ch["wu"] = _dot(ch["inv"], ch["rhs"])
    for ch in chains:
        ch["s"] = s_ref[ch["d"], ch["h"]]
        ch["ws"] = _dot(jnp.concatenate([ch["wu"][:, :HEAD_D], ch["q"] * ch["ecum"]], 0), ch["s"])
    for ch in chains:
        ch["v_new"] = ch["wu"][:, HEAD_D:] - ch["ws"][:c]
    for ch in chains:
        out_ref = (of_ref, ob_ref)[ch["d"]]
        h = ch["h"]
        out_ref[:, h * HEAD_D:(h + 1) * HEAD_D] = ch["ws"][c:] + _dot(ch["qk"] * ch["dec"], ch["v_new"])
        s_ref[ch["d"], h] = ch["s"] * jnp.exp(ch["tot"]) + _dot_tn(ch["k_tail"], ch["v_new"])


def gdn_scan(qkv, ab, ab_rows, par_r, par_c, n_lat, n_ctx, n_batch):
    t = qkv.shape[0]
    c = GDN_CHUNK
    nlc, ncc = n_lat // c, n_ctx // c
    base = n_batch * nlc

    def fwd(b, s):
        return jnp.where(s < ncc, base + b * ncc + s, b * nlc + (s - ncc))

    def bwd(b, s):
        return jnp.where(s < ncc, base + b * ncc + (ncc - 1 - s), b * nlc + (nlc - 1 - (s - ncc)))

    w3 = 3 * BRANCH_W
    return pl.pallas_call(
        _gdn_body,
        grid=(n_batch, ncc + nlc),
        in_specs=[pl.BlockSpec((c, w3), lambda b, s: (fwd(b, s), 0)),
                  pl.BlockSpec((c, w3), lambda b, s: (bwd(b, s), 0)),
                  pl.BlockSpec((c, LANE), lambda b, s: (fwd(b, s), 0)),
                  pl.BlockSpec((c, LANE), lambda b, s: (bwd(b, s), 0)),
                  pl.BlockSpec((None, 4 * HEADS, c), lambda b, s: (fwd(b, s), 0, 0)),
                  pl.BlockSpec((None, 4 * HEADS, c), lambda b, s: (bwd(b, s), 0, 0)),
                  pl.BlockSpec((SUB, LANE), lambda b, s: (0, 0)),
                  pl.BlockSpec((4 * HEADS, LANE), lambda b, s: (0, 0))],
        out_specs=[pl.BlockSpec((c, BRANCH_W), lambda b, s: (fwd(b, s), 0)),
                   pl.BlockSpec((c, BRANCH_W), lambda b, s: (bwd(b, s), 0))],
        out_shape=[jax.ShapeDtypeStruct((t, BRANCH_W), F32), jax.ShapeDtypeStruct((t, BRANCH_W), F32)],
        scratch_shapes=[pltpu.VMEM((2, HEADS, HEAD_D, HEAD_D), F32)],
        compiler_params=_cparams(("arbitrary", "arbitrary")),
        name="gdn_scan",
    )(qkv, qkv, ab, ab, ab_rows, ab_rows, par_r, par_c)


def _hyfilt_body(z_ref, t_ref, w1_ref, b1_ref, w2_ref, b2_ref, w3_ref, b3_ref, w4_ref, fr_ref, dl_ref, o_ref):
    fr = fr_ref[...]
    h = jnp.sin(fr * (_dot(z_ref[...], w1_ref[...], hi=True) + b1_ref[...]))
    h = jnp.sin(fr * (_dot(h, w2_ref[...], hi=True) + b2_ref[...]))
    h = jnp.sin(fr * (_dot(h, w3_ref[...], hi=True) + b3_ref[...]))
    o_ref[...] = _dot(h, w4_ref[...], hi=True) * jnp.exp(-t_ref[...] * dl_ref[...])


def hyena_filter(n, w1p, b1, w2, b2, w3, b3, w4, fr, layer):
    pos = jnp.arange(n, dtype=F32)
    tt = pos / max(n - 1, 1)
    ang = (2.0 * math.pi / n) * pos[:, None] * jnp.linspace(1e-4, HY_BANDS - 1, HY_BANDS, dtype=F32)
    z = jnp.concatenate([tt[:, None], jnp.cos(ang), -jnp.sin(ang), jnp.zeros((n, LANE - HY_EMB), F32)], -1)
    deltas = jnp.abs(jnp.linspace(HY_MIN_DECAY, HY_MAX_DECAY, BRANCH_W, dtype=F32))
    dl = jnp.tile(deltas, 2 * HY_ORDER)[None, :]
    r = min(n, 512)
    wo = 2 * HY_ORDER * BRANCH_W
    full = lambda shape: pl.BlockSpec((None,) + shape, lambda i: (layer,) + (0,) * len(shape))
    return pl.pallas_call(
        _hyfilt_body,
        grid=(n // r,),
        in_specs=[pl.BlockSpec((r, LANE), lambda i: (i, 0)),
                  pl.BlockSpec((r, 1), lambda i: (i, 0)),
                  full((LANE, HY_FH)), full((1, HY_FH)), full((HY_FH, HY_FH)), full((1, HY_FH)),
                  full((HY_FH, HY_FH)), full((1, HY_FH)), full((HY_FH, wo)), full((1, HY_FH)),
                  pl.BlockSpec((1, wo), lambda i: (0, 0))],
        out_specs=pl.BlockSpec((r, wo), lambda i: (i, 0)),
        out_shape=jax.ShapeDtypeStruct((n, wo), F32),
        compiler_params=_cparams(("arbitrary",)),
        name="hyena_filter",
    )(z, tt[:, None], w1p, b1, w2, b2, w3, b3, w4, fr, dl)


def _two_sided(hf, n):
    w = BRANCH_W
    cols = []
    for o in range(HY_ORDER):
        h0 = hf[:, o * 2 * w:o * 2 * w + w]
        h1 = hf[:, o * 2 * w + w:(o + 1) * 2 * w]
        cols.append(jnp.concatenate([h0, jnp.zeros((1, w), F32), h1[:0:-1]], 0))
    return jnp.concatenate(cols, 1)


@functools.lru_cache(maxsize=None)
def _dense_dft_tables(n):
    nn = 2 * n
    k = np.arange(nn)[:, None].astype(np.float64)
    m = np.arange(nn)[None, :].astype(np.float64)
    ang = -2.0 * np.pi * k * m / nn
    wr, wi = np.cos(ang), np.sin(ang)
    f_real = np.concatenate([wr, wi], 0)
    wr_h, wi_h = wr[:, :n], wi[:, :n]
    f_fwd = np.block([[wr_h, -wi_h], [wi_h, wr_h]])
    cr, ci = wr.T[:n] / nn, -wi.T[:n] / nn
    f_inv = np.block([[cr, -ci], [ci, cr]])
    return (np.asarray(f_real, np.float32), np.asarray(f_fwd, np.float32), np.asarray(f_inv, np.float32))


@functools.lru_cache(maxsize=None)
def _two_stage_dft_tables(n):
    nn = 2 * n
    n2c = FFT_N2
    n1c = nn // n2c
    n1h = n1c // 2
    k1 = np.arange(n1c).astype(np.float64)
    n1 = np.arange(n1c).astype(np.float64)
    n2 = np.arange(n2c).astype(np.float64)
    ang = -2.0 * np.pi * (k1[None, :, None] * n1[None, None, :] / n1c + n2[:, None, None] * k1[None, :, None] / nn)
    mr, mi = np.cos(ang), np.sin(ang)
    f1_real = np.concatenate([mr, mi], 1)
    mrh, mih = mr[:, :, :n1h], mi[:, :, :n1h]
    f1_cplx = np.concatenate([np.concatenate([mrh, -mih], 2), np.concatenate([mih, mrh], 2)], 1)
    gr = np.transpose(mr, (0, 2, 1))[:, :n1h, :] / nn
    gi = -np.transpose(mi, (0, 2, 1))[:, :n1h, :] / nn
    g1 = np.concatenate([np.concatenate([gr, -gi], 2), np.concatenate([gi, gr], 2)], 1)
    k2 = np.arange(n2c).astype(np.float64)
    a2 = -2.0 * np.pi * k2[:, None] * n2[None, :] / n2c
    fr, fi = np.cos(a2), np.sin(a2)
    f2 = np.block([[fr, -fi], [fi, fr]])
    f2i = np.block([[fr.T, fi.T], [-fi.T, fr.T]])
    f32 = lambda a: np.asarray(a, np.float32)
    return f32(f1_real), f32(f1_cplx), f32(g1), f32(f2), f32(f2i)


def _spec_dense_body(f_ref, x_ref, o_ref):
    o_ref[...] = _dot(f_ref[...], x_ref[...], hi=True)


def hyena_spec_dense(full, n):
    f_real, _, _ = _dense_dft_tables(n)
    nn, cols = full.shape
    return pl.pallas_call(
        _spec_dense_body,
        grid=(cols // LANE,),
        in_specs=[pl.BlockSpec((2 * nn, nn), lambda j: (0, 0)),
                  pl.BlockSpec((nn, LANE), lambda j: (0, j))],
        out_specs=pl.BlockSpec((2 * nn, LANE), lambda j: (0, j)),
        out_shape=jax.ShapeDtypeStruct((2 * nn, cols), F32),
        compiler_params=_cparams(("arbitrary",)),
        name="hyena_spec_dense",
    )(jnp.asarray(f_real), full)


def _conv_dense_body(zr_ref, zi_ref, h_ref, ff_ref, fi_ref, bias_ref, or_ref, oi_ref):
    zr, zi = zr_ref[...], zi_ref[...]
    n = zr.shape[0]
    nn = 2 * n
    x = _dot(ff_ref[...], jnp.concatenate([zr, zi], 0), hi=True)
    xr, xi = x[:nn], x[nn:]
    hr, hi_ = h_ref[0:nn, :], h_ref[nn:2 * nn, :]
    y = _dot(fi_ref[...], jnp.concatenate([xr * hr - xi * hi_, xr * hi_ + xi * hr], 0), hi=True)
    bias = bias_ref[...]
    or_ref[...] = y[:n] + zr * bias
    oi_ref[...] = y[n:] + zi * bias


def hyena_conv_dense(zsrc, zcol, row0, n, n_batch, spec, bias3, layer, order):
    _, f_fwd, f_inv = _dense_dft_tables(n)
    nn = 2 * n
    rb, cb = row0 // n, zcol // LANE
    wb = BRANCH_W // LANE
    outs = pl.pallas_call(
        _conv_dense_body,
        grid=(n_batch // 2, wb),
        in_specs=[pl.BlockSpec((n, LANE), lambda p, j: (rb + 2 * p, cb + j)),
                  pl.BlockSpec((n, LANE), lambda p, j: (rb + 2 * p + 1, cb + j)),
                  pl.BlockSpec((2 * nn, LANE), lambda p, j: (0, order * wb + j)),
                  pl.BlockSpec((2 * nn, 2 * n), lambda p, j: (0, 0)),
                  pl.BlockSpec((2 * n, 2 * nn), lambda p, j: (0, 0)),
                  pl.BlockSpec((None, 1, LANE), lambda p, j: (layer * HY_ORDER + order, 0, j))],
        out_specs=[pl.BlockSpec((None, n, LANE), lambda p, j: (p, 0, j)),
                   pl.BlockSpec((None, n, LANE), lambda p, j: (p, 0, j))],
        out_shape=[jax.ShapeDtypeStruct((n_batch // 2, n, BRANCH_W), F32)] * 2,
        compiler_params=_cparams(("arbitrary", "arbitrary")),
        name="hyena_conv_dense",
    )(zsrc, zsrc, spec, jnp.asarray(f_fwd), jnp.asarray(f_inv), bias3)
    return jnp.stack(outs, 1).reshape(n_batch * n, BRANCH_W)


def _spec_fft_body(x_ref, f1_ref, f2_ref, o_ref, a_ref):
    n1c = o_ref.shape[0]

    def stage1(n2, carry):
        xs = x_ref[pl.ds(n2, n1c, stride=FFT_N2), :]
        a_ref[n2] = _dot(f1_ref[n2], xs)
        return carry

    lax.fori_loop(0, FFT_N2, stage1, 0, unroll=FFT_UNROLL)

    def stage2(k1, carry):
        blk = jnp.concatenate([a_ref[:, k1, :], a_ref[:, n1c + k1, :]], 0)
        o_ref[k1] = _dot(f2_ref[...], blk)
        return carry

    lax.fori_loop(0, n1c, stage2, 0, unroll=FFT_UNROLL // 2)


def hyena_spec_fft(full, n):
    f1_real, _, _, f2, _ = _two_stage_dft_tables(n)
    nn, cols = full.shape
    n1c = nn // FFT_N2
    const = lambda shape: pl.BlockSpec(shape, lambda j: (0,) * len(shape), pipeline_mode=pl.Buffered(1))
    return pl.pallas_call(
        _spec_fft_body,
        grid=(cols // LANE,),
        in_specs=[pl.BlockSpec((nn, LANE), lambda j: (0, j)),
                  const((FFT_N2, 2 * n1c, n1c)), const((2 * FFT_N2, 2 * FFT_N2))],
        out_specs=pl.BlockSpec((n1c, 2 * FFT_N2, LANE), lambda j: (0, 0, j)),
        out_shape=jax.ShapeDtypeStruct((n1c, 2 * FFT_N2, cols), F32),
        scratch_shapes=[pltpu.VMEM((FFT_N2, 2 * n1c, LANE), F32)],
        compiler_params=_cparams(("arbitrary",)),
        name="hyena_spec_fft",
    )(full, jnp.asarray(f1_real, BF16), jnp.asarray(f2, BF16))


def _conv_fft_body(zr_ref, zi_ref, h_ref, f1_ref, f2_ref, f2i_ref, g1_ref, bias_ref, or_ref, oi_ref, a_ref):
    n1c = h_ref.shape[0]
    n1h = n1c // 2
    n2c = FFT_N2

    def stage1(n2, carry):
        xs = jnp.concatenate([zr_ref[pl.ds(n2, n1h, stride=n2c), :], zi_ref[pl.ds(n2, n1h, stride=n2c), :]], 0)
        a_ref[n2] = _dot(f1_ref[n2], xs)
        return carry

    lax.fori_loop(0, n2c, stage1, 0, unroll=FFT_UNROLL)

    def stage2(k1, carry):
        x = _dot(f2_ref[...], jnp.concatenate([a_ref[:, k1, :], a_ref[:, n1c + k1, :]], 0))
        xr, xi = x[:n2c], x[n2c:]
        hr, hi_ = h_ref[k1, 0:n2c, :], h_ref[k1, n2c:2 * n2c, :]
        b = _dot(f2i_ref[...], jnp.concatenate([xr * hr - xi * hi_, xr * hi_ + xi * hr], 0))
        a_ref[:, k1, :] = b[:n2c]
        a_ref[:, n1c + k1, :] = b[n2c:]
        return carry

    lax.fori_loop(0, n1c, stage2, 0, unroll=FFT_UNROLL // 2)
    bias = bias_ref[...]

    def stage3(n2, carry):
        y = _dot(g1_ref[n2], a_ref[n2])
        rows = pl.ds(n2, n1h, stride=n2c)
        or_ref[rows, :] = y[:n1h] + zr_ref[rows, :] * bias
        oi_ref[rows, :] = y[n1h:] + zi_ref[rows, :] * bias
        return carry

    lax.fori_loop(0, n2c, stage3, 0, unroll=FFT_UNROLL)


def hyena_conv_fft(zsrc, zcol, n, n_batch, spec, bias3, layer, order):
    _, f1_cplx, g1, f2, f2i = _two_stage_dft_tables(n)
    n1c = 2 * n // FFT_N2
    cb = zcol // LANE
    wb = BRANCH_W // LANE
    const = lambda shape: pl.BlockSpec(shape, lambda j, p: (0,) * len(shape), pipeline_mode=pl.Buffered(1))
    outs = pl.pallas_call(
        _conv_fft_body,
        grid=(wb, n_batch // 2),
        in_specs=[pl.BlockSpec((n, LANE), lambda j, p: (2 * p, cb + j)),
                  pl.BlockSpec((n, LANE), lambda j, p: (2 * p + 1, cb + j)),
                  pl.BlockSpec((n1c, 2 * FFT_N2, LANE), lambda j, p: (0, 0, order * wb + j),
                               pipeline_mode=pl.Buffered(1)),
                  const((FFT_N2, 2 * n1c, n1c)), const((2 * FFT_N2, 2 * FFT_N2)), const((2 * FFT_N2, 2 * FFT_N2)),
                  const((FFT_N2, n1c, 2 * n1c)),
                  pl.BlockSpec((None, 1, LANE), lambda j, p: (layer * HY_ORDER + order, 0, j))],
        out_specs=[pl.BlockSpec((None, n, LANE), lambda j, p: (p, 0, j)),
                   pl.BlockSpec((None, n, LANE), lambda j, p: (p, 0, j))],
        out_shape=[jax.ShapeDtypeStruct((n_batch // 2, n, BRANCH_W), F32)] * 2,
        scratch_shapes=[pltpu.VMEM((FFT_N2, 2 * n1c, LANE), F32)],
        compiler_params=_cparams(("arbitrary", "arbitrary")),
        name="hyena_conv_fft",
    )(zsrc, zsrc, spec, jnp.asarray(f1_cplx, BF16), jnp.asarray(f2, BF16), jnp.asarray(f2i, BF16),
      jnp.asarray(g1, BF16), bias3)
    return jnp.stack(outs, 1).reshape(n_batch * n, BRANCH_W)


def _mul_body(a_ref, b_ref, o_ref):
    o_ref[...] = a_ref[...] * b_ref[...]


def mul_cols(a, b, bcol):
    t, w = a.shape
    r = 256
    return pl.pallas_call(
        _mul_body,
        grid=(t // r,),
        in_specs=[pl.BlockSpec((r, w), lambda i: (i, 0)), pl.BlockSpec((r, w), lambda i: (i, bcol // w))],
        out_specs=pl.BlockSpec((r, w), lambda i: (i, 0)),
        out_shape=jax.ShapeDtypeStruct((t, w), F32),
        compiler_params=_cparams(("arbitrary",)),
        name="mul_cols",
    )(a, b)


def _swap_pairs(x):
    w = x.shape[-1]
    lane = lax.broadcasted_iota(jnp.int32, x.shape, x.ndim - 1)
    return jnp.where(lane % 2 == 0, pltpu.roll(x, w - 1, x.ndim - 1), pltpu.roll(x, 1, x.ndim - 1))


def _attn_prep_body(g_ref, dq_ref, dk_ref, dv_ref, cg_ref, sg_ref, cd_ref, sd_ref, qn_ref, kn_ref,
                    qg_ref, kg_ref, vg_ref, qd_ref, kd_ref, vd_ref, *, lat_blocks):
    is_lat = pl.program_id(0) < lat_blocks
    cg = jnp.where(is_lat, cg_ref[...], 1.0)
    sg = jnp.where(is_lat, sg_ref[...], 0.0)
    cd = jnp.where(is_lat, cd_ref[...], 1.0)
    sd = jnp.where(is_lat, sd_ref[...], 0.0)

    def rope(x, cs, sn):
        return x * cs + _swap_pairs(x) * sn

    def rms(x, w):
        return x * lax.rsqrt(jnp.mean(x * x, axis=-1, keepdims=True) + EPS) * w

    for h in range(HEADS):
        sl = slice(h * HEAD_D, (h + 1) * HEAD_D)
        q = rope(rms(g_ref[:, sl], qn_ref[...]), cg, sg)
        qg_ref[:, sl] = (q * HEAD_D ** -0.5).astype(BF16)
        qd_ref[:, sl] = (rope(dq_ref[:, sl], cd, sd) * DIFF_QK ** -0.5).astype(BF16)
        kd_ref[:, sl] = rope(dk_ref[:, sl], cd, sd).astype(BF16)
    for h in range(GQA_KV):
        sl = slice(h * HEAD_D, (h + 1) * HEAD_D)
        kin = g_ref[:, BRANCH_W + h * HEAD_D:BRANCH_W + (h + 1) * HEAD_D]
        kg_ref[:, sl] = rope(rms(kin, kn_ref[...]), cg, sg).astype(BF16)
    vg_ref[...] = g_ref[:, BRANCH_W + GQA_KV * HEAD_D:BRANCH_W + 2 * GQA_KV * HEAD_D].astype(BF16)
    vd_ref[...] = dv_ref[...].astype(BF16)


def attn_prep(p, ropes, qn3, kn3, layer, n_lat, n_ctx, n_batch):
    t = p.shape[0]
    r = 256 if n_ctx % 256 == 0 else n_ctx
    nlb, ncb = n_lat // r, n_ctx // r
    lat_blocks = n_batch * nlb
    kvw = GQA_KV * HEAD_D
    w = BRANCH_W

    def kv_row(i):
        lat = (i // nlb) * (nlb + ncb) + ncb + i % nlb
        j = i - lat_blocks
        ctx = (j // ncb) * (nlb + ncb) + j % ncb
        return jnp.where(i < lat_blocks, lat, ctx)

    rope_spec = pl.BlockSpec((r, LANE), lambda i: (jnp.where(i < lat_blocks, i % nlb, 0), 0))
    nkv = n_batch * (n_lat + n_ctx)
    return pl.pallas_call(
        functools.partial(_attn_prep_body, lat_blocks=lat_blocks),
        grid=(t // r,),
        in_specs=[pl.BlockSpec((r, 2 * w), lambda i: (i, C_GQA_QKV // (2 * w))),
                  pl.BlockSpec((r, w), lambda i: (i, C_DIFF_Q // w)),
                  pl.BlockSpec((r, w), lambda i: (i, C_DIFF_K // w)),
                  pl.BlockSpec((r, w), lambda i: (i, C_DIFF_V // w)),
                  rope_spec, rope_spec, rope_spec, rope_spec,
                  pl.BlockSpec((None, 1, LANE), lambda i: (layer, 0, 0)),
                  pl.BlockSpec((None, 1, LANE), lambda i: (layer, 0, 0))],
        out_specs=[pl.BlockSpec((r, w), lambda i: (i, 0)),
                   pl.BlockSpec((r, kvw), lambda i: (kv_row(i), 0)),
                   pl.BlockSpec((r, kvw), lambda i: (kv_row(i), 0)),
                   pl.BlockSpec((r, w), lambda i: (i, 0)),
                   pl.BlockSpec((r, w), lambda i: (kv_row(i), 0)),
                   pl.BlockSpec((r, w), lambda i: (kv_row(i), 0))],
        out_shape=[jax.ShapeDtypeStruct((t, w), BF16), jax.ShapeDtypeStruct((nkv, kvw), BF16),
                   jax.ShapeDtypeStruct((nkv, kvw), BF16), jax.ShapeDtypeStruct((t, w), BF16),
                   jax.ShapeDtypeStruct((nkv, w), BF16), jax.ShapeDtypeStruct((nkv, w), BF16)],
        compiler_params=_cparams(("arbitrary",)),
        name="attn_prep",
    )(p, p, p, p, *ropes, qn3, kn3)


def _softmax_parts(s):
    e = jnp.exp(s - jnp.max(s, axis=-1, keepdims=True))
    return e, jnp.sum(e, axis=-1, keepdims=True)


def _gqa_body(q_ref, k_ref, v_ref, o_ref):
    group = HEADS // GQA_KV
    for kvh in range(GQA_KV):
        k = k_ref[:, kvh * HEAD_D:(kvh + 1) * HEAD_D]
        v = v_ref[:, kvh * HEAD_D:(kvh + 1) * HEAD_D]
        for g in range(group):
            sl = slice((kvh * group + g) * HEAD_D, (kvh * group + g + 1) * HEAD_D)
            s = lax.dot_general(q_ref[:, sl], k, (((1,), (1,)), ((), ())), preferred_element_type=F32)
            e, l = _softmax_parts(s)
            o_ref[:, sl] = jnp.dot(e.astype(BF16), v, preferred_element_type=F32) / l


def _diff_body(q_ref, k_ref, v_ref, lam_ref, o_ref, *, lam_init):
    lam4 = lam_ref[...]
    lam = (jnp.exp(jnp.sum(lam4[0:1] * lam4[1:2], axis=-1, keepdims=True))
           - jnp.exp(jnp.sum(lam4[2:3] * lam4[3:4], axis=-1, keepdims=True)) + lam_init)
    dn = (((1,), (1,)), ((), ()))
    for h in range(HEADS):
        sl = slice(h * HEAD_D, (h + 1) * HEAD_D)
        q = q_ref[:, sl]
        k = k_ref[:, sl]
        v = v_ref[:, sl]
        first = lax.broadcasted_iota(jnp.int32, q.shape, 1) < DIFF_QK
        zero = jnp.zeros_like(q)
        e1, l1 = _softmax_parts(lax.dot_general(jnp.where(first, q, zero), k, dn, preferred_element_type=F32))
        e2, l2 = _softmax_parts(lax.dot_general(jnp.where(first, zero, q), k, dn, preferred_element_type=F32))
        o1 = jnp.dot(e1.astype(BF16), v, preferred_element_type=F32) / l1
        o2 = jnp.dot(e2.astype(BF16), v, preferred_element_type=F32) / l2
        o_ref[:, sl] = o1 - lam * o2


def attention(body, q, k, v, extra, extra_specs, q_row0, nq, kv_per_batch, kv_len, n_batch, tq, name):
    w = q.shape[1]
    qb0 = q_row0 // tq
    nqb = nq // tq
    kvb = kv_per_batch // kv_len
    return pl.pallas_call(
        body,
        grid=(n_batch, nqb),
        in_specs=[pl.BlockSpec((tq, w), lambda b, i: (qb0 + b * nqb + i, 0)),
                  pl.BlockSpec((kv_len, k.shape[1]), lambda b, i: (b * kvb, 0)),
                  pl.BlockSpec((kv_len, v.shape[1]), lambda b, i: (b * kvb, 0))] + extra_specs,
        out_specs=pl.BlockSpec((tq, w), lambda b, i: (b * nqb + i, 0)),
        out_shape=jax.ShapeDtypeStruct((n_batch * nq, w), F32),
        compiler_params=_cparams(("arbitrary", "arbitrary")),
        name=name,
    )(q, k, v, *extra)


def _merge_body(h_ref, mod_ref, mg_ref, of_ref, ob_ref, ggate_ref, y1_ref, x2_ref, hgate_ref, oc_ref, cgate_ref,
                od_ref, dgate_ref, gnorm_ref, dnorm_ref, wbr_ref, wout_ref, lng_ref, lnb_ref, o_ref, *, diff_scale):
    def rms_heads(x, w):
        parts = []
        for h in range(HEADS):
            xh = x[:, h * HEAD_D:(h + 1) * HEAD_D]
            parts.append(xh * lax.rsqrt(jnp.mean(xh * xh, axis=-1, keepdims=True) + EPS) * w)
        return jnp.concatenate(parts, -1)

    ys = (rms_heads(of_ref[...] + ob_ref[...], gnorm_ref[...]) * _silu(ggate_ref[...]),
          x2_ref[...] * y1_ref[...] * _silu(hgate_ref[...]),
          oc_ref[...] * _silu(cgate_ref[...]),
          rms_heads(od_ref[...], dnorm_ref[...]) * diff_scale * _silu(dgate_ref[...]))
    acc = None
    for n in range(N_BRANCH):
        proj = jnp.dot(ys[n].astype(BF16), wbr_ref[n], preferred_element_type=F32)
        term = _sigmoid(mg_ref[:, n * D_MODEL:(n + 1) * D_MODEL]) * proj
        acc = term if acc is None else acc + term
    out = jnp.dot(acc.astype(BF16), wout_ref[...], preferred_element_type=F32)
    x = ALPHA * h_ref[...] + mod_ref[2:3, :] * out
    mu = jnp.mean(x, axis=-1, keepdims=True)
    xc = x - mu
    var = jnp.mean(xc * xc, axis=-1, keepdims=True)
    o_ref[...] = xc * lax.rsqrt(var + EPS) * lng_ref[...] + lnb_ref[...]


def merge_postnorm(h_all, mod3, p, o_f, o_b, y1, xv, oc, od, gnorm3, dnorm3, wbr, wout, lng3, lnb3, layer, lam_init,
                   n_lat, n_batch):
    t, d = h_all.shape
    r = 256 if n_lat % 256 == 0 else 64
    w = BRANCH_W
    lbb = n_lat // r
    row = lambda i: jnp.minimum(i // lbb, n_batch)
    tok = lambda cb: pl.BlockSpec((r, w), lambda i: (i, cb))
    vec = lambda width: pl.BlockSpec((None, 1, width), lambda i: (layer, 0, 0))
    return pl.pallas_call(
        functools.partial(_merge_body, diff_scale=1.0 - lam_init),
        grid=(t // r,),
        in_specs=[pl.BlockSpec((r, d), lambda i: (i, 0)),
                  pl.BlockSpec((None, 3, d), lambda i: (row(i), 0, 0)),
                  pl.BlockSpec((r, N_BRANCH * d), lambda i: (i, C_MERGE // (N_BRANCH * d))),
                  tok(0), tok(0), tok(C_GDN_GATE // w), tok(0), tok(1), tok(C_HY_GATE // w), tok(0),
                  tok(C_GQA_GATE // w), tok(0), tok(C_DIFF_GATE // w),
                  vec(LANE), vec(LANE),
                  pl.BlockSpec((None, N_BRANCH, w, d), lambda i: (layer, 0, 0, 0)),
                  pl.BlockSpec((None, d, d), lambda i: (layer, 0, 0)),
                  vec(d), vec(d)],
        out_specs=pl.BlockSpec((r, d), lambda i: (i, 0)),
        out_shape=jax.ShapeDtypeStruct((t, d), F32),
        compiler_params=_cparams(("arbitrary",)),
        name="merge_postnorm",
    )(h_all, mod3, p, o_f, o_b, p, y1, xv, p, oc, p, od, p, gnorm3, dnorm3, wbr, wout, lng3, lnb3)


def _rope_tables(n_lat, dim):
    rows = n_lat // GRID_W
    row = jnp.repeat(jnp.arange(rows, dtype=F32), GRID_W)
    col = jnp.tile(jnp.arange(GRID_W, dtype=F32), rows)
    half = dim // 2
    inv = ROPE_THETA ** (-jnp.arange(0, half, 2, dtype=F32) / half)
    ang = jnp.concatenate([row[:, None] * inv, col[:, None] * inv], -1)
    cos = jnp.repeat(jnp.cos(ang), 2, axis=-1)
    sin = jnp.repeat(jnp.sin(ang), 2, axis=-1)
    sign = jnp.tile(jnp.array([-1.0, 1.0], F32), dim // 2)
    reps = LANE // dim
    return jnp.tile(cos, (1, reps)), jnp.tile(sin * sign, (1, reps))


def kernel(x, c, ctx, c_ctx, w_ada, b_ada, w_in, gdn_conv, gdn_a_log, gdn_dt_bias, gdn_norm, hy_conv, hy_w1, hy_b1,
           hy_w2, hy_b2, hy_w3, hy_b3, hy_w4, hy_freq, hy_bias, gqa_qn, gqa_kn, diff_lam, diff_norm, w_br, w_out,
           ln_g, ln_b):
    nb, n_lat, d = x.shape
    n_ctx = ctx.shape[1]
    t_lat, t_ctx = nb * n_lat, nb * n_ctx
    depth = w_in.shape[0]
    w = BRANCH_W

    w_main = jnp.concatenate([w_in[:, :, O_MERGE:], w_in[:, :, :O_GDN_AB], w_in[:, :, O_GDN_AB + 4 * HEADS:O_MERGE]],
                             axis=2).astype(BF16)
    w_ab = jnp.pad(w_in[:, :, O_GDN_AB:O_GDN_AB + 4 * HEADS], ((0, 0), (0, 0), (0, LANE - 4 * HEADS)))
    wbr_bf = w_br.astype(BF16)
    wout_bf = w_out.astype(BF16)
    b_ada3 = b_ada[:, None, :]
    cvec = jnp.concatenate([c, c_ctx[None, :], jnp.zeros((SUB - nb - 1, d), F32)], 0)
    as3 = lambda a: a[:, None, :]
    gdn_par_r = jnp.pad(jnp.stack([gdn_a_log.reshape(depth, -1), gdn_dt_bias.reshape(depth, -1)], 1),
                        ((0, 0), (0, SUB - 2), (0, LANE - 2 * HEADS)))
    gdn_par_c = jnp.pad(jnp.stack([gdn_a_log.reshape(depth, -1), gdn_dt_bias.reshape(depth, -1)], 2),
                        ((0, 0), (0, 2 * HEADS), (0, LANE - 2)))
    hy_w1p = jnp.pad(hy_w1, ((0, 0), (0, LANE - HY_EMB), (0, 0)))
    hy_bias3 = hy_bias.reshape(depth * HY_ORDER, 1, w)
    ropes = _rope_tables(n_lat, HEAD_D) + _rope_tables(n_lat, DIFF_QK)

    tm = 1024 if (n_lat % 1024 == 0 and t_ctx % 1024 == 0) else n_ctx
    h_all = jnp.concatenate([x.reshape(t_lat, d), ctx.reshape(t_ctx, d)], 0)
    for l in range(depth):
        lam_init = 0.8 - 0.6 * math.exp(-0.3 * l)
        mod3 = ada_mod(cvec, w_ada, b_ada3, l).reshape(SUB, 3, d)
        p, ab = in_proj(h_all, mod3, w_main, w_ab, l, tm, n_lat // tm, nb)

        qkv = dwconv(p, gdn_conv, l, C_GDN_QKV, 3 * w, n_lat, n_ctx, nb, act=True)
        ab_rows = jnp.transpose(ab[:, :4 * HEADS].reshape(-1, GDN_CHUNK, 4 * HEADS), (0, 2, 1))
        o_f, o_b = gdn_scan(qkv, ab, ab_rows, gdn_par_r[l], gdn_par_c[l], n_lat, n_ctx, nb)

        xv = dwconv(p, hy_conv, l, C_HY_XV, 3 * w, n_lat, n_ctx, nb, act=False)
        filt = lambda n: _two_sided(hyena_filter(n, hy_w1p, as3(hy_b1), hy_w2, as3(hy_b2), hy_w3, as3(hy_b3), hy_w4,
                                                 as3(hy_freq), l), n)
        spec_lat = hyena_spec_fft(filt(n_lat), n_lat)
        spec_ctx = hyena_spec_dense(filt(n_ctx), n_ctx)
        y0 = jnp.concatenate([hyena_conv_fft(xv, 2 * w, n_lat, nb, spec_lat, hy_bias3, l, 0),
                              hyena_conv_dense(xv, 2 * w, t_lat, n_ctx, nb, spec_ctx, hy_bias3, l, 0)], 0)
        z1 = mul_cols(y0, xv, 0)
        y1 = jnp.concatenate([hyena_conv_fft(z1, 0, n_lat, nb, spec_lat, hy_bias3, l, 1),
                              hyena_conv_dense(z1, 0, t_lat, n_ctx, nb, spec_ctx, hy_bias3, l, 1)], 0)

        qg, kg, vg, qd, kd, vd = attn_prep(p, ropes, as3(gqa_qn), as3(gqa_kn), l, n_lat, n_ctx, nb)
        kv_all = n_lat + n_ctx
        tq = min(256, n_ctx)
        lam_spec = [pl.BlockSpec((None, 4, DIFF_QK), lambda b, i: (l, 0, 0))]
        diff_body = functools.partial(_diff_body, lam_init=lam_init)
        oc = jnp.concatenate([
            attention(_gqa_body, qg, kg, vg, (), [], 0, n_lat, kv_all, kv_all, nb, tq, "gqa_lat"),
            attention(_gqa_body, qg, kg, vg, (), [], t_lat, n_ctx, kv_all, n_ctx, nb, tq, "gqa_ctx")], 0)
        od = jnp.concatenate([
            attention(diff_body, qd, kd, vd, (diff_lam,), lam_spec, 0, n_lat, kv_all, kv_all, nb, tq, "diff_lat"),
            attention(diff_body, qd, kd, vd, (diff_lam,), lam_spec, t_lat, n_ctx, kv_all, n_ctx, nb, tq, "diff_ctx")], 0)

        h_all = merge_postnorm(h_all, mod3, p, o_f, o_b, y1, xv, oc, od, as3(gdn_norm), as3(diff_norm), wbr_bf, wout_bf,
                               as3(ln_g), as3(ln_b), l, lam_init, n_lat, nb)
    return h_all[:t_lat].reshape(nb, n_lat, d)
```

```python
import functools
import math

import numpy as np
import jax
import jax.numpy as jnp
from jax import lax
from jax.experimental import pallas as pl
from jax.experimental.pallas import tpu as pltpu

F32 = jnp.float32
BF16 = jnp.bfloat16
HI = lax.Precision.HIGHEST

D_MODEL = 1024
DEPTH = 4
GRID_W = 64
BRANCH_W = D_MODEL // 2
N_BRANCH = 4
HEADS = 4
HEAD_D = BRANCH_W // HEADS
GDN_CONV = 4
GDN_CHUNK = 64
HY_CONV = 3
HY_EMB = 33
HY_BANDS = (HY_EMB - 1) // 2
HY_FH = 64
HY_ORDER = 2
HY_MIN_DECAY = math.log(1e-2) / 1.5
HY_MAX_DECAY = math.log(1e-2) / 0.3
GQA_KV = 2
DIFF_QK = HEAD_D // 2
ROPE_THETA = 10000.0
EPS = 1e-6
ALPHA = (2.0 * DEPTH) ** 0.25

LANE = 128
SUB = 8
FFT_N2 = 128
FFT_UNROLL = 8
VMEM_LIMIT = 60 * 1024 * 1024

C_MERGE = 0
C_GDN_QKV = 4096
C_GDN_GATE = 5632
C_HY_XV = 6144
C_HY_GATE = 7680
C_GQA_QKV = 8192
C_GQA_GATE = 9216
C_DIFF_Q = 9728
C_DIFF_K = 10240
C_DIFF_V = 10752
C_DIFF_GATE = 11264
N_MAIN = 11776
O_GDN_AB = 1536
O_MERGE = 7696


def _cparams(sem):
    return pltpu.CompilerParams(dimension_semantics=sem, vmem_limit_bytes=VMEM_LIMIT)


def _dot(a, b, hi=False):
    if hi:
        return jnp.dot(a, b, precision=HI, preferred_element_type=F32)
    return jnp.dot(a.astype(BF16), b.astype(BF16), preferred_element_type=F32)


def _dot_nt(a, b, hi=False):
    dn = (((1,), (1,)), ((), ()))
    if hi:
        return lax.dot_general(a, b, dn, precision=HI, preferred_element_type=F32)
    return lax.dot_general(a.astype(BF16), b.astype(BF16), dn, preferred_element_type=F32)


def _dot_tn(a, b):
    return lax.dot_general(a.astype(BF16), b.astype(BF16), (((0,), (0,)), ((), ())), preferred_element_type=F32)


def _sigmoid(x):
    return 1.0 / (1.0 + jnp.exp(-x))


def _silu(x):
    return x * _sigmoid(x)


def _softplus(x):
    return jnp.maximum(x, 0.0) + jnp.log1p(jnp.exp(-jnp.abs(x)))


def _ada_body(c_ref, w_ref, b_ref, o_ref):
    o_ref[...] = _dot(_silu(c_ref[...]), w_ref[...], hi=True) + b_ref[...]


def ada_mod(cvec, w_ada, b_ada3, layer):
    d = cvec.shape[1]
    tn = 512
    return pl.pallas_call(
        _ada_body,
        grid=(3 * d // tn,),
        in_specs=[pl.BlockSpec((SUB, d), lambda j: (0, 0)),
                  pl.BlockSpec((None, d, tn), lambda j: (layer, 0, j)),
                  pl.BlockSpec((None, 1, tn), lambda j: (layer, 0, j))],
        out_specs=pl.BlockSpec((SUB, tn), lambda j: (0, j)),
        out_shape=jax.ShapeDtypeStruct((SUB, 3 * d), F32),
        compiler_params=_cparams(("arbitrary",)),
        name="ada_mod",
    )(cvec, w_ada, b_ada3)


def _inproj_body(h_ref, mod_ref, w_ref, wab_ref, o_ref, ab_ref, u_ref):
    @pl.when(pl.program_id(1) == 0)
    def _():
        x = h_ref[...]
        mu = jnp.mean(x, axis=-1, keepdims=True)
        xc = x - mu
        var = jnp.mean(xc * xc, axis=-1, keepdims=True)
        u = xc * lax.rsqrt(var + EPS) * (1.0 + mod_ref[1:2, :]) + mod_ref[0:1, :]
        u_ref[...] = u.astype(BF16)
        ab_ref[...] = _dot(u, wab_ref[...], hi=True)

    o_ref[...] = jnp.dot(u_ref[...], w_ref[...], preferred_element_type=F32)


def in_proj(h_all, mod3, w_main, w_ab, layer, tm, lat_blocks_per_batch, n_batch):
    t, d = h_all.shape
    tn = 512
    n_main = w_main.shape[2]
    row = lambda i: jnp.minimum(i // lat_blocks_per_batch, n_batch)
    return pl.pallas_call(
        _inproj_body,
        grid=(t // tm, n_main // tn),
        in_specs=[pl.BlockSpec((tm, d), lambda i, j: (i, 0)),
                  pl.BlockSpec((None, 3, d), lambda i, j: (row(i), 0, 0)),
                  pl.BlockSpec((None, d, tn), lambda i, j: (layer, 0, j)),
                  pl.BlockSpec((None, d, LANE), lambda i, j: (layer, 0, 0))],
        out_specs=[pl.BlockSpec((tm, tn), lambda i, j: (i, j)),
                   pl.BlockSpec((tm, LANE), lambda i, j: (i, 0))],
        out_shape=[jax.ShapeDtypeStruct((t, n_main), F32), jax.ShapeDtypeStruct((t, LANE), F32)],
        scratch_shapes=[pltpu.VMEM((tm, d), BF16)],
        compiler_params=_cparams(("arbitrary", "arbitrary")),
        name="in_proj",
    )(h_all, mod3, w_main, w_ab)


def _dwconv_body(xp_ref, x_ref, xn_ref, w_ref, o_ref, pad_ref, *, taps, pad_l, lat_blocks, bl, bc, act):
    i = pl.program_id(0)
    r = x_ref.shape[0]
    is_lat = i < lat_blocks
    pos = jnp.where(is_lat, i % bl, (i - lat_blocks) % bc)
    last = jnp.where(is_lat, bl - 1, bc - 1)
    pad_ref[0:SUB, :] = jnp.where(pos == 0, 0.0, xp_ref[...])
    pad_ref[SUB:SUB + r, :] = x_ref[...]
    pad_ref[SUB + r:2 * SUB + r, :] = jnp.where(pos == last, 0.0, xn_ref[...])
    acc = None
    for j in range(taps):
        off = SUB + j - pad_l
        term = w_ref[j:j + 1, :] * pad_ref[off:off + r, :]
        acc = term if acc is None else acc + term
    if act:
        acc = _silu(acc)
    o_ref[...] = acc


def dwconv(p, w_conv, layer, col0, width, n_lat, n_ctx, n_batch, act):
    t = p.shape[0]
    taps = w_conv.shape[1]
    r = 256 if n_ctx % 256 == 0 else n_ctx
    lw = 512
    cb = col0 // lw
    rs = r // SUB
    body = functools.partial(_dwconv_body, taps=taps, pad_l=(taps - 1) // 2, lat_blocks=n_batch * n_lat // r,
                             bl=n_lat // r, bc=n_ctx // r, act=act)
    return pl.pallas_call(
        body,
        grid=(t // r, width // lw),
        in_specs=[pl.BlockSpec((SUB, lw), lambda i, j: (jnp.maximum(i * rs - 1, 0), cb + j)),
                  pl.BlockSpec((r, lw), lambda i, j: (i, cb + j)),
                  pl.BlockSpec((SUB, lw), lambda i, j: (jnp.minimum((i + 1) * rs, t // SUB - 1), cb + j)),
                  pl.BlockSpec((None, taps, lw), lambda i, j: (layer, 0, j))],
        out_specs=pl.BlockSpec((r, lw), lambda i, j: (i, j)),
        out_shape=jax.ShapeDtypeStruct((t, width), F32),
        scratch_shapes=[pltpu.VMEM((r + 2 * SUB, lw), F32)],
        compiler_params=_cparams(("arbitrary", "arbitrary")),
        name="dwconv",
    )(p, p, p, w_conv)


def _gdn_body(qf_ref, qb_ref, abcf_ref, abcb_ref, abrf_ref, abrb_ref, pr_ref, pc_ref, of_ref, ob_ref, s_ref):
    c = GDN_CHUNK

    @pl.when(pl.program_id(1) == 0)
    def _():
        s_ref[...] = jnp.zeros_like(s_ref)

    ii = lax.broadcasted_iota(jnp.int32, (c, c), 0)
    jj = lax.broadcasted_iota(jnp.int32, (c, c), 1)
    lmat = (jj <= ii).astype(F32)
    eye = (jj == ii).astype(F32)
    alr, dtr = pr_ref[0:1, :], pr_ref[1:2, :]
    alc, dtc = pc_ref[:, 0:1], pc_ref[:, 1:2]
    chains = []
    for d in range(2):
        qkv_ref = (qf_ref, qb_ref)[d]
        abc = (abcf_ref, abcb_ref)[d][...]
        abr = (abrf_ref, abrb_ref)[d][...]
        g_c = -jnp.exp(alr) * _softplus(abc + dtr)
        g_r = -jnp.exp(alc) * _softplus(abr + dtc)
        cum_c = _dot(lmat, g_c, hi=True)
        cum_r = _dot_nt(g_r, lmat, hi=True)
        if d == 1:
            cum_c = cum_c[c - 1:c, :] - cum_c + g_c
            cum_r = cum_r[:, c - 1:c] - cum_r + g_r
        beta_all = _sigmoid(abc)
        incl = (jj <= ii) if d == 0 else (jj >= ii)
        strict = (jj < ii) if d == 0 else (jj > ii)
        for h in range(HEADS):
            idx = HEADS * d + h
            q = qkv_ref[:, h * HEAD_D:(h + 1) * HEAD_D]
            k = qkv_ref[:, BRANCH_W + h * HEAD_D:BRANCH_W + (h + 1) * HEAD_D]
            v = qkv_ref[:, 2 * BRANCH_W + h * HEAD_D:2 * BRANCH_W + (h + 1) * HEAD_D]
            q = q * lax.rsqrt(jnp.sum(q * q, axis=-1, keepdims=True) + EPS) * (HEAD_D ** -0.5)
            k = k * lax.rsqrt(jnp.sum(k * k, axis=-1, keepdims=True) + EPS)
            cc = cum_c[:, idx:idx + 1]
            cr = cum_r[idx:idx + 1, :]
            dec = jnp.exp(jnp.where(incl, cc - cr, -1e30))
            beta = beta_all[:, 2 * HEADS + idx:2 * HEADS + idx + 1]
            ecum = jnp.exp(cc)
            tot = cc[c - 1:c, :] if d == 0 else cc[0:1, :]
            chains.append(dict(d=d, h=h, q=q, k=k, dec=dec, strict=strict, beta=beta, ecum=ecum, tot=tot,
                               rhs=jnp.concatenate([k * (beta * ecum), v * beta], 1),
                               k_tail=k * jnp.exp(tot - cc)))
    for ch in chains:
        ch["kk"] = _dot_nt(ch["k"], ch["k"])
        ch["qk"] = _dot_nt(ch["q"], ch["k"])
    for ch in chains:
        ch["p"] = -jnp.where(ch["strict"], ch["beta"] * ch["kk"] * ch["dec"], 0.0)
        ch["inv"] = eye + ch["p"]
    for _ in range(int(math.log2(c)) - 1):
        for ch in chains:
            ch["p"] = _dot(ch["p"], ch["p"])
        for ch in chains:
            ch["inv"] = ch["inv"] + _dot(ch["inv"], ch["p"])
    for ch in chains:
        ch["wu"] = _dot(ch["inv"], ch["rhs"])
    for ch in chains:
        ch["s"] = s_ref[ch["d"], ch["h"]]
        ch["ws"] = _dot(jnp.concatenate([ch["wu"][:, :HEAD_D], ch["q"] * ch["ecum"]], 0), ch["s"])
    for ch in chains:
        ch["v_new"] = ch["wu"][:, HEAD_D:] - ch["ws"][:c]
    for ch in chains:
        out_ref = (of_ref, ob_ref)[ch["d"]]
        h = ch["h"]
        out_ref[:, h * HEAD_D:(h + 1) * HEAD_D] = ch["ws"][c:] + _dot(ch["qk"] * ch["dec"], ch["v_new"])
        s_ref[ch["d"], h] = ch["s"] * jnp.exp(ch["tot"]) + _dot_tn(ch["k_tail"], ch["v_new"])


def gdn_scan(qkv, ab, ab_rows, par_r, par_c, n_lat, n_ctx, n_batch):
    t = qkv.shape[0]
    c = GDN_CHUNK
    nlc, ncc = n_lat // c, n_ctx // c
    base = n_batch * nlc

    def fwd(b, s):
        return jnp.where(s < ncc, base + b * ncc + s, b * nlc + (s - ncc))

    def bwd(b, s):
        return jnp.where(s < ncc, base + b * ncc + (ncc - 1 - s), b * nlc + (nlc - 1 - (s - ncc)))

    w3 = 3 * BRANCH_W
    return pl.pallas_call(
        _gdn_body,
        grid=(n_batch, ncc + nlc),
        in_specs=[pl.BlockSpec((c, w3), lambda b, s: (fwd(b, s), 0)),
                  pl.BlockSpec((c, w3), lambda b, s: (bwd(b, s), 0)),
                  pl.BlockSpec((c, LANE), lambda b, s: (fwd(b, s), 0)),
                  pl.BlockSpec((c, LANE), lambda b, s: (bwd(b, s), 0)),
                  pl.BlockSpec((None, 4 * HEADS, c), lambda b, s: (fwd(b, s), 0, 0)),
                  pl.BlockSpec((None, 4 * HEADS, c), lambda b, s: (bwd(b, s), 0, 0)),
                  pl.BlockSpec((SUB, LANE), lambda b, s: (0, 0)),
                  pl.BlockSpec((4 * HEADS, LANE), lambda b, s: (0, 0))],
        out_specs=[pl.BlockSpec((c, BRANCH_W), lambda b, s: (fwd(b, s), 0)),
                   pl.BlockSpec((c, BRANCH_W), lambda b, s: (bwd(b, s), 0))],
        out_shape=[jax.ShapeDtypeStruct((t, BRANCH_W), F32), jax.ShapeDtypeStruct((t, BRANCH_W), F32)],
        scratch_shapes=[pltpu.VMEM((2, HEADS, HEAD_D, HEAD_D), F32)],
        compiler_params=_cparams(("arbitrary", "arbitrary")),
        name="gdn_scan",
    )(qkv, qkv, ab, ab, ab_rows, ab_rows, par_r, par_c)


def _hyfilt_body(z_ref, aux_ref, w1_ref, b1_ref, w2_ref, b2_ref, w3_ref, b3_ref, w4_ref, fr_ref, dl_ref, o_ref):
    fr = fr_ref[...]
    h = jnp.sin(fr * (_dot(z_ref[...], w1_ref[...], hi=True) + b1_ref[...]))
    h = jnp.sin(fr * (_dot(h, w2_ref[...], hi=True) + b2_ref[...]))
    h = jnp.sin(fr * (_dot(h, w3_ref[...], hi=True) + b3_ref[...]))
    taps = _dot(h, w4_ref[...], hi=True) * jnp.exp(-aux_ref[:, 0:1] * dl_ref[...])
    w = BRANCH_W
    negative = aux_ref[:, 1:2] > 0.5
    keep = aux_ref[:, 2:3]
    for o in range(HY_ORDER):
        fwd = taps[:, o * 2 * w:o * 2 * w + w]
        bwd = taps[:, o * 2 * w + w:(o + 1) * 2 * w]
        o_ref[:, o * w:(o + 1) * w] = jnp.where(negative, bwd, fwd) * keep


def hyena_filter(n, w1p, b1, w2, b2, w3, b3, w4, fr, layer):
    row = jnp.arange(2 * n)
    src = jnp.where(row <= n, row, 2 * n - row)
    pos = jnp.where(row == n, 0, src).astype(F32)
    tt = pos / max(n - 1, 1)
    ang = (2.0 * math.pi / n) * pos[:, None] * jnp.linspace(1e-4, HY_BANDS - 1, HY_BANDS, dtype=F32)
    z = jnp.concatenate([tt[:, None], jnp.cos(ang), -jnp.sin(ang), jnp.zeros((2 * n, LANE - HY_EMB), F32)], -1)
    aux = jnp.stack([tt, (row > n).astype(F32), (row != n).astype(F32)], 1)
    aux = jnp.pad(aux, ((0, 0), (0, SUB - 3)))
    deltas = jnp.abs(jnp.linspace(HY_MIN_DECAY, HY_MAX_DECAY, BRANCH_W, dtype=F32))
    dl = jnp.tile(deltas, 2 * HY_ORDER)[None, :]
    r = 512
    wo = 2 * HY_ORDER * BRANCH_W
    full = lambda shape: pl.BlockSpec((None,) + shape, lambda i: (layer,) + (0,) * len(shape))
    return pl.pallas_call(
        _hyfilt_body,
        grid=(2 * n // r,),
        in_specs=[pl.BlockSpec((r, LANE), lambda i: (i, 0)),
                  pl.BlockSpec((r, SUB), lambda i: (i, 0)),
                  full((LANE, HY_FH)), full((1, HY_FH)), full((HY_FH, HY_FH)), full((1, HY_FH)),
                  full((HY_FH, HY_FH)), full((1, HY_FH)), full((HY_FH, wo)), full((1, HY_FH)),
                  pl.BlockSpec((1, wo), lambda i: (0, 0))],
        out_specs=pl.BlockSpec((r, HY_ORDER * BRANCH_W), lambda i: (i, 0)),
        out_shape=jax.ShapeDtypeStruct((2 * n, HY_ORDER * BRANCH_W), F32),
        compiler_params=_cparams(("arbitrary",)),
        name="hyena_filter",
    )(z, aux, w1p, b1, w2, b2, w3, b3, w4, fr, dl)


@functools.lru_cache(maxsize=None)
def _dense_dft_tables(n):
    nn = 2 * n
    k = np.arange(nn)[:, None].astype(np.float64)
    m = np.arange(nn)[None, :].astype(np.float64)
    ang = -2.0 * np.pi * k * m / nn
    wr, wi = np.cos(ang), np.sin(ang)
    f_real = np.concatenate([wr, wi], 0)
    wr_h, wi_h = wr[:, :n], wi[:, :n]
    f_fwd = np.block([[wr_h, -wi_h], [wi_h, wr_h]])
    cr, ci = wr.T[:n] / nn, -wi.T[:n] / nn
    f_inv = np.block([[cr, -ci], [ci, cr]])
    return (np.asarray(f_real, np.float32), np.asarray(f_fwd, np.float32), np.asarray(f_inv, np.float32))


@functools.lru_cache(maxsize=None)
def _two_stage_dft_tables(n):
    nn = 2 * n
    n2c = FFT_N2
    n1c = nn // n2c
    n1h = n1c // 2
    k1 = np.arange(n1c).astype(np.float64)
    n1 = np.arange(n1c).astype(np.float64)
    n2 = np.arange(n2c).astype(np.float64)
    ang = -2.0 * np.pi * (k1[None, :, None] * n1[None, None, :] / n1c + n2[:, None, None] * k1[None, :, None] / nn)
    mr, mi = np.cos(ang), np.sin(ang)
    f1_real = np.concatenate([mr, mi], 1)
    mrh, mih = mr[:, :, :n1h], mi[:, :, :n1h]
    f1_cplx = np.concatenate([np.concatenate([mrh, -mih], 2), np.concatenate([mih, mrh], 2)], 1)
    gr = np.transpose(mr, (0, 2, 1))[:, :n1h, :] / nn
    gi = -np.transpose(mi, (0, 2, 1))[:, :n1h, :] / nn
    g1 = np.concatenate([np.concatenate([gr, -gi], 2), np.concatenate([gi, gr], 2)], 1)
    k2 = np.arange(n2c).astype(np.float64)
    a2 = -2.0 * np.pi * k2[:, None] * n2[None, :] / n2c
    fr, fi = np.cos(a2), np.sin(a2)
    f2 = np.block([[fr, -fi], [fi, fr]])
    f2i = np.block([[fr.T, fi.T], [-fi.T, fr.T]])
    f32 = lambda a: np.asarray(a, np.float32)
    return f32(f1_real), f32(f1_cplx), f32(g1), f32(f2), f32(f2i)


def _spec_dense_body(f_ref, x_ref, o_ref):
    o_ref[...] = _dot(f_ref[...], x_ref[...], hi=True)


def hyena_spec_dense(full, n):
    f_real, _, _ = _dense_dft_tables(n)
    nn, cols = full.shape
    return pl.pallas_call(
        _spec_dense_body,
        grid=(cols // LANE,),
        in_specs=[pl.BlockSpec((2 * nn, nn), lambda j: (0, 0)),
                  pl.BlockSpec((nn, LANE), lambda j: (0, j))],
        out_specs=pl.BlockSpec((2 * nn, LANE), lambda j: (0, j)),
        out_shape=jax.ShapeDtypeStruct((2 * nn, cols), F32),
        compiler_params=_cparams(("arbitrary",)),
        name="hyena_spec_dense",
    )(jnp.asarray(f_real), full)


def _conv_dense_body(*refs, has_mult):
    z_ref, h_ref, ff_ref, fi_ref, bias_ref = refs[:5]
    m_ref = refs[5] if has_mult else None
    o_ref = refs[-1]
    z = z_ref[...]
    nn = z.shape[0]
    x = _dot(ff_ref[...], z, hi=True)
    xr, xi = x[:nn], x[nn:]
    hr, hi_ = h_ref[0:nn, :], h_ref[nn:2 * nn, :]
    y = _dot(fi_ref[...], jnp.concatenate([xr * hr - xi * hi_, xr * hi_ + xi * hr], 0), hi=True)
    out = y + z * bias_ref[...]
    if has_mult:
        out = out * m_ref[...]
    o_ref[...] = out


def hyena_conv_dense(zsrc, zcol, row0, n, n_batch, spec, bias3, layer, order, prev, mult=None):
    _, f_fwd, f_inv = _dense_dft_tables(n)
    nn = 2 * n
    rb, cb = row0 // nn, zcol // LANE
    wb = BRANCH_W // LANE
    in_specs = [pl.BlockSpec((nn, LANE), lambda p, j: (rb + p, cb + j)),
                pl.BlockSpec((2 * nn, LANE), lambda p, j: (0, order * wb + j)),
                pl.BlockSpec((2 * nn, nn), lambda p, j: (0, 0)),
                pl.BlockSpec((nn, 2 * nn), lambda p, j: (0, 0)),
                pl.BlockSpec((None, 1, LANE), lambda p, j: (layer * HY_ORDER + order, 0, j))]
    args = [zsrc, spec, jnp.asarray(f_fwd), jnp.asarray(f_inv), bias3]
    if mult is not None:
        mb = mult[1] // LANE
        in_specs.append(pl.BlockSpec((nn, LANE), lambda p, j: (rb + p, mb + j)))
        args.append(mult[0])
    in_specs.append(pl.BlockSpec(memory_space=pl.ANY))
    args.append(prev)
    return pl.pallas_call(
        functools.partial(_conv_dense_body, has_mult=mult is not None),
        grid=(n_batch // 2, wb),
        in_specs=in_specs,
        out_specs=pl.BlockSpec((nn, LANE), lambda p, j: (rb + p, j)),
        out_shape=jax.ShapeDtypeStruct(prev.shape, F32),
        input_output_aliases={len(args) - 1: 0},
        compiler_params=_cparams(("arbitrary", "arbitrary")),
        name="hyena_conv_dense",
    )(*args)


def _spec_fft_body(x_ref, f1_ref, f2_ref, o_ref, a_ref):
    n1c = o_ref.shape[0]

    def stage1(n2, carry):
        xs = x_ref[pl.ds(n2, n1c, stride=FFT_N2), :]
        a_ref[n2] = _dot(f1_ref[n2], xs)
        return carry

    lax.fori_loop(0, FFT_N2, stage1, 0, unroll=FFT_UNROLL)

    def stage2(k1, carry):
        blk = jnp.concatenate([a_ref[:, k1, :], a_ref[:, n1c + k1, :]], 0)
        o_ref[k1] = _dot(f2_ref[...], blk)
        return carry

    lax.fori_loop(0, n1c, stage2, 0, unroll=FFT_UNROLL // 2)


def hyena_spec_fft(full, n):
    f1_real, _, _, f2, _ = _two_stage_dft_tables(n)
    nn, cols = full.shape
    n1c = nn // FFT_N2
    const = lambda shape: pl.BlockSpec(shape, lambda j: (0,) * len(shape), pipeline_mode=pl.Buffered(1))
    return pl.pallas_call(
        _spec_fft_body,
        grid=(cols // LANE,),
        in_specs=[pl.BlockSpec((nn, LANE), lambda j: (0, j)),
                  const((FFT_N2, 2 * n1c, n1c)), const((2 * FFT_N2, 2 * FFT_N2))],
        out_specs=pl.BlockSpec((n1c, 2 * FFT_N2, LANE), lambda j: (0, 0, j)),
        out_shape=jax.ShapeDtypeStruct((n1c, 2 * FFT_N2, cols), F32),
        scratch_shapes=[pltpu.VMEM((FFT_N2, 2 * n1c, LANE), F32)],
        compiler_params=_cparams(("arbitrary",)),
        name="hyena_spec_fft",
    )(full, jnp.asarray(f1_real, BF16), jnp.asarray(f2, BF16))


def _conv_fft_body(*refs, has_mult):
    z_ref, h_ref, f1_ref, f2_ref, f2i_ref, g1_ref, bias_ref = refs[:7]
    m_ref = refs[7] if has_mult else None
    o_ref, a_ref = refs[-2], refs[-1]
    n1c = h_ref.shape[0]
    n1h = n1c // 2
    n2c = FFT_N2
    n = n1h * n2c

    def stage1(n2, carry):
        xs = jnp.concatenate([z_ref[pl.ds(n2, n1h, stride=n2c), :], z_ref[pl.ds(n + n2, n1h, stride=n2c), :]], 0)
        a_ref[n2] = _dot(f1_ref[n2], xs)
        return carry

    lax.fori_loop(0, n2c, stage1, 0, unroll=FFT_UNROLL)

    def stage2(k1, carry):
        x = _dot(f2_ref[...], jnp.concatenate([a_ref[:, k1, :], a_ref[:, n1c + k1, :]], 0))
        xr, xi = x[:n2c], x[n2c:]
        hr, hi_ = h_ref[k1, 0:n2c, :], h_ref[k1, n2c:2 * n2c, :]
        b = _dot(f2i_ref[...], jnp.concatenate([xr * hr - xi * hi_, xr * hi_ + xi * hr], 0))
        a_ref[:, k1, :] = b[:n2c]
        a_ref[:, n1c + k1, :] = b[n2c:]
        return carry

    lax.fori_loop(0, n1c, stage2, 0, unroll=FFT_UNROLL // 2)
    bias = bias_ref[...]

    def stage3(n2, carry):
        y = _dot(g1_ref[n2], a_ref[n2])
        for part, rows in ((y[:n1h], pl.ds(n2, n1h, stride=n2c)), (y[n1h:], pl.ds(n + n2, n1h, stride=n2c))):
            out = part + z_ref[rows, :] * bias
            if has_mult:
                out = out * m_ref[rows, :]
            o_ref[rows, :] = out
        return carry

    lax.fori_loop(0, n2c, stage3, 0, unroll=FFT_UNROLL)


def hyena_conv_fft(zsrc, zcol, n, n_batch, spec, bias3, layer, order, t_rows, mult=None):
    _, f1_cplx, g1, f2, f2i = _two_stage_dft_tables(n)
    n1c = 2 * n // FFT_N2
    cb = zcol // LANE
    wb = BRANCH_W // LANE
    const = lambda shape: pl.BlockSpec(shape, lambda j, p: (0,) * len(shape), pipeline_mode=pl.Buffered(1))
    in_specs = [pl.BlockSpec((2 * n, LANE), lambda j, p: (p, cb + j)),
                pl.BlockSpec((n1c, 2 * FFT_N2, LANE), lambda j, p: (0, 0, order * wb + j),
                             pipeline_mode=pl.Buffered(1)),
                const((FFT_N2, 2 * n1c, n1c)), const((2 * FFT_N2, 2 * FFT_N2)), const((2 * FFT_N2, 2 * FFT_N2)),
                const((FFT_N2, n1c, 2 * n1c)),
                pl.BlockSpec((None, 1, LANE), lambda j, p: (layer * HY_ORDER + order, 0, j))]
    args = [zsrc, spec, jnp.asarray(f1_cplx, BF16), jnp.asarray(f2, BF16), jnp.asarray(f2i, BF16),
            jnp.asarray(g1, BF16), bias3]
    if mult is not None:
        mb = mult[1] // LANE
        in_specs.append(pl.BlockSpec((2 * n, LANE), lambda j, p: (p, mb + j)))
        args.append(mult[0])
    return pl.pallas_call(
        functools.partial(_conv_fft_body, has_mult=mult is not None),
        grid=(wb, n_batch // 2),
        in_specs=in_specs,
        out_specs=pl.BlockSpec((2 * n, LANE), lambda j, p: (p, j)),
        out_shape=jax.ShapeDtypeStruct((t_rows, BRANCH_W), F32),
        scratch_shapes=[pltpu.VMEM((FFT_N2, 2 * n1c, LANE), F32)],
        compiler_params=_cparams(("arbitrary", "arbitrary")),
        name="hyena_conv_fft",
    )(*args)


def _swap_pairs(x):
    w = x.shape[-1]
    lane = lax.broadcasted_iota(jnp.int32, x.shape, x.ndim - 1)
    return jnp.where(lane % 2 == 0, pltpu.roll(x, w - 1, x.ndim - 1), pltpu.roll(x, 1, x.ndim - 1))


def _attn_prep_body(g_ref, dq_ref, dk_ref, dv_ref, cg_ref, sg_ref, cd_ref, sd_ref, qn_ref, kn_ref,
                    qg_ref, kg_ref, vg_ref, qd_ref, kd_ref, vd_ref, *, lat_blocks):
    is_lat = pl.program_id(0) < lat_blocks
    cg = jnp.where(is_lat, cg_ref[...], 1.0)
    sg = jnp.where(is_lat, sg_ref[...], 0.0)
    cd = jnp.where(is_lat, cd_ref[...], 1.0)
    sd = jnp.where(is_lat, sd_ref[...], 0.0)

    def rope(x, cs, sn):
        return x * cs + _swap_pairs(x) * sn

    def rms(x, w):
        return x * lax.rsqrt(jnp.mean(x * x, axis=-1, keepdims=True) + EPS) * w

    for h in range(HEADS):
        sl = slice(h * HEAD_D, (h + 1) * HEAD_D)
        q = rope(rms(g_ref[:, sl], qn_ref[...]), cg, sg)
        qg_ref[:, sl] = (q * HEAD_D ** -0.5).astype(BF16)
        qd_ref[:, sl] = (rope(dq_ref[:, sl], cd, sd) * DIFF_QK ** -0.5).astype(BF16)
        kd_ref[:, sl] = rope(dk_ref[:, sl], cd, sd).astype(BF16)
    for h in range(GQA_KV):
        sl = slice(h * HEAD_D, (h + 1) * HEAD_D)
        kin = g_ref[:, BRANCH_W + h * HEAD_D:BRANCH_W + (h + 1) * HEAD_D]
        kg_ref[:, sl] = rope(rms(kin, kn_ref[...]), cg, sg).astype(BF16)
    vg_ref[...] = g_ref[:, BRANCH_W + GQA_KV * HEAD_D:BRANCH_W + 2 * GQA_KV * HEAD_D].astype(BF16)
    vd_ref[...] = dv_ref[...].astype(BF16)


def attn_prep(p, ropes, qn3, kn3, layer, n_lat, n_ctx, n_batch):
    t = p.shape[0]
    r = 256 if n_ctx % 256 == 0 else n_ctx
    nlb, ncb = n_lat // r, n_ctx // r
    lat_blocks = n_batch * nlb
    kvw = GQA_KV * HEAD_D
    w = BRANCH_W

    def kv_row(i):
        lat = (i // nlb) * (nlb + ncb) + ncb + i % nlb
        j = i - lat_blocks
        ctx = (j // ncb) * (nlb + ncb) + j % ncb
        return jnp.where(i < lat_blocks, lat, ctx)

    rope_spec = pl.BlockSpec((r, LANE), lambda i: (jnp.where(i < lat_blocks, i % nlb, 0), 0))
    nkv = n_batch * (n_lat + n_ctx)
    return pl.pallas_call(
        functools.partial(_attn_prep_body, lat_blocks=lat_blocks),
        grid=(t // r,),
        in_specs=[pl.BlockSpec((r, 2 * w), lambda i: (i, C_GQA_QKV // (2 * w))),
                  pl.BlockSpec((r, w), lambda i: (i, C_DIFF_Q // w)),
                  pl.BlockSpec((r, w), lambda i: (i, C_DIFF_K // w)),
                  pl.BlockSpec((r, w), lambda i: (i, C_DIFF_V // w)),
                  rope_spec, rope_spec, rope_spec, rope_spec,
                  pl.BlockSpec((None, 1, LANE), lambda i: (layer, 0, 0)),
                  pl.BlockSpec((None, 1, LANE), lambda i: (layer, 0, 0))],
        out_specs=[pl.BlockSpec((r, w), lambda i: (i, 0)),
                   pl.BlockSpec((r, kvw), lambda i: (kv_row(i), 0)),
                   pl.BlockSpec((r, kvw), lambda i: (kv_row(i), 0)),
                   pl.BlockSpec((r, w), lambda i: (i, 0)),
                   pl.BlockSpec((r, w), lambda i: (kv_row(i), 0)),
                   pl.BlockSpec((r, w), lambda i: (kv_row(i), 0))],
        out_shape=[jax.ShapeDtypeStruct((t, w), BF16), jax.ShapeDtypeStruct((nkv, kvw), BF16),
                   jax.ShapeDtypeStruct((nkv, kvw), BF16), jax.ShapeDtypeStruct((t, w), BF16),
                   jax.ShapeDtypeStruct((nkv, w), BF16), jax.ShapeDtypeStruct((nkv, w), BF16)],
        compiler_params=_cparams(("arbitrary",)),
        name="attn_prep",
    )(p, p, p, p, *ropes, qn3, kn3)


def _softmax_parts(s):
    e = jnp.exp(s - jnp.max(s, axis=-1, keepdims=True))
    return e, jnp.sum(e, axis=-1, keepdims=True)


def _gqa_body(q_ref, k_ref, v_ref, *rest):
    o_ref = rest[-1]
    group = HEADS // GQA_KV
    for kvh in range(GQA_KV):
        k = k_ref[:, kvh * HEAD_D:(kvh + 1) * HEAD_D]
        v = v_ref[:, kvh * HEAD_D:(kvh + 1) * HEAD_D]
        for g in range(group):
            sl = slice((kvh * group + g) * HEAD_D, (kvh * group + g + 1) * HEAD_D)
            s = lax.dot_general(q_ref[:, sl], k, (((1,), (1,)), ((), ())), preferred_element_type=F32)
            e, l = _softmax_parts(s)
            o_ref[:, sl] = jnp.dot(e.astype(BF16), v, preferred_element_type=F32) / l


def _diff_body(q_ref, k_ref, v_ref, lam_ref, *rest, lam_init):
    o_ref = rest[-1]
    lam4 = lam_ref[...]
    lam = (jnp.exp(jnp.sum(lam4[0:1] * lam4[1:2], axis=-1, keepdims=True))
           - jnp.exp(jnp.sum(lam4[2:3] * lam4[3:4], axis=-1, keepdims=True)) + lam_init)
    dn = (((1,), (1,)), ((), ()))
    for h in range(HEADS):
        sl = slice(h * HEAD_D, (h + 1) * HEAD_D)
        q = q_ref[:, sl]
        k = k_ref[:, sl]
        v = v_ref[:, sl]
        first = lax.broadcasted_iota(jnp.int32, q.shape, 1) < DIFF_QK
        zero = jnp.zeros_like(q)
        e1, l1 = _softmax_parts(lax.dot_general(jnp.where(first, q, zero), k, dn, preferred_element_type=F32))
        e2, l2 = _softmax_parts(lax.dot_general(jnp.where(first, zero, q), k, dn, preferred_element_type=F32))
        o1 = jnp.dot(e1.astype(BF16), v, preferred_element_type=F32) / l1
        o2 = jnp.dot(e2.astype(BF16), v, preferred_element_type=F32) / l2
        o_ref[:, sl] = o1 - lam * o2


def attention(body, q, k, v, extra, extra_specs, q_row0, nq, kv_per_batch, kv_len, n_batch, tq, name, prev=None):
    t, w = q.shape
    qb0 = q_row0 // tq
    nqb = nq // tq
    kvb = kv_per_batch // kv_len
    in_specs = [pl.BlockSpec((tq, w), lambda b, i: (qb0 + b * nqb + i, 0)),
                pl.BlockSpec((kv_len, k.shape[1]), lambda b, i: (b * kvb, 0)),
                pl.BlockSpec((kv_len, v.shape[1]), lambda b, i: (b * kvb, 0))] + extra_specs
    args = [q, k, v, *extra]
    aliases = {}
    if prev is not None:
        in_specs.append(pl.BlockSpec(memory_space=pl.ANY))
        args.append(prev)
        aliases = {len(args) - 1: 0}
    return pl.pallas_call(
        body,
        grid=(n_batch, nqb),
        in_specs=in_specs,
        out_specs=pl.BlockSpec((tq, w), lambda b, i: (qb0 + b * nqb + i, 0)),
        out_shape=jax.ShapeDtypeStruct((t, w), F32),
        input_output_aliases=aliases,
        compiler_params=_cparams(("arbitrary", "arbitrary")),
        name=name,
    )(*args)


def _merge_body(h_ref, mod_ref, mg_ref, of_ref, ob_ref, ggate_ref, y1_ref, x2_ref, hgate_ref, oc_ref, cgate_ref,
                od_ref, dgate_ref, gnorm_ref, dnorm_ref, wbr_ref, wout_ref, lng_ref, lnb_ref, o_ref, *, diff_scale):
    def rms_heads(x, w):
        parts = []
        for h in range(HEADS):
            xh = x[:, h * HEAD_D:(h + 1) * HEAD_D]
            parts.append(xh * lax.rsqrt(jnp.mean(xh * xh, axis=-1, keepdims=True) + EPS) * w)
        return jnp.concatenate(parts, -1)

    ys = (rms_heads(of_ref[...] + ob_ref[...], gnorm_ref[...]) * _silu(ggate_ref[...]),
          x2_ref[...] * y1_ref[...] * _silu(hgate_ref[...]),
          oc_ref[...] * _silu(cgate_ref[...]),
          rms_heads(od_ref[...], dnorm_ref[...]) * diff_scale * _silu(dgate_ref[...]))
    acc = None
    for n in range(N_BRANCH):
        proj = jnp.dot(ys[n].astype(BF16), wbr_ref[n], preferred_element_type=F32)
        term = _sigmoid(mg_ref[:, n * D_MODEL:(n + 1) * D_MODEL]) * proj
        acc = term if acc is None else acc + term
    out = jnp.dot(acc.astype(BF16), wout_ref[...], preferred_element_type=F32)
    x = ALPHA * h_ref[...] + mod_ref[2:3, :] * out
    mu = jnp.mean(x, axis=-1, keepdims=True)
    xc = x - mu
    var = jnp.mean(xc * xc, axis=-1, keepdims=True)
    o_ref[...] = xc * lax.rsqrt(var + EPS) * lng_ref[...] + lnb_ref[...]


def merge_postnorm(h_all, mod3, p, o_f, o_b, y1, xv, oc, od, gnorm3, dnorm3, wbr, wout, lng3, lnb3, layer, lam_init,
                   n_lat, n_batch):
    t, d = h_all.shape
    r = 256 if n_lat % 256 == 0 else 64
    w = BRANCH_W
    lbb = n_lat // r
    row = lambda i: jnp.minimum(i // lbb, n_batch)
    tok = lambda cb: pl.BlockSpec((r, w), lambda i: (i, cb))
    vec = lambda width: pl.BlockSpec((None, 1, width), lambda i: (layer, 0, 0))
    return pl.pallas_call(
        functools.partial(_merge_body, diff_scale=1.0 - lam_init),
        grid=(t // r,),
        in_specs=[pl.BlockSpec((r, d), lambda i: (i, 0)),
                  pl.BlockSpec((None, 3, d), lambda i: (row(i), 0, 0)),
                  pl.BlockSpec((r, N_BRANCH * d), lambda i: (i, C_MERGE // (N_BRANCH * d))),
                  tok(0), tok(0), tok(C_GDN_GATE // w), tok(0), tok(1), tok(C_HY_GATE // w), tok(0),
                  tok(C_GQA_GATE // w), tok(0), tok(C_DIFF_GATE // w),
                  vec(LANE), vec(LANE),
                  pl.BlockSpec((None, N_BRANCH, w, d), lambda i: (layer, 0, 0, 0)),
                  pl.BlockSpec((None, d, d), lambda i: (layer, 0, 0)),
                  vec(d), vec(d)],
        out_specs=pl.BlockSpec((r, d), lambda i: (i, 0)),
        out_shape=jax.ShapeDtypeStruct((t, d), F32),
        compiler_params=_cparams(("arbitrary",)),
        name="merge_postnorm",
    )(h_all, mod3, p, o_f, o_b, p, y1, xv, p, oc, p, od, p, gnorm3, dnorm3, wbr, wout, lng3, lnb3)


def _rope_tables(n_lat, dim):
    rows = n_lat // GRID_W
    row = jnp.repeat(jnp.arange(rows, dtype=F32), GRID_W)
    col = jnp.tile(jnp.arange(GRID_W, dtype=F32), rows)
    half = dim // 2
    inv = ROPE_THETA ** (-jnp.arange(0, half, 2, dtype=F32) / half)
    ang = jnp.concatenate([row[:, None] * inv, col[:, None] * inv], -1)
    cos = jnp.repeat(jnp.cos(ang), 2, axis=-1)
    sin = jnp.repeat(jnp.sin(ang), 2, axis=-1)
    sign = jnp.tile(jnp.array([-1.0, 1.0], F32), dim // 2)
    reps = LANE // dim
    return jnp.tile(cos, (1, reps)), jnp.tile(sin * sign, (1, reps))


def kernel(x, c, ctx, c_ctx, w_ada, b_ada, w_in, gdn_conv, gdn_a_log, gdn_dt_bias, gdn_norm, hy_conv, hy_w1, hy_b1,
           hy_w2, hy_b2, hy_w3, hy_b3, hy_w4, hy_freq, hy_bias, gqa_qn, gqa_kn, diff_lam, diff_norm, w_br, w_out,
           ln_g, ln_b):
    nb, n_lat, d = x.shape
    n_ctx = ctx.shape[1]
    t_lat, t_ctx = nb * n_lat, nb * n_ctx
    depth = w_in.shape[0]
    w = BRANCH_W

    w_main = jnp.concatenate([w_in[:, :, O_MERGE:], w_in[:, :, :O_GDN_AB], w_in[:, :, O_GDN_AB + 4 * HEADS:O_MERGE]],
                             axis=2).astype(BF16)
    w_ab = jnp.pad(w_in[:, :, O_GDN_AB:O_GDN_AB + 4 * HEADS], ((0, 0), (0, 0), (0, LANE - 4 * HEADS)))
    wbr_bf = w_br.astype(BF16)
    wout_bf = w_out.astype(BF16)
    b_ada3 = b_ada[:, None, :]
    cvec = jnp.concatenate([c, c_ctx[None, :], jnp.zeros((SUB - nb - 1, d), F32)], 0)
    as3 = lambda a: a[:, None, :]
    gdn_par_r = jnp.pad(jnp.stack([gdn_a_log.reshape(depth, -1), gdn_dt_bias.reshape(depth, -1)], 1),
                        ((0, 0), (0, SUB - 2), (0, LANE - 2 * HEADS)))
    gdn_par_c = jnp.pad(jnp.stack([gdn_a_log.reshape(depth, -1), gdn_dt_bias.reshape(depth, -1)], 2),
                        ((0, 0), (0, 2 * HEADS), (0, LANE - 2)))
    hy_w1p = jnp.pad(hy_w1, ((0, 0), (0, LANE - HY_EMB), (0, 0)))
    hy_bias3 = hy_bias.reshape(depth * HY_ORDER, 1, w)
    ropes = _rope_tables(n_lat, HEAD_D) + _rope_tables(n_lat, DIFF_QK)

    tm = 1024 if (n_lat % 1024 == 0 and t_ctx % 1024 == 0) else n_ctx
    h_all = jnp.concatenate([x.reshape(t_lat, d), ctx.reshape(t_ctx, d)], 0)
    for l in range(depth):
        lam_init = 0.8 - 0.6 * math.exp(-0.3 * l)
        mod3 = ada_mod(cvec, w_ada, b_ada3, l).reshape(SUB, 3, d)
        p, ab = in_proj(h_all, mod3, w_main, w_ab, l, tm, n_lat // tm, nb)

        qkv = dwconv(p, gdn_conv, l, C_GDN_QKV, 3 * w, n_lat, n_ctx, nb, act=True)
        ab_rows = jnp.transpose(ab[:, :4 * HEADS].reshape(-1, GDN_CHUNK, 4 * HEADS), (0, 2, 1))
        o_f, o_b = gdn_scan(qkv, ab, ab_rows, gdn_par_r[l], gdn_par_c[l], n_lat, n_ctx, nb)

        xv = dwconv(p, hy_conv, l, C_HY_XV, 3 * w, n_lat, n_ctx, nb, act=False)
        filt = lambda n: hyena_filter(n, hy_w1p, as3(hy_b1), hy_w2, as3(hy_b2), hy_w3, as3(hy_b3), hy_w4,
                                      as3(hy_freq), l)
        spec_lat = hyena_spec_fft(filt(n_lat), n_lat)
        spec_ctx = hyena_spec_dense(filt(n_ctx), n_ctx)
        z1 = hyena_conv_fft(xv, 2 * w, n_lat, nb, spec_lat, hy_bias3, l, 0, t_lat + t_ctx, mult=(xv, 0))
        z1 = hyena_conv_dense(xv, 2 * w, t_lat, n_ctx, nb, spec_ctx, hy_bias3, l, 0, z1, mult=(xv, 0))
        y1 = hyena_conv_fft(z1, 0, n_lat, nb, spec_lat, hy_bias3, l, 1, t_lat + t_ctx)
        y1 = hyena_conv_dense(z1, 0, t_lat, n_ctx, nb, spec_ctx, hy_bias3, l, 1, y1)

        qg, kg, vg, qd, kd, vd = attn_prep(p, ropes, as3(gqa_qn), as3(gqa_kn), l, n_lat, n_ctx, nb)
        kv_all = n_lat + n_ctx
        tq = min(256, n_ctx)
        lam_spec = [pl.BlockSpec((None, 4, DIFF_QK), lambda b, i: (l, 0, 0))]
        diff_body = functools.partial(_diff_body, lam_init=lam_init)
        oc = attention(_gqa_body, qg, kg, vg, (), [], 0, n_lat, kv_all, kv_all, nb, tq, "gqa_lat")
        oc = attention(_gqa_body, qg, kg, vg, (), [], t_lat, n_ctx, kv_all, n_ctx, nb, tq, "gqa_ctx", prev=oc)
        od = attention(diff_body, qd, kd, vd, (diff_lam,), lam_spec, 0, n_lat, kv_all, kv_all, nb, tq, "diff_lat")
        od = attention(diff_body, qd, kd, vd, (diff_lam,), lam_spec, t_lat, n_ctx, kv_all, n_ctx, nb, tq, "diff_ctx",
                       prev=od)

        h_all = merge_postnorm(h_all, mod3, p, o_f, o_b, y1, xv, oc, od, as3(gdn_norm), as3(diff_norm), wbr_bf, wout_bf,
                               as3(ln_g), as3(ln_b), l, lam_init, n_lat, nb)
    return h_all[:t_lat].reshape(nb, n_lat, d)
```

```python
import functools
import math

import numpy as np
import jax
import jax.numpy as jnp
from jax import lax
from jax.experimental import pallas as pl
from jax.experimental.pallas import tpu as pltpu

F32 = jnp.float32
BF16 = jnp.bfloat16
HI = lax.Precision.HIGHEST

D_MODEL = 1024
DEPTH = 4
GRID_W = 64
BRANCH_W = D_MODEL // 2
N_BRANCH = 4
HEADS = 4
HEAD_D = BRANCH_W // HEADS
GDN_CONV = 4
GDN_CHUNK = 64
HY_CONV = 3
HY_EMB = 33
HY_BANDS = (HY_EMB - 1) // 2
HY_FH = 64
HY_ORDER = 2
HY_MIN_DECAY = math.log(1e-2) / 1.5
HY_MAX_DECAY = math.log(1e-2) / 0.3
GQA_KV = 2
DIFF_QK = HEAD_D // 2
ROPE_THETA = 10000.0
EPS = 1e-6
ALPHA = (2.0 * DEPTH) ** 0.25

LANE = 128
SUB = 8
FFT_N2 = 128
FFT_UNROLL = 8
VMEM_LIMIT = 60 * 1024 * 1024

C_MERGE = 0
C_GDN_QKV = 4096
C_GDN_GATE = 5632
C_HY_XV = 6144
C_HY_GATE = 7680
C_GQA_QKV = 8192
C_GQA_GATE = 9216
C_DIFF_Q = 9728
C_DIFF_K = 10240
C_DIFF_V = 10752
C_DIFF_GATE = 11264
N_MAIN = 11776
O_GDN_AB = 1536
O_MERGE = 7696


def _cparams(sem):
    return pltpu.CompilerParams(dimension_semantics=sem, vmem_limit_bytes=VMEM_LIMIT)


def _dot(a, b, hi=False):
    if hi:
        return jnp.dot(a, b, precision=HI, preferred_element_type=F32)
    return jnp.dot(a.astype(BF16), b.astype(BF16), preferred_element_type=F32)


def _dot_nt(a, b, hi=False):
    dn = (((1,), (1,)), ((), ()))
    if hi:
        return lax.dot_general(a, b, dn, precision=HI, preferred_element_type=F32)
    return lax.dot_general(a.astype(BF16), b.astype(BF16), dn, preferred_element_type=F32)


def _dot_tn(a, b):
    return lax.dot_general(a.astype(BF16), b.astype(BF16), (((0,), (0,)), ((), ())), preferred_element_type=F32)


def _sigmoid(x):
    return 1.0 / (1.0 + jnp.exp(-x))


def _silu(x):
    return x * _sigmoid(x)


def _softplus(x):
    return jnp.maximum(x, 0.0) + jnp.log1p(jnp.exp(-jnp.abs(x)))


def _ada_body(c_ref, w_ref, b_ref, o_ref):
    o_ref[...] = _dot(_silu(c_ref[...]), w_ref[...], hi=True) + b_ref[...]


def ada_mod(cvec, w_ada, b_ada3, layer):
    d = cvec.shape[1]
    tn = 512
    return pl.pallas_call(
        _ada_body,
        grid=(3 * d // tn,),
        in_specs=[pl.BlockSpec((SUB, d), lambda j: (0, 0)),
                  pl.BlockSpec((None, d, tn), lambda j: (layer, 0, j)),
                  pl.BlockSpec((None, 1, tn), lambda j: (layer, 0, j))],
        out_specs=pl.BlockSpec((SUB, tn), lambda j: (0, j)),
        out_shape=jax.ShapeDtypeStruct((SUB, 3 * d), F32),
        compiler_params=_cparams(("arbitrary",)),
        name="ada_mod",
    )(cvec, w_ada, b_ada3)


def _inproj_body(h_ref, mod_ref, w_ref, wab_ref, o_ref, ab_ref, u_ref):
    @pl.when(pl.program_id(1) == 0)
    def _():
        x = h_ref[...]
        mu = jnp.mean(x, axis=-1, keepdims=True)
        xc = x - mu
        var = jnp.mean(xc * xc, axis=-1, keepdims=True)
        u = xc * lax.rsqrt(var + EPS) * (1.0 + mod_ref[1:2, :]) + mod_ref[0:1, :]
        u_ref[...] = u.astype(BF16)
        ab_ref[...] = _dot(u, wab_ref[...], hi=True)

    o_ref[...] = jnp.dot(u_ref[...], w_ref[...], preferred_element_type=F32)


def in_proj(h_all, mod3, w_main, w_ab, layer, tm, lat_blocks_per_batch, n_batch):
    t, d = h_all.shape
    tn = 512
    n_main = w_main.shape[2]
    row = lambda i: jnp.minimum(i // lat_blocks_per_batch, n_batch)
    return pl.pallas_call(
        _inproj_body,
        grid=(t // tm, n_main // tn),
        in_specs=[pl.BlockSpec((tm, d), lambda i, j: (i, 0)),
                  pl.BlockSpec((None, 3, d), lambda i, j: (row(i), 0, 0)),
                  pl.BlockSpec((None, d, tn), lambda i, j: (layer, 0, j)),
                  pl.BlockSpec((None, d, LANE), lambda i, j: (layer, 0, 0))],
        out_specs=[pl.BlockSpec((tm, tn), lambda i, j: (i, j)),
                   pl.BlockSpec((tm, LANE), lambda i, j: (i, 0))],
        out_shape=[jax.ShapeDtypeStruct((t, n_main), F32), jax.ShapeDtypeStruct((t, LANE), F32)],
        scratch_shapes=[pltpu.VMEM((tm, d), BF16)],
        compiler_params=_cparams(("arbitrary", "arbitrary")),
        name="in_proj",
    )(h_all, mod3, w_main, w_ab)


def _dwconv_body(xp_ref, x_ref, xn_ref, w_ref, o_ref, pad_ref, *, taps, pad_l, lat_blocks, bl, bc, act):
    i = pl.program_id(0)
    r = x_ref.shape[0]
    is_lat = i < lat_blocks
    pos = jnp.where(is_lat, i % bl, (i - lat_blocks) % bc)
    last = jnp.where(is_lat, bl - 1, bc - 1)
    pad_ref[0:SUB, :] = jnp.where(pos == 0, 0.0, xp_ref[...])
    pad_ref[SUB:SUB + r, :] = x_ref[...]
    pad_ref[SUB + r:2 * SUB + r, :] = jnp.where(pos == last, 0.0, xn_ref[...])
    acc = None
    for j in range(taps):
        off = SUB + j - pad_l
        term = w_ref[j:j + 1, :] * pad_ref[off:off + r, :]
        acc = term if acc is None else acc + term
    if act:
        acc = _silu(acc)
    o_ref[...] = acc


def dwconv(p, w_conv, layer, col0, width, n_lat, n_ctx, n_batch, act):
    t = p.shape[0]
    taps = w_conv.shape[1]
    r = 256 if n_ctx % 256 == 0 else n_ctx
    lw = 512
    cb = col0 // lw
    rs = r // SUB
    body = functools.partial(_dwconv_body, taps=taps, pad_l=(taps - 1) // 2, lat_blocks=n_batch * n_lat // r,
                             bl=n_lat // r, bc=n_ctx // r, act=act)
    return pl.pallas_call(
        body,
        grid=(t // r, width // lw),
        in_specs=[pl.BlockSpec((SUB, lw), lambda i, j: (jnp.maximum(i * rs - 1, 0), cb + j)),
                  pl.BlockSpec((r, lw), lambda i, j: (i, cb + j)),
                  pl.BlockSpec((SUB, lw), lambda i, j: (jnp.minimum((i + 1) * rs, t // SUB - 1), cb + j)),
                  pl.BlockSpec((None, taps, lw), lambda i, j: (layer, 0, j))],
        out_specs=pl.BlockSpec((r, lw), lambda i, j: (i, j)),
        out_shape=jax.ShapeDtypeStruct((t, width), F32),
        scratch_shapes=[pltpu.VMEM((r + 2 * SUB, lw), F32)],
        compiler_params=_cparams(("arbitrary", "arbitrary")),
        name="dwconv",
    )(p, p, p, w_conv)


def _gdn_body(qf_ref, qb_ref, abcf_ref, abcb_ref, abrf_ref, abrb_ref, pr_ref, pc_ref, of_ref, ob_ref, s_ref):
    c = GDN_CHUNK

    @pl.when(pl.program_id(1) == 0)
    def _():
        s_ref[...] = jnp.zeros_like(s_ref)

    ii = lax.broadcasted_iota(jnp.int32, (c, c), 0)
    jj = lax.broadcasted_iota(jnp.int32, (c, c), 1)
    lmat = (jj <= ii).astype(F32)
    eye = (jj == ii).astype(F32)
    alr, dtr = pr_ref[0:1, :], pr_ref[1:2, :]
    alc, dtc = pc_ref[:, 0:1], pc_ref[:, 1:2]
    chains = []
    for d in range(2):
        qkv_ref = (qf_ref, qb_ref)[d]
        abc = (abcf_ref, abcb_ref)[d][...]
        abr = (abrf_ref, abrb_ref)[d][...]
        g_c = -jnp.exp(alr) * _softplus(abc + dtr)
        g_r = -jnp.exp(alc) * _softplus(abr + dtc)
        cum_c = _dot(lmat, g_c, hi=True)
        cum_r = _dot_nt(g_r, lmat, hi=True)
        if d == 1:
            cum_c = cum_c[c - 1:c, :] - cum_c + g_c
            cum_r = cum_r[:, c - 1:c] - cum_r + g_r
        beta_all = _sigmoid(abc)
        incl = (jj <= ii) if d == 0 else (jj >= ii)
        strict = (jj < ii) if d == 0 else (jj > ii)
        for h in range(HEADS):
            idx = HEADS * d + h
            q = qkv_ref[:, h * HEAD_D:(h + 1) * HEAD_D]
            k = qkv_ref[:, BRANCH_W + h * HEAD_D:BRANCH_W + (h + 1) * HEAD_D]
            v = qkv_ref[:, 2 * BRANCH_W + h * HEAD_D:2 * BRANCH_W + (h + 1) * HEAD_D]
            q = q * lax.rsqrt(jnp.sum(q * q, axis=-1, keepdims=True) + EPS) * (HEAD_D ** -0.5)
            k = k * lax.rsqrt(jnp.sum(k * k, axis=-1, keepdims=True) + EPS)
            cc = cum_c[:, idx:idx + 1]
            cr = cum_r[idx:idx + 1, :]
            dec = jnp.exp(jnp.where(incl, cc - cr, -1e30))
            beta = beta_all[:, 2 * HEADS + idx:2 * HEADS + idx + 1]
            ecum = jnp.exp(cc)
            tot = cc[c - 1:c, :] if d == 0 else cc[0:1, :]
            chains.append(dict(d=d, h=h, q=q, k=k, dec=dec, strict=strict, beta=beta, ecum=ecum, tot=tot,
                               rhs=jnp.concatenate([k * (beta * ecum), v * beta], 1),
                               k_tail=k * jnp.exp(tot - cc)))
    for ch in chains:
        ch["kk"] = _dot_nt(ch["k"], ch["k"])
        ch["qk"] = _dot_nt(ch["q"], ch["k"])
    for ch in chains:
        ch["p"] = -jnp.where(ch["strict"], ch["beta"] * ch["kk"] * ch["dec"], 0.0)
        ch["inv"] = eye + ch["p"]
    for _ in range(int(math.log2(c)) - 1):
        for ch in chains:
            ch["p"] = _dot(ch["p"], ch["p"])
        for ch in chains:
            ch["inv"] = ch["inv"] + _dot(ch["inv"], ch["p"])
    for ch in chains:
        ch["wu"] = _dot(ch["inv"], ch["rhs"])
    for ch in chains:
        ch["s"] = s_ref[ch["d"], ch["h"]]
        ch["ws"] = _dot(jnp.concatenate([ch["wu"][:, :HEAD_D], ch["q"] * ch["ecum"]], 0), ch["s"])
    for ch in chains:
        ch["v_new"] = ch["wu"][:, HEAD_D:] - ch["ws"][:c]
    for ch in chains:
        out_ref = (of_ref, ob_ref)[ch["d"]]
        h = ch["h"]
        out_ref[:, h * HEAD_D:(h + 1) * HEAD_D] = ch["ws"][c:] + _dot(ch["qk"] * ch["dec"], ch["v_new"])
        s_ref[ch["d"], h] = ch["s"] * jnp.exp(ch["tot"]) + _dot_tn(ch["k_tail"], ch["v_new"])


def gdn_scan(qkv, ab, ab_rows, par_r, par_c, n_lat, n_ctx, n_batch):
    t = qkv.shape[0]
    c = GDN_CHUNK
    nlc, ncc = n_lat // c, n_ctx // c
    base = n_batch * nlc

    def fwd(b, s):
        return jnp.where(s < ncc, base + b * ncc + s, b * nlc + (s - ncc))

    def bwd(b, s):
        return jnp.where(s < ncc, base + b * ncc + (ncc - 1 - s), b * nlc + (nlc - 1 - (s - ncc)))

    w3 = 3 * BRANCH_W
    return pl.pallas_call(
        _gdn_body,
        grid=(n_batch, ncc + nlc),
        in_specs=[pl.BlockSpec((c, w3), lambda b, s: (fwd(b, s), 0)),
                  pl.BlockSpec((c, w3), lambda b, s: (bwd(b, s), 0)),
                  pl.BlockSpec((c, LANE), lambda b, s: (fwd(b, s), 0)),
                  pl.BlockSpec((c, LANE), lambda b, s: (bwd(b, s), 0)),
                  pl.BlockSpec((None, 4 * HEADS, c), lambda b, s: (fwd(b, s), 0, 0)),
                  pl.BlockSpec((None, 4 * HEADS, c), lambda b, s: (bwd(b, s), 0, 0)),
                  pl.BlockSpec((SUB, LANE), lambda b, s: (0, 0)),
                  pl.BlockSpec((4 * HEADS, LANE), lambda b, s: (0, 0))],
        out_specs=[pl.BlockSpec((c, BRANCH_W), lambda b, s: (fwd(b, s), 0)),
                   pl.BlockSpec((c, BRANCH_W), lambda b, s: (bwd(b, s), 0))],
        out_shape=[jax.ShapeDtypeStruct((t, BRANCH_W), F32), jax.ShapeDtypeStruct((t, BRANCH_W), F32)],
        scratch_shapes=[pltpu.VMEM((2, HEADS, HEAD_D, HEAD_D), F32)],
        compiler_params=_cparams(("arbitrary", "arbitrary")),
        name="gdn_scan",
    )(qkv, qkv, ab, ab, ab_rows, ab_rows, par_r, par_c)


def _hyfilt_body(z_ref, aux_ref, w1_ref, b1_ref, w2_ref, b2_ref, w3_ref, b3_ref, w4_ref, fr_ref, dl_ref, o_ref):
    fr = fr_ref[...]
    h = jnp.sin(fr * (_dot(z_ref[...], w1_ref[...], hi=True) + b1_ref[...]))
    h = jnp.sin(fr * (_dot(h, w2_ref[...], hi=True) + b2_ref[...]))
    h = jnp.sin(fr * (_dot(h, w3_ref[...], hi=True) + b3_ref[...]))
    taps = _dot(h, w4_ref[...], hi=True) * jnp.exp(-aux_ref[:, 0:1] * dl_ref[...])
    w = BRANCH_W
    negative = aux_ref[:, 1:2] > 0.5
    keep = aux_ref[:, 2:3]
    for o in range(HY_ORDER):
        fwd = taps[:, o * 2 * w:o * 2 * w + w]
        bwd = taps[:, o * 2 * w + w:(o + 1) * 2 * w]
        o_ref[:, o * w:(o + 1) * w] = jnp.where(negative, bwd, fwd) * keep


def hyena_filter(n, w1p, b1, w2, b2, w3, b3, w4, fr, layer):
    row = jnp.arange(2 * n)
    src = jnp.where(row <= n, row, 2 * n - row)
    pos = jnp.where(row == n, 0, src).astype(F32)
    tt = pos / max(n - 1, 1)
    ang = (2.0 * math.pi / n) * pos[:, None] * jnp.linspace(1e-4, HY_BANDS - 1, HY_BANDS, dtype=F32)
    z = jnp.concatenate([tt[:, None], jnp.cos(ang), -jnp.sin(ang), jnp.zeros((2 * n, LANE - HY_EMB), F32)], -1)
    aux = jnp.stack([tt, (row > n).astype(F32), (row != n).astype(F32)], 1)
    aux = jnp.pad(aux, ((0, 0), (0, SUB - 3)))
    deltas = jnp.abs(jnp.linspace(HY_MIN_DECAY, HY_MAX_DECAY, BRANCH_W, dtype=F32))
    dl = jnp.tile(deltas, 2 * HY_ORDER)[None, :]
    r = 512
    wo = 2 * HY_ORDER * BRANCH_W
    full = lambda shape: pl.BlockSpec((None,) + shape, lambda i: (layer,) + (0,) * len(shape))
    return pl.pallas_call(
        _hyfilt_body,
        grid=(2 * n // r,),
        in_specs=[pl.BlockSpec((r, LANE), lambda i: (i, 0)),
                  pl.BlockSpec((r, SUB), lambda i: (i, 0)),
                  full((LANE, HY_FH)), full((1, HY_FH)), full((HY_FH, HY_FH)), full((1, HY_FH)),
                  full((HY_FH, HY_FH)), full((1, HY_FH)), full((HY_FH, wo)), full((1, HY_FH)),
                  pl.BlockSpec((1, wo), lambda i: (0, 0))],
        out_specs=pl.BlockSpec((r, HY_ORDER * BRANCH_W), lambda i: (i, 0)),
        out_shape=jax.ShapeDtypeStruct((2 * n, HY_ORDER * BRANCH_W), F32),
        compiler_params=_cparams(("arbitrary",)),
        name="hyena_filter",
    )(z, aux, w1p, b1, w2, b2, w3, b3, w4, fr, dl)


@functools.lru_cache(maxsize=None)
def _dense_dft_tables(n):
    nn = 2 * n
    k = np.arange(nn)[:, None].astype(np.float64)
    m = np.arange(nn)[None, :].astype(np.float64)
    ang = -2.0 * np.pi * k * m / nn
    wr, wi = np.cos(ang), np.sin(ang)
    f_real = np.concatenate([wr, wi], 0)
    wr_h, wi_h = wr[:, :n], wi[:, :n]
    f_fwd = np.block([[wr_h, -wi_h], [wi_h, wr_h]])
    cr, ci = wr.T[:n] / nn, -wi.T[:n] / nn
    f_inv = np.block([[cr, -ci], [ci, cr]])
    return (np.asarray(f_real, np.float32), np.asarray(f_fwd, np.float32), np.asarray(f_inv, np.float32))


@functools.lru_cache(maxsize=None)
def _two_stage_dft_tables(n):
    nn = 2 * n
    n2c = FFT_N2
    n1c = nn // n2c
    n1h = n1c // 2
    k1 = np.arange(n1c).astype(np.float64)
    n1 = np.arange(n1c).astype(np.float64)
    n2 = np.arange(n2c).astype(np.float64)
    ang = -2.0 * np.pi * (k1[None, :, None] * n1[None, None, :] / n1c + n2[:, None, None] * k1[None, :, None] / nn)
    mr, mi = np.cos(ang), np.sin(ang)
    f1_real = np.concatenate([mr, mi], 1)
    mrh, mih = mr[:, :, :n1h], mi[:, :, :n1h]
    f1_cplx = np.concatenate([np.concatenate([mrh, -mih], 2), np.concatenate([mih, mrh], 2)], 1)
    gr = np.transpose(mr, (0, 2, 1))[:, :n1h, :] / nn
    gi = -np.transpose(mi, (0, 2, 1))[:, :n1h, :] / nn
    g1 = np.concatenate([np.concatenate([gr, -gi], 2), np.concatenate([gi, gr], 2)], 1)
    k2 = np.arange(n2c).astype(np.float64)
    a2 = -2.0 * np.pi * k2[:, None] * n2[None, :] / n2c
    fr, fi = np.cos(a2), np.sin(a2)
    f2 = np.block([[fr, -fi], [fi, fr]])
    f2i = np.block([[fr.T, fi.T], [-fi.T, fr.T]])
    f32 = lambda a: np.asarray(a, np.float32)
    return f32(f1_real), f32(f1_cplx), f32(g1), f32(f2), f32(f2i)


def _spec_dense_body(f_ref, x_ref, o_ref):
    o_ref[...] = _dot(f_ref[...], x_ref[...], hi=True)


def hyena_spec_dense(full, n):
    f_real, _, _ = _dense_dft_tables(n)
    nn, cols = full.shape
    return pl.pallas_call(
        _spec_dense_body,
        grid=(cols // LANE,),
        in_specs=[pl.BlockSpec((2 * nn, nn), lambda j: (0, 0)),
                  pl.BlockSpec((nn, LANE), lambda j: (0, j))],
        out_specs=pl.BlockSpec((2 * nn, LANE), lambda j: (0, j)),
        out_shape=jax.ShapeDtypeStruct((2 * nn, cols), F32),
        compiler_params=_cparams(("arbitrary",)),
        name="hyena_spec_dense",
    )(jnp.asarray(f_real), full)


def _conv_dense_body(*refs, has_mult):
    z_ref, h_ref, ff_ref, fi_ref, bias_ref = refs[:5]
    m_ref = refs[5] if has_mult else None
    o_ref = refs[-1]
    z = z_ref[...]
    nn = z.shape[0]
    x = _dot(ff_ref[...], z, hi=True)
    xr, xi = x[:nn], x[nn:]
    hr, hi_ = h_ref[0:nn, :], h_ref[nn:2 * nn, :]
    y = _dot(fi_ref[...], jnp.concatenate([xr * hr - xi * hi_, xr * hi_ + xi * hr], 0), hi=True)
    out = y + z * bias_ref[...]
    if has_mult:
        out = out * m_ref[...]
    o_ref[...] = out


def hyena_conv_dense(zsrc, zcol, row0, n, n_batch, spec, bias3, layer, order, prev, mult=None):
    _, f_fwd, f_inv = _dense_dft_tables(n)
    nn = 2 * n
    rb, cb = row0 // nn, zcol // LANE
    wb = BRANCH_W // LANE
    in_specs = [pl.BlockSpec((nn, LANE), lambda p, j: (rb + p, cb + j)),
                pl.BlockSpec((2 * nn, LANE), lambda p, j: (0, order * wb + j)),
                pl.BlockSpec((2 * nn, nn), lambda p, j: (0, 0)),
                pl.BlockSpec((nn, 2 * nn), lambda p, j: (0, 0)),
                pl.BlockSpec((None, 1, LANE), lambda p, j: (layer * HY_ORDER + order, 0, j))]
    args = [zsrc, spec, jnp.asarray(f_fwd), jnp.asarray(f_inv), bias3]
    if mult is not None:
        mb = mult[1] // LANE
        in_specs.append(pl.BlockSpec((nn, LANE), lambda p, j: (rb + p, mb + j)))
        args.append(mult[0])
    in_specs.append(pl.BlockSpec(memory_space=pl.ANY))
    args.append(prev)
    return pl.pallas_call(
        functools.partial(_conv_dense_body, has_mult=mult is not None),
        grid=(n_batch // 2, wb),
        in_specs=in_specs,
        out_specs=pl.BlockSpec((nn, LANE), lambda p, j: (rb + p, j)),
        out_shape=jax.ShapeDtypeStruct(prev.shape, F32),
        input_output_aliases={len(args) - 1: 0},
        compiler_params=_cparams(("arbitrary", "arbitrary")),
        name="hyena_conv_dense",
    )(*args)


def _spec_fft_body(x_ref, f1_ref, f2_ref, o_ref, a_ref):
    n1c = o_ref.shape[0]

    def stage1(n2, carry):
        xs = x_ref[pl.ds(n2, n1c, stride=FFT_N2), :]
        a_ref[pl.ds(pl.multiple_of(n2 * 2 * n1c, 2 * n1c), 2 * n1c), :] = _dot(f1_ref[n2], xs)
        return carry

    lax.fori_loop(0, FFT_N2, stage1, 0, unroll=FFT_UNROLL)

    def stage2(k1, carry):
        blk = jnp.concatenate([a_ref[pl.ds(k1, FFT_N2, stride=2 * n1c), :],
                               a_ref[pl.ds(n1c + k1, FFT_N2, stride=2 * n1c), :]], 0)
        o_ref[k1] = _dot(f2_ref[...], blk)
        return carry

    lax.fori_loop(0, n1c, stage2, 0, unroll=FFT_UNROLL // 2)


def hyena_spec_fft(full, n):
    f1_real, _, _, f2, _ = _two_stage_dft_tables(n)
    nn, cols = full.shape
    n1c = nn // FFT_N2
    const = lambda shape: pl.BlockSpec(shape, lambda j: (0,) * len(shape), pipeline_mode=pl.Buffered(1))
    return pl.pallas_call(
        _spec_fft_body,
        grid=(cols // LANE,),
        in_specs=[pl.BlockSpec((nn, LANE), lambda j: (0, j)),
                  const((FFT_N2, 2 * n1c, n1c)), const((2 * FFT_N2, 2 * FFT_N2))],
        out_specs=pl.BlockSpec((n1c, 2 * FFT_N2, LANE), lambda j: (0, 0, j)),
        out_shape=jax.ShapeDtypeStruct((n1c, 2 * FFT_N2, cols), F32),
        scratch_shapes=[pltpu.VMEM((FFT_N2 * 2 * n1c, LANE), F32)],
        compiler_params=_cparams(("arbitrary",)),
        name="hyena_spec_fft",
    )(full, jnp.asarray(f1_real, BF16), jnp.asarray(f2, BF16))


def _conv_fft_body(*refs, has_mult):
    z_ref, h_ref, f1_ref, f2_ref, f2i_ref, g1_ref, bias_ref = refs[:7]
    m_ref = refs[7] if has_mult else None
    o_ref, a_ref = refs[-2], refs[-1]
    n1c = h_ref.shape[0]
    n1h = n1c // 2
    n2c = FFT_N2
    n = n1h * n2c

    def stage1(n2, carry):
        xs = jnp.concatenate([z_ref[pl.ds(n2, n1h, stride=n2c), :], z_ref[pl.ds(n + n2, n1h, stride=n2c), :]], 0)
        a_ref[pl.ds(pl.multiple_of(n2 * 2 * n1c, 2 * n1c), 2 * n1c), :] = _dot(f1_ref[n2], xs)
        return carry

    lax.fori_loop(0, n2c, stage1, 0, unroll=FFT_UNROLL)

    def stage2(k1, carry):
        re_rows = pl.ds(k1, n2c, stride=2 * n1c)
        im_rows = pl.ds(n1c + k1, n2c, stride=2 * n1c)
        x = _dot(f2_ref[...], jnp.concatenate([a_ref[re_rows, :], a_ref[im_rows, :]], 0))
        xr, xi = x[:n2c], x[n2c:]
        hr, hi_ = h_ref[k1, 0:n2c, :], h_ref[k1, n2c:2 * n2c, :]
        b = _dot(f2i_ref[...], jnp.concatenate([xr * hr - xi * hi_, xr * hi_ + xi * hr], 0))
        a_ref[re_rows, :] = b[:n2c]
        a_ref[im_rows, :] = b[n2c:]
        return carry

    lax.fori_loop(0, n1c, stage2, 0, unroll=FFT_UNROLL // 2)
    bias = bias_ref[...]

    def stage3(n2, carry):
        y = _dot(g1_ref[n2], a_ref[pl.ds(pl.multiple_of(n2 * 2 * n1c, 2 * n1c), 2 * n1c), :])
        for part, rows in ((y[:n1h], pl.ds(n2, n1h, stride=n2c)), (y[n1h:], pl.ds(n + n2, n1h, stride=n2c))):
            out = part + z_ref[rows, :] * bias
            if has_mult:
                out = out * m_ref[rows, :]
            o_ref[rows, :] = out
        return carry

    lax.fori_loop(0, n2c, stage3, 0, unroll=FFT_UNROLL)


def hyena_conv_fft(zsrc, zcol, n, n_batch, spec, bias3, layer, order, t_rows, mult=None):
    _, f1_cplx, g1, f2, f2i = _two_stage_dft_tables(n)
    n1c = 2 * n // FFT_N2
    cb = zcol // LANE
    wb = BRANCH_W // LANE
    const = lambda shape: pl.BlockSpec(shape, lambda j, p: (0,) * len(shape), pipeline_mode=pl.Buffered(1))
    in_specs = [pl.BlockSpec((2 * n, LANE), lambda j, p: (p, cb + j)),
                pl.BlockSpec((n1c, 2 * FFT_N2, LANE), lambda j, p: (0, 0, order * wb + j),
                             pipeline_mode=pl.Buffered(1)),
                const((FFT_N2, 2 * n1c, n1c)), const((2 * FFT_N2, 2 * FFT_N2)), const((2 * FFT_N2, 2 * FFT_N2)),
                const((FFT_N2, n1c, 2 * n1c)),
                pl.BlockSpec((None, 1, LANE), lambda j, p: (layer * HY_ORDER + order, 0, j))]
    args = [zsrc, spec, jnp.asarray(f1_cplx, BF16), jnp.asarray(f2, BF16), jnp.asarray(f2i, BF16),
            jnp.asarray(g1, BF16), bias3]
    if mult is not None:
        mb = mult[1] // LANE
        in_specs.append(pl.BlockSpec((2 * n, LANE), lambda j, p: (p, mb + j)))
        args.append(mult[0])
    return pl.pallas_call(
        functools.partial(_conv_fft_body, has_mult=mult is not None),
        grid=(wb, n_batch // 2),
        in_specs=in_specs,
        out_specs=pl.BlockSpec((2 * n, LANE), lambda j, p: (p, j)),
        out_shape=jax.ShapeDtypeStruct((t_rows, BRANCH_W), F32),
        scratch_shapes=[pltpu.VMEM((FFT_N2 * 2 * n1c, LANE), F32)],
        compiler_params=_cparams(("arbitrary", "arbitrary")),
        name="hyena_conv_fft",
    )(*args)


def _swap_pairs(x):
    w = x.shape[-1]
    lane = lax.broadcasted_iota(jnp.int32, x.shape, x.ndim - 1)
    return jnp.where(lane % 2 == 0, pltpu.roll(x, w - 1, x.ndim - 1), pltpu.roll(x, 1, x.ndim - 1))


def _attn_prep_body(g_ref, dq_ref, dk_ref, dv_ref, cg_ref, sg_ref, cd_ref, sd_ref, qn_ref, kn_ref,
                    qg_ref, kg_ref, vg_ref, qd_ref, kd_ref, vd_ref, *, lat_blocks):
    is_lat = pl.program_id(0) < lat_blocks
    cg = jnp.where(is_lat, cg_ref[...], 1.0)
    sg = jnp.where(is_lat, sg_ref[...], 0.0)
    cd = jnp.where(is_lat, cd_ref[...], 1.0)
    sd = jnp.where(is_lat, sd_ref[...], 0.0)

    def rope(x, cs, sn):
        return x * cs + _swap_pairs(x) * sn

    def rms(x, w):
        return x * lax.rsqrt(jnp.mean(x * x, axis=-1, keepdims=True) + EPS) * w

    for h in range(HEADS):
        sl = slice(h * HEAD_D, (h + 1) * HEAD_D)
        q = rope(rms(g_ref[:, sl], qn_ref[...]), cg, sg)
        qg_ref[:, sl] = (q * HEAD_D ** -0.5).astype(BF16)
        qd_ref[:, sl] = (rope(dq_ref[:, sl], cd, sd) * DIFF_QK ** -0.5).astype(BF16)
        kd_ref[:, sl] = rope(dk_ref[:, sl], cd, sd).astype(BF16)
    for h in range(GQA_KV):
        sl = slice(h * HEAD_D, (h + 1) * HEAD_D)
        kin = g_ref[:, BRANCH_W + h * HEAD_D:BRANCH_W + (h + 1) * HEAD_D]
        kg_ref[:, sl] = rope(rms(kin, kn_ref[...]), cg, sg).astype(BF16)
    vg_ref[...] = g_ref[:, BRANCH_W + GQA_KV * HEAD_D:BRANCH_W + 2 * GQA_KV * HEAD_D].astype(BF16)
    vd_ref[...] = dv_ref[...].astype(BF16)


def attn_prep(p, ropes, qn3, kn3, layer, n_lat, n_ctx, n_batch):
    t = p.shape[0]
    r = 256 if n_ctx % 256 == 0 else n_ctx
    nlb, ncb = n_lat // r, n_ctx // r
    lat_blocks = n_batch * nlb
    kvw = GQA_KV * HEAD_D
    w = BRANCH_W

    def kv_row(i):
        lat = (i // nlb) * (nlb + ncb) + ncb + i % nlb
        j = i - lat_blocks
        ctx = (j // ncb) * (nlb + ncb) + j % ncb
        return jnp.where(i < lat_blocks, lat, ctx)

    rope_spec = pl.BlockSpec((r, LANE), lambda i: (jnp.where(i < lat_blocks, i % nlb, 0), 0))
    nkv = n_batch * (n_lat + n_ctx)
    return pl.pallas_call(
        functools.partial(_attn_prep_body, lat_blocks=lat_blocks),
        grid=(t // r,),
        in_specs=[pl.BlockSpec((r, 2 * w), lambda i: (i, C_GQA_QKV // (2 * w))),
                  pl.BlockSpec((r, w), lambda i: (i, C_DIFF_Q // w)),
                  pl.BlockSpec((r, w), lambda i: (i, C_DIFF_K // w)),
                  pl.BlockSpec((r, w), lambda i: (i, C_DIFF_V // w)),
                  rope_spec, rope_spec, rope_spec, rope_spec,
                  pl.BlockSpec((None, 1, LANE), lambda i: (layer, 0, 0)),
                  pl.BlockSpec((None, 1, LANE), lambda i: (layer, 0, 0))],
        out_specs=[pl.BlockSpec((r, w), lambda i: (i, 0)),
                   pl.BlockSpec((r, kvw), lambda i: (kv_row(i), 0)),
                   pl.BlockSpec((r, kvw), lambda i: (kv_row(i), 0)),
                   pl.BlockSpec((r, w), lambda i: (i, 0)),
                   pl.BlockSpec((r, w), lambda i: (kv_row(i), 0)),
                   pl.BlockSpec((r, w), lambda i: (kv_row(i), 0))],
        out_shape=[jax.ShapeDtypeStruct((t, w), BF16), jax.ShapeDtypeStruct((nkv, kvw), BF16),
                   jax.ShapeDtypeStruct((nkv, kvw), BF16), jax.ShapeDtypeStruct((t, w), BF16),
                   jax.ShapeDtypeStruct((nkv, w), BF16), jax.ShapeDtypeStruct((nkv, w), BF16)],
        compiler_params=_cparams(("arbitrary",)),
        name="attn_prep",
    )(p, p, p, p, *ropes, qn3, kn3)


def _softmax_parts(s):
    e = jnp.exp(s - jnp.max(s, axis=-1, keepdims=True))
    return e, jnp.sum(e, axis=-1, keepdims=True)


def _gqa_body(q_ref, k_ref, v_ref, *rest):
    o_ref = rest[-1]
    group = HEADS // GQA_KV
    for kvh in range(GQA_KV):
        k = k_ref[:, kvh * HEAD_D:(kvh + 1) * HEAD_D]
        v = v_ref[:, kvh * HEAD_D:(kvh + 1) * HEAD_D]
        for g in range(group):
            sl = slice((kvh * group + g) * HEAD_D, (kvh * group + g + 1) * HEAD_D)
            s = lax.dot_general(q_ref[:, sl], k, (((1,), (1,)), ((), ())), preferred_element_type=F32)
            e, l = _softmax_parts(s)
            o_ref[:, sl] = jnp.dot(e.astype(BF16), v, preferred_element_type=F32) / l


def _diff_body(q_ref, k_ref, v_ref, lam_ref, *rest, lam_init):
    o_ref = rest[-1]
    lam4 = lam_ref[...]
    lam = (jnp.exp(jnp.sum(lam4[0:1] * lam4[1:2], axis=-1, keepdims=True))
           - jnp.exp(jnp.sum(lam4[2:3] * lam4[3:4], axis=-1, keepdims=True)) + lam_init)
    dn = (((1,), (1,)), ((), ()))
    for h in range(HEADS):
        sl = slice(h * HEAD_D, (h + 1) * HEAD_D)
        q = q_ref[:, sl]
        k = k_ref[:, sl]
        v = v_ref[:, sl]
        first = lax.broadcasted_iota(jnp.int32, q.shape, 1) < DIFF_QK
        zero = jnp.zeros_like(q)
        e1, l1 = _softmax_parts(lax.dot_general(jnp.where(first, q, zero), k, dn, preferred_element_type=F32))
        e2, l2 = _softmax_parts(lax.dot_general(jnp.where(first, zero, q), k, dn, preferred_element_type=F32))
        o1 = jnp.dot(e1.astype(BF16), v, preferred_element_type=F32) / l1
        o2 = jnp.dot(e2.astype(BF16), v, preferred_element_type=F32) / l2
        o_ref[:, sl] = o1 - lam * o2


def attention(body, q, k, v, extra, extra_specs, q_row0, nq, kv_per_batch, kv_len, n_batch, tq, name, prev=None):
    t, w = q.shape
    qb0 = q_row0 // tq
    nqb = nq // tq
    kvb = kv_per_batch // kv_len
    in_specs = [pl.BlockSpec((tq, w), lambda b, i: (qb0 + b * nqb + i, 0)),
                pl.BlockSpec((kv_len, k.shape[1]), lambda b, i: (b * kvb, 0)),
                pl.BlockSpec((kv_len, v.shape[1]), lambda b, i: (b * kvb, 0))] + extra_specs
    args = [q, k, v, *extra]
    aliases = {}
    if prev is not None:
        in_specs.append(pl.BlockSpec(memory_space=pl.ANY))
        args.append(prev)
        aliases = {len(args) - 1: 0}
    return pl.pallas_call(
        body,
        grid=(n_batch, nqb),
        in_specs=in_specs,
        out_specs=pl.BlockSpec((tq, w), lambda b, i: (qb0 + b * nqb + i, 0)),
        out_shape=jax.ShapeDtypeStruct((t, w), F32),
        input_output_aliases=aliases,
        compiler_params=_cparams(("arbitrary", "arbitrary")),
        name=name,
    )(*args)


def _merge_body(h_ref, mod_ref, mg_ref, of_ref, ob_ref, ggate_ref, y1_ref, x2_ref, hgate_ref, oc_ref, cgate_ref,
                od_ref, dgate_ref, gnorm_ref, dnorm_ref, wbr_ref, wout_ref, lng_ref, lnb_ref, o_ref, *, diff_scale):
    def rms_heads(x, w):
        parts = []
        for h in range(HEADS):
            xh = x[:, h * HEAD_D:(h + 1) * HEAD_D]
            parts.append(xh * lax.rsqrt(jnp.mean(xh * xh, axis=-1, keepdims=True) + EPS) * w)
        return jnp.concatenate(parts, -1)

    ys = (rms_heads(of_ref[...] + ob_ref[...], gnorm_ref[...]) * _silu(ggate_ref[...]),
          x2_ref[...] * y1_ref[...] * _silu(hgate_ref[...]),
          oc_ref[...] * _silu(cgate_ref[...]),
          rms_heads(od_ref[...], dnorm_ref[...]) * diff_scale * _silu(dgate_ref[...]))
    acc = None
    for n in range(N_BRANCH):
        proj = jnp.dot(ys[n].astype(BF16), wbr_ref[n], preferred_element_type=F32)
        term = _sigmoid(mg_ref[:, n * D_MODEL:(n + 1) * D_MODEL]) * proj
        acc = term if acc is None else acc + term
    out = jnp.dot(acc.astype(BF16), wout_ref[...], preferred_element_type=F32)
    x = ALPHA * h_ref[...] + mod_ref[2:3, :] * out
    mu = jnp.mean(x, axis=-1, keepdims=True)
    xc = x - mu
    var = jnp.mean(xc * xc, axis=-1, keepdims=True)
    o_ref[...] = xc * lax.rsqrt(var + EPS) * lng_ref[...] + lnb_ref[...]


def merge_postnorm(h_all, mod3, p, o_f, o_b, y1, xv, oc, od, gnorm3, dnorm3, wbr, wout, lng3, lnb3, layer, lam_init,
                   n_lat, n_batch):
    t, d = h_all.shape
    r = 256 if n_lat % 256 == 0 else 64
    w = BRANCH_W
    lbb = n_lat // r
    row = lambda i: jnp.minimum(i // lbb, n_batch)
    tok = lambda cb: pl.BlockSpec((r, w), lambda i: (i, cb))
    vec = lambda width: pl.BlockSpec((None, 1, width), lambda i: (layer, 0, 0))
    return pl.pallas_call(
        functools.partial(_merge_body, diff_scale=1.0 - lam_init),
        grid=(t // r,),
        in_specs=[pl.BlockSpec((r, d), lambda i: (i, 0)),
                  pl.BlockSpec((None, 3, d), lambda i: (row(i), 0, 0)),
                  pl.BlockSpec((r, N_BRANCH * d), lambda i: (i, C_MERGE // (N_BRANCH * d))),
                  tok(0), tok(0), tok(C_GDN_GATE // w), tok(0), tok(1), tok(C_HY_GATE // w), tok(0),
                  tok(C_GQA_GATE // w), tok(0), tok(C_DIFF_GATE // w),
                  vec(LANE), vec(LANE),
                  pl.BlockSpec((None, N_BRANCH, w, d), lambda i: (layer, 0, 0, 0)),
                  pl.BlockSpec((None, d, d), lambda i: (layer, 0, 0)),
                  vec(d), vec(d)],
        out_specs=pl.BlockSpec((r, d), lambda i: (i, 0)),
        out_shape=jax.ShapeDtypeStruct((t, d), F32),
        compiler_params=_cparams(("arbitrary",)),
        name="merge_postnorm",
    )(h_all, mod3, p, o_f, o_b, p, y1, xv, p, oc, p, od, p, gnorm3, dnorm3, wbr, wout, lng3, lnb3)


def _rope_tables(n_lat, dim):
    rows = n_lat // GRID_W
    row = jnp.repeat(jnp.arange(rows, dtype=F32), GRID_W)
    col = jnp.tile(jnp.arange(GRID_W, dtype=F32), rows)
    half = dim // 2
    inv = ROPE_THETA ** (-jnp.arange(0, half, 2, dtype=F32) / half)
    ang = jnp.concatenate([row[:, None] * inv, col[:, None] * inv], -1)
    cos = jnp.repeat(jnp.cos(ang), 2, axis=-1)
    sin = jnp.repeat(jnp.sin(ang), 2, axis=-1)
    sign = jnp.tile(jnp.array([-1.0, 1.0], F32), dim // 2)
    reps = LANE // dim
    return jnp.tile(cos, (1, reps)), jnp.tile(sin * sign, (1, reps))


def kernel(x, c, ctx, c_ctx, w_ada, b_ada, w_in, gdn_conv, gdn_a_log, gdn_dt_bias, gdn_norm, hy_conv, hy_w1, hy_b1,
           hy_w2, hy_b2, hy_w3, hy_b3, hy_w4, hy_freq, hy_bias, gqa_qn, gqa_kn, diff_lam, diff_norm, w_br, w_out,
           ln_g, ln_b):
    nb, n_lat, d = x.shape
    n_ctx = ctx.shape[1]
    t_lat, t_ctx = nb * n_lat, nb * n_ctx
    depth = w_in.shape[0]
    w = BRANCH_W

    w_main = jnp.concatenate([w_in[:, :, O_MERGE:], w_in[:, :, :O_GDN_AB], w_in[:, :, O_GDN_AB + 4 * HEADS:O_MERGE]],
                             axis=2).astype(BF16)
    w_ab = jnp.pad(w_in[:, :, O_GDN_AB:O_GDN_AB + 4 * HEADS], ((0, 0), (0, 0), (0, LANE - 4 * HEADS)))
    wbr_bf = w_br.astype(BF16)
    wout_bf = w_out.astype(BF16)
    b_ada3 = b_ada[:, None, :]
    cvec = jnp.concatenate([c, c_ctx[None, :], jnp.zeros((SUB - nb - 1, d), F32)], 0)
    as3 = lambda a: a[:, None, :]
    gdn_par_r = jnp.pad(jnp.stack([gdn_a_log.reshape(depth, -1), gdn_dt_bias.reshape(depth, -1)], 1),
                        ((0, 0), (0, SUB - 2), (0, LANE - 2 * HEADS)))
    gdn_par_c = jnp.pad(jnp.stack([gdn_a_log.reshape(depth, -1), gdn_dt_bias.reshape(depth, -1)], 2),
                        ((0, 0), (0, 2 * HEADS), (0, LANE - 2)))
    hy_w1p = jnp.pad(hy_w1, ((0, 0), (0, LANE - HY_EMB), (0, 0)))
    hy_bias3 = hy_bias.reshape(depth * HY_ORDER, 1, w)
    ropes = _rope_tables(n_lat, HEAD_D) + _rope_tables(n_lat, DIFF_QK)

    tm = 1024 if (n_lat % 1024 == 0 and t_ctx % 1024 == 0) else n_ctx
    h_all = jnp.concatenate([x.reshape(t_lat, d), ctx.reshape(t_ctx, d)], 0)
    for l in range(depth):
        lam_init = 0.8 - 0.6 * math.exp(-0.3 * l)
        mod3 = ada_mod(cvec, w_ada, b_ada3, l).reshape(SUB, 3, d)
        p, ab = in_proj(h_all, mod3, w_main, w_ab, l, tm, n_lat // tm, nb)

        qkv = dwconv(p, gdn_conv, l, C_GDN_QKV, 3 * w, n_lat, n_ctx, nb, act=True)
        ab_rows = jnp.transpose(ab[:, :4 * HEADS].reshape(-1, GDN_CHUNK, 4 * HEADS), (0, 2, 1))
        o_f, o_b = gdn_scan(qkv, ab, ab_rows, gdn_par_r[l], gdn_par_c[l], n_lat, n_ctx, nb)

        xv = dwconv(p, hy_conv, l, C_HY_XV, 3 * w, n_lat, n_ctx, nb, act=False)
        filt = lambda n: hyena_filter(n, hy_w1p, as3(hy_b1), hy_w2, as3(hy_b2), hy_w3, as3(hy_b3), hy_w4,
                                      as3(hy_freq), l)
        spec_lat = hyena_spec_fft(filt(n_lat), n_lat)
        spec_ctx = hyena_spec_dense(filt(n_ctx), n_ctx)
        z1 = hyena_conv_fft(xv, 2 * w, n_lat, nb, spec_lat, hy_bias3, l, 0, t_lat + t_ctx, mult=(xv, 0))
        z1 = hyena_conv_dense(xv, 2 * w, t_lat, n_ctx, nb, spec_ctx, hy_bias3, l, 0, z1, mult=(xv, 0))
        y1 = hyena_conv_fft(z1, 0, n_lat, nb, spec_lat, hy_bias3, l, 1, t_lat + t_ctx)
        y1 = hyena_conv_dense(z1, 0, t_lat, n_ctx, nb, spec_ctx, hy_bias3, l, 1, y1)

        qg, kg, vg, qd, kd, vd = attn_prep(p, ropes, as3(gqa_qn), as3(gqa_kn), l, n_lat, n_ctx, nb)
        kv_all = n_lat + n_ctx
        tq = min(256, n_ctx)
        lam_spec = [pl.BlockSpec((None, 4, DIFF_QK), lambda b, i: (l, 0, 0))]
        diff_body = functools.partial(_diff_body, lam_init=lam_init)
        oc = attention(_gqa_body, qg, kg, vg, (), [], 0, n_lat, kv_all, kv_all, nb, tq, "gqa_lat")
        oc = attention(_gqa_body, qg, kg, vg, (), [], t_lat, n_ctx, kv_all, n_ctx, nb, tq, "gqa_ctx", prev=oc)
        od = attention(diff_body, qd, kd, vd, (diff_lam,), lam_spec, 0, n_lat, kv_all, kv_all, nb, tq, "diff_lat")
        od = attention(diff_body, qd, kd, vd, (diff_lam,), lam_spec, t_lat, n_ctx, kv_all, n_ctx, nb, tq, "diff_ctx",
                       prev=od)

        h_all = merge_postnorm(h_all, mod3, p, o_f, o_b, y1, xv, oc, od, as3(gdn_norm), as3(diff_norm), wbr_bf, wout_bf,
                               as3(ln_g), as3(ln_b), l, lam_init, n_lat, nb)
    return h_all[:t_lat].reshape(nb, n_lat, d)
```

```python
import functools
import math

import numpy as np
import jax
import jax.numpy as jnp
from jax import lax
from jax.experimental import pallas as pl
from jax.experimental.pallas import tpu as pltpu

F32 = jnp.float32
BF16 = jnp.bfloat16
HI = lax.Precision.HIGHEST

D_MODEL = 1024
DEPTH = 4
GRID_W = 64
BRANCH_W = D_MODEL // 2
N_BRANCH = 4
HEADS = 4
HEAD_D = BRANCH_W // HEADS
GDN_CONV = 4
GDN_CHUNK = 64
HY_CONV = 3
HY_EMB = 33
HY_BANDS = (HY_EMB - 1) // 2
HY_FH = 64
HY_ORDER = 2
HY_MIN_DECAY = math.log(1e-2) / 1.5
HY_MAX_DECAY = math.log(1e-2) / 0.3
GQA_KV = 2
DIFF_QK = HEAD_D // 2
ROPE_THETA = 10000.0
EPS = 1e-6
ALPHA = (2.0 * DEPTH) ** 0.25

LANE = 128
SUB = 8
FFT_N2 = 128
FFT_UNROLL = 8
VMEM_LIMIT = 60 * 1024 * 1024

C_MERGE = 0
C_GDN_QKV = 4096
C_GDN_GATE = 5632
C_HY_XV = 6144
C_HY_GATE = 7680
C_GQA_QKV = 8192
C_GQA_GATE = 9216
C_DIFF_Q = 9728
C_DIFF_K = 10240
C_DIFF_V = 10752
C_DIFF_GATE = 11264
N_MAIN = 11776
O_GDN_AB = 1536
O_MERGE = 7696


def _cparams(sem):
    return pltpu.CompilerParams(dimension_semantics=sem, vmem_limit_bytes=VMEM_LIMIT)


def _dot(a, b, hi=False):
    if hi:
        return jnp.dot(a, b, precision=HI, preferred_element_type=F32)
    return jnp.dot(a.astype(BF16), b.astype(BF16), preferred_element_type=F32)


def _dot_nt(a, b, hi=False):
    dn = (((1,), (1,)), ((), ()))
    if hi:
        return lax.dot_general(a, b, dn, precision=HI, preferred_element_type=F32)
    return lax.dot_general(a.astype(BF16), b.astype(BF16), dn, preferred_element_type=F32)


def _dot_tn(a, b):
    return lax.dot_general(a.astype(BF16), b.astype(BF16), (((0,), (0,)), ((), ())), preferred_element_type=F32)


def _sigmoid(x):
    return 1.0 / (1.0 + jnp.exp(-x))


def _silu(x):
    return x * _sigmoid(x)


def _softplus(x):
    return jnp.maximum(x, 0.0) + jnp.log1p(jnp.exp(-jnp.abs(x)))


def _ada_body(c_ref, w_ref, b_ref, o_ref):
    o_ref[...] = _dot(_silu(c_ref[...]), w_ref[...], hi=True) + b_ref[...]


def ada_mod(cvec, w_ada, b_ada3, layer):
    d = cvec.shape[1]
    tn = 512
    return pl.pallas_call(
        _ada_body,
        grid=(3 * d // tn,),
        in_specs=[pl.BlockSpec((SUB, d), lambda j: (0, 0)),
                  pl.BlockSpec((None, d, tn), lambda j: (layer, 0, j)),
                  pl.BlockSpec((None, 1, tn), lambda j: (layer, 0, j))],
        out_specs=pl.BlockSpec((SUB, tn), lambda j: (0, j)),
        out_shape=jax.ShapeDtypeStruct((SUB, 3 * d), F32),
        compiler_params=_cparams(("arbitrary",)),
        name="ada_mod",
    )(cvec, w_ada, b_ada3)


def _inproj_body(h_ref, mod_ref, w_ref, wab_ref, o_ref, ab_ref, u_ref):
    @pl.when(pl.program_id(1) == 0)
    def _():
        x = h_ref[...]
        mu = jnp.mean(x, axis=-1, keepdims=True)
        xc = x - mu
        var = jnp.mean(xc * xc, axis=-1, keepdims=True)
        u = xc * lax.rsqrt(var + EPS) * (1.0 + mod_ref[1:2, :]) + mod_ref[0:1, :]
        u_ref[...] = u.astype(BF16)
        ab_ref[...] = _dot(u, wab_ref[...], hi=True)

    o_ref[...] = jnp.dot(u_ref[...], w_ref[...], preferred_element_type=F32)


def in_proj(h_all, mod3, w_main, w_ab, layer, tm, lat_blocks_per_batch, n_batch):
    t, d = h_all.shape
    tn = 512
    n_main = w_main.shape[2]
    row = lambda i: jnp.minimum(i // lat_blocks_per_batch, n_batch)
    return pl.pallas_call(
        _inproj_body,
        grid=(t // tm, n_main // tn),
        in_specs=[pl.BlockSpec((tm, d), lambda i, j: (i, 0)),
                  pl.BlockSpec((None, 3, d), lambda i, j: (row(i), 0, 0)),
                  pl.BlockSpec((None, d, tn), lambda i, j: (layer, 0, j)),
                  pl.BlockSpec((None, d, LANE), lambda i, j: (layer, 0, 0))],
        out_specs=[pl.BlockSpec((tm, tn), lambda i, j: (i, j)),
                   pl.BlockSpec((tm, LANE), lambda i, j: (i, 0))],
        out_shape=[jax.ShapeDtypeStruct((t, n_main), F32), jax.ShapeDtypeStruct((t, LANE), F32)],
        scratch_shapes=[pltpu.VMEM((tm, d), BF16)],
        compiler_params=_cparams(("arbitrary", "arbitrary")),
        name="in_proj",
    )(h_all, mod3, w_main, w_ab)


def _dwconv_body(xp_ref, x_ref, xn_ref, w_ref, o_ref, pad_ref, *, taps, pad_l, lat_blocks, bl, bc, act):
    i = pl.program_id(0)
    r = x_ref.shape[0]
    is_lat = i < lat_blocks
    pos = jnp.where(is_lat, i % bl, (i - lat_blocks) % bc)
    last = jnp.where(is_lat, bl - 1, bc - 1)
    pad_ref[0:SUB, :] = jnp.where(pos == 0, 0.0, xp_ref[...])
    pad_ref[SUB:SUB + r, :] = x_ref[...]
    pad_ref[SUB + r:2 * SUB + r, :] = jnp.where(pos == last, 0.0, xn_ref[...])
    acc = None
    for j in range(taps):
        off = SUB + j - pad_l
        term = w_ref[j:j + 1, :] * pad_ref[off:off + r, :]
        acc = term if acc is None else acc + term
    if act:
        acc = _silu(acc)
    o_ref[...] = acc


def dwconv(p, w_conv, layer, col0, width, n_lat, n_ctx, n_batch, act):
    t = p.shape[0]
    taps = w_conv.shape[1]
    r = 256 if n_ctx % 256 == 0 else n_ctx
    lw = 512
    cb = col0 // lw
    rs = r // SUB
    body = functools.partial(_dwconv_body, taps=taps, pad_l=(taps - 1) // 2, lat_blocks=n_batch * n_lat // r,
                             bl=n_lat // r, bc=n_ctx // r, act=act)
    return pl.pallas_call(
        body,
        grid=(t // r, width // lw),
        in_specs=[pl.BlockSpec((SUB, lw), lambda i, j: (jnp.maximum(i * rs - 1, 0), cb + j)),
                  pl.BlockSpec((r, lw), lambda i, j: (i, cb + j)),
                  pl.BlockSpec((SUB, lw), lambda i, j: (jnp.minimum((i + 1) * rs, t // SUB - 1), cb + j)),
                  pl.BlockSpec((None, taps, lw), lambda i, j: (layer, 0, j))],
        out_specs=pl.BlockSpec((r, lw), lambda i, j: (i, j)),
        out_shape=jax.ShapeDtypeStruct((t, width), F32),
        scratch_shapes=[pltpu.VMEM((r + 2 * SUB, lw), F32)],
        compiler_params=_cparams(("arbitrary", "arbitrary")),
        name="dwconv",
    )(p, p, p, w_conv)


def _gdn_body(qf_ref, qb_ref, abcf_ref, abcb_ref, abrf_ref, abrb_ref, pr_ref, pc_ref, of_ref, ob_ref, s_ref):
    c = GDN_CHUNK

    @pl.when(pl.program_id(1) == 0)
    def _():
        s_ref[...] = jnp.zeros_like(s_ref)

    ii = lax.broadcasted_iota(jnp.int32, (c, c), 0)
    jj = lax.broadcasted_iota(jnp.int32, (c, c), 1)
    lmat = (jj <= ii).astype(F32)
    eye = (jj == ii).astype(F32)
    alr, dtr = pr_ref[0:1, :], pr_ref[1:2, :]
    alc, dtc = pc_ref[:, 0:1], pc_ref[:, 1:2]
    chains = []
    for d in range(2):
        qkv_ref = (qf_ref, qb_ref)[d]
        abc = (abcf_ref, abcb_ref)[d][...]
        abr = (abrf_ref, abrb_ref)[d][...]
        g_c = -jnp.exp(alr) * _softplus(abc + dtr)
        g_r = -jnp.exp(alc) * _softplus(abr + dtc)
        cum_c = _dot(lmat, g_c, hi=True)
        cum_r = _dot_nt(g_r, lmat, hi=True)
        if d == 1:
            cum_c = cum_c[c - 1:c, :] - cum_c + g_c
            cum_r = cum_r[:, c - 1:c] - cum_r + g_r
        beta_all = _sigmoid(abc)
        incl = (jj <= ii) if d == 0 else (jj >= ii)
        strict = (jj < ii) if d == 0 else (jj > ii)
        for h in range(HEADS):
            idx = HEADS * d + h
            q = qkv_ref[:, h * HEAD_D:(h + 1) * HEAD_D]
            k = qkv_ref[:, BRANCH_W + h * HEAD_D:BRANCH_W + (h + 1) * HEAD_D]
            v = qkv_ref[:, 2 * BRANCH_W + h * HEAD_D:2 * BRANCH_W + (h + 1) * HEAD_D]
            q = q * lax.rsqrt(jnp.sum(q * q, axis=-1, keepdims=True) + EPS) * (HEAD_D ** -0.5)
            k = k * lax.rsqrt(jnp.sum(k * k, axis=-1, keepdims=True) + EPS)
            cc = cum_c[:, idx:idx + 1]
            cr = cum_r[idx:idx + 1, :]
            dec = jnp.exp(jnp.where(incl, cc - cr, -1e30))
            beta = beta_all[:, 2 * HEADS + idx:2 * HEADS + idx + 1]
            ecum = jnp.exp(cc)
            tot = cc[c - 1:c, :] if d == 0 else cc[0:1, :]
            chains.append(dict(d=d, h=h, q=q, k=k, dec=dec, strict=strict, beta=beta, ecum=ecum, tot=tot,
                               rhs=jnp.concatenate([k * (beta * ecum), v * beta], 1),
                               k_tail=k * jnp.exp(tot - cc)))
    for ch in chains:
        ch["kk"] = _dot_nt(ch["k"], ch["k"])
        ch["qk"] = _dot_nt(ch["q"], ch["k"])
    for ch in chains:
        ch["p"] = -jnp.where(ch["strict"], ch["beta"] * ch["kk"] * ch["dec"], 0.0)
        ch["inv"] = eye + ch["p"]
    for _ in range(int(math.log2(c)) - 1):
        for ch in chains:
            ch["p"] = _dot(ch["p"], ch["p"])
        for ch in chains:
            ch["inv"] = ch["inv"] + _dot(ch["inv"], ch["p"])
    for ch in chains:
        ch["wu"] = _dot(ch["inv"], ch["rhs"])
    for ch in chains:
        ch["s"] = s_ref[ch["d"], ch["h"]]
        ch["ws"] = _dot(jnp.concatenate([ch["wu"][:, :HEAD_D], ch["q"] * ch["ecum"]], 0), ch["s"])
    for ch in chains:
        ch["v_new"] = ch["wu"][:, HEAD_D:] - ch["ws"][:c]
    for ch in chains:
        out_ref = (of_ref, ob_ref)[ch["d"]]
        h = ch["h"]
        out_ref[:, h * HEAD_D:(h + 1) * HEAD_D] = ch["ws"][c:] + _dot(ch["qk"] * ch["dec"], ch["v_new"])
        s_ref[ch["d"], h] = ch["s"] * jnp.exp(ch["tot"]) + _dot_tn(ch["k_tail"], ch["v_new"])


def gdn_scan(qkv, ab, ab_rows, par_r, par_c, n_lat, n_ctx, n_batch):
    t = qkv.shape[0]
    c = GDN_CHUNK
    nlc, ncc = n_lat // c, n_ctx // c
    base = n_batch * nlc

    def fwd(b, s):
        return jnp.where(s < ncc, base + b * ncc + s, b * nlc + (s - ncc))

    def bwd(b, s):
        return jnp.where(s < ncc, base + b * ncc + (ncc - 1 - s), b * nlc + (nlc - 1 - (s - ncc)))

    w3 = 3 * BRANCH_W
    return pl.pallas_call(
        _gdn_body,
        grid=(n_batch, ncc + nlc),
        in_specs=[pl.BlockSpec((c, w3), lambda b, s: (fwd(b, s), 0)),
                  pl.BlockSpec((c, w3), lambda b, s: (bwd(b, s), 0)),
                  pl.BlockSpec((c, LANE), lambda b, s: (fwd(b, s), 0)),
                  pl.BlockSpec((c, LANE), lambda b, s: (bwd(b, s), 0)),
                  pl.BlockSpec((None, 4 * HEADS, c), lambda b, s: (fwd(b, s), 0, 0)),
                  pl.BlockSpec((None, 4 * HEADS, c), lambda b, s: (bwd(b, s), 0, 0)),
                  pl.BlockSpec((SUB, LANE), lambda b, s: (0, 0)),
                  pl.BlockSpec((4 * HEADS, LANE), lambda b, s: (0, 0))],
        out_specs=[pl.BlockSpec((c, BRANCH_W), lambda b, s: (fwd(b, s), 0)),
                   pl.BlockSpec((c, BRANCH_W), lambda b, s: (bwd(b, s), 0))],
        out_shape=[jax.ShapeDtypeStruct((t, BRANCH_W), F32), jax.ShapeDtypeStruct((t, BRANCH_W), F32)],
        scratch_shapes=[pltpu.VMEM((2, HEADS, HEAD_D, HEAD_D), F32)],
        compiler_params=_cparams(("arbitrary", "arbitrary")),
        name="gdn_scan",
    )(qkv, qkv, ab, ab, ab_rows, ab_rows, par_r, par_c)


def _hyfilt_body(z_ref, aux_ref, w1_ref, b1_ref, w2_ref, b2_ref, w3_ref, b3_ref, w4_ref, fr_ref, dl_ref, o_ref):
    fr = fr_ref[...]
    h = jnp.sin(fr * (_dot(z_ref[...], w1_ref[...], hi=True) + b1_ref[...]))
    h = jnp.sin(fr * (_dot(h, w2_ref[...], hi=True) + b2_ref[...]))
    h = jnp.sin(fr * (_dot(h, w3_ref[...], hi=True) + b3_ref[...]))
    taps = _dot(h, w4_ref[...], hi=True) * jnp.exp(-aux_ref[:, 0:1] * dl_ref[...])
    w = BRANCH_W
    negative = aux_ref[:, 1:2] > 0.5
    keep = aux_ref[:, 2:3]
    for o in range(HY_ORDER):
        fwd = taps[:, o * 2 * w:o * 2 * w + w]
        bwd = taps[:, o * 2 * w + w:(o + 1) * 2 * w]
        o_ref[:, o * w:(o + 1) * w] = jnp.where(negative, bwd, fwd) * keep


def hyena_filter(n, w1p, b1, w2, b2, w3, b3, w4, fr, layer):
    row = jnp.arange(2 * n)
    src = jnp.where(row <= n, row, 2 * n - row)
    pos = jnp.where(row == n, 0, src).astype(F32)
    tt = pos / max(n - 1, 1)
    ang = (2.0 * math.pi / n) * pos[:, None] * jnp.linspace(1e-4, HY_BANDS - 1, HY_BANDS, dtype=F32)
    z = jnp.concatenate([tt[:, None], jnp.cos(ang), -jnp.sin(ang), jnp.zeros((2 * n, LANE - HY_EMB), F32)], -1)
    aux = jnp.stack([tt, (row > n).astype(F32), (row != n).astype(F32)], 1)
    aux = jnp.pad(aux, ((0, 0), (0, SUB - 3)))
    deltas = jnp.abs(jnp.linspace(HY_MIN_DECAY, HY_MAX_DECAY, BRANCH_W, dtype=F32))
    dl = jnp.tile(deltas, 2 * HY_ORDER)[None, :]
    r = 512
    wo = 2 * HY_ORDER * BRANCH_W
    full = lambda shape: pl.BlockSpec((None,) + shape, lambda i: (layer,) + (0,) * len(shape))
    return pl.pallas_call(
        _hyfilt_body,
        grid=(2 * n // r,),
        in_specs=[pl.BlockSpec((r, LANE), lambda i: (i, 0)),
                  pl.BlockSpec((r, SUB), lambda i: (i, 0)),
                  full((LANE, HY_FH)), full((1, HY_FH)), full((HY_FH, HY_FH)), full((1, HY_FH)),
                  full((HY_FH, HY_FH)), full((1, HY_FH)), full((HY_FH, wo)), full((1, HY_FH)),
                  pl.BlockSpec((1, wo), lambda i: (0, 0))],
        out_specs=pl.BlockSpec((r, HY_ORDER * BRANCH_W), lambda i: (i, 0)),
        out_shape=jax.ShapeDtypeStruct((2 * n, HY_ORDER * BRANCH_W), F32),
        compiler_params=_cparams(("arbitrary",)),
        name="hyena_filter",
    )(z, aux, w1p, b1, w2, b2, w3, b3, w4, fr, dl)


@functools.lru_cache(maxsize=None)
def _dense_dft_tables(n):
    nn = 2 * n
    k = np.arange(nn)[:, None].astype(np.float64)
    m = np.arange(nn)[None, :].astype(np.float64)
    ang = -2.0 * np.pi * k * m / nn
    wr, wi = np.cos(ang), np.sin(ang)
    f_real = np.concatenate([wr, wi], 0)
    wr_h, wi_h = wr[:, :n], wi[:, :n]
    f_fwd = np.block([[wr_h, -wi_h], [wi_h, wr_h]])
    cr, ci = wr.T[:n] / nn, -wi.T[:n] / nn
    f_inv = np.block([[cr, -ci], [ci, cr]])
    return (np.asarray(f_real, np.float32), np.asarray(f_fwd, np.float32), np.asarray(f_inv, np.float32))


@functools.lru_cache(maxsize=None)
def _two_stage_dft_tables(n):
    nn = 2 * n
    n2c = FFT_N2
    n1c = nn // n2c
    n1h = n1c // 2
    k1 = np.arange(n1c).astype(np.float64)
    n1 = np.arange(n1c).astype(np.float64)
    n2 = np.arange(n2c).astype(np.float64)
    ang = -2.0 * np.pi * (k1[None, :, None] * n1[None, None, :] / n1c + n2[:, None, None] * k1[None, :, None] / nn)
    mr, mi = np.cos(ang), np.sin(ang)
    f1_real = np.concatenate([mr, mi], 1)
    mrh, mih = mr[:, :, :n1h], mi[:, :, :n1h]
    f1_cplx = np.concatenate([np.concatenate([mrh, -mih], 2), np.concatenate([mih, mrh], 2)], 1)
    gr = np.transpose(mr, (0, 2, 1))[:, :n1h, :] / nn
    gi = -np.transpose(mi, (0, 2, 1))[:, :n1h, :] / nn
    g1 = np.concatenate([np.concatenate([gr, -gi], 2), np.concatenate([gi, gr], 2)], 1)
    k2 = np.arange(n2c).astype(np.float64)
    a2 = -2.0 * np.pi * k2[:, None] * n2[None, :] / n2c
    fr, fi = np.cos(a2), np.sin(a2)
    f2 = np.block([[fr, -fi], [fi, fr]])
    f2i = np.block([[fr.T, fi.T], [-fi.T, fr.T]])
    f32 = lambda a: np.asarray(a, np.float32)
    return f32(f1_real), f32(f1_cplx), f32(g1), f32(f2), f32(f2i)


def _spec_dense_body(f_ref, x_ref, o_ref):
    o_ref[...] = _dot(f_ref[...], x_ref[...], hi=True)


def hyena_spec_dense(full, n):
    f_real, _, _ = _dense_dft_tables(n)
    nn, cols = full.shape
    return pl.pallas_call(
        _spec_dense_body,
        grid=(cols // LANE,),
        in_specs=[pl.BlockSpec((2 * nn, nn), lambda j: (0, 0)),
                  pl.BlockSpec((nn, LANE), lambda j: (0, j))],
        out_specs=pl.BlockSpec((2 * nn, LANE), lambda j: (0, j)),
        out_shape=jax.ShapeDtypeStruct((2 * nn, cols), F32),
        compiler_params=_cparams(("arbitrary",)),
        name="hyena_spec_dense",
    )(jnp.asarray(f_real), full)


def _conv_dense_body(*refs, has_mult):
    z_ref, h_ref, ff_ref, fi_ref, bias_ref = refs[:5]
    m_ref = refs[5] if has_mult else None
    o_ref = refs[-1]
    z = z_ref[...]
    nn = z.shape[0]
    x = _dot(ff_ref[...], z, hi=True)
    xr, xi = x[:nn], x[nn:]
    hr, hi_ = h_ref[0:nn, :], h_ref[nn:2 * nn, :]
    y = _dot(fi_ref[...], jnp.concatenate([xr * hr - xi * hi_, xr * hi_ + xi * hr], 0), hi=True)
    out = y + z * bias_ref[...]
    if has_mult:
        out = out * m_ref[...]
    o_ref[...] = out


def hyena_conv_dense(zsrc, zcol, row0, n, n_batch, spec, bias3, layer, order, prev, mult=None):
    _, f_fwd, f_inv = _dense_dft_tables(n)
    nn = 2 * n
    rb, cb = row0 // nn, zcol // LANE
    wb = BRANCH_W // LANE
    in_specs = [pl.BlockSpec((nn, LANE), lambda p, j: (rb + p, cb + j)),
                pl.BlockSpec((2 * nn, LANE), lambda p, j: (0, order * wb + j)),
                pl.BlockSpec((2 * nn, nn), lambda p, j: (0, 0)),
                pl.BlockSpec((nn, 2 * nn), lambda p, j: (0, 0)),
                pl.BlockSpec((None, 1, LANE), lambda p, j: (layer * HY_ORDER + order, 0, j))]
    args = [zsrc, spec, jnp.asarray(f_fwd), jnp.asarray(f_inv), bias3]
    if mult is not None:
        mb = mult[1] // LANE
        in_specs.append(pl.BlockSpec((nn, LANE), lambda p, j: (rb + p, mb + j)))
        args.append(mult[0])
    in_specs.append(pl.BlockSpec(memory_space=pl.ANY))
    args.append(prev)
    return pl.pallas_call(
        functools.partial(_conv_dense_body, has_mult=mult is not None),
        grid=(n_batch // 2, wb),
        in_specs=in_specs,
        out_specs=pl.BlockSpec((nn, LANE), lambda p, j: (rb + p, j)),
        out_shape=jax.ShapeDtypeStruct(prev.shape, F32),
        input_output_aliases={len(args) - 1: 0},
        compiler_params=_cparams(("arbitrary", "arbitrary")),
        name="hyena_conv_dense",
    )(*args)


def _spec_fft_body(x_ref, f1_ref, f2_ref, o_ref, a_ref):
    n1c = o_ref.shape[0]

    def stage1(g, carry):
        n2s = [g * FFT_UNROLL + u for u in range(FFT_UNROLL)]
        xs = [x_ref[pl.ds(n2, n1c, stride=FFT_N2), :] for n2 in n2s]
        res = [_dot(f1_ref[n2], x) for n2, x in zip(n2s, xs)]
        for n2, r in zip(n2s, res):
            a_ref[pl.ds(pl.multiple_of(n2 * 2 * n1c, 2 * n1c), 2 * n1c), :] = r
        return carry

    lax.fori_loop(0, FFT_N2 // FFT_UNROLL, stage1, 0, unroll=2)
    g2 = FFT_UNROLL // 2

    def stage2(g, carry):
        k1s = [g * g2 + u for u in range(g2)]
        blks = [jnp.concatenate([a_ref[pl.ds(k1, FFT_N2, stride=2 * n1c), :],
                                 a_ref[pl.ds(n1c + k1, FFT_N2, stride=2 * n1c), :]], 0) for k1 in k1s]
        res = [_dot(f2_ref[...], blk) for blk in blks]
        for k1, r in zip(k1s, res):
            o_ref[k1] = r
        return carry

    lax.fori_loop(0, n1c // g2, stage2, 0, unroll=2)


def hyena_spec_fft(full, n):
    f1_real, _, _, f2, _ = _two_stage_dft_tables(n)
    nn, cols = full.shape
    n1c = nn // FFT_N2
    const = lambda shape: pl.BlockSpec(shape, lambda j: (0,) * len(shape), pipeline_mode=pl.Buffered(1))
    return pl.pallas_call(
        _spec_fft_body,
        grid=(cols // LANE,),
        in_specs=[pl.BlockSpec((nn, LANE), lambda j: (0, j)),
                  const((FFT_N2, 2 * n1c, n1c)), const((2 * FFT_N2, 2 * FFT_N2))],
        out_specs=pl.BlockSpec((n1c, 2 * FFT_N2, LANE), lambda j: (0, 0, j)),
        out_shape=jax.ShapeDtypeStruct((n1c, 2 * FFT_N2, cols), F32),
        scratch_shapes=[pltpu.VMEM((FFT_N2 * 2 * n1c, LANE), F32)],
        compiler_params=_cparams(("arbitrary",)),
        name="hyena_spec_fft",
    )(full, jnp.asarray(f1_real, BF16), jnp.asarray(f2, BF16))


def _conv_fft_body(*refs, has_mult):
    z_ref, h_ref, f1_ref, f2_ref, f2i_ref, g1_ref, bias_ref = refs[:7]
    m_ref = refs[7] if has_mult else None
    o_ref, a_ref, b_ref = refs[-3], refs[-2], refs[-1]
    n1c = h_ref.shape[0]
    n1h = n1c // 2
    n2c = FFT_N2
    n = n1h * n2c

    def slab(n2):
        return pl.ds(pl.multiple_of(n2 * 2 * n1c, 2 * n1c), 2 * n1c)

    def stage1(g, carry):
        n2s = [g * FFT_UNROLL + u for u in range(FFT_UNROLL)]
        xs = [jnp.concatenate([z_ref[pl.ds(n2, n1h, stride=n2c), :], z_ref[pl.ds(n + n2, n1h, stride=n2c), :]], 0)
              for n2 in n2s]
        res = [_dot(f1_ref[n2], x) for n2, x in zip(n2s, xs)]
        for n2, r in zip(n2s, res):
            a_ref[slab(n2), :] = r
        return carry

    lax.fori_loop(0, n2c // FFT_UNROLL, stage1, 0, unroll=2)
    g2 = FFT_UNROLL // 2

    def stage2(g, carry):
        k1s = [g * g2 + u for u in range(g2)]
        rows = [(pl.ds(k1, n2c, stride=2 * n1c), pl.ds(n1c + k1, n2c, stride=2 * n1c)) for k1 in k1s]
        blks = [jnp.concatenate([a_ref[re, :], a_ref[im, :]], 0) for re, im in rows]
        xs = [_dot(f2_ref[...], blk) for blk in blks]
        ys = []
        for k1, x in zip(k1s, xs):
            xr, xi = x[:n2c], x[n2c:]
            hr, hi_ = h_ref[k1, 0:n2c, :], h_ref[k1, n2c:2 * n2c, :]
            ys.append(jnp.concatenate([xr * hr - xi * hi_, xr * hi_ + xi * hr], 0))
        bs = [_dot(f2i_ref[...], y) for y in ys]
        for (re, im), b in zip(rows, bs):
            b_ref[re, :] = b[:n2c]
            b_ref[im, :] = b[n2c:]
        return carry

    lax.fori_loop(0, n1c // g2, stage2, 0, unroll=2)
    bias = bias_ref[...]

    def stage3(g, carry):
        n2s = [g * FFT_UNROLL + u for u in range(FFT_UNROLL)]
        blks = [b_ref[slab(n2), :] for n2 in n2s]
        ys = [_dot(g1_ref[n2], blk) for n2, blk in zip(n2s, blks)]
        outs = []
        for n2, y in zip(n2s, ys):
            for part, rows in ((y[:n1h], pl.ds(n2, n1h, stride=n2c)), (y[n1h:], pl.ds(n + n2, n1h, stride=n2c))):
                out = part + z_ref[rows, :] * bias
                if has_mult:
                    out = out * m_ref[rows, :]
                outs.append((rows, out))
        for rows, out in outs:
            o_ref[rows, :] = out
        return carry

    lax.fori_loop(0, n2c // FFT_UNROLL, stage3, 0, unroll=2)


def hyena_conv_fft(zsrc, zcol, n, n_batch, spec, bias3, layer, order, t_rows, mult=None):
    _, f1_cplx, g1, f2, f2i = _two_stage_dft_tables(n)
    n1c = 2 * n // FFT_N2
    cb = zcol // LANE
    wb = BRANCH_W // LANE
    const = lambda shape: pl.BlockSpec(shape, lambda j, p: (0,) * len(shape), pipeline_mode=pl.Buffered(1))
    in_specs = [pl.BlockSpec((2 * n, LANE), lambda j, p: (p, cb + j)),
                pl.BlockSpec((n1c, 2 * FFT_N2, LANE), lambda j, p: (0, 0, order * wb + j),
                             pipeline_mode=pl.Buffered(1)),
                const((FFT_N2, 2 * n1c, n1c)), const((2 * FFT_N2, 2 * FFT_N2)), const((2 * FFT_N2, 2 * FFT_N2)),
                const((FFT_N2, n1c, 2 * n1c)),
                pl.BlockSpec((None, 1, LANE), lambda j, p: (layer * HY_ORDER + order, 0, j))]
    args = [zsrc, spec, jnp.asarray(f1_cplx, BF16), jnp.asarray(f2, BF16), jnp.asarray(f2i, BF16),
            jnp.asarray(g1, BF16), bias3]
    if mult is not None:
        mb = mult[1] // LANE
        in_specs.append(pl.BlockSpec((2 * n, LANE), lambda j, p: (p, mb + j)))
        args.append(mult[0])
    return pl.pallas_call(
        functools.partial(_conv_fft_body, has_mult=mult is not None),
        grid=(wb, n_batch // 2),
        in_specs=in_specs,
        out_specs=pl.BlockSpec((2 * n, LANE), lambda j, p: (p, j)),
        out_shape=jax.ShapeDtypeStruct((t_rows, BRANCH_W), F32),
        scratch_shapes=[pltpu.VMEM((FFT_N2 * 2 * n1c, LANE), F32)] * 2,
        compiler_params=_cparams(("arbitrary", "arbitrary")),
        name="hyena_conv_fft",
    )(*args)


def _swap_pairs(x):
    w = x.shape[-1]
    lane = lax.broadcasted_iota(jnp.int32, x.shape, x.ndim - 1)
    return jnp.where(lane % 2 == 0, pltpu.roll(x, w - 1, x.ndim - 1), pltpu.roll(x, 1, x.ndim - 1))


def _attn_prep_body(g_ref, dq_ref, dk_ref, dv_ref, cg_ref, sg_ref, cd_ref, sd_ref, qn_ref, kn_ref,
                    qg_ref, kg_ref, vg_ref, qd_ref, kd_ref, vd_ref, *, lat_blocks):
    is_lat = pl.program_id(0) < lat_blocks
    cg = jnp.where(is_lat, cg_ref[...], 1.0)
    sg = jnp.where(is_lat, sg_ref[...], 0.0)
    cd = jnp.where(is_lat, cd_ref[...], 1.0)
    sd = jnp.where(is_lat, sd_ref[...], 0.0)

    def rope(x, cs, sn):
        return x * cs + _swap_pairs(x) * sn

    def rms(x, w):
        return x * lax.rsqrt(jnp.mean(x * x, axis=-1, keepdims=True) + EPS) * w

    for h in range(HEADS):
        sl = slice(h * HEAD_D, (h + 1) * HEAD_D)
        q = rope(rms(g_ref[:, sl], qn_ref[...]), cg, sg)
        qg_ref[:, sl] = (q * HEAD_D ** -0.5).astype(BF16)
        qd_ref[:, sl] = (rope(dq_ref[:, sl], cd, sd) * DIFF_QK ** -0.5).astype(BF16)
        kd_ref[:, sl] = rope(dk_ref[:, sl], cd, sd).astype(BF16)
    for h in range(GQA_KV):
        sl = slice(h * HEAD_D, (h + 1) * HEAD_D)
        kin = g_ref[:, BRANCH_W + h * HEAD_D:BRANCH_W + (h + 1) * HEAD_D]
        kg_ref[:, sl] = rope(rms(kin, kn_ref[...]), cg, sg).astype(BF16)
    vg_ref[...] = g_ref[:, BRANCH_W + GQA_KV * HEAD_D:BRANCH_W + 2 * GQA_KV * HEAD_D].astype(BF16)
    vd_ref[...] = dv_ref[...].astype(BF16)


def attn_prep(p, ropes, qn3, kn3, layer, n_lat, n_ctx, n_batch):
    t = p.shape[0]
    r = 256 if n_ctx % 256 == 0 else n_ctx
    nlb, ncb = n_lat // r, n_ctx // r
    lat_blocks = n_batch * nlb
    kvw = GQA_KV * HEAD_D
    w = BRANCH_W

    def kv_row(i):
        lat = (i // nlb) * (nlb + ncb) + ncb + i % nlb
        j = i - lat_blocks
        ctx = (j // ncb) * (nlb + ncb) + j % ncb
        return jnp.where(i < lat_blocks, lat, ctx)

    rope_spec = pl.BlockSpec((r, LANE), lambda i: (jnp.where(i < lat_blocks, i % nlb, 0), 0))
    nkv = n_batch * (n_lat + n_ctx)
    return pl.pallas_call(
        functools.partial(_attn_prep_body, lat_blocks=lat_blocks),
        grid=(t // r,),
        in_specs=[pl.BlockSpec((r, 2 * w), lambda i: (i, C_GQA_QKV // (2 * w))),
                  pl.BlockSpec((r, w), lambda i: (i, C_DIFF_Q // w)),
                  pl.BlockSpec((r, w), lambda i: (i, C_DIFF_K // w)),
                  pl.BlockSpec((r, w), lambda i: (i, C_DIFF_V // w)),
                  rope_spec, rope_spec, rope_spec, rope_spec,
                  pl.BlockSpec((None, 1, LANE), lambda i: (layer, 0, 0)),
                  pl.BlockSpec((None, 1, LANE), lambda i: (layer, 0, 0))],
        out_specs=[pl.BlockSpec((r, w), lambda i: (i, 0)),
                   pl.BlockSpec((r, kvw), lambda i: (kv_row(i), 0)),
                   pl.BlockSpec((r, kvw), lambda i: (kv_row(i), 0)),
                   pl.BlockSpec((r, w), lambda i: (i, 0)),
                   pl.BlockSpec((r, w), lambda i: (kv_row(i), 0)),
                   pl.BlockSpec((r, w), lambda i: (kv_row(i), 0))],
        out_shape=[jax.ShapeDtypeStruct((t, w), BF16), jax.ShapeDtypeStruct((nkv, kvw), BF16),
                   jax.ShapeDtypeStruct((nkv, kvw), BF16), jax.ShapeDtypeStruct((t, w), BF16),
                   jax.ShapeDtypeStruct((nkv, w), BF16), jax.ShapeDtypeStruct((nkv, w), BF16)],
        compiler_params=_cparams(("arbitrary",)),
        name="attn_prep",
    )(p, p, p, p, *ropes, qn3, kn3)


def _softmax_parts(s):
    e = jnp.exp(s - jnp.max(s, axis=-1, keepdims=True))
    return e, jnp.sum(e, axis=-1, keepdims=True)


def _gqa_body(q_ref, k_ref, v_ref, *rest):
    o_ref = rest[-1]
    group = HEADS // GQA_KV
    for kvh in range(GQA_KV):
        k = k_ref[:, kvh * HEAD_D:(kvh + 1) * HEAD_D]
        v = v_ref[:, kvh * HEAD_D:(kvh + 1) * HEAD_D]
        for g in range(group):
            sl = slice((kvh * group + g) * HEAD_D, (kvh * group + g + 1) * HEAD_D)
            s = lax.dot_general(q_ref[:, sl], k, (((1,), (1,)), ((), ())), preferred_element_type=F32)
            e, l = _softmax_parts(s)
            o_ref[:, sl] = jnp.dot(e.astype(BF16), v, preferred_element_type=F32) / l


def _diff_body(q_ref, k_ref, v_ref, lam_ref, *rest, lam_init):
    o_ref = rest[-1]
    lam4 = lam_ref[...]
    lam = (jnp.exp(jnp.sum(lam4[0:1] * lam4[1:2], axis=-1, keepdims=True))
           - jnp.exp(jnp.sum(lam4[2:3] * lam4[3:4], axis=-1, keepdims=True)) + lam_init)
    dn = (((1,), (1,)), ((), ()))
    for h in range(HEADS):
        sl = slice(h * HEAD_D, (h + 1) * HEAD_D)
        q = q_ref[:, sl]
        k = k_ref[:, sl]
        v = v_ref[:, sl]
        first = lax.broadcasted_iota(jnp.int32, q.shape, 1) < DIFF_QK
        zero = jnp.zeros_like(q)
        e1, l1 = _softmax_parts(lax.dot_general(jnp.where(first, q, zero), k, dn, preferred_element_type=F32))
        e2, l2 = _softmax_parts(lax.dot_general(jnp.where(first, zero, q), k, dn, preferred_element_type=F32))
        o1 = jnp.dot(e1.astype(BF16), v, preferred_element_type=F32) / l1
        o2 = jnp.dot(e2.astype(BF16), v, preferred_element_type=F32) / l2
        o_ref[:, sl] = o1 - lam * o2


def attention(body, q, k, v, extra, extra_specs, q_row0, nq, kv_per_batch, kv_len, n_batch, tq, name, prev=None):
    t, w = q.shape
    qb0 = q_row0 // tq
    nqb = nq // tq
    kvb = kv_per_batch // kv_len
    in_specs = [pl.BlockSpec((tq, w), lambda b, i: (qb0 + b * nqb + i, 0)),
                pl.BlockSpec((kv_len, k.shape[1]), lambda b, i: (b * kvb, 0)),
                pl.BlockSpec((kv_len, v.shape[1]), lambda b, i: (b * kvb, 0))] + extra_specs
    args = [q, k, v, *extra]
    aliases = {}
    if prev is not None:
        in_specs.append(pl.BlockSpec(memory_space=pl.ANY))
        args.append(prev)
        aliases = {len(args) - 1: 0}
    return pl.pallas_call(
        body,
        grid=(n_batch, nqb),
        in_specs=in_specs,
        out_specs=pl.BlockSpec((tq, w), lambda b, i: (qb0 + b * nqb + i, 0)),
        out_shape=jax.ShapeDtypeStruct((t, w), F32),
        input_output_aliases=aliases,
        compiler_params=_cparams(("arbitrary", "arbitrary")),
        name=name,
    )(*args)


def _merge_body(h_ref, mod_ref, mg_ref, of_ref, ob_ref, ggate_ref, y1_ref, x2_ref, hgate_ref, oc_ref, cgate_ref,
                od_ref, dgate_ref, gnorm_ref, dnorm_ref, wbr_ref, wout_ref, lng_ref, lnb_ref, o_ref, *, diff_scale):
    def rms_heads(x, w):
        parts = []
        for h in range(HEADS):
            xh = x[:, h * HEAD_D:(h + 1) * HEAD_D]
            parts.append(xh * lax.rsqrt(jnp.mean(xh * xh, axis=-1, keepdims=True) + EPS) * w)
        return jnp.concatenate(parts, -1)

    ys = (rms_heads(of_ref[...] + ob_ref[...], gnorm_ref[...]) * _silu(ggate_ref[...]),
          x2_ref[...] * y1_ref[...] * _silu(hgate_ref[...]),
          oc_ref[...] * _silu(cgate_ref[...]),
          rms_heads(od_ref[...], dnorm_ref[...]) * diff_scale * _silu(dgate_ref[...]))
    acc = None
    for n in range(N_BRANCH):
        proj = jnp.dot(ys[n].astype(BF16), wbr_ref[n], preferred_element_type=F32)
        term = _sigmoid(mg_ref[:, n * D_MODEL:(n + 1) * D_MODEL]) * proj
        acc = term if acc is None else acc + term
    out = jnp.dot(acc.astype(BF16), wout_ref[...], preferred_element_type=F32)
    x = ALPHA * h_ref[...] + mod_ref[2:3, :] * out
    mu = jnp.mean(x, axis=-1, keepdims=True)
    xc = x - mu
    var = jnp.mean(xc * xc, axis=-1, keepdims=True)
    o_ref[...] = xc * lax.rsqrt(var + EPS) * lng_ref[...] + lnb_ref[...]


def merge_postnorm(h_all, mod3, p, o_f, o_b, y1, xv, oc, od, gnorm3, dnorm3, wbr, wout, lng3, lnb3, layer, lam_init,
                   n_lat, n_batch):
    t, d = h_all.shape
    r = 256 if n_lat % 256 == 0 else 64
    w = BRANCH_W
    lbb = n_lat // r
    row = lambda i: jnp.minimum(i // lbb, n_batch)
    tok = lambda cb: pl.BlockSpec((r, w), lambda i: (i, cb))
    vec = lambda width: pl.BlockSpec((None, 1, width), lambda i: (layer, 0, 0))
    return pl.pallas_call(
        functools.partial(_merge_body, diff_scale=1.0 - lam_init),
        grid=(t // r,),
        in_specs=[pl.BlockSpec((r, d), lambda i: (i, 0)),
                  pl.BlockSpec((None, 3, d), lambda i: (row(i), 0, 0)),
                  pl.BlockSpec((r, N_BRANCH * d), lambda i: (i, C_MERGE // (N_BRANCH * d))),
                  tok(0), tok(0), tok(C_GDN_GATE // w), tok(0), tok(1), tok(C_HY_GATE // w), tok(0),
                  tok(C_GQA_GATE // w), tok(0), tok(C_DIFF_GATE // w),
                  vec(LANE), vec(LANE),
                  pl.BlockSpec((None, N_BRANCH, w, d), lambda i: (layer, 0, 0, 0)),
                  pl.BlockSpec((None, d, d), lambda i: (layer, 0, 0)),
                  vec(d), vec(d)],
        out_specs=pl.BlockSpec((r, d), lambda i: (i, 0)),
        out_shape=jax.ShapeDtypeStruct((t, d), F32),
        compiler_params=_cparams(("arbitrary",)),
        name="merge_postnorm",
    )(h_all, mod3, p, o_f, o_b, p, y1, xv, p, oc, p, od, p, gnorm3, dnorm3, wbr, wout, lng3, lnb3)


def _rope_tables(n_lat, dim):
    rows = n_lat // GRID_W
    row = jnp.repeat(jnp.arange(rows, dtype=F32), GRID_W)
    col = jnp.tile(jnp.arange(GRID_W, dtype=F32), rows)
    half = dim // 2
    inv = ROPE_THETA ** (-jnp.arange(0, half, 2, dtype=F32) / half)
    ang = jnp.concatenate([row[:, None] * inv, col[:, None] * inv], -1)
    cos = jnp.repeat(jnp.cos(ang), 2, axis=-1)
    sin = jnp.repeat(jnp.sin(ang), 2, axis=-1)
    sign = jnp.tile(jnp.array([-1.0, 1.0], F32), dim // 2)
    reps = LANE // dim
    return jnp.tile(cos, (1, reps)), jnp.tile(sin * sign, (1, reps))


def kernel(x, c, ctx, c_ctx, w_ada, b_ada, w_in, gdn_conv, gdn_a_log, gdn_dt_bias, gdn_norm, hy_conv, hy_w1, hy_b1,
           hy_w2, hy_b2, hy_w3, hy_b3, hy_w4, hy_freq, hy_bias, gqa_qn, gqa_kn, diff_lam, diff_norm, w_br, w_out,
           ln_g, ln_b):
    nb, n_lat, d = x.shape
    n_ctx = ctx.shape[1]
    t_lat, t_ctx = nb * n_lat, nb * n_ctx
    depth = w_in.shape[0]
    w = BRANCH_W

    w_main = jnp.concatenate([w_in[:, :, O_MERGE:], w_in[:, :, :O_GDN_AB], w_in[:, :, O_GDN_AB + 4 * HEADS:O_MERGE]],
                             axis=2).astype(BF16)
    w_ab = jnp.pad(w_in[:, :, O_GDN_AB:O_GDN_AB + 4 * HEADS], ((0, 0), (0, 0), (0, LANE - 4 * HEADS)))
    wbr_bf = w_br.astype(BF16)
    wout_bf = w_out.astype(BF16)
    b_ada3 = b_ada[:, None, :]
    cvec = jnp.concatenate([c, c_ctx[None, :], jnp.zeros((SUB - nb - 1, d), F32)], 0)
    as3 = lambda a: a[:, None, :]
    gdn_par_r = jnp.pad(jnp.stack([gdn_a_log.reshape(depth, -1), gdn_dt_bias.reshape(depth, -1)], 1),
                        ((0, 0), (0, SUB - 2), (0, LANE - 2 * HEADS)))
    gdn_par_c = jnp.pad(jnp.stack([gdn_a_log.reshape(depth, -1), gdn_dt_bias.reshape(depth, -1)], 2),
                        ((0, 0), (0, 2 * HEADS), (0, LANE - 2)))
    hy_w1p = jnp.pad(hy_w1, ((0, 0), (0, LANE - HY_EMB), (0, 0)))
    hy_bias3 = hy_bias.reshape(depth * HY_ORDER, 1, w)
    ropes = _rope_tables(n_lat, HEAD_D) + _rope_tables(n_lat, DIFF_QK)

    tm = 1024 if (n_lat % 1024 == 0 and t_ctx % 1024 == 0) else n_ctx
    h_all = jnp.concatenate([x.reshape(t_lat, d), ctx.reshape(t_ctx, d)], 0)
    for l in range(depth):
        lam_init = 0.8 - 0.6 * math.exp(-0.3 * l)
        mod3 = ada_mod(cvec, w_ada, b_ada3, l).reshape(SUB, 3, d)
        p, ab = in_proj(h_all, mod3, w_main, w_ab, l, tm, n_lat // tm, nb)

        qkv = dwconv(p, gdn_conv, l, C_GDN_QKV, 3 * w, n_lat, n_ctx, nb, act=True)
        ab_rows = jnp.transpose(ab[:, :4 * HEADS].reshape(-1, GDN_CHUNK, 4 * HEADS), (0, 2, 1))
        o_f, o_b = gdn_scan(qkv, ab, ab_rows, gdn_par_r[l], gdn_par_c[l], n_lat, n_ctx, nb)

        xv = dwconv(p, hy_conv, l, C_HY_XV, 3 * w, n_lat, n_ctx, nb, act=False)
        filt = lambda n: hyena_filter(n, hy_w1p, as3(hy_b1), hy_w2, as3(hy_b2), hy_w3, as3(hy_b3), hy_w4,
                                      as3(hy_freq), l)
        spec_lat = hyena_spec_fft(filt(n_lat), n_lat)
        spec_ctx = hyena_spec_dense(filt(n_ctx), n_ctx)
        z1 = hyena_conv_fft(xv, 2 * w, n_lat, nb, spec_lat, hy_bias3, l, 0, t_lat + t_ctx, mult=(xv, 0))
        z1 = hyena_conv_dense(xv, 2 * w, t_lat, n_ctx, nb, spec_ctx, hy_bias3, l, 0, z1, mult=(xv, 0))
        y1 = hyena_conv_fft(z1, 0, n_lat, nb, spec_lat, hy_bias3, l, 1, t_lat + t_ctx)
        y1 = hyena_conv_dense(z1, 0, t_lat, n_ctx, nb, spec_ctx, hy_bias3, l, 1, y1)

        qg, kg, vg, qd, kd, vd = attn_prep(p, ropes, as3(gqa_qn), as3(gqa_kn), l, n_lat, n_ctx, nb)
        kv_all = n_lat + n_ctx
        tq = min(256, n_ctx)
        lam_spec = [pl.BlockSpec((None, 4, DIFF_QK), lambda b, i: (l, 0, 0))]
        diff_body = functools.partial(_diff_body, lam_init=lam_init)
        oc = attention(_gqa_body, qg, kg, vg, (), [], 0, n_lat, kv_all, kv_all, nb, tq, "gqa_lat")
        oc = attention(_gqa_body, qg, kg, vg, (), [], t_lat, n_ctx, kv_all, n_ctx, nb, tq, "gqa_ctx", prev=oc)
        od = attention(diff_body, qd, kd, vd, (diff_lam,), lam_spec, 0, n_lat, kv_all, kv_all, nb, tq, "diff_lat")
        od = attention(diff_body, qd, kd, vd, (diff_lam,), lam_spec, t_lat, n_ctx, kv_all, n_ctx, nb, tq, "diff_ctx",
                       prev=od)

        h_all = merge_postnorm(h_all, mod3, p, o_f, o_b, y1, xv, oc, od, as3(gdn_norm), as3(diff_norm), wbr_bf, wout_bf,
                               as3(ln_g), as3(ln_b), l, lam_init, n_lat, nb)
    return h_all[:t_lat].reshape(nb, n_lat, d)
```

```python
import functools
import math

import numpy as np
import jax
import jax.numpy as jnp
from jax import lax
from jax.experimental import pallas as pl
from jax.experimental.pallas import tpu as pltpu

F32 = jnp.float32
BF16 = jnp.bfloat16
HI = lax.Precision.HIGHEST

D_MODEL = 1024
DEPTH = 4
GRID_W = 64
BRANCH_W = D_MODEL // 2
N_BRANCH = 4
HEADS = 4
HEAD_D = BRANCH_W // HEADS
GDN_CONV = 4
GDN_CHUNK = 64
HY_CONV = 3
HY_EMB = 33
HY_BANDS = (HY_EMB - 1) // 2
HY_FH = 64
HY_ORDER = 2
HY_MIN_DECAY = math.log(1e-2) / 1.5
HY_MAX_DECAY = math.log(1e-2) / 0.3
GQA_KV = 2
DIFF_QK = HEAD_D // 2
ROPE_THETA = 10000.0
EPS = 1e-6
ALPHA = (2.0 * DEPTH) ** 0.25

LANE = 128
SUB = 8
FFT_N2 = 128
FFT_UNROLL = 8
VMEM_LIMIT = 60 * 1024 * 1024

C_MERGE = 0
C_GDN_QKV = 4096
C_GDN_GATE = 5632
C_HY_XV = 6144
C_HY_GATE = 7680
C_GQA_QKV = 8192
C_GQA_GATE = 9216
C_DIFF_Q = 9728
C_DIFF_K = 10240
C_DIFF_V = 10752
C_DIFF_GATE = 11264
N_MAIN = 11776
O_GDN_AB = 1536
O_MERGE = 7696


def _cparams(sem):
    return pltpu.CompilerParams(dimension_semantics=sem, vmem_limit_bytes=VMEM_LIMIT)


def _dot(a, b, hi=False):
    if hi:
        return jnp.dot(a, b, precision=HI, preferred_element_type=F32)
    return jnp.dot(a.astype(BF16), b.astype(BF16), preferred_element_type=F32)


def _dot_nt(a, b, hi=False):
    dn = (((1,), (1,)), ((), ()))
    if hi:
        return lax.dot_general(a, b, dn, precision=HI, preferred_element_type=F32)
    return lax.dot_general(a.astype(BF16), b.astype(BF16), dn, preferred_element_type=F32)


def _dot_tn(a, b):
    return lax.dot_general(a.astype(BF16), b.astype(BF16), (((0,), (0,)), ((), ())), preferred_element_type=F32)


def _sigmoid(x):
    return 1.0 / (1.0 + jnp.exp(-x))


def _silu(x):
    return x * _sigmoid(x)


def _softplus(x):
    return jnp.maximum(x, 0.0) + jnp.log1p(jnp.exp(-jnp.abs(x)))


def _ada_body(c_ref, w_ref, b_ref, o_ref):
    o_ref[...] = _dot(_silu(c_ref[...]), w_ref[...], hi=True) + b_ref[...]


def ada_mod(cvec, w_ada, b_ada3, layer):
    d = cvec.shape[1]
    tn = 512
    return pl.pallas_call(
        _ada_body,
        grid=(3 * d // tn,),
        in_specs=[pl.BlockSpec((SUB, d), lambda j: (0, 0)),
                  pl.BlockSpec((None, d, tn), lambda j: (layer, 0, j)),
                  pl.BlockSpec((None, 1, tn), lambda j: (layer, 0, j))],
        out_specs=pl.BlockSpec((SUB, tn), lambda j: (0, j)),
        out_shape=jax.ShapeDtypeStruct((SUB, 3 * d), F32),
        compiler_params=_cparams(("arbitrary",)),
        name="ada_mod",
    )(cvec, w_ada, b_ada3)


def _inproj_body(h_ref, mod_ref, w_ref, wab_ref, o_ref, ab_ref, u_ref):
    @pl.when(pl.program_id(1) == 0)
    def _():
        x = h_ref[...]
        mu = jnp.mean(x, axis=-1, keepdims=True)
        xc = x - mu
        var = jnp.mean(xc * xc, axis=-1, keepdims=True)
        u = xc * lax.rsqrt(var + EPS) * (1.0 + mod_ref[1:2, :]) + mod_ref[0:1, :]
        u_ref[...] = u.astype(BF16)
        ab_ref[...] = _dot(u, wab_ref[...], hi=True)

    o_ref[...] = jnp.dot(u_ref[...], w_ref[...], preferred_element_type=F32)


def in_proj(h_all, mod3, w_main, w_ab, layer, tm, lat_blocks_per_batch, n_batch):
    t, d = h_all.shape
    tn = 512
    n_main = w_main.shape[2]
    row = lambda i: jnp.minimum(i // lat_blocks_per_batch, n_batch)
    return pl.pallas_call(
        _inproj_body,
        grid=(t // tm, n_main // tn),
        in_specs=[pl.BlockSpec((tm, d), lambda i, j: (i, 0)),
                  pl.BlockSpec((None, 3, d), lambda i, j: (row(i), 0, 0)),
                  pl.BlockSpec((None, d, tn), lambda i, j: (layer, 0, j)),
                  pl.BlockSpec((None, d, LANE), lambda i, j: (layer, 0, 0))],
        out_specs=[pl.BlockSpec((tm, tn), lambda i, j: (i, j)),
                   pl.BlockSpec((tm, LANE), lambda i, j: (i, 0))],
        out_shape=[jax.ShapeDtypeStruct((t, n_main), F32), jax.ShapeDtypeStruct((t, LANE), F32)],
        scratch_shapes=[pltpu.VMEM((tm, d), BF16)],
        compiler_params=_cparams(("arbitrary", "arbitrary")),
        name="in_proj",
    )(h_all, mod3, w_main, w_ab)


def _dwconv_body(xp_ref, x_ref, xn_ref, w_ref, o_ref, pad_ref, *, taps, pad_l, t_lat, n_lat, n_ctx, sb, act):
    i = pl.program_id(0)
    r = x_ref.shape[0]
    pad_ref[0:SUB, :] = xp_ref[...]
    pad_ref[SUB:SUB + r, :] = x_ref[...]
    pad_ref[SUB + r:2 * SUB + r, :] = xn_ref[...]
    row = lax.broadcasted_iota(jnp.int32, (sb, 1), 0)
    for k in range(r // sb):
        g0 = i * r + k * sb
        in_lat = g0 < t_lat
        starts = jnp.where(in_lat, g0 % n_lat == 0, (g0 - t_lat) % n_ctx == 0)
        ends = jnp.where(in_lat, (g0 + sb) % n_lat == 0, (g0 + sb - t_lat) % n_ctx == 0)
        acc = None
        for j in range(taps):
            d = j - pad_l
            off = SUB + k * sb + d
            xs = pad_ref[off:off + sb, :]
            if d < 0:
                xs = jnp.where(jnp.logical_and(starts, row < -d), 0.0, xs)
            elif d > 0:
                xs = jnp.where(jnp.logical_and(ends, row >= sb - d), 0.0, xs)
            term = w_ref[j:j + 1, :] * xs
            acc = term if acc is None else acc + term
        if act:
            acc = _silu(acc)
        o_ref[k * sb:(k + 1) * sb, :] = acc


def dwconv(p, w_conv, layer, col0, width, n_lat, n_ctx, n_batch, act):
    t = p.shape[0]
    taps = w_conv.shape[1]
    sb = min(256, n_ctx)
    r = 1024 if t % 1024 == 0 else sb
    lw = 512
    cb = col0 // lw
    rs = r // SUB
    body = functools.partial(_dwconv_body, taps=taps, pad_l=(taps - 1) // 2, t_lat=n_batch * n_lat, n_lat=n_lat,
                             n_ctx=n_ctx, sb=sb, act=act)
    return pl.pallas_call(
        body,
        grid=(t // r, width // lw),
        in_specs=[pl.BlockSpec((SUB, lw), lambda i, j: (jnp.maximum(i * rs - 1, 0), cb + j)),
                  pl.BlockSpec((r, lw), lambda i, j: (i, cb + j)),
                  pl.BlockSpec((SUB, lw), lambda i, j: (jnp.minimum((i + 1) * rs, t // SUB - 1), cb + j)),
                  pl.BlockSpec((None, taps, lw), lambda i, j: (layer, 0, j))],
        out_specs=pl.BlockSpec((r, lw), lambda i, j: (i, j)),
        out_shape=jax.ShapeDtypeStruct((t, width), F32),
        scratch_shapes=[pltpu.VMEM((r + 2 * SUB, lw), F32)],
        compiler_params=_cparams(("arbitrary", "arbitrary")),
        name="dwconv",
    )(p, p, p, w_conv)


def _gdn_body(qf_ref, qb_ref, abcf_ref, abcb_ref, abrf_ref, abrb_ref, pr_ref, pc_ref, of_ref, ob_ref, s_ref):
    c = GDN_CHUNK

    @pl.when(pl.program_id(1) == 0)
    def _():
        s_ref[...] = jnp.zeros_like(s_ref)

    ii = lax.broadcasted_iota(jnp.int32, (c, c), 0)
    jj = lax.broadcasted_iota(jnp.int32, (c, c), 1)
    lmat = (jj <= ii).astype(F32)
    eye = (jj == ii).astype(F32)
    alr, dtr = pr_ref[0:1, :], pr_ref[1:2, :]
    alc, dtc = pc_ref[:, 0:1], pc_ref[:, 1:2]
    chains = []
    for d in range(2):
        qkv_ref = (qf_ref, qb_ref)[d]
        abc = (abcf_ref, abcb_ref)[d][...]
        abr = (abrf_ref, abrb_ref)[d][...]
        g_c = -jnp.exp(alr) * _softplus(abc + dtr)
        g_r = -jnp.exp(alc) * _softplus(abr + dtc)
        cum_c = _dot(lmat, g_c, hi=True)
        cum_r = _dot_nt(g_r, lmat, hi=True)
        if d == 1:
            cum_c = cum_c[c - 1:c, :] - cum_c + g_c
            cum_r = cum_r[:, c - 1:c] - cum_r + g_r
        beta_all = _sigmoid(abc)
        incl = (jj <= ii) if d == 0 else (jj >= ii)
        strict = (jj < ii) if d == 0 else (jj > ii)
        for h in range(HEADS):
            idx = HEADS * d + h
            q = qkv_ref[:, h * HEAD_D:(h + 1) * HEAD_D]
            k = qkv_ref[:, BRANCH_W + h * HEAD_D:BRANCH_W + (h + 1) * HEAD_D]
            v = qkv_ref[:, 2 * BRANCH_W + h * HEAD_D:2 * BRANCH_W + (h + 1) * HEAD_D]
            q = q * lax.rsqrt(jnp.sum(q * q, axis=-1, keepdims=True) + EPS) * (HEAD_D ** -0.5)
            k = k * lax.rsqrt(jnp.sum(k * k, axis=-1, keepdims=True) + EPS)
            cc = cum_c[:, idx:idx + 1]
            cr = cum_r[idx:idx + 1, :]
            dec = jnp.exp(jnp.where(incl, cc - cr, -1e30))
            beta = beta_all[:, 2 * HEADS + idx:2 * HEADS + idx + 1]
            ecum = jnp.exp(cc)
            tot = cc[c - 1:c, :] if d == 0 else cc[0:1, :]
            chains.append(dict(d=d, h=h, q=q, k=k, dec=dec, strict=strict, beta=beta, ecum=ecum, tot=tot,
                               rhs=jnp.concatenate([k * (beta * ecum), v * beta], 1),
                               k_tail=k * jnp.exp(tot - cc)))
    for ch in chains:
        ch["kk"] = _dot_nt(ch["k"], ch["k"])
        ch["qk"] = _dot_nt(ch["q"], ch["k"])
    for ch in chains:
        ch["p"] = -jnp.where(ch["strict"], ch["beta"] * ch["kk"] * ch["dec"], 0.0)
        ch["inv"] = eye + ch["p"]
    for _ in range(int(math.log2(c)) - 1):
        for ch in chains:
            ch["p"] = _dot(ch["p"], ch["p"])
        for ch in chains:
            ch["inv"] = ch["inv"] + _dot(ch["inv"], ch["p"])
    for ch in chains:
        ch["wu"] = _dot(ch["inv"], ch["rhs"])
    for ch in chains:
        ch["s"] = s_ref[ch["d"], ch["h"]]
        ch["ws"] = _dot(jnp.concatenate([ch["wu"][:, :HEAD_D], ch["q"] * ch["ecum"]], 0), ch["s"])
    for ch in chains:
        ch["v_new"] = ch["wu"][:, HEAD_D:] - ch["ws"][:c]
    for ch in chains:
        out_ref = (of_ref, ob_ref)[ch["d"]]
        h = ch["h"]
        out_ref[:, h * HEAD_D:(h + 1) * HEAD_D] = ch["ws"][c:] + _dot(ch["qk"] * ch["dec"], ch["v_new"])
        s_ref[ch["d"], h] = ch["s"] * jnp.exp(ch["tot"]) + _dot_tn(ch["k_tail"], ch["v_new"])


def gdn_scan(qkv, ab, ab_rows, par_r, par_c, n_lat, n_ctx, n_batch):
    t = qkv.shape[0]
    c = GDN_CHUNK
    nlc, ncc = n_lat // c, n_ctx // c
    base = n_batch * nlc

    def fwd(b, s):
        return jnp.where(s < ncc, base + b * ncc + s, b * nlc + (s - ncc))

    def bwd(b, s):
        return jnp.where(s < ncc, base + b * ncc + (ncc - 1 - s), b * nlc + (nlc - 1 - (s - ncc)))

    w3 = 3 * BRANCH_W
    return pl.pallas_call(
        _gdn_body,
        grid=(n_batch, ncc + nlc),
        in_specs=[pl.BlockSpec((c, w3), lambda b, s: (fwd(b, s), 0)),
                  pl.BlockSpec((c, w3), lambda b, s: (bwd(b, s), 0)),
                  pl.BlockSpec((c, LANE), lambda b, s: (fwd(b, s), 0)),
                  pl.BlockSpec((c, LANE), lambda b, s: (bwd(b, s), 0)),
                  pl.BlockSpec((None, 4 * HEADS, c), lambda b, s: (fwd(b, s), 0, 0)),
                  pl.BlockSpec((None, 4 * HEADS, c), lambda b, s: (bwd(b, s), 0, 0)),
                  pl.BlockSpec((SUB, LANE), lambda b, s: (0, 0)),
                  pl.BlockSpec((4 * HEADS, LANE), lambda b, s: (0, 0))],
        out_specs=[pl.BlockSpec((c, BRANCH_W), lambda b, s: (fwd(b, s), 0)),
                   pl.BlockSpec((c, BRANCH_W), lambda b, s: (bwd(b, s), 0))],
        out_shape=[jax.ShapeDtypeStruct((t, BRANCH_W), F32), jax.ShapeDtypeStruct((t, BRANCH_W), F32)],
        scratch_shapes=[pltpu.VMEM((2, HEADS, HEAD_D, HEAD_D), F32)],
        compiler_params=_cparams(("arbitrary", "arbitrary")),
        name="gdn_scan",
    )(qkv, qkv, ab, ab, ab_rows, ab_rows, par_r, par_c)


def _hyfilt_body(z_ref, aux_ref, w1_ref, b1_ref, w2_ref, b2_ref, w3_ref, b3_ref, w4_ref, fr_ref, dl_ref, o_ref):
    fr = fr_ref[...]
    h = jnp.sin(fr * (_dot(z_ref[...], w1_ref[...], hi=True) + b1_ref[...]))
    h = jnp.sin(fr * (_dot(h, w2_ref[...], hi=True) + b2_ref[...]))
    h = jnp.sin(fr * (_dot(h, w3_ref[...], hi=True) + b3_ref[...]))
    taps = _dot(h, w4_ref[...], hi=True) * jnp.exp(-aux_ref[:, 0:1] * dl_ref[...])
    w = BRANCH_W
    negative = aux_ref[:, 1:2] > 0.5
    keep = aux_ref[:, 2:3]
    for o in range(HY_ORDER):
        fwd = taps[:, o * 2 * w:o * 2 * w + w]
        bwd = taps[:, o * 2 * w + w:(o + 1) * 2 * w]
        o_ref[:, o * w:(o + 1) * w] = jnp.where(negative, bwd, fwd) * keep


def hyena_filter(n, w1p, b1, w2, b2, w3, b3, w4, fr, layer):
    row = jnp.arange(2 * n)
    src = jnp.where(row <= n, row, 2 * n - row)
    pos = jnp.where(row == n, 0, src).astype(F32)
    tt = pos / max(n - 1, 1)
    ang = (2.0 * math.pi / n) * pos[:, None] * jnp.linspace(1e-4, HY_BANDS - 1, HY_BANDS, dtype=F32)
    z = jnp.concatenate([tt[:, None], jnp.cos(ang), -jnp.sin(ang), jnp.zeros((2 * n, LANE - HY_EMB), F32)], -1)
    aux = jnp.stack([tt, (row > n).astype(F32), (row != n).astype(F32)], 1)
    aux = jnp.pad(aux, ((0, 0), (0, SUB - 3)))
    deltas = jnp.abs(jnp.linspace(HY_MIN_DECAY, HY_MAX_DECAY, BRANCH_W, dtype=F32))
    dl = jnp.tile(deltas, 2 * HY_ORDER)[None, :]
    r = 512
    wo = 2 * HY_ORDER * BRANCH_W
    full = lambda shape: pl.BlockSpec((None,) + shape, lambda i: (layer,) + (0,) * len(shape))
    return pl.pallas_call(
        _hyfilt_body,
        grid=(2 * n // r,),
        in_specs=[pl.BlockSpec((r, LANE), lambda i: (i, 0)),
                  pl.BlockSpec((r, SUB), lambda i: (i, 0)),
                  full((LANE, HY_FH)), full((1, HY_FH)), full((HY_FH, HY_FH)), full((1, HY_FH)),
                  full((HY_FH, HY_FH)), full((1, HY_FH)), full((HY_FH, wo)), full((1, HY_FH)),
                  pl.BlockSpec((1, wo), lambda i: (0, 0))],
        out_specs=pl.BlockSpec((r, HY_ORDER * BRANCH_W), lambda i: (i, 0)),
        out_shape=jax.ShapeDtypeStruct((2 * n, HY_ORDER * BRANCH_W), F32),
        compiler_params=_cparams(("arbitrary",)),
        name="hyena_filter",
    )(z, aux, w1p, b1, w2, b2, w3, b3, w4, fr, dl)


@functools.lru_cache(maxsize=None)
def _dense_dft_tables(n):
    nn = 2 * n
    k = np.arange(nn)[:, None].astype(np.float64)
    m = np.arange(nn)[None, :].astype(np.float64)
    ang = -2.0 * np.pi * k * m / nn
    wr, wi = np.cos(ang), np.sin(ang)
    f_real = np.concatenate([wr, wi], 0)
    wr_h, wi_h = wr[:, :n], wi[:, :n]
    f_fwd = np.block([[wr_h, -wi_h], [wi_h, wr_h]])
    cr, ci = wr.T[:n] / nn, -wi.T[:n] / nn
    f_inv = np.block([[cr, -ci], [ci, cr]])
    return (np.asarray(f_real, np.float32), np.asarray(f_fwd, np.float32), np.asarray(f_inv, np.float32))


@functools.lru_cache(maxsize=None)
def _two_stage_dft_tables(n):
    nn = 2 * n
    n2c = FFT_N2
    n1c = nn // n2c
    n1h = n1c // 2
    k1 = np.arange(n1c).astype(np.float64)
    n1 = np.arange(n1c).astype(np.float64)
    n2 = np.arange(n2c).astype(np.float64)
    ang = -2.0 * np.pi * (k1[None, :, None] * n1[None, None, :] / n1c + n2[:, None, None] * k1[None, :, None] / nn)
    mr, mi = np.cos(ang), np.sin(ang)
    f1_real = np.concatenate([mr, mi], 1)
    mrh, mih = mr[:, :, :n1h], mi[:, :, :n1h]
    f1_cplx = np.concatenate([np.concatenate([mrh, -mih], 2), np.concatenate([mih, mrh], 2)], 1)
    gr = np.transpose(mr, (0, 2, 1))[:, :n1h, :] / nn
    gi = -np.transpose(mi, (0, 2, 1))[:, :n1h, :] / nn
    g1 = np.concatenate([np.concatenate([gr, -gi], 2), np.concatenate([gi, gr], 2)], 1)
    k2 = np.arange(n2c).astype(np.float64)
    a2 = -2.0 * np.pi * k2[:, None] * n2[None, :] / n2c
    fr, fi = np.cos(a2), np.sin(a2)
    f2 = np.block([[fr, -fi], [fi, fr]])
    f2i = np.block([[fr.T, fi.T], [-fi.T, fr.T]])
    f32 = lambda a: np.asarray(a, np.float32)
    return f32(f1_real), f32(f1_cplx), f32(g1), f32(f2), f32(f2i)


def _spec_dense_body(f_ref, x_ref, o_ref):
    o_ref[...] = _dot(f_ref[...], x_ref[...], hi=True)


def hyena_spec_dense(full, n):
    f_real, _, _ = _dense_dft_tables(n)
    nn, cols = full.shape
    return pl.pallas_call(
        _spec_dense_body,
        grid=(cols // LANE,),
        in_specs=[pl.BlockSpec((2 * nn, nn), lambda j: (0, 0)),
                  pl.BlockSpec((nn, LANE), lambda j: (0, j))],
        out_specs=pl.BlockSpec((2 * nn, LANE), lambda j: (0, j)),
        out_shape=jax.ShapeDtypeStruct((2 * nn, cols), F32),
        compiler_params=_cparams(("arbitrary",)),
        name="hyena_spec_dense",
    )(jnp.asarray(f_real), full)


def _conv_dense_body(*refs, has_mult):
    z_ref, h_ref, ff_ref, fi_ref, bias_ref = refs[:5]
    m_ref = refs[5] if has_mult else None
    o_ref = refs[-1]
    z = z_ref[...]
    nn = z.shape[0]
    x = _dot(ff_ref[...], z, hi=True)
    xr, xi = x[:nn], x[nn:]
    hr, hi_ = h_ref[0:nn, :], h_ref[nn:2 * nn, :]
    y = _dot(fi_ref[...], jnp.concatenate([xr * hr - xi * hi_, xr * hi_ + xi * hr], 0), hi=True)
    out = y + z * bias_ref[...]
    if has_mult:
        out = out * m_ref[...]
    o_ref[...] = out


def hyena_conv_dense(zsrc, zcol, row0, n, n_batch, spec, bias3, layer, order, prev, mult=None):
    _, f_fwd, f_inv = _dense_dft_tables(n)
    nn = 2 * n
    rb, cb = row0 // nn, zcol // LANE
    wb = BRANCH_W // LANE
    in_specs = [pl.BlockSpec((nn, LANE), lambda p, j: (rb + p, cb + j)),
                pl.BlockSpec((2 * nn, LANE), lambda p, j: (0, order * wb + j)),
                pl.BlockSpec((2 * nn, nn), lambda p, j: (0, 0)),
                pl.BlockSpec((nn, 2 * nn), lambda p, j: (0, 0)),
                pl.BlockSpec((None, 1, LANE), lambda p, j: (layer * HY_ORDER + order, 0, j))]
    args = [zsrc, spec, jnp.asarray(f_fwd), jnp.asarray(f_inv), bias3]
    if mult is not None:
        mb = mult[1] // LANE
        in_specs.append(pl.BlockSpec((nn, LANE), lambda p, j: (rb + p, mb + j)))
        args.append(mult[0])
    in_specs.append(pl.BlockSpec(memory_space=pl.ANY))
    args.append(prev)
    return pl.pallas_call(
        functools.partial(_conv_dense_body, has_mult=mult is not None),
        grid=(n_batch // 2, wb),
        in_specs=in_specs,
        out_specs=pl.BlockSpec((nn, LANE), lambda p, j: (rb + p, j)),
        out_shape=jax.ShapeDtypeStruct(prev.shape, F32),
        input_output_aliases={len(args) - 1: 0},
        compiler_params=_cparams(("arbitrary", "arbitrary")),
        name="hyena_conv_dense",
    )(*args)


def _spec_fft_body(x_ref, f1_ref, f2_ref, o_ref, a_ref):
    n1c = o_ref.shape[0]

    def stage1(g, carry):
        n2s = [g * FFT_UNROLL + u for u in range(FFT_UNROLL)]
        xs = [x_ref[pl.ds(n2, n1c, stride=FFT_N2), :] for n2 in n2s]
        res = [_dot(f1_ref[n2], x) for n2, x in zip(n2s, xs)]
        for n2, r in zip(n2s, res):
            a_ref[pl.ds(pl.multiple_of(n2 * 2 * n1c, 2 * n1c), 2 * n1c), :] = r
        return carry

    lax.fori_loop(0, FFT_N2 // FFT_UNROLL, stage1, 0, unroll=2)
    g2 = FFT_UNROLL // 2

    def stage2(g, carry):
        k1s = [g * g2 + u for u in range(g2)]
        blks = [jnp.concatenate([a_ref[pl.ds(k1, FFT_N2, stride=2 * n1c), :],
                                 a_ref[pl.ds(n1c + k1, FFT_N2, stride=2 * n1c), :]], 0) for k1 in k1s]
        res = [_dot(f2_ref[...], blk) for blk in blks]
        for k1, r in zip(k1s, res):
            o_ref[k1] = r
        return carry

    lax.fori_loop(0, n1c // g2, stage2, 0, unroll=2)


def hyena_spec_fft(full, n):
    f1_real, _, _, f2, _ = _two_stage_dft_tables(n)
    nn, cols = full.shape
    n1c = nn // FFT_N2
    const = lambda shape: pl.BlockSpec(shape, lambda j: (0,) * len(shape), pipeline_mode=pl.Buffered(1))
    return pl.pallas_call(
        _spec_fft_body,
        grid=(cols // LANE,),
        in_specs=[pl.BlockSpec((nn, LANE), lambda j: (0, j)),
                  const((FFT_N2, 2 * n1c, n1c)), const((2 * FFT_N2, 2 * FFT_N2))],
        out_specs=pl.BlockSpec((n1c, 2 * FFT_N2, LANE), lambda j: (0, 0, j)),
        out_shape=jax.ShapeDtypeStruct((n1c, 2 * FFT_N2, cols), F32),
        scratch_shapes=[pltpu.VMEM((FFT_N2 * 2 * n1c, LANE), F32)],
        compiler_params=_cparams(("arbitrary",)),
        name="hyena_spec_fft",
    )(full, jnp.asarray(f1_real, BF16), jnp.asarray(f2, BF16))


def _conv_fft_body(*refs, has_mult):
    z_ref, h_ref, f1_ref, f2_ref, f2i_ref, g1_ref, bias_ref = refs[:7]
    m_ref = refs[7] if has_mult else None
    o_ref, a_ref, b_ref = refs[-3], refs[-2], refs[-1]
    n1c = h_ref.shape[0]
    n1h = n1c // 2
    n2c = FFT_N2
    n = n1h * n2c

    def slab(n2):
        return pl.ds(pl.multiple_of(n2 * 2 * n1c, 2 * n1c), 2 * n1c)

    def stage1(g, carry):
        n2s = [g * FFT_UNROLL + u for u in range(FFT_UNROLL)]
        xs = [jnp.concatenate([z_ref[pl.ds(n2, n1h, stride=n2c), :], z_ref[pl.ds(n + n2, n1h, stride=n2c), :]], 0)
              for n2 in n2s]
        res = [_dot(f1_ref[n2], x) for n2, x in zip(n2s, xs)]
        for n2, r in zip(n2s, res):
            a_ref[slab(n2), :] = r
        return carry

    lax.fori_loop(0, n2c // FFT_UNROLL, stage1, 0, unroll=2)
    g2 = FFT_UNROLL // 2

    def stage2(g, carry):
        k1s = [g * g2 + u for u in range(g2)]
        rows = [(pl.ds(k1, n2c, stride=2 * n1c), pl.ds(n1c + k1, n2c, stride=2 * n1c)) for k1 in k1s]
        blks = [jnp.concatenate([a_ref[re, :], a_ref[im, :]], 0) for re, im in rows]
        xs = [_dot(f2_ref[...], blk) for blk in blks]
        ys = []
        for k1, x in zip(k1s, xs):
            xr, xi = x[:n2c], x[n2c:]
            hr, hi_ = h_ref[k1, 0:n2c, :], h_ref[k1, n2c:2 * n2c, :]
            ys.append(jnp.concatenate([xr * hr - xi * hi_, xr * hi_ + xi * hr], 0))
        bs = [_dot(f2i_ref[...], y) for y in ys]
        for (re, im), b in zip(rows, bs):
            b_ref[re, :] = b[:n2c]
            b_ref[im, :] = b[n2c:]
        return carry

    lax.fori_loop(0, n1c // g2, stage2, 0, unroll=2)
    bias = bias_ref[...]

    def stage3(g, carry):
        n2s = [g * FFT_UNROLL + u for u in range(FFT_UNROLL)]
        blks = [b_ref[slab(n2), :] for n2 in n2s]
        ys = [_dot(g1_ref[n2], blk) for n2, blk in zip(n2s, blks)]
        outs = []
        for n2, y in zip(n2s, ys):
            for part, rows in ((y[:n1h], pl.ds(n2, n1h, stride=n2c)), (y[n1h:], pl.ds(n + n2, n1h, stride=n2c))):
                out = part + z_ref[rows, :] * bias
                if has_mult:
                    out = out * m_ref[rows, :]
                outs.append((rows, out))
        for rows, out in outs:
            o_ref[rows, :] = out
        return carry

    lax.fori_loop(0, n2c // FFT_UNROLL, stage3, 0, unroll=2)


def hyena_conv_fft(zsrc, zcol, n, n_batch, spec, bias3, layer, order, t_rows, mult=None):
    _, f1_cplx, g1, f2, f2i = _two_stage_dft_tables(n)
    n1c = 2 * n // FFT_N2
    cb = zcol // LANE
    wb = BRANCH_W // LANE
    const = lambda shape: pl.BlockSpec(shape, lambda j, p: (0,) * len(shape), pipeline_mode=pl.Buffered(1))
    in_specs = [pl.BlockSpec((2 * n, LANE), lambda j, p: (p, cb + j)),
                pl.BlockSpec((n1c, 2 * FFT_N2, LANE), lambda j, p: (0, 0, order * wb + j),
                             pipeline_mode=pl.Buffered(1)),
                const((FFT_N2, 2 * n1c, n1c)), const((2 * FFT_N2, 2 * FFT_N2)), const((2 * FFT_N2, 2 * FFT_N2)),
                const((FFT_N2, n1c, 2 * n1c)),
                pl.BlockSpec((None, 1, LANE), lambda j, p: (layer * HY_ORDER + order, 0, j))]
    args = [zsrc, spec, jnp.asarray(f1_cplx, BF16), jnp.asarray(f2, BF16), jnp.asarray(f2i, BF16),
            jnp.asarray(g1, BF16), bias3]
    if mult is not None:
        mb = mult[1] // LANE
        in_specs.append(pl.BlockSpec((2 * n, LANE), lambda j, p: (p, mb + j)))
        args.append(mult[0])
    return pl.pallas_call(
        functools.partial(_conv_fft_body, has_mult=mult is not None),
        grid=(wb, n_batch // 2),
        in_specs=in_specs,
        out_specs=pl.BlockSpec((2 * n, LANE), lambda j, p: (p, j)),
        out_shape=jax.ShapeDtypeStruct((t_rows, BRANCH_W), F32),
        scratch_shapes=[pltpu.VMEM((FFT_N2 * 2 * n1c, LANE), F32)] * 2,
        compiler_params=_cparams(("arbitrary", "arbitrary")),
        name="hyena_conv_fft",
    )(*args)


def _swap_pairs(x):
    w = x.shape[-1]
    lane = lax.broadcasted_iota(jnp.int32, x.shape, x.ndim - 1)
    return jnp.where(lane % 2 == 0, pltpu.roll(x, w - 1, x.ndim - 1), pltpu.roll(x, 1, x.ndim - 1))


def _attn_prep_body(g_ref, dq_ref, dk_ref, dv_ref, cg_ref, sg_ref, cd_ref, sd_ref, qn_ref, kn_ref,
                    qg_ref, kg_ref, vg_ref, qd_ref, kd_ref, vd_ref, *, lat_blocks):
    is_lat = pl.program_id(0) < lat_blocks
    cg = jnp.where(is_lat, cg_ref[...], 1.0)
    sg = jnp.where(is_lat, sg_ref[...], 0.0)
    cd = jnp.where(is_lat, cd_ref[...], 1.0)
    sd = jnp.where(is_lat, sd_ref[...], 0.0)

    def rope(x, cs, sn):
        return x * cs + _swap_pairs(x) * sn

    def rms(x, w):
        return x * lax.rsqrt(jnp.mean(x * x, axis=-1, keepdims=True) + EPS) * w

    for h in range(HEADS):
        sl = slice(h * HEAD_D, (h + 1) * HEAD_D)
        q = rope(rms(g_ref[:, sl], qn_ref[...]), cg, sg)
        qg_ref[:, sl] = (q * HEAD_D ** -0.5).astype(BF16)
        qd_ref[:, sl] = (rope(dq_ref[:, sl], cd, sd) * DIFF_QK ** -0.5).astype(BF16)
        kd_ref[:, sl] = rope(dk_ref[:, sl], cd, sd).astype(BF16)
    for h in range(GQA_KV):
        sl = slice(h * HEAD_D, (h + 1) * HEAD_D)
        kin = g_ref[:, BRANCH_W + h * HEAD_D:BRANCH_W + (h + 1) * HEAD_D]
        kg_ref[:, sl] = rope(rms(kin, kn_ref[...]), cg, sg).astype(BF16)
    vg_ref[...] = g_ref[:, BRANCH_W + GQA_KV * HEAD_D:BRANCH_W + 2 * GQA_KV * HEAD_D].astype(BF16)
    vd_ref[...] = dv_ref[...].astype(BF16)


def attn_prep(p, ropes, qn3, kn3, layer, n_lat, n_ctx, n_batch):
    t = p.shape[0]
    r = 256 if n_ctx % 256 == 0 else n_ctx
    nlb, ncb = n_lat // r, n_ctx // r
    lat_blocks = n_batch * nlb
    kvw = GQA_KV * HEAD_D
    w = BRANCH_W

    def kv_row(i):
        lat = (i // nlb) * (nlb + ncb) + ncb + i % nlb
        j = i - lat_blocks
        ctx = (j // ncb) * (nlb + ncb) + j % ncb
        return jnp.where(i < lat_blocks, lat, ctx)

    rope_spec = pl.BlockSpec((r, LANE), lambda i: (jnp.where(i < lat_blocks, i % nlb, 0), 0))
    nkv = n_batch * (n_lat + n_ctx)
    return pl.pallas_call(
        functools.partial(_attn_prep_body, lat_blocks=lat_blocks),
        grid=(t // r,),
        in_specs=[pl.BlockSpec((r, 2 * w), lambda i: (i, C_GQA_QKV // (2 * w))),
                  pl.BlockSpec((r, w), lambda i: (i, C_DIFF_Q // w)),
                  pl.BlockSpec((r, w), lambda i: (i, C_DIFF_K // w)),
                  pl.BlockSpec((r, w), lambda i: (i, C_DIFF_V // w)),
                  rope_spec, rope_spec, rope_spec, rope_spec,
                  pl.BlockSpec((None, 1, LANE), lambda i: (layer, 0, 0)),
                  pl.BlockSpec((None, 1, LANE), lambda i: (layer, 0, 0))],
        out_specs=[pl.BlockSpec((r, w), lambda i: (i, 0)),
                   pl.BlockSpec((r, kvw), lambda i: (kv_row(i), 0)),
                   pl.BlockSpec((r, kvw), lambda i: (kv_row(i), 0)),
                   pl.BlockSpec((r, w), lambda i: (i, 0)),
                   pl.BlockSpec((r, w), lambda i: (kv_row(i), 0)),
                   pl.BlockSpec((r, w), lambda i: (kv_row(i), 0))],
        out_shape=[jax.ShapeDtypeStruct((t, w), BF16), jax.ShapeDtypeStruct((nkv, kvw), BF16),
                   jax.ShapeDtypeStruct((nkv, kvw), BF16), jax.ShapeDtypeStruct((t, w), BF16),
                   jax.ShapeDtypeStruct((nkv, w), BF16), jax.ShapeDtypeStruct((nkv, w), BF16)],
        compiler_params=_cparams(("arbitrary",)),
        name="attn_prep",
    )(p, p, p, p, *ropes, qn3, kn3)


def _softmax_parts(s):
    e = jnp.exp(s - jnp.max(s, axis=-1, keepdims=True))
    return e, jnp.sum(e, axis=-1, keepdims=True)


def _gqa_body(q_ref, k_ref, v_ref, *rest):
    o_ref = rest[-1]
    group = HEADS // GQA_KV
    for kvh in range(GQA_KV):
        k = k_ref[:, kvh * HEAD_D:(kvh + 1) * HEAD_D]
        v = v_ref[:, kvh * HEAD_D:(kvh + 1) * HEAD_D]
        for g in range(group):
            sl = slice((kvh * group + g) * HEAD_D, (kvh * group + g + 1) * HEAD_D)
            s = lax.dot_general(q_ref[:, sl], k, (((1,), (1,)), ((), ())), preferred_element_type=F32)
            e, l = _softmax_parts(s)
            o_ref[:, sl] = jnp.dot(e.astype(BF16), v, preferred_element_type=F32) / l


def _diff_body(q_ref, k_ref, v_ref, lam_ref, *rest, lam_init):
    o_ref = rest[-1]
    lam4 = lam_ref[...]
    lam = (jnp.exp(jnp.sum(lam4[0:1] * lam4[1:2], axis=-1, keepdims=True))
           - jnp.exp(jnp.sum(lam4[2:3] * lam4[3:4], axis=-1, keepdims=True)) + lam_init)
    dn = (((1,), (1,)), ((), ()))
    for h in range(HEADS):
        sl = slice(h * HEAD_D, (h + 1) * HEAD_D)
        q = q_ref[:, sl]
        k = k_ref[:, sl]
        v = v_ref[:, sl]
        first = lax.broadcasted_iota(jnp.int32, q.shape, 1) < DIFF_QK
        zero = jnp.zeros_like(q)
        e1, l1 = _softmax_parts(lax.dot_general(jnp.where(first, q, zero), k, dn, preferred_element_type=F32))
        e2, l2 = _softmax_parts(lax.dot_general(jnp.where(first, zero, q), k, dn, preferred_element_type=F32))
        wgt = e1 * (1.0 / l1) - e2 * (lam / l2)
        o_ref[:, sl] = jnp.dot(wgt.astype(BF16), v, preferred_element_type=F32)


def attention(body, q, k, v, extra, extra_specs, q_row0, nq, kv_per_batch, kv_len, n_batch, tq, name, prev=None):
    t, w = q.shape
    qb0 = q_row0 // tq
    nqb = nq // tq
    kvb = kv_per_batch // kv_len
    in_specs = [pl.BlockSpec((tq, w), lambda b, i: (qb0 + b * nqb + i, 0)),
                pl.BlockSpec((kv_len, k.shape[1]), lambda b, i: (b * kvb, 0)),
                pl.BlockSpec((kv_len, v.shape[1]), lambda b, i: (b * kvb, 0))] + extra_specs
    args = [q, k, v, *extra]
    aliases = {}
    if prev is not None:
        in_specs.append(pl.BlockSpec(memory_space=pl.ANY))
        args.append(prev)
        aliases = {len(args) - 1: 0}
    return pl.pallas_call(
        body,
        grid=(n_batch, nqb),
        in_specs=in_specs,
        out_specs=pl.BlockSpec((tq, w), lambda b, i: (qb0 + b * nqb + i, 0)),
        out_shape=jax.ShapeDtypeStruct((t, w), F32),
        input_output_aliases=aliases,
        compiler_params=_cparams(("arbitrary", "arbitrary")),
        name=name,
    )(*args)


def _merge_body(h_ref, mod_ref, mg_ref, of_ref, ob_ref, ggate_ref, y1_ref, x2_ref, hgate_ref, oc_ref, cgate_ref,
                od_ref, dgate_ref, gnorm_ref, dnorm_ref, wbr_ref, wout_ref, lng_ref, lnb_ref, o_ref, *, diff_scale):
    def rms_heads(x, w):
        parts = []
        for h in range(HEADS):
            xh = x[:, h * HEAD_D:(h + 1) * HEAD_D]
            parts.append(xh * lax.rsqrt(jnp.mean(xh * xh, axis=-1, keepdims=True) + EPS) * w)
        return jnp.concatenate(parts, -1)

    ys = (rms_heads(of_ref[...] + ob_ref[...], gnorm_ref[...]) * _silu(ggate_ref[...]),
          x2_ref[...] * y1_ref[...] * _silu(hgate_ref[...]),
          oc_ref[...] * _silu(cgate_ref[...]),
          rms_heads(od_ref[...], dnorm_ref[...]) * diff_scale * _silu(dgate_ref[...]))
    acc = None
    for n in range(N_BRANCH):
        proj = jnp.dot(ys[n].astype(BF16), wbr_ref[n], preferred_element_type=F32)
        term = _sigmoid(mg_ref[:, n * D_MODEL:(n + 1) * D_MODEL]) * proj
        acc = term if acc is None else acc + term
    out = jnp.dot(acc.astype(BF16), wout_ref[...], preferred_element_type=F32)
    x = ALPHA * h_ref[...] + mod_ref[2:3, :] * out
    mu = jnp.mean(x, axis=-1, keepdims=True)
    xc = x - mu
    var = jnp.mean(xc * xc, axis=-1, keepdims=True)
    o_ref[...] = xc * lax.rsqrt(var + EPS) * lng_ref[...] + lnb_ref[...]


def merge_postnorm(h_all, mod3, p, o_f, o_b, y1, xv, oc, od, gnorm3, dnorm3, wbr, wout, lng3, lnb3, layer, lam_init,
                   n_lat, n_batch):
    t, d = h_all.shape
    r = 256 if n_lat % 256 == 0 else 64
    w = BRANCH_W
    lbb = n_lat // r
    row = lambda i: jnp.minimum(i // lbb, n_batch)
    tok = lambda cb: pl.BlockSpec((r, w), lambda i: (i, cb))
    vec = lambda width: pl.BlockSpec((None, 1, width), lambda i: (layer, 0, 0))
    return pl.pallas_call(
        functools.partial(_merge_body, diff_scale=1.0 - lam_init),
        grid=(t // r,),
        in_specs=[pl.BlockSpec((r, d), lambda i: (i, 0)),
                  pl.BlockSpec((None, 3, d), lambda i: (row(i), 0, 0)),
                  pl.BlockSpec((r, N_BRANCH * d), lambda i: (i, C_MERGE // (N_BRANCH * d))),
                  tok(0), tok(0), tok(C_GDN_GATE // w), tok(0), tok(1), tok(C_HY_GATE // w), tok(0),
                  tok(C_GQA_GATE // w), tok(0), tok(C_DIFF_GATE // w),
                  vec(LANE), vec(LANE),
                  pl.BlockSpec((None, N_BRANCH, w, d), lambda i: (layer, 0, 0, 0)),
                  pl.BlockSpec((None, d, d), lambda i: (layer, 0, 0)),
                  vec(d), vec(d)],
        out_specs=pl.BlockSpec((r, d), lambda i: (i, 0)),
        out_shape=jax.ShapeDtypeStruct((t, d), F32),
        compiler_params=_cparams(("arbitrary",)),
        name="merge_postnorm",
    )(h_all, mod3, p, o_f, o_b, p, y1, xv, p, oc, p, od, p, gnorm3, dnorm3, wbr, wout, lng3, lnb3)


def _rope_tables(n_lat, dim):
    rows = n_lat // GRID_W
    row = jnp.repeat(jnp.arange(rows, dtype=F32), GRID_W)
    col = jnp.tile(jnp.arange(GRID_W, dtype=F32), rows)
    half = dim // 2
    inv = ROPE_THETA ** (-jnp.arange(0, half, 2, dtype=F32) / half)
    ang = jnp.concatenate([row[:, None] * inv, col[:, None] * inv], -1)
    cos = jnp.repeat(jnp.cos(ang), 2, axis=-1)
    sin = jnp.repeat(jnp.sin(ang), 2, axis=-1)
    sign = jnp.tile(jnp.array([-1.0, 1.0], F32), dim // 2)
    reps = LANE // dim
    return jnp.tile(cos, (1, reps)), jnp.tile(sin * sign, (1, reps))


def kernel(x, c, ctx, c_ctx, w_ada, b_ada, w_in, gdn_conv, gdn_a_log, gdn_dt_bias, gdn_norm, hy_conv, hy_w1, hy_b1,
           hy_w2, hy_b2, hy_w3, hy_b3, hy_w4, hy_freq, hy_bias, gqa_qn, gqa_kn, diff_lam, diff_norm, w_br, w_out,
           ln_g, ln_b):
    nb, n_lat, d = x.shape
    n_ctx = ctx.shape[1]
    t_lat, t_ctx = nb * n_lat, nb * n_ctx
    depth = w_in.shape[0]
    w = BRANCH_W

    w_main = jnp.concatenate([w_in[:, :, O_MERGE:], w_in[:, :, :O_GDN_AB], w_in[:, :, O_GDN_AB + 4 * HEADS:O_MERGE]],
                             axis=2).astype(BF16)
    w_ab = jnp.pad(w_in[:, :, O_GDN_AB:O_GDN_AB + 4 * HEADS], ((0, 0), (0, 0), (0, LANE - 4 * HEADS)))
    wbr_bf = w_br.astype(BF16)
    wout_bf = w_out.astype(BF16)
    b_ada3 = b_ada[:, None, :]
    cvec = jnp.concatenate([c, c_ctx[None, :], jnp.zeros((SUB - nb - 1, d), F32)], 0)
    as3 = lambda a: a[:, None, :]
    gdn_par_r = jnp.pad(jnp.stack([gdn_a_log.reshape(depth, -1), gdn_dt_bias.reshape(depth, -1)], 1),
                        ((0, 0), (0, SUB - 2), (0, LANE - 2 * HEADS)))
    gdn_par_c = jnp.pad(jnp.stack([gdn_a_log.reshape(depth, -1), gdn_dt_bias.reshape(depth, -1)], 2),
                        ((0, 0), (0, 2 * HEADS), (0, LANE - 2)))
    hy_w1p = jnp.pad(hy_w1, ((0, 0), (0, LANE - HY_EMB), (0, 0)))
    hy_bias3 = hy_bias.reshape(depth * HY_ORDER, 1, w)
    ropes = _rope_tables(n_lat, HEAD_D) + _rope_tables(n_lat, DIFF_QK)

    tm = 1024 if (n_lat % 1024 == 0 and t_ctx % 1024 == 0) else n_ctx
    h_all = jnp.concatenate([x.reshape(t_lat, d), ctx.reshape(t_ctx, d)], 0)
    for l in range(depth):
        lam_init = 0.8 - 0.6 * math.exp(-0.3 * l)
        mod3 = ada_mod(cvec, w_ada, b_ada3, l).reshape(SUB, 3, d)
        p, ab = in_proj(h_all, mod3, w_main, w_ab, l, tm, n_lat // tm, nb)

        qkv = dwconv(p, gdn_conv, l, C_GDN_QKV, 3 * w, n_lat, n_ctx, nb, act=True)
        ab_rows = jnp.transpose(ab[:, :4 * HEADS].reshape(-1, GDN_CHUNK, 4 * HEADS), (0, 2, 1))
        o_f, o_b = gdn_scan(qkv, ab, ab_rows, gdn_par_r[l], gdn_par_c[l], n_lat, n_ctx, nb)

        xv = dwconv(p, hy_conv, l, C_HY_XV, 3 * w, n_lat, n_ctx, nb, act=False)
        filt = lambda n: hyena_filter(n, hy_w1p, as3(hy_b1), hy_w2, as3(hy_b2), hy_w3, as3(hy_b3), hy_w4,
                                      as3(hy_freq), l)
        spec_lat = hyena_spec_fft(filt(n_lat), n_lat)
        spec_ctx = hyena_spec_dense(filt(n_ctx), n_ctx)
        z1 = hyena_conv_fft(xv, 2 * w, n_lat, nb, spec_lat, hy_bias3, l, 0, t_lat + t_ctx, mult=(xv, 0))
        z1 = hyena_conv_dense(xv, 2 * w, t_lat, n_ctx, nb, spec_ctx, hy_bias3, l, 0, z1, mult=(xv, 0))
        y1 = hyena_conv_fft(z1, 0, n_lat, nb, spec_lat, hy_bias3, l, 1, t_lat + t_ctx)
        y1 = hyena_conv_dense(z1, 0, t_lat, n_ctx, nb, spec_ctx, hy_bias3, l, 1, y1)

        qg, kg, vg, qd, kd, vd = attn_prep(p, ropes, as3(gqa_qn), as3(gqa_kn), l, n_lat, n_ctx, nb)
        kv_all = n_lat + n_ctx
        tq = min(256, n_ctx)
        lam_spec = [pl.BlockSpec((None, 4, DIFF_QK), lambda b, i: (l, 0, 0))]
        diff_body = functools.partial(_diff_body, lam_init=lam_init)
        oc = attention(_gqa_body, qg, kg, vg, (), [], 0, n_lat, kv_all, kv_all, nb, tq, "gqa_lat")
        oc = attention(_gqa_body, qg, kg, vg, (), [], t_lat, n_ctx, kv_all, n_ctx, nb, tq, "gqa_ctx", prev=oc)
        od = attention(diff_body, qd, kd, vd, (diff_lam,), lam_spec, 0, n_lat, kv_all, kv_all, nb, tq, "diff_lat")
        od = attention(diff_body, qd, kd, vd, (diff_lam,), lam_spec, t_lat, n_ctx, kv_all, n_ctx, nb, tq, "diff_ctx",
                       prev=od)

        h_all = merge_postnorm(h_all, mod3, p, o_f, o_b, y1, xv, oc, od, as3(gdn_norm), as3(diff_norm), wbr_bf, wout_bf,
                               as3(ln_g), as3(ln_b), l, lam_init, n_lat, nb)
    return h_all[:t_lat].reshape(nb, n_lat, d)
```

```python
import functools
import math

import numpy as np
import jax
import jax.numpy as jnp
from jax import lax
from jax.experimental import pallas as pl
from jax.experimental.pallas import tpu as pltpu

F32 = jnp.float32
BF16 = jnp.bfloat16
HI = lax.Precision.HIGHEST

D_MODEL = 1024
DEPTH = 4
GRID_W = 64
BRANCH_W = D_MODEL // 2
N_BRANCH = 4
HEADS = 4
HEAD_D = BRANCH_W // HEADS
GDN_CONV = 4
GDN_CHUNK = 64
GDN_SUB = 2
HY_CONV = 3
HY_EMB = 33
HY_BANDS = (HY_EMB - 1) // 2
HY_FH = 64
HY_ORDER = 2
HY_MIN_DECAY = math.log(1e-2) / 1.5
HY_MAX_DECAY = math.log(1e-2) / 0.3
GQA_KV = 2
DIFF_QK = HEAD_D // 2
ROPE_THETA = 10000.0
EPS = 1e-6
ALPHA = (2.0 * DEPTH) ** 0.25

LANE = 128
SUB = 8
FFT_N2 = 128
FFT_UNROLL = 8
VMEM_LIMIT = 60 * 1024 * 1024

C_MERGE = 0
C_GDN_QKV = 4096
C_GDN_GATE = 5632
C_HY_XV = 6144
C_HY_GATE = 7680
C_GQA_QKV = 8192
C_GQA_GATE = 9216
C_DIFF_Q = 9728
C_DIFF_K = 10240
C_DIFF_V = 10752
C_DIFF_GATE = 11264
N_MAIN = 11776
O_GDN_AB = 1536
O_MERGE = 7696


def _cparams(sem):
    return pltpu.CompilerParams(dimension_semantics=sem, vmem_limit_bytes=VMEM_LIMIT)


def _dot(a, b, hi=False):
    if hi:
        return jnp.dot(a, b, precision=HI, preferred_element_type=F32)
    return jnp.dot(a.astype(BF16), b.astype(BF16), preferred_element_type=F32)


def _dot_nt(a, b, hi=False):
    dn = (((1,), (1,)), ((), ()))
    if hi:
        return lax.dot_general(a, b, dn, precision=HI, preferred_element_type=F32)
    return lax.dot_general(a.astype(BF16), b.astype(BF16), dn, preferred_element_type=F32)


def _dot_tn(a, b):
    return lax.dot_general(a.astype(BF16), b.astype(BF16), (((0,), (0,)), ((), ())), preferred_element_type=F32)


def _sigmoid(x):
    return 1.0 / (1.0 + jnp.exp(-x))


def _silu(x):
    return x * _sigmoid(x)


def _softplus(x):
    return jnp.maximum(x, 0.0) + jnp.log1p(jnp.exp(-jnp.abs(x)))


def _ada_body(c_ref, w_ref, b_ref, o_ref):
    o_ref[...] = _dot(_silu(c_ref[...]), w_ref[...], hi=True) + b_ref[...]


def ada_mod(cvec, w_ada, b_ada3, layer):
    d = cvec.shape[1]
    tn = 512
    return pl.pallas_call(
        _ada_body,
        grid=(3 * d // tn,),
        in_specs=[pl.BlockSpec((SUB, d), lambda j: (0, 0)),
                  pl.BlockSpec((None, d, tn), lambda j: (layer, 0, j)),
                  pl.BlockSpec((None, 1, tn), lambda j: (layer, 0, j))],
        out_specs=pl.BlockSpec((SUB, tn), lambda j: (0, j)),
        out_shape=jax.ShapeDtypeStruct((SUB, 3 * d), F32),
        compiler_params=_cparams(("arbitrary",)),
        name="ada_mod",
    )(cvec, w_ada, b_ada3)


def _inproj_body(h_ref, mod_ref, w_ref, wab_ref, o_ref, ab_ref, u_ref):
    @pl.when(pl.program_id(1) == 0)
    def _():
        x = h_ref[...]
        mu = jnp.mean(x, axis=-1, keepdims=True)
        xc = x - mu
        var = jnp.mean(xc * xc, axis=-1, keepdims=True)
        u = xc * lax.rsqrt(var + EPS) * (1.0 + mod_ref[1:2, :]) + mod_ref[0:1, :]
        u_ref[...] = u.astype(BF16)
        ab_ref[...] = _dot(u, wab_ref[...], hi=True)

    o_ref[...] = jnp.dot(u_ref[...], w_ref[...], preferred_element_type=F32)


def in_proj(h_all, mod3, w_main, w_ab, layer, tm, lat_blocks_per_batch, n_batch):
    t, d = h_all.shape
    tn = 512
    n_main = w_main.shape[2]
    row = lambda i: jnp.minimum(i // lat_blocks_per_batch, n_batch)
    return pl.pallas_call(
        _inproj_body,
        grid=(t // tm, n_main // tn),
        in_specs=[pl.BlockSpec((tm, d), lambda i, j: (i, 0)),
                  pl.BlockSpec((None, 3, d), lambda i, j: (row(i), 0, 0)),
                  pl.BlockSpec((None, d, tn), lambda i, j: (layer, 0, j)),
                  pl.BlockSpec((None, d, LANE), lambda i, j: (layer, 0, 0))],
        out_specs=[pl.BlockSpec((tm, tn), lambda i, j: (i, j)),
                   pl.BlockSpec((tm, LANE), lambda i, j: (i, 0))],
        out_shape=[jax.ShapeDtypeStruct((t, n_main), F32), jax.ShapeDtypeStruct((t, LANE), F32)],
        scratch_shapes=[pltpu.VMEM((tm, d), BF16)],
        compiler_params=_cparams(("arbitrary", "arbitrary")),
        name="in_proj",
    )(h_all, mod3, w_main, w_ab)


def _dwconv_body(xp_ref, x_ref, xn_ref, w_ref, o_ref, pad_ref, *, taps, pad_l, t_lat, n_lat, n_ctx, sb, act):
    i = pl.program_id(0)
    r = x_ref.shape[0]
    pad_ref[0:SUB, :] = xp_ref[...]
    pad_ref[SUB:SUB + r, :] = x_ref[...]
    pad_ref[SUB + r:2 * SUB + r, :] = xn_ref[...]
    row = lax.broadcasted_iota(jnp.int32, (sb, 1), 0)
    for k in range(r // sb):
        g0 = i * r + k * sb
        in_lat = g0 < t_lat
        starts = jnp.where(in_lat, g0 % n_lat == 0, (g0 - t_lat) % n_ctx == 0)
        ends = jnp.where(in_lat, (g0 + sb) % n_lat == 0, (g0 + sb - t_lat) % n_ctx == 0)
        acc = None
        for j in range(taps):
            d = j - pad_l
            off = SUB + k * sb + d
            xs = pad_ref[off:off + sb, :]
            if d < 0:
                xs = jnp.where(jnp.logical_and(starts, row < -d), 0.0, xs)
            elif d > 0:
                xs = jnp.where(jnp.logical_and(ends, row >= sb - d), 0.0, xs)
            term = w_ref[j:j + 1, :] * xs
            acc = term if acc is None else acc + term
        if act:
            acc = _silu(acc)
        o_ref[k * sb:(k + 1) * sb, :] = acc


def dwconv(p, w_conv, layer, col0, width, n_lat, n_ctx, n_batch, act):
    t = p.shape[0]
    taps = w_conv.shape[1]
    sb = min(256, n_ctx)
    r = 1024 if t % 1024 == 0 else sb
    lw = 512
    cb = col0 // lw
    rs = r // SUB
    body = functools.partial(_dwconv_body, taps=taps, pad_l=(taps - 1) // 2, t_lat=n_batch * n_lat, n_lat=n_lat,
                             n_ctx=n_ctx, sb=sb, act=act)
    return pl.pallas_call(
        body,
        grid=(t // r, width // lw),
        in_specs=[pl.BlockSpec((SUB, lw), lambda i, j: (jnp.maximum(i * rs - 1, 0), cb + j)),
                  pl.BlockSpec((r, lw), lambda i, j: (i, cb + j)),
                  pl.BlockSpec((SUB, lw), lambda i, j: (jnp.minimum((i + 1) * rs, t // SUB - 1), cb + j)),
                  pl.BlockSpec((None, taps, lw), lambda i, j: (layer, 0, j))],
        out_specs=pl.BlockSpec((r, lw), lambda i, j: (i, j)),
        out_shape=jax.ShapeDtypeStruct((t, width), F32),
        scratch_shapes=[pltpu.VMEM((r + 2 * SUB, lw), F32)],
        compiler_params=_cparams(("arbitrary", "arbitrary")),
        name="dwconv",
    )(p, p, p, w_conv)


def _gdn_body(qf_ref, qb_ref, abcf_ref, abcb_ref, abrf_ref, abrb_ref, pr_ref, pc_ref, of_ref, ob_ref, s_ref):
    c = GDN_CHUNK

    @pl.when(pl.program_id(1) == 0)
    def _():
        s_ref[...] = jnp.zeros_like(s_ref)

    ii = lax.broadcasted_iota(jnp.int32, (c, c), 0)
    jj = lax.broadcasted_iota(jnp.int32, (c, c), 1)
    lmat = (jj <= ii).astype(F32)
    eye = (jj == ii).astype(F32)
    alr, dtr = pr_ref[0:1, :], pr_ref[1:2, :]
    alc, dtc = pc_ref[:, 0:1], pc_ref[:, 1:2]
    chains = []
    for d in range(2):
        qkv_ref = (qf_ref, qb_ref)[d]
        abc_ref = (abcf_ref, abcb_ref)[d]
        abr_ref = (abrf_ref, abrb_ref)[d]
        incl = (jj <= ii) if d == 0 else (jj >= ii)
        strict = (jj < ii) if d == 0 else (jj > ii)
        for j in range(GDN_SUB):
            rows = slice(j * c, (j + 1) * c)
            abc = abc_ref[rows, :]
            abr = abr_ref[j]
            g_c = -jnp.exp(alr) * _softplus(abc + dtr)
            g_r = -jnp.exp(alc) * _softplus(abr + dtc)
            cum_c = _dot(lmat, g_c, hi=True)
            cum_r = _dot_nt(g_r, lmat, hi=True)
            if d == 1:
                cum_c = cum_c[c - 1:c, :] - cum_c + g_c
                cum_r = cum_r[:, c - 1:c] - cum_r + g_r
            beta_all = _sigmoid(abc)
            for h in range(HEADS):
                idx = HEADS * d + h
                q = qkv_ref[rows, h * HEAD_D:(h + 1) * HEAD_D]
                k = qkv_ref[rows, BRANCH_W + h * HEAD_D:BRANCH_W + (h + 1) * HEAD_D]
                v = qkv_ref[rows, 2 * BRANCH_W + h * HEAD_D:2 * BRANCH_W + (h + 1) * HEAD_D]
                q = q * lax.rsqrt(jnp.sum(q * q, axis=-1, keepdims=True) + EPS) * (HEAD_D ** -0.5)
                k = k * lax.rsqrt(jnp.sum(k * k, axis=-1, keepdims=True) + EPS)
                cc = cum_c[:, idx:idx + 1]
                cr = cum_r[idx:idx + 1, :]
                dec = jnp.exp(jnp.where(incl, cc - cr, -1e30))
                beta = beta_all[:, 2 * HEADS + idx:2 * HEADS + idx + 1]
                ecum = jnp.exp(cc)
                tot = cc[c - 1:c, :] if d == 0 else cc[0:1, :]
                chains.append(dict(d=d, h=h, j=j, rows=rows, q=q, k=k, dec=dec, strict=strict, beta=beta, ecum=ecum,
                                   tot=tot, rhs=jnp.concatenate([k * (beta * ecum), v * beta], 1),
                                   k_tail=k * jnp.exp(tot - cc)))
    for ch in chains:
        ch["kk"] = _dot_nt(ch["k"], ch["k"])
        ch["qk"] = _dot_nt(ch["q"], ch["k"])
    for ch in chains:
        ch["p"] = -jnp.where(ch["strict"], ch["beta"] * ch["kk"] * ch["dec"], 0.0)
        ch["inv"] = eye + ch["p"]
    for _ in range(int(math.log2(c)) - 1):
        for ch in chains:
            ch["p"] = _dot(ch["p"], ch["p"])
        for ch in chains:
            ch["inv"] = ch["inv"] + _dot(ch["inv"], ch["p"])
    for ch in chains:
        ch["wu"] = _dot(ch["inv"], ch["rhs"])
        ch["lhs"] = jnp.concatenate([ch["wu"][:, :HEAD_D], ch["q"] * ch["ecum"]], 0)
    state = {(d, h): s_ref[d, h] for d in range(2) for h in range(HEADS)}
    for step in range(GDN_SUB):
        cur = [ch for ch in chains if ch["j"] == (step if ch["d"] == 0 else GDN_SUB - 1 - step)]
        for ch in cur:
            ch["ws"] = _dot(ch["lhs"], state[ch["d"], ch["h"]])
        for ch in cur:
            ch["v_new"] = ch["wu"][:, HEAD_D:] - ch["ws"][:c]
        for ch in cur:
            out_ref = (of_ref, ob_ref)[ch["d"]]
            h = ch["h"]
            out_ref[ch["rows"], h * HEAD_D:(h + 1) * HEAD_D] = ch["ws"][c:] + _dot(ch["qk"] * ch["dec"], ch["v_new"])
            state[ch["d"], h] = state[ch["d"], h] * jnp.exp(ch["tot"]) + _dot_tn(ch["k_tail"], ch["v_new"])
    for (d, h), val in state.items():
        s_ref[d, h] = val


def gdn_scan(qkv, ab, ab_rows, par_r, par_c, n_lat, n_ctx, n_batch):
    t = qkv.shape[0]
    c = GDN_SUB * GDN_CHUNK
    nlc, ncc = n_lat // c, n_ctx // c
    base = n_batch * nlc

    def fwd(b, s):
        return jnp.where(s < ncc, base + b * ncc + s, b * nlc + (s - ncc))

    def bwd(b, s):
        return jnp.where(s < ncc, base + b * ncc + (ncc - 1 - s), b * nlc + (nlc - 1 - (s - ncc)))

    w3 = 3 * BRANCH_W
    return pl.pallas_call(
        _gdn_body,
        grid=(n_batch, ncc + nlc),
        in_specs=[pl.BlockSpec((c, w3), lambda b, s: (fwd(b, s), 0)),
                  pl.BlockSpec((c, w3), lambda b, s: (bwd(b, s), 0)),
                  pl.BlockSpec((c, LANE), lambda b, s: (fwd(b, s), 0)),
                  pl.BlockSpec((c, LANE), lambda b, s: (bwd(b, s), 0)),
                  pl.BlockSpec((GDN_SUB, 4 * HEADS, GDN_CHUNK), lambda b, s: (fwd(b, s), 0, 0)),
                  pl.BlockSpec((GDN_SUB, 4 * HEADS, GDN_CHUNK), lambda b, s: (bwd(b, s), 0, 0)),
                  pl.BlockSpec((SUB, LANE), lambda b, s: (0, 0)),
                  pl.BlockSpec((4 * HEADS, LANE), lambda b, s: (0, 0))],
        out_specs=[pl.BlockSpec((c, BRANCH_W), lambda b, s: (fwd(b, s), 0)),
                   pl.BlockSpec((c, BRANCH_W), lambda b, s: (bwd(b, s), 0))],
        out_shape=[jax.ShapeDtypeStruct((t, BRANCH_W), F32), jax.ShapeDtypeStruct((t, BRANCH_W), F32)],
        scratch_shapes=[pltpu.VMEM((2, HEADS, HEAD_D, HEAD_D), F32)],
        compiler_params=_cparams(("arbitrary", "arbitrary")),
        name="gdn_scan",
    )(qkv, qkv, ab, ab, ab_rows, ab_rows, par_r, par_c)


def _hyfilt_body(z_ref, aux_ref, w1_ref, b1_ref, w2_ref, b2_ref, w3_ref, b3_ref, w4_ref, fr_ref, dl_ref, o_ref):
    fr = fr_ref[...]
    h = jnp.sin(fr * (_dot(z_ref[...], w1_ref[...], hi=True) + b1_ref[...]))
    h = jnp.sin(fr * (_dot(h, w2_ref[...], hi=True) + b2_ref[...]))
    h = jnp.sin(fr * (_dot(h, w3_ref[...], hi=True) + b3_ref[...]))
    taps = _dot(h, w4_ref[...], hi=True) * jnp.exp(-aux_ref[:, 0:1] * dl_ref[...])
    w = BRANCH_W
    negative = aux_ref[:, 1:2] > 0.5
    keep = aux_ref[:, 2:3]
    for o in range(HY_ORDER):
        fwd = taps[:, o * 2 * w:o * 2 * w + w]
        bwd = taps[:, o * 2 * w + w:(o + 1) * 2 * w]
        o_ref[:, o * w:(o + 1) * w] = jnp.where(negative, bwd, fwd) * keep


def hyena_filter(n, w1p, b1, w2, b2, w3, b3, w4, fr, layer):
    row = jnp.arange(2 * n)
    src = jnp.where(row <= n, row, 2 * n - row)
    pos = jnp.where(row == n, 0, src).astype(F32)
    tt = pos / max(n - 1, 1)
    ang = (2.0 * math.pi / n) * pos[:, None] * jnp.linspace(1e-4, HY_BANDS - 1, HY_BANDS, dtype=F32)
    z = jnp.concatenate([tt[:, None], jnp.cos(ang), -jnp.sin(ang), jnp.zeros((2 * n, LANE - HY_EMB), F32)], -1)
    aux = jnp.stack([tt, (row > n).astype(F32), (row != n).astype(F32)], 1)
    aux = jnp.pad(aux, ((0, 0), (0, SUB - 3)))
    deltas = jnp.abs(jnp.linspace(HY_MIN_DECAY, HY_MAX_DECAY, BRANCH_W, dtype=F32))
    dl = jnp.tile(deltas, 2 * HY_ORDER)[None, :]
    r = 512
    wo = 2 * HY_ORDER * BRANCH_W
    full = lambda shape: pl.BlockSpec((None,) + shape, lambda i: (layer,) + (0,) * len(shape))
    return pl.pallas_call(
        _hyfilt_body,
        grid=(2 * n // r,),
        in_specs=[pl.BlockSpec((r, LANE), lambda i: (i, 0)),
                  pl.BlockSpec((r, SUB), lambda i: (i, 0)),
                  full((LANE, HY_FH)), full((1, HY_FH)), full((HY_FH, HY_FH)), full((1, HY_FH)),
                  full((HY_FH, HY_FH)), full((1, HY_FH)), full((HY_FH, wo)), full((1, HY_FH)),
                  pl.BlockSpec((1, wo), lambda i: (0, 0))],
        out_specs=pl.BlockSpec((r, HY_ORDER * BRANCH_W), lambda i: (i, 0)),
        out_shape=jax.ShapeDtypeStruct((2 * n, HY_ORDER * BRANCH_W), F32),
        compiler_params=_cparams(("arbitrary",)),
        name="hyena_filter",
    )(z, aux, w1p, b1, w2, b2, w3, b3, w4, fr, dl)


@functools.lru_cache(maxsize=None)
def _dense_dft_tables(n):
    nn = 2 * n
    k = np.arange(nn)[:, None].astype(np.float64)
    m = np.arange(nn)[None, :].astype(np.float64)
    ang = -2.0 * np.pi * k * m / nn
    wr, wi = np.cos(ang), np.sin(ang)
    f_real = np.concatenate([wr, wi], 0)
    wr_h, wi_h = wr[:, :n], wi[:, :n]
    f_fwd = np.block([[wr_h, -wi_h], [wi_h, wr_h]])
    cr, ci = wr.T[:n] / nn, -wi.T[:n] / nn
    f_inv = np.block([[cr, -ci], [ci, cr]])
    return (np.asarray(f_real, np.float32), np.asarray(f_fwd, np.float32), np.asarray(f_inv, np.float32))


@functools.lru_cache(maxsize=None)
def _two_stage_dft_tables(n):
    nn = 2 * n
    n2c = FFT_N2
    n1c = nn // n2c
    n1h = n1c // 2
    k1 = np.arange(n1c).astype(np.float64)
    n1 = np.arange(n1c).astype(np.float64)
    n2 = np.arange(n2c).astype(np.float64)
    ang = -2.0 * np.pi * (k1[None, :, None] * n1[None, None, :] / n1c + n2[:, None, None] * k1[None, :, None] / nn)
    mr, mi = np.cos(ang), np.sin(ang)
    f1_real = np.concatenate([mr, mi], 1)
    mrh, mih = mr[:, :, :n1h], mi[:, :, :n1h]
    f1_cplx = np.concatenate([np.concatenate([mrh, -mih], 2), np.concatenate([mih, mrh], 2)], 1)
    gr = np.transpose(mr, (0, 2, 1))[:, :n1h, :] / nn
    gi = -np.transpose(mi, (0, 2, 1))[:, :n1h, :] / nn
    g1 = np.concatenate([np.concatenate([gr, -gi], 2), np.concatenate([gi, gr], 2)], 1)
    k2 = np.arange(n2c).astype(np.float64)
    a2 = -2.0 * np.pi * k2[:, None] * n2[None, :] / n2c
    fr, fi = np.cos(a2), np.sin(a2)
    f2 = np.block([[fr, -fi], [fi, fr]])
    f2i = np.block([[fr.T, fi.T], [-fi.T, fr.T]])
    f32 = lambda a: np.asarray(a, np.float32)
    return f32(f1_real), f32(f1_cplx), f32(g1), f32(f2), f32(f2i)


def _spec_dense_body(f_ref, x_ref, o_ref):
    o_ref[...] = _dot(f_ref[...], x_ref[...], hi=True)


def hyena_spec_dense(full, n):
    f_real, _, _ = _dense_dft_tables(n)
    nn, cols = full.shape
    return pl.pallas_call(
        _spec_dense_body,
        grid=(cols // LANE,),
        in_specs=[pl.BlockSpec((2 * nn, nn), lambda j: (0, 0)),
                  pl.BlockSpec((nn, LANE), lambda j: (0, j))],
        out_specs=pl.BlockSpec((2 * nn, LANE), lambda j: (0, j)),
        out_shape=jax.ShapeDtypeStruct((2 * nn, cols), F32),
        compiler_params=_cparams(("arbitrary",)),
        name="hyena_spec_dense",
    )(jnp.asarray(f_real), full)


def _conv_dense_body(*refs, has_mult):
    z_ref, h_ref, ff_ref, fi_ref, bias_ref = refs[:5]
    m_ref = refs[5] if has_mult else None
    o_ref = refs[-1]
    z = z_ref[...]
    nn = z.shape[0]
    x = _dot(ff_ref[...], z, hi=True)
    xr, xi = x[:nn], x[nn:]
    hr, hi_ = h_ref[0:nn, :], h_ref[nn:2 * nn, :]
    y = _dot(fi_ref[...], jnp.concatenate([xr * hr - xi * hi_, xr * hi_ + xi * hr], 0), hi=True)
    out = y + z * bias_ref[...]
    if has_mult:
        out = out * m_ref[...]
    o_ref[...] = out


def hyena_conv_dense(zsrc, zcol, row0, n, n_batch, spec, bias3, layer, order, prev, mult=None):
    _, f_fwd, f_inv = _dense_dft_tables(n)
    nn = 2 * n
    rb, cb = row0 // nn, zcol // LANE
    wb = BRANCH_W // LANE
    in_specs = [pl.BlockSpec((nn, LANE), lambda p, j: (rb + p, cb + j)),
                pl.BlockSpec((2 * nn, LANE), lambda p, j: (0, order * wb + j)),
                pl.BlockSpec((2 * nn, nn), lambda p, j: (0, 0)),
                pl.BlockSpec((nn, 2 * nn), lambda p, j: (0, 0)),
                pl.BlockSpec((None, 1, LANE), lambda p, j: (layer * HY_ORDER + order, 0, j))]
    args = [zsrc, spec, jnp.asarray(f_fwd), jnp.asarray(f_inv), bias3]
    if mult is not None:
        mb = mult[1] // LANE
        in_specs.append(pl.BlockSpec((nn, LANE), lambda p, j: (rb + p, mb + j)))
        args.append(mult[0])
    in_specs.append(pl.BlockSpec(memory_space=pl.ANY))
    args.append(prev)
    return pl.pallas_call(
        functools.partial(_conv_dense_body, has_mult=mult is not None),
        grid=(n_batch // 2, wb),
        in_specs=in_specs,
        out_specs=pl.BlockSpec((nn, LANE), lambda p, j: (rb + p, j)),
        out_shape=jax.ShapeDtypeStruct(prev.shape, F32),
        input_output_aliases={len(args) - 1: 0},
        compiler_params=_cparams(("arbitrary", "arbitrary")),
        name="hyena_conv_dense",
    )(*args)


def _spec_fft_body(x_ref, f1_ref, f2_ref, o_ref, a_ref):
    n1c = o_ref.shape[0]

    def stage1(g, carry):
        n2s = [g * FFT_UNROLL + u for u in range(FFT_UNROLL)]
        xs = [x_ref[pl.ds(n2, n1c, stride=FFT_N2), :] for n2 in n2s]
        res = [_dot(f1_ref[n2], x) for n2, x in zip(n2s, xs)]
        for n2, r in zip(n2s, res):
            a_ref[pl.ds(pl.multiple_of(n2 * 2 * n1c, 2 * n1c), 2 * n1c), :] = r
        return carry

    lax.fori_loop(0, FFT_N2 // FFT_UNROLL, stage1, 0, unroll=2)
    g2 = FFT_UNROLL // 2

    def stage2(g, carry):
        k1s = [g * g2 + u for u in range(g2)]
        blks = [jnp.concatenate([a_ref[pl.ds(k1, FFT_N2, stride=2 * n1c), :],
                                 a_ref[pl.ds(n1c + k1, FFT_N2, stride=2 * n1c), :]], 0) for k1 in k1s]
        res = [_dot(f2_ref[...], blk) for blk in blks]
        for k1, r in zip(k1s, res):
            o_ref[k1] = r
        return carry

    lax.fori_loop(0, n1c // g2, stage2, 0, unroll=2)


def hyena_spec_fft(full, n):
    f1_real, _, _, f2, _ = _two_stage_dft_tables(n)
    nn, cols = full.shape
    n1c = nn // FFT_N2
    const = lambda shape: pl.BlockSpec(shape, lambda j: (0,) * len(shape), pipeline_mode=pl.Buffered(1))
    return pl.pallas_call(
        _spec_fft_body,
        grid=(cols // LANE,),
        in_specs=[pl.BlockSpec((nn, LANE), lambda j: (0, j)),
                  const((FFT_N2, 2 * n1c, n1c)), const((2 * FFT_N2, 2 * FFT_N2))],
        out_specs=pl.BlockSpec((n1c, 2 * FFT_N2, LANE), lambda j: (0, 0, j)),
        out_shape=jax.ShapeDtypeStruct((n1c, 2 * FFT_N2, cols), F32),
        scratch_shapes=[pltpu.VMEM((FFT_N2 * 2 * n1c, LANE), F32)],
        compiler_params=_cparams(("arbitrary",)),
        name="hyena_spec_fft",
    )(full, jnp.asarray(f1_real, BF16), jnp.asarray(f2, BF16))


def _conv_fft_body(*refs, has_mult):
    z_ref, h_ref, f1_ref, f2_ref, f2i_ref, g1_ref, bias_ref = refs[:7]
    m_ref = refs[7] if has_mult else None
    o_ref, a_ref, b_ref = refs[-3], refs[-2], refs[-1]
    n1c = h_ref.shape[0]
    n1h = n1c // 2
    n2c = FFT_N2
    n = n1h * n2c

    def slab(n2):
        return pl.ds(pl.multiple_of(n2 * 2 * n1c, 2 * n1c), 2 * n1c)

    def stage1(g, carry):
        n2s = [g * FFT_UNROLL + u for u in range(FFT_UNROLL)]
        xs = [jnp.concatenate([z_ref[pl.ds(n2, n1h, stride=n2c), :], z_ref[pl.ds(n + n2, n1h, stride=n2c), :]], 0)
              for n2 in n2s]
        res = [_dot(f1_ref[n2], x) for n2, x in zip(n2s, xs)]
        for n2, r in zip(n2s, res):
            a_ref[slab(n2), :] = r
        return carry

    lax.fori_loop(0, n2c // FFT_UNROLL, stage1, 0, unroll=2)
    g2 = FFT_UNROLL // 2

    def stage2(g, carry):
        k1s = [g * g2 + u for u in range(g2)]
        rows = [(pl.ds(k1, n2c, stride=2 * n1c), pl.ds(n1c + k1, n2c, stride=2 * n1c)) for k1 in k1s]
        blks = [jnp.concatenate([a_ref[re, :], a_ref[im, :]], 0) for re, im in rows]
        xs = [_dot(f2_ref[...], blk) for blk in blks]
        ys = []
        for k1, x in zip(k1s, xs):
            xr, xi = x[:n2c], x[n2c:]
            hr, hi_ = h_ref[k1, 0:n2c, :], h_ref[k1, n2c:2 * n2c, :]
            ys.append(jnp.concatenate([xr * hr - xi * hi_, xr * hi_ + xi * hr], 0))
        bs = [_dot(f2i_ref[...], y) for y in ys]
        for (re, im), b in zip(rows, bs):
            b_ref[re, :] = b[:n2c]
            b_ref[im, :] = b[n2c:]
        return carry

    lax.fori_loop(0, n1c // g2, stage2, 0, unroll=2)
    bias = bias_ref[...]

    def stage3(g, carry):
        n2s = [g * FFT_UNROLL + u for u in range(FFT_UNROLL)]
        blks = [b_ref[slab(n2), :] for n2 in n2s]
        ys = [_dot(g1_ref[n2], blk) for n2, blk in zip(n2s, blks)]
        outs = []
        for n2, y in zip(n2s, ys):
            for part, rows in ((y[:n1h], pl.ds(n2, n1h, stride=n2c)), (y[n1h:], pl.ds(n + n2, n1h, stride=n2c))):
                out = part + z_ref[rows, :] * bias
                if has_mult:
                    out = out * m_ref[rows, :]
                outs.append((rows, out))
        for rows, out in outs:
            o_ref[rows, :] = out
        return carry

    lax.fori_loop(0, n2c // FFT_UNROLL, stage3, 0, unroll=2)


def hyena_conv_fft(zsrc, zcol, n, n_batch, spec, bias3, layer, order, t_rows, mult=None):
    _, f1_cplx, g1, f2, f2i = _two_stage_dft_tables(n)
    n1c = 2 * n // FFT_N2
    cb = zcol // LANE
    wb = BRANCH_W // LANE
    const = lambda shape: pl.BlockSpec(shape, lambda j, p: (0,) * len(shape), pipeline_mode=pl.Buffered(1))
    in_specs = [pl.BlockSpec((2 * n, LANE), lambda j, p: (p, cb + j)),
                pl.BlockSpec((n1c, 2 * FFT_N2, LANE), lambda j, p: (0, 0, order * wb + j),
                             pipeline_mode=pl.Buffered(1)),
                const((FFT_N2, 2 * n1c, n1c)), const((2 * FFT_N2, 2 * FFT_N2)), const((2 * FFT_N2, 2 * FFT_N2)),
                const((FFT_N2, n1c, 2 * n1c)),
                pl.BlockSpec((None, 1, LANE), lambda j, p: (layer * HY_ORDER + order, 0, j))]
    args = [zsrc, spec, jnp.asarray(f1_cplx, BF16), jnp.asarray(f2, BF16), jnp.asarray(f2i, BF16),
            jnp.asarray(g1, BF16), bias3]
    if mult is not None:
        mb = mult[1] // LANE
        in_specs.append(pl.BlockSpec((2 * n, LANE), lambda j, p: (p, mb + j)))
        args.append(mult[0])
    return pl.pallas_call(
        functools.partial(_conv_fft_body, has_mult=mult is not None),
        grid=(wb, n_batch // 2),
        in_specs=in_specs,
        out_specs=pl.BlockSpec((2 * n, LANE), lambda j, p: (p, j)),
        out_shape=jax.ShapeDtypeStruct((t_rows, BRANCH_W), F32),
        scratch_shapes=[pltpu.VMEM((FFT_N2 * 2 * n1c, LANE), F32)] * 2,
        compiler_params=_cparams(("arbitrary", "arbitrary")),
        name="hyena_conv_fft",
    )(*args)


def _swap_pairs(x):
    w = x.shape[-1]
    lane = lax.broadcasted_iota(jnp.int32, x.shape, x.ndim - 1)
    return jnp.where(lane % 2 == 0, pltpu.roll(x, w - 1, x.ndim - 1), pltpu.roll(x, 1, x.ndim - 1))


def _attn_prep_body(g_ref, dq_ref, dk_ref, dv_ref, cg_ref, sg_ref, cd_ref, sd_ref, qn_ref, kn_ref,
                    qg_ref, kg_ref, vg_ref, qd_ref, kd_ref, vd_ref, *, lat_blocks):
    is_lat = pl.program_id(0) < lat_blocks
    cg = jnp.where(is_lat, cg_ref[...], 1.0)
    sg = jnp.where(is_lat, sg_ref[...], 0.0)
    cd = jnp.where(is_lat, cd_ref[...], 1.0)
    sd = jnp.where(is_lat, sd_ref[...], 0.0)

    def rope(x, cs, sn):
        return x * cs + _swap_pairs(x) * sn

    def rms(x, w):
        return x * lax.rsqrt(jnp.mean(x * x, axis=-1, keepdims=True) + EPS) * w

    for h in range(HEADS):
        sl = slice(h * HEAD_D, (h + 1) * HEAD_D)
        q = rope(rms(g_ref[:, sl], qn_ref[...]), cg, sg)
        qg_ref[:, sl] = (q * HEAD_D ** -0.5).astype(BF16)
        qd_ref[:, sl] = (rope(dq_ref[:, sl], cd, sd) * DIFF_QK ** -0.5).astype(BF16)
        kd_ref[:, sl] = rope(dk_ref[:, sl], cd, sd).astype(BF16)
    for h in range(GQA_KV):
        sl = slice(h * HEAD_D, (h + 1) * HEAD_D)
        kin = g_ref[:, BRANCH_W + h * HEAD_D:BRANCH_W + (h + 1) * HEAD_D]
        kg_ref[:, sl] = rope(rms(kin, kn_ref[...]), cg, sg).astype(BF16)
    vg_ref[...] = g_ref[:, BRANCH_W + GQA_KV * HEAD_D:BRANCH_W + 2 * GQA_KV * HEAD_D].astype(BF16)
    vd_ref[...] = dv_ref[...].astype(BF16)


def attn_prep(p, ropes, qn3, kn3, layer, n_lat, n_ctx, n_batch):
    t = p.shape[0]
    r = 256 if n_ctx % 256 == 0 else n_ctx
    nlb, ncb = n_lat // r, n_ctx // r
    lat_blocks = n_batch * nlb
    kvw = GQA_KV * HEAD_D
    w = BRANCH_W

    def kv_row(i):
        lat = (i // nlb) * (nlb + ncb) + ncb + i % nlb
        j = i - lat_blocks
        ctx = (j // ncb) * (nlb + ncb) + j % ncb
        return jnp.where(i < lat_blocks, lat, ctx)

    rope_spec = pl.BlockSpec((r, LANE), lambda i: (jnp.where(i < lat_blocks, i % nlb, 0), 0))
    nkv = n_batch * (n_lat + n_ctx)
    return pl.pallas_call(
        functools.partial(_attn_prep_body, lat_blocks=lat_blocks),
        grid=(t // r,),
        in_specs=[pl.BlockSpec((r, 2 * w), lambda i: (i, C_GQA_QKV // (2 * w))),
                  pl.BlockSpec((r, w), lambda i: (i, C_DIFF_Q // w)),
                  pl.BlockSpec((r, w), lambda i: (i, C_DIFF_K // w)),
                  pl.BlockSpec((r, w), lambda i: (i, C_DIFF_V // w)),
                  rope_spec, rope_spec, rope_spec, rope_spec,
                  pl.BlockSpec((None, 1, LANE), lambda i: (layer, 0, 0)),
                  pl.BlockSpec((None, 1, LANE), lambda i: (layer, 0, 0))],
        out_specs=[pl.BlockSpec((r, w), lambda i: (i, 0)),
                   pl.BlockSpec((r, kvw), lambda i: (kv_row(i), 0)),
                   pl.BlockSpec((r, kvw), lambda i: (kv_row(i), 0)),
                   pl.BlockSpec((r, w), lambda i: (i, 0)),
                   pl.BlockSpec((r, w), lambda i: (kv_row(i), 0)),
                   pl.BlockSpec((r, w), lambda i: (kv_row(i), 0))],
        out_shape=[jax.ShapeDtypeStruct((t, w), BF16), jax.ShapeDtypeStruct((nkv, kvw), BF16),
                   jax.ShapeDtypeStruct((nkv, kvw), BF16), jax.ShapeDtypeStruct((t, w), BF16),
                   jax.ShapeDtypeStruct((nkv, w), BF16), jax.ShapeDtypeStruct((nkv, w), BF16)],
        compiler_params=_cparams(("arbitrary",)),
        name="attn_prep",
    )(p, p, p, p, *ropes, qn3, kn3)


def _softmax_parts(s):
    e = jnp.exp(s - jnp.max(s, axis=-1, keepdims=True))
    return e, jnp.sum(e, axis=-1, keepdims=True)


def _gqa_body(q_ref, k_ref, v_ref, *rest):
    o_ref = rest[-1]
    group = HEADS // GQA_KV
    for kvh in range(GQA_KV):
        k = k_ref[:, kvh * HEAD_D:(kvh + 1) * HEAD_D]
        v = v_ref[:, kvh * HEAD_D:(kvh + 1) * HEAD_D]
        for g in range(group):
            sl = slice((kvh * group + g) * HEAD_D, (kvh * group + g + 1) * HEAD_D)
            s = lax.dot_general(q_ref[:, sl], k, (((1,), (1,)), ((), ())), preferred_element_type=F32)
            e, l = _softmax_parts(s)
            o_ref[:, sl] = jnp.dot(e.astype(BF16), v, preferred_element_type=F32) / l


def _diff_body(q_ref, k_ref, v_ref, lam_ref, *rest, lam_init):
    o_ref = rest[-1]
    lam4 = lam_ref[...]
    lam = (jnp.exp(jnp.sum(lam4[0:1] * lam4[1:2], axis=-1, keepdims=True))
           - jnp.exp(jnp.sum(lam4[2:3] * lam4[3:4], axis=-1, keepdims=True)) + lam_init)
    dn = (((1,), (1,)), ((), ()))
    for h in range(HEADS):
        sl = slice(h * HEAD_D, (h + 1) * HEAD_D)
        q = q_ref[:, sl]
        k = k_ref[:, sl]
        v = v_ref[:, sl]
        first = lax.broadcasted_iota(jnp.int32, q.shape, 1) < DIFF_QK
        zero = jnp.zeros_like(q)
        e1, l1 = _softmax_parts(lax.dot_general(jnp.where(first, q, zero), k, dn, preferred_element_type=F32))
        e2, l2 = _softmax_parts(lax.dot_general(jnp.where(first, zero, q), k, dn, preferred_element_type=F32))
        o1 = jnp.dot(e1.astype(BF16), v, preferred_element_type=F32) / l1
        o2 = jnp.dot(e2.astype(BF16), v, preferred_element_type=F32) / l2
        o_ref[:, sl] = o1 - lam * o2


def attention(body, q, k, v, extra, extra_specs, q_row0, nq, kv_per_batch, kv_len, n_batch, tq, name, prev=None):
    t, w = q.shape
    qb0 = q_row0 // tq
    nqb = nq // tq
    kvb = kv_per_batch // kv_len
    in_specs = [pl.BlockSpec((tq, w), lambda b, i: (qb0 + b * nqb + i, 0)),
                pl.BlockSpec((kv_len, k.shape[1]), lambda b, i: (b * kvb, 0)),
                pl.BlockSpec((kv_len, v.shape[1]), lambda b, i: (b * kvb, 0))] + extra_specs
    args = [q, k, v, *extra]
    aliases = {}
    if prev is not None:
        in_specs.append(pl.BlockSpec(memory_space=pl.ANY))
        args.append(prev)
        aliases = {len(args) - 1: 0}
    return pl.pallas_call(
        body,
        grid=(n_batch, nqb),
        in_specs=in_specs,
        out_specs=pl.BlockSpec((tq, w), lambda b, i: (qb0 + b * nqb + i, 0)),
        out_shape=jax.ShapeDtypeStruct((t, w), F32),
        input_output_aliases=aliases,
        compiler_params=_cparams(("arbitrary", "arbitrary")),
        name=name,
    )(*args)


def _merge_body(h_ref, mod_ref, mg_ref, of_ref, ob_ref, ggate_ref, y1_ref, x2_ref, hgate_ref, oc_ref, cgate_ref,
                od_ref, dgate_ref, gnorm_ref, dnorm_ref, wbr_ref, wout_ref, lng_ref, lnb_ref, o_ref, *, diff_scale):
    def rms_heads(x, w):
        parts = []
        for h in range(HEADS):
            xh = x[:, h * HEAD_D:(h + 1) * HEAD_D]
            parts.append(xh * lax.rsqrt(jnp.mean(xh * xh, axis=-1, keepdims=True) + EPS) * w)
        return jnp.concatenate(parts, -1)

    ys = (rms_heads(of_ref[...] + ob_ref[...], gnorm_ref[...]) * _silu(ggate_ref[...]),
          x2_ref[...] * y1_ref[...] * _silu(hgate_ref[...]),
          oc_ref[...] * _silu(cgate_ref[...]),
          rms_heads(od_ref[...], dnorm_ref[...]) * diff_scale * _silu(dgate_ref[...]))
    acc = None
    for n in range(N_BRANCH):
        proj = jnp.dot(ys[n].astype(BF16), wbr_ref[n], preferred_element_type=F32)
        term = _sigmoid(mg_ref[:, n * D_MODEL:(n + 1) * D_MODEL]) * proj
        acc = term if acc is None else acc + term
    out = jnp.dot(acc.astype(BF16), wout_ref[...], preferred_element_type=F32)
    x = ALPHA * h_ref[...] + mod_ref[2:3, :] * out
    mu = jnp.mean(x, axis=-1, keepdims=True)
    xc = x - mu
    var = jnp.mean(xc * xc, axis=-1, keepdims=True)
    o_ref[...] = xc * lax.rsqrt(var + EPS) * lng_ref[...] + lnb_ref[...]


def merge_postnorm(h_all, mod3, p, o_f, o_b, y1, xv, oc, od, gnorm3, dnorm3, wbr, wout, lng3, lnb3, layer, lam_init,
                   n_lat, n_batch):
    t, d = h_all.shape
    r = 256 if n_lat % 256 == 0 else 64
    w = BRANCH_W
    lbb = n_lat // r
    row = lambda i: jnp.minimum(i // lbb, n_batch)
    tok = lambda cb: pl.BlockSpec((r, w), lambda i: (i, cb))
    vec = lambda width: pl.BlockSpec((None, 1, width), lambda i: (layer, 0, 0))
    return pl.pallas_call(
        functools.partial(_merge_body, diff_scale=1.0 - lam_init),
        grid=(t // r,),
        in_specs=[pl.BlockSpec((r, d), lambda i: (i, 0)),
                  pl.BlockSpec((None, 3, d), lambda i: (row(i), 0, 0)),
                  pl.BlockSpec((r, N_BRANCH * d), lambda i: (i, C_MERGE // (N_BRANCH * d))),
                  tok(0), tok(0), tok(C_GDN_GATE // w), tok(0), tok(1), tok(C_HY_GATE // w), tok(0),
                  tok(C_GQA_GATE // w), tok(0), tok(C_DIFF_GATE // w),
                  vec(LANE), vec(LANE),
                  pl.BlockSpec((None, N_BRANCH, w, d), lambda i: (layer, 0, 0, 0)),
                  pl.BlockSpec((None, d, d), lambda i: (layer, 0, 0)),
                  vec(d), vec(d)],
        out_specs=pl.BlockSpec((r, d), lambda i: (i, 0)),
        out_shape=jax.ShapeDtypeStruct((t, d), F32),
        compiler_params=_cparams(("arbitrary",)),
        name="merge_postnorm",
    )(h_all, mod3, p, o_f, o_b, p, y1, xv, p, oc, p, od, p, gnorm3, dnorm3, wbr, wout, lng3, lnb3)


def _rope_tables(n_lat, dim):
    rows = n_lat // GRID_W
    row = jnp.repeat(jnp.arange(rows, dtype=F32), GRID_W)
    col = jnp.tile(jnp.arange(GRID_W, dtype=F32), rows)
    half = dim // 2
    inv = ROPE_THETA ** (-jnp.arange(0, half, 2, dtype=F32) / half)
    ang = jnp.concatenate([row[:, None] * inv, col[:, None] * inv], -1)
    cos = jnp.repeat(jnp.cos(ang), 2, axis=-1)
    sin = jnp.repeat(jnp.sin(ang), 2, axis=-1)
    sign = jnp.tile(jnp.array([-1.0, 1.0], F32), dim // 2)
    reps = LANE // dim
    return jnp.tile(cos, (1, reps)), jnp.tile(sin * sign, (1, reps))


def kernel(x, c, ctx, c_ctx, w_ada, b_ada, w_in, gdn_conv, gdn_a_log, gdn_dt_bias, gdn_norm, hy_conv, hy_w1, hy_b1,
           hy_w2, hy_b2, hy_w3, hy_b3, hy_w4, hy_freq, hy_bias, gqa_qn, gqa_kn, diff_lam, diff_norm, w_br, w_out,
           ln_g, ln_b):
    nb, n_lat, d = x.shape
    n_ctx = ctx.shape[1]
    t_lat, t_ctx = nb * n_lat, nb * n_ctx
    depth = w_in.shape[0]
    w = BRANCH_W

    w_main = jnp.concatenate([w_in[:, :, O_MERGE:], w_in[:, :, :O_GDN_AB], w_in[:, :, O_GDN_AB + 4 * HEADS:O_MERGE]],
                             axis=2).astype(BF16)
    w_ab = jnp.pad(w_in[:, :, O_GDN_AB:O_GDN_AB + 4 * HEADS], ((0, 0), (0, 0), (0, LANE - 4 * HEADS)))
    wbr_bf = w_br.astype(BF16)
    wout_bf = w_out.astype(BF16)
    b_ada3 = b_ada[:, None, :]
    cvec = jnp.concatenate([c, c_ctx[None, :], jnp.zeros((SUB - nb - 1, d), F32)], 0)
    as3 = lambda a: a[:, None, :]
    gdn_par_r = jnp.pad(jnp.stack([gdn_a_log.reshape(depth, -1), gdn_dt_bias.reshape(depth, -1)], 1),
                        ((0, 0), (0, SUB - 2), (0, LANE - 2 * HEADS)))
    gdn_par_c = jnp.pad(jnp.stack([gdn_a_log.reshape(depth, -1), gdn_dt_bias.reshape(depth, -1)], 2),
                        ((0, 0), (0, 2 * HEADS), (0, LANE - 2)))
    hy_w1p = jnp.pad(hy_w1, ((0, 0), (0, LANE - HY_EMB), (0, 0)))
    hy_bias3 = hy_bias.reshape(depth * HY_ORDER, 1, w)
    ropes = _rope_tables(n_lat, HEAD_D) + _rope_tables(n_lat, DIFF_QK)

    tm = 1024 if (n_lat % 1024 == 0 and t_ctx % 1024 == 0) else n_ctx
    h_all = jnp.concatenate([x.reshape(t_lat, d), ctx.reshape(t_ctx, d)], 0)
    for l in range(depth):
        lam_init = 0.8 - 0.6 * math.exp(-0.3 * l)
        mod3 = ada_mod(cvec, w_ada, b_ada3, l).reshape(SUB, 3, d)
        p, ab = in_proj(h_all, mod3, w_main, w_ab, l, tm, n_lat // tm, nb)

        qkv = dwconv(p, gdn_conv, l, C_GDN_QKV, 3 * w, n_lat, n_ctx, nb, act=True)
        ab_rows = jnp.transpose(ab[:, :4 * HEADS].reshape(-1, GDN_CHUNK, 4 * HEADS), (0, 2, 1))
        o_f, o_b = gdn_scan(qkv, ab, ab_rows, gdn_par_r[l], gdn_par_c[l], n_lat, n_ctx, nb)

        xv = dwconv(p, hy_conv, l, C_HY_XV, 3 * w, n_lat, n_ctx, nb, act=False)
        filt = lambda n: hyena_filter(n, hy_w1p, as3(hy_b1), hy_w2, as3(hy_b2), hy_w3, as3(hy_b3), hy_w4,
                                      as3(hy_freq), l)
        spec_lat = hyena_spec_fft(filt(n_lat), n_lat)
        spec_ctx = hyena_spec_dense(filt(n_ctx), n_ctx)
        z1 = hyena_conv_fft(xv, 2 * w, n_lat, nb, spec_lat, hy_bias3, l, 0, t_lat + t_ctx, mult=(xv, 0))
        z1 = hyena_conv_dense(xv, 2 * w, t_lat, n_ctx, nb, spec_ctx, hy_bias3, l, 0, z1, mult=(xv, 0))
        y1 = hyena_conv_fft(z1, 0, n_lat, nb, spec_lat, hy_bias3, l, 1, t_lat + t_ctx)
        y1 = hyena_conv_dense(z1, 0, t_lat, n_ctx, nb, spec_ctx, hy_bias3, l, 1, y1)

        qg, kg, vg, qd, kd, vd = attn_prep(p, ropes, as3(gqa_qn), as3(gqa_kn), l, n_lat, n_ctx, nb)
        kv_all = n_lat + n_ctx
        tq = min(256, n_ctx)
        lam_spec = [pl.BlockSpec((None, 4, DIFF_QK), lambda b, i: (l, 0, 0))]
        diff_body = functools.partial(_diff_body, lam_init=lam_init)
        oc = attention(_gqa_body, qg, kg, vg, (), [], 0, n_lat, kv_all, kv_all, nb, tq, "gqa_lat")
        oc = attention(_gqa_body, qg, kg, vg, (), [], t_lat, n_ctx, kv_all, n_ctx, nb, tq, "gqa_ctx", prev=oc)
        od = attention(diff_body, qd, kd, vd, (diff_lam,), lam_spec, 0, n_lat, kv_all, kv_all, nb, tq, "diff_lat")
        od = attention(diff_body, qd, kd, vd, (diff_lam,), lam_spec, t_lat, n_ctx, kv_all, n_ctx, nb, tq, "diff_ctx",
                       prev=od)

        h_all = merge_postnorm(h_all, mod3, p, o_f, o_b, y1, xv, oc, od, as3(gdn_norm), as3(diff_norm), wbr_bf, wout_bf,
                               as3(ln_g), as3(ln_b), l, lam_init, n_lat, nb)
    return h_all[:t_lat].reshape(nb, n_lat, d)
```

```python
import functools
import math

import numpy as np
import jax
import jax.numpy as jnp
from jax import lax
from jax.experimental import pallas as pl
from jax.experimental.pallas import tpu as pltpu

F32 = jnp.float32
BF16 = jnp.bfloat16
HI = lax.Precision.HIGHEST

D_MODEL = 1024
DEPTH = 4
GRID_W = 64
BRANCH_W = D_MODEL // 2
N_BRANCH = 4
HEADS = 4
HEAD_D = BRANCH_W // HEADS
GDN_CONV = 4
GDN_CHUNK = 64
GDN_SUB = 4
HY_CONV = 3
HY_EMB = 33
HY_BANDS = (HY_EMB - 1) // 2
HY_FH = 64
HY_ORDER = 2
HY_MIN_DECAY = math.log(1e-2) / 1.5
HY_MAX_DECAY = math.log(1e-2) / 0.3
GQA_KV = 2
DIFF_QK = HEAD_D // 2
ROPE_THETA = 10000.0
EPS = 1e-6
ALPHA = (2.0 * DEPTH) ** 0.25

LANE = 128
SUB = 8
FFT_N2 = 128
FFT_UNROLL = 8
VMEM_LIMIT = 60 * 1024 * 1024

C_MERGE = 0
C_GDN_QKV = 4096
C_GDN_GATE = 5632
C_HY_XV = 6144
C_HY_GATE = 7680
C_GQA_QKV = 8192
C_GQA_GATE = 9216
C_DIFF_Q = 9728
C_DIFF_K = 10240
C_DIFF_V = 10752
C_DIFF_GATE = 11264
N_MAIN = 11776
O_GDN_AB = 1536
O_MERGE = 7696


def _cparams(sem):
    return pltpu.CompilerParams(dimension_semantics=sem, vmem_limit_bytes=VMEM_LIMIT)


def _dot(a, b, hi=False):
    if hi:
        return jnp.dot(a, b, precision=HI, preferred_element_type=F32)
    return jnp.dot(a.astype(BF16), b.astype(BF16), preferred_element_type=F32)


def _dot_nt(a, b, hi=False):
    dn = (((1,), (1,)), ((), ()))
    if hi:
        return lax.dot_general(a, b, dn, precision=HI, preferred_element_type=F32)
    return lax.dot_general(a.astype(BF16), b.astype(BF16), dn, preferred_element_type=F32)


def _dot_tn(a, b):
    return lax.dot_general(a.astype(BF16), b.astype(BF16), (((0,), (0,)), ((), ())), preferred_element_type=F32)


def _sigmoid(x):
    return 1.0 / (1.0 + jnp.exp(-x))


def _silu(x):
    return x * _sigmoid(x)


def _softplus(x):
    return jnp.maximum(x, 0.0) + jnp.log1p(jnp.exp(-jnp.abs(x)))


def _ada_body(c_ref, w_ref, b_ref, o_ref):
    o_ref[...] = _dot(_silu(c_ref[...]), w_ref[...], hi=True) + b_ref[...]


def ada_mod(cvec, w_ada, b_ada3, layer):
    d = cvec.shape[1]
    tn = 512
    return pl.pallas_call(
        _ada_body,
        grid=(3 * d // tn,),
        in_specs=[pl.BlockSpec((SUB, d), lambda j: (0, 0)),
                  pl.BlockSpec((None, d, tn), lambda j: (layer, 0, j)),
                  pl.BlockSpec((None, 1, tn), lambda j: (layer, 0, j))],
        out_specs=pl.BlockSpec((SUB, tn), lambda j: (0, j)),
        out_shape=jax.ShapeDtypeStruct((SUB, 3 * d), F32),
        compiler_params=_cparams(("arbitrary",)),
        name="ada_mod",
    )(cvec, w_ada, b_ada3)


def _inproj_body(h_ref, mod_ref, w_ref, wab_ref, o_ref, ab_ref, u_ref):
    @pl.when(pl.program_id(1) == 0)
    def _():
        x = h_ref[...]
        mu = jnp.mean(x, axis=-1, keepdims=True)
        xc = x - mu
        var = jnp.mean(xc * xc, axis=-1, keepdims=True)
        u = xc * lax.rsqrt(var + EPS) * (1.0 + mod_ref[1:2, :]) + mod_ref[0:1, :]
        u_ref[...] = u.astype(BF16)
        ab_ref[...] = _dot(u, wab_ref[...], hi=True)

    o_ref[...] = jnp.dot(u_ref[...], w_ref[...], preferred_element_type=F32)


def in_proj(h_all, mod3, w_main, w_ab, layer, tm, lat_blocks_per_batch, n_batch):
    t, d = h_all.shape
    tn = 512
    n_main = w_main.shape[2]
    row = lambda i: jnp.minimum(i // lat_blocks_per_batch, n_batch)
    return pl.pallas_call(
        _inproj_body,
        grid=(t // tm, n_main // tn),
        in_specs=[pl.BlockSpec((tm, d), lambda i, j: (i, 0)),
                  pl.BlockSpec((None, 3, d), lambda i, j: (row(i), 0, 0)),
                  pl.BlockSpec((None, d, tn), lambda i, j: (layer, 0, j)),
                  pl.BlockSpec((None, d, LANE), lambda i, j: (layer, 0, 0))],
        out_specs=[pl.BlockSpec((tm, tn), lambda i, j: (i, j)),
                   pl.BlockSpec((tm, LANE), lambda i, j: (i, 0))],
        out_shape=[jax.ShapeDtypeStruct((t, n_main), F32), jax.ShapeDtypeStruct((t, LANE), F32)],
        scratch_shapes=[pltpu.VMEM((tm, d), BF16)],
        compiler_params=_cparams(("arbitrary", "arbitrary")),
        name="in_proj",
    )(h_all, mod3, w_main, w_ab)


def _dwconv_body(xp_ref, x_ref, xn_ref, w_ref, o_ref, pad_ref, *, taps, pad_l, t_lat, n_lat, n_ctx, sb, act):
    i = pl.program_id(0)
    r = x_ref.shape[0]
    pad_ref[0:SUB, :] = xp_ref[...]
    pad_ref[SUB:SUB + r, :] = x_ref[...]
    pad_ref[SUB + r:2 * SUB + r, :] = xn_ref[...]
    row = lax.broadcasted_iota(jnp.int32, (sb, 1), 0)
    for k in range(r // sb):
        g0 = i * r + k * sb
        in_lat = g0 < t_lat
        starts = jnp.where(in_lat, g0 % n_lat == 0, (g0 - t_lat) % n_ctx == 0)
        ends = jnp.where(in_lat, (g0 + sb) % n_lat == 0, (g0 + sb - t_lat) % n_ctx == 0)
        acc = None
        for j in range(taps):
            d = j - pad_l
            off = SUB + k * sb + d
            xs = pad_ref[off:off + sb, :]
            if d < 0:
                xs = jnp.where(jnp.logical_and(starts, row < -d), 0.0, xs)
            elif d > 0:
                xs = jnp.where(jnp.logical_and(ends, row >= sb - d), 0.0, xs)
            term = w_ref[j:j + 1, :] * xs
            acc = term if acc is None else acc + term
        if act:
            acc = _silu(acc)
        o_ref[k * sb:(k + 1) * sb, :] = acc


def dwconv(p, w_conv, layer, col0, width, n_lat, n_ctx, n_batch, act):
    t = p.shape[0]
    taps = w_conv.shape[1]
    sb = min(256, n_ctx)
    r = 1024 if t % 1024 == 0 else sb
    lw = 512
    cb = col0 // lw
    rs = r // SUB
    body = functools.partial(_dwconv_body, taps=taps, pad_l=(taps - 1) // 2, t_lat=n_batch * n_lat, n_lat=n_lat,
                             n_ctx=n_ctx, sb=sb, act=act)
    return pl.pallas_call(
        body,
        grid=(t // r, width // lw),
        in_specs=[pl.BlockSpec((SUB, lw), lambda i, j: (jnp.maximum(i * rs - 1, 0), cb + j)),
                  pl.BlockSpec((r, lw), lambda i, j: (i, cb + j)),
                  pl.BlockSpec((SUB, lw), lambda i, j: (jnp.minimum((i + 1) * rs, t // SUB - 1), cb + j)),
                  pl.BlockSpec((None, taps, lw), lambda i, j: (layer, 0, j))],
        out_specs=pl.BlockSpec((r, lw), lambda i, j: (i, j)),
        out_shape=jax.ShapeDtypeStruct((t, width), F32),
        scratch_shapes=[pltpu.VMEM((r + 2 * SUB, lw), F32)],
        compiler_params=_cparams(("arbitrary", "arbitrary")),
        name="dwconv",
    )(p, p, p, w_conv)


def _gdn_body(qf_ref, qb_ref, abcf_ref, abcb_ref, abrf_ref, abrb_ref, pr_ref, pc_ref, of_ref, ob_ref, s_ref):
    c = GDN_CHUNK

    @pl.when(pl.program_id(1) == 0)
    def _():
        s_ref[...] = jnp.zeros_like(s_ref)

    ii = lax.broadcasted_iota(jnp.int32, (c, c), 0)
    jj = lax.broadcasted_iota(jnp.int32, (c, c), 1)
    lmat = (jj <= ii).astype(F32)
    eye = (jj == ii).astype(F32)
    alr, dtr = pr_ref[0:1, :], pr_ref[1:2, :]
    alc, dtc = pc_ref[:, 0:1], pc_ref[:, 1:2]
    chains = []
    for d in range(2):
        qkv_ref = (qf_ref, qb_ref)[d]
        abc_ref = (abcf_ref, abcb_ref)[d]
        abr_ref = (abrf_ref, abrb_ref)[d]
        incl = (jj <= ii) if d == 0 else (jj >= ii)
        strict = (jj < ii) if d == 0 else (jj > ii)
        for j in range(GDN_SUB):
            rows = slice(j * c, (j + 1) * c)
            abc = abc_ref[rows, :]
            abr = abr_ref[j]
            g_c = -jnp.exp(alr) * _softplus(abc + dtr)
            g_r = -jnp.exp(alc) * _softplus(abr + dtc)
            cum_c = _dot(lmat, g_c, hi=True)
            cum_r = _dot_nt(g_r, lmat, hi=True)
            if d == 1:
                cum_c = cum_c[c - 1:c, :] - cum_c + g_c
                cum_r = cum_r[:, c - 1:c] - cum_r + g_r
            beta_all = _sigmoid(abc)
            for h in range(HEADS):
                idx = HEADS * d + h
                q = qkv_ref[rows, h * HEAD_D:(h + 1) * HEAD_D]
                k = qkv_ref[rows, BRANCH_W + h * HEAD_D:BRANCH_W + (h + 1) * HEAD_D]
                v = qkv_ref[rows, 2 * BRANCH_W + h * HEAD_D:2 * BRANCH_W + (h + 1) * HEAD_D]
                q = q * lax.rsqrt(jnp.sum(q * q, axis=-1, keepdims=True) + EPS) * (HEAD_D ** -0.5)
                k = k * lax.rsqrt(jnp.sum(k * k, axis=-1, keepdims=True) + EPS)
                cc = cum_c[:, idx:idx + 1]
                cr = cum_r[idx:idx + 1, :]
                dec = jnp.exp(jnp.where(incl, cc - cr, -1e30))
                beta = beta_all[:, 2 * HEADS + idx:2 * HEADS + idx + 1]
                ecum = jnp.exp(cc)
                tot = cc[c - 1:c, :] if d == 0 else cc[0:1, :]
                chains.append(dict(d=d, h=h, j=j, rows=rows, q=q, k=k, dec=dec, strict=strict, beta=beta, ecum=ecum,
                                   tot=tot, rhs=jnp.concatenate([k * (beta * ecum), v * beta], 1),
                                   k_tail=k * jnp.exp(tot - cc)))
    for ch in chains:
        ch["kk"] = _dot_nt(ch["k"], ch["k"])
        ch["qk"] = _dot_nt(ch["q"], ch["k"])
    for ch in chains:
        ch["p"] = -jnp.where(ch["strict"], ch["beta"] * ch["kk"] * ch["dec"], 0.0)
        ch["inv"] = eye + ch["p"]
    for _ in range(int(math.log2(c)) - 1):
        for ch in chains:
            ch["p"] = _dot(ch["p"], ch["p"])
        for ch in chains:
            ch["inv"] = ch["inv"] + _dot(ch["inv"], ch["p"])
    for ch in chains:
        ch["wu"] = _dot(ch["inv"], ch["rhs"])
        ch["lhs"] = jnp.concatenate([ch["wu"][:, :HEAD_D], ch["q"] * ch["ecum"]], 0)
    state = {(d, h): s_ref[d, h] for d in range(2) for h in range(HEADS)}
    for step in range(GDN_SUB):
        cur = [ch for ch in chains if ch["j"] == (step if ch["d"] == 0 else GDN_SUB - 1 - step)]
        for ch in cur:
            ch["ws"] = _dot(ch["lhs"], state[ch["d"], ch["h"]])
        for ch in cur:
            ch["v_new"] = ch["wu"][:, HEAD_D:] - ch["ws"][:c]
        for ch in cur:
            out_ref = (of_ref, ob_ref)[ch["d"]]
            h = ch["h"]
            out_ref[ch["rows"], h * HEAD_D:(h + 1) * HEAD_D] = ch["ws"][c:] + _dot(ch["qk"] * ch["dec"], ch["v_new"])
            state[ch["d"], h] = state[ch["d"], h] * jnp.exp(ch["tot"]) + _dot_tn(ch["k_tail"], ch["v_new"])
    for (d, h), val in state.items():
        s_ref[d, h] = val


def gdn_scan(qkv, ab, ab_rows, par_r, par_c, n_lat, n_ctx, n_batch):
    t = qkv.shape[0]
    c = GDN_SUB * GDN_CHUNK
    nlc, ncc = n_lat // c, n_ctx // c
    base = n_batch * nlc

    def fwd(b, s):
        return jnp.where(s < ncc, base + b * ncc + s, b * nlc + (s - ncc))

    def bwd(b, s):
        return jnp.where(s < ncc, base + b * ncc + (ncc - 1 - s), b * nlc + (nlc - 1 - (s - ncc)))

    w3 = 3 * BRANCH_W
    return pl.pallas_call(
        _gdn_body,
        grid=(n_batch, ncc + nlc),
        in_specs=[pl.BlockSpec((c, w3), lambda b, s: (fwd(b, s), 0)),
                  pl.BlockSpec((c, w3), lambda b, s: (bwd(b, s), 0)),
                  pl.BlockSpec((c, LANE), lambda b, s: (fwd(b, s), 0)),
                  pl.BlockSpec((c, LANE), lambda b, s: (bwd(b, s), 0)),
                  pl.BlockSpec((GDN_SUB, 4 * HEADS, GDN_CHUNK), lambda b, s: (fwd(b, s), 0, 0)),
                  pl.BlockSpec((GDN_SUB, 4 * HEADS, GDN_CHUNK), lambda b, s: (bwd(b, s), 0, 0)),
                  pl.BlockSpec((SUB, LANE), lambda b, s: (0, 0)),
                  pl.BlockSpec((4 * HEADS, LANE), lambda b, s: (0, 0))],
        out_specs=[pl.BlockSpec((c, BRANCH_W), lambda b, s: (fwd(b, s), 0)),
                   pl.BlockSpec((c, BRANCH_W), lambda b, s: (bwd(b, s), 0))],
        out_shape=[jax.ShapeDtypeStruct((t, BRANCH_W), F32), jax.ShapeDtypeStruct((t, BRANCH_W), F32)],
        scratch_shapes=[pltpu.VMEM((2, HEADS, HEAD_D, HEAD_D), F32)],
        compiler_params=_cparams(("arbitrary", "arbitrary")),
        name="gdn_scan",
    )(qkv, qkv, ab, ab, ab_rows, ab_rows, par_r, par_c)


def _hyfilt_body(z_ref, aux_ref, w1_ref, b1_ref, w2_ref, b2_ref, w3_ref, b3_ref, w4_ref, fr_ref, dl_ref, o_ref):
    fr = fr_ref[...]
    h = jnp.sin(fr * (_dot(z_ref[...], w1_ref[...], hi=True) + b1_ref[...]))
    h = jnp.sin(fr * (_dot(h, w2_ref[...], hi=True) + b2_ref[...]))
    h = jnp.sin(fr * (_dot(h, w3_ref[...], hi=True) + b3_ref[...]))
    taps = _dot(h, w4_ref[...], hi=True) * jnp.exp(-aux_ref[:, 0:1] * dl_ref[...])
    w = BRANCH_W
    negative = aux_ref[:, 1:2] > 0.5
    keep = aux_ref[:, 2:3]
    for o in range(HY_ORDER):
        fwd = taps[:, o * 2 * w:o * 2 * w + w]
        bwd = taps[:, o * 2 * w + w:(o + 1) * 2 * w]
        o_ref[:, o * w:(o + 1) * w] = jnp.where(negative, bwd, fwd) * keep


def hyena_filter(n, w1p, b1, w2, b2, w3, b3, w4, fr, layer):
    row = jnp.arange(2 * n)
    src = jnp.where(row <= n, row, 2 * n - row)
    pos = jnp.where(row == n, 0, src).astype(F32)
    tt = pos / max(n - 1, 1)
    ang = (2.0 * math.pi / n) * pos[:, None] * jnp.linspace(1e-4, HY_BANDS - 1, HY_BANDS, dtype=F32)
    z = jnp.concatenate([tt[:, None], jnp.cos(ang), -jnp.sin(ang), jnp.zeros((2 * n, LANE - HY_EMB), F32)], -1)
    aux = jnp.stack([tt, (row > n).astype(F32), (row != n).astype(F32)], 1)
    aux = jnp.pad(aux, ((0, 0), (0, SUB - 3)))
    deltas = jnp.abs(jnp.linspace(HY_MIN_DECAY, HY_MAX_DECAY, BRANCH_W, dtype=F32))
    dl = jnp.tile(deltas, 2 * HY_ORDER)[None, :]
    r = 512
    wo = 2 * HY_ORDER * BRANCH_W
    full = lambda shape: pl.BlockSpec((None,) + shape, lambda i: (layer,) + (0,) * len(shape))
    return pl.pallas_call(
        _hyfilt_body,
        grid=(2 * n // r,),
        in_specs=[pl.BlockSpec((r, LANE), lambda i: (i, 0)),
                  pl.BlockSpec((r, SUB), lambda i: (i, 0)),
                  full((LANE, HY_FH)), full((1, HY_FH)), full((HY_FH, HY_FH)), full((1, HY_FH)),
                  full((HY_FH, HY_FH)), full((1, HY_FH)), full((HY_FH, wo)), full((1, HY_FH)),
                  pl.BlockSpec((1, wo), lambda i: (0, 0))],
        out_specs=pl.BlockSpec((r, HY_ORDER * BRANCH_W), lambda i: (i, 0)),
        out_shape=jax.ShapeDtypeStruct((2 * n, HY_ORDER * BRANCH_W), F32),
        compiler_params=_cparams(("arbitrary",)),
        name="hyena_filter",
    )(z, aux, w1p, b1, w2, b2, w3, b3, w4, fr, dl)


@functools.lru_cache(maxsize=None)
def _dense_dft_tables(n):
    nn = 2 * n
    k = np.arange(nn)[:, None].astype(np.float64)
    m = np.arange(nn)[None, :].astype(np.float64)
    ang = -2.0 * np.pi * k * m / nn
    wr, wi = np.cos(ang), np.sin(ang)
    f_real = np.concatenate([wr, wi], 0)
    wr_h, wi_h = wr[:, :n], wi[:, :n]
    f_fwd = np.block([[wr_h, -wi_h], [wi_h, wr_h]])
    cr, ci = wr.T[:n] / nn, -wi.T[:n] / nn
    f_inv = np.block([[cr, -ci], [ci, cr]])
    return (np.asarray(f_real, np.float32), np.asarray(f_fwd, np.float32), np.asarray(f_inv, np.float32))


@functools.lru_cache(maxsize=None)
def _two_stage_dft_tables(n):
    nn = 2 * n
    n2c = FFT_N2
    n1c = nn // n2c
    n1h = n1c // 2
    k1 = np.arange(n1c).astype(np.float64)
    n1 = np.arange(n1c).astype(np.float64)
    n2 = np.arange(n2c).astype(np.float64)
    ang = -2.0 * np.pi * (k1[None, :, None] * n1[None, None, :] / n1c + n2[:, None, None] * k1[None, :, None] / nn)
    mr, mi = np.cos(ang), np.sin(ang)
    f1_real = np.concatenate([mr, mi], 1)
    mrh, mih = mr[:, :, :n1h], mi[:, :, :n1h]
    f1_cplx = np.concatenate([np.concatenate([mrh, -mih], 2), np.concatenate([mih, mrh], 2)], 1)
    gr = np.transpose(mr, (0, 2, 1))[:, :n1h, :] / nn
    gi = -np.transpose(mi, (0, 2, 1))[:, :n1h, :] / nn
    g1 = np.concatenate([np.concatenate([gr, -gi], 2), np.concatenate([gi, gr], 2)], 1)
    k2 = np.arange(n2c).astype(np.float64)
    a2 = -2.0 * np.pi * k2[:, None] * n2[None, :] / n2c
    fr, fi = np.cos(a2), np.sin(a2)
    f2 = np.block([[fr, -fi], [fi, fr]])
    f2i = np.block([[fr.T, fi.T], [-fi.T, fr.T]])
    f32 = lambda a: np.asarray(a, np.float32)
    return f32(f1_real), f32(f1_cplx), f32(g1), f32(f2), f32(f2i)


def _spec_dense_body(f_ref, x_ref, o_ref):
    o_ref[...] = _dot(f_ref[...], x_ref[...], hi=True)


def hyena_spec_dense(full, n):
    f_real, _, _ = _dense_dft_tables(n)
    nn, cols = full.shape
    return pl.pallas_call(
        _spec_dense_body,
        grid=(cols // LANE,),
        in_specs=[pl.BlockSpec((2 * nn, nn), lambda j: (0, 0)),
                  pl.BlockSpec((nn, LANE), lambda j: (0, j))],
        out_specs=pl.BlockSpec((2 * nn, LANE), lambda j: (0, j)),
        out_shape=jax.ShapeDtypeStruct((2 * nn, cols), F32),
        compiler_params=_cparams(("arbitrary",)),
        name="hyena_spec_dense",
    )(jnp.asarray(f_real), full)


def _conv_dense_body(*refs, has_mult):
    z_ref, h_ref, ff_ref, fi_ref, bias_ref = refs[:5]
    m_ref = refs[5] if has_mult else None
    o_ref = refs[-1]
    z = z_ref[...]
    nn = z.shape[0]
    x = _dot(ff_ref[...], z, hi=True)
    xr, xi = x[:nn], x[nn:]
    hr, hi_ = h_ref[0:nn, :], h_ref[nn:2 * nn, :]
    y = _dot(fi_ref[...], jnp.concatenate([xr * hr - xi * hi_, xr * hi_ + xi * hr], 0), hi=True)
    out = y + z * bias_ref[...]
    if has_mult:
        out = out * m_ref[...]
    o_ref[...] = out


def hyena_conv_dense(zsrc, zcol, row0, n, n_batch, spec, bias3, layer, order, prev, mult=None):
    _, f_fwd, f_inv = _dense_dft_tables(n)
    nn = 2 * n
    rb, cb = row0 // nn, zcol // LANE
    wb = BRANCH_W // LANE
    in_specs = [pl.BlockSpec((nn, LANE), lambda p, j: (rb + p, cb + j)),
                pl.BlockSpec((2 * nn, LANE), lambda p, j: (0, order * wb + j)),
                pl.BlockSpec((2 * nn, nn), lambda p, j: (0, 0)),
                pl.BlockSpec((nn, 2 * nn), lambda p, j: (0, 0)),
                pl.BlockSpec((None, 1, LANE), lambda p, j: (layer * HY_ORDER + order, 0, j))]
    args = [zsrc, spec, jnp.asarray(f_fwd), jnp.asarray(f_inv), bias3]
    if mult is not None:
        mb = mult[1] // LANE
        in_specs.append(pl.BlockSpec((nn, LANE), lambda p, j: (rb + p, mb + j)))
        args.append(mult[0])
    in_specs.append(pl.BlockSpec(memory_space=pl.ANY))
    args.append(prev)
    return pl.pallas_call(
        functools.partial(_conv_dense_body, has_mult=mult is not None),
        grid=(n_batch // 2, wb),
        in_specs=in_specs,
        out_specs=pl.BlockSpec((nn, LANE), lambda p, j: (rb + p, j)),
        out_shape=jax.ShapeDtypeStruct(prev.shape, F32),
        input_output_aliases={len(args) - 1: 0},
        compiler_params=_cparams(("arbitrary", "arbitrary")),
        name="hyena_conv_dense",
    )(*args)


def _spec_fft_body(x_ref, f1_ref, f2_ref, o_ref, a_ref):
    n1c = o_ref.shape[0]

    def stage1(g, carry):
        n2s = [g * FFT_UNROLL + u for u in range(FFT_UNROLL)]
        xs = [x_ref[pl.ds(n2, n1c, stride=FFT_N2), :] for n2 in n2s]
        res = [_dot(f1_ref[n2], x) for n2, x in zip(n2s, xs)]
        for n2, r in zip(n2s, res):
            a_ref[pl.ds(pl.multiple_of(n2 * 2 * n1c, 2 * n1c), 2 * n1c), :] = r
        return carry

    lax.fori_loop(0, FFT_N2 // FFT_UNROLL, stage1, 0, unroll=2)
    g2 = FFT_UNROLL // 2

    def stage2(g, carry):
        k1s = [g * g2 + u for u in range(g2)]
        blks = [jnp.concatenate([a_ref[pl.ds(k1, FFT_N2, stride=2 * n1c), :],
                                 a_ref[pl.ds(n1c + k1, FFT_N2, stride=2 * n1c), :]], 0) for k1 in k1s]
        res = [_dot(f2_ref[...], blk) for blk in blks]
        for k1, r in zip(k1s, res):
            o_ref[k1] = r
        return carry

    lax.fori_loop(0, n1c // g2, stage2, 0, unroll=2)


def hyena_spec_fft(full, n):
    f1_real, _, _, f2, _ = _two_stage_dft_tables(n)
    nn, cols = full.shape
    n1c = nn // FFT_N2
    const = lambda shape: pl.BlockSpec(shape, lambda j: (0,) * len(shape), pipeline_mode=pl.Buffered(1))
    return pl.pallas_call(
        _spec_fft_body,
        grid=(cols // LANE,),
        in_specs=[pl.BlockSpec((nn, LANE), lambda j: (0, j)),
                  const((FFT_N2, 2 * n1c, n1c)), const((2 * FFT_N2, 2 * FFT_N2))],
        out_specs=pl.BlockSpec((n1c, 2 * FFT_N2, LANE), lambda j: (0, 0, j)),
        out_shape=jax.ShapeDtypeStruct((n1c, 2 * FFT_N2, cols), F32),
        scratch_shapes=[pltpu.VMEM((FFT_N2 * 2 * n1c, LANE), F32)],
        compiler_params=_cparams(("arbitrary",)),
        name="hyena_spec_fft",
    )(full, jnp.asarray(f1_real, BF16), jnp.asarray(f2, BF16))


def _conv_fft_body(*refs, has_mult):
    z_ref, h_ref, f1_ref, f2_ref, f2i_ref, g1_ref, bias_ref = refs[:7]
    m_ref = refs[7] if has_mult else None
    o_ref, a_ref, b_ref = refs[-3], refs[-2], refs[-1]
    n1c = h_ref.shape[0]
    n1h = n1c // 2
    n2c = FFT_N2
    n = n1h * n2c

    def slab(n2):
        return pl.ds(pl.multiple_of(n2 * 2 * n1c, 2 * n1c), 2 * n1c)

    def stage1(g, carry):
        n2s = [g * FFT_UNROLL + u for u in range(FFT_UNROLL)]
        xs = [jnp.concatenate([z_ref[pl.ds(n2, n1h, stride=n2c), :], z_ref[pl.ds(n + n2, n1h, stride=n2c), :]], 0)
              for n2 in n2s]
        res = [_dot(f1_ref[n2], x) for n2, x in zip(n2s, xs)]
        for n2, r in zip(n2s, res):
            a_ref[slab(n2), :] = r
        return carry

    lax.fori_loop(0, n2c // FFT_UNROLL, stage1, 0, unroll=2)
    g2 = FFT_UNROLL // 2

    def stage2(g, carry):
        k1s = [g * g2 + u for u in range(g2)]
        rows = [(pl.ds(k1, n2c, stride=2 * n1c), pl.ds(n1c + k1, n2c, stride=2 * n1c)) for k1 in k1s]
        blks = [jnp.concatenate([a_ref[re, :], a_ref[im, :]], 0) for re, im in rows]
        xs = [_dot(f2_ref[...], blk) for blk in blks]
        ys = []
        for k1, x in zip(k1s, xs):
            xr, xi = x[:n2c], x[n2c:]
            hr, hi_ = h_ref[k1, 0:n2c, :], h_ref[k1, n2c:2 * n2c, :]
            ys.append(jnp.concatenate([xr * hr - xi * hi_, xr * hi_ + xi * hr], 0))
        bs = [_dot(f2i_ref[...], y) for y in ys]
        for (re, im), b in zip(rows, bs):
            b_ref[re, :] = b[:n2c]
            b_ref[im, :] = b[n2c:]
        return carry

    lax.fori_loop(0, n1c // g2, stage2, 0, unroll=2)
    bias = bias_ref[...]

    def stage3(g, carry):
        n2s = [g * FFT_UNROLL + u for u in range(FFT_UNROLL)]
        blks = [b_ref[slab(n2), :] for n2 in n2s]
        ys = [_dot(g1_ref[n2], blk) for n2, blk in zip(n2s, blks)]
        outs = []
        for n2, y in zip(n2s, ys):
            for part, rows in ((y[:n1h], pl.ds(n2, n1h, stride=n2c)), (y[n1h:], pl.ds(n + n2, n1h, stride=n2c))):
                out = part + z_ref[rows, :] * bias
                if has_mult:
                    out = out * m_ref[rows, :]
                outs.append((rows, out))
        for rows, out in outs:
            o_ref[rows, :] = out
        return carry

    lax.fori_loop(0, n2c // FFT_UNROLL, stage3, 0, unroll=2)


def hyena_conv_fft(zsrc, zcol, n, n_batch, spec, bias3, layer, order, t_rows, mult=None):
    _, f1_cplx, g1, f2, f2i = _two_stage_dft_tables(n)
    n1c = 2 * n // FFT_N2
    cb = zcol // LANE
    wb = BRANCH_W // LANE
    const = lambda shape: pl.BlockSpec(shape, lambda j, p: (0,) * len(shape), pipeline_mode=pl.Buffered(1))
    in_specs = [pl.BlockSpec((2 * n, LANE), lambda j, p: (p, cb + j)),
                pl.BlockSpec((n1c, 2 * FFT_N2, LANE), lambda j, p: (0, 0, order * wb + j),
                             pipeline_mode=pl.Buffered(1)),
                const((FFT_N2, 2 * n1c, n1c)), const((2 * FFT_N2, 2 * FFT_N2)), const((2 * FFT_N2, 2 * FFT_N2)),
                const((FFT_N2, n1c, 2 * n1c)),
                pl.BlockSpec((None, 1, LANE), lambda j, p: (layer * HY_ORDER + order, 0, j))]
    args = [zsrc, spec, jnp.asarray(f1_cplx, BF16), jnp.asarray(f2, BF16), jnp.asarray(f2i, BF16),
            jnp.asarray(g1, BF16), bias3]
    if mult is not None:
        mb = mult[1] // LANE
        in_specs.append(pl.BlockSpec((2 * n, LANE), lambda j, p: (p, mb + j)))
        args.append(mult[0])
    return pl.pallas_call(
        functools.partial(_conv_fft_body, has_mult=mult is not None),
        grid=(wb, n_batch // 2),
        in_specs=in_specs,
        out_specs=pl.BlockSpec((2 * n, LANE), lambda j, p: (p, j)),
        out_shape=jax.ShapeDtypeStruct((t_rows, BRANCH_W), F32),
        scratch_shapes=[pltpu.VMEM((FFT_N2 * 2 * n1c, LANE), F32)] * 2,
        compiler_params=_cparams(("arbitrary", "arbitrary")),
        name="hyena_conv_fft",
    )(*args)


def _swap_pairs(x):
    w = x.shape[-1]
    lane = lax.broadcasted_iota(jnp.int32, x.shape, x.ndim - 1)
    return jnp.where(lane % 2 == 0, pltpu.roll(x, w - 1, x.ndim - 1), pltpu.roll(x, 1, x.ndim - 1))


def _attn_prep_body(g_ref, dq_ref, dk_ref, dv_ref, cg_ref, sg_ref, cd_ref, sd_ref, qn_ref, kn_ref,
                    qg_ref, kg_ref, vg_ref, qd_ref, kd_ref, vd_ref, *, lat_blocks):
    is_lat = pl.program_id(0) < lat_blocks
    cg = jnp.where(is_lat, cg_ref[...], 1.0)
    sg = jnp.where(is_lat, sg_ref[...], 0.0)
    cd = jnp.where(is_lat, cd_ref[...], 1.0)
    sd = jnp.where(is_lat, sd_ref[...], 0.0)

    def rope(x, cs, sn):
        return x * cs + _swap_pairs(x) * sn

    def rms(x, w):
        return x * lax.rsqrt(jnp.mean(x * x, axis=-1, keepdims=True) + EPS) * w

    for h in range(HEADS):
        sl = slice(h * HEAD_D, (h + 1) * HEAD_D)
        q = rope(rms(g_ref[:, sl], qn_ref[...]), cg, sg)
        qg_ref[:, sl] = (q * HEAD_D ** -0.5).astype(BF16)
        qd_ref[:, sl] = (rope(dq_ref[:, sl], cd, sd) * DIFF_QK ** -0.5).astype(BF16)
        kd_ref[:, sl] = rope(dk_ref[:, sl], cd, sd).astype(BF16)
    for h in range(GQA_KV):
        sl = slice(h * HEAD_D, (h + 1) * HEAD_D)
        kin = g_ref[:, BRANCH_W + h * HEAD_D:BRANCH_W + (h + 1) * HEAD_D]
        kg_ref[:, sl] = rope(rms(kin, kn_ref[...]), cg, sg).astype(BF16)
    vg_ref[...] = g_ref[:, BRANCH_W + GQA_KV * HEAD_D:BRANCH_W + 2 * GQA_KV * HEAD_D].astype(BF16)
    vd_ref[...] = dv_ref[...].astype(BF16)


def attn_prep(p, ropes, qn3, kn3, layer, n_lat, n_ctx, n_batch):
    t = p.shape[0]
    r = 256 if n_ctx % 256 == 0 else n_ctx
    nlb, ncb = n_lat // r, n_ctx // r
    lat_blocks = n_batch * nlb
    kvw = GQA_KV * HEAD_D
    w = BRANCH_W

    def kv_row(i):
        lat = (i // nlb) * (nlb + ncb) + ncb + i % nlb
        j = i - lat_blocks
        ctx = (j // ncb) * (nlb + ncb) + j % ncb
        return jnp.where(i < lat_blocks, lat, ctx)

    rope_spec = pl.BlockSpec((r, LANE), lambda i: (jnp.where(i < lat_blocks, i % nlb, 0), 0))
    nkv = n_batch * (n_lat + n_ctx)
    return pl.pallas_call(
        functools.partial(_attn_prep_body, lat_blocks=lat_blocks),
        grid=(t // r,),
        in_specs=[pl.BlockSpec((r, 2 * w), lambda i: (i, C_GQA_QKV // (2 * w))),
                  pl.BlockSpec((r, w), lambda i: (i, C_DIFF_Q // w)),
                  pl.BlockSpec((r, w), lambda i: (i, C_DIFF_K // w)),
                  pl.BlockSpec((r, w), lambda i: (i, C_DIFF_V // w)),
                  rope_spec, rope_spec, rope_spec, rope_spec,
                  pl.BlockSpec((None, 1, LANE), lambda i: (layer, 0, 0)),
                  pl.BlockSpec((None, 1, LANE), lambda i: (layer, 0, 0))],
        out_specs=[pl.BlockSpec((r, w), lambda i: (i, 0)),
                   pl.BlockSpec((r, kvw), lambda i: (kv_row(i), 0)),
                   pl.BlockSpec((r, kvw), lambda i: (kv_row(i), 0)),
                   pl.BlockSpec((r, w), lambda i: (i, 0)),
                   pl.BlockSpec((r, w), lambda i: (kv_row(i), 0)),
                   pl.BlockSpec((r, w), lambda i: (kv_row(i), 0))],
        out_shape=[jax.ShapeDtypeStruct((t, w), BF16), jax.ShapeDtypeStruct((nkv, kvw), BF16),
                   jax.ShapeDtypeStruct((nkv, kvw), BF16), jax.ShapeDtypeStruct((t, w), BF16),
                   jax.ShapeDtypeStruct((nkv, w), BF16), jax.ShapeDtypeStruct((nkv, w), BF16)],
        compiler_params=_cparams(("arbitrary",)),
        name="attn_prep",
    )(p, p, p, p, *ropes, qn3, kn3)


def _softmax_parts(s):
    e = jnp.exp(s - jnp.max(s, axis=-1, keepdims=True))
    return e, jnp.sum(e, axis=-1, keepdims=True)


def _gqa_body(q_ref, k_ref, v_ref, *rest):
    o_ref = rest[-1]
    group = HEADS // GQA_KV
    for kvh in range(GQA_KV):
        k = k_ref[:, kvh * HEAD_D:(kvh + 1) * HEAD_D]
        v = v_ref[:, kvh * HEAD_D:(kvh + 1) * HEAD_D]
        for g in range(group):
            sl = slice((kvh * group + g) * HEAD_D, (kvh * group + g + 1) * HEAD_D)
            s = lax.dot_general(q_ref[:, sl], k, (((1,), (1,)), ((), ())), preferred_element_type=F32)
            e, l = _softmax_parts(s)
            o_ref[:, sl] = jnp.dot(e.astype(BF16), v, preferred_element_type=F32) / l


def _diff_body(q_ref, k_ref, v_ref, lam_ref, *rest, lam_init):
    o_ref = rest[-1]
    lam4 = lam_ref[...]
    lam = (jnp.exp(jnp.sum(lam4[0:1] * lam4[1:2], axis=-1, keepdims=True))
           - jnp.exp(jnp.sum(lam4[2:3] * lam4[3:4], axis=-1, keepdims=True)) + lam_init)
    dn = (((1,), (1,)), ((), ()))
    for h in range(HEADS):
        sl = slice(h * HEAD_D, (h + 1) * HEAD_D)
        q = q_ref[:, sl]
        k = k_ref[:, sl]
        v = v_ref[:, sl]
        first = lax.broadcasted_iota(jnp.int32, q.shape, 1) < DIFF_QK
        zero = jnp.zeros_like(q)
        e1, l1 = _softmax_parts(lax.dot_general(jnp.where(first, q, zero), k, dn, preferred_element_type=F32))
        e2, l2 = _softmax_parts(lax.dot_general(jnp.where(first, zero, q), k, dn, preferred_element_type=F32))
        o1 = jnp.dot(e1.astype(BF16), v, preferred_element_type=F32) / l1
        o2 = jnp.dot(e2.astype(BF16), v, preferred_element_type=F32) / l2
        o_ref[:, sl] = o1 - lam * o2


def attention(body, q, k, v, extra, extra_specs, q_row0, nq, kv_per_batch, kv_len, n_batch, tq, name, prev=None):
    t, w = q.shape
    qb0 = q_row0 // tq
    nqb = nq // tq
    kvb = kv_per_batch // kv_len
    in_specs = [pl.BlockSpec((tq, w), lambda b, i: (qb0 + b * nqb + i, 0)),
                pl.BlockSpec((kv_len, k.shape[1]), lambda b, i: (b * kvb, 0)),
                pl.BlockSpec((kv_len, v.shape[1]), lambda b, i: (b * kvb, 0))] + extra_specs
    args = [q, k, v, *extra]
    aliases = {}
    if prev is not None:
        in_specs.append(pl.BlockSpec(memory_space=pl.ANY))
        args.append(prev)
        aliases = {len(args) - 1: 0}
    return pl.pallas_call(
        body,
        grid=(n_batch, nqb),
        in_specs=in_specs,
        out_specs=pl.BlockSpec((tq, w), lambda b, i: (qb0 + b * nqb + i, 0)),
        out_shape=jax.ShapeDtypeStruct((t, w), F32),
        input_output_aliases=aliases,
        compiler_params=_cparams(("arbitrary", "arbitrary")),
        name=name,
    )(*args)


def _merge_body(h_ref, mod_ref, mg_ref, of_ref, ob_ref, ggate_ref, y1_ref, x2_ref, hgate_ref, oc_ref, cgate_ref,
                od_ref, dgate_ref, gnorm_ref, dnorm_ref, wbr_ref, wout_ref, lng_ref, lnb_ref, o_ref, *, diff_scale):
    def rms_heads(x, w):
        parts = []
        for h in range(HEADS):
            xh = x[:, h * HEAD_D:(h + 1) * HEAD_D]
            parts.append(xh * lax.rsqrt(jnp.mean(xh * xh, axis=-1, keepdims=True) + EPS) * w)
        return jnp.concatenate(parts, -1)

    ys = (rms_heads(of_ref[...] + ob_ref[...], gnorm_ref[...]) * _silu(ggate_ref[...]),
          x2_ref[...] * y1_ref[...] * _silu(hgate_ref[...]),
          oc_ref[...] * _silu(cgate_ref[...]),
          rms_heads(od_ref[...], dnorm_ref[...]) * diff_scale * _silu(dgate_ref[...]))
    acc = None
    for n in range(N_BRANCH):
        proj = jnp.dot(ys[n].astype(BF16), wbr_ref[n], preferred_element_type=F32)
        term = _sigmoid(mg_ref[:, n * D_MODEL:(n + 1) * D_MODEL]) * proj
        acc = term if acc is None else acc + term
    out = jnp.dot(acc.astype(BF16), wout_ref[...], preferred_element_type=F32)
    x = ALPHA * h_ref[...] + mod_ref[2:3, :] * out
    mu = jnp.mean(x, axis=-1, keepdims=True)
    xc = x - mu
    var = jnp.mean(xc * xc, axis=-1, keepdims=True)
    o_ref[...] = xc * lax.rsqrt(var + EPS) * lng_ref[...] + lnb_ref[...]


def merge_postnorm(h_all, mod3, p, o_f, o_b, y1, xv, oc, od, gnorm3, dnorm3, wbr, wout, lng3, lnb3, layer, lam_init,
                   n_lat, n_batch):
    t, d = h_all.shape
    r = 256 if n_lat % 256 == 0 else 64
    w = BRANCH_W
    lbb = n_lat // r
    row = lambda i: jnp.minimum(i // lbb, n_batch)
    tok = lambda cb: pl.BlockSpec((r, w), lambda i: (i, cb))
    vec = lambda width: pl.BlockSpec((None, 1, width), lambda i: (layer, 0, 0))
    return pl.pallas_call(
        functools.partial(_merge_body, diff_scale=1.0 - lam_init),
        grid=(t // r,),
        in_specs=[pl.BlockSpec((r, d), lambda i: (i, 0)),
                  pl.BlockSpec((None, 3, d), lambda i: (row(i), 0, 0)),
                  pl.BlockSpec((r, N_BRANCH * d), lambda i: (i, C_MERGE // (N_BRANCH * d))),
                  tok(0), tok(0), tok(C_GDN_GATE // w), tok(0), tok(1), tok(C_HY_GATE // w), tok(0),
                  tok(C_GQA_GATE // w), tok(0), tok(C_DIFF_GATE // w),
                  vec(LANE), vec(LANE),
                  pl.BlockSpec((None, N_BRANCH, w, d), lambda i: (layer, 0, 0, 0)),
                  pl.BlockSpec((None, d, d), lambda i: (layer, 0, 0)),
                  vec(d), vec(d)],
        out_specs=pl.BlockSpec((r, d), lambda i: (i, 0)),
        out_shape=jax.ShapeDtypeStruct((t, d), F32),
        compiler_params=_cparams(("arbitrary",)),
        name="merge_postnorm",
    )(h_all, mod3, p, o_f, o_b, p, y1, xv, p, oc, p, od, p, gnorm3, dnorm3, wbr, wout, lng3, lnb3)


def _rope_tables(n_lat, dim):
    rows = n_lat // GRID_W
    row = jnp.repeat(jnp.arange(rows, dtype=F32), GRID_W)
    col = jnp.tile(jnp.arange(GRID_W, dtype=F32), rows)
    half = dim // 2
    inv = ROPE_THETA ** (-jnp.arange(0, half, 2, dtype=F32) / half)
    ang = jnp.concatenate([row[:, None] * inv, col[:, None] * inv], -1)
    cos = jnp.repeat(jnp.cos(ang), 2, axis=-1)
    sin = jnp.repeat(jnp.sin(ang), 2, axis=-1)
    sign = jnp.tile(jnp.array([-1.0, 1.0], F32), dim // 2)
    reps = LANE // dim
    return jnp.tile(cos, (1, reps)), jnp.tile(sin * sign, (1, reps))


def kernel(x, c, ctx, c_ctx, w_ada, b_ada, w_in, gdn_conv, gdn_a_log, gdn_dt_bias, gdn_norm, hy_conv, hy_w1, hy_b1,
           hy_w2, hy_b2, hy_w3, hy_b3, hy_w4, hy_freq, hy_bias, gqa_qn, gqa_kn, diff_lam, diff_norm, w_br, w_out,
           ln_g, ln_b):
    nb, n_lat, d = x.shape
    n_ctx = ctx.shape[1]
    t_lat, t_ctx = nb * n_lat, nb * n_ctx
    depth = w_in.shape[0]
    w = BRANCH_W

    w_main = jnp.concatenate([w_in[:, :, O_MERGE:], w_in[:, :, :O_GDN_AB], w_in[:, :, O_GDN_AB + 4 * HEADS:O_MERGE]],
                             axis=2).astype(BF16)
    w_ab = jnp.pad(w_in[:, :, O_GDN_AB:O_GDN_AB + 4 * HEADS], ((0, 0), (0, 0), (0, LANE - 4 * HEADS)))
    wbr_bf = w_br.astype(BF16)
    wout_bf = w_out.astype(BF16)
    b_ada3 = b_ada[:, None, :]
    cvec = jnp.concatenate([c, c_ctx[None, :], jnp.zeros((SUB - nb - 1, d), F32)], 0)
    as3 = lambda a: a[:, None, :]
    gdn_par_r = jnp.pad(jnp.stack([gdn_a_log.reshape(depth, -1), gdn_dt_bias.reshape(depth, -1)], 1),
                        ((0, 0), (0, SUB - 2), (0, LANE - 2 * HEADS)))
    gdn_par_c = jnp.pad(jnp.stack([gdn_a_log.reshape(depth, -1), gdn_dt_bias.reshape(depth, -1)], 2),
                        ((0, 0), (0, 2 * HEADS), (0, LANE - 2)))
    hy_w1p = jnp.pad(hy_w1, ((0, 0), (0, LANE - HY_EMB), (0, 0)))
    hy_bias3 = hy_bias.reshape(depth * HY_ORDER, 1, w)
    ropes = _rope_tables(n_lat, HEAD_D) + _rope_tables(n_lat, DIFF_QK)

    tm = 1024 if (n_lat % 1024 == 0 and t_ctx % 1024 == 0) else n_ctx
    h_all = jnp.concatenate([x.reshape(t_lat, d), ctx.reshape(t_ctx, d)], 0)
    for l in range(depth):
        lam_init = 0.8 - 0.6 * math.exp(-0.3 * l)
        mod3 = ada_mod(cvec, w_ada, b_ada3, l).reshape(SUB, 3, d)
        p, ab = in_proj(h_all, mod3, w_main, w_ab, l, tm, n_lat // tm, nb)

        qkv = dwconv(p, gdn_conv, l, C_GDN_QKV, 3 * w, n_lat, n_ctx, nb, act=True)
        ab_rows = jnp.transpose(ab[:, :4 * HEADS].reshape(-1, GDN_CHUNK, 4 * HEADS), (0, 2, 1))
        o_f, o_b = gdn_scan(qkv, ab, ab_rows, gdn_par_r[l], gdn_par_c[l], n_lat, n_ctx, nb)

        xv = dwconv(p, hy_conv, l, C_HY_XV, 3 * w, n_lat, n_ctx, nb, act=False)
        filt = lambda n: hyena_filter(n, hy_w1p, as3(hy_b1), hy_w2, as3(hy_b2), hy_w3, as3(hy_b3), hy_w4,
                                      as3(hy_freq), l)
        spec_lat = hyena_spec_fft(filt(n_lat), n_lat)
        spec_ctx = hyena_spec_dense(filt(n_ctx), n_ctx)
        z1 = hyena_conv_fft(xv, 2 * w, n_lat, nb, spec_lat, hy_bias3, l, 0, t_lat + t_ctx, mult=(xv, 0))
        z1 = hyena_conv_dense(xv, 2 * w, t_lat, n_ctx, nb, spec_ctx, hy_bias3, l, 0, z1, mult=(xv, 0))
        y1 = hyena_conv_fft(z1, 0, n_lat, nb, spec_lat, hy_bias3, l, 1, t_lat + t_ctx)
        y1 = hyena_conv_dense(z1, 0, t_lat, n_ctx, nb, spec_ctx, hy_bias3, l, 1, y1)

        qg, kg, vg, qd, kd, vd = attn_prep(p, ropes, as3(gqa_qn), as3(gqa_kn), l, n_lat, n_ctx, nb)
        kv_all = n_lat + n_ctx
        tq = min(256, n_ctx)
        lam_spec = [pl.BlockSpec((None, 4, DIFF_QK), lambda b, i: (l, 0, 0))]
        diff_body = functools.partial(_diff_body, lam_init=lam_init)
        oc = attention(_gqa_body, qg, kg, vg, (), [], 0, n_lat, kv_all, kv_all, nb, tq, "gqa_lat")
        oc = attention(_gqa_body, qg, kg, vg, (), [], t_lat, n_ctx, kv_all, n_ctx, nb, tq, "gqa_ctx", prev=oc)
        od = attention(diff_body, qd, kd, vd, (diff_lam,), lam_spec, 0, n_lat, kv_all, kv_all, nb, tq, "diff_lat")
        od = attention(diff_body, qd, kd, vd, (diff_lam,), lam_spec, t_lat, n_ctx, kv_all, n_ctx, nb, tq, "diff_ctx",
                       prev=od)

        h_all = merge_postnorm(h_all, mod3, p, o_f, o_b, y1, xv, oc, od, as3(gdn_norm), as3(diff_norm), wbr_bf, wout_bf,
                               as3(ln_g), as3(ln_b), l, lam_init, n_lat, nb)
    return h_all[:t_lat].reshape(nb, n_lat, d)
```

```python
import functools
import math

import numpy as np
import jax
import jax.numpy as jnp
from jax import lax
from jax.experimental import pallas as pl
from jax.experimental.pallas import tpu as pltpu

F32 = jnp.float32
BF16 = jnp.bfloat16
HI = lax.Precision.HIGHEST

D_MODEL = 1024
DEPTH = 4
GRID_W = 64
BRANCH_W = D_MODEL // 2
N_BRANCH = 4
HEADS = 4
HEAD_D = BRANCH_W // HEADS
GDN_CONV = 4
GDN_CHUNK = 64
GDN_SUB = 4
HY_CONV = 3
HY_EMB = 33
HY_BANDS = (HY_EMB - 1) // 2
HY_FH = 64
HY_ORDER = 2
HY_MIN_DECAY = math.log(1e-2) / 1.5
HY_MAX_DECAY = math.log(1e-2) / 0.3
GQA_KV = 2
DIFF_QK = HEAD_D // 2
ROPE_THETA = 10000.0
EPS = 1e-6
ALPHA = (2.0 * DEPTH) ** 0.25

LANE = 128
SUB = 8
FFT_N2 = 128
FFT_UNROLL = 8
VMEM_LIMIT = 60 * 1024 * 1024

C_MERGE = 0
C_GDN_QKV = 4096
C_GDN_GATE = 5632
C_HY_XV = 6144
C_HY_GATE = 7680
C_GQA_QKV = 8192
C_GQA_GATE = 9216
C_DIFF_Q = 9728
C_DIFF_K = 10240
C_DIFF_V = 10752
C_DIFF_GATE = 11264
N_MAIN = 11776
O_GDN_AB = 1536
O_MERGE = 7696


def _cparams(sem):
    return pltpu.CompilerParams(dimension_semantics=sem, vmem_limit_bytes=VMEM_LIMIT)


def _dot(a, b, hi=False):
    if hi:
        return jnp.dot(a, b, precision=HI, preferred_element_type=F32)
    return jnp.dot(a.astype(BF16), b.astype(BF16), preferred_element_type=F32)


def _dot_nt(a, b, hi=False):
    dn = (((1,), (1,)), ((), ()))
    if hi:
        return lax.dot_general(a, b, dn, precision=HI, preferred_element_type=F32)
    return lax.dot_general(a.astype(BF16), b.astype(BF16), dn, preferred_element_type=F32)


def _dot_tn(a, b):
    return lax.dot_general(a.astype(BF16), b.astype(BF16), (((0,), (0,)), ((), ())), preferred_element_type=F32)


def _sigmoid(x):
    return 1.0 / (1.0 + jnp.exp(-x))


def _silu(x):
    return x * _sigmoid(x)


def _softplus(x):
    return jnp.maximum(x, 0.0) + jnp.log1p(jnp.exp(-jnp.abs(x)))


def _ada_body(c_ref, w_ref, b_ref, o_ref):
    o_ref[...] = _dot(_silu(c_ref[...]), w_ref[...], hi=True) + b_ref[...]


def ada_mod(cvec, w_ada, b_ada3, layer):
    d = cvec.shape[1]
    tn = 512
    return pl.pallas_call(
        _ada_body,
        grid=(3 * d // tn,),
        in_specs=[pl.BlockSpec((SUB, d), lambda j: (0, 0)),
                  pl.BlockSpec((None, d, tn), lambda j: (layer, 0, j)),
                  pl.BlockSpec((None, 1, tn), lambda j: (layer, 0, j))],
        out_specs=pl.BlockSpec((SUB, tn), lambda j: (0, j)),
        out_shape=jax.ShapeDtypeStruct((SUB, 3 * d), F32),
        compiler_params=_cparams(("arbitrary",)),
        name="ada_mod",
    )(cvec, w_ada, b_ada3)


def _inproj_body(h_ref, mod_ref, w_ref, wab_ref, o_ref, ab_ref, u_ref):
    @pl.when(pl.program_id(1) == 0)
    def _():
        x = h_ref[...]
        mu = jnp.mean(x, axis=-1, keepdims=True)
        xc = x - mu
        var = jnp.mean(xc * xc, axis=-1, keepdims=True)
        u = xc * lax.rsqrt(var + EPS) * (1.0 + mod_ref[1:2, :]) + mod_ref[0:1, :]
        u_ref[...] = u.astype(BF16)
        ab_ref[...] = _dot(u, wab_ref[...], hi=True)

    o_ref[...] = jnp.dot(u_ref[...], w_ref[...], preferred_element_type=F32)


def in_proj(h_all, mod3, w_main, w_ab, layer, tm, lat_blocks_per_batch, n_batch):
    t, d = h_all.shape
    n_main = w_main.shape[2]
    tn = n_main // 4
    row = lambda i: jnp.minimum(i // lat_blocks_per_batch, n_batch)
    return pl.pallas_call(
        _inproj_body,
        grid=(t // tm, n_main // tn),
        in_specs=[pl.BlockSpec((tm, d), lambda i, j: (i, 0)),
                  pl.BlockSpec((None, 3, d), lambda i, j: (row(i), 0, 0)),
                  pl.BlockSpec((None, d, tn), lambda i, j: (layer, 0, j)),
                  pl.BlockSpec((None, d, LANE), lambda i, j: (layer, 0, 0))],
        out_specs=[pl.BlockSpec((tm, tn), lambda i, j: (i, j)),
                   pl.BlockSpec((tm, LANE), lambda i, j: (i, 0))],
        out_shape=[jax.ShapeDtypeStruct((t, n_main), F32), jax.ShapeDtypeStruct((t, LANE), F32)],
        scratch_shapes=[pltpu.VMEM((tm, d), BF16)],
        compiler_params=_cparams(("arbitrary", "arbitrary")),
        name="in_proj",
    )(h_all, mod3, w_main, w_ab)


def _dwconv_body(xp_ref, x_ref, xn_ref, w_ref, o_ref, pad_ref, *, taps, pad_l, t_lat, n_lat, n_ctx, sb, act):
    i = pl.program_id(0)
    r = x_ref.shape[0]
    pad_ref[0:SUB, :] = xp_ref[...]
    pad_ref[SUB:SUB + r, :] = x_ref[...]
    pad_ref[SUB + r:2 * SUB + r, :] = xn_ref[...]
    row = lax.broadcasted_iota(jnp.int32, (sb, 1), 0)
    for k in range(r // sb):
        g0 = i * r + k * sb
        in_lat = g0 < t_lat
        starts = jnp.where(in_lat, g0 % n_lat == 0, (g0 - t_lat) % n_ctx == 0)
        ends = jnp.where(in_lat, (g0 + sb) % n_lat == 0, (g0 + sb - t_lat) % n_ctx == 0)
        acc = None
        for j in range(taps):
            d = j - pad_l
            off = SUB + k * sb + d
            xs = pad_ref[off:off + sb, :]
            if d < 0:
                xs = jnp.where(jnp.logical_and(starts, row < -d), 0.0, xs)
            elif d > 0:
                xs = jnp.where(jnp.logical_and(ends, row >= sb - d), 0.0, xs)
            term = w_ref[j:j + 1, :] * xs
            acc = term if acc is None else acc + term
        if act:
            acc = _silu(acc)
        o_ref[k * sb:(k + 1) * sb, :] = acc


def dwconv(p, w_conv, layer, col0, width, n_lat, n_ctx, n_batch, act):
    t = p.shape[0]
    taps = w_conv.shape[1]
    sb = min(256, n_ctx)
    r = 1024 if t % 1024 == 0 else sb
    lw = 512
    cb = col0 // lw
    rs = r // SUB
    body = functools.partial(_dwconv_body, taps=taps, pad_l=(taps - 1) // 2, t_lat=n_batch * n_lat, n_lat=n_lat,
                             n_ctx=n_ctx, sb=sb, act=act)
    return pl.pallas_call(
        body,
        grid=(t // r, width // lw),
        in_specs=[pl.BlockSpec((SUB, lw), lambda i, j: (jnp.maximum(i * rs - 1, 0), cb + j)),
                  pl.BlockSpec((r, lw), lambda i, j: (i, cb + j)),
                  pl.BlockSpec((SUB, lw), lambda i, j: (jnp.minimum((i + 1) * rs, t // SUB - 1), cb + j)),
                  pl.BlockSpec((None, taps, lw), lambda i, j: (layer, 0, j))],
        out_specs=pl.BlockSpec((r, lw), lambda i, j: (i, j)),
        out_shape=jax.ShapeDtypeStruct((t, width), F32),
        scratch_shapes=[pltpu.VMEM((r + 2 * SUB, lw), F32)],
        compiler_params=_cparams(("arbitrary", "arbitrary")),
        name="dwconv",
    )(p, p, p, w_conv)


def _gdn_body(qf_ref, qb_ref, abcf_ref, abcb_ref, abrf_ref, abrb_ref, pr_ref, pc_ref, of_ref, ob_ref, s_ref):
    c = GDN_CHUNK

    @pl.when(pl.program_id(1) == 0)
    def _():
        s_ref[...] = jnp.zeros_like(s_ref)

    ii = lax.broadcasted_iota(jnp.int32, (c, c), 0)
    jj = lax.broadcasted_iota(jnp.int32, (c, c), 1)
    lmat = (jj <= ii).astype(F32)
    eye = (jj == ii).astype(F32)
    alr, dtr = pr_ref[0:1, :], pr_ref[1:2, :]
    alc, dtc = pc_ref[:, 0:1], pc_ref[:, 1:2]
    chains = []
    for d in range(2):
        qkv_ref = (qf_ref, qb_ref)[d]
        abc_ref = (abcf_ref, abcb_ref)[d]
        abr_ref = (abrf_ref, abrb_ref)[d]
        incl = (jj <= ii) if d == 0 else (jj >= ii)
        strict = (jj < ii) if d == 0 else (jj > ii)
        for j in range(GDN_SUB):
            rows = slice(j * c, (j + 1) * c)
            abc = abc_ref[rows, :]
            abr = abr_ref[j]
            g_c = -jnp.exp(alr) * _softplus(abc + dtr)
            g_r = -jnp.exp(alc) * _softplus(abr + dtc)
            cum_c = _dot(lmat, g_c, hi=True)
            cum_r = _dot_nt(g_r, lmat, hi=True)
            if d == 1:
                cum_c = cum_c[c - 1:c, :] - cum_c + g_c
                cum_r = cum_r[:, c - 1:c] - cum_r + g_r
            beta_all = _sigmoid(abc)
            for h in range(HEADS):
                idx = HEADS * d + h
                q = qkv_ref[rows, h * HEAD_D:(h + 1) * HEAD_D]
                k = qkv_ref[rows, BRANCH_W + h * HEAD_D:BRANCH_W + (h + 1) * HEAD_D]
                v = qkv_ref[rows, 2 * BRANCH_W + h * HEAD_D:2 * BRANCH_W + (h + 1) * HEAD_D]
                q = q * lax.rsqrt(jnp.sum(q * q, axis=-1, keepdims=True) + EPS) * (HEAD_D ** -0.5)
                k = k * lax.rsqrt(jnp.sum(k * k, axis=-1, keepdims=True) + EPS)
                cc = cum_c[:, idx:idx + 1]
                cr = cum_r[idx:idx + 1, :]
                dec = jnp.exp(jnp.where(incl, cc - cr, -1e30))
                beta = beta_all[:, 2 * HEADS + idx:2 * HEADS + idx + 1]
                ecum = jnp.exp(cc)
                tot = cc[c - 1:c, :] if d == 0 else cc[0:1, :]
                chains.append(dict(d=d, h=h, j=j, rows=rows, q=q, k=k, dec=dec, strict=strict, beta=beta, ecum=ecum,
                                   tot=tot, rhs=jnp.concatenate([k * (beta * ecum), v * beta], 1),
                                   k_tail=k * jnp.exp(tot - cc)))
    for ch in chains:
        ch["kk"] = _dot_nt(ch["k"], ch["k"])
        ch["qk"] = _dot_nt(ch["q"], ch["k"])
    for ch in chains:
        ch["p"] = -jnp.where(ch["strict"], ch["beta"] * ch["kk"] * ch["dec"], 0.0)
        ch["inv"] = eye + ch["p"]
    for _ in range(int(math.log2(c)) - 1):
        for ch in chains:
            ch["p"] = _dot(ch["p"], ch["p"])
        for ch in chains:
            ch["inv"] = ch["inv"] + _dot(ch["inv"], ch["p"])
    for ch in chains:
        ch["wu"] = _dot(ch["inv"], ch["rhs"])
        ch["lhs"] = jnp.concatenate([ch["wu"][:, :HEAD_D], ch["q"] * ch["ecum"]], 0)
    state = {(d, h): s_ref[d, h] for d in range(2) for h in range(HEADS)}
    for step in range(GDN_SUB):
        cur = [ch for ch in chains if ch["j"] == (step if ch["d"] == 0 else GDN_SUB - 1 - step)]
        for ch in cur:
            ch["ws"] = _dot(ch["lhs"], state[ch["d"], ch["h"]])
        for ch in cur:
            ch["v_new"] = ch["wu"][:, HEAD_D:] - ch["ws"][:c]
        for ch in cur:
            out_ref = (of_ref, ob_ref)[ch["d"]]
            h = ch["h"]
            out_ref[ch["rows"], h * HEAD_D:(h + 1) * HEAD_D] = ch["ws"][c:] + _dot(ch["qk"] * ch["dec"], ch["v_new"])
            state[ch["d"], h] = state[ch["d"], h] * jnp.exp(ch["tot"]) + _dot_tn(ch["k_tail"], ch["v_new"])
    for (d, h), val in state.items():
        s_ref[d, h] = val


def gdn_scan(qkv, ab, ab_rows, par_r, par_c, n_lat, n_ctx, n_batch):
    t = qkv.shape[0]
    c = GDN_SUB * GDN_CHUNK
    nlc, ncc = n_lat // c, n_ctx // c
    base = n_batch * nlc

    def fwd(b, s):
        return jnp.where(s < ncc, base + b * ncc + s, b * nlc + (s - ncc))

    def bwd(b, s):
        return jnp.where(s < ncc, base + b * ncc + (ncc - 1 - s), b * nlc + (nlc - 1 - (s - ncc)))

    w3 = 3 * BRANCH_W
    return pl.pallas_call(
        _gdn_body,
        grid=(n_batch, ncc + nlc),
        in_specs=[pl.BlockSpec((c, w3), lambda b, s: (fwd(b, s), 0)),
                  pl.BlockSpec((c, w3), lambda b, s: (bwd(b, s), 0)),
                  pl.BlockSpec((c, LANE), lambda b, s: (fwd(b, s), 0)),
                  pl.BlockSpec((c, LANE), lambda b, s: (bwd(b, s), 0)),
                  pl.BlockSpec((GDN_SUB, 4 * HEADS, GDN_CHUNK), lambda b, s: (fwd(b, s), 0, 0)),
                  pl.BlockSpec((GDN_SUB, 4 * HEADS, GDN_CHUNK), lambda b, s: (bwd(b, s), 0, 0)),
                  pl.BlockSpec((SUB, LANE), lambda b, s: (0, 0)),
                  pl.BlockSpec((4 * HEADS, LANE), lambda b, s: (0, 0))],
        out_specs=[pl.BlockSpec((c, BRANCH_W), lambda b, s: (fwd(b, s), 0)),
                   pl.BlockSpec((c, BRANCH_W), lambda b, s: (bwd(b, s), 0))],
        out_shape=[jax.ShapeDtypeStruct((t, BRANCH_W), F32), jax.ShapeDtypeStruct((t, BRANCH_W), F32)],
        scratch_shapes=[pltpu.VMEM((2, HEADS, HEAD_D, HEAD_D), F32)],
        compiler_params=_cparams(("arbitrary", "arbitrary")),
        name="gdn_scan",
    )(qkv, qkv, ab, ab, ab_rows, ab_rows, par_r, par_c)


def _hyfilt_body(z_ref, aux_ref, w1_ref, b1_ref, w2_ref, b2_ref, w3_ref, b3_ref, w4_ref, fr_ref, dl_ref, o_ref):
    fr = fr_ref[...]
    h = jnp.sin(fr * (_dot(z_ref[...], w1_ref[...], hi=True) + b1_ref[...]))
    h = jnp.sin(fr * (_dot(h, w2_ref[...], hi=True) + b2_ref[...]))
    h = jnp.sin(fr * (_dot(h, w3_ref[...], hi=True) + b3_ref[...]))
    taps = _dot(h, w4_ref[...], hi=True) * jnp.exp(-aux_ref[:, 0:1] * dl_ref[...])
    w = BRANCH_W
    negative = aux_ref[:, 1:2] > 0.5
    keep = aux_ref[:, 2:3]
    for o in range(HY_ORDER):
        fwd = taps[:, o * 2 * w:o * 2 * w + w]
        bwd = taps[:, o * 2 * w + w:(o + 1) * 2 * w]
        o_ref[:, o * w:(o + 1) * w] = jnp.where(negative, bwd, fwd) * keep


def hyena_filter(n, w1p, b1, w2, b2, w3, b3, w4, fr, layer):
    row = jnp.arange(2 * n)
    src = jnp.where(row <= n, row, 2 * n - row)
    pos = jnp.where(row == n, 0, src).astype(F32)
    tt = pos / max(n - 1, 1)
    ang = (2.0 * math.pi / n) * pos[:, None] * jnp.linspace(1e-4, HY_BANDS - 1, HY_BANDS, dtype=F32)
    z = jnp.concatenate([tt[:, None], jnp.cos(ang), -jnp.sin(ang), jnp.zeros((2 * n, LANE - HY_EMB), F32)], -1)
    aux = jnp.stack([tt, (row > n).astype(F32), (row != n).astype(F32)], 1)
    aux = jnp.pad(aux, ((0, 0), (0, SUB - 3)))
    deltas = jnp.abs(jnp.linspace(HY_MIN_DECAY, HY_MAX_DECAY, BRANCH_W, dtype=F32))
    dl = jnp.tile(deltas, 2 * HY_ORDER)[None, :]
    r = 512
    wo = 2 * HY_ORDER * BRANCH_W
    full = lambda shape: pl.BlockSpec((None,) + shape, lambda i: (layer,) + (0,) * len(shape))
    return pl.pallas_call(
        _hyfilt_body,
        grid=(2 * n // r,),
        in_specs=[pl.BlockSpec((r, LANE), lambda i: (i, 0)),
                  pl.BlockSpec((r, SUB), lambda i: (i, 0)),
                  full((LANE, HY_FH)), full((1, HY_FH)), full((HY_FH, HY_FH)), full((1, HY_FH)),
                  full((HY_FH, HY_FH)), full((1, HY_FH)), full((HY_FH, wo)), full((1, HY_FH)),
                  pl.BlockSpec((1, wo), lambda i: (0, 0))],
        out_specs=pl.BlockSpec((r, HY_ORDER * BRANCH_W), lambda i: (i, 0)),
        out_shape=jax.ShapeDtypeStruct((2 * n, HY_ORDER * BRANCH_W), F32),
        compiler_params=_cparams(("arbitrary",)),
        name="hyena_filter",
    )(z, aux, w1p, b1, w2, b2, w3, b3, w4, fr, dl)


@functools.lru_cache(maxsize=None)
def _dense_dft_tables(n):
    nn = 2 * n
    k = np.arange(nn)[:, None].astype(np.float64)
    m = np.arange(nn)[None, :].astype(np.float64)
    ang = -2.0 * np.pi * k * m / nn
    wr, wi = np.cos(ang), np.sin(ang)
    f_real = np.concatenate([wr, wi], 0)
    wr_h, wi_h = wr[:, :n], wi[:, :n]
    f_fwd = np.block([[wr_h, -wi_h], [wi_h, wr_h]])
    cr, ci = wr.T[:n] / nn, -wi.T[:n] / nn
    f_inv = np.block([[cr, -ci], [ci, cr]])
    return (np.asarray(f_real, np.float32), np.asarray(f_fwd, np.float32), np.asarray(f_inv, np.float32))


@functools.lru_cache(maxsize=None)
def _two_stage_dft_tables(n):
    nn = 2 * n
    n2c = FFT_N2
    n1c = nn // n2c
    n1h = n1c // 2
    k1 = np.arange(n1c).astype(np.float64)
    n1 = np.arange(n1c).astype(np.float64)
    n2 = np.arange(n2c).astype(np.float64)
    ang = -2.0 * np.pi * (k1[None, :, None] * n1[None, None, :] / n1c + n2[:, None, None] * k1[None, :, None] / nn)
    mr, mi = np.cos(ang), np.sin(ang)
    f1_real = np.concatenate([mr, mi], 1)
    mrh, mih = mr[:, :, :n1h], mi[:, :, :n1h]
    f1_cplx = np.concatenate([np.concatenate([mrh, -mih], 2), np.concatenate([mih, mrh], 2)], 1)
    gr = np.transpose(mr, (0, 2, 1))[:, :n1h, :] / nn
    gi = -np.transpose(mi, (0, 2, 1))[:, :n1h, :] / nn
    g1 = np.concatenate([np.concatenate([gr, -gi], 2), np.concatenate([gi, gr], 2)], 1)
    k2 = np.arange(n2c).astype(np.float64)
    a2 = -2.0 * np.pi * k2[:, None] * n2[None, :] / n2c
    fr, fi = np.cos(a2), np.sin(a2)
    f2 = np.block([[fr, -fi], [fi, fr]])
    f2i = np.block([[fr.T, fi.T], [-fi.T, fr.T]])
    f32 = lambda a: np.asarray(a, np.float32)
    return f32(f1_real), f32(f1_cplx), f32(g1), f32(f2), f32(f2i)


def _spec_dense_body(f_ref, x_ref, o_ref):
    o_ref[...] = _dot(f_ref[...], x_ref[...], hi=True)


def hyena_spec_dense(full, n):
    f_real, _, _ = _dense_dft_tables(n)
    nn, cols = full.shape
    return pl.pallas_call(
        _spec_dense_body,
        grid=(cols // LANE,),
        in_specs=[pl.BlockSpec((2 * nn, nn), lambda j: (0, 0)),
                  pl.BlockSpec((nn, LANE), lambda j: (0, j))],
        out_specs=pl.BlockSpec((2 * nn, LANE), lambda j: (0, j)),
        out_shape=jax.ShapeDtypeStruct((2 * nn, cols), F32),
        compiler_params=_cparams(("arbitrary",)),
        name="hyena_spec_dense",
    )(jnp.asarray(f_real), full)


def _conv_dense_body(*refs, has_mult):
    z_ref, h_ref, ff_ref, fi_ref, bias_ref = refs[:5]
    m_ref = refs[5] if has_mult else None
    o_ref = refs[-1]
    z = z_ref[...]
    nn = z.shape[0]
    x = _dot(ff_ref[...], z, hi=True)
    xr, xi = x[:nn], x[nn:]
    hr, hi_ = h_ref[0:nn, :], h_ref[nn:2 * nn, :]
    y = _dot(fi_ref[...], jnp.concatenate([xr * hr - xi * hi_, xr * hi_ + xi * hr], 0), hi=True)
    out = y + z * bias_ref[...]
    if has_mult:
        out = out * m_ref[...]
    o_ref[...] = out


def hyena_conv_dense(zsrc, zcol, row0, n, n_batch, spec, bias3, layer, order, prev, mult=None):
    _, f_fwd, f_inv = _dense_dft_tables(n)
    nn = 2 * n
    rb, cb = row0 // nn, zcol // LANE
    wb = BRANCH_W // LANE
    in_specs = [pl.BlockSpec((nn, LANE), lambda p, j: (rb + p, cb + j)),
                pl.BlockSpec((2 * nn, LANE), lambda p, j: (0, order * wb + j)),
                pl.BlockSpec((2 * nn, nn), lambda p, j: (0, 0)),
                pl.BlockSpec((nn, 2 * nn), lambda p, j: (0, 0)),
                pl.BlockSpec((None, 1, LANE), lambda p, j: (layer * HY_ORDER + order, 0, j))]
    args = [zsrc, spec, jnp.asarray(f_fwd), jnp.asarray(f_inv), bias3]
    if mult is not None:
        mb = mult[1] // LANE
        in_specs.append(pl.BlockSpec((nn, LANE), lambda p, j: (rb + p, mb + j)))
        args.append(mult[0])
    in_specs.append(pl.BlockSpec(memory_space=pl.ANY))
    args.append(prev)
    return pl.pallas_call(
        functools.partial(_conv_dense_body, has_mult=mult is not None),
        grid=(n_batch // 2, wb),
        in_specs=in_specs,
        out_specs=pl.BlockSpec((nn, LANE), lambda p, j: (rb + p, j)),
        out_shape=jax.ShapeDtypeStruct(prev.shape, F32),
        input_output_aliases={len(args) - 1: 0},
        compiler_params=_cparams(("arbitrary", "arbitrary")),
        name="hyena_conv_dense",
    )(*args)


def _spec_fft_body(x_ref, f1_ref, f2_ref, o_ref, a_ref):
    n1c = o_ref.shape[0]

    def stage1(g, carry):
        n2s = [g * FFT_UNROLL + u for u in range(FFT_UNROLL)]
        xs = [x_ref[pl.ds(n2, n1c, stride=FFT_N2), :] for n2 in n2s]
        res = [_dot(f1_ref[n2], x) for n2, x in zip(n2s, xs)]
        for n2, r in zip(n2s, res):
            a_ref[pl.ds(pl.multiple_of(n2 * 2 * n1c, 2 * n1c), 2 * n1c), :] = r
        return carry

    lax.fori_loop(0, FFT_N2 // FFT_UNROLL, stage1, 0, unroll=2)
    g2 = FFT_UNROLL // 2

    def stage2(g, carry):
        k1s = [g * g2 + u for u in range(g2)]
        blks = [jnp.concatenate([a_ref[pl.ds(k1, FFT_N2, stride=2 * n1c), :],
                                 a_ref[pl.ds(n1c + k1, FFT_N2, stride=2 * n1c), :]], 0) for k1 in k1s]
        res = [_dot(f2_ref[...], blk) for blk in blks]
        for k1, r in zip(k1s, res):
            o_ref[k1] = r
        return carry

    lax.fori_loop(0, n1c // g2, stage2, 0, unroll=2)


def hyena_spec_fft(full, n):
    f1_real, _, _, f2, _ = _two_stage_dft_tables(n)
    nn, cols = full.shape
    n1c = nn // FFT_N2
    const = lambda shape: pl.BlockSpec(shape, lambda j: (0,) * len(shape), pipeline_mode=pl.Buffered(1))
    return pl.pallas_call(
        _spec_fft_body,
        grid=(cols // LANE,),
        in_specs=[pl.BlockSpec((nn, LANE), lambda j: (0, j)),
                  const((FFT_N2, 2 * n1c, n1c)), const((2 * FFT_N2, 2 * FFT_N2))],
        out_specs=pl.BlockSpec((n1c, 2 * FFT_N2, LANE), lambda j: (0, 0, j)),
        out_shape=jax.ShapeDtypeStruct((n1c, 2 * FFT_N2, cols), F32),
        scratch_shapes=[pltpu.VMEM((FFT_N2 * 2 * n1c, LANE), F32)],
        compiler_params=_cparams(("arbitrary",)),
        name="hyena_spec_fft",
    )(full, jnp.asarray(f1_real, BF16), jnp.asarray(f2, BF16))


def _conv_fft_body(*refs, has_mult):
    z_ref, h_ref, f1_ref, f2_ref, f2i_ref, g1_ref, bias_ref = refs[:7]
    m_ref = refs[7] if has_mult else None
    o_ref, a_ref, b_ref = refs[-3], refs[-2], refs[-1]
    n1c = h_ref.shape[0]
    n1h = n1c // 2
    n2c = FFT_N2
    n = n1h * n2c

    def slab(n2):
        return pl.ds(pl.multiple_of(n2 * 2 * n1c, 2 * n1c), 2 * n1c)

    def stage1(g, carry):
        n2s = [g * FFT_UNROLL + u for u in range(FFT_UNROLL)]
        xs = [jnp.concatenate([z_ref[pl.ds(n2, n1h, stride=n2c), :], z_ref[pl.ds(n + n2, n1h, stride=n2c), :]], 0)
              for n2 in n2s]
        res = [_dot(f1_ref[n2], x) for n2, x in zip(n2s, xs)]
        for n2, r in zip(n2s, res):
            a_ref[slab(n2), :] = r
        return carry

    lax.fori_loop(0, n2c // FFT_UNROLL, stage1, 0, unroll=2)
    g2 = FFT_UNROLL // 2

    def stage2(g, carry):
        k1s = [g * g2 + u for u in range(g2)]
        rows = [(pl.ds(k1, n2c, stride=2 * n1c), pl.ds(n1c + k1, n2c, stride=2 * n1c)) for k1 in k1s]
        blks = [jnp.concatenate([a_ref[re, :], a_ref[im, :]], 0) for re, im in rows]
        xs = [_dot(f2_ref[...], blk) for blk in blks]
        ys = []
        for k1, x in zip(k1s, xs):
            xr, xi = x[:n2c], x[n2c:]
            hr, hi_ = h_ref[k1, 0:n2c, :], h_ref[k1, n2c:2 * n2c, :]
            ys.append(jnp.concatenate([xr * hr - xi * hi_, xr * hi_ + xi * hr], 0))
        bs = [_dot(f2i_ref[...], y) for y in ys]
        for (re, im), b in zip(rows, bs):
            b_ref[re, :] = b[:n2c]
            b_ref[im, :] = b[n2c:]
        return carry

    lax.fori_loop(0, n1c // g2, stage2, 0, unroll=2)
    bias = bias_ref[...]

    def stage3(g, carry):
        n2s = [g * FFT_UNROLL + u for u in range(FFT_UNROLL)]
        blks = [b_ref[slab(n2), :] for n2 in n2s]
        ys = [_dot(g1_ref[n2], blk) for n2, blk in zip(n2s, blks)]
        outs = []
        for n2, y in zip(n2s, ys):
            for part, rows in ((y[:n1h], pl.ds(n2, n1h, stride=n2c)), (y[n1h:], pl.ds(n + n2, n1h, stride=n2c))):
                out = part + z_ref[rows, :] * bias
                if has_mult:
                    out = out * m_ref[rows, :]
                outs.append((rows, out))
        for rows, out in outs:
            o_ref[rows, :] = out
        return carry

    lax.fori_loop(0, n2c // FFT_UNROLL, stage3, 0, unroll=2)


def hyena_conv_fft(zsrc, zcol, n, n_batch, spec, bias3, layer, order, t_rows, mult=None):
    _, f1_cplx, g1, f2, f2i = _two_stage_dft_tables(n)
    n1c = 2 * n // FFT_N2
    cb = zcol // LANE
    wb = BRANCH_W // LANE
    const = lambda shape: pl.BlockSpec(shape, lambda j, p: (0,) * len(shape), pipeline_mode=pl.Buffered(1))
    in_specs = [pl.BlockSpec((2 * n, LANE), lambda j, p: (p, cb + j)),
                pl.BlockSpec((n1c, 2 * FFT_N2, LANE), lambda j, p: (0, 0, order * wb + j),
                             pipeline_mode=pl.Buffered(1)),
                const((FFT_N2, 2 * n1c, n1c)), const((2 * FFT_N2, 2 * FFT_N2)), const((2 * FFT_N2, 2 * FFT_N2)),
                const((FFT_N2, n1c, 2 * n1c)),
                pl.BlockSpec((None, 1, LANE), lambda j, p: (layer * HY_ORDER + order, 0, j))]
    args = [zsrc, spec, jnp.asarray(f1_cplx, BF16), jnp.asarray(f2, BF16), jnp.asarray(f2i, BF16),
            jnp.asarray(g1, BF16), bias3]
    if mult is not None:
        mb = mult[1] // LANE
        in_specs.append(pl.BlockSpec((2 * n, LANE), lambda j, p: (p, mb + j)))
        args.append(mult[0])
    return pl.pallas_call(
        functools.partial(_conv_fft_body, has_mult=mult is not None),
        grid=(wb, n_batch // 2),
        in_specs=in_specs,
        out_specs=pl.BlockSpec((2 * n, LANE), lambda j, p: (p, j)),
        out_shape=jax.ShapeDtypeStruct((t_rows, BRANCH_W), F32),
        scratch_shapes=[pltpu.VMEM((FFT_N2 * 2 * n1c, LANE), F32)] * 2,
        compiler_params=_cparams(("arbitrary", "arbitrary")),
        name="hyena_conv_fft",
    )(*args)


def _swap_pairs(x):
    w = x.shape[-1]
    lane = lax.broadcasted_iota(jnp.int32, x.shape, x.ndim - 1)
    return jnp.where(lane % 2 == 0, pltpu.roll(x, w - 1, x.ndim - 1), pltpu.roll(x, 1, x.ndim - 1))


def _attn_prep_body(g_ref, dq_ref, dk_ref, dv_ref, cg_ref, sg_ref, cd_ref, sd_ref, qn_ref, kn_ref,
                    qg_ref, kg_ref, vg_ref, qd_ref, kd_ref, vd_ref, *, lat_blocks):
    is_lat = pl.program_id(0) < lat_blocks
    cg = jnp.where(is_lat, cg_ref[...], 1.0)
    sg = jnp.where(is_lat, sg_ref[...], 0.0)
    cd = jnp.where(is_lat, cd_ref[...], 1.0)
    sd = jnp.where(is_lat, sd_ref[...], 0.0)

    def rope(x, cs, sn):
        return x * cs + _swap_pairs(x) * sn

    def rms(x, w):
        return x * lax.rsqrt(jnp.mean(x * x, axis=-1, keepdims=True) + EPS) * w

    for h in range(HEADS):
        sl = slice(h * HEAD_D, (h + 1) * HEAD_D)
        q = rope(rms(g_ref[:, sl], qn_ref[...]), cg, sg)
        qg_ref[:, sl] = (q * HEAD_D ** -0.5).astype(BF16)
        qd_ref[:, sl] = (rope(dq_ref[:, sl], cd, sd) * DIFF_QK ** -0.5).astype(BF16)
        kd_ref[:, sl] = rope(dk_ref[:, sl], cd, sd).astype(BF16)
    for h in range(GQA_KV):
        sl = slice(h * HEAD_D, (h + 1) * HEAD_D)
        kin = g_ref[:, BRANCH_W + h * HEAD_D:BRANCH_W + (h + 1) * HEAD_D]
        kg_ref[:, sl] = rope(rms(kin, kn_ref[...]), cg, sg).astype(BF16)
    vg_ref[...] = g_ref[:, BRANCH_W + GQA_KV * HEAD_D:BRANCH_W + 2 * GQA_KV * HEAD_D].astype(BF16)
    vd_ref[...] = dv_ref[...].astype(BF16)


def attn_prep(p, ropes, qn3, kn3, layer, n_lat, n_ctx, n_batch):
    t = p.shape[0]
    r = 256 if n_ctx % 256 == 0 else n_ctx
    nlb, ncb = n_lat // r, n_ctx // r
    lat_blocks = n_batch * nlb
    kvw = GQA_KV * HEAD_D
    w = BRANCH_W

    def kv_row(i):
        lat = (i // nlb) * (nlb + ncb) + ncb + i % nlb
        j = i - lat_blocks
        ctx = (j // ncb) * (nlb + ncb) + j % ncb
        return jnp.where(i < lat_blocks, lat, ctx)

    rope_spec = pl.BlockSpec((r, LANE), lambda i: (jnp.where(i < lat_blocks, i % nlb, 0), 0))
    nkv = n_batch * (n_lat + n_ctx)
    return pl.pallas_call(
        functools.partial(_attn_prep_body, lat_blocks=lat_blocks),
        grid=(t // r,),
        in_specs=[pl.BlockSpec((r, 2 * w), lambda i: (i, C_GQA_QKV // (2 * w))),
                  pl.BlockSpec((r, w), lambda i: (i, C_DIFF_Q // w)),
                  pl.BlockSpec((r, w), lambda i: (i, C_DIFF_K // w)),
                  pl.BlockSpec((r, w), lambda i: (i, C_DIFF_V // w)),
                  rope_spec, rope_spec, rope_spec, rope_spec,
                  pl.BlockSpec((None, 1, LANE), lambda i: (layer, 0, 0)),
                  pl.BlockSpec((None, 1, LANE), lambda i: (layer, 0, 0))],
        out_specs=[pl.BlockSpec((r, w), lambda i: (i, 0)),
                   pl.BlockSpec((r, kvw), lambda i: (kv_row(i), 0)),
                   pl.BlockSpec((r, kvw), lambda i: (kv_row(i), 0)),
                   pl.BlockSpec((r, w), lambda i: (i, 0)),
                   pl.BlockSpec((r, w), lambda i: (kv_row(i), 0)),
                   pl.BlockSpec((r, w), lambda i: (kv_row(i), 0))],
        out_shape=[jax.ShapeDtypeStruct((t, w), BF16), jax.ShapeDtypeStruct((nkv, kvw), BF16),
                   jax.ShapeDtypeStruct((nkv, kvw), BF16), jax.ShapeDtypeStruct((t, w), BF16),
                   jax.ShapeDtypeStruct((nkv, w), BF16), jax.ShapeDtypeStruct((nkv, w), BF16)],
        compiler_params=_cparams(("arbitrary",)),
        name="attn_prep",
    )(p, p, p, p, *ropes, qn3, kn3)


def _softmax_parts(s):
    e = jnp.exp(s - jnp.max(s, axis=-1, keepdims=True))
    return e, jnp.sum(e, axis=-1, keepdims=True)


def _gqa_body(q_ref, k_ref, v_ref, *rest):
    o_ref = rest[-1]
    group = HEADS // GQA_KV
    for kvh in range(GQA_KV):
        k = k_ref[:, kvh * HEAD_D:(kvh + 1) * HEAD_D]
        v = v_ref[:, kvh * HEAD_D:(kvh + 1) * HEAD_D]
        for g in range(group):
            sl = slice((kvh * group + g) * HEAD_D, (kvh * group + g + 1) * HEAD_D)
            s = lax.dot_general(q_ref[:, sl], k, (((1,), (1,)), ((), ())), preferred_element_type=F32)
            e, l = _softmax_parts(s)
            o_ref[:, sl] = jnp.dot(e.astype(BF16), v, preferred_element_type=F32) / l


def _diff_body(q_ref, k_ref, v_ref, lam_ref, *rest, lam_init):
    o_ref = rest[-1]
    lam4 = lam_ref[...]
    lam = (jnp.exp(jnp.sum(lam4[0:1] * lam4[1:2], axis=-1, keepdims=True))
           - jnp.exp(jnp.sum(lam4[2:3] * lam4[3:4], axis=-1, keepdims=True)) + lam_init)
    dn = (((1,), (1,)), ((), ()))
    for h in range(HEADS):
        sl = slice(h * HEAD_D, (h + 1) * HEAD_D)
        q = q_ref[:, sl]
        k = k_ref[:, sl]
        v = v_ref[:, sl]
        first = lax.broadcasted_iota(jnp.int32, q.shape, 1) < DIFF_QK
        zero = jnp.zeros_like(q)
        e1, l1 = _softmax_parts(lax.dot_general(jnp.where(first, q, zero), k, dn, preferred_element_type=F32))
        e2, l2 = _softmax_parts(lax.dot_general(jnp.where(first, zero, q), k, dn, preferred_element_type=F32))
        o1 = jnp.dot(e1.astype(BF16), v, preferred_element_type=F32) / l1
        o2 = jnp.dot(e2.astype(BF16), v, preferred_element_type=F32) / l2
        o_ref[:, sl] = o1 - lam * o2


def attention(body, q, k, v, extra, extra_specs, q_row0, nq, kv_per_batch, kv_len, n_batch, tq, name, prev=None):
    t, w = q.shape
    qb0 = q_row0 // tq
    nqb = nq // tq
    kvb = kv_per_batch // kv_len
    in_specs = [pl.BlockSpec((tq, w), lambda b, i: (qb0 + b * nqb + i, 0)),
                pl.BlockSpec((kv_len, k.shape[1]), lambda b, i: (b * kvb, 0)),
                pl.BlockSpec((kv_len, v.shape[1]), lambda b, i: (b * kvb, 0))] + extra_specs
    args = [q, k, v, *extra]
    aliases = {}
    if prev is not None:
        in_specs.append(pl.BlockSpec(memory_space=pl.ANY))
        args.append(prev)
        aliases = {len(args) - 1: 0}
    return pl.pallas_call(
        body,
        grid=(n_batch, nqb),
        in_specs=in_specs,
        out_specs=pl.BlockSpec((tq, w), lambda b, i: (qb0 + b * nqb + i, 0)),
        out_shape=jax.ShapeDtypeStruct((t, w), F32),
        input_output_aliases=aliases,
        compiler_params=_cparams(("arbitrary", "arbitrary")),
        name=name,
    )(*args)


def _merge_body(h_ref, mod_ref, mg_ref, of_ref, ob_ref, ggate_ref, y1_ref, x2_ref, hgate_ref, oc_ref, cgate_ref,
                od_ref, dgate_ref, gnorm_ref, dnorm_ref, wbr_ref, wout_ref, lng_ref, lnb_ref, o_ref, *, diff_scale):
    def rms_heads(x, w):
        parts = []
        for h in range(HEADS):
            xh = x[:, h * HEAD_D:(h + 1) * HEAD_D]
            parts.append(xh * lax.rsqrt(jnp.mean(xh * xh, axis=-1, keepdims=True) + EPS) * w)
        return jnp.concatenate(parts, -1)

    ys = (rms_heads(of_ref[...] + ob_ref[...], gnorm_ref[...]) * _silu(ggate_ref[...]),
          x2_ref[...] * y1_ref[...] * _silu(hgate_ref[...]),
          oc_ref[...] * _silu(cgate_ref[...]),
          rms_heads(od_ref[...], dnorm_ref[...]) * diff_scale * _silu(dgate_ref[...]))
    acc = None
    for n in range(N_BRANCH):
        proj = jnp.dot(ys[n].astype(BF16), wbr_ref[n], preferred_element_type=F32)
        term = _sigmoid(mg_ref[:, n * D_MODEL:(n + 1) * D_MODEL]) * proj
        acc = term if acc is None else acc + term
    out = jnp.dot(acc.astype(BF16), wout_ref[...], preferred_element_type=F32)
    x = ALPHA * h_ref[...] + mod_ref[2:3, :] * out
    mu = jnp.mean(x, axis=-1, keepdims=True)
    xc = x - mu
    var = jnp.mean(xc * xc, axis=-1, keepdims=True)
    o_ref[...] = xc * lax.rsqrt(var + EPS) * lng_ref[...] + lnb_ref[...]


def merge_postnorm(h_all, mod3, p, o_f, o_b, y1, xv, oc, od, gnorm3, dnorm3, wbr, wout, lng3, lnb3, layer, lam_init,
                   n_lat, n_batch):
    t, d = h_all.shape
    r = 256 if n_lat % 256 == 0 else 64
    w = BRANCH_W
    lbb = n_lat // r
    row = lambda i: jnp.minimum(i // lbb, n_batch)
    tok = lambda cb: pl.BlockSpec((r, w), lambda i: (i, cb))
    vec = lambda width: pl.BlockSpec((None, 1, width), lambda i: (layer, 0, 0))
    return pl.pallas_call(
        functools.partial(_merge_body, diff_scale=1.0 - lam_init),
        grid=(t // r,),
        in_specs=[pl.BlockSpec((r, d), lambda i: (i, 0)),
                  pl.BlockSpec((None, 3, d), lambda i: (row(i), 0, 0)),
                  pl.BlockSpec((r, N_BRANCH * d), lambda i: (i, C_MERGE // (N_BRANCH * d))),
                  tok(0), tok(0), tok(C_GDN_GATE // w), tok(0), tok(1), tok(C_HY_GATE // w), tok(0),
                  tok(C_GQA_GATE // w), tok(0), tok(C_DIFF_GATE // w),
                  vec(LANE), vec(LANE),
                  pl.BlockSpec((None, N_BRANCH, w, d), lambda i: (layer, 0, 0, 0)),
                  pl.BlockSpec((None, d, d), lambda i: (layer, 0, 0)),
                  vec(d), vec(d)],
        out_specs=pl.BlockSpec((r, d), lambda i: (i, 0)),
        out_shape=jax.ShapeDtypeStruct((t, d), F32),
        compiler_params=_cparams(("arbitrary",)),
        name="merge_postnorm",
    )(h_all, mod3, p, o_f, o_b, p, y1, xv, p, oc, p, od, p, gnorm3, dnorm3, wbr, wout, lng3, lnb3)


def _rope_tables(n_lat, dim):
    rows = n_lat // GRID_W
    row = jnp.repeat(jnp.arange(rows, dtype=F32), GRID_W)
    col = jnp.tile(jnp.arange(GRID_W, dtype=F32), rows)
    half = dim // 2
    inv = ROPE_THETA ** (-jnp.arange(0, half, 2, dtype=F32) / half)
    ang = jnp.concatenate([row[:, None] * inv, col[:, None] * inv], -1)
    cos = jnp.repeat(jnp.cos(ang), 2, axis=-1)
    sin = jnp.repeat(jnp.sin(ang), 2, axis=-1)
    sign = jnp.tile(jnp.array([-1.0, 1.0], F32), dim // 2)
    reps = LANE // dim
    return jnp.tile(cos, (1, reps)), jnp.tile(sin * sign, (1, reps))


def kernel(x, c, ctx, c_ctx, w_ada, b_ada, w_in, gdn_conv, gdn_a_log, gdn_dt_bias, gdn_norm, hy_conv, hy_w1, hy_b1,
           hy_w2, hy_b2, hy_w3, hy_b3, hy_w4, hy_freq, hy_bias, gqa_qn, gqa_kn, diff_lam, diff_norm, w_br, w_out,
           ln_g, ln_b):
    nb, n_lat, d = x.shape
    n_ctx = ctx.shape[1]
    t_lat, t_ctx = nb * n_lat, nb * n_ctx
    depth = w_in.shape[0]
    w = BRANCH_W

    w_main = jnp.concatenate([w_in[:, :, O_MERGE:], w_in[:, :, :O_GDN_AB], w_in[:, :, O_GDN_AB + 4 * HEADS:O_MERGE]],
                             axis=2).astype(BF16)
    w_ab = jnp.pad(w_in[:, :, O_GDN_AB:O_GDN_AB + 4 * HEADS], ((0, 0), (0, 0), (0, LANE - 4 * HEADS)))
    wbr_bf = w_br.astype(BF16)
    wout_bf = w_out.astype(BF16)
    b_ada3 = b_ada[:, None, :]
    cvec = jnp.concatenate([c, c_ctx[None, :], jnp.zeros((SUB - nb - 1, d), F32)], 0)
    as3 = lambda a: a[:, None, :]
    gdn_par_r = jnp.pad(jnp.stack([gdn_a_log.reshape(depth, -1), gdn_dt_bias.reshape(depth, -1)], 1),
                        ((0, 0), (0, SUB - 2), (0, LANE - 2 * HEADS)))
    gdn_par_c = jnp.pad(jnp.stack([gdn_a_log.reshape(depth, -1), gdn_dt_bias.reshape(depth, -1)], 2),
                        ((0, 0), (0, 2 * HEADS), (0, LANE - 2)))
    hy_w1p = jnp.pad(hy_w1, ((0, 0), (0, LANE - HY_EMB), (0, 0)))
    hy_bias3 = hy_bias.reshape(depth * HY_ORDER, 1, w)
    ropes = _rope_tables(n_lat, HEAD_D) + _rope_tables(n_lat, DIFF_QK)

    tm = 1024 if (n_lat % 1024 == 0 and t_ctx % 1024 == 0) else n_ctx
    h_all = jnp.concatenate([x.reshape(t_lat, d), ctx.reshape(t_ctx, d)], 0)
    for l in range(depth):
        lam_init = 0.8 - 0.6 * math.exp(-0.3 * l)
        mod3 = ada_mod(cvec, w_ada, b_ada3, l).reshape(SUB, 3, d)
        p, ab = in_proj(h_all, mod3, w_main, w_ab, l, tm, n_lat // tm, nb)

        qkv = dwconv(p, gdn_conv, l, C_GDN_QKV, 3 * w, n_lat, n_ctx, nb, act=True)
        ab_rows = jnp.transpose(ab[:, :4 * HEADS].reshape(-1, GDN_CHUNK, 4 * HEADS), (0, 2, 1))
        o_f, o_b = gdn_scan(qkv, ab, ab_rows, gdn_par_r[l], gdn_par_c[l], n_lat, n_ctx, nb)

        xv = dwconv(p, hy_conv, l, C_HY_XV, 3 * w, n_lat, n_ctx, nb, act=False)
        filt = lambda n: hyena_filter(n, hy_w1p, as3(hy_b1), hy_w2, as3(hy_b2), hy_w3, as3(hy_b3), hy_w4,
                                      as3(hy_freq), l)
        spec_lat = hyena_spec_fft(filt(n_lat), n_lat)
        spec_ctx = hyena_spec_dense(filt(n_ctx), n_ctx)
        z1 = hyena_conv_fft(xv, 2 * w, n_lat, nb, spec_lat, hy_bias3, l, 0, t_lat + t_ctx, mult=(xv, 0))
        z1 = hyena_conv_dense(xv, 2 * w, t_lat, n_ctx, nb, spec_ctx, hy_bias3, l, 0, z1, mult=(xv, 0))
        y1 = hyena_conv_fft(z1, 0, n_lat, nb, spec_lat, hy_bias3, l, 1, t_lat + t_ctx)
        y1 = hyena_conv_dense(z1, 0, t_lat, n_ctx, nb, spec_ctx, hy_bias3, l, 1, y1)

        qg, kg, vg, qd, kd, vd = attn_prep(p, ropes, as3(gqa_qn), as3(gqa_kn), l, n_lat, n_ctx, nb)
        kv_all = n_lat + n_ctx
        tq = min(256, n_ctx)
        lam_spec = [pl.BlockSpec((None, 4, DIFF_QK), lambda b, i: (l, 0, 0))]
        diff_body = functools.partial(_diff_body, lam_init=lam_init)
        oc = attention(_gqa_body, qg, kg, vg, (), [], 0, n_lat, kv_all, kv_all, nb, tq, "gqa_lat")
        oc = attention(_gqa_body, qg, kg, vg, (), [], t_lat, n_ctx, kv_all, n_ctx, nb, tq, "gqa_ctx", prev=oc)
        od = attention(diff_body, qd, kd, vd, (diff_lam,), lam_spec, 0, n_lat, kv_all, kv_all, nb, tq, "diff_lat")
        od = attention(diff_body, qd, kd, vd, (diff_lam,), lam_spec, t_lat, n_ctx, kv_all, n_ctx, nb, tq, "diff_ctx",
                       prev=od)

        h_all = merge_postnorm(h_all, mod3, p, o_f, o_b, y1, xv, oc, od, as3(gdn_norm), as3(diff_norm), wbr_bf, wout_bf,
                               as3(ln_g), as3(ln_b), l, lam_init, n_lat, nb)
    return h_all[:t_lat].reshape(nb, n_lat, d)
```

```python
import functools
import math

import numpy as np
import jax
import jax.numpy as jnp
from jax import lax
from jax.experimental import pallas as pl
from jax.experimental.pallas import tpu as pltpu

F32 = jnp.float32
BF16 = jnp.bfloat16
HI = lax.Precision.HIGHEST

D_MODEL = 1024
DEPTH = 4
GRID_W = 64
BRANCH_W = D_MODEL // 2
N_BRANCH = 4
HEADS = 4
HEAD_D = BRANCH_W // HEADS
GDN_CONV = 4
GDN_CHUNK = 64
GDN_SUB = 4
HY_CONV = 3
HY_EMB = 33
HY_BANDS = (HY_EMB - 1) // 2
HY_FH = 64
HY_ORDER = 2
HY_MIN_DECAY = math.log(1e-2) / 1.5
HY_MAX_DECAY = math.log(1e-2) / 0.3
GQA_KV = 2
DIFF_QK = HEAD_D // 2
ROPE_THETA = 10000.0
EPS = 1e-6
ALPHA = (2.0 * DEPTH) ** 0.25

LANE = 128
SUB = 8
HALO = 16
FFT_N2 = 128
FFT_UNROLL = 8
VMEM_LIMIT = 60 * 1024 * 1024

C_MERGE = 0
C_GDN_QKV = 4096
C_GDN_GATE = 5632
C_HY_XV = 6144
C_HY_GATE = 7680
C_GQA_QKV = 8192
C_GQA_GATE = 9216
C_DIFF_Q = 9728
C_DIFF_K = 10240
C_DIFF_V = 10752
C_DIFF_GATE = 11264
N_MAIN = 11776
O_GDN_AB = 1536
O_MERGE = 7696


def _cparams(sem):
    return pltpu.CompilerParams(dimension_semantics=sem, vmem_limit_bytes=VMEM_LIMIT)


def _dot(a, b, hi=False):
    if hi:
        return jnp.dot(a, b, precision=HI, preferred_element_type=F32)
    return jnp.dot(a.astype(BF16), b.astype(BF16), preferred_element_type=F32)


def _dot_nt(a, b, hi=False):
    dn = (((1,), (1,)), ((), ()))
    if hi:
        return lax.dot_general(a, b, dn, precision=HI, preferred_element_type=F32)
    return lax.dot_general(a.astype(BF16), b.astype(BF16), dn, preferred_element_type=F32)


def _dot_tn(a, b):
    return lax.dot_general(a.astype(BF16), b.astype(BF16), (((0,), (0,)), ((), ())), preferred_element_type=F32)


def _sigmoid(x):
    return 1.0 / (1.0 + jnp.exp(-x))


def _silu(x):
    return x * _sigmoid(x)


def _softplus(x):
    return jnp.maximum(x, 0.0) + jnp.log1p(jnp.exp(-jnp.abs(x)))


def _ada_body(c_ref, w_ref, b_ref, o_ref):
    o_ref[...] = _dot(_silu(c_ref[...]), w_ref[...], hi=True) + b_ref[...]


def ada_mod(cvec, w_ada, b_ada3, layer):
    d = cvec.shape[1]
    tn = 512
    return pl.pallas_call(
        _ada_body,
        grid=(3 * d // tn,),
        in_specs=[pl.BlockSpec((SUB, d), lambda j: (0, 0)),
                  pl.BlockSpec((None, d, tn), lambda j: (layer, 0, j)),
                  pl.BlockSpec((None, 1, tn), lambda j: (layer, 0, j))],
        out_specs=pl.BlockSpec((SUB, tn), lambda j: (0, j)),
        out_shape=jax.ShapeDtypeStruct((SUB, 3 * d), F32),
        compiler_params=_cparams(("arbitrary",)),
        name="ada_mod",
    )(cvec, w_ada, b_ada3)


def _inproj_body(h_ref, mod_ref, w_ref, wab_ref, o_ref, ab_ref, u_ref):
    @pl.when(pl.program_id(1) == 0)
    def _():
        x = h_ref[...]
        mu = jnp.mean(x, axis=-1, keepdims=True)
        xc = x - mu
        var = jnp.mean(xc * xc, axis=-1, keepdims=True)
        u = xc * lax.rsqrt(var + EPS) * (1.0 + mod_ref[1:2, :]) + mod_ref[0:1, :]
        u_ref[...] = u.astype(BF16)
        ab_ref[...] = _dot(u, wab_ref[...], hi=True)

    o_ref[...] = jnp.dot(u_ref[...], w_ref[...], preferred_element_type=F32).astype(BF16)


def in_proj(h_all, mod3, w_main, w_ab, layer, tm, lat_blocks_per_batch, n_batch):
    t, d = h_all.shape
    n_main = w_main.shape[2]
    tn = n_main // 4
    row = lambda i: jnp.minimum(i // lat_blocks_per_batch, n_batch)
    return pl.pallas_call(
        _inproj_body,
        grid=(t // tm, n_main // tn),
        in_specs=[pl.BlockSpec((tm, d), lambda i, j: (i, 0)),
                  pl.BlockSpec((None, 3, d), lambda i, j: (row(i), 0, 0)),
                  pl.BlockSpec((None, d, tn), lambda i, j: (layer, 0, j)),
                  pl.BlockSpec((None, d, LANE), lambda i, j: (layer, 0, 0))],
        out_specs=[pl.BlockSpec((tm, tn), lambda i, j: (i, j)),
                   pl.BlockSpec((tm, LANE), lambda i, j: (i, 0))],
        out_shape=[jax.ShapeDtypeStruct((t, n_main), BF16), jax.ShapeDtypeStruct((t, LANE), F32)],
        scratch_shapes=[pltpu.VMEM((tm, d), BF16)],
        compiler_params=_cparams(("arbitrary", "arbitrary")),
        name="in_proj",
    )(h_all, mod3, w_main, w_ab)


def _dwconv_body(xp_ref, x_ref, xn_ref, w_ref, o_ref, pad_ref, *, taps, pad_l, t_lat, n_lat, n_ctx, sb, act):
    i = pl.program_id(0)
    r = x_ref.shape[0]
    pad_ref[0:HALO, :] = xp_ref[...].astype(F32)
    pad_ref[HALO:HALO + r, :] = x_ref[...].astype(F32)
    pad_ref[HALO + r:2 * HALO + r, :] = xn_ref[...].astype(F32)
    row = lax.broadcasted_iota(jnp.int32, (sb, 1), 0)
    for k in range(r // sb):
        g0 = i * r + k * sb
        in_lat = g0 < t_lat
        starts = jnp.where(in_lat, g0 % n_lat == 0, (g0 - t_lat) % n_ctx == 0)
        ends = jnp.where(in_lat, (g0 + sb) % n_lat == 0, (g0 + sb - t_lat) % n_ctx == 0)
        acc = None
        for j in range(taps):
            d = j - pad_l
            off = HALO + k * sb + d
            xs = pad_ref[off:off + sb, :]
            if d < 0:
                xs = jnp.where(jnp.logical_and(starts, row < -d), 0.0, xs)
            elif d > 0:
                xs = jnp.where(jnp.logical_and(ends, row >= sb - d), 0.0, xs)
            term = w_ref[j:j + 1, :] * xs
            acc = term if acc is None else acc + term
        if act:
            acc = _silu(acc)
        o_ref[k * sb:(k + 1) * sb, :] = acc


def dwconv(p, w_conv, layer, col0, width, n_lat, n_ctx, n_batch, act):
    t = p.shape[0]
    taps = w_conv.shape[1]
    sb = min(256, n_ctx)
    r = 1024 if t % 1024 == 0 else sb
    lw = 512
    cb = col0 // lw
    rs = r // HALO
    body = functools.partial(_dwconv_body, taps=taps, pad_l=(taps - 1) // 2, t_lat=n_batch * n_lat, n_lat=n_lat,
                             n_ctx=n_ctx, sb=sb, act=act)
    return pl.pallas_call(
        body,
        grid=(t // r, width // lw),
        in_specs=[pl.BlockSpec((HALO, lw), lambda i, j: (jnp.maximum(i * rs - 1, 0), cb + j)),
                  pl.BlockSpec((r, lw), lambda i, j: (i, cb + j)),
                  pl.BlockSpec((HALO, lw), lambda i, j: (jnp.minimum((i + 1) * rs, t // HALO - 1), cb + j)),
                  pl.BlockSpec((None, taps, lw), lambda i, j: (layer, 0, j))],
        out_specs=pl.BlockSpec((r, lw), lambda i, j: (i, j)),
        out_shape=jax.ShapeDtypeStruct((t, width), F32),
        scratch_shapes=[pltpu.VMEM((r + 2 * HALO, lw), F32)],
        compiler_params=_cparams(("arbitrary", "arbitrary")),
        name="dwconv",
    )(p, p, p, w_conv)


def _gdn_body(qf_ref, qb_ref, abcf_ref, abcb_ref, abrf_ref, abrb_ref, pr_ref, pc_ref, of_ref, ob_ref, s_ref):
    c = GDN_CHUNK

    @pl.when(pl.program_id(1) == 0)
    def _():
        s_ref[...] = jnp.zeros_like(s_ref)

    ii = lax.broadcasted_iota(jnp.int32, (c, c), 0)
    jj = lax.broadcasted_iota(jnp.int32, (c, c), 1)
    lmat = (jj <= ii).astype(F32)
    eye = (jj == ii).astype(F32)
    alr, dtr = pr_ref[0:1, :], pr_ref[1:2, :]
    alc, dtc = pc_ref[:, 0:1], pc_ref[:, 1:2]
    chains = []
    for d in range(2):
        qkv_ref = (qf_ref, qb_ref)[d]
        abc_ref = (abcf_ref, abcb_ref)[d]
        abr_ref = (abrf_ref, abrb_ref)[d]
        incl = (jj <= ii) if d == 0 else (jj >= ii)
        strict = (jj < ii) if d == 0 else (jj > ii)
        for j in range(GDN_SUB):
            rows = slice(j * c, (j + 1) * c)
            abc = abc_ref[rows, :]
            abr = abr_ref[j]
            g_c = -jnp.exp(alr) * _softplus(abc + dtr)
            g_r = -jnp.exp(alc) * _softplus(abr + dtc)
            cum_c = _dot(lmat, g_c, hi=True)
            cum_r = _dot_nt(g_r, lmat, hi=True)
            if d == 1:
                cum_c = cum_c[c - 1:c, :] - cum_c + g_c
                cum_r = cum_r[:, c - 1:c] - cum_r + g_r
            beta_all = _sigmoid(abc)
            for h in range(HEADS):
                idx = HEADS * d + h
                q = qkv_ref[rows, h * HEAD_D:(h + 1) * HEAD_D]
                k = qkv_ref[rows, BRANCH_W + h * HEAD_D:BRANCH_W + (h + 1) * HEAD_D]
                v = qkv_ref[rows, 2 * BRANCH_W + h * HEAD_D:2 * BRANCH_W + (h + 1) * HEAD_D]
                q = q * lax.rsqrt(jnp.sum(q * q, axis=-1, keepdims=True) + EPS) * (HEAD_D ** -0.5)
                k = k * lax.rsqrt(jnp.sum(k * k, axis=-1, keepdims=True) + EPS)
                cc = cum_c[:, idx:idx + 1]
                cr = cum_r[idx:idx + 1, :]
                dec = jnp.exp(jnp.where(incl, cc - cr, -1e30))
                beta = beta_all[:, 2 * HEADS + idx:2 * HEADS + idx + 1]
                ecum = jnp.exp(cc)
                tot = cc[c - 1:c, :] if d == 0 else cc[0:1, :]
                chains.append(dict(d=d, h=h, j=j, rows=rows, q=q, k=k, dec=dec, strict=strict, beta=beta, ecum=ecum,
                                   tot=tot, rhs=jnp.concatenate([k * (beta * ecum), v * beta], 1),
                                   k_tail=k * jnp.exp(tot - cc)))
    for ch in chains:
        ch["kk"] = _dot_nt(ch["k"], ch["k"])
        ch["qk"] = _dot_nt(ch["q"], ch["k"])
    for ch in chains:
        ch["p"] = -jnp.where(ch["strict"], ch["beta"] * ch["kk"] * ch["dec"], 0.0)
        ch["inv"] = eye + ch["p"]
    for _ in range(int(math.log2(c)) - 1):
        for ch in chains:
            ch["p"] = _dot(ch["p"], ch["p"])
        for ch in chains:
            ch["inv"] = ch["inv"] + _dot(ch["inv"], ch["p"])
    for ch in chains:
        ch["wu"] = _dot(ch["inv"], ch["rhs"])
        ch["lhs"] = jnp.concatenate([ch["wu"][:, :HEAD_D], ch["q"] * ch["ecum"]], 0)
    state = {(d, h): s_ref[d, h] for d in range(2) for h in range(HEADS)}
    for step in range(GDN_SUB):
        cur = [ch for ch in chains if ch["j"] == (step if ch["d"] == 0 else GDN_SUB - 1 - step)]
        for ch in cur:
            ch["ws"] = _dot(ch["lhs"], state[ch["d"], ch["h"]])
        for ch in cur:
            ch["v_new"] = ch["wu"][:, HEAD_D:] - ch["ws"][:c]
        for ch in cur:
            out_ref = (of_ref, ob_ref)[ch["d"]]
            h = ch["h"]
            out_ref[ch["rows"], h * HEAD_D:(h + 1) * HEAD_D] = ch["ws"][c:] + _dot(ch["qk"] * ch["dec"], ch["v_new"])
            state[ch["d"], h] = state[ch["d"], h] * jnp.exp(ch["tot"]) + _dot_tn(ch["k_tail"], ch["v_new"])
    for (d, h), val in state.items():
        s_ref[d, h] = val


def gdn_scan(qkv, ab, ab_rows, par_r, par_c, n_lat, n_ctx, n_batch):
    t = qkv.shape[0]
    c = GDN_SUB * GDN_CHUNK
    nlc, ncc = n_lat // c, n_ctx // c
    base = n_batch * nlc

    def fwd(b, s):
        return jnp.where(s < ncc, base + b * ncc + s, b * nlc + (s - ncc))

    def bwd(b, s):
        return jnp.where(s < ncc, base + b * ncc + (ncc - 1 - s), b * nlc + (nlc - 1 - (s - ncc)))

    w3 = 3 * BRANCH_W
    return pl.pallas_call(
        _gdn_body,
        grid=(n_batch, ncc + nlc),
        in_specs=[pl.BlockSpec((c, w3), lambda b, s: (fwd(b, s), 0)),
                  pl.BlockSpec((c, w3), lambda b, s: (bwd(b, s), 0)),
                  pl.BlockSpec((c, LANE), lambda b, s: (fwd(b, s), 0)),
                  pl.BlockSpec((c, LANE), lambda b, s: (bwd(b, s), 0)),
                  pl.BlockSpec((GDN_SUB, 4 * HEADS, GDN_CHUNK), lambda b, s: (fwd(b, s), 0, 0)),
                  pl.BlockSpec((GDN_SUB, 4 * HEADS, GDN_CHUNK), lambda b, s: (bwd(b, s), 0, 0)),
                  pl.BlockSpec((SUB, LANE), lambda b, s: (0, 0)),
                  pl.BlockSpec((4 * HEADS, LANE), lambda b, s: (0, 0))],
        out_specs=[pl.BlockSpec((c, BRANCH_W), lambda b, s: (fwd(b, s), 0)),
                   pl.BlockSpec((c, BRANCH_W), lambda b, s: (bwd(b, s), 0))],
        out_shape=[jax.ShapeDtypeStruct((t, BRANCH_W), F32), jax.ShapeDtypeStruct((t, BRANCH_W), F32)],
        scratch_shapes=[pltpu.VMEM((2, HEADS, HEAD_D, HEAD_D), F32)],
        compiler_params=_cparams(("arbitrary", "arbitrary")),
        name="gdn_scan",
    )(qkv, qkv, ab, ab, ab_rows, ab_rows, par_r, par_c)


def _hyfilt_body(z_ref, aux_ref, w1_ref, b1_ref, w2_ref, b2_ref, w3_ref, b3_ref, w4_ref, fr_ref, dl_ref, o_ref):
    fr = fr_ref[...]
    h = jnp.sin(fr * (_dot(z_ref[...], w1_ref[...], hi=True) + b1_ref[...]))
    h = jnp.sin(fr * (_dot(h, w2_ref[...], hi=True) + b2_ref[...]))
    h = jnp.sin(fr * (_dot(h, w3_ref[...], hi=True) + b3_ref[...]))
    taps = _dot(h, w4_ref[...], hi=True) * jnp.exp(-aux_ref[:, 0:1] * dl_ref[...])
    w = BRANCH_W
    negative = aux_ref[:, 1:2] > 0.5
    keep = aux_ref[:, 2:3]
    for o in range(HY_ORDER):
        fwd = taps[:, o * 2 * w:o * 2 * w + w]
        bwd = taps[:, o * 2 * w + w:(o + 1) * 2 * w]
        o_ref[:, o * w:(o + 1) * w] = jnp.where(negative, bwd, fwd) * keep


def hyena_filter(n, w1p, b1, w2, b2, w3, b3, w4, fr, layer):
    row = jnp.arange(2 * n)
    src = jnp.where(row <= n, row, 2 * n - row)
    pos = jnp.where(row == n, 0, src).astype(F32)
    tt = pos / max(n - 1, 1)
    ang = (2.0 * math.pi / n) * pos[:, None] * jnp.linspace(1e-4, HY_BANDS - 1, HY_BANDS, dtype=F32)
    z = jnp.concatenate([tt[:, None], jnp.cos(ang), -jnp.sin(ang), jnp.zeros((2 * n, LANE - HY_EMB), F32)], -1)
    aux = jnp.stack([tt, (row > n).astype(F32), (row != n).astype(F32)], 1)
    aux = jnp.pad(aux, ((0, 0), (0, SUB - 3)))
    deltas = jnp.abs(jnp.linspace(HY_MIN_DECAY, HY_MAX_DECAY, BRANCH_W, dtype=F32))
    dl = jnp.tile(deltas, 2 * HY_ORDER)[None, :]
    r = 512
    wo = 2 * HY_ORDER * BRANCH_W
    full = lambda shape: pl.BlockSpec((None,) + shape, lambda i: (layer,) + (0,) * len(shape))
    return pl.pallas_call(
        _hyfilt_body,
        grid=(2 * n // r,),
        in_specs=[pl.BlockSpec((r, LANE), lambda i: (i, 0)),
                  pl.BlockSpec((r, SUB), lambda i: (i, 0)),
                  full((LANE, HY_FH)), full((1, HY_FH)), full((HY_FH, HY_FH)), full((1, HY_FH)),
                  full((HY_FH, HY_FH)), full((1, HY_FH)), full((HY_FH, wo)), full((1, HY_FH)),
                  pl.BlockSpec((1, wo), lambda i: (0, 0))],
        out_specs=pl.BlockSpec((r, HY_ORDER * BRANCH_W), lambda i: (i, 0)),
        out_shape=jax.ShapeDtypeStruct((2 * n, HY_ORDER * BRANCH_W), F32),
        compiler_params=_cparams(("arbitrary",)),
        name="hyena_filter",
    )(z, aux, w1p, b1, w2, b2, w3, b3, w4, fr, dl)


@functools.lru_cache(maxsize=None)
def _dense_dft_tables(n):
    nn = 2 * n
    k = np.arange(nn)[:, None].astype(np.float64)
    m = np.arange(nn)[None, :].astype(np.float64)
    ang = -2.0 * np.pi * k * m / nn
    wr, wi = np.cos(ang), np.sin(ang)
    f_real = np.concatenate([wr, wi], 0)
    wr_h, wi_h = wr[:, :n], wi[:, :n]
    f_fwd = np.block([[wr_h, -wi_h], [wi_h, wr_h]])
    cr, ci = wr.T[:n] / nn, -wi.T[:n] / nn
    f_inv = np.block([[cr, -ci], [ci, cr]])
    return (np.asarray(f_real, np.float32), np.asarray(f_fwd, np.float32), np.asarray(f_inv, np.float32))


@functools.lru_cache(maxsize=None)
def _two_stage_dft_tables(n):
    nn = 2 * n
    n2c = FFT_N2
    n1c = nn // n2c
    n1h = n1c // 2
    k1 = np.arange(n1c).astype(np.float64)
    n1 = np.arange(n1c).astype(np.float64)
    n2 = np.arange(n2c).astype(np.float64)
    ang = -2.0 * np.pi * (k1[None, :, None] * n1[None, None, :] / n1c + n2[:, None, None] * k1[None, :, None] / nn)
    mr, mi = np.cos(ang), np.sin(ang)
    f1_real = np.concatenate([mr, mi], 1)
    mrh, mih = mr[:, :, :n1h], mi[:, :, :n1h]
    f1_cplx = np.concatenate([np.concatenate([mrh, -mih], 2), np.concatenate([mih, mrh], 2)], 1)
    gr = np.transpose(mr, (0, 2, 1))[:, :n1h, :] / nn
    gi = -np.transpose(mi, (0, 2, 1))[:, :n1h, :] / nn
    g1 = np.concatenate([np.concatenate([gr, -gi], 2), np.concatenate([gi, gr], 2)], 1)
    k2 = np.arange(n2c).astype(np.float64)
    a2 = -2.0 * np.pi * k2[:, None] * n2[None, :] / n2c
    fr, fi = np.cos(a2), np.sin(a2)
    f2 = np.block([[fr, -fi], [fi, fr]])
    f2i = np.block([[fr.T, fi.T], [-fi.T, fr.T]])
    f32 = lambda a: np.asarray(a, np.float32)
    return f32(f1_real), f32(f1_cplx), f32(g1), f32(f2), f32(f2i)


def _spec_dense_body(f_ref, x_ref, o_ref):
    o_ref[...] = _dot(f_ref[...], x_ref[...], hi=True)


def hyena_spec_dense(full, n):
    f_real, _, _ = _dense_dft_tables(n)
    nn, cols = full.shape
    return pl.pallas_call(
        _spec_dense_body,
        grid=(cols // LANE,),
        in_specs=[pl.BlockSpec((2 * nn, nn), lambda j: (0, 0)),
                  pl.BlockSpec((nn, LANE), lambda j: (0, j))],
        out_specs=pl.BlockSpec((2 * nn, LANE), lambda j: (0, j)),
        out_shape=jax.ShapeDtypeStruct((2 * nn, cols), F32),
        compiler_params=_cparams(("arbitrary",)),
        name="hyena_spec_dense",
    )(jnp.asarray(f_real), full)


def _conv_dense_body(*refs, has_mult):
    z_ref, h_ref, ff_ref, fi_ref, bias_ref = refs[:5]
    m_ref = refs[5] if has_mult else None
    o_ref = refs[-1]
    z = z_ref[...]
    nn = z.shape[0]
    x = _dot(ff_ref[...], z, hi=True)
    xr, xi = x[:nn], x[nn:]
    hr, hi_ = h_ref[0:nn, :], h_ref[nn:2 * nn, :]
    y = _dot(fi_ref[...], jnp.concatenate([xr * hr - xi * hi_, xr * hi_ + xi * hr], 0), hi=True)
    out = y + z * bias_ref[...]
    if has_mult:
        out = out * m_ref[...]
    o_ref[...] = out


def hyena_conv_dense(zsrc, zcol, row0, n, n_batch, spec, bias3, layer, order, prev, mult=None):
    _, f_fwd, f_inv = _dense_dft_tables(n)
    nn = 2 * n
    rb, cb = row0 // nn, zcol // LANE
    wb = BRANCH_W // LANE
    in_specs = [pl.BlockSpec((nn, LANE), lambda p, j: (rb + p, cb + j)),
                pl.BlockSpec((2 * nn, LANE), lambda p, j: (0, order * wb + j)),
                pl.BlockSpec((2 * nn, nn), lambda p, j: (0, 0)),
                pl.BlockSpec((nn, 2 * nn), lambda p, j: (0, 0)),
                pl.BlockSpec((None, 1, LANE), lambda p, j: (layer * HY_ORDER + order, 0, j))]
    args = [zsrc, spec, jnp.asarray(f_fwd), jnp.asarray(f_inv), bias3]
    if mult is not None:
        mb = mult[1] // LANE
        in_specs.append(pl.BlockSpec((nn, LANE), lambda p, j: (rb + p, mb + j)))
        args.append(mult[0])
    in_specs.append(pl.BlockSpec(memory_space=pl.ANY))
    args.append(prev)
    return pl.pallas_call(
        functools.partial(_conv_dense_body, has_mult=mult is not None),
        grid=(n_batch // 2, wb),
        in_specs=in_specs,
        out_specs=pl.BlockSpec((nn, LANE), lambda p, j: (rb + p, j)),
        out_shape=jax.ShapeDtypeStruct(prev.shape, F32),
        input_output_aliases={len(args) - 1: 0},
        compiler_params=_cparams(("arbitrary", "arbitrary")),
        name="hyena_conv_dense",
    )(*args)


def _spec_fft_body(x_ref, f1_ref, f2_ref, o_ref, a_ref):
    n1c = o_ref.shape[0]

    def stage1(g, carry):
        n2s = [g * FFT_UNROLL + u for u in range(FFT_UNROLL)]
        xs = [x_ref[pl.ds(n2, n1c, stride=FFT_N2), :] for n2 in n2s]
        res = [_dot(f1_ref[n2], x) for n2, x in zip(n2s, xs)]
        for n2, r in zip(n2s, res):
            a_ref[pl.ds(pl.multiple_of(n2 * 2 * n1c, 2 * n1c), 2 * n1c), :] = r
        return carry

    lax.fori_loop(0, FFT_N2 // FFT_UNROLL, stage1, 0, unroll=2)
    g2 = FFT_UNROLL // 2

    def stage2(g, carry):
        k1s = [g * g2 + u for u in range(g2)]
        blks = [jnp.concatenate([a_ref[pl.ds(k1, FFT_N2, stride=2 * n1c), :],
                                 a_ref[pl.ds(n1c + k1, FFT_N2, stride=2 * n1c), :]], 0) for k1 in k1s]
        res = [_dot(f2_ref[...], blk) for blk in blks]
        for k1, r in zip(k1s, res):
            o_ref[k1] = r
        return carry

    lax.fori_loop(0, n1c // g2, stage2, 0, unroll=2)


def hyena_spec_fft(full, n):
    f1_real, _, _, f2, _ = _two_stage_dft_tables(n)
    nn, cols = full.shape
    n1c = nn // FFT_N2
    const = lambda shape: pl.BlockSpec(shape, lambda j: (0,) * len(shape), pipeline_mode=pl.Buffered(1))
    return pl.pallas_call(
        _spec_fft_body,
        grid=(cols // LANE,),
        in_specs=[pl.BlockSpec((nn, LANE), lambda j: (0, j)),
                  const((FFT_N2, 2 * n1c, n1c)), const((2 * FFT_N2, 2 * FFT_N2))],
        out_specs=pl.BlockSpec((n1c, 2 * FFT_N2, LANE), lambda j: (0, 0, j)),
        out_shape=jax.ShapeDtypeStruct((n1c, 2 * FFT_N2, cols), F32),
        scratch_shapes=[pltpu.VMEM((FFT_N2 * 2 * n1c, LANE), F32)],
        compiler_params=_cparams(("arbitrary",)),
        name="hyena_spec_fft",
    )(full, jnp.asarray(f1_real, BF16), jnp.asarray(f2, BF16))


def _conv_fft_body(*refs, has_mult):
    z_ref, h_ref, f1_ref, f2_ref, f2i_ref, g1_ref, bias_ref = refs[:7]
    m_ref = refs[7] if has_mult else None
    o_ref, a_ref, b_ref = refs[-3], refs[-2], refs[-1]
    n1c = h_ref.shape[0]
    n1h = n1c // 2
    n2c = FFT_N2
    n = n1h * n2c

    def slab(n2):
        return pl.ds(pl.multiple_of(n2 * 2 * n1c, 2 * n1c), 2 * n1c)

    def stage1(g, carry):
        n2s = [g * FFT_UNROLL + u for u in range(FFT_UNROLL)]
        xs = [jnp.concatenate([z_ref[pl.ds(n2, n1h, stride=n2c), :], z_ref[pl.ds(n + n2, n1h, stride=n2c), :]], 0)
              for n2 in n2s]
        res = [_dot(f1_ref[n2], x) for n2, x in zip(n2s, xs)]
        for n2, r in zip(n2s, res):
            a_ref[slab(n2), :] = r
        return carry

    lax.fori_loop(0, n2c // FFT_UNROLL, stage1, 0, unroll=2)
    g2 = FFT_UNROLL // 2

    def stage2(g, carry):
        k1s = [g * g2 + u for u in range(g2)]
        rows = [(pl.ds(k1, n2c, stride=2 * n1c), pl.ds(n1c + k1, n2c, stride=2 * n1c)) for k1 in k1s]
        blks = [jnp.concatenate([a_ref[re, :], a_ref[im, :]], 0) for re, im in rows]
        xs = [_dot(f2_ref[...], blk) for blk in blks]
        ys = []
        for k1, x in zip(k1s, xs):
            xr, xi = x[:n2c], x[n2c:]
            hr, hi_ = h_ref[k1, 0:n2c, :], h_ref[k1, n2c:2 * n2c, :]
            ys.append(jnp.concatenate([xr * hr - xi * hi_, xr * hi_ + xi * hr], 0))
        bs = [_dot(f2i_ref[...], y) for y in ys]
        for (re, im), b in zip(rows, bs):
            b_ref[re, :] = b[:n2c]
            b_ref[im, :] = b[n2c:]
        return carry

    lax.fori_loop(0, n1c // g2, stage2, 0, unroll=2)
    bias = bias_ref[...]

    def stage3(g, carry):
        n2s = [g * FFT_UNROLL + u for u in range(FFT_UNROLL)]
        blks = [b_ref[slab(n2), :] for n2 in n2s]
        ys = [_dot(g1_ref[n2], blk) for n2, blk in zip(n2s, blks)]
        outs = []
        for n2, y in zip(n2s, ys):
            for part, rows in ((y[:n1h], pl.ds(n2, n1h, stride=n2c)), (y[n1h:], pl.ds(n + n2, n1h, stride=n2c))):
                out = part + z_ref[rows, :] * bias
                if has_mult:
                    out = out * m_ref[rows, :]
                outs.append((rows, out))
        for rows, out in outs:
            o_ref[rows, :] = out
        return carry

    lax.fori_loop(0, n2c // FFT_UNROLL, stage3, 0, unroll=2)


def hyena_conv_fft(zsrc, zcol, n, n_batch, spec, bias3, layer, order, t_rows, mult=None):
    _, f1_cplx, g1, f2, f2i = _two_stage_dft_tables(n)
    n1c = 2 * n // FFT_N2
    cb = zcol // LANE
    wb = BRANCH_W // LANE
    const = lambda shape: pl.BlockSpec(shape, lambda j, p: (0,) * len(shape), pipeline_mode=pl.Buffered(1))
    in_specs = [pl.BlockSpec((2 * n, LANE), lambda j, p: (p, cb + j)),
                pl.BlockSpec((n1c, 2 * FFT_N2, LANE), lambda j, p: (0, 0, order * wb + j),
                             pipeline_mode=pl.Buffered(1)),
                const((FFT_N2, 2 * n1c, n1c)), const((2 * FFT_N2, 2 * FFT_N2)), const((2 * FFT_N2, 2 * FFT_N2)),
                const((FFT_N2, n1c, 2 * n1c)),
                pl.BlockSpec((None, 1, LANE), lambda j, p: (layer * HY_ORDER + order, 0, j))]
    args = [zsrc, spec, jnp.asarray(f1_cplx, BF16), jnp.asarray(f2, BF16), jnp.asarray(f2i, BF16),
            jnp.asarray(g1, BF16), bias3]
    if mult is not None:
        mb = mult[1] // LANE
        in_specs.append(pl.BlockSpec((2 * n, LANE), lambda j, p: (p, mb + j)))
        args.append(mult[0])
    return pl.pallas_call(
        functools.partial(_conv_fft_body, has_mult=mult is not None),
        grid=(wb, n_batch // 2),
        in_specs=in_specs,
        out_specs=pl.BlockSpec((2 * n, LANE), lambda j, p: (p, j)),
        out_shape=jax.ShapeDtypeStruct((t_rows, BRANCH_W), F32),
        scratch_shapes=[pltpu.VMEM((FFT_N2 * 2 * n1c, LANE), F32)] * 2,
        compiler_params=_cparams(("arbitrary", "arbitrary")),
        name="hyena_conv_fft",
    )(*args)


def _swap_pairs(x):
    w = x.shape[-1]
    lane = lax.broadcasted_iota(jnp.int32, x.shape, x.ndim - 1)
    return jnp.where(lane % 2 == 0, pltpu.roll(x, w - 1, x.ndim - 1), pltpu.roll(x, 1, x.ndim - 1))


def _attn_prep_body(g_ref, dq_ref, dk_ref, dv_ref, cg_ref, sg_ref, cd_ref, sd_ref, qn_ref, kn_ref,
                    qg_ref, kg_ref, vg_ref, qd_ref, kd_ref, vd_ref, *, lat_blocks):
    is_lat = pl.program_id(0) < lat_blocks
    cg = jnp.where(is_lat, cg_ref[...], 1.0)
    sg = jnp.where(is_lat, sg_ref[...], 0.0)
    cd = jnp.where(is_lat, cd_ref[...], 1.0)
    sd = jnp.where(is_lat, sd_ref[...], 0.0)

    def rope(x, cs, sn):
        return x * cs + _swap_pairs(x) * sn

    def rms(x, w):
        return x * lax.rsqrt(jnp.mean(x * x, axis=-1, keepdims=True) + EPS) * w

    for h in range(HEADS):
        sl = slice(h * HEAD_D, (h + 1) * HEAD_D)
        q = rope(rms(g_ref[:, sl].astype(F32), qn_ref[...]), cg, sg)
        qg_ref[:, sl] = (q * HEAD_D ** -0.5).astype(BF16)
        qd_ref[:, sl] = (rope(dq_ref[:, sl].astype(F32), cd, sd) * DIFF_QK ** -0.5).astype(BF16)
        kd_ref[:, sl] = rope(dk_ref[:, sl].astype(F32), cd, sd).astype(BF16)
    for h in range(GQA_KV):
        sl = slice(h * HEAD_D, (h + 1) * HEAD_D)
        kin = g_ref[:, BRANCH_W + h * HEAD_D:BRANCH_W + (h + 1) * HEAD_D].astype(F32)
        kg_ref[:, sl] = rope(rms(kin, kn_ref[...]), cg, sg).astype(BF16)
    vg_ref[...] = g_ref[:, BRANCH_W + GQA_KV * HEAD_D:BRANCH_W + 2 * GQA_KV * HEAD_D].astype(BF16)
    vd_ref[...] = dv_ref[...].astype(BF16)


def attn_prep(p, ropes, qn3, kn3, layer, n_lat, n_ctx, n_batch):
    t = p.shape[0]
    r = 256 if n_ctx % 256 == 0 else n_ctx
    nlb, ncb = n_lat // r, n_ctx // r
    lat_blocks = n_batch * nlb
    kvw = GQA_KV * HEAD_D
    w = BRANCH_W

    def kv_row(i):
        lat = (i // nlb) * (nlb + ncb) + ncb + i % nlb
        j = i - lat_blocks
        ctx = (j // ncb) * (nlb + ncb) + j % ncb
        return jnp.where(i < lat_blocks, lat, ctx)

    rope_spec = pl.BlockSpec((r, LANE), lambda i: (jnp.where(i < lat_blocks, i % nlb, 0), 0))
    nkv = n_batch * (n_lat + n_ctx)
    return pl.pallas_call(
        functools.partial(_attn_prep_body, lat_blocks=lat_blocks),
        grid=(t // r,),
        in_specs=[pl.BlockSpec((r, 2 * w), lambda i: (i, C_GQA_QKV // (2 * w))),
                  pl.BlockSpec((r, w), lambda i: (i, C_DIFF_Q // w)),
                  pl.BlockSpec((r, w), lambda i: (i, C_DIFF_K // w)),
                  pl.BlockSpec((r, w), lambda i: (i, C_DIFF_V // w)),
                  rope_spec, rope_spec, rope_spec, rope_spec,
                  pl.BlockSpec((None, 1, LANE), lambda i: (layer, 0, 0)),
                  pl.BlockSpec((None, 1, LANE), lambda i: (layer, 0, 0))],
        out_specs=[pl.BlockSpec((r, w), lambda i: (i, 0)),
                   pl.BlockSpec((r, kvw), lambda i: (kv_row(i), 0)),
                   pl.BlockSpec((r, kvw), lambda i: (kv_row(i), 0)),
                   pl.BlockSpec((r, w), lambda i: (i, 0)),
                   pl.BlockSpec((r, w), lambda i: (kv_row(i), 0)),
                   pl.BlockSpec((r, w), lambda i: (kv_row(i), 0))],
        out_shape=[jax.ShapeDtypeStruct((t, w), BF16), jax.ShapeDtypeStruct((nkv, kvw), BF16),
                   jax.ShapeDtypeStruct((nkv, kvw), BF16), jax.ShapeDtypeStruct((t, w), BF16),
                   jax.ShapeDtypeStruct((nkv, w), BF16), jax.ShapeDtypeStruct((nkv, w), BF16)],
        compiler_params=_cparams(("arbitrary",)),
        name="attn_prep",
    )(p, p, p, p, *ropes, qn3, kn3)


def _softmax_parts(s):
    e = jnp.exp(s - jnp.max(s, axis=-1, keepdims=True))
    return e, jnp.sum(e, axis=-1, keepdims=True)


def _gqa_body(q_ref, k_ref, v_ref, *rest):
    o_ref = rest[-1]
    group = HEADS // GQA_KV
    for kvh in range(GQA_KV):
        k = k_ref[:, kvh * HEAD_D:(kvh + 1) * HEAD_D]
        v = v_ref[:, kvh * HEAD_D:(kvh + 1) * HEAD_D]
        for g in range(group):
            sl = slice((kvh * group + g) * HEAD_D, (kvh * group + g + 1) * HEAD_D)
            s = lax.dot_general(q_ref[:, sl], k, (((1,), (1,)), ((), ())), preferred_element_type=F32)
            e, l = _softmax_parts(s)
            o_ref[:, sl] = jnp.dot(e.astype(BF16), v, preferred_element_type=F32) / l


def _diff_body(q_ref, k_ref, v_ref, lam_ref, *rest, lam_init):
    o_ref = rest[-1]
    lam4 = lam_ref[...]
    lam = (jnp.exp(jnp.sum(lam4[0:1] * lam4[1:2], axis=-1, keepdims=True))
           - jnp.exp(jnp.sum(lam4[2:3] * lam4[3:4], axis=-1, keepdims=True)) + lam_init)
    dn = (((1,), (1,)), ((), ()))
    for h in range(HEADS):
        sl = slice(h * HEAD_D, (h + 1) * HEAD_D)
        q = q_ref[:, sl]
        k = k_ref[:, sl]
        v = v_ref[:, sl]
        first = lax.broadcasted_iota(jnp.int32, q.shape, 1) < DIFF_QK
        zero = jnp.zeros_like(q)
        e1, l1 = _softmax_parts(lax.dot_general(jnp.where(first, q, zero), k, dn, preferred_element_type=F32))
        e2, l2 = _softmax_parts(lax.dot_general(jnp.where(first, zero, q), k, dn, preferred_element_type=F32))
        o1 = jnp.dot(e1.astype(BF16), v, preferred_element_type=F32) / l1
        o2 = jnp.dot(e2.astype(BF16), v, preferred_element_type=F32) / l2
        o_ref[:, sl] = o1 - lam * o2


def attention(body, q, k, v, extra, extra_specs, q_row0, nq, kv_per_batch, kv_len, n_batch, tq, name, prev=None):
    t, w = q.shape
    qb0 = q_row0 // tq
    nqb = nq // tq
    kvb = kv_per_batch // kv_len
    in_specs = [pl.BlockSpec((tq, w), lambda b, i: (qb0 + b * nqb + i, 0)),
                pl.BlockSpec((kv_len, k.shape[1]), lambda b, i: (b * kvb, 0)),
                pl.BlockSpec((kv_len, v.shape[1]), lambda b, i: (b * kvb, 0))] + extra_specs
    args = [q, k, v, *extra]
    aliases = {}
    if prev is not None:
        in_specs.append(pl.BlockSpec(memory_space=pl.ANY))
        args.append(prev)
        aliases = {len(args) - 1: 0}
    return pl.pallas_call(
        body,
        grid=(n_batch, nqb),
        in_specs=in_specs,
        out_specs=pl.BlockSpec((tq, w), lambda b, i: (qb0 + b * nqb + i, 0)),
        out_shape=jax.ShapeDtypeStruct((t, w), F32),
        input_output_aliases=aliases,
        compiler_params=_cparams(("arbitrary", "arbitrary")),
        name=name,
    )(*args)


def _merge_body(h_ref, mod_ref, mg_ref, of_ref, ob_ref, ggate_ref, y1_ref, x2_ref, hgate_ref, oc_ref, cgate_ref,
                od_ref, dgate_ref, gnorm_ref, dnorm_ref, wbr_ref, wout_ref, lng_ref, lnb_ref, o_ref, *, diff_scale):
    def rms_heads(x, w):
        parts = []
        for h in range(HEADS):
            xh = x[:, h * HEAD_D:(h + 1) * HEAD_D]
            parts.append(xh * lax.rsqrt(jnp.mean(xh * xh, axis=-1, keepdims=True) + EPS) * w)
        return jnp.concatenate(parts, -1)

    ys = (rms_heads(of_ref[...] + ob_ref[...], gnorm_ref[...]) * _silu(ggate_ref[...].astype(F32)),
          x2_ref[...] * y1_ref[...] * _silu(hgate_ref[...].astype(F32)),
          oc_ref[...] * _silu(cgate_ref[...].astype(F32)),
          rms_heads(od_ref[...], dnorm_ref[...]) * diff_scale * _silu(dgate_ref[...].astype(F32)))
    acc = None
    for n in range(N_BRANCH):
        proj = jnp.dot(ys[n].astype(BF16), wbr_ref[n], preferred_element_type=F32)
        term = _sigmoid(mg_ref[:, n * D_MODEL:(n + 1) * D_MODEL].astype(F32)) * proj
        acc = term if acc is None else acc + term
    out = jnp.dot(acc.astype(BF16), wout_ref[...], preferred_element_type=F32)
    x = ALPHA * h_ref[...] + mod_ref[2:3, :] * out
    mu = jnp.mean(x, axis=-1, keepdims=True)
    xc = x - mu
    var = jnp.mean(xc * xc, axis=-1, keepdims=True)
    o_ref[...] = xc * lax.rsqrt(var + EPS) * lng_ref[...] + lnb_ref[...]


def merge_postnorm(h_all, mod3, p, o_f, o_b, y1, xv, oc, od, gnorm3, dnorm3, wbr, wout, lng3, lnb3, layer, lam_init,
                   n_lat, n_batch):
    t, d = h_all.shape
    r = 256 if n_lat % 256 == 0 else 64
    w = BRANCH_W
    lbb = n_lat // r
    row = lambda i: jnp.minimum(i // lbb, n_batch)
    tok = lambda cb: pl.BlockSpec((r, w), lambda i: (i, cb))
    vec = lambda width: pl.BlockSpec((None, 1, width), lambda i: (layer, 0, 0))
    return pl.pallas_call(
        functools.partial(_merge_body, diff_scale=1.0 - lam_init),
        grid=(t // r,),
        in_specs=[pl.BlockSpec((r, d), lambda i: (i, 0)),
                  pl.BlockSpec((None, 3, d), lambda i: (row(i), 0, 0)),
                  pl.BlockSpec((r, N_BRANCH * d), lambda i: (i, C_MERGE // (N_BRANCH * d))),
                  tok(0), tok(0), tok(C_GDN_GATE // w), tok(0), tok(1), tok(C_HY_GATE // w), tok(0),
                  tok(C_GQA_GATE // w), tok(0), tok(C_DIFF_GATE // w),
                  vec(LANE), vec(LANE),
                  pl.BlockSpec((None, N_BRANCH, w, d), lambda i: (layer, 0, 0, 0)),
                  pl.BlockSpec((None, d, d), lambda i: (layer, 0, 0)),
                  vec(d), vec(d)],
        out_specs=pl.BlockSpec((r, d), lambda i: (i, 0)),
        out_shape=jax.ShapeDtypeStruct((t, d), F32),
        compiler_params=_cparams(("arbitrary",)),
        name="merge_postnorm",
    )(h_all, mod3, p, o_f, o_b, p, y1, xv, p, oc, p, od, p, gnorm3, dnorm3, wbr, wout, lng3, lnb3)


def _rope_tables(n_lat, dim):
    rows = n_lat // GRID_W
    row = jnp.repeat(jnp.arange(rows, dtype=F32), GRID_W)
    col = jnp.tile(jnp.arange(GRID_W, dtype=F32), rows)
    half = dim // 2
    inv = ROPE_THETA ** (-jnp.arange(0, half, 2, dtype=F32) / half)
    ang = jnp.concatenate([row[:, None] * inv, col[:, None] * inv], -1)
    cos = jnp.repeat(jnp.cos(ang), 2, axis=-1)
    sin = jnp.repeat(jnp.sin(ang), 2, axis=-1)
    sign = jnp.tile(jnp.array([-1.0, 1.0], F32), dim // 2)
    reps = LANE // dim
    return jnp.tile(cos, (1, reps)), jnp.tile(sin * sign, (1, reps))


def kernel(x, c, ctx, c_ctx, w_ada, b_ada, w_in, gdn_conv, gdn_a_log, gdn_dt_bias, gdn_norm, hy_conv, hy_w1, hy_b1,
           hy_w2, hy_b2, hy_w3, hy_b3, hy_w4, hy_freq, hy_bias, gqa_qn, gqa_kn, diff_lam, diff_norm, w_br, w_out,
           ln_g, ln_b):
    nb, n_lat, d = x.shape
    n_ctx = ctx.shape[1]
    t_lat, t_ctx = nb * n_lat, nb * n_ctx
    depth = w_in.shape[0]
    w = BRANCH_W

    w_main = jnp.concatenate([w_in[:, :, O_MERGE:], w_in[:, :, :O_GDN_AB], w_in[:, :, O_GDN_AB + 4 * HEADS:O_MERGE]],
                             axis=2).astype(BF16)
    w_ab = jnp.pad(w_in[:, :, O_GDN_AB:O_GDN_AB + 4 * HEADS], ((0, 0), (0, 0), (0, LANE - 4 * HEADS)))
    wbr_bf = w_br.astype(BF16)
    wout_bf = w_out.astype(BF16)
    b_ada3 = b_ada[:, None, :]
    cvec = jnp.concatenate([c, c_ctx[None, :], jnp.zeros((SUB - nb - 1, d), F32)], 0)
    as3 = lambda a: a[:, None, :]
    gdn_par_r = jnp.pad(jnp.stack([gdn_a_log.reshape(depth, -1), gdn_dt_bias.reshape(depth, -1)], 1),
                        ((0, 0), (0, SUB - 2), (0, LANE - 2 * HEADS)))
    gdn_par_c = jnp.pad(jnp.stack([gdn_a_log.reshape(depth, -1), gdn_dt_bias.reshape(depth, -1)], 2),
                        ((0, 0), (0, 2 * HEADS), (0, LANE - 2)))
    hy_w1p = jnp.pad(hy_w1, ((0, 0), (0, LANE - HY_EMB), (0, 0)))
    hy_bias3 = hy_bias.reshape(depth * HY_ORDER, 1, w)
    ropes = _rope_tables(n_lat, HEAD_D) + _rope_tables(n_lat, DIFF_QK)

    tm = 1024 if (n_lat % 1024 == 0 and t_ctx % 1024 == 0) else n_ctx
    h_all = jnp.concatenate([x.reshape(t_lat, d), ctx.reshape(t_ctx, d)], 0)
    for l in range(depth):
        lam_init = 0.8 - 0.6 * math.exp(-0.3 * l)
        mod3 = ada_mod(cvec, w_ada, b_ada3, l).reshape(SUB, 3, d)
        p, ab = in_proj(h_all, mod3, w_main, w_ab, l, tm, n_lat // tm, nb)

        qkv = dwconv(p, gdn_conv, l, C_GDN_QKV, 3 * w, n_lat, n_ctx, nb, act=True)
        ab_rows = jnp.transpose(ab[:, :4 * HEADS].reshape(-1, GDN_CHUNK, 4 * HEADS), (0, 2, 1))
        o_f, o_b = gdn_scan(qkv, ab, ab_rows, gdn_par_r[l], gdn_par_c[l], n_lat, n_ctx, nb)

        xv = dwconv(p, hy_conv, l, C_HY_XV, 3 * w, n_lat, n_ctx, nb, act=False)
        filt = lambda n: hyena_filter(n, hy_w1p, as3(hy_b1), hy_w2, as3(hy_b2), hy_w3, as3(hy_b3), hy_w4,
                                      as3(hy_freq), l)
        spec_lat = hyena_spec_fft(filt(n_lat), n_lat)
        spec_ctx = hyena_spec_dense(filt(n_ctx), n_ctx)
        z1 = hyena_conv_fft(xv, 2 * w, n_lat, nb, spec_lat, hy_bias3, l, 0, t_lat + t_ctx, mult=(xv, 0))
        z1 = hyena_conv_dense(xv, 2 * w, t_lat, n_ctx, nb, spec_ctx, hy_bias3, l, 0, z1, mult=(xv, 0))
        y1 = hyena_conv_fft(z1, 0, n_lat, nb, spec_lat, hy_bias3, l, 1, t_lat + t_ctx)
        y1 = hyena_conv_dense(z1, 0, t_lat, n_ctx, nb, spec_ctx, hy_bias3, l, 1, y1)

        qg, kg, vg, qd, kd, vd = attn_prep(p, ropes, as3(gqa_qn), as3(gqa_kn), l, n_lat, n_ctx, nb)
        kv_all = n_lat + n_ctx
        tq = min(256, n_ctx)
        lam_spec = [pl.BlockSpec((None, 4, DIFF_QK), lambda b, i: (l, 0, 0))]
        diff_body = functools.partial(_diff_body, lam_init=lam_init)
        oc = attention(_gqa_body, qg, kg, vg, (), [], 0, n_lat, kv_all, kv_all, nb, tq, "gqa_lat")
        oc = attention(_gqa_body, qg, kg, vg, (), [], t_lat, n_ctx, kv_all, n_ctx, nb, tq, "gqa_ctx", prev=oc)
        od = attention(diff_body, qd, kd, vd, (diff_lam,), lam_spec, 0, n_lat, kv_all, kv_all, nb, tq, "diff_lat")
        od = attention(diff_body, qd, kd, vd, (diff_lam,), lam_spec, t_lat, n_ctx, kv_all, n_ctx, nb, tq, "diff_ctx",
                       prev=od)

        h_all = merge_postnorm(h_all, mod3, p, o_f, o_b, y1, xv, oc, od, as3(gdn_norm), as3(diff_norm), wbr_bf, wout_bf,
                               as3(ln_g), as3(ln_b), l, lam_init, n_lat, nb)
    return h_all[:t_lat].reshape(nb, n_lat, d)
```

```python
import functools
import math

import numpy as np
import jax
import jax.numpy as jnp
from jax import lax
from jax.experimental import pallas as pl
from jax.experimental.pallas import tpu as pltpu

F32 = jnp.float32
BF16 = jnp.bfloat16
HI = lax.Precision.HIGHEST

D_MODEL = 1024
DEPTH = 4
GRID_W = 64
BRANCH_W = D_MODEL // 2
N_BRANCH = 4
HEADS = 4
HEAD_D = BRANCH_W // HEADS
GDN_CONV = 4
GDN_CHUNK = 64
GDN_SUB = 4
HY_CONV = 3
HY_EMB = 33
HY_BANDS = (HY_EMB - 1) // 2
HY_FH = 64
HY_ORDER = 2
HY_MIN_DECAY = math.log(1e-2) / 1.5
HY_MAX_DECAY = math.log(1e-2) / 0.3
GQA_KV = 2
DIFF_QK = HEAD_D // 2
ROPE_THETA = 10000.0
EPS = 1e-6
ALPHA = (2.0 * DEPTH) ** 0.25

LANE = 128
SUB = 8
HALO = 16
FFT_N2 = 128
FFT_UNROLL = 8
VMEM_LIMIT = 60 * 1024 * 1024

C_MERGE = 0
C_GDN_QKV = 4096
C_GDN_GATE = 5632
C_HY_XV = 6144
C_HY_GATE = 7680
C_GQA_QKV = 8192
C_GQA_GATE = 9216
C_DIFF_Q = 9728
C_DIFF_K = 10240
C_DIFF_V = 10752
C_DIFF_GATE = 11264
N_MAIN = 11776
O_GDN_AB = 1536
O_MERGE = 7696


def _cparams(sem):
    return pltpu.CompilerParams(dimension_semantics=sem, vmem_limit_bytes=VMEM_LIMIT)


def _dot(a, b, hi=False):
    if hi:
        return jnp.dot(a, b, precision=HI, preferred_element_type=F32)
    return jnp.dot(a.astype(BF16), b.astype(BF16), preferred_element_type=F32)


def _dot_nt(a, b, hi=False):
    dn = (((1,), (1,)), ((), ()))
    if hi:
        return lax.dot_general(a, b, dn, precision=HI, preferred_element_type=F32)
    return lax.dot_general(a.astype(BF16), b.astype(BF16), dn, preferred_element_type=F32)


def _dot_tn(a, b):
    return lax.dot_general(a.astype(BF16), b.astype(BF16), (((0,), (0,)), ((), ())), preferred_element_type=F32)


def _sigmoid(x):
    return 1.0 / (1.0 + jnp.exp(-x))


def _silu(x):
    return x * _sigmoid(x)


def _softplus(x):
    return jnp.maximum(x, 0.0) + jnp.log1p(jnp.exp(-jnp.abs(x)))


def _ada_body(c_ref, w_ref, b_ref, o_ref):
    o_ref[...] = _dot(_silu(c_ref[...]), w_ref[...], hi=True) + b_ref[...]


def ada_mod(cvec, w_ada, b_ada3, layer):
    d = cvec.shape[1]
    tn = 512
    return pl.pallas_call(
        _ada_body,
        grid=(3 * d // tn,),
        in_specs=[pl.BlockSpec((SUB, d), lambda j: (0, 0)),
                  pl.BlockSpec((None, d, tn), lambda j: (layer, 0, j)),
                  pl.BlockSpec((None, 1, tn), lambda j: (layer, 0, j))],
        out_specs=pl.BlockSpec((SUB, tn), lambda j: (0, j)),
        out_shape=jax.ShapeDtypeStruct((SUB, 3 * d), F32),
        compiler_params=_cparams(("arbitrary",)),
        name="ada_mod",
    )(cvec, w_ada, b_ada3)


def _inproj_body(h_ref, mod_ref, w_ref, wab_ref, o_ref, ab_ref, u_ref):
    @pl.when(pl.program_id(1) == 0)
    def _():
        x = h_ref[...]
        mu = jnp.mean(x, axis=-1, keepdims=True)
        xc = x - mu
        var = jnp.mean(xc * xc, axis=-1, keepdims=True)
        u = xc * lax.rsqrt(var + EPS) * (1.0 + mod_ref[1:2, :]) + mod_ref[0:1, :]
        u_ref[...] = u.astype(BF16)
        ab_ref[...] = _dot(u, wab_ref[...], hi=True)

    o_ref[...] = jnp.dot(u_ref[...], w_ref[...], preferred_element_type=F32).astype(BF16)


def in_proj(h_all, mod3, w_main, w_ab, layer, tm, lat_blocks_per_batch, n_batch):
    t, d = h_all.shape
    n_main = w_main.shape[2]
    tn = n_main // 4
    row = lambda i: jnp.minimum(i // lat_blocks_per_batch, n_batch)
    return pl.pallas_call(
        _inproj_body,
        grid=(t // tm, n_main // tn),
        in_specs=[pl.BlockSpec((tm, d), lambda i, j: (i, 0)),
                  pl.BlockSpec((None, 3, d), lambda i, j: (row(i), 0, 0)),
                  pl.BlockSpec((None, d, tn), lambda i, j: (layer, 0, j)),
                  pl.BlockSpec((None, d, LANE), lambda i, j: (layer, 0, 0))],
        out_specs=[pl.BlockSpec((tm, tn), lambda i, j: (i, j)),
                   pl.BlockSpec((tm, LANE), lambda i, j: (i, 0))],
        out_shape=[jax.ShapeDtypeStruct((t, n_main), BF16), jax.ShapeDtypeStruct((t, LANE), F32)],
        scratch_shapes=[pltpu.VMEM((tm, d), BF16)],
        compiler_params=_cparams(("arbitrary", "arbitrary")),
        name="in_proj",
    )(h_all, mod3, w_main, w_ab)


def _dwconv_body(xp_ref, x_ref, xn_ref, w_ref, o_ref, pad_ref, *, taps, pad_l, t_lat, n_lat, n_ctx, sb, act):
    i = pl.program_id(0)
    r = x_ref.shape[0]
    pad_ref[0:HALO, :] = xp_ref[...].astype(F32)
    pad_ref[HALO:HALO + r, :] = x_ref[...].astype(F32)
    pad_ref[HALO + r:2 * HALO + r, :] = xn_ref[...].astype(F32)
    row = lax.broadcasted_iota(jnp.int32, (sb, 1), 0)
    for k in range(r // sb):
        g0 = i * r + k * sb
        in_lat = g0 < t_lat
        starts = jnp.where(in_lat, g0 % n_lat == 0, (g0 - t_lat) % n_ctx == 0)
        ends = jnp.where(in_lat, (g0 + sb) % n_lat == 0, (g0 + sb - t_lat) % n_ctx == 0)
        acc = None
        for j in range(taps):
            d = j - pad_l
            off = HALO + k * sb + d
            xs = pad_ref[off:off + sb, :]
            if d < 0:
                xs = jnp.where(jnp.logical_and(starts, row < -d), 0.0, xs)
            elif d > 0:
                xs = jnp.where(jnp.logical_and(ends, row >= sb - d), 0.0, xs)
            term = w_ref[j:j + 1, :] * xs
            acc = term if acc is None else acc + term
        if act:
            acc = _silu(acc)
        o_ref[k * sb:(k + 1) * sb, :] = acc


def dwconv(p, w_conv, layer, col0, width, n_lat, n_ctx, n_batch, act):
    t = p.shape[0]
    taps = w_conv.shape[1]
    sb = min(256, n_ctx)
    r = 1024 if t % 1024 == 0 else sb
    lw = 512
    cb = col0 // lw
    rs = r // HALO
    body = functools.partial(_dwconv_body, taps=taps, pad_l=(taps - 1) // 2, t_lat=n_batch * n_lat, n_lat=n_lat,
                             n_ctx=n_ctx, sb=sb, act=act)
    return pl.pallas_call(
        body,
        grid=(t // r, width // lw),
        in_specs=[pl.BlockSpec((HALO, lw), lambda i, j: (jnp.maximum(i * rs - 1, 0), cb + j)),
                  pl.BlockSpec((r, lw), lambda i, j: (i, cb + j)),
                  pl.BlockSpec((HALO, lw), lambda i, j: (jnp.minimum((i + 1) * rs, t // HALO - 1), cb + j)),
                  pl.BlockSpec((None, taps, lw), lambda i, j: (layer, 0, j))],
        out_specs=pl.BlockSpec((r, lw), lambda i, j: (i, j)),
        out_shape=jax.ShapeDtypeStruct((t, width), F32),
        scratch_shapes=[pltpu.VMEM((r + 2 * HALO, lw), F32)],
        compiler_params=_cparams(("arbitrary", "arbitrary")),
        name="dwconv",
    )(p, p, p, w_conv)


def _gdn_body(qf_ref, qb_ref, abcf_ref, abcb_ref, abrf_ref, abrb_ref, pr_ref, pc_ref, of_ref, ob_ref, s_ref):
    c = GDN_CHUNK

    @pl.when(pl.program_id(1) == 0)
    def _():
        s_ref[...] = jnp.zeros_like(s_ref)

    ii = lax.broadcasted_iota(jnp.int32, (c, c), 0)
    jj = lax.broadcasted_iota(jnp.int32, (c, c), 1)
    lmat = (jj <= ii).astype(F32)
    eye = (jj == ii).astype(F32)
    alr, dtr = pr_ref[0:1, :], pr_ref[1:2, :]
    alc, dtc = pc_ref[:, 0:1], pc_ref[:, 1:2]
    chains = []
    for d in range(2):
        qkv_ref = (qf_ref, qb_ref)[d]
        abc_ref = (abcf_ref, abcb_ref)[d]
        abr_ref = (abrf_ref, abrb_ref)[d]
        incl = (jj <= ii) if d == 0 else (jj >= ii)
        strict = (jj < ii) if d == 0 else (jj > ii)
        for j in range(GDN_SUB):
            rows = slice(j * c, (j + 1) * c)
            abc = abc_ref[rows, :]
            abr = abr_ref[j]
            g_c = -jnp.exp(alr) * _softplus(abc + dtr)
            g_r = -jnp.exp(alc) * _softplus(abr + dtc)
            cum_c = _dot(lmat, g_c, hi=True)
            cum_r = _dot_nt(g_r, lmat, hi=True)
            if d == 1:
                cum_c = cum_c[c - 1:c, :] - cum_c + g_c
                cum_r = cum_r[:, c - 1:c] - cum_r + g_r
            beta_all = _sigmoid(abc)
            for h in range(HEADS):
                idx = HEADS * d + h
                q = qkv_ref[rows, h * HEAD_D:(h + 1) * HEAD_D]
                k = qkv_ref[rows, BRANCH_W + h * HEAD_D:BRANCH_W + (h + 1) * HEAD_D]
                v = qkv_ref[rows, 2 * BRANCH_W + h * HEAD_D:2 * BRANCH_W + (h + 1) * HEAD_D]
                q = q * lax.rsqrt(jnp.sum(q * q, axis=-1, keepdims=True) + EPS) * (HEAD_D ** -0.5)
                k = k * lax.rsqrt(jnp.sum(k * k, axis=-1, keepdims=True) + EPS)
                cc = cum_c[:, idx:idx + 1]
                cr = cum_r[idx:idx + 1, :]
                dec = jnp.exp(jnp.where(incl, cc - cr, -1e30))
                beta = beta_all[:, 2 * HEADS + idx:2 * HEADS + idx + 1]
                ecum = jnp.exp(cc)
                tot = cc[c - 1:c, :] if d == 0 else cc[0:1, :]
                chains.append(dict(d=d, h=h, j=j, rows=rows, q=q, k=k, dec=dec, strict=strict, beta=beta, ecum=ecum,
                                   tot=tot, rhs=jnp.concatenate([k * (beta * ecum), v * beta], 1),
                                   k_tail=k * jnp.exp(tot - cc)))
    for ch in chains:
        ch["kk"] = _dot_nt(ch["k"], ch["k"])
        ch["qk"] = _dot_nt(ch["q"], ch["k"])
    for ch in chains:
        ch["p"] = -jnp.where(ch["strict"], ch["beta"] * ch["kk"] * ch["dec"], 0.0)
        ch["inv"] = eye + ch["p"]
    for _ in range(int(math.log2(c)) - 1):
        for ch in chains:
            ch["p"] = _dot(ch["p"], ch["p"])
        for ch in chains:
            ch["inv"] = ch["inv"] + _dot(ch["inv"], ch["p"])
    for ch in chains:
        ch["wu"] = _dot(ch["inv"], ch["rhs"])
        ch["lhs"] = jnp.concatenate([ch["wu"][:, :HEAD_D], ch["q"] * ch["ecum"]], 0)
    state = {(d, h): s_ref[d, h] for d in range(2) for h in range(HEADS)}
    for step in range(GDN_SUB):
        cur = [ch for ch in chains if ch["j"] == (step if ch["d"] == 0 else GDN_SUB - 1 - step)]
        for ch in cur:
            ch["ws"] = _dot(ch["lhs"], state[ch["d"], ch["h"]])
        for ch in cur:
            ch["v_new"] = ch["wu"][:, HEAD_D:] - ch["ws"][:c]
        for ch in cur:
            out_ref = (of_ref, ob_ref)[ch["d"]]
            h = ch["h"]
            out_ref[ch["rows"], h * HEAD_D:(h + 1) * HEAD_D] = ch["ws"][c:] + _dot(ch["qk"] * ch["dec"], ch["v_new"])
            state[ch["d"], h] = state[ch["d"], h] * jnp.exp(ch["tot"]) + _dot_tn(ch["k_tail"], ch["v_new"])
    for (d, h), val in state.items():
        s_ref[d, h] = val


def gdn_scan(qkv, ab, ab_rows, par_r, par_c, n_lat, n_ctx, n_batch):
    t = qkv.shape[0]
    c = GDN_SUB * GDN_CHUNK
    nlc, ncc = n_lat // c, n_ctx // c
    base = n_batch * nlc

    def fwd(b, s):
        return jnp.where(s < ncc, base + b * ncc + s, b * nlc + (s - ncc))

    def bwd(b, s):
        return jnp.where(s < ncc, base + b * ncc + (ncc - 1 - s), b * nlc + (nlc - 1 - (s - ncc)))

    w3 = 3 * BRANCH_W
    return pl.pallas_call(
        _gdn_body,
        grid=(n_batch, ncc + nlc),
        in_specs=[pl.BlockSpec((c, w3), lambda b, s: (fwd(b, s), 0)),
                  pl.BlockSpec((c, w3), lambda b, s: (bwd(b, s), 0)),
                  pl.BlockSpec((c, LANE), lambda b, s: (fwd(b, s), 0)),
                  pl.BlockSpec((c, LANE), lambda b, s: (bwd(b, s), 0)),
                  pl.BlockSpec((GDN_SUB, 4 * HEADS, GDN_CHUNK), lambda b, s: (fwd(b, s), 0, 0)),
                  pl.BlockSpec((GDN_SUB, 4 * HEADS, GDN_CHUNK), lambda b, s: (bwd(b, s), 0, 0)),
                  pl.BlockSpec((SUB, LANE), lambda b, s: (0, 0)),
                  pl.BlockSpec((4 * HEADS, LANE), lambda b, s: (0, 0))],
        out_specs=[pl.BlockSpec((c, BRANCH_W), lambda b, s: (fwd(b, s), 0)),
                   pl.BlockSpec((c, BRANCH_W), lambda b, s: (bwd(b, s), 0))],
        out_shape=[jax.ShapeDtypeStruct((t, BRANCH_W), F32), jax.ShapeDtypeStruct((t, BRANCH_W), F32)],
        scratch_shapes=[pltpu.VMEM((2, HEADS, HEAD_D, HEAD_D), F32)],
        compiler_params=_cparams(("arbitrary", "arbitrary")),
        name="gdn_scan",
    )(qkv, qkv, ab, ab, ab_rows, ab_rows, par_r, par_c)


def _hyfilt_body(z_ref, aux_ref, w1_ref, b1_ref, w2_ref, b2_ref, w3_ref, b3_ref, w4_ref, fr_ref, dl_ref, o_ref):
    fr = fr_ref[...]
    h = jnp.sin(fr * (_dot(z_ref[...], w1_ref[...], hi=True) + b1_ref[...]))
    h = jnp.sin(fr * (_dot(h, w2_ref[...], hi=True) + b2_ref[...]))
    h = jnp.sin(fr * (_dot(h, w3_ref[...], hi=True) + b3_ref[...]))
    taps = _dot(h, w4_ref[...], hi=True) * jnp.exp(-aux_ref[:, 0:1] * dl_ref[...])
    w = BRANCH_W
    negative = aux_ref[:, 1:2] > 0.5
    keep = aux_ref[:, 2:3]
    for o in range(HY_ORDER):
        fwd = taps[:, o * 2 * w:o * 2 * w + w]
        bwd = taps[:, o * 2 * w + w:(o + 1) * 2 * w]
        o_ref[:, o * w:(o + 1) * w] = jnp.where(negative, bwd, fwd) * keep


def hyena_filter(n, w1p, b1, w2, b2, w3, b3, w4, fr, layer):
    row = jnp.arange(2 * n)
    src = jnp.where(row <= n, row, 2 * n - row)
    pos = jnp.where(row == n, 0, src).astype(F32)
    tt = pos / max(n - 1, 1)
    ang = (2.0 * math.pi / n) * pos[:, None] * jnp.linspace(1e-4, HY_BANDS - 1, HY_BANDS, dtype=F32)
    z = jnp.concatenate([tt[:, None], jnp.cos(ang), -jnp.sin(ang), jnp.zeros((2 * n, LANE - HY_EMB), F32)], -1)
    aux = jnp.stack([tt, (row > n).astype(F32), (row != n).astype(F32)], 1)
    aux = jnp.pad(aux, ((0, 0), (0, SUB - 3)))
    deltas = jnp.abs(jnp.linspace(HY_MIN_DECAY, HY_MAX_DECAY, BRANCH_W, dtype=F32))
    dl = jnp.tile(deltas, 2 * HY_ORDER)[None, :]
    r = 512
    wo = 2 * HY_ORDER * BRANCH_W
    full = lambda shape: pl.BlockSpec((None,) + shape, lambda i: (layer,) + (0,) * len(shape))
    return pl.pallas_call(
        _hyfilt_body,
        grid=(2 * n // r,),
        in_specs=[pl.BlockSpec((r, LANE), lambda i: (i, 0)),
                  pl.BlockSpec((r, SUB), lambda i: (i, 0)),
                  full((LANE, HY_FH)), full((1, HY_FH)), full((HY_FH, HY_FH)), full((1, HY_FH)),
                  full((HY_FH, HY_FH)), full((1, HY_FH)), full((HY_FH, wo)), full((1, HY_FH)),
                  pl.BlockSpec((1, wo), lambda i: (0, 0))],
        out_specs=pl.BlockSpec((r, HY_ORDER * BRANCH_W), lambda i: (i, 0)),
        out_shape=jax.ShapeDtypeStruct((2 * n, HY_ORDER * BRANCH_W), F32),
        compiler_params=_cparams(("arbitrary",)),
        name="hyena_filter",
    )(z, aux, w1p, b1, w2, b2, w3, b3, w4, fr, dl)


@functools.lru_cache(maxsize=None)
def _dense_dft_tables(n):
    nn = 2 * n
    k = np.arange(nn)[:, None].astype(np.float64)
    m = np.arange(nn)[None, :].astype(np.float64)
    ang = -2.0 * np.pi * k * m / nn
    wr, wi = np.cos(ang), np.sin(ang)
    f_real = np.concatenate([wr, wi], 0)
    wr_h, wi_h = wr[:, :n], wi[:, :n]
    f_fwd = np.block([[wr_h, -wi_h], [wi_h, wr_h]])
    cr, ci = wr.T[:n] / nn, -wi.T[:n] / nn
    f_inv = np.block([[cr, -ci], [ci, cr]])
    return (np.asarray(f_real, np.float32), np.asarray(f_fwd, np.float32), np.asarray(f_inv, np.float32))


@functools.lru_cache(maxsize=None)
def _two_stage_dft_tables(n):
    nn = 2 * n
    n2c = FFT_N2
    n1c = nn // n2c
    n1h = n1c // 2
    k1 = np.arange(n1c).astype(np.float64)
    n1 = np.arange(n1c).astype(np.float64)
    n2 = np.arange(n2c).astype(np.float64)
    ang = -2.0 * np.pi * (k1[None, :, None] * n1[None, None, :] / n1c + n2[:, None, None] * k1[None, :, None] / nn)
    mr, mi = np.cos(ang), np.sin(ang)
    f1_real = np.concatenate([mr, mi], 1)
    mrh, mih = mr[:, :, :n1h], mi[:, :, :n1h]
    f1_cplx = np.concatenate([np.concatenate([mrh, -mih], 2), np.concatenate([mih, mrh], 2)], 1)
    gr = np.transpose(mr, (0, 2, 1))[:, :n1h, :] / nn
    gi = -np.transpose(mi, (0, 2, 1))[:, :n1h, :] / nn
    g1 = np.concatenate([np.concatenate([gr, -gi], 2), np.concatenate([gi, gr], 2)], 1)
    k2 = np.arange(n2c).astype(np.float64)
    a2 = -2.0 * np.pi * k2[:, None] * n2[None, :] / n2c
    fr, fi = np.cos(a2), np.sin(a2)
    f2 = np.block([[fr, -fi], [fi, fr]])
    f2i = np.block([[fr.T, fi.T], [-fi.T, fr.T]])
    f32 = lambda a: np.asarray(a, np.float32)
    return f32(f1_real), f32(f1_cplx), f32(g1), f32(f2), f32(f2i)


def _spec_dense_body(f_ref, x_ref, o_ref):
    o_ref[...] = _dot(f_ref[...], x_ref[...], hi=True)


def hyena_spec_dense(full, n):
    f_real, _, _ = _dense_dft_tables(n)
    nn, cols = full.shape
    return pl.pallas_call(
        _spec_dense_body,
        grid=(cols // LANE,),
        in_specs=[pl.BlockSpec((2 * nn, nn), lambda j: (0, 0)),
                  pl.BlockSpec((nn, LANE), lambda j: (0, j))],
        out_specs=pl.BlockSpec((2 * nn, LANE), lambda j: (0, j)),
        out_shape=jax.ShapeDtypeStruct((2 * nn, cols), F32),
        compiler_params=_cparams(("arbitrary",)),
        name="hyena_spec_dense",
    )(jnp.asarray(f_real), full)


def _conv_dense_body(*refs, has_mult):
    z_ref, h_ref, ff_ref, fi_ref, bias_ref = refs[:5]
    m_ref = refs[5] if has_mult else None
    o_ref = refs[-1]
    z = z_ref[...]
    nn = z.shape[0]
    x = _dot(ff_ref[...], z, hi=True)
    xr, xi = x[:nn], x[nn:]
    hr, hi_ = h_ref[0:nn, :], h_ref[nn:2 * nn, :]
    y = _dot(fi_ref[...], jnp.concatenate([xr * hr - xi * hi_, xr * hi_ + xi * hr], 0), hi=True)
    out = y + z * bias_ref[...]
    if has_mult:
        out = out * m_ref[...]
    o_ref[...] = out


def hyena_conv_dense(zsrc, zcol, row0, n, n_batch, spec, bias3, layer, order, prev, mult=None):
    _, f_fwd, f_inv = _dense_dft_tables(n)
    nn = 2 * n
    rb, cb = row0 // nn, zcol // LANE
    wb = BRANCH_W // LANE
    in_specs = [pl.BlockSpec((nn, LANE), lambda p, j: (rb + p, cb + j)),
                pl.BlockSpec((2 * nn, LANE), lambda p, j: (0, order * wb + j)),
                pl.BlockSpec((2 * nn, nn), lambda p, j: (0, 0)),
                pl.BlockSpec((nn, 2 * nn), lambda p, j: (0, 0)),
                pl.BlockSpec((None, 1, LANE), lambda p, j: (layer * HY_ORDER + order, 0, j))]
    args = [zsrc, spec, jnp.asarray(f_fwd), jnp.asarray(f_inv), bias3]
    if mult is not None:
        mb = mult[1] // LANE
        in_specs.append(pl.BlockSpec((nn, LANE), lambda p, j: (rb + p, mb + j)))
        args.append(mult[0])
    in_specs.append(pl.BlockSpec(memory_space=pl.ANY))
    args.append(prev)
    return pl.pallas_call(
        functools.partial(_conv_dense_body, has_mult=mult is not None),
        grid=(n_batch // 2, wb),
        in_specs=in_specs,
        out_specs=pl.BlockSpec((nn, LANE), lambda p, j: (rb + p, j)),
        out_shape=jax.ShapeDtypeStruct(prev.shape, F32),
        input_output_aliases={len(args) - 1: 0},
        compiler_params=_cparams(("arbitrary", "arbitrary")),
        name="hyena_conv_dense",
    )(*args)


def _spec_fft_body(x_ref, f1_ref, f2_ref, o_ref, a_ref):
    n1c = o_ref.shape[0]

    def stage1(g, carry):
        n2s = [g * FFT_UNROLL + u for u in range(FFT_UNROLL)]
        xs = [x_ref[pl.ds(n2, n1c, stride=FFT_N2), :] for n2 in n2s]
        res = [_dot(f1_ref[n2], x) for n2, x in zip(n2s, xs)]
        for n2, r in zip(n2s, res):
            a_ref[pl.ds(pl.multiple_of(n2 * 2 * n1c, 2 * n1c), 2 * n1c), :] = r
        return carry

    lax.fori_loop(0, FFT_N2 // FFT_UNROLL, stage1, 0, unroll=2)
    g2 = FFT_UNROLL // 2

    def stage2(g, carry):
        k1s = [g * g2 + u for u in range(g2)]
        blks = [jnp.concatenate([a_ref[pl.ds(k1, FFT_N2, stride=2 * n1c), :],
                                 a_ref[pl.ds(n1c + k1, FFT_N2, stride=2 * n1c), :]], 0) for k1 in k1s]
        res = [_dot(f2_ref[...], blk) for blk in blks]
        for k1, r in zip(k1s, res):
            o_ref[k1] = r
        return carry

    lax.fori_loop(0, n1c // g2, stage2, 0, unroll=2)


def hyena_spec_fft(full, n):
    f1_real, _, _, f2, _ = _two_stage_dft_tables(n)
    nn, cols = full.shape
    n1c = nn // FFT_N2
    const = lambda shape: pl.BlockSpec(shape, lambda j: (0,) * len(shape), pipeline_mode=pl.Buffered(1))
    return pl.pallas_call(
        _spec_fft_body,
        grid=(cols // LANE,),
        in_specs=[pl.BlockSpec((nn, LANE), lambda j: (0, j)),
                  const((FFT_N2, 2 * n1c, n1c)), const((2 * FFT_N2, 2 * FFT_N2))],
        out_specs=pl.BlockSpec((n1c, 2 * FFT_N2, LANE), lambda j: (0, 0, j)),
        out_shape=jax.ShapeDtypeStruct((n1c, 2 * FFT_N2, cols), F32),
        scratch_shapes=[pltpu.VMEM((FFT_N2 * 2 * n1c, LANE), F32)],
        compiler_params=_cparams(("arbitrary",)),
        name="hyena_spec_fft",
    )(full, jnp.asarray(f1_real, BF16), jnp.asarray(f2, BF16))


def _conv_fft_body(*refs, has_mult):
    z_ref, h_ref, f1_ref, f2_ref, f2i_ref, g1_ref, bias_ref = refs[:7]
    m_ref = refs[7] if has_mult else None
    o_ref, a_ref, b_ref = refs[-3], refs[-2], refs[-1]
    n1c = h_ref.shape[0]
    n1h = n1c // 2
    n2c = FFT_N2
    n = n1h * n2c

    def slab(n2):
        return pl.ds(pl.multiple_of(n2 * 2 * n1c, 2 * n1c), 2 * n1c)

    def stage1(g, carry):
        n2s = [g * FFT_UNROLL + u for u in range(FFT_UNROLL)]
        xs = [jnp.concatenate([z_ref[pl.ds(n2, n1h, stride=n2c), :], z_ref[pl.ds(n + n2, n1h, stride=n2c), :]], 0)
              for n2 in n2s]
        res = [_dot(f1_ref[n2], x) for n2, x in zip(n2s, xs)]
        for n2, r in zip(n2s, res):
            a_ref[slab(n2), :] = r
        return carry

    lax.fori_loop(0, n2c // FFT_UNROLL, stage1, 0, unroll=2)
    g2 = FFT_UNROLL // 2

    def stage2(g, carry):
        k1s = [g * g2 + u for u in range(g2)]
        rows = [(pl.ds(k1, n2c, stride=2 * n1c), pl.ds(n1c + k1, n2c, stride=2 * n1c)) for k1 in k1s]
        blks = [jnp.concatenate([a_ref[re, :], a_ref[im, :]], 0) for re, im in rows]
        xs = [_dot(f2_ref[...], blk) for blk in blks]
        ys = []
        for k1, x in zip(k1s, xs):
            xr, xi = x[:n2c], x[n2c:]
            hr, hi_ = h_ref[k1, 0:n2c, :], h_ref[k1, n2c:2 * n2c, :]
            ys.append(jnp.concatenate([xr * hr - xi * hi_, xr * hi_ + xi * hr], 0))
        bs = [_dot(f2i_ref[...], y) for y in ys]
        for (re, im), b in zip(rows, bs):
            b_ref[re, :] = b[:n2c]
            b_ref[im, :] = b[n2c:]
        return carry

    lax.fori_loop(0, n1c // g2, stage2, 0, unroll=2)
    bias = bias_ref[...]

    def stage3(g, carry):
        n2s = [g * FFT_UNROLL + u for u in range(FFT_UNROLL)]
        blks = [b_ref[slab(n2), :] for n2 in n2s]
        ys = [_dot(g1_ref[n2], blk) for n2, blk in zip(n2s, blks)]
        outs = []
        for n2, y in zip(n2s, ys):
            for part, rows in ((y[:n1h], pl.ds(n2, n1h, stride=n2c)), (y[n1h:], pl.ds(n + n2, n1h, stride=n2c))):
                out = part + z_ref[rows, :] * bias
                if has_mult:
                    out = out * m_ref[rows, :]
                outs.append((rows, out))
        for rows, out in outs:
            o_ref[rows, :] = out
        return carry

    lax.fori_loop(0, n2c // FFT_UNROLL, stage3, 0, unroll=2)


def hyena_conv_fft(zsrc, zcol, n, n_batch, spec, bias3, layer, order, t_rows, mult=None):
    _, f1_cplx, g1, f2, f2i = _two_stage_dft_tables(n)
    n1c = 2 * n // FFT_N2
    cb = zcol // LANE
    wb = BRANCH_W // LANE
    const = lambda shape: pl.BlockSpec(shape, lambda j, p: (0,) * len(shape), pipeline_mode=pl.Buffered(1))
    in_specs = [pl.BlockSpec((2 * n, LANE), lambda j, p: (p, cb + j)),
                pl.BlockSpec((n1c, 2 * FFT_N2, LANE), lambda j, p: (0, 0, order * wb + j),
                             pipeline_mode=pl.Buffered(1)),
                const((FFT_N2, 2 * n1c, n1c)), const((2 * FFT_N2, 2 * FFT_N2)), const((2 * FFT_N2, 2 * FFT_N2)),
                const((FFT_N2, n1c, 2 * n1c)),
                pl.BlockSpec((None, 1, LANE), lambda j, p: (layer * HY_ORDER + order, 0, j))]
    args = [zsrc, spec, jnp.asarray(f1_cplx, BF16), jnp.asarray(f2, BF16), jnp.asarray(f2i, BF16),
            jnp.asarray(g1, BF16), bias3]
    if mult is not None:
        mb = mult[1] // LANE
        in_specs.append(pl.BlockSpec((2 * n, LANE), lambda j, p: (p, mb + j)))
        args.append(mult[0])
    return pl.pallas_call(
        functools.partial(_conv_fft_body, has_mult=mult is not None),
        grid=(wb, n_batch // 2),
        in_specs=in_specs,
        out_specs=pl.BlockSpec((2 * n, LANE), lambda j, p: (p, j)),
        out_shape=jax.ShapeDtypeStruct((t_rows, BRANCH_W), F32),
        scratch_shapes=[pltpu.VMEM((FFT_N2 * 2 * n1c, LANE), F32)] * 2,
        compiler_params=_cparams(("arbitrary", "arbitrary")),
        name="hyena_conv_fft",
    )(*args)


def _swap_pairs(x):
    w = x.shape[-1]
    lane = lax.broadcasted_iota(jnp.int32, x.shape, x.ndim - 1)
    return jnp.where(lane % 2 == 0, pltpu.roll(x, w - 1, x.ndim - 1), pltpu.roll(x, 1, x.ndim - 1))


def _attn_prep_body(g_ref, dq_ref, dk_ref, dv_ref, cg_ref, sg_ref, cd_ref, sd_ref, qn_ref, kn_ref,
                    qg_ref, kg_ref, vg_ref, qd_ref, kd_ref, vd_ref, *, lat_blocks):
    is_lat = pl.program_id(0) < lat_blocks
    cg = jnp.where(is_lat, cg_ref[...], 1.0)
    sg = jnp.where(is_lat, sg_ref[...], 0.0)
    cd = jnp.where(is_lat, cd_ref[...], 1.0)
    sd = jnp.where(is_lat, sd_ref[...], 0.0)

    def rope(x, cs, sn):
        return x * cs + _swap_pairs(x) * sn

    def rms(x, w):
        return x * lax.rsqrt(jnp.mean(x * x, axis=-1, keepdims=True) + EPS) * w

    for h in range(HEADS):
        sl = slice(h * HEAD_D, (h + 1) * HEAD_D)
        q = rope(rms(g_ref[:, sl].astype(F32), qn_ref[...]), cg, sg)
        qg_ref[:, sl] = (q * HEAD_D ** -0.5).astype(BF16)
        qd_ref[:, sl] = (rope(dq_ref[:, sl].astype(F32), cd, sd) * DIFF_QK ** -0.5).astype(BF16)
        kd_ref[:, sl] = rope(dk_ref[:, sl].astype(F32), cd, sd).astype(BF16)
    for h in range(GQA_KV):
        sl = slice(h * HEAD_D, (h + 1) * HEAD_D)
        kin = g_ref[:, BRANCH_W + h * HEAD_D:BRANCH_W + (h + 1) * HEAD_D].astype(F32)
        kg_ref[:, sl] = rope(rms(kin, kn_ref[...]), cg, sg).astype(BF16)
    vg = g_ref[:, BRANCH_W + GQA_KV * HEAD_D:BRANCH_W + 2 * GQA_KV * HEAD_D].astype(F32)
    vg_ref[...] = vg.T.astype(BF16)
    vd_ref[...] = dv_ref[...].astype(F32).T.astype(BF16)


def attn_prep(p, ropes, qn3, kn3, layer, n_lat, n_ctx, n_batch):
    t = p.shape[0]
    r = 256 if n_ctx % 256 == 0 else n_ctx
    nlb, ncb = n_lat // r, n_ctx // r
    lat_blocks = n_batch * nlb
    kvw = GQA_KV * HEAD_D
    w = BRANCH_W

    def kv_row(i):
        lat = (i // nlb) * (nlb + ncb) + ncb + i % nlb
        j = i - lat_blocks
        ctx = (j // ncb) * (nlb + ncb) + j % ncb
        return jnp.where(i < lat_blocks, lat, ctx)

    rope_spec = pl.BlockSpec((r, LANE), lambda i: (jnp.where(i < lat_blocks, i % nlb, 0), 0))
    nkv = n_batch * (n_lat + n_ctx)
    return pl.pallas_call(
        functools.partial(_attn_prep_body, lat_blocks=lat_blocks),
        grid=(t // r,),
        in_specs=[pl.BlockSpec((r, 2 * w), lambda i: (i, C_GQA_QKV // (2 * w))),
                  pl.BlockSpec((r, w), lambda i: (i, C_DIFF_Q // w)),
                  pl.BlockSpec((r, w), lambda i: (i, C_DIFF_K // w)),
                  pl.BlockSpec((r, w), lambda i: (i, C_DIFF_V // w)),
                  rope_spec, rope_spec, rope_spec, rope_spec,
                  pl.BlockSpec((None, 1, LANE), lambda i: (layer, 0, 0)),
                  pl.BlockSpec((None, 1, LANE), lambda i: (layer, 0, 0))],
        out_specs=[pl.BlockSpec((r, w), lambda i: (i, 0)),
                   pl.BlockSpec((r, kvw), lambda i: (kv_row(i), 0)),
                   pl.BlockSpec((kvw, r), lambda i: (0, kv_row(i))),
                   pl.BlockSpec((r, w), lambda i: (i, 0)),
                   pl.BlockSpec((r, w), lambda i: (kv_row(i), 0)),
                   pl.BlockSpec((w, r), lambda i: (0, kv_row(i)))],
        out_shape=[jax.ShapeDtypeStruct((t, w), BF16), jax.ShapeDtypeStruct((nkv, kvw), BF16),
                   jax.ShapeDtypeStruct((kvw, nkv), BF16), jax.ShapeDtypeStruct((t, w), BF16),
                   jax.ShapeDtypeStruct((nkv, w), BF16), jax.ShapeDtypeStruct((w, nkv), BF16)],
        compiler_params=_cparams(("arbitrary",)),
        name="attn_prep",
    )(p, p, p, p, *ropes, qn3, kn3)


def _softmax_cols(s):
    e = jnp.exp(s - jnp.max(s, axis=0, keepdims=True))
    return e, jnp.sum(e, axis=0, keepdims=True)


def _gqa_body(q_ref, k_ref, vt_ref, *rest):
    o_ref = rest[-1]
    group = HEADS // GQA_KV
    dn = (((1,), (1,)), ((), ()))
    for kvh in range(GQA_KV):
        k = k_ref[:, kvh * HEAD_D:(kvh + 1) * HEAD_D]
        vt = vt_ref[kvh * HEAD_D:(kvh + 1) * HEAD_D, :]
        for g in range(group):
            sl = slice((kvh * group + g) * HEAD_D, (kvh * group + g + 1) * HEAD_D)
            e, l = _softmax_cols(lax.dot_general(k, q_ref[:, sl], dn, preferred_element_type=F32))
            ot = jnp.dot(vt, e.astype(BF16), preferred_element_type=F32) / l
            o_ref[:, sl] = ot.T


def _diff_body(q_ref, k_ref, vt_ref, lam_ref, *rest, lam_init):
    o_ref = rest[-1]
    lam4 = lam_ref[...]
    lam = (jnp.exp(jnp.sum(lam4[0:1] * lam4[1:2], axis=-1, keepdims=True))
           - jnp.exp(jnp.sum(lam4[2:3] * lam4[3:4], axis=-1, keepdims=True)) + lam_init)
    dn = (((1,), (1,)), ((), ()))
    for h in range(HEADS):
        sl = slice(h * HEAD_D, (h + 1) * HEAD_D)
        q = q_ref[:, sl]
        k = k_ref[:, sl]
        vt = vt_ref[sl, :]
        first = lax.broadcasted_iota(jnp.int32, q.shape, 1) < DIFF_QK
        zero = jnp.zeros_like(q)
        e1, l1 = _softmax_cols(lax.dot_general(k, jnp.where(first, q, zero), dn, preferred_element_type=F32))
        e2, l2 = _softmax_cols(lax.dot_general(k, jnp.where(first, zero, q), dn, preferred_element_type=F32))
        o1 = jnp.dot(vt, e1.astype(BF16), preferred_element_type=F32) / l1
        o2 = jnp.dot(vt, e2.astype(BF16), preferred_element_type=F32) / l2
        o_ref[:, sl] = (o1 - lam * o2).T


def attention(body, q, k, v, extra, extra_specs, q_row0, nq, kv_per_batch, kv_len, n_batch, tq, name, prev=None):
    t, w = q.shape
    qb0 = q_row0 // tq
    nqb = nq // tq
    kvb = kv_per_batch // kv_len
    in_specs = [pl.BlockSpec((tq, w), lambda b, i: (qb0 + b * nqb + i, 0)),
                pl.BlockSpec((kv_len, k.shape[1]), lambda b, i: (b * kvb, 0)),
                pl.BlockSpec((v.shape[0], kv_len), lambda b, i: (0, b * kvb))] + extra_specs
    args = [q, k, v, *extra]
    aliases = {}
    if prev is not None:
        in_specs.append(pl.BlockSpec(memory_space=pl.ANY))
        args.append(prev)
        aliases = {len(args) - 1: 0}
    return pl.pallas_call(
        body,
        grid=(n_batch, nqb),
        in_specs=in_specs,
        out_specs=pl.BlockSpec((tq, w), lambda b, i: (qb0 + b * nqb + i, 0)),
        out_shape=jax.ShapeDtypeStruct((t, w), F32),
        input_output_aliases=aliases,
        compiler_params=_cparams(("arbitrary", "arbitrary")),
        name=name,
    )(*args)


def _merge_body(h_ref, mod_ref, mg_ref, of_ref, ob_ref, ggate_ref, y1_ref, x2_ref, hgate_ref, oc_ref, cgate_ref,
                od_ref, dgate_ref, gnorm_ref, dnorm_ref, wbr_ref, wout_ref, lng_ref, lnb_ref, o_ref, *, diff_scale):
    def rms_heads(x, w):
        parts = []
        for h in range(HEADS):
            xh = x[:, h * HEAD_D:(h + 1) * HEAD_D]
            parts.append(xh * lax.rsqrt(jnp.mean(xh * xh, axis=-1, keepdims=True) + EPS) * w)
        return jnp.concatenate(parts, -1)

    ys = (rms_heads(of_ref[...] + ob_ref[...], gnorm_ref[...]) * _silu(ggate_ref[...].astype(F32)),
          x2_ref[...] * y1_ref[...] * _silu(hgate_ref[...].astype(F32)),
          oc_ref[...] * _silu(cgate_ref[...].astype(F32)),
          rms_heads(od_ref[...], dnorm_ref[...]) * diff_scale * _silu(dgate_ref[...].astype(F32)))
    acc = None
    for n in range(N_BRANCH):
        proj = jnp.dot(ys[n].astype(BF16), wbr_ref[n], preferred_element_type=F32)
        term = _sigmoid(mg_ref[:, n * D_MODEL:(n + 1) * D_MODEL].astype(F32)) * proj
        acc = term if acc is None else acc + term
    out = jnp.dot(acc.astype(BF16), wout_ref[...], preferred_element_type=F32)
    x = ALPHA * h_ref[...] + mod_ref[2:3, :] * out
    mu = jnp.mean(x, axis=-1, keepdims=True)
    xc = x - mu
    var = jnp.mean(xc * xc, axis=-1, keepdims=True)
    o_ref[...] = xc * lax.rsqrt(var + EPS) * lng_ref[...] + lnb_ref[...]


def merge_postnorm(h_all, mod3, p, o_f, o_b, y1, xv, oc, od, gnorm3, dnorm3, wbr, wout, lng3, lnb3, layer, lam_init,
                   n_lat, n_batch):
    t, d = h_all.shape
    r = 256 if n_lat % 256 == 0 else 64
    w = BRANCH_W
    lbb = n_lat // r
    row = lambda i: jnp.minimum(i // lbb, n_batch)
    tok = lambda cb: pl.BlockSpec((r, w), lambda i: (i, cb))
    vec = lambda width: pl.BlockSpec((None, 1, width), lambda i: (layer, 0, 0))
    return pl.pallas_call(
        functools.partial(_merge_body, diff_scale=1.0 - lam_init),
        grid=(t // r,),
        in_specs=[pl.BlockSpec((r, d), lambda i: (i, 0)),
                  pl.BlockSpec((None, 3, d), lambda i: (row(i), 0, 0)),
                  pl.BlockSpec((r, N_BRANCH * d), lambda i: (i, C_MERGE // (N_BRANCH * d))),
                  tok(0), tok(0), tok(C_GDN_GATE // w), tok(0), tok(1), tok(C_HY_GATE // w), tok(0),
                  tok(C_GQA_GATE // w), tok(0), tok(C_DIFF_GATE // w),
                  vec(LANE), vec(LANE),
                  pl.BlockSpec((None, N_BRANCH, w, d), lambda i: (layer, 0, 0, 0)),
                  pl.BlockSpec((None, d, d), lambda i: (layer, 0, 0)),
                  vec(d), vec(d)],
        out_specs=pl.BlockSpec((r, d), lambda i: (i, 0)),
        out_shape=jax.ShapeDtypeStruct((t, d), F32),
        compiler_params=_cparams(("arbitrary",)),
        name="merge_postnorm",
    )(h_all, mod3, p, o_f, o_b, p, y1, xv, p, oc, p, od, p, gnorm3, dnorm3, wbr, wout, lng3, lnb3)


def _rope_tables(n_lat, dim):
    rows = n_lat // GRID_W
    row = jnp.repeat(jnp.arange(rows, dtype=F32), GRID_W)
    col = jnp.tile(jnp.arange(GRID_W, dtype=F32), rows)
    half = dim // 2
    inv = ROPE_THETA ** (-jnp.arange(0, half, 2, dtype=F32) / half)
    ang = jnp.concatenate([row[:, None] * inv, col[:, None] * inv], -1)
    cos = jnp.repeat(jnp.cos(ang), 2, axis=-1)
    sin = jnp.repeat(jnp.sin(ang), 2, axis=-1)
    sign = jnp.tile(jnp.array([-1.0, 1.0], F32), dim // 2)
    reps = LANE // dim
    return jnp.tile(cos, (1, reps)), jnp.tile(sin * sign, (1, reps))


def kernel(x, c, ctx, c_ctx, w_ada, b_ada, w_in, gdn_conv, gdn_a_log, gdn_dt_bias, gdn_norm, hy_conv, hy_w1, hy_b1,
           hy_w2, hy_b2, hy_w3, hy_b3, hy_w4, hy_freq, hy_bias, gqa_qn, gqa_kn, diff_lam, diff_norm, w_br, w_out,
           ln_g, ln_b):
    nb, n_lat, d = x.shape
    n_ctx = ctx.shape[1]
    t_lat, t_ctx = nb * n_lat, nb * n_ctx
    depth = w_in.shape[0]
    w = BRANCH_W

    w_main = jnp.concatenate([w_in[:, :, O_MERGE:], w_in[:, :, :O_GDN_AB], w_in[:, :, O_GDN_AB + 4 * HEADS:O_MERGE]],
                             axis=2).astype(BF16)
    w_ab = jnp.pad(w_in[:, :, O_GDN_AB:O_GDN_AB + 4 * HEADS], ((0, 0), (0, 0), (0, LANE - 4 * HEADS)))
    wbr_bf = w_br.astype(BF16)
    wout_bf = w_out.astype(BF16)
    b_ada3 = b_ada[:, None, :]
    cvec = jnp.concatenate([c, c_ctx[None, :], jnp.zeros((SUB - nb - 1, d), F32)], 0)
    as3 = lambda a: a[:, None, :]
    gdn_par_r = jnp.pad(jnp.stack([gdn_a_log.reshape(depth, -1), gdn_dt_bias.reshape(depth, -1)], 1),
                        ((0, 0), (0, SUB - 2), (0, LANE - 2 * HEADS)))
    gdn_par_c = jnp.pad(jnp.stack([gdn_a_log.reshape(depth, -1), gdn_dt_bias.reshape(depth, -1)], 2),
                        ((0, 0), (0, 2 * HEADS), (0, LANE - 2)))
    hy_w1p = jnp.pad(hy_w1, ((0, 0), (0, LANE - HY_EMB), (0, 0)))
    hy_bias3 = hy_bias.reshape(depth * HY_ORDER, 1, w)
    ropes = _rope_tables(n_lat, HEAD_D) + _rope_tables(n_lat, DIFF_QK)

    tm = 1024 if (n_lat % 1024 == 0 and t_ctx % 1024 == 0) else n_ctx
    h_all = jnp.concatenate([x.reshape(t_lat, d), ctx.reshape(t_ctx, d)], 0)
    for l in range(depth):
        lam_init = 0.8 - 0.6 * math.exp(-0.3 * l)
        mod3 = ada_mod(cvec, w_ada, b_ada3, l).reshape(SUB, 3, d)
        p, ab = in_proj(h_all, mod3, w_main, w_ab, l, tm, n_lat // tm, nb)

        qkv = dwconv(p, gdn_conv, l, C_GDN_QKV, 3 * w, n_lat, n_ctx, nb, act=True)
        ab_rows = jnp.transpose(ab[:, :4 * HEADS].reshape(-1, GDN_CHUNK, 4 * HEADS), (0, 2, 1))
        o_f, o_b = gdn_scan(qkv, ab, ab_rows, gdn_par_r[l], gdn_par_c[l], n_lat, n_ctx, nb)

        xv = dwconv(p, hy_conv, l, C_HY_XV, 3 * w, n_lat, n_ctx, nb, act=False)
        filt = lambda n: hyena_filter(n, hy_w1p, as3(hy_b1), hy_w2, as3(hy_b2), hy_w3, as3(hy_b3), hy_w4,
                                      as3(hy_freq), l)
        spec_lat = hyena_spec_fft(filt(n_lat), n_lat)
        spec_ctx = hyena_spec_dense(filt(n_ctx), n_ctx)
        z1 = hyena_conv_fft(xv, 2 * w, n_lat, nb, spec_lat, hy_bias3, l, 0, t_lat + t_ctx, mult=(xv, 0))
        z1 = hyena_conv_dense(xv, 2 * w, t_lat, n_ctx, nb, spec_ctx, hy_bias3, l, 0, z1, mult=(xv, 0))
        y1 = hyena_conv_fft(z1, 0, n_lat, nb, spec_lat, hy_bias3, l, 1, t_lat + t_ctx)
        y1 = hyena_conv_dense(z1, 0, t_lat, n_ctx, nb, spec_ctx, hy_bias3, l, 1, y1)

        qg, kg, vg, qd, kd, vd = attn_prep(p, ropes, as3(gqa_qn), as3(gqa_kn), l, n_lat, n_ctx, nb)
        kv_all = n_lat + n_ctx
        tq = min(256, n_ctx)
        lam_spec = [pl.BlockSpec((None, 4, DIFF_QK), lambda b, i: (l, 0, 0))]
        diff_body = functools.partial(_diff_body, lam_init=lam_init)
        oc = attention(_gqa_body, qg, kg, vg, (), [], 0, n_lat, kv_all, kv_all, nb, tq, "gqa_lat")
        oc = attention(_gqa_body, qg, kg, vg, (), [], t_lat, n_ctx, kv_all, n_ctx, nb, tq, "gqa_ctx", prev=oc)
        od = attention(diff_body, qd, kd, vd, (diff_lam,), lam_spec, 0, n_lat, kv_all, kv_all, nb, tq, "diff_lat")
        od = attention(diff_body, qd, kd, vd, (diff_lam,), lam_spec, t_lat, n_ctx, kv_all, n_ctx, nb, tq, "diff_ctx",
                       prev=od)

        h_all = merge_postnorm(h_all, mod3, p, o_f, o_b, y1, xv, oc, od, as3(gdn_norm), as3(diff_norm), wbr_bf, wout_bf,
                               as3(ln_g), as3(ln_b), l, lam_init, n_lat, nb)
    return h_all[:t_lat].reshape(nb, n_lat, d)
```

```python
import functools
import math

import numpy as np
import jax
import jax.numpy as jnp
from jax import lax
from jax.experimental import pallas as pl
from jax.experimental.pallas import tpu as pltpu

F32 = jnp.float32
BF16 = jnp.bfloat16
HI = lax.Precision.HIGHEST

D_MODEL = 1024
DEPTH = 4
GRID_W = 64
BRANCH_W = D_MODEL // 2
N_BRANCH = 4
HEADS = 4
HEAD_D = BRANCH_W // HEADS
GDN_CONV = 4
GDN_CHUNK = 64
GDN_SUB = 4
HY_CONV = 3
HY_EMB = 33
HY_BANDS = (HY_EMB - 1) // 2
HY_FH = 64
HY_ORDER = 2
HY_MIN_DECAY = math.log(1e-2) / 1.5
HY_MAX_DECAY = math.log(1e-2) / 0.3
GQA_KV = 2
DIFF_QK = HEAD_D // 2
ROPE_THETA = 10000.0
EPS = 1e-6
ALPHA = (2.0 * DEPTH) ** 0.25

LANE = 128
SUB = 8
HALO = 16
FFT_N2 = 128
FFT_UNROLL = 8
VMEM_LIMIT = 60 * 1024 * 1024

C_MERGE = 0
C_GDN_QKV = 4096
C_GDN_GATE = 5632
C_HY_XV = 6144
C_HY_GATE = 7680
C_GQA_QKV = 8192
C_GQA_GATE = 9216
C_DIFF_Q = 9728
C_DIFF_K = 10240
C_DIFF_V = 10752
C_DIFF_GATE = 11264
N_MAIN = 11776
O_GDN_AB = 1536
O_MERGE = 7696


def _cparams(sem):
    return pltpu.CompilerParams(dimension_semantics=sem, vmem_limit_bytes=VMEM_LIMIT)


def _dot(a, b, hi=False):
    if hi:
        return jnp.dot(a, b, precision=HI, preferred_element_type=F32)
    return jnp.dot(a.astype(BF16), b.astype(BF16), preferred_element_type=F32)


def _dot_nt(a, b, hi=False):
    dn = (((1,), (1,)), ((), ()))
    if hi:
        return lax.dot_general(a, b, dn, precision=HI, preferred_element_type=F32)
    return lax.dot_general(a.astype(BF16), b.astype(BF16), dn, preferred_element_type=F32)


def _dot_tn(a, b):
    return lax.dot_general(a.astype(BF16), b.astype(BF16), (((0,), (0,)), ((), ())), preferred_element_type=F32)


def _sigmoid(x):
    return 1.0 / (1.0 + jnp.exp(-x))


def _silu(x):
    return x * _sigmoid(x)


def _softplus(x):
    return jnp.maximum(x, 0.0) + jnp.log1p(jnp.exp(-jnp.abs(x)))


def _ada_body(c_ref, w_ref, b_ref, o_ref):
    o_ref[...] = _dot(_silu(c_ref[...]), w_ref[...], hi=True) + b_ref[...]


def ada_mod(cvec, w_ada, b_ada3, layer):
    d = cvec.shape[1]
    tn = 512
    return pl.pallas_call(
        _ada_body,
        grid=(3 * d // tn,),
        in_specs=[pl.BlockSpec((SUB, d), lambda j: (0, 0)),
                  pl.BlockSpec((None, d, tn), lambda j: (layer, 0, j)),
                  pl.BlockSpec((None, 1, tn), lambda j: (layer, 0, j))],
        out_specs=pl.BlockSpec((SUB, tn), lambda j: (0, j)),
        out_shape=jax.ShapeDtypeStruct((SUB, 3 * d), F32),
        compiler_params=_cparams(("arbitrary",)),
        name="ada_mod",
    )(cvec, w_ada, b_ada3)


def _inproj_body(h_ref, mod_ref, w_ref, wab_ref, o_ref, ab_ref, u_ref):
    @pl.when(pl.program_id(1) == 0)
    def _():
        x = h_ref[...]
        mu = jnp.mean(x, axis=-1, keepdims=True)
        xc = x - mu
        var = jnp.mean(xc * xc, axis=-1, keepdims=True)
        u = xc * lax.rsqrt(var + EPS) * (1.0 + mod_ref[1:2, :]) + mod_ref[0:1, :]
        u_ref[...] = u.astype(BF16)
        ab_ref[...] = _dot(u, wab_ref[...], hi=True)

    o_ref[...] = jnp.dot(u_ref[...], w_ref[...], preferred_element_type=F32).astype(BF16)


def in_proj(h_all, mod3, w_main, w_ab, layer, tm, lat_blocks_per_batch, n_batch):
    t, d = h_all.shape
    n_main = w_main.shape[2]
    tn = n_main // 4
    row = lambda i: jnp.minimum(i // lat_blocks_per_batch, n_batch)
    return pl.pallas_call(
        _inproj_body,
        grid=(t // tm, n_main // tn),
        in_specs=[pl.BlockSpec((tm, d), lambda i, j: (i, 0)),
                  pl.BlockSpec((None, 3, d), lambda i, j: (row(i), 0, 0)),
                  pl.BlockSpec((None, d, tn), lambda i, j: (layer, 0, j)),
                  pl.BlockSpec((None, d, LANE), lambda i, j: (layer, 0, 0))],
        out_specs=[pl.BlockSpec((tm, tn), lambda i, j: (i, j)),
                   pl.BlockSpec((tm, LANE), lambda i, j: (i, 0))],
        out_shape=[jax.ShapeDtypeStruct((t, n_main), BF16), jax.ShapeDtypeStruct((t, LANE), F32)],
        scratch_shapes=[pltpu.VMEM((tm, d), BF16)],
        compiler_params=_cparams(("arbitrary", "arbitrary")),
        name="in_proj",
    )(h_all, mod3, w_main, w_ab)


def _dwconv_body(xp_ref, x_ref, xn_ref, w_ref, o_ref, pad_ref, *, taps, pad_l, t_lat, n_lat, n_ctx, sb, act):
    i = pl.program_id(0)
    r = x_ref.shape[0]
    pad_ref[0:HALO, :] = xp_ref[...].astype(F32)
    pad_ref[HALO:HALO + r, :] = x_ref[...].astype(F32)
    pad_ref[HALO + r:2 * HALO + r, :] = xn_ref[...].astype(F32)
    row = lax.broadcasted_iota(jnp.int32, (sb, 1), 0)
    for k in range(r // sb):
        g0 = i * r + k * sb
        in_lat = g0 < t_lat
        starts = jnp.where(in_lat, g0 % n_lat == 0, (g0 - t_lat) % n_ctx == 0)
        ends = jnp.where(in_lat, (g0 + sb) % n_lat == 0, (g0 + sb - t_lat) % n_ctx == 0)
        acc = None
        for j in range(taps):
            d = j - pad_l
            off = HALO + k * sb + d
            xs = pad_ref[off:off + sb, :]
            if d < 0:
                xs = jnp.where(jnp.logical_and(starts, row < -d), 0.0, xs)
            elif d > 0:
                xs = jnp.where(jnp.logical_and(ends, row >= sb - d), 0.0, xs)
            term = w_ref[j:j + 1, :] * xs
            acc = term if acc is None else acc + term
        if act:
            acc = _silu(acc)
        o_ref[k * sb:(k + 1) * sb, :] = acc


def dwconv(p, w_conv, layer, col0, width, n_lat, n_ctx, n_batch, act):
    t = p.shape[0]
    taps = w_conv.shape[1]
    sb = min(256, n_ctx)
    r = 1024 if t % 1024 == 0 else sb
    lw = 512
    cb = col0 // lw
    rs = r // HALO
    body = functools.partial(_dwconv_body, taps=taps, pad_l=(taps - 1) // 2, t_lat=n_batch * n_lat, n_lat=n_lat,
                             n_ctx=n_ctx, sb=sb, act=act)
    return pl.pallas_call(
        body,
        grid=(t // r, width // lw),
        in_specs=[pl.BlockSpec((HALO, lw), lambda i, j: (jnp.maximum(i * rs - 1, 0), cb + j)),
                  pl.BlockSpec((r, lw), lambda i, j: (i, cb + j)),
                  pl.BlockSpec((HALO, lw), lambda i, j: (jnp.minimum((i + 1) * rs, t // HALO - 1), cb + j)),
                  pl.BlockSpec((None, taps, lw), lambda i, j: (layer, 0, j))],
        out_specs=pl.BlockSpec((r, lw), lambda i, j: (i, j)),
        out_shape=jax.ShapeDtypeStruct((t, width), F32),
        scratch_shapes=[pltpu.VMEM((r + 2 * HALO, lw), F32)],
        compiler_params=_cparams(("arbitrary", "arbitrary")),
        name="dwconv",
    )(p, p, p, w_conv)


def _gdn_body(qf_ref, qb_ref, abcf_ref, abcb_ref, abrf_ref, abrb_ref, pr_ref, pc_ref, of_ref, ob_ref, s_ref):
    c = GDN_CHUNK

    @pl.when(pl.program_id(1) == 0)
    def _():
        s_ref[...] = jnp.zeros_like(s_ref)

    ii = lax.broadcasted_iota(jnp.int32, (c, c), 0)
    jj = lax.broadcasted_iota(jnp.int32, (c, c), 1)
    lmat = (jj <= ii).astype(F32)
    eye = (jj == ii).astype(F32)
    alr, dtr = pr_ref[0:1, :], pr_ref[1:2, :]
    alc, dtc = pc_ref[:, 0:1], pc_ref[:, 1:2]
    chains = []
    for d in range(2):
        qkv_ref = (qf_ref, qb_ref)[d]
        abc_ref = (abcf_ref, abcb_ref)[d]
        abr_ref = (abrf_ref, abrb_ref)[d]
        incl = (jj <= ii) if d == 0 else (jj >= ii)
        strict = (jj < ii) if d == 0 else (jj > ii)
        for j in range(GDN_SUB):
            rows = slice(j * c, (j + 1) * c)
            abc = abc_ref[rows, :]
            abr = abr_ref[j]
            g_c = -jnp.exp(alr) * _softplus(abc + dtr)
            g_r = -jnp.exp(alc) * _softplus(abr + dtc)
            cum_c = _dot(lmat, g_c, hi=True)
            cum_r = _dot_nt(g_r, lmat, hi=True)
            if d == 1:
                cum_c = cum_c[c - 1:c, :] - cum_c + g_c
                cum_r = cum_r[:, c - 1:c] - cum_r + g_r
            beta_all = _sigmoid(abc)
            for h in range(HEADS):
                idx = HEADS * d + h
                q = qkv_ref[rows, h * HEAD_D:(h + 1) * HEAD_D]
                k = qkv_ref[rows, BRANCH_W + h * HEAD_D:BRANCH_W + (h + 1) * HEAD_D]
                v = qkv_ref[rows, 2 * BRANCH_W + h * HEAD_D:2 * BRANCH_W + (h + 1) * HEAD_D]
                q = q * lax.rsqrt(jnp.sum(q * q, axis=-1, keepdims=True) + EPS) * (HEAD_D ** -0.5)
                k = k * lax.rsqrt(jnp.sum(k * k, axis=-1, keepdims=True) + EPS)
                cc = cum_c[:, idx:idx + 1]
                cr = cum_r[idx:idx + 1, :]
                dec = jnp.exp(jnp.where(incl, cc - cr, -1e30))
                beta = beta_all[:, 2 * HEADS + idx:2 * HEADS + idx + 1]
                ecum = jnp.exp(cc)
                tot = cc[c - 1:c, :] if d == 0 else cc[0:1, :]
                chains.append(dict(d=d, h=h, j=j, rows=rows, q=q, k=k, dec=dec, strict=strict, beta=beta, ecum=ecum,
                                   tot=tot, rhs=jnp.concatenate([k * (beta * ecum), v * beta], 1),
                                   k_tail=k * jnp.exp(tot - cc)))
    for ch in chains:
        ch["kk"] = _dot_nt(ch["k"], ch["k"])
        ch["qk"] = _dot_nt(ch["q"], ch["k"])
    for ch in chains:
        ch["p"] = -jnp.where(ch["strict"], ch["beta"] * ch["kk"] * ch["dec"], 0.0)
        ch["inv"] = eye + ch["p"]
    for _ in range(int(math.log2(c)) - 1):
        for ch in chains:
            ch["p"] = _dot(ch["p"], ch["p"])
        for ch in chains:
            ch["inv"] = ch["inv"] + _dot(ch["inv"], ch["p"])
    for ch in chains:
        ch["wu"] = _dot(ch["inv"], ch["rhs"])
        ch["lhs"] = jnp.concatenate([ch["wu"][:, :HEAD_D], ch["q"] * ch["ecum"]], 0)
    state = {(d, h): s_ref[d, h] for d in range(2) for h in range(HEADS)}
    for step in range(GDN_SUB):
        cur = [ch for ch in chains if ch["j"] == (step if ch["d"] == 0 else GDN_SUB - 1 - step)]
        for ch in cur:
            ch["ws"] = _dot(ch["lhs"], state[ch["d"], ch["h"]])
        for ch in cur:
            ch["v_new"] = ch["wu"][:, HEAD_D:] - ch["ws"][:c]
        for ch in cur:
            out_ref = (of_ref, ob_ref)[ch["d"]]
            h = ch["h"]
            out_ref[ch["rows"], h * HEAD_D:(h + 1) * HEAD_D] = ch["ws"][c:] + _dot(ch["qk"] * ch["dec"], ch["v_new"])
            state[ch["d"], h] = state[ch["d"], h] * jnp.exp(ch["tot"]) + _dot_tn(ch["k_tail"], ch["v_new"])
    for (d, h), val in state.items():
        s_ref[d, h] = val


def gdn_scan(qkv, ab, ab_rows, par_r, par_c, n_lat, n_ctx, n_batch):
    t = qkv.shape[0]
    c = GDN_SUB * GDN_CHUNK
    nlc, ncc = n_lat // c, n_ctx // c
    base = n_batch * nlc

    def fwd(b, s):
        return jnp.where(s < ncc, base + b * ncc + s, b * nlc + (s - ncc))

    def bwd(b, s):
        return jnp.where(s < ncc, base + b * ncc + (ncc - 1 - s), b * nlc + (nlc - 1 - (s - ncc)))

    w3 = 3 * BRANCH_W
    return pl.pallas_call(
        _gdn_body,
        grid=(n_batch, ncc + nlc),
        in_specs=[pl.BlockSpec((c, w3), lambda b, s: (fwd(b, s), 0)),
                  pl.BlockSpec((c, w3), lambda b, s: (bwd(b, s), 0)),
                  pl.BlockSpec((c, LANE), lambda b, s: (fwd(b, s), 0)),
                  pl.BlockSpec((c, LANE), lambda b, s: (bwd(b, s), 0)),
                  pl.BlockSpec((GDN_SUB, 4 * HEADS, GDN_CHUNK), lambda b, s: (fwd(b, s), 0, 0)),
                  pl.BlockSpec((GDN_SUB, 4 * HEADS, GDN_CHUNK), lambda b, s: (bwd(b, s), 0, 0)),
                  pl.BlockSpec((SUB, LANE), lambda b, s: (0, 0)),
                  pl.BlockSpec((4 * HEADS, LANE), lambda b, s: (0, 0))],
        out_specs=[pl.BlockSpec((c, BRANCH_W), lambda b, s: (fwd(b, s), 0)),
                   pl.BlockSpec((c, BRANCH_W), lambda b, s: (bwd(b, s), 0))],
        out_shape=[jax.ShapeDtypeStruct((t, BRANCH_W), F32), jax.ShapeDtypeStruct((t, BRANCH_W), F32)],
        scratch_shapes=[pltpu.VMEM((2, HEADS, HEAD_D, HEAD_D), F32)],
        compiler_params=_cparams(("arbitrary", "arbitrary")),
        name="gdn_scan",
    )(qkv, qkv, ab, ab, ab_rows, ab_rows, par_r, par_c)


def _hyfilt_body(z_ref, aux_ref, w1_ref, b1_ref, w2_ref, b2_ref, w3_ref, b3_ref, w4_ref, fr_ref, dl_ref, o_ref):
    fr = fr_ref[...]
    h = jnp.sin(fr * (_dot(z_ref[...], w1_ref[...], hi=True) + b1_ref[...]))
    h = jnp.sin(fr * (_dot(h, w2_ref[...], hi=True) + b2_ref[...]))
    h = jnp.sin(fr * (_dot(h, w3_ref[...], hi=True) + b3_ref[...]))
    taps = _dot(h, w4_ref[...], hi=True) * jnp.exp(-aux_ref[:, 0:1] * dl_ref[...])
    w = BRANCH_W
    negative = aux_ref[:, 1:2] > 0.5
    keep = aux_ref[:, 2:3]
    for o in range(HY_ORDER):
        fwd = taps[:, o * 2 * w:o * 2 * w + w]
        bwd = taps[:, o * 2 * w + w:(o + 1) * 2 * w]
        o_ref[:, o * w:(o + 1) * w] = jnp.where(negative, bwd, fwd) * keep


def hyena_filter(n, w1p, b1, w2, b2, w3, b3, w4, fr, layer):
    row = jnp.arange(2 * n)
    src = jnp.where(row <= n, row, 2 * n - row)
    pos = jnp.where(row == n, 0, src).astype(F32)
    tt = pos / max(n - 1, 1)
    ang = (2.0 * math.pi / n) * pos[:, None] * jnp.linspace(1e-4, HY_BANDS - 1, HY_BANDS, dtype=F32)
    z = jnp.concatenate([tt[:, None], jnp.cos(ang), -jnp.sin(ang), jnp.zeros((2 * n, LANE - HY_EMB), F32)], -1)
    aux = jnp.stack([tt, (row > n).astype(F32), (row != n).astype(F32)], 1)
    aux = jnp.pad(aux, ((0, 0), (0, SUB - 3)))
    deltas = jnp.abs(jnp.linspace(HY_MIN_DECAY, HY_MAX_DECAY, BRANCH_W, dtype=F32))
    dl = jnp.tile(deltas, 2 * HY_ORDER)[None, :]
    r = 512
    wo = 2 * HY_ORDER * BRANCH_W
    full = lambda shape: pl.BlockSpec((None,) + shape, lambda i: (layer,) + (0,) * len(shape))
    return pl.pallas_call(
        _hyfilt_body,
        grid=(2 * n // r,),
        in_specs=[pl.BlockSpec((r, LANE), lambda i: (i, 0)),
                  pl.BlockSpec((r, SUB), lambda i: (i, 0)),
                  full((LANE, HY_FH)), full((1, HY_FH)), full((HY_FH, HY_FH)), full((1, HY_FH)),
                  full((HY_FH, HY_FH)), full((1, HY_FH)), full((HY_FH, wo)), full((1, HY_FH)),
                  pl.BlockSpec((1, wo), lambda i: (0, 0))],
        out_specs=pl.BlockSpec((r, HY_ORDER * BRANCH_W), lambda i: (i, 0)),
        out_shape=jax.ShapeDtypeStruct((2 * n, HY_ORDER * BRANCH_W), F32),
        compiler_params=_cparams(("arbitrary",)),
        name="hyena_filter",
    )(z, aux, w1p, b1, w2, b2, w3, b3, w4, fr, dl)


@functools.lru_cache(maxsize=None)
def _dense_dft_tables(n):
    nn = 2 * n
    k = np.arange(nn)[:, None].astype(np.float64)
    m = np.arange(nn)[None, :].astype(np.float64)
    ang = -2.0 * np.pi * k * m / nn
    wr, wi = np.cos(ang), np.sin(ang)
    f_real = np.concatenate([wr, wi], 0)
    wr_h, wi_h = wr[:, :n], wi[:, :n]
    f_fwd = np.block([[wr_h, -wi_h], [wi_h, wr_h]])
    cr, ci = wr.T[:n] / nn, -wi.T[:n] / nn
    f_inv = np.block([[cr, -ci], [ci, cr]])
    return (np.asarray(f_real, np.float32), np.asarray(f_fwd, np.float32), np.asarray(f_inv, np.float32))


@functools.lru_cache(maxsize=None)
def _two_stage_dft_tables(n):
    nn = 2 * n
    n2c = FFT_N2
    n1c = nn // n2c
    n1h = n1c // 2
    k1 = np.arange(n1c).astype(np.float64)
    n1 = np.arange(n1c).astype(np.float64)
    n2 = np.arange(n2c).astype(np.float64)
    ang = -2.0 * np.pi * (k1[None, :, None] * n1[None, None, :] / n1c + n2[:, None, None] * k1[None, :, None] / nn)
    mr, mi = np.cos(ang), np.sin(ang)
    f1_real = np.concatenate([mr, mi], 1)
    mrh, mih = mr[:, :, :n1h], mi[:, :, :n1h]
    f1_cplx = np.concatenate([np.concatenate([mrh, -mih], 2), np.concatenate([mih, mrh], 2)], 1)
    gr = np.transpose(mr, (0, 2, 1))[:, :n1h, :] / nn
    gi = -np.transpose(mi, (0, 2, 1))[:, :n1h, :] / nn
    g1 = np.concatenate([np.concatenate([gr, -gi], 2), np.concatenate([gi, gr], 2)], 1)
    k2 = np.arange(n2c).astype(np.float64)
    a2 = -2.0 * np.pi * k2[:, None] * n2[None, :] / n2c
    fr, fi = np.cos(a2), np.sin(a2)
    f2 = np.block([[fr, -fi], [fi, fr]])
    f2i = np.block([[fr.T, fi.T], [-fi.T, fr.T]])
    f32 = lambda a: np.asarray(a, np.float32)
    return f32(f1_real), f32(f1_cplx), f32(g1), f32(f2), f32(f2i)


def _spec_dense_body(f_ref, x_ref, o_ref):
    o_ref[...] = _dot(f_ref[...], x_ref[...], hi=True)


def hyena_spec_dense(full, n):
    f_real, _, _ = _dense_dft_tables(n)
    nn, cols = full.shape
    return pl.pallas_call(
        _spec_dense_body,
        grid=(cols // LANE,),
        in_specs=[pl.BlockSpec((2 * nn, nn), lambda j: (0, 0)),
                  pl.BlockSpec((nn, LANE), lambda j: (0, j))],
        out_specs=pl.BlockSpec((2 * nn, LANE), lambda j: (0, j)),
        out_shape=jax.ShapeDtypeStruct((2 * nn, cols), F32),
        compiler_params=_cparams(("arbitrary",)),
        name="hyena_spec_dense",
    )(jnp.asarray(f_real), full)


def _conv_dense_body(*refs, has_mult):
    z_ref, h_ref, ff_ref, fi_ref, bias_ref = refs[:5]
    m_ref = refs[5] if has_mult else None
    o_ref = refs[-1]
    z = z_ref[...]
    nn = z.shape[0]
    x = _dot(ff_ref[...], z, hi=True)
    xr, xi = x[:nn], x[nn:]
    hr, hi_ = h_ref[0:nn, :], h_ref[nn:2 * nn, :]
    y = _dot(fi_ref[...], jnp.concatenate([xr * hr - xi * hi_, xr * hi_ + xi * hr], 0), hi=True)
    out = y + z * bias_ref[...]
    if has_mult:
        out = out * m_ref[...]
    o_ref[...] = out


def hyena_conv_dense(zsrc, zcol, row0, n, n_batch, spec, bias3, layer, order, prev, mult=None):
    _, f_fwd, f_inv = _dense_dft_tables(n)
    nn = 2 * n
    rb, cb = row0 // nn, zcol // LANE
    wb = BRANCH_W // LANE
    in_specs = [pl.BlockSpec((nn, LANE), lambda p, j: (rb + p, cb + j)),
                pl.BlockSpec((2 * nn, LANE), lambda p, j: (0, order * wb + j)),
                pl.BlockSpec((2 * nn, nn), lambda p, j: (0, 0)),
                pl.BlockSpec((nn, 2 * nn), lambda p, j: (0, 0)),
                pl.BlockSpec((None, 1, LANE), lambda p, j: (layer * HY_ORDER + order, 0, j))]
    args = [zsrc, spec, jnp.asarray(f_fwd), jnp.asarray(f_inv), bias3]
    if mult is not None:
        mb = mult[1] // LANE
        in_specs.append(pl.BlockSpec((nn, LANE), lambda p, j: (rb + p, mb + j)))
        args.append(mult[0])
    in_specs.append(pl.BlockSpec(memory_space=pl.ANY))
    args.append(prev)
    return pl.pallas_call(
        functools.partial(_conv_dense_body, has_mult=mult is not None),
        grid=(n_batch // 2, wb),
        in_specs=in_specs,
        out_specs=pl.BlockSpec((nn, LANE), lambda p, j: (rb + p, j)),
        out_shape=jax.ShapeDtypeStruct(prev.shape, F32),
        input_output_aliases={len(args) - 1: 0},
        compiler_params=_cparams(("arbitrary", "arbitrary")),
        name="hyena_conv_dense",
    )(*args)


def _spec_fft_body(x_ref, f1_ref, f2_ref, o_ref, a_ref):
    n1c = o_ref.shape[0]

    def stage1(g, carry):
        n2s = [g * FFT_UNROLL + u for u in range(FFT_UNROLL)]
        xs = [x_ref[pl.ds(n2, n1c, stride=FFT_N2), :] for n2 in n2s]
        res = [_dot(f1_ref[n2], x) for n2, x in zip(n2s, xs)]
        for n2, r in zip(n2s, res):
            a_ref[pl.ds(pl.multiple_of(n2 * 2 * n1c, 2 * n1c), 2 * n1c), :] = r
        return carry

    lax.fori_loop(0, FFT_N2 // FFT_UNROLL, stage1, 0, unroll=2)
    g2 = FFT_UNROLL // 2

    def stage2(g, carry):
        k1s = [g * g2 + u for u in range(g2)]
        blks = [jnp.concatenate([a_ref[pl.ds(k1, FFT_N2, stride=2 * n1c), :],
                                 a_ref[pl.ds(n1c + k1, FFT_N2, stride=2 * n1c), :]], 0) for k1 in k1s]
        res = [_dot(f2_ref[...], blk) for blk in blks]
        for k1, r in zip(k1s, res):
            o_ref[k1] = r
        return carry

    lax.fori_loop(0, n1c // g2, stage2, 0, unroll=2)


def hyena_spec_fft(full, n):
    f1_real, _, _, f2, _ = _two_stage_dft_tables(n)
    nn, cols = full.shape
    n1c = nn // FFT_N2
    const = lambda shape: pl.BlockSpec(shape, lambda j: (0,) * len(shape), pipeline_mode=pl.Buffered(1))
    return pl.pallas_call(
        _spec_fft_body,
        grid=(cols // LANE,),
        in_specs=[pl.BlockSpec((nn, LANE), lambda j: (0, j)),
                  const((FFT_N2, 2 * n1c, n1c)), const((2 * FFT_N2, 2 * FFT_N2))],
        out_specs=pl.BlockSpec((n1c, 2 * FFT_N2, LANE), lambda j: (0, 0, j)),
        out_shape=jax.ShapeDtypeStruct((n1c, 2 * FFT_N2, cols), F32),
        scratch_shapes=[pltpu.VMEM((FFT_N2 * 2 * n1c, LANE), F32)],
        compiler_params=_cparams(("arbitrary",)),
        name="hyena_spec_fft",
    )(full, jnp.asarray(f1_real, BF16), jnp.asarray(f2, BF16))


def _conv_fft_body(*refs, has_mult):
    z_ref, h_ref, f1_ref, f2_ref, f2i_ref, g1_ref, bias_ref = refs[:7]
    m_ref = refs[7] if has_mult else None
    o_ref, a_ref, b_ref = refs[-3], refs[-2], refs[-1]
    n1c = h_ref.shape[0]
    n1h = n1c // 2
    n2c = FFT_N2
    n = n1h * n2c

    def slab(n2):
        return pl.ds(pl.multiple_of(n2 * 2 * n1c, 2 * n1c), 2 * n1c)

    def stage1(g, carry):
        n2s = [g * FFT_UNROLL + u for u in range(FFT_UNROLL)]
        xs = [jnp.concatenate([z_ref[pl.ds(n2, n1h, stride=n2c), :], z_ref[pl.ds(n + n2, n1h, stride=n2c), :]], 0)
              for n2 in n2s]
        res = [_dot(f1_ref[n2], x) for n2, x in zip(n2s, xs)]
        for n2, r in zip(n2s, res):
            a_ref[slab(n2), :] = r
        return carry

    lax.fori_loop(0, n2c // FFT_UNROLL, stage1, 0, unroll=2)
    g2 = FFT_UNROLL // 2

    def stage2(g, carry):
        k1s = [g * g2 + u for u in range(g2)]
        rows = [(pl.ds(k1, n2c, stride=2 * n1c), pl.ds(n1c + k1, n2c, stride=2 * n1c)) for k1 in k1s]
        blks = [jnp.concatenate([a_ref[re, :], a_ref[im, :]], 0) for re, im in rows]
        xs = [_dot(f2_ref[...], blk) for blk in blks]
        ys = []
        for k1, x in zip(k1s, xs):
            xr, xi = x[:n2c], x[n2c:]
            hr, hi_ = h_ref[k1, 0:n2c, :], h_ref[k1, n2c:2 * n2c, :]
            ys.append(jnp.concatenate([xr * hr - xi * hi_, xr * hi_ + xi * hr], 0))
        bs = [_dot(f2i_ref[...], y) for y in ys]
        for (re, im), b in zip(rows, bs):
            b_ref[re, :] = b[:n2c]
            b_ref[im, :] = b[n2c:]
        return carry

    lax.fori_loop(0, n1c // g2, stage2, 0, unroll=2)
    bias = bias_ref[...]

    def stage3(g, carry):
        n2s = [g * FFT_UNROLL + u for u in range(FFT_UNROLL)]
        blks = [b_ref[slab(n2), :] for n2 in n2s]
        ys = [_dot(g1_ref[n2], blk) for n2, blk in zip(n2s, blks)]
        outs = []
        for n2, y in zip(n2s, ys):
            for part, rows in ((y[:n1h], pl.ds(n2, n1h, stride=n2c)), (y[n1h:], pl.ds(n + n2, n1h, stride=n2c))):
                out = part + z_ref[rows, :] * bias
                if has_mult:
                    out = out * m_ref[rows, :]
                outs.append((rows, out))
        for rows, out in outs:
            o_ref[rows, :] = out
        return carry

    lax.fori_loop(0, n2c // FFT_UNROLL, stage3, 0, unroll=2)


def hyena_conv_fft(zsrc, zcol, n, n_batch, spec, bias3, layer, order, t_rows, mult=None):
    _, f1_cplx, g1, f2, f2i = _two_stage_dft_tables(n)
    n1c = 2 * n // FFT_N2
    cb = zcol // LANE
    wb = BRANCH_W // LANE
    const = lambda shape: pl.BlockSpec(shape, lambda j, p: (0,) * len(shape), pipeline_mode=pl.Buffered(1))
    in_specs = [pl.BlockSpec((2 * n, LANE), lambda j, p: (p, cb + j)),
                pl.BlockSpec((n1c, 2 * FFT_N2, LANE), lambda j, p: (0, 0, order * wb + j),
                             pipeline_mode=pl.Buffered(1)),
                const((FFT_N2, 2 * n1c, n1c)), const((2 * FFT_N2, 2 * FFT_N2)), const((2 * FFT_N2, 2 * FFT_N2)),
                const((FFT_N2, n1c, 2 * n1c)),
                pl.BlockSpec((None, 1, LANE), lambda j, p: (layer * HY_ORDER + order, 0, j))]
    args = [zsrc, spec, jnp.asarray(f1_cplx, BF16), jnp.asarray(f2, BF16), jnp.asarray(f2i, BF16),
            jnp.asarray(g1, BF16), bias3]
    if mult is not None:
        mb = mult[1] // LANE
        in_specs.append(pl.BlockSpec((2 * n, LANE), lambda j, p: (p, mb + j)))
        args.append(mult[0])
    return pl.pallas_call(
        functools.partial(_conv_fft_body, has_mult=mult is not None),
        grid=(wb, n_batch // 2),
        in_specs=in_specs,
        out_specs=pl.BlockSpec((2 * n, LANE), lambda j, p: (p, j)),
        out_shape=jax.ShapeDtypeStruct((t_rows, BRANCH_W), F32),
        scratch_shapes=[pltpu.VMEM((FFT_N2 * 2 * n1c, LANE), F32)] * 2,
        compiler_params=_cparams(("arbitrary", "arbitrary")),
        name="hyena_conv_fft",
    )(*args)


def _swap_pairs(x):
    w = x.shape[-1]
    lane = lax.broadcasted_iota(jnp.int32, x.shape, x.ndim - 1)
    return jnp.where(lane % 2 == 0, pltpu.roll(x, w - 1, x.ndim - 1), pltpu.roll(x, 1, x.ndim - 1))


def _attn_prep_body(g_ref, dq_ref, dk_ref, dv_ref, cg_ref, sg_ref, cd_ref, sd_ref, qn_ref, kn_ref,
                    qg_ref, kg_ref, vg_ref, qd_ref, kd_ref, vd_ref, *, lat_blocks):
    is_lat = pl.program_id(0) < lat_blocks
    cg = jnp.where(is_lat, cg_ref[...], 1.0)
    sg = jnp.where(is_lat, sg_ref[...], 0.0)
    cd = jnp.where(is_lat, cd_ref[...], 1.0)
    sd = jnp.where(is_lat, sd_ref[...], 0.0)

    def rope(x, cs, sn):
        return x * cs + _swap_pairs(x) * sn

    def rms(x, w):
        return x * lax.rsqrt(jnp.mean(x * x, axis=-1, keepdims=True) + EPS) * w

    for h in range(HEADS):
        sl = slice(h * HEAD_D, (h + 1) * HEAD_D)
        q = rope(rms(g_ref[:, sl].astype(F32), qn_ref[...]), cg, sg)
        qg_ref[:, sl] = (q * HEAD_D ** -0.5).astype(BF16)
        qd_ref[:, sl] = (rope(dq_ref[:, sl].astype(F32), cd, sd) * DIFF_QK ** -0.5).astype(BF16)
        kd_ref[:, sl] = rope(dk_ref[:, sl].astype(F32), cd, sd).astype(BF16)
    for h in range(GQA_KV):
        sl = slice(h * HEAD_D, (h + 1) * HEAD_D)
        kin = g_ref[:, BRANCH_W + h * HEAD_D:BRANCH_W + (h + 1) * HEAD_D].astype(F32)
        kg_ref[:, sl] = rope(rms(kin, kn_ref[...]), cg, sg).astype(BF16)
    ones = jnp.ones((g_ref.shape[0], HEAD_D), BF16)
    for h in range(GQA_KV):
        src = BRANCH_W + (GQA_KV + h) * HEAD_D
        vg_ref[:, 2 * h * HEAD_D:(2 * h + 1) * HEAD_D] = g_ref[:, src:src + HEAD_D].astype(BF16)
        vg_ref[:, (2 * h + 1) * HEAD_D:(2 * h + 2) * HEAD_D] = ones
    for h in range(HEADS):
        vd_ref[:, 2 * h * HEAD_D:(2 * h + 1) * HEAD_D] = dv_ref[:, h * HEAD_D:(h + 1) * HEAD_D].astype(BF16)
        vd_ref[:, (2 * h + 1) * HEAD_D:(2 * h + 2) * HEAD_D] = ones


def attn_prep(p, ropes, qn3, kn3, layer, n_lat, n_ctx, n_batch):
    t = p.shape[0]
    r = 256 if n_ctx % 256 == 0 else n_ctx
    nlb, ncb = n_lat // r, n_ctx // r
    lat_blocks = n_batch * nlb
    kvw = GQA_KV * HEAD_D
    w = BRANCH_W

    def kv_row(i):
        lat = (i // nlb) * (nlb + ncb) + ncb + i % nlb
        j = i - lat_blocks
        ctx = (j // ncb) * (nlb + ncb) + j % ncb
        return jnp.where(i < lat_blocks, lat, ctx)

    rope_spec = pl.BlockSpec((r, LANE), lambda i: (jnp.where(i < lat_blocks, i % nlb, 0), 0))
    nkv = n_batch * (n_lat + n_ctx)
    return pl.pallas_call(
        functools.partial(_attn_prep_body, lat_blocks=lat_blocks),
        grid=(t // r,),
        in_specs=[pl.BlockSpec((r, 2 * w), lambda i: (i, C_GQA_QKV // (2 * w))),
                  pl.BlockSpec((r, w), lambda i: (i, C_DIFF_Q // w)),
                  pl.BlockSpec((r, w), lambda i: (i, C_DIFF_K // w)),
                  pl.BlockSpec((r, w), lambda i: (i, C_DIFF_V // w)),
                  rope_spec, rope_spec, rope_spec, rope_spec,
                  pl.BlockSpec((None, 1, LANE), lambda i: (layer, 0, 0)),
                  pl.BlockSpec((None, 1, LANE), lambda i: (layer, 0, 0))],
        out_specs=[pl.BlockSpec((r, w), lambda i: (i, 0)),
                   pl.BlockSpec((r, kvw), lambda i: (kv_row(i), 0)),
                   pl.BlockSpec((r, 2 * kvw), lambda i: (kv_row(i), 0)),
                   pl.BlockSpec((r, w), lambda i: (i, 0)),
                   pl.BlockSpec((r, w), lambda i: (kv_row(i), 0)),
                   pl.BlockSpec((r, 2 * w), lambda i: (kv_row(i), 0))],
        out_shape=[jax.ShapeDtypeStruct((t, w), BF16), jax.ShapeDtypeStruct((nkv, kvw), BF16),
                   jax.ShapeDtypeStruct((nkv, 2 * kvw), BF16), jax.ShapeDtypeStruct((t, w), BF16),
                   jax.ShapeDtypeStruct((nkv, w), BF16), jax.ShapeDtypeStruct((nkv, 2 * w), BF16)],
        compiler_params=_cparams(("arbitrary",)),
        name="attn_prep",
    )(p, p, p, p, *ropes, qn3, kn3)


def _softmax_pv(s, v_ext):
    e = jnp.exp((s - jnp.max(s, axis=-1, keepdims=True)).astype(BF16))
    acc = jnp.dot(e, v_ext, preferred_element_type=F32)
    return acc[:, :HEAD_D] / acc[:, HEAD_D:HEAD_D + 1]


def _gqa_body(q_ref, k_ref, v_ref, *rest):
    o_ref = rest[-1]
    group = HEADS // GQA_KV
    for kvh in range(GQA_KV):
        k = k_ref[:, kvh * HEAD_D:(kvh + 1) * HEAD_D]
        v_ext = v_ref[:, 2 * kvh * HEAD_D:(2 * kvh + 2) * HEAD_D]
        for g in range(group):
            sl = slice((kvh * group + g) * HEAD_D, (kvh * group + g + 1) * HEAD_D)
            s = lax.dot_general(q_ref[:, sl], k, (((1,), (1,)), ((), ())), preferred_element_type=F32)
            o_ref[:, sl] = _softmax_pv(s, v_ext)


def _diff_body(q_ref, k_ref, v_ref, lam_ref, *rest, lam_init):
    o_ref = rest[-1]
    lam4 = lam_ref[...]
    lam = (jnp.exp(jnp.sum(lam4[0:1] * lam4[1:2], axis=-1, keepdims=True))
           - jnp.exp(jnp.sum(lam4[2:3] * lam4[3:4], axis=-1, keepdims=True)) + lam_init)
    dn = (((1,), (1,)), ((), ()))
    for h in range(HEADS):
        sl = slice(h * HEAD_D, (h + 1) * HEAD_D)
        q = q_ref[:, sl]
        k = k_ref[:, sl]
        v_ext = v_ref[:, 2 * h * HEAD_D:(2 * h + 2) * HEAD_D]
        first = lax.broadcasted_iota(jnp.int32, q.shape, 1) < DIFF_QK
        zero = jnp.zeros_like(q)
        o1 = _softmax_pv(lax.dot_general(jnp.where(first, q, zero), k, dn, preferred_element_type=F32), v_ext)
        o2 = _softmax_pv(lax.dot_general(jnp.where(first, zero, q), k, dn, preferred_element_type=F32), v_ext)
        o_ref[:, sl] = o1 - lam * o2


def attention(body, q, k, v, extra, extra_specs, q_row0, nq, kv_per_batch, kv_len, n_batch, tq, name, prev=None):
    t, w = q.shape
    qb0 = q_row0 // tq
    nqb = nq // tq
    kvb = kv_per_batch // kv_len
    in_specs = [pl.BlockSpec((tq, w), lambda b, i: (qb0 + b * nqb + i, 0)),
                pl.BlockSpec((kv_len, k.shape[1]), lambda b, i: (b * kvb, 0)),
                pl.BlockSpec((kv_len, v.shape[1]), lambda b, i: (b * kvb, 0))] + extra_specs
    args = [q, k, v, *extra]
    aliases = {}
    if prev is not None:
        in_specs.append(pl.BlockSpec(memory_space=pl.ANY))
        args.append(prev)
        aliases = {len(args) - 1: 0}
    return pl.pallas_call(
        body,
        grid=(n_batch, nqb),
        in_specs=in_specs,
        out_specs=pl.BlockSpec((tq, w), lambda b, i: (qb0 + b * nqb + i, 0)),
        out_shape=jax.ShapeDtypeStruct((t, w), F32),
        input_output_aliases=aliases,
        compiler_params=_cparams(("arbitrary", "arbitrary")),
        name=name,
    )(*args)


def _merge_body(h_ref, mod_ref, mg_ref, of_ref, ob_ref, ggate_ref, y1_ref, x2_ref, hgate_ref, oc_ref, cgate_ref,
                od_ref, dgate_ref, gnorm_ref, dnorm_ref, wbr_ref, wout_ref, lng_ref, lnb_ref, o_ref, *, diff_scale):
    def rms_heads(x, w):
        parts = []
        for h in range(HEADS):
            xh = x[:, h * HEAD_D:(h + 1) * HEAD_D]
            parts.append(xh * lax.rsqrt(jnp.mean(xh * xh, axis=-1, keepdims=True) + EPS) * w)
        return jnp.concatenate(parts, -1)

    ys = (rms_heads(of_ref[...] + ob_ref[...], gnorm_ref[...]) * _silu(ggate_ref[...].astype(F32)),
          x2_ref[...] * y1_ref[...] * _silu(hgate_ref[...].astype(F32)),
          oc_ref[...] * _silu(cgate_ref[...].astype(F32)),
          rms_heads(od_ref[...], dnorm_ref[...]) * diff_scale * _silu(dgate_ref[...].astype(F32)))
    acc = None
    for n in range(N_BRANCH):
        proj = jnp.dot(ys[n].astype(BF16), wbr_ref[n], preferred_element_type=F32)
        term = _sigmoid(mg_ref[:, n * D_MODEL:(n + 1) * D_MODEL].astype(F32)) * proj
        acc = term if acc is None else acc + term
    out = jnp.dot(acc.astype(BF16), wout_ref[...], preferred_element_type=F32)
    x = ALPHA * h_ref[...] + mod_ref[2:3, :] * out
    mu = jnp.mean(x, axis=-1, keepdims=True)
    xc = x - mu
    var = jnp.mean(xc * xc, axis=-1, keepdims=True)
    o_ref[...] = xc * lax.rsqrt(var + EPS) * lng_ref[...] + lnb_ref[...]


def merge_postnorm(h_all, mod3, p, o_f, o_b, y1, xv, oc, od, gnorm3, dnorm3, wbr, wout, lng3, lnb3, layer, lam_init,
                   n_lat, n_batch):
    t, d = h_all.shape
    r = 256 if n_lat % 256 == 0 else 64
    w = BRANCH_W
    lbb = n_lat // r
    row = lambda i: jnp.minimum(i // lbb, n_batch)
    tok = lambda cb: pl.BlockSpec((r, w), lambda i: (i, cb))
    vec = lambda width: pl.BlockSpec((None, 1, width), lambda i: (layer, 0, 0))
    return pl.pallas_call(
        functools.partial(_merge_body, diff_scale=1.0 - lam_init),
        grid=(t // r,),
        in_specs=[pl.BlockSpec((r, d), lambda i: (i, 0)),
                  pl.BlockSpec((None, 3, d), lambda i: (row(i), 0, 0)),
                  pl.BlockSpec((r, N_BRANCH * d), lambda i: (i, C_MERGE // (N_BRANCH * d))),
                  tok(0), tok(0), tok(C_GDN_GATE // w), tok(0), tok(1), tok(C_HY_GATE // w), tok(0),
                  tok(C_GQA_GATE // w), tok(0), tok(C_DIFF_GATE // w),
                  vec(LANE), vec(LANE),
                  pl.BlockSpec((None, N_BRANCH, w, d), lambda i: (layer, 0, 0, 0)),
                  pl.BlockSpec((None, d, d), lambda i: (layer, 0, 0)),
                  vec(d), vec(d)],
        out_specs=pl.BlockSpec((r, d), lambda i: (i, 0)),
        out_shape=jax.ShapeDtypeStruct((t, d), F32),
        compiler_params=_cparams(("arbitrary",)),
        name="merge_postnorm",
    )(h_all, mod3, p, o_f, o_b, p, y1, xv, p, oc, p, od, p, gnorm3, dnorm3, wbr, wout, lng3, lnb3)


def _rope_tables(n_lat, dim):
    rows = n_lat // GRID_W
    row = jnp.repeat(jnp.arange(rows, dtype=F32), GRID_W)
    col = jnp.tile(jnp.arange(GRID_W, dtype=F32), rows)
    half = dim // 2
    inv = ROPE_THETA ** (-jnp.arange(0, half, 2, dtype=F32) / half)
    ang = jnp.concatenate([row[:, None] * inv, col[:, None] * inv], -1)
    cos = jnp.repeat(jnp.cos(ang), 2, axis=-1)
    sin = jnp.repeat(jnp.sin(ang), 2, axis=-1)
    sign = jnp.tile(jnp.array([-1.0, 1.0], F32), dim // 2)
    reps = LANE // dim
    return jnp.tile(cos, (1, reps)), jnp.tile(sin * sign, (1, reps))


def kernel(x, c, ctx, c_ctx, w_ada, b_ada, w_in, gdn_conv, gdn_a_log, gdn_dt_bias, gdn_norm, hy_conv, hy_w1, hy_b1,
           hy_w2, hy_b2, hy_w3, hy_b3, hy_w4, hy_freq, hy_bias, gqa_qn, gqa_kn, diff_lam, diff_norm, w_br, w_out,
           ln_g, ln_b):
    nb, n_lat, d = x.shape
    n_ctx = ctx.shape[1]
    t_lat, t_ctx = nb * n_lat, nb * n_ctx
    depth = w_in.shape[0]
    w = BRANCH_W

    w_main = jnp.concatenate([w_in[:, :, O_MERGE:], w_in[:, :, :O_GDN_AB], w_in[:, :, O_GDN_AB + 4 * HEADS:O_MERGE]],
                             axis=2).astype(BF16)
    w_ab = jnp.pad(w_in[:, :, O_GDN_AB:O_GDN_AB + 4 * HEADS], ((0, 0), (0, 0), (0, LANE - 4 * HEADS)))
    wbr_bf = w_br.astype(BF16)
    wout_bf = w_out.astype(BF16)
    b_ada3 = b_ada[:, None, :]
    cvec = jnp.concatenate([c, c_ctx[None, :], jnp.zeros((SUB - nb - 1, d), F32)], 0)
    as3 = lambda a: a[:, None, :]
    gdn_par_r = jnp.pad(jnp.stack([gdn_a_log.reshape(depth, -1), gdn_dt_bias.reshape(depth, -1)], 1),
                        ((0, 0), (0, SUB - 2), (0, LANE - 2 * HEADS)))
    gdn_par_c = jnp.pad(jnp.stack([gdn_a_log.reshape(depth, -1), gdn_dt_bias.reshape(depth, -1)], 2),
                        ((0, 0), (0, 2 * HEADS), (0, LANE - 2)))
    hy_w1p = jnp.pad(hy_w1, ((0, 0), (0, LANE - HY_EMB), (0, 0)))
    hy_bias3 = hy_bias.reshape(depth * HY_ORDER, 1, w)
    ropes = _rope_tables(n_lat, HEAD_D) + _rope_tables(n_lat, DIFF_QK)

    tm = 1024 if (n_lat % 1024 == 0 and t_ctx % 1024 == 0) else n_ctx
    h_all = jnp.concatenate([x.reshape(t_lat, d), ctx.reshape(t_ctx, d)], 0)
    for l in range(depth):
        lam_init = 0.8 - 0.6 * math.exp(-0.3 * l)
        mod3 = ada_mod(cvec, w_ada, b_ada3, l).reshape(SUB, 3, d)
        p, ab = in_proj(h_all, mod3, w_main, w_ab, l, tm, n_lat // tm, nb)

        qkv = dwconv(p, gdn_conv, l, C_GDN_QKV, 3 * w, n_lat, n_ctx, nb, act=True)
        ab_rows = jnp.transpose(ab[:, :4 * HEADS].reshape(-1, GDN_CHUNK, 4 * HEADS), (0, 2, 1))
        o_f, o_b = gdn_scan(qkv, ab, ab_rows, gdn_par_r[l], gdn_par_c[l], n_lat, n_ctx, nb)

        xv = dwconv(p, hy_conv, l, C_HY_XV, 3 * w, n_lat, n_ctx, nb, act=False)
        filt = lambda n: hyena_filter(n, hy_w1p, as3(hy_b1), hy_w2, as3(hy_b2), hy_w3, as3(hy_b3), hy_w4,
                                      as3(hy_freq), l)
        spec_lat = hyena_spec_fft(filt(n_lat), n_lat)
        spec_ctx = hyena_spec_dense(filt(n_ctx), n_ctx)
        z1 = hyena_conv_fft(xv, 2 * w, n_lat, nb, spec_lat, hy_bias3, l, 0, t_lat + t_ctx, mult=(xv, 0))
        z1 = hyena_conv_dense(xv, 2 * w, t_lat, n_ctx, nb, spec_ctx, hy_bias3, l, 0, z1, mult=(xv, 0))
        y1 = hyena_conv_fft(z1, 0, n_lat, nb, spec_lat, hy_bias3, l, 1, t_lat + t_ctx)
        y1 = hyena_conv_dense(z1, 0, t_lat, n_ctx, nb, spec_ctx, hy_bias3, l, 1, y1)

        qg, kg, vg, qd, kd, vd = attn_prep(p, ropes, as3(gqa_qn), as3(gqa_kn), l, n_lat, n_ctx, nb)
        kv_all = n_lat + n_ctx
        tq = min(256, n_ctx)
        lam_spec = [pl.BlockSpec((None, 4, DIFF_QK), lambda b, i: (l, 0, 0))]
        diff_body = functools.partial(_diff_body, lam_init=lam_init)
        oc = attention(_gqa_body, qg, kg, vg, (), [], 0, n_lat, kv_all, kv_all, nb, tq, "gqa_lat")
        oc = attention(_gqa_body, qg, kg, vg, (), [], t_lat, n_ctx, kv_all, n_ctx, nb, tq, "gqa_ctx", prev=oc)
        od = attention(diff_body, qd, kd, vd, (diff_lam,), lam_spec, 0, n_lat, kv_all, kv_all, nb, tq, "diff_lat")
        od = attention(diff_body, qd, kd, vd, (diff_lam,), lam_spec, t_lat, n_ctx, kv_all, n_ctx, nb, tq, "diff_ctx",
                       prev=od)

        h_all = merge_postnorm(h_all, mod3, p, o_f, o_b, y1, xv, oc, od, as3(gdn_norm), as3(diff_norm), wbr_bf, wout_bf,
                               as3(ln_g), as3(ln_b), l, lam_init, n_lat, nb)
    return h_all[:t_lat].reshape(nb, n_lat, d)
```

```python
import functools
import math

import numpy as np
import jax
import jax.numpy as jnp
from jax import lax
from jax.experimental import pallas as pl
from jax.experimental.pallas import tpu as pltpu

F32 = jnp.float32
BF16 = jnp.bfloat16
HI = lax.Precision.HIGHEST

D_MODEL = 1024
DEPTH = 4
GRID_W = 64
BRANCH_W = D_MODEL // 2
N_BRANCH = 4
HEADS = 4
HEAD_D = BRANCH_W // HEADS
GDN_CONV = 4
GDN_CHUNK = 64
GDN_SUB = 4
HY_CONV = 3
HY_EMB = 33
HY_BANDS = (HY_EMB - 1) // 2
HY_FH = 64
HY_ORDER = 2
HY_MIN_DECAY = math.log(1e-2) / 1.5
HY_MAX_DECAY = math.log(1e-2) / 0.3
GQA_KV = 2
DIFF_QK = HEAD_D // 2
ROPE_THETA = 10000.0
EPS = 1e-6
ALPHA = (2.0 * DEPTH) ** 0.25

LANE = 128
SUB = 8
HALO = 16
FFT_N2 = 128
FFT_UNROLL = 8
VMEM_LIMIT = 60 * 1024 * 1024

C_MERGE = 0
C_GDN_QKV = 4096
C_GDN_GATE = 5632
C_HY_XV = 6144
C_HY_GATE = 7680
C_GQA_QKV = 8192
C_GQA_GATE = 9216
C_DIFF_Q = 9728
C_DIFF_K = 10240
C_DIFF_V = 10752
C_DIFF_GATE = 11264
N_MAIN = 11776
O_GDN_AB = 1536
O_MERGE = 7696


def _cparams(sem):
    return pltpu.CompilerParams(dimension_semantics=sem, vmem_limit_bytes=VMEM_LIMIT)


def _dot(a, b, hi=False):
    if hi:
        return jnp.dot(a, b, precision=HI, preferred_element_type=F32)
    return jnp.dot(a.astype(BF16), b.astype(BF16), preferred_element_type=F32)


def _dot_nt(a, b, hi=False):
    dn = (((1,), (1,)), ((), ()))
    if hi:
        return lax.dot_general(a, b, dn, precision=HI, preferred_element_type=F32)
    return lax.dot_general(a.astype(BF16), b.astype(BF16), dn, preferred_element_type=F32)


def _dot_tn(a, b):
    return lax.dot_general(a.astype(BF16), b.astype(BF16), (((0,), (0,)), ((), ())), preferred_element_type=F32)


def _sigmoid(x):
    return 1.0 / (1.0 + jnp.exp(-x))


def _silu(x):
    return x * _sigmoid(x)


def _softplus(x):
    return jnp.maximum(x, 0.0) + jnp.log1p(jnp.exp(-jnp.abs(x)))


def _ada_body(c_ref, w_ref, b_ref, o_ref):
    o_ref[...] = _dot(_silu(c_ref[...]), w_ref[...], hi=True) + b_ref[...]


def ada_mod(cvec, w_ada, b_ada3, layer):
    d = cvec.shape[1]
    tn = 512
    return pl.pallas_call(
        _ada_body,
        grid=(3 * d // tn,),
        in_specs=[pl.BlockSpec((SUB, d), lambda j: (0, 0)),
                  pl.BlockSpec((None, d, tn), lambda j: (layer, 0, j)),
                  pl.BlockSpec((None, 1, tn), lambda j: (layer, 0, j))],
        out_specs=pl.BlockSpec((SUB, tn), lambda j: (0, j)),
        out_shape=jax.ShapeDtypeStruct((SUB, 3 * d), F32),
        compiler_params=_cparams(("arbitrary",)),
        name="ada_mod",
    )(cvec, w_ada, b_ada3)


def _inproj_body(h_ref, mod_ref, w_ref, wab_ref, o_ref, ab_ref, u_ref):
    @pl.when(pl.program_id(1) == 0)
    def _():
        x = h_ref[...]
        mu = jnp.mean(x, axis=-1, keepdims=True)
        xc = x - mu
        var = jnp.mean(xc * xc, axis=-1, keepdims=True)
        u = xc * lax.rsqrt(var + EPS) * (1.0 + mod_ref[1:2, :]) + mod_ref[0:1, :]
        u_hi = u.astype(BF16)
        u_ref[...] = u_hi
        u_lo = (u - u_hi.astype(F32)).astype(BF16)
        w_ab = wab_ref[...]
        w_hi = w_ab.astype(BF16)
        w_lo = (w_ab - w_hi.astype(F32)).astype(BF16)
        ab_ref[...] = (jnp.dot(u_hi, w_hi, preferred_element_type=F32) + jnp.dot(u_hi, w_lo, preferred_element_type=F32)
                       + jnp.dot(u_lo, w_hi, preferred_element_type=F32))

    o_ref[...] = jnp.dot(u_ref[...], w_ref[...], preferred_element_type=F32).astype(BF16)


def in_proj(h_all, mod3, w_main, w_ab, layer, tm, lat_blocks_per_batch, n_batch):
    t, d = h_all.shape
    n_main = w_main.shape[2]
    tn = n_main // 4
    row = lambda i: jnp.minimum(i // lat_blocks_per_batch, n_batch)
    return pl.pallas_call(
        _inproj_body,
        grid=(t // tm, n_main // tn),
        in_specs=[pl.BlockSpec((tm, d), lambda i, j: (i, 0)),
                  pl.BlockSpec((None, 3, d), lambda i, j: (row(i), 0, 0)),
                  pl.BlockSpec((None, d, tn), lambda i, j: (layer, 0, j)),
                  pl.BlockSpec((None, d, LANE), lambda i, j: (layer, 0, 0))],
        out_specs=[pl.BlockSpec((tm, tn), lambda i, j: (i, j)),
                   pl.BlockSpec((tm, LANE), lambda i, j: (i, 0))],
        out_shape=[jax.ShapeDtypeStruct((t, n_main), BF16), jax.ShapeDtypeStruct((t, LANE), F32)],
        scratch_shapes=[pltpu.VMEM((tm, d), BF16)],
        compiler_params=_cparams(("arbitrary", "arbitrary")),
        name="in_proj",
    )(h_all, mod3, w_main, w_ab)


def _dwconv_body(xp_ref, x_ref, xn_ref, w_ref, o_ref, pad_ref, *, taps, pad_l, t_lat, n_lat, n_ctx, sb, act):
    i = pl.program_id(0)
    r = x_ref.shape[0]
    pad_ref[0:HALO, :] = xp_ref[...].astype(F32)
    pad_ref[HALO:HALO + r, :] = x_ref[...].astype(F32)
    pad_ref[HALO + r:2 * HALO + r, :] = xn_ref[...].astype(F32)
    row = lax.broadcasted_iota(jnp.int32, (sb, 1), 0)
    for k in range(r // sb):
        g0 = i * r + k * sb
        in_lat = g0 < t_lat
        starts = jnp.where(in_lat, g0 % n_lat == 0, (g0 - t_lat) % n_ctx == 0)
        ends = jnp.where(in_lat, (g0 + sb) % n_lat == 0, (g0 + sb - t_lat) % n_ctx == 0)
        acc = None
        for j in range(taps):
            d = j - pad_l
            off = HALO + k * sb + d
            xs = pad_ref[off:off + sb, :]
            if d < 0:
                xs = jnp.where(jnp.logical_and(starts, row < -d), 0.0, xs)
            elif d > 0:
                xs = jnp.where(jnp.logical_and(ends, row >= sb - d), 0.0, xs)
            term = w_ref[j:j + 1, :] * xs
            acc = term if acc is None else acc + term
        if act:
            acc = _silu(acc)
        o_ref[k * sb:(k + 1) * sb, :] = acc


def dwconv(p, w_conv, layer, col0, width, n_lat, n_ctx, n_batch, act):
    t = p.shape[0]
    taps = w_conv.shape[1]
    sb = min(256, n_ctx)
    r = 1024 if t % 1024 == 0 else sb
    lw = 512
    cb = col0 // lw
    rs = r // HALO
    body = functools.partial(_dwconv_body, taps=taps, pad_l=(taps - 1) // 2, t_lat=n_batch * n_lat, n_lat=n_lat,
                             n_ctx=n_ctx, sb=sb, act=act)
    return pl.pallas_call(
        body,
        grid=(t // r, width // lw),
        in_specs=[pl.BlockSpec((HALO, lw), lambda i, j: (jnp.maximum(i * rs - 1, 0), cb + j)),
                  pl.BlockSpec((r, lw), lambda i, j: (i, cb + j)),
                  pl.BlockSpec((HALO, lw), lambda i, j: (jnp.minimum((i + 1) * rs, t // HALO - 1), cb + j)),
                  pl.BlockSpec((None, taps, lw), lambda i, j: (layer, 0, j))],
        out_specs=pl.BlockSpec((r, lw), lambda i, j: (i, j)),
        out_shape=jax.ShapeDtypeStruct((t, width), F32),
        scratch_shapes=[pltpu.VMEM((r + 2 * HALO, lw), F32)],
        compiler_params=_cparams(("arbitrary", "arbitrary")),
        name="dwconv",
    )(p, p, p, w_conv)


def _gdn_body(qf_ref, qb_ref, abcf_ref, abcb_ref, abrf_ref, abrb_ref, pr_ref, pc_ref, of_ref, ob_ref, s_ref):
    c = GDN_CHUNK

    @pl.when(pl.program_id(1) == 0)
    def _():
        s_ref[...] = jnp.zeros_like(s_ref)

    ii = lax.broadcasted_iota(jnp.int32, (c, c), 0)
    jj = lax.broadcasted_iota(jnp.int32, (c, c), 1)
    lmat = (jj <= ii).astype(F32)
    eye = (jj == ii).astype(F32)
    alr, dtr = pr_ref[0:1, :], pr_ref[1:2, :]
    alc, dtc = pc_ref[:, 0:1], pc_ref[:, 1:2]
    chains = []
    for d in range(2):
        qkv_ref = (qf_ref, qb_ref)[d]
        abc_ref = (abcf_ref, abcb_ref)[d]
        abr_ref = (abrf_ref, abrb_ref)[d]
        incl = (jj <= ii) if d == 0 else (jj >= ii)
        strict = (jj < ii) if d == 0 else (jj > ii)
        for j in range(GDN_SUB):
            rows = slice(j * c, (j + 1) * c)
            abc = abc_ref[rows, :]
            abr = abr_ref[j]
            g_c = -jnp.exp(alr) * _softplus(abc + dtr)
            g_r = -jnp.exp(alc) * _softplus(abr + dtc)
            cum_c = _dot(lmat, g_c, hi=True)
            cum_r = _dot_nt(g_r, lmat, hi=True)
            if d == 1:
                cum_c = cum_c[c - 1:c, :] - cum_c + g_c
                cum_r = cum_r[:, c - 1:c] - cum_r + g_r
            beta_all = _sigmoid(abc)
            for h in range(HEADS):
                idx = HEADS * d + h
                q = qkv_ref[rows, h * HEAD_D:(h + 1) * HEAD_D]
                k = qkv_ref[rows, BRANCH_W + h * HEAD_D:BRANCH_W + (h + 1) * HEAD_D]
                v = qkv_ref[rows, 2 * BRANCH_W + h * HEAD_D:2 * BRANCH_W + (h + 1) * HEAD_D]
                q = q * lax.rsqrt(jnp.sum(q * q, axis=-1, keepdims=True) + EPS) * (HEAD_D ** -0.5)
                k = k * lax.rsqrt(jnp.sum(k * k, axis=-1, keepdims=True) + EPS)
                cc = cum_c[:, idx:idx + 1]
                cr = cum_r[idx:idx + 1, :]
                dec = jnp.exp(jnp.where(incl, cc - cr, -1e30))
                beta = beta_all[:, 2 * HEADS + idx:2 * HEADS + idx + 1]
                ecum = jnp.exp(cc)
                tot = cc[c - 1:c, :] if d == 0 else cc[0:1, :]
                chains.append(dict(d=d, h=h, j=j, rows=rows, q=q, k=k, dec=dec, strict=strict, beta=beta, ecum=ecum,
                                   tot=tot, rhs=jnp.concatenate([k * (beta * ecum), v * beta], 1),
                                   k_tail=k * jnp.exp(tot - cc)))
    for ch in chains:
        ch["kk"] = _dot_nt(ch["k"], ch["k"])
        ch["qk"] = _dot_nt(ch["q"], ch["k"])
    for ch in chains:
        ch["p"] = -jnp.where(ch["strict"], ch["beta"] * ch["kk"] * ch["dec"], 0.0)
        ch["inv"] = eye + ch["p"]
    for _ in range(int(math.log2(c)) - 1):
        for ch in chains:
            ch["p"] = _dot(ch["p"], ch["p"])
        for ch in chains:
            ch["inv"] = ch["inv"] + _dot(ch["inv"], ch["p"])
    for ch in chains:
        ch["wu"] = _dot(ch["inv"], ch["rhs"])
        ch["lhs"] = jnp.concatenate([ch["wu"][:, :HEAD_D], ch["q"] * ch["ecum"]], 0)
    state = {(d, h): s_ref[d, h] for d in range(2) for h in range(HEADS)}
    for step in range(GDN_SUB):
        cur = [ch for ch in chains if ch["j"] == (step if ch["d"] == 0 else GDN_SUB - 1 - step)]
        for ch in cur:
            ch["ws"] = _dot(ch["lhs"], state[ch["d"], ch["h"]])
        for ch in cur:
            ch["v_new"] = ch["wu"][:, HEAD_D:] - ch["ws"][:c]
        for ch in cur:
            out_ref = (of_ref, ob_ref)[ch["d"]]
            h = ch["h"]
            out_ref[ch["rows"], h * HEAD_D:(h + 1) * HEAD_D] = ch["ws"][c:] + _dot(ch["qk"] * ch["dec"], ch["v_new"])
            state[ch["d"], h] = state[ch["d"], h] * jnp.exp(ch["tot"]) + _dot_tn(ch["k_tail"], ch["v_new"])
    for (d, h), val in state.items():
        s_ref[d, h] = val


def gdn_scan(qkv, ab, ab_rows, par_r, par_c, n_lat, n_ctx, n_batch):
    t = qkv.shape[0]
    c = GDN_SUB * GDN_CHUNK
    nlc, ncc = n_lat // c, n_ctx // c
    base = n_batch * nlc

    def fwd(b, s):
        return jnp.where(s < ncc, base + b * ncc + s, b * nlc + (s - ncc))

    def bwd(b, s):
        return jnp.where(s < ncc, base + b * ncc + (ncc - 1 - s), b * nlc + (nlc - 1 - (s - ncc)))

    w3 = 3 * BRANCH_W
    return pl.pallas_call(
        _gdn_body,
        grid=(n_batch, ncc + nlc),
        in_specs=[pl.BlockSpec((c, w3), lambda b, s: (fwd(b, s), 0)),
                  pl.BlockSpec((c, w3), lambda b, s: (bwd(b, s), 0)),
                  pl.BlockSpec((c, LANE), lambda b, s: (fwd(b, s), 0)),
                  pl.BlockSpec((c, LANE), lambda b, s: (bwd(b, s), 0)),
                  pl.BlockSpec((GDN_SUB, 4 * HEADS, GDN_CHUNK), lambda b, s: (fwd(b, s), 0, 0)),
                  pl.BlockSpec((GDN_SUB, 4 * HEADS, GDN_CHUNK), lambda b, s: (bwd(b, s), 0, 0)),
                  pl.BlockSpec((SUB, LANE), lambda b, s: (0, 0)),
                  pl.BlockSpec((4 * HEADS, LANE), lambda b, s: (0, 0))],
        out_specs=[pl.BlockSpec((c, BRANCH_W), lambda b, s: (fwd(b, s), 0)),
                   pl.BlockSpec((c, BRANCH_W), lambda b, s: (bwd(b, s), 0))],
        out_shape=[jax.ShapeDtypeStruct((t, BRANCH_W), F32), jax.ShapeDtypeStruct((t, BRANCH_W), F32)],
        scratch_shapes=[pltpu.VMEM((2, HEADS, HEAD_D, HEAD_D), F32)],
        compiler_params=_cparams(("arbitrary", "arbitrary")),
        name="gdn_scan",
    )(qkv, qkv, ab, ab, ab_rows, ab_rows, par_r, par_c)


def _hyfilt_body(z_ref, aux_ref, w1_ref, b1_ref, w2_ref, b2_ref, w3_ref, b3_ref, w4_ref, fr_ref, dl_ref, o_ref):
    fr = fr_ref[...]
    h = jnp.sin(fr * (_dot(z_ref[...], w1_ref[...], hi=True) + b1_ref[...]))
    h = jnp.sin(fr * (_dot(h, w2_ref[...], hi=True) + b2_ref[...]))
    h = jnp.sin(fr * (_dot(h, w3_ref[...], hi=True) + b3_ref[...]))
    taps = _dot(h, w4_ref[...], hi=True) * jnp.exp(-aux_ref[:, 0:1] * dl_ref[...])
    w = BRANCH_W
    negative = aux_ref[:, 1:2] > 0.5
    keep = aux_ref[:, 2:3]
    for o in range(HY_ORDER):
        fwd = taps[:, o * 2 * w:o * 2 * w + w]
        bwd = taps[:, o * 2 * w + w:(o + 1) * 2 * w]
        o_ref[:, o * w:(o + 1) * w] = jnp.where(negative, bwd, fwd) * keep


def hyena_filter(n, w1p, b1, w2, b2, w3, b3, w4, fr, layer):
    row = jnp.arange(2 * n)
    src = jnp.where(row <= n, row, 2 * n - row)
    pos = jnp.where(row == n, 0, src).astype(F32)
    tt = pos / max(n - 1, 1)
    ang = (2.0 * math.pi / n) * pos[:, None] * jnp.linspace(1e-4, HY_BANDS - 1, HY_BANDS, dtype=F32)
    z = jnp.concatenate([tt[:, None], jnp.cos(ang), -jnp.sin(ang), jnp.zeros((2 * n, LANE - HY_EMB), F32)], -1)
    aux = jnp.stack([tt, (row > n).astype(F32), (row != n).astype(F32)], 1)
    aux = jnp.pad(aux, ((0, 0), (0, SUB - 3)))
    deltas = jnp.abs(jnp.linspace(HY_MIN_DECAY, HY_MAX_DECAY, BRANCH_W, dtype=F32))
    dl = jnp.tile(deltas, 2 * HY_ORDER)[None, :]
    r = 512
    wo = 2 * HY_ORDER * BRANCH_W
    full = lambda shape: pl.BlockSpec((None,) + shape, lambda i: (layer,) + (0,) * len(shape))
    return pl.pallas_call(
        _hyfilt_body,
        grid=(2 * n // r,),
        in_specs=[pl.BlockSpec((r, LANE), lambda i: (i, 0)),
                  pl.BlockSpec((r, SUB), lambda i: (i, 0)),
                  full((LANE, HY_FH)), full((1, HY_FH)), full((HY_FH, HY_FH)), full((1, HY_FH)),
                  full((HY_FH, HY_FH)), full((1, HY_FH)), full((HY_FH, wo)), full((1, HY_FH)),
                  pl.BlockSpec((1, wo), lambda i: (0, 0))],
        out_specs=pl.BlockSpec((r, HY_ORDER * BRANCH_W), lambda i: (i, 0)),
        out_shape=jax.ShapeDtypeStruct((2 * n, HY_ORDER * BRANCH_W), F32),
        compiler_params=_cparams(("arbitrary",)),
        name="hyena_filter",
    )(z, aux, w1p, b1, w2, b2, w3, b3, w4, fr, dl)


@functools.lru_cache(maxsize=None)
def _dense_dft_tables(n):
    nn = 2 * n
    k = np.arange(nn)[:, None].astype(np.float64)
    m = np.arange(nn)[None, :].astype(np.float64)
    ang = -2.0 * np.pi * k * m / nn
    wr, wi = np.cos(ang), np.sin(ang)
    f_real = np.concatenate([wr, wi], 0)
    wr_h, wi_h = wr[:, :n], wi[:, :n]
    f_fwd = np.block([[wr_h, -wi_h], [wi_h, wr_h]])
    cr, ci = wr.T[:n] / nn, -wi.T[:n] / nn
    f_inv = np.block([[cr, -ci], [ci, cr]])
    return (np.asarray(f_real, np.float32), np.asarray(f_fwd, np.float32), np.asarray(f_inv, np.float32))


@functools.lru_cache(maxsize=None)
def _two_stage_dft_tables(n):
    nn = 2 * n
    n2c = FFT_N2
    n1c = nn // n2c
    n1h = n1c // 2
    k1 = np.arange(n1c).astype(np.float64)
    n1 = np.arange(n1c).astype(np.float64)
    n2 = np.arange(n2c).astype(np.float64)
    ang = -2.0 * np.pi * (k1[None, :, None] * n1[None, None, :] / n1c + n2[:, None, None] * k1[None, :, None] / nn)
    mr, mi = np.cos(ang), np.sin(ang)
    f1_real = np.concatenate([mr, mi], 1)
    mrh, mih = mr[:, :, :n1h], mi[:, :, :n1h]
    f1_cplx = np.concatenate([np.concatenate([mrh, -mih], 2), np.concatenate([mih, mrh], 2)], 1)
    gr = np.transpose(mr, (0, 2, 1))[:, :n1h, :] / nn
    gi = -np.transpose(mi, (0, 2, 1))[:, :n1h, :] / nn
    g1 = np.concatenate([np.concatenate([gr, -gi], 2), np.concatenate([gi, gr], 2)], 1)
    k2 = np.arange(n2c).astype(np.float64)
    a2 = -2.0 * np.pi * k2[:, None] * n2[None, :] / n2c
    fr, fi = np.cos(a2), np.sin(a2)
    f2 = np.block([[fr, -fi], [fi, fr]])
    f2i = np.block([[fr.T, fi.T], [-fi.T, fr.T]])
    f32 = lambda a: np.asarray(a, np.float32)
    return f32(f1_real), f32(f1_cplx), f32(g1), f32(f2), f32(f2i)


def _spec_dense_body(f_ref, x_ref, o_ref):
    o_ref[...] = _dot(f_ref[...], x_ref[...], hi=True)


def hyena_spec_dense(full, n):
    f_real, _, _ = _dense_dft_tables(n)
    nn, cols = full.shape
    return pl.pallas_call(
        _spec_dense_body,
        grid=(cols // LANE,),
        in_specs=[pl.BlockSpec((2 * nn, nn), lambda j: (0, 0)),
                  pl.BlockSpec((nn, LANE), lambda j: (0, j))],
        out_specs=pl.BlockSpec((2 * nn, LANE), lambda j: (0, j)),
        out_shape=jax.ShapeDtypeStruct((2 * nn, cols), F32),
        compiler_params=_cparams(("arbitrary",)),
        name="hyena_spec_dense",
    )(jnp.asarray(f_real), full)


def _conv_dense_body(*refs, has_mult):
    z_ref, h_ref, ff_ref, fi_ref, bias_ref = refs[:5]
    m_ref = refs[5] if has_mult else None
    o_ref = refs[-1]
    z = z_ref[...]
    nn = z.shape[0]
    x = _dot(ff_ref[...], z, hi=True)
    xr, xi = x[:nn], x[nn:]
    hr, hi_ = h_ref[0:nn, :], h_ref[nn:2 * nn, :]
    y = _dot(fi_ref[...], jnp.concatenate([xr * hr - xi * hi_, xr * hi_ + xi * hr], 0), hi=True)
    out = y + z * bias_ref[...]
    if has_mult:
        out = out * m_ref[...]
    o_ref[...] = out


def hyena_conv_dense(zsrc, zcol, row0, n, n_batch, spec, bias3, layer, order, prev, mult=None):
    _, f_fwd, f_inv = _dense_dft_tables(n)
    nn = 2 * n
    rb, cb = row0 // nn, zcol // LANE
    wb = BRANCH_W // LANE
    in_specs = [pl.BlockSpec((nn, LANE), lambda p, j: (rb + p, cb + j)),
                pl.BlockSpec((2 * nn, LANE), lambda p, j: (0, order * wb + j)),
                pl.BlockSpec((2 * nn, nn), lambda p, j: (0, 0)),
                pl.BlockSpec((nn, 2 * nn), lambda p, j: (0, 0)),
                pl.BlockSpec((None, 1, LANE), lambda p, j: (layer * HY_ORDER + order, 0, j))]
    args = [zsrc, spec, jnp.asarray(f_fwd), jnp.asarray(f_inv), bias3]
    if mult is not None:
        mb = mult[1] // LANE
        in_specs.append(pl.BlockSpec((nn, LANE), lambda p, j: (rb + p, mb + j)))
        args.append(mult[0])
    in_specs.append(pl.BlockSpec(memory_space=pl.ANY))
    args.append(prev)
    return pl.pallas_call(
        functools.partial(_conv_dense_body, has_mult=mult is not None),
        grid=(n_batch // 2, wb),
        in_specs=in_specs,
        out_specs=pl.BlockSpec((nn, LANE), lambda p, j: (rb + p, j)),
        out_shape=jax.ShapeDtypeStruct(prev.shape, F32),
        input_output_aliases={len(args) - 1: 0},
        compiler_params=_cparams(("arbitrary", "arbitrary")),
        name="hyena_conv_dense",
    )(*args)


def _spec_fft_body(x_ref, f1_ref, f2_ref, o_ref, a_ref):
    n1c = o_ref.shape[0]

    def stage1(g, carry):
        n2s = [g * FFT_UNROLL + u for u in range(FFT_UNROLL)]
        xs = [x_ref[pl.ds(n2, n1c, stride=FFT_N2), :] for n2 in n2s]
        res = [_dot(f1_ref[n2], x) for n2, x in zip(n2s, xs)]
        for n2, r in zip(n2s, res):
            a_ref[pl.ds(pl.multiple_of(n2 * 2 * n1c, 2 * n1c), 2 * n1c), :] = r
        return carry

    lax.fori_loop(0, FFT_N2 // FFT_UNROLL, stage1, 0, unroll=2)
    g2 = FFT_UNROLL // 2

    def stage2(g, carry):
        k1s = [g * g2 + u for u in range(g2)]
        blks = [jnp.concatenate([a_ref[pl.ds(k1, FFT_N2, stride=2 * n1c), :],
                                 a_ref[pl.ds(n1c + k1, FFT_N2, stride=2 * n1c), :]], 0) for k1 in k1s]
        res = [_dot(f2_ref[...], blk) for blk in blks]
        for k1, r in zip(k1s, res):
            o_ref[k1] = r
        return carry

    lax.fori_loop(0, n1c // g2, stage2, 0, unroll=2)


def hyena_spec_fft(full, n):
    f1_real, _, _, f2, _ = _two_stage_dft_tables(n)
    nn, cols = full.shape
    n1c = nn // FFT_N2
    const = lambda shape: pl.BlockSpec(shape, lambda j: (0,) * len(shape), pipeline_mode=pl.Buffered(1))
    return pl.pallas_call(
        _spec_fft_body,
        grid=(cols // LANE,),
        in_specs=[pl.BlockSpec((nn, LANE), lambda j: (0, j)),
                  const((FFT_N2, 2 * n1c, n1c)), const((2 * FFT_N2, 2 * FFT_N2))],
        out_specs=pl.BlockSpec((n1c, 2 * FFT_N2, LANE), lambda j: (0, 0, j)),
        out_shape=jax.ShapeDtypeStruct((n1c, 2 * FFT_N2, cols), F32),
        scratch_shapes=[pltpu.VMEM((FFT_N2 * 2 * n1c, LANE), F32)],
        compiler_params=_cparams(("arbitrary",)),
        name="hyena_spec_fft",
    )(full, jnp.asarray(f1_real, BF16), jnp.asarray(f2, BF16))


def _conv_fft_body(*refs, has_mult):
    z_ref, h_ref, f1_ref, f2_ref, f2i_ref, g1_ref, bias_ref = refs[:7]
    m_ref = refs[7] if has_mult else None
    o_ref, a_ref, b_ref = refs[-3], refs[-2], refs[-1]
    n1c = h_ref.shape[0]
    n1h = n1c // 2
    n2c = FFT_N2
    n = n1h * n2c

    def slab(n2):
        return pl.ds(pl.multiple_of(n2 * 2 * n1c, 2 * n1c), 2 * n1c)

    def stage1(g, carry):
        n2s = [g * FFT_UNROLL + u for u in range(FFT_UNROLL)]
        xs = [jnp.concatenate([z_ref[pl.ds(n2, n1h, stride=n2c), :], z_ref[pl.ds(n + n2, n1h, stride=n2c), :]], 0)
              for n2 in n2s]
        res = [_dot(f1_ref[n2], x) for n2, x in zip(n2s, xs)]
        for n2, r in zip(n2s, res):
            a_ref[slab(n2), :] = r
        return carry

    lax.fori_loop(0, n2c // FFT_UNROLL, stage1, 0, unroll=2)
    g2 = FFT_UNROLL // 2

    def stage2(g, carry):
        k1s = [g * g2 + u for u in range(g2)]
        rows = [(pl.ds(k1, n2c, stride=2 * n1c), pl.ds(n1c + k1, n2c, stride=2 * n1c)) for k1 in k1s]
        blks = [jnp.concatenate([a_ref[re, :], a_ref[im, :]], 0) for re, im in rows]
        xs = [_dot(f2_ref[...], blk) for blk in blks]
        ys = []
        for k1, x in zip(k1s, xs):
            xr, xi = x[:n2c], x[n2c:]
            hr, hi_ = h_ref[k1, 0:n2c, :], h_ref[k1, n2c:2 * n2c, :]
            ys.append(jnp.concatenate([xr * hr - xi * hi_, xr * hi_ + xi * hr], 0))
        bs = [_dot(f2i_ref[...], y) for y in ys]
        for (re, im), b in zip(rows, bs):
            b_ref[re, :] = b[:n2c]
            b_ref[im, :] = b[n2c:]
        return carry

    lax.fori_loop(0, n1c // g2, stage2, 0, unroll=2)
    bias = bias_ref[...]

    def stage3(g, carry):
        n2s = [g * FFT_UNROLL + u for u in range(FFT_UNROLL)]
        blks = [b_ref[slab(n2), :] for n2 in n2s]
        ys = [_dot(g1_ref[n2], blk) for n2, blk in zip(n2s, blks)]
        outs = []
        for n2, y in zip(n2s, ys):
            for part, rows in ((y[:n1h], pl.ds(n2, n1h, stride=n2c)), (y[n1h:], pl.ds(n + n2, n1h, stride=n2c))):
                out = part + z_ref[rows, :] * bias
                if has_mult:
                    out = out * m_ref[rows, :]
                outs.append((rows, out))
        for rows, out in outs:
            o_ref[rows, :] = out
        return carry

    lax.fori_loop(0, n2c // FFT_UNROLL, stage3, 0, unroll=2)


def hyena_conv_fft(zsrc, zcol, n, n_batch, spec, bias3, layer, order, t_rows, mult=None):
    _, f1_cplx, g1, f2, f2i = _two_stage_dft_tables(n)
    n1c = 2 * n // FFT_N2
    cb = zcol // LANE
    wb = BRANCH_W // LANE
    const = lambda shape: pl.BlockSpec(shape, lambda j, p: (0,) * len(shape), pipeline_mode=pl.Buffered(1))
    in_specs = [pl.BlockSpec((2 * n, LANE), lambda j, p: (p, cb + j)),
                pl.BlockSpec((n1c, 2 * FFT_N2, LANE), lambda j, p: (0, 0, order * wb + j),
                             pipeline_mode=pl.Buffered(1)),
                const((FFT_N2, 2 * n1c, n1c)), const((2 * FFT_N2, 2 * FFT_N2)), const((2 * FFT_N2, 2 * FFT_N2)),
                const((FFT_N2, n1c, 2 * n1c)),
                pl.BlockSpec((None, 1, LANE), lambda j, p: (layer * HY_ORDER + order, 0, j))]
    args = [zsrc, spec, jnp.asarray(f1_cplx, BF16), jnp.asarray(f2, BF16), jnp.asarray(f2i, BF16),
            jnp.asarray(g1, BF16), bias3]
    if mult is not None:
        mb = mult[1] // LANE
        in_specs.append(pl.BlockSpec((2 * n, LANE), lambda j, p: (p, mb + j)))
        args.append(mult[0])
    return pl.pallas_call(
        functools.partial(_conv_fft_body, has_mult=mult is not None),
        grid=(wb, n_batch // 2),
        in_specs=in_specs,
        out_specs=pl.BlockSpec((2 * n, LANE), lambda j, p: (p, j)),
        out_shape=jax.ShapeDtypeStruct((t_rows, BRANCH_W), F32),
        scratch_shapes=[pltpu.VMEM((FFT_N2 * 2 * n1c, LANE), F32)] * 2,
        compiler_params=_cparams(("arbitrary", "arbitrary")),
        name="hyena_conv_fft",
    )(*args)


def _swap_pairs(x):
    w = x.shape[-1]
    lane = lax.broadcasted_iota(jnp.int32, x.shape, x.ndim - 1)
    return jnp.where(lane % 2 == 0, pltpu.roll(x, w - 1, x.ndim - 1), pltpu.roll(x, 1, x.ndim - 1))


def _attn_prep_body(g_ref, dq_ref, dk_ref, dv_ref, cg_ref, sg_ref, cd_ref, sd_ref, qn_ref, kn_ref,
                    qg_ref, kg_ref, vg_ref, qd_ref, kd_ref, vd_ref, *, lat_blocks):
    is_lat = pl.program_id(0) < lat_blocks
    cg = jnp.where(is_lat, cg_ref[...], 1.0)
    sg = jnp.where(is_lat, sg_ref[...], 0.0)
    cd = jnp.where(is_lat, cd_ref[...], 1.0)
    sd = jnp.where(is_lat, sd_ref[...], 0.0)

    def rope(x, cs, sn):
        return x * cs + _swap_pairs(x) * sn

    def rms(x, w):
        return x * lax.rsqrt(jnp.mean(x * x, axis=-1, keepdims=True) + EPS) * w

    for h in range(HEADS):
        sl = slice(h * HEAD_D, (h + 1) * HEAD_D)
        q = rope(rms(g_ref[:, sl].astype(F32), qn_ref[...]), cg, sg)
        qg_ref[:, sl] = (q * HEAD_D ** -0.5).astype(BF16)
        qd_ref[:, sl] = (rope(dq_ref[:, sl].astype(F32), cd, sd) * DIFF_QK ** -0.5).astype(BF16)
        kd_ref[:, sl] = rope(dk_ref[:, sl].astype(F32), cd, sd).astype(BF16)
    for h in range(GQA_KV):
        sl = slice(h * HEAD_D, (h + 1) * HEAD_D)
        kin = g_ref[:, BRANCH_W + h * HEAD_D:BRANCH_W + (h + 1) * HEAD_D].astype(F32)
        kg_ref[:, sl] = rope(rms(kin, kn_ref[...]), cg, sg).astype(BF16)
    ones = jnp.ones((g_ref.shape[0], HEAD_D), BF16)
    for h in range(GQA_KV):
        src = BRANCH_W + (GQA_KV + h) * HEAD_D
        vg_ref[:, 2 * h * HEAD_D:(2 * h + 1) * HEAD_D] = g_ref[:, src:src + HEAD_D].astype(BF16)
        vg_ref[:, (2 * h + 1) * HEAD_D:(2 * h + 2) * HEAD_D] = ones
    for h in range(HEADS):
        vd_ref[:, 2 * h * HEAD_D:(2 * h + 1) * HEAD_D] = dv_ref[:, h * HEAD_D:(h + 1) * HEAD_D].astype(BF16)
        vd_ref[:, (2 * h + 1) * HEAD_D:(2 * h + 2) * HEAD_D] = ones


def attn_prep(p, ropes, qn3, kn3, layer, n_lat, n_ctx, n_batch):
    t = p.shape[0]
    r = 256 if n_ctx % 256 == 0 else n_ctx
    nlb, ncb = n_lat // r, n_ctx // r
    lat_blocks = n_batch * nlb
    kvw = GQA_KV * HEAD_D
    w = BRANCH_W

    def kv_row(i):
        lat = (i // nlb) * (nlb + ncb) + ncb + i % nlb
        j = i - lat_blocks
        ctx = (j // ncb) * (nlb + ncb) + j % ncb
        return jnp.where(i < lat_blocks, lat, ctx)

    rope_spec = pl.BlockSpec((r, LANE), lambda i: (jnp.where(i < lat_blocks, i % nlb, 0), 0))
    nkv = n_batch * (n_lat + n_ctx)
    return pl.pallas_call(
        functools.partial(_attn_prep_body, lat_blocks=lat_blocks),
        grid=(t // r,),
        in_specs=[pl.BlockSpec((r, 2 * w), lambda i: (i, C_GQA_QKV // (2 * w))),
                  pl.BlockSpec((r, w), lambda i: (i, C_DIFF_Q // w)),
                  pl.BlockSpec((r, w), lambda i: (i, C_DIFF_K // w)),
                  pl.BlockSpec((r, w), lambda i: (i, C_DIFF_V // w)),
                  rope_spec, rope_spec, rope_spec, rope_spec,
                  pl.BlockSpec((None, 1, LANE), lambda i: (layer, 0, 0)),
                  pl.BlockSpec((None, 1, LANE), lambda i: (layer, 0, 0))],
        out_specs=[pl.BlockSpec((r, w), lambda i: (i, 0)),
                   pl.BlockSpec((r, kvw), lambda i: (kv_row(i), 0)),
                   pl.BlockSpec((r, 2 * kvw), lambda i: (kv_row(i), 0)),
                   pl.BlockSpec((r, w), lambda i: (i, 0)),
                   pl.BlockSpec((r, w), lambda i: (kv_row(i), 0)),
                   pl.BlockSpec((r, 2 * w), lambda i: (kv_row(i), 0))],
        out_shape=[jax.ShapeDtypeStruct((t, w), BF16), jax.ShapeDtypeStruct((nkv, kvw), BF16),
                   jax.ShapeDtypeStruct((nkv, 2 * kvw), BF16), jax.ShapeDtypeStruct((t, w), BF16),
                   jax.ShapeDtypeStruct((nkv, w), BF16), jax.ShapeDtypeStruct((nkv, 2 * w), BF16)],
        compiler_params=_cparams(("arbitrary",)),
        name="attn_prep",
    )(p, p, p, p, *ropes, qn3, kn3)


def _softmax_pv(s, v_ext):
    e = jnp.exp((s - jnp.max(s, axis=-1, keepdims=True)).astype(BF16))
    acc = jnp.dot(e, v_ext, preferred_element_type=F32)
    return acc[:, :HEAD_D] / acc[:, HEAD_D:HEAD_D + 1]


def _gqa_body(q_ref, k_ref, v_ref, *rest):
    o_ref = rest[-1]
    group = HEADS // GQA_KV
    for kvh in range(GQA_KV):
        k = k_ref[:, kvh * HEAD_D:(kvh + 1) * HEAD_D]
        v_ext = v_ref[:, 2 * kvh * HEAD_D:(2 * kvh + 2) * HEAD_D]
        for g in range(group):
            sl = slice((kvh * group + g) * HEAD_D, (kvh * group + g + 1) * HEAD_D)
            s = lax.dot_general(q_ref[:, sl], k, (((1,), (1,)), ((), ())), preferred_element_type=F32)
            o_ref[:, sl] = _softmax_pv(s, v_ext)


def _diff_body(q_ref, k_ref, v_ref, lam_ref, *rest, lam_init):
    o_ref = rest[-1]
    lam4 = lam_ref[...]
    lam = (jnp.exp(jnp.sum(lam4[0:1] * lam4[1:2], axis=-1, keepdims=True))
           - jnp.exp(jnp.sum(lam4[2:3] * lam4[3:4], axis=-1, keepdims=True)) + lam_init)
    dn = (((1,), (1,)), ((), ()))
    for h in range(HEADS):
        sl = slice(h * HEAD_D, (h + 1) * HEAD_D)
        q = q_ref[:, sl]
        k = k_ref[:, sl]
        v_ext = v_ref[:, 2 * h * HEAD_D:(2 * h + 2) * HEAD_D]
        first = lax.broadcasted_iota(jnp.int32, q.shape, 1) < DIFF_QK
        zero = jnp.zeros_like(q)
        o1 = _softmax_pv(lax.dot_general(jnp.where(first, q, zero), k, dn, preferred_element_type=F32), v_ext)
        o2 = _softmax_pv(lax.dot_general(jnp.where(first, zero, q), k, dn, preferred_element_type=F32), v_ext)
        o_ref[:, sl] = o1 - lam * o2


def attention(body, q, k, v, extra, extra_specs, q_row0, nq, kv_per_batch, kv_len, n_batch, tq, name, prev=None):
    t, w = q.shape
    qb0 = q_row0 // tq
    nqb = nq // tq
    kvb = kv_per_batch // kv_len
    in_specs = [pl.BlockSpec((tq, w), lambda b, i: (qb0 + b * nqb + i, 0)),
                pl.BlockSpec((kv_len, k.shape[1]), lambda b, i: (b * kvb, 0)),
                pl.BlockSpec((kv_len, v.shape[1]), lambda b, i: (b * kvb, 0))] + extra_specs
    args = [q, k, v, *extra]
    aliases = {}
    if prev is not None:
        in_specs.append(pl.BlockSpec(memory_space=pl.ANY))
        args.append(prev)
        aliases = {len(args) - 1: 0}
    return pl.pallas_call(
        body,
        grid=(n_batch, nqb),
        in_specs=in_specs,
        out_specs=pl.BlockSpec((tq, w), lambda b, i: (qb0 + b * nqb + i, 0)),
        out_shape=jax.ShapeDtypeStruct((t, w), F32),
        input_output_aliases=aliases,
        compiler_params=_cparams(("arbitrary", "arbitrary")),
        name=name,
    )(*args)


def _merge_body(h_ref, mod_ref, mg_ref, of_ref, ob_ref, ggate_ref, y1_ref, x2_ref, hgate_ref, oc_ref, cgate_ref,
                od_ref, dgate_ref, gnorm_ref, dnorm_ref, wbr_ref, wout_ref, lng_ref, lnb_ref, o_ref, *, diff_scale):
    def rms_heads(x, w):
        parts = []
        for h in range(HEADS):
            xh = x[:, h * HEAD_D:(h + 1) * HEAD_D]
            parts.append(xh * lax.rsqrt(jnp.mean(xh * xh, axis=-1, keepdims=True) + EPS) * w)
        return jnp.concatenate(parts, -1)

    ys = (rms_heads(of_ref[...] + ob_ref[...], gnorm_ref[...]) * _silu(ggate_ref[...].astype(F32)),
          x2_ref[...] * y1_ref[...] * _silu(hgate_ref[...].astype(F32)),
          oc_ref[...] * _silu(cgate_ref[...].astype(F32)),
          rms_heads(od_ref[...], dnorm_ref[...]) * diff_scale * _silu(dgate_ref[...].astype(F32)))
    acc = None
    for n in range(N_BRANCH):
        proj = jnp.dot(ys[n].astype(BF16), wbr_ref[n], preferred_element_type=F32)
        term = _sigmoid(mg_ref[:, n * D_MODEL:(n + 1) * D_MODEL].astype(F32)) * proj
        acc = term if acc is None else acc + term
    out = jnp.dot(acc.astype(BF16), wout_ref[...], preferred_element_type=F32)
    x = ALPHA * h_ref[...] + mod_ref[2:3, :] * out
    mu = jnp.mean(x, axis=-1, keepdims=True)
    xc = x - mu
    var = jnp.mean(xc * xc, axis=-1, keepdims=True)
    o_ref[...] = xc * lax.rsqrt(var + EPS) * lng_ref[...] + lnb_ref[...]


def merge_postnorm(h_all, mod3, p, o_f, o_b, y1, xv, oc, od, gnorm3, dnorm3, wbr, wout, lng3, lnb3, layer, lam_init,
                   n_lat, n_batch):
    t, d = h_all.shape
    r = 256 if n_lat % 256 == 0 else 64
    w = BRANCH_W
    lbb = n_lat // r
    row = lambda i: jnp.minimum(i // lbb, n_batch)
    tok = lambda cb: pl.BlockSpec((r, w), lambda i: (i, cb))
    vec = lambda width: pl.BlockSpec((None, 1, width), lambda i: (layer, 0, 0))
    return pl.pallas_call(
        functools.partial(_merge_body, diff_scale=1.0 - lam_init),
        grid=(t // r,),
        in_specs=[pl.BlockSpec((r, d), lambda i: (i, 0)),
                  pl.BlockSpec((None, 3, d), lambda i: (row(i), 0, 0)),
                  pl.BlockSpec((r, N_BRANCH * d), lambda i: (i, C_MERGE // (N_BRANCH * d))),
                  tok(0), tok(0), tok(C_GDN_GATE // w), tok(0), tok(1), tok(C_HY_GATE // w), tok(0),
                  tok(C_GQA_GATE // w), tok(0), tok(C_DIFF_GATE // w),
                  vec(LANE), vec(LANE),
                  pl.BlockSpec((None, N_BRANCH, w, d), lambda i: (layer, 0, 0, 0)),
                  pl.BlockSpec((None, d, d), lambda i: (layer, 0, 0)),
                  vec(d), vec(d)],
        out_specs=pl.BlockSpec((r, d), lambda i: (i, 0)),
        out_shape=jax.ShapeDtypeStruct((t, d), F32),
        compiler_params=_cparams(("arbitrary",)),
        name="merge_postnorm",
    )(h_all, mod3, p, o_f, o_b, p, y1, xv, p, oc, p, od, p, gnorm3, dnorm3, wbr, wout, lng3, lnb3)


def _rope_tables(n_lat, dim):
    rows = n_lat // GRID_W
    row = jnp.repeat(jnp.arange(rows, dtype=F32), GRID_W)
    col = jnp.tile(jnp.arange(GRID_W, dtype=F32), rows)
    half = dim // 2
    inv = ROPE_THETA ** (-jnp.arange(0, half, 2, dtype=F32) / half)
    ang = jnp.concatenate([row[:, None] * inv, col[:, None] * inv], -1)
    cos = jnp.repeat(jnp.cos(ang), 2, axis=-1)
    sin = jnp.repeat(jnp.sin(ang), 2, axis=-1)
    sign = jnp.tile(jnp.array([-1.0, 1.0], F32), dim // 2)
    reps = LANE // dim
    return jnp.tile(cos, (1, reps)), jnp.tile(sin * sign, (1, reps))


def kernel(x, c, ctx, c_ctx, w_ada, b_ada, w_in, gdn_conv, gdn_a_log, gdn_dt_bias, gdn_norm, hy_conv, hy_w1, hy_b1,
           hy_w2, hy_b2, hy_w3, hy_b3, hy_w4, hy_freq, hy_bias, gqa_qn, gqa_kn, diff_lam, diff_norm, w_br, w_out,
           ln_g, ln_b):
    nb, n_lat, d = x.shape
    n_ctx = ctx.shape[1]
    t_lat, t_ctx = nb * n_lat, nb * n_ctx
    depth = w_in.shape[0]
    w = BRANCH_W

    w_main = jnp.concatenate([w_in[:, :, O_MERGE:], w_in[:, :, :O_GDN_AB], w_in[:, :, O_GDN_AB + 4 * HEADS:O_MERGE]],
                             axis=2).astype(BF16)
    w_ab = jnp.pad(w_in[:, :, O_GDN_AB:O_GDN_AB + 4 * HEADS], ((0, 0), (0, 0), (0, LANE - 4 * HEADS)))
    wbr_bf = w_br.astype(BF16)
    wout_bf = w_out.astype(BF16)
    b_ada3 = b_ada[:, None, :]
    cvec = jnp.concatenate([c, c_ctx[None, :], jnp.zeros((SUB - nb - 1, d), F32)], 0)
    as3 = lambda a: a[:, None, :]
    gdn_par_r = jnp.pad(jnp.stack([gdn_a_log.reshape(depth, -1), gdn_dt_bias.reshape(depth, -1)], 1),
                        ((0, 0), (0, SUB - 2), (0, LANE - 2 * HEADS)))
    gdn_par_c = jnp.pad(jnp.stack([gdn_a_log.reshape(depth, -1), gdn_dt_bias.reshape(depth, -1)], 2),
                        ((0, 0), (0, 2 * HEADS), (0, LANE - 2)))
    hy_w1p = jnp.pad(hy_w1, ((0, 0), (0, LANE - HY_EMB), (0, 0)))
    hy_bias3 = hy_bias.reshape(depth * HY_ORDER, 1, w)
    ropes = _rope_tables(n_lat, HEAD_D) + _rope_tables(n_lat, DIFF_QK)

    tm = 1024 if (n_lat % 1024 == 0 and t_ctx % 1024 == 0) else n_ctx
    h_all = jnp.concatenate([x.reshape(t_lat, d), ctx.reshape(t_ctx, d)], 0)
    for l in range(depth):
        lam_init = 0.8 - 0.6 * math.exp(-0.3 * l)
        mod3 = ada_mod(cvec, w_ada, b_ada3, l).reshape(SUB, 3, d)
        p, ab = in_proj(h_all, mod3, w_main, w_ab, l, tm, n_lat // tm, nb)

        qkv = dwconv(p, gdn_conv, l, C_GDN_QKV, 3 * w, n_lat, n_ctx, nb, act=True)
        ab_rows = jnp.transpose(ab[:, :4 * HEADS].reshape(-1, GDN_CHUNK, 4 * HEADS), (0, 2, 1))
        o_f, o_b = gdn_scan(qkv, ab, ab_rows, gdn_par_r[l], gdn_par_c[l], n_lat, n_ctx, nb)

        xv = dwconv(p, hy_conv, l, C_HY_XV, 3 * w, n_lat, n_ctx, nb, act=False)
        filt = lambda n: hyena_filter(n, hy_w1p, as3(hy_b1), hy_w2, as3(hy_b2), hy_w3, as3(hy_b3), hy_w4,
                                      as3(hy_freq), l)
        spec_lat = hyena_spec_fft(filt(n_lat), n_lat)
        spec_ctx = hyena_spec_dense(filt(n_ctx), n_ctx)
        z1 = hyena_conv_fft(xv, 2 * w, n_lat, nb, spec_lat, hy_bias3, l, 0, t_lat + t_ctx, mult=(xv, 0))
        z1 = hyena_conv_dense(xv, 2 * w, t_lat, n_ctx, nb, spec_ctx, hy_bias3, l, 0, z1, mult=(xv, 0))
        y1 = hyena_conv_fft(z1, 0, n_lat, nb, spec_lat, hy_bias3, l, 1, t_lat + t_ctx)
        y1 = hyena_conv_dense(z1, 0, t_lat, n_ctx, nb, spec_ctx, hy_bias3, l, 1, y1)

        qg, kg, vg, qd, kd, vd = attn_prep(p, ropes, as3(gqa_qn), as3(gqa_kn), l, n_lat, n_ctx, nb)
        kv_all = n_lat + n_ctx
        tq = min(256, n_ctx)
        lam_spec = [pl.BlockSpec((None, 4, DIFF_QK), lambda b, i: (l, 0, 0))]
        diff_body = functools.partial(_diff_body, lam_init=lam_init)
        oc = attention(_gqa_body, qg, kg, vg, (), [], 0, n_lat, kv_all, kv_all, nb, tq, "gqa_lat")
        oc = attention(_gqa_body, qg, kg, vg, (), [], t_lat, n_ctx, kv_all, n_ctx, nb, tq, "gqa_ctx", prev=oc)
        od = attention(diff_body, qd, kd, vd, (diff_lam,), lam_spec, 0, n_lat, kv_all, kv_all, nb, tq, "diff_lat")
        od = attention(diff_body, qd, kd, vd, (diff_lam,), lam_spec, t_lat, n_ctx, kv_all, n_ctx, nb, tq, "diff_ctx",
                       prev=od)

        h_all = merge_postnorm(h_all, mod3, p, o_f, o_b, y1, xv, oc, od, as3(gdn_norm), as3(diff_norm), wbr_bf, wout_bf,
                               as3(ln_g), as3(ln_b), l, lam_init, n_lat, nb)
    return h_all[:t_lat].reshape(nb, n_lat, d)
```

```python
import functools
import math

import numpy as np
import jax
import jax.numpy as jnp
from jax import lax
from jax.experimental import pallas as pl
from jax.experimental.pallas import tpu as pltpu

F32 = jnp.float32
BF16 = jnp.bfloat16
HI = lax.Precision.HIGHEST

D_MODEL = 1024
DEPTH = 4
GRID_W = 64
BRANCH_W = D_MODEL // 2
N_BRANCH = 4
HEADS = 4
HEAD_D = BRANCH_W // HEADS
GDN_CONV = 4
GDN_CHUNK = 64
GDN_SUB = 4
HY_CONV = 3
HY_EMB = 33
HY_BANDS = (HY_EMB - 1) // 2
HY_FH = 64
HY_ORDER = 2
HY_MIN_DECAY = math.log(1e-2) / 1.5
HY_MAX_DECAY = math.log(1e-2) / 0.3
GQA_KV = 2
DIFF_QK = HEAD_D // 2
ROPE_THETA = 10000.0
EPS = 1e-6
ALPHA = (2.0 * DEPTH) ** 0.25

LANE = 128
SUB = 8
HALO = 16
FFT_N2 = 128
FFT_UNROLL = 8
VMEM_LIMIT = 60 * 1024 * 1024

C_MERGE = 0
C_GDN_QKV = 4096
C_GDN_GATE = 5632
C_HY_XV = 6144
C_HY_GATE = 7680
C_GQA_QKV = 8192
C_GQA_GATE = 9216
C_DIFF_Q = 9728
C_DIFF_K = 10240
C_DIFF_V = 10752
C_DIFF_GATE = 11264
N_MAIN = 11776
O_GDN_AB = 1536
O_MERGE = 7696


def _cparams(sem):
    return pltpu.CompilerParams(dimension_semantics=sem, vmem_limit_bytes=VMEM_LIMIT)


def _dot(a, b, hi=False):
    if hi:
        return jnp.dot(a, b, precision=HI, preferred_element_type=F32)
    return jnp.dot(a.astype(BF16), b.astype(BF16), preferred_element_type=F32)


def _dot_nt(a, b, hi=False):
    dn = (((1,), (1,)), ((), ()))
    if hi:
        return lax.dot_general(a, b, dn, precision=HI, preferred_element_type=F32)
    return lax.dot_general(a.astype(BF16), b.astype(BF16), dn, preferred_element_type=F32)


def _dot_tn(a, b):
    return lax.dot_general(a.astype(BF16), b.astype(BF16), (((0,), (0,)), ((), ())), preferred_element_type=F32)


def _sigmoid(x):
    return 1.0 / (1.0 + jnp.exp(-x))


def _silu(x):
    return x * _sigmoid(x)


def _softplus(x):
    return jnp.maximum(x, 0.0) + jnp.log1p(jnp.exp(-jnp.abs(x)))


def _ada_body(c_ref, w_ref, b_ref, o_ref):
    o_ref[...] = _dot(_silu(c_ref[...]), w_ref[...], hi=True) + b_ref[...]


def ada_mod(cvec, w_ada, b_ada3, layer):
    d = cvec.shape[1]
    tn = 512
    return pl.pallas_call(
        _ada_body,
        grid=(3 * d // tn,),
        in_specs=[pl.BlockSpec((SUB, d), lambda j: (0, 0)),
                  pl.BlockSpec((None, d, tn), lambda j: (layer, 0, j)),
                  pl.BlockSpec((None, 1, tn), lambda j: (layer, 0, j))],
        out_specs=pl.BlockSpec((SUB, tn), lambda j: (0, j)),
        out_shape=jax.ShapeDtypeStruct((SUB, 3 * d), F32),
        compiler_params=_cparams(("arbitrary",)),
        name="ada_mod",
    )(cvec, w_ada, b_ada3)


def _inproj_body(h_ref, mod_ref, w_ref, wab_ref, o_ref, ab_ref, u_ref):
    @pl.when(pl.program_id(1) == 0)
    def _():
        x = h_ref[...]
        mu = jnp.mean(x, axis=-1, keepdims=True)
        xc = x - mu
        var = jnp.mean(xc * xc, axis=-1, keepdims=True)
        u = xc * lax.rsqrt(var + EPS) * (1.0 + mod_ref[1:2, :]) + mod_ref[0:1, :]
        u_hi = u.astype(BF16)
        u_ref[...] = u_hi
        u_lo = (u - u_hi.astype(F32)).astype(BF16)
        w_ab = wab_ref[...]
        w_hi = w_ab.astype(BF16)
        w_lo = (w_ab - w_hi.astype(F32)).astype(BF16)
        ab_ref[...] = (jnp.dot(u_hi, w_hi, preferred_element_type=F32) + jnp.dot(u_hi, w_lo, preferred_element_type=F32)
                       + jnp.dot(u_lo, w_hi, preferred_element_type=F32))

    o_ref[...] = jnp.dot(u_ref[...], w_ref[...], preferred_element_type=F32).astype(BF16)


def in_proj(h_all, mod3, w_main, w_ab, layer, tm, lat_blocks_per_batch, n_batch):
    t, d = h_all.shape
    n_main = w_main.shape[2]
    tn = n_main // 4
    row = lambda i: jnp.minimum(i // lat_blocks_per_batch, n_batch)
    return pl.pallas_call(
        _inproj_body,
        grid=(t // tm, n_main // tn),
        in_specs=[pl.BlockSpec((tm, d), lambda i, j: (i, 0)),
                  pl.BlockSpec((None, 3, d), lambda i, j: (row(i), 0, 0)),
                  pl.BlockSpec((None, d, tn), lambda i, j: (layer, 0, j)),
                  pl.BlockSpec((None, d, LANE), lambda i, j: (layer, 0, 0))],
        out_specs=[pl.BlockSpec((tm, tn), lambda i, j: (i, j)),
                   pl.BlockSpec((tm, LANE), lambda i, j: (i, 0))],
        out_shape=[jax.ShapeDtypeStruct((t, n_main), BF16), jax.ShapeDtypeStruct((t, LANE), F32)],
        scratch_shapes=[pltpu.VMEM((tm, d), BF16)],
        compiler_params=_cparams(("arbitrary", "arbitrary")),
        name="in_proj",
    )(h_all, mod3, w_main, w_ab)


def _dwconv_body(xp_ref, x_ref, xn_ref, w_ref, o_ref, pad_ref, *, taps, pad_l, t_lat, n_lat, n_ctx, sb, act):
    i = pl.program_id(0)
    r = x_ref.shape[0]
    pad_ref[0:HALO, :] = xp_ref[...].astype(F32)
    pad_ref[HALO:HALO + r, :] = x_ref[...].astype(F32)
    pad_ref[HALO + r:2 * HALO + r, :] = xn_ref[...].astype(F32)
    row = lax.broadcasted_iota(jnp.int32, (sb, 1), 0)
    for k in range(r // sb):
        g0 = i * r + k * sb
        in_lat = g0 < t_lat
        starts = jnp.where(in_lat, g0 % n_lat == 0, (g0 - t_lat) % n_ctx == 0)
        ends = jnp.where(in_lat, (g0 + sb) % n_lat == 0, (g0 + sb - t_lat) % n_ctx == 0)
        acc = None
        for j in range(taps):
            d = j - pad_l
            off = HALO + k * sb + d
            xs = pad_ref[off:off + sb, :]
            if d < 0:
                xs = jnp.where(jnp.logical_and(starts, row < -d), 0.0, xs)
            elif d > 0:
                xs = jnp.where(jnp.logical_and(ends, row >= sb - d), 0.0, xs)
            term = w_ref[j:j + 1, :] * xs
            acc = term if acc is None else acc + term
        if act:
            acc = _silu(acc)
        o_ref[k * sb:(k + 1) * sb, :] = acc


def dwconv(p, w_conv, layer, col0, width, n_lat, n_ctx, n_batch, act):
    t = p.shape[0]
    taps = w_conv.shape[1]
    sb = min(256, n_ctx)
    r = 1024 if t % 1024 == 0 else sb
    lw = 512
    cb = col0 // lw
    rs = r // HALO
    body = functools.partial(_dwconv_body, taps=taps, pad_l=(taps - 1) // 2, t_lat=n_batch * n_lat, n_lat=n_lat,
                             n_ctx=n_ctx, sb=sb, act=act)
    return pl.pallas_call(
        body,
        grid=(t // r, width // lw),
        in_specs=[pl.BlockSpec((HALO, lw), lambda i, j: (jnp.maximum(i * rs - 1, 0), cb + j)),
                  pl.BlockSpec((r, lw), lambda i, j: (i, cb + j)),
                  pl.BlockSpec((HALO, lw), lambda i, j: (jnp.minimum((i + 1) * rs, t // HALO - 1), cb + j)),
                  pl.BlockSpec((None, taps, lw), lambda i, j: (layer, 0, j))],
        out_specs=pl.BlockSpec((r, lw), lambda i, j: (i, j)),
        out_shape=jax.ShapeDtypeStruct((t, width), F32),
        scratch_shapes=[pltpu.VMEM((r + 2 * HALO, lw), F32)],
        compiler_params=_cparams(("arbitrary", "arbitrary")),
        name="dwconv",
    )(p, p, p, w_conv)


def _gdn_body(qf_ref, qb_ref, abcf_ref, abcb_ref, abrf_ref, abrb_ref, pr_ref, pc_ref, of_ref, ob_ref, s_ref):
    c = GDN_CHUNK

    @pl.when(pl.program_id(1) == 0)
    def _():
        s_ref[...] = jnp.zeros_like(s_ref)

    ii = lax.broadcasted_iota(jnp.int32, (c, c), 0)
    jj = lax.broadcasted_iota(jnp.int32, (c, c), 1)
    lmat = (jj <= ii).astype(F32)
    eye = (jj == ii).astype(F32)
    alr, dtr = pr_ref[0:1, :], pr_ref[1:2, :]
    alc, dtc = pc_ref[:, 0:1], pc_ref[:, 1:2]
    chains = []
    for d in range(2):
        qkv_ref = (qf_ref, qb_ref)[d]
        abc_ref = (abcf_ref, abcb_ref)[d]
        abr_ref = (abrf_ref, abrb_ref)[d]
        incl = (jj <= ii) if d == 0 else (jj >= ii)
        strict = (jj < ii) if d == 0 else (jj > ii)
        for j in range(GDN_SUB):
            rows = slice(j * c, (j + 1) * c)
            abc = abc_ref[rows, :]
            abr = abr_ref[j]
            g_c = -jnp.exp(alr) * _softplus(abc + dtr)
            g_r = -jnp.exp(alc) * _softplus(abr + dtc)
            cum_c = _dot(lmat, g_c, hi=True)
            cum_r = _dot_nt(g_r, lmat, hi=True)
            if d == 1:
                cum_c = cum_c[c - 1:c, :] - cum_c + g_c
                cum_r = cum_r[:, c - 1:c] - cum_r + g_r
            beta_all = _sigmoid(abc)
            for h in range(HEADS):
                idx = HEADS * d + h
                q = qkv_ref[rows, h * HEAD_D:(h + 1) * HEAD_D]
                k = qkv_ref[rows, BRANCH_W + h * HEAD_D:BRANCH_W + (h + 1) * HEAD_D]
                v = qkv_ref[rows, 2 * BRANCH_W + h * HEAD_D:2 * BRANCH_W + (h + 1) * HEAD_D]
                q = q * lax.rsqrt(jnp.sum(q * q, axis=-1, keepdims=True) + EPS) * (HEAD_D ** -0.5)
                k = k * lax.rsqrt(jnp.sum(k * k, axis=-1, keepdims=True) + EPS)
                cc = cum_c[:, idx:idx + 1]
                cr = cum_r[idx:idx + 1, :]
                dec = jnp.exp(jnp.where(incl, cc - cr, -1e30))
                beta = beta_all[:, 2 * HEADS + idx:2 * HEADS + idx + 1]
                ecum = jnp.exp(cc)
                tot = cc[c - 1:c, :] if d == 0 else cc[0:1, :]
                chains.append(dict(d=d, h=h, j=j, rows=rows, q=q, k=k, dec=dec, strict=strict, beta=beta, ecum=ecum,
                                   tot=tot, rhs=jnp.concatenate([k * (beta * ecum), v * beta], 1),
                                   k_tail=k * jnp.exp(tot - cc)))
    for ch in chains:
        ch["kk"] = _dot_nt(ch["k"], ch["k"])
        ch["qk"] = _dot_nt(ch["q"], ch["k"])
    for ch in chains:
        ch["p"] = -jnp.where(ch["strict"], ch["beta"] * ch["kk"] * ch["dec"], 0.0)
        ch["inv"] = eye + ch["p"]
    for _ in range(int(math.log2(c)) - 1):
        for ch in chains:
            ch["p"] = _dot(ch["p"], ch["p"])
        for ch in chains:
            ch["inv"] = ch["inv"] + _dot(ch["inv"], ch["p"])
    for ch in chains:
        ch["wu"] = _dot(ch["inv"], ch["rhs"])
        ch["lhs"] = jnp.concatenate([ch["wu"][:, :HEAD_D], ch["q"] * ch["ecum"]], 0)
    state = {(d, h): s_ref[d, h] for d in range(2) for h in range(HEADS)}
    for step in range(GDN_SUB):
        cur = [ch for ch in chains if ch["j"] == (step if ch["d"] == 0 else GDN_SUB - 1 - step)]
        for ch in cur:
            ch["ws"] = _dot(ch["lhs"], state[ch["d"], ch["h"]])
        for ch in cur:
            ch["v_new"] = ch["wu"][:, HEAD_D:] - ch["ws"][:c]
        for ch in cur:
            out_ref = (of_ref, ob_ref)[ch["d"]]
            h = ch["h"]
            out_ref[ch["rows"], h * HEAD_D:(h + 1) * HEAD_D] = ch["ws"][c:] + _dot(ch["qk"] * ch["dec"], ch["v_new"])
            state[ch["d"], h] = state[ch["d"], h] * jnp.exp(ch["tot"]) + _dot_tn(ch["k_tail"], ch["v_new"])
    for (d, h), val in state.items():
        s_ref[d, h] = val


def gdn_scan(qkv, ab, ab_rows, par_r, par_c, n_lat, n_ctx, n_batch):
    t = qkv.shape[0]
    c = GDN_SUB * GDN_CHUNK
    nlc, ncc = n_lat // c, n_ctx // c
    base = n_batch * nlc

    def fwd(b, s):
        return jnp.where(s < ncc, base + b * ncc + s, b * nlc + (s - ncc))

    def bwd(b, s):
        return jnp.where(s < ncc, base + b * ncc + (ncc - 1 - s), b * nlc + (nlc - 1 - (s - ncc)))

    w3 = 3 * BRANCH_W
    return pl.pallas_call(
        _gdn_body,
        grid=(n_batch, ncc + nlc),
        in_specs=[pl.BlockSpec((c, w3), lambda b, s: (fwd(b, s), 0)),
                  pl.BlockSpec((c, w3), lambda b, s: (bwd(b, s), 0)),
                  pl.BlockSpec((c, LANE), lambda b, s: (fwd(b, s), 0)),
                  pl.BlockSpec((c, LANE), lambda b, s: (bwd(b, s), 0)),
                  pl.BlockSpec((GDN_SUB, 4 * HEADS, GDN_CHUNK), lambda b, s: (fwd(b, s), 0, 0)),
                  pl.BlockSpec((GDN_SUB, 4 * HEADS, GDN_CHUNK), lambda b, s: (bwd(b, s), 0, 0)),
                  pl.BlockSpec((SUB, LANE), lambda b, s: (0, 0)),
                  pl.BlockSpec((4 * HEADS, LANE), lambda b, s: (0, 0))],
        out_specs=[pl.BlockSpec((c, BRANCH_W), lambda b, s: (fwd(b, s), 0)),
                   pl.BlockSpec((c, BRANCH_W), lambda b, s: (bwd(b, s), 0))],
        out_shape=[jax.ShapeDtypeStruct((t, BRANCH_W), F32), jax.ShapeDtypeStruct((t, BRANCH_W), F32)],
        scratch_shapes=[pltpu.VMEM((2, HEADS, HEAD_D, HEAD_D), F32)],
        compiler_params=_cparams(("arbitrary", "arbitrary")),
        name="gdn_scan",
    )(qkv, qkv, ab, ab, ab_rows, ab_rows, par_r, par_c)


def _hyfilt_body(z_ref, aux_ref, w1_ref, b1_ref, w2_ref, b2_ref, w3_ref, b3_ref, w4_ref, fr_ref, dl_ref, o_ref):
    fr = fr_ref[...]
    h = jnp.sin(fr * (_dot(z_ref[...], w1_ref[...], hi=True) + b1_ref[...]))
    h = jnp.sin(fr * (_dot(h, w2_ref[...], hi=True) + b2_ref[...]))
    h = jnp.sin(fr * (_dot(h, w3_ref[...], hi=True) + b3_ref[...]))
    taps = _dot(h, w4_ref[...], hi=True) * jnp.exp(-aux_ref[:, 0:1] * dl_ref[...])
    w = BRANCH_W
    negative = aux_ref[:, 1:2] > 0.5
    keep = aux_ref[:, 2:3]
    for o in range(HY_ORDER):
        fwd = taps[:, o * 2 * w:o * 2 * w + w]
        bwd = taps[:, o * 2 * w + w:(o + 1) * 2 * w]
        o_ref[:, o * w:(o + 1) * w] = jnp.where(negative, bwd, fwd) * keep


def hyena_filter(n, w1p, b1, w2, b2, w3, b3, w4, fr, layer):
    row = jnp.arange(2 * n)
    src = jnp.where(row <= n, row, 2 * n - row)
    pos = jnp.where(row == n, 0, src).astype(F32)
    tt = pos / max(n - 1, 1)
    ang = (2.0 * math.pi / n) * pos[:, None] * jnp.linspace(1e-4, HY_BANDS - 1, HY_BANDS, dtype=F32)
    z = jnp.concatenate([tt[:, None], jnp.cos(ang), -jnp.sin(ang), jnp.zeros((2 * n, LANE - HY_EMB), F32)], -1)
    aux = jnp.stack([tt, (row > n).astype(F32), (row != n).astype(F32)], 1)
    aux = jnp.pad(aux, ((0, 0), (0, SUB - 3)))
    deltas = jnp.abs(jnp.linspace(HY_MIN_DECAY, HY_MAX_DECAY, BRANCH_W, dtype=F32))
    dl = jnp.tile(deltas, 2 * HY_ORDER)[None, :]
    r = 512
    wo = 2 * HY_ORDER * BRANCH_W
    full = lambda shape: pl.BlockSpec((None,) + shape, lambda i: (layer,) + (0,) * len(shape))
    return pl.pallas_call(
        _hyfilt_body,
        grid=(2 * n // r,),
        in_specs=[pl.BlockSpec((r, LANE), lambda i: (i, 0)),
                  pl.BlockSpec((r, SUB), lambda i: (i, 0)),
                  full((LANE, HY_FH)), full((1, HY_FH)), full((HY_FH, HY_FH)), full((1, HY_FH)),
                  full((HY_FH, HY_FH)), full((1, HY_FH)), full((HY_FH, wo)), full((1, HY_FH)),
                  pl.BlockSpec((1, wo), lambda i: (0, 0))],
        out_specs=pl.BlockSpec((r, HY_ORDER * BRANCH_W), lambda i: (i, 0)),
        out_shape=jax.ShapeDtypeStruct((2 * n, HY_ORDER * BRANCH_W), F32),
        compiler_params=_cparams(("arbitrary",)),
        name="hyena_filter",
    )(z, aux, w1p, b1, w2, b2, w3, b3, w4, fr, dl)


@functools.lru_cache(maxsize=None)
def _dense_dft_tables(n):
    nn = 2 * n
    k = np.arange(nn)[:, None].astype(np.float64)
    m = np.arange(nn)[None, :].astype(np.float64)
    ang = -2.0 * np.pi * k * m / nn
    wr, wi = np.cos(ang), np.sin(ang)
    f_real = np.concatenate([wr, wi], 0)
    wr_h, wi_h = wr[:, :n], wi[:, :n]
    f_fwd = np.block([[wr_h, -wi_h], [wi_h, wr_h]])
    cr, ci = wr.T[:n] / nn, -wi.T[:n] / nn
    f_inv = np.block([[cr, -ci], [ci, cr]])
    return (np.asarray(f_real, np.float32), np.asarray(f_fwd, np.float32), np.asarray(f_inv, np.float32))


@functools.lru_cache(maxsize=None)
def _two_stage_dft_tables(n):
    nn = 2 * n
    n2c = FFT_N2
    n1c = nn // n2c
    n1h = n1c // 2
    k1 = np.arange(n1c).astype(np.float64)
    n1 = np.arange(n1c).astype(np.float64)
    n2 = np.arange(n2c).astype(np.float64)
    ang = -2.0 * np.pi * (k1[None, :, None] * n1[None, None, :] / n1c + n2[:, None, None] * k1[None, :, None] / nn)
    mr, mi = np.cos(ang), np.sin(ang)
    f1_real = np.concatenate([mr, mi], 1)
    mrh, mih = mr[:, :, :n1h], mi[:, :, :n1h]
    f1_cplx = np.concatenate([np.concatenate([mrh, -mih], 2), np.concatenate([mih, mrh], 2)], 1)
    gr = np.transpose(mr, (0, 2, 1))[:, :n1h, :] / nn
    gi = -np.transpose(mi, (0, 2, 1))[:, :n1h, :] / nn
    g1 = np.concatenate([np.concatenate([gr, -gi], 2), np.concatenate([gi, gr], 2)], 1)
    k2 = np.arange(n2c).astype(np.float64)
    a2 = -2.0 * np.pi * k2[:, None] * n2[None, :] / n2c
    fr, fi = np.cos(a2), np.sin(a2)
    f2 = np.block([[fr, -fi], [fi, fr]])
    f2i = np.block([[fr.T, fi.T], [-fi.T, fr.T]])
    f32 = lambda a: np.asarray(a, np.float32)
    return f32(f1_real), f32(f1_cplx), f32(g1), f32(f2), f32(f2i)


def _spec_dense_body(f_ref, x_ref, o_ref):
    o_ref[...] = _dot(f_ref[...], x_ref[...], hi=True)


def hyena_spec_dense(full, n):
    f_real, _, _ = _dense_dft_tables(n)
    nn, cols = full.shape
    return pl.pallas_call(
        _spec_dense_body,
        grid=(cols // LANE,),
        in_specs=[pl.BlockSpec((2 * nn, nn), lambda j: (0, 0)),
                  pl.BlockSpec((nn, LANE), lambda j: (0, j))],
        out_specs=pl.BlockSpec((2 * nn, LANE), lambda j: (0, j)),
        out_shape=jax.ShapeDtypeStruct((2 * nn, cols), F32),
        compiler_params=_cparams(("arbitrary",)),
        name="hyena_spec_dense",
    )(jnp.asarray(f_real), full)


def _conv_dense_body(*refs, has_mult):
    z_ref, h_ref, ff_ref, fi_ref, bias_ref = refs[:5]
    m_ref = refs[5] if has_mult else None
    o_ref = refs[-1]
    z = z_ref[...]
    nn = z.shape[0]
    x = _dot(ff_ref[...], z, hi=True)
    xr, xi = x[:nn], x[nn:]
    hr, hi_ = h_ref[0:nn, :], h_ref[nn:2 * nn, :]
    y = _dot(fi_ref[...], jnp.concatenate([xr * hr - xi * hi_, xr * hi_ + xi * hr], 0), hi=True)
    out = y + z * bias_ref[...]
    if has_mult:
        out = out * m_ref[...]
    o_ref[...] = out


def hyena_conv_dense(zsrc, zcol, row0, n, n_batch, spec, bias3, layer, order, prev, mult=None):
    _, f_fwd, f_inv = _dense_dft_tables(n)
    nn = 2 * n
    rb, cb = row0 // nn, zcol // LANE
    wb = BRANCH_W // LANE
    in_specs = [pl.BlockSpec((nn, LANE), lambda p, j: (rb + p, cb + j)),
                pl.BlockSpec((2 * nn, LANE), lambda p, j: (0, order * wb + j)),
                pl.BlockSpec((2 * nn, nn), lambda p, j: (0, 0)),
                pl.BlockSpec((nn, 2 * nn), lambda p, j: (0, 0)),
                pl.BlockSpec((None, 1, LANE), lambda p, j: (layer * HY_ORDER + order, 0, j))]
    args = [zsrc, spec, jnp.asarray(f_fwd), jnp.asarray(f_inv), bias3]
    if mult is not None:
        mb = mult[1] // LANE
        in_specs.append(pl.BlockSpec((nn, LANE), lambda p, j: (rb + p, mb + j)))
        args.append(mult[0])
    in_specs.append(pl.BlockSpec(memory_space=pl.ANY))
    args.append(prev)
    return pl.pallas_call(
        functools.partial(_conv_dense_body, has_mult=mult is not None),
        grid=(n_batch // 2, wb),
        in_specs=in_specs,
        out_specs=pl.BlockSpec((nn, LANE), lambda p, j: (rb + p, j)),
        out_shape=jax.ShapeDtypeStruct(prev.shape, F32),
        input_output_aliases={len(args) - 1: 0},
        compiler_params=_cparams(("arbitrary", "arbitrary")),
        name="hyena_conv_dense",
    )(*args)


def _spec_fft_body(x_ref, f1_ref, f2_ref, o_ref, a_ref):
    n1c = o_ref.shape[0]

    def stage1(g, carry):
        n2s = [g * FFT_UNROLL + u for u in range(FFT_UNROLL)]
        xs = [x_ref[pl.ds(n2, n1c, stride=FFT_N2), :] for n2 in n2s]
        res = [_dot(f1_ref[n2], x) for n2, x in zip(n2s, xs)]
        for n2, r in zip(n2s, res):
            a_ref[pl.ds(pl.multiple_of(n2 * 2 * n1c, 2 * n1c), 2 * n1c), :] = r
        return carry

    lax.fori_loop(0, FFT_N2 // FFT_UNROLL, stage1, 0, unroll=2)
    g2 = FFT_UNROLL // 2

    def stage2(g, carry):
        k1s = [g * g2 + u for u in range(g2)]
        blks = [jnp.concatenate([a_ref[pl.ds(k1, FFT_N2, stride=2 * n1c), :],
                                 a_ref[pl.ds(n1c + k1, FFT_N2, stride=2 * n1c), :]], 0) for k1 in k1s]
        res = [_dot(f2_ref[...], blk) for blk in blks]
        for k1, r in zip(k1s, res):
            o_ref[k1] = r
        return carry

    lax.fori_loop(0, n1c // g2, stage2, 0, unroll=2)


def hyena_spec_fft(full, n):
    f1_real, _, _, f2, _ = _two_stage_dft_tables(n)
    nn, cols = full.shape
    n1c = nn // FFT_N2
    const = lambda shape: pl.BlockSpec(shape, lambda j: (0,) * len(shape), pipeline_mode=pl.Buffered(1))
    return pl.pallas_call(
        _spec_fft_body,
        grid=(cols // LANE,),
        in_specs=[pl.BlockSpec((nn, LANE), lambda j: (0, j)),
                  const((FFT_N2, 2 * n1c, n1c)), const((2 * FFT_N2, 2 * FFT_N2))],
        out_specs=pl.BlockSpec((n1c, 2 * FFT_N2, LANE), lambda j: (0, 0, j)),
        out_shape=jax.ShapeDtypeStruct((n1c, 2 * FFT_N2, cols), F32),
        scratch_shapes=[pltpu.VMEM((FFT_N2 * 2 * n1c, LANE), F32)],
        compiler_params=_cparams(("arbitrary",)),
        name="hyena_spec_fft",
    )(full, jnp.asarray(f1_real, BF16), jnp.asarray(f2, BF16))


def _conv_fft_body(*refs, has_mult):
    z_ref, h_ref, f1_ref, f2_ref, f2i_ref, g1_ref, bias_ref = refs[:7]
    m_ref = refs[7] if has_mult else None
    o_ref, a_ref, b_ref = refs[-3], refs[-2], refs[-1]
    n1c = h_ref.shape[0]
    n1h = n1c // 2
    n2c = FFT_N2
    n = n1h * n2c

    def slab(n2):
        return pl.ds(pl.multiple_of(n2 * 2 * n1c, 2 * n1c), 2 * n1c)

    def stage1(g, carry):
        n2s = [g * FFT_UNROLL + u for u in range(FFT_UNROLL)]
        xs = [jnp.concatenate([z_ref[pl.ds(n2, n1h, stride=n2c), :], z_ref[pl.ds(n + n2, n1h, stride=n2c), :]], 0)
              for n2 in n2s]
        res = [_dot(f1_ref[n2], x) for n2, x in zip(n2s, xs)]
        for n2, r in zip(n2s, res):
            a_ref[slab(n2), :] = r
        return carry

    lax.fori_loop(0, n2c // FFT_UNROLL, stage1, 0, unroll=2)
    g2 = FFT_UNROLL // 2

    def stage2(g, carry):
        k1s = [g * g2 + u for u in range(g2)]
        rows = [(pl.ds(k1, n2c, stride=2 * n1c), pl.ds(n1c + k1, n2c, stride=2 * n1c)) for k1 in k1s]
        blks = [jnp.concatenate([a_ref[re, :], a_ref[im, :]], 0) for re, im in rows]
        xs = [_dot(f2_ref[...], blk) for blk in blks]
        ys = []
        for k1, x in zip(k1s, xs):
            xr, xi = x[:n2c], x[n2c:]
            hr, hi_ = h_ref[k1, 0:n2c, :], h_ref[k1, n2c:2 * n2c, :]
            ys.append(jnp.concatenate([xr * hr - xi * hi_, xr * hi_ + xi * hr], 0))
        bs = [_dot(f2i_ref[...], y) for y in ys]
        for (re, im), b in zip(rows, bs):
            b_ref[re, :] = b[:n2c]
            b_ref[im, :] = b[n2c:]
        return carry

    lax.fori_loop(0, n1c // g2, stage2, 0, unroll=2)
    bias = bias_ref[...]

    def stage3(g, carry):
        n2s = [g * FFT_UNROLL + u for u in range(FFT_UNROLL)]
        blks = [b_ref[slab(n2), :] for n2 in n2s]
        ys = [_dot(g1_ref[n2], blk) for n2, blk in zip(n2s, blks)]
        outs = []
        for n2, y in zip(n2s, ys):
            for part, rows in ((y[:n1h], pl.ds(n2, n1h, stride=n2c)), (y[n1h:], pl.ds(n + n2, n1h, stride=n2c))):
                out = part + z_ref[rows, :] * bias
                if has_mult:
                    out = out * m_ref[rows, :]
                outs.append((rows, out))
        for rows, out in outs:
            o_ref[rows, :] = out
        return carry

    lax.fori_loop(0, n2c // FFT_UNROLL, stage3, 0, unroll=2)


def hyena_conv_fft(zsrc, zcol, n, n_batch, spec, bias3, layer, order, t_rows, mult=None):
    _, f1_cplx, g1, f2, f2i = _two_stage_dft_tables(n)
    n1c = 2 * n // FFT_N2
    cb = zcol // LANE
    wb = BRANCH_W // LANE
    const = lambda shape: pl.BlockSpec(shape, lambda j, p: (0,) * len(shape), pipeline_mode=pl.Buffered(1))
    in_specs = [pl.BlockSpec((2 * n, LANE), lambda j, p: (p, cb + j)),
                pl.BlockSpec((n1c, 2 * FFT_N2, LANE), lambda j, p: (0, 0, order * wb + j),
                             pipeline_mode=pl.Buffered(1)),
                const((FFT_N2, 2 * n1c, n1c)), const((2 * FFT_N2, 2 * FFT_N2)), const((2 * FFT_N2, 2 * FFT_N2)),
                const((FFT_N2, n1c, 2 * n1c)),
                pl.BlockSpec((None, 1, LANE), lambda j, p: (layer * HY_ORDER + order, 0, j))]
    args = [zsrc, spec, jnp.asarray(f1_cplx, BF16), jnp.asarray(f2, BF16), jnp.asarray(f2i, BF16),
            jnp.asarray(g1, BF16), bias3]
    if mult is not None:
        mb = mult[1] // LANE
        in_specs.append(pl.BlockSpec((2 * n, LANE), lambda j, p: (p, mb + j)))
        args.append(mult[0])
    return pl.pallas_call(
        functools.partial(_conv_fft_body, has_mult=mult is not None),
        grid=(wb, n_batch // 2),
        in_specs=in_specs,
        out_specs=pl.BlockSpec((2 * n, LANE), lambda j, p: (p, j)),
        out_shape=jax.ShapeDtypeStruct((t_rows, BRANCH_W), F32),
        scratch_shapes=[pltpu.VMEM((FFT_N2 * 2 * n1c, LANE), F32)] * 2,
        compiler_params=_cparams(("arbitrary", "arbitrary")),
        name="hyena_conv_fft",
    )(*args)


def _swap_pairs(x):
    w = x.shape[-1]
    lane = lax.broadcasted_iota(jnp.int32, x.shape, x.ndim - 1)
    return jnp.where(lane % 2 == 0, pltpu.roll(x, w - 1, x.ndim - 1), pltpu.roll(x, 1, x.ndim - 1))


def _attn_prep_body(g_ref, dq_ref, dk_ref, dv_ref, cg_ref, sg_ref, cd_ref, sd_ref, qn_ref, kn_ref,
                    qg_ref, kg_ref, vg_ref, qd_ref, kd_ref, vd_ref, *, lat_blocks):
    is_lat = pl.program_id(0) < lat_blocks
    cg = jnp.where(is_lat, cg_ref[...], 1.0)
    sg = jnp.where(is_lat, sg_ref[...], 0.0)
    cd = jnp.where(is_lat, cd_ref[...], 1.0)
    sd = jnp.where(is_lat, sd_ref[...], 0.0)

    def rope(x, cs, sn):
        return x * cs + _swap_pairs(x) * sn

    def rms(x, w):
        return x * lax.rsqrt(jnp.mean(x * x, axis=-1, keepdims=True) + EPS) * w

    for h in range(HEADS):
        sl = slice(h * HEAD_D, (h + 1) * HEAD_D)
        q = rope(rms(g_ref[:, sl].astype(F32), qn_ref[...]), cg, sg)
        qg_ref[:, sl] = (q * HEAD_D ** -0.5).astype(BF16)
        qd_ref[:, sl] = (rope(dq_ref[:, sl].astype(F32), cd, sd) * DIFF_QK ** -0.5).astype(BF16)
        kd_ref[:, sl] = rope(dk_ref[:, sl].astype(F32), cd, sd).astype(BF16)
    for h in range(GQA_KV):
        sl = slice(h * HEAD_D, (h + 1) * HEAD_D)
        kin = g_ref[:, BRANCH_W + h * HEAD_D:BRANCH_W + (h + 1) * HEAD_D].astype(F32)
        kg_ref[:, sl] = rope(rms(kin, kn_ref[...]), cg, sg).astype(BF16)
    ones = jnp.ones((g_ref.shape[0], HEAD_D), BF16)
    for h in range(GQA_KV):
        src = BRANCH_W + (GQA_KV + h) * HEAD_D
        vg_ref[:, 2 * h * HEAD_D:(2 * h + 1) * HEAD_D] = g_ref[:, src:src + HEAD_D].astype(BF16)
        vg_ref[:, (2 * h + 1) * HEAD_D:(2 * h + 2) * HEAD_D] = ones
    for h in range(HEADS):
        vd_ref[:, 2 * h * HEAD_D:(2 * h + 1) * HEAD_D] = dv_ref[:, h * HEAD_D:(h + 1) * HEAD_D].astype(BF16)
        vd_ref[:, (2 * h + 1) * HEAD_D:(2 * h + 2) * HEAD_D] = ones


def attn_prep(p, ropes, qn3, kn3, layer, n_lat, n_ctx, n_batch):
    t = p.shape[0]
    r = 256 if n_ctx % 256 == 0 else n_ctx
    nlb, ncb = n_lat // r, n_ctx // r
    lat_blocks = n_batch * nlb
    kvw = GQA_KV * HEAD_D
    w = BRANCH_W

    def kv_row(i):
        lat = (i // nlb) * (nlb + ncb) + ncb + i % nlb
        j = i - lat_blocks
        ctx = (j // ncb) * (nlb + ncb) + j % ncb
        return jnp.where(i < lat_blocks, lat, ctx)

    rope_spec = pl.BlockSpec((r, LANE), lambda i: (jnp.where(i < lat_blocks, i % nlb, 0), 0))
    nkv = n_batch * (n_lat + n_ctx)
    return pl.pallas_call(
        functools.partial(_attn_prep_body, lat_blocks=lat_blocks),
        grid=(t // r,),
        in_specs=[pl.BlockSpec((r, 2 * w), lambda i: (i, C_GQA_QKV // (2 * w))),
                  pl.BlockSpec((r, w), lambda i: (i, C_DIFF_Q // w)),
                  pl.BlockSpec((r, w), lambda i: (i, C_DIFF_K // w)),
                  pl.BlockSpec((r, w), lambda i: (i, C_DIFF_V // w)),
                  rope_spec, rope_spec, rope_spec, rope_spec,
                  pl.BlockSpec((None, 1, LANE), lambda i: (layer, 0, 0)),
                  pl.BlockSpec((None, 1, LANE), lambda i: (layer, 0, 0))],
        out_specs=[pl.BlockSpec((r, w), lambda i: (i, 0)),
                   pl.BlockSpec((r, kvw), lambda i: (kv_row(i), 0)),
                   pl.BlockSpec((r, 2 * kvw), lambda i: (kv_row(i), 0)),
                   pl.BlockSpec((r, w), lambda i: (i, 0)),
                   pl.BlockSpec((r, w), lambda i: (kv_row(i), 0)),
                   pl.BlockSpec((r, 2 * w), lambda i: (kv_row(i), 0))],
        out_shape=[jax.ShapeDtypeStruct((t, w), BF16), jax.ShapeDtypeStruct((nkv, kvw), BF16),
                   jax.ShapeDtypeStruct((nkv, 2 * kvw), BF16), jax.ShapeDtypeStruct((t, w), BF16),
                   jax.ShapeDtypeStruct((nkv, w), BF16), jax.ShapeDtypeStruct((nkv, 2 * w), BF16)],
        compiler_params=_cparams(("arbitrary",)),
        name="attn_prep",
    )(p, p, p, p, *ropes, qn3, kn3)


def _softmax_pv(s, v_ext):
    e = jnp.exp((s - jnp.max(s, axis=-1, keepdims=True)).astype(BF16))
    acc = jnp.dot(e, v_ext, preferred_element_type=F32)
    return acc[:, :HEAD_D] / acc[:, HEAD_D:HEAD_D + 1]


def _gqa_body(q_ref, k_ref, v_ref, *rest):
    o_ref = rest[-1]
    group = HEADS // GQA_KV
    for kvh in range(GQA_KV):
        k = k_ref[:, kvh * HEAD_D:(kvh + 1) * HEAD_D]
        v_ext = v_ref[:, 2 * kvh * HEAD_D:(2 * kvh + 2) * HEAD_D]
        for g in range(group):
            sl = slice((kvh * group + g) * HEAD_D, (kvh * group + g + 1) * HEAD_D)
            s = lax.dot_general(q_ref[:, sl], k, (((1,), (1,)), ((), ())), preferred_element_type=F32)
            o_ref[:, sl] = _softmax_pv(s, v_ext)


def _diff_body(q_ref, k_ref, v_ref, lam_ref, *rest, lam_init):
    o_ref = rest[-1]
    lam4 = lam_ref[...]
    lam = (jnp.exp(jnp.sum(lam4[0:1] * lam4[1:2], axis=-1, keepdims=True))
           - jnp.exp(jnp.sum(lam4[2:3] * lam4[3:4], axis=-1, keepdims=True)) + lam_init)
    dn = (((1,), (1,)), ((), ()))
    for h in range(HEADS):
        sl = slice(h * HEAD_D, (h + 1) * HEAD_D)
        q = q_ref[:, sl]
        k = k_ref[:, sl]
        v_ext = v_ref[:, 2 * h * HEAD_D:(2 * h + 2) * HEAD_D]
        first = lax.broadcasted_iota(jnp.int32, q.shape, 1) < DIFF_QK
        zero = jnp.zeros_like(q)
        o1 = _softmax_pv(lax.dot_general(jnp.where(first, q, zero), k, dn, preferred_element_type=F32), v_ext)
        o2 = _softmax_pv(lax.dot_general(jnp.where(first, zero, q), k, dn, preferred_element_type=F32), v_ext)
        o_ref[:, sl] = o1 - lam * o2


def attention(body, q, k, v, extra, extra_specs, q_row0, nq, kv_per_batch, kv_len, n_batch, tq, name, prev=None):
    t, w = q.shape
    qb0 = q_row0 // tq
    nqb = nq // tq
    kvb = kv_per_batch // kv_len
    in_specs = [pl.BlockSpec((tq, w), lambda b, i: (qb0 + b * nqb + i, 0)),
                pl.BlockSpec((kv_len, k.shape[1]), lambda b, i: (b * kvb, 0)),
                pl.BlockSpec((kv_len, v.shape[1]), lambda b, i: (b * kvb, 0))] + extra_specs
    args = [q, k, v, *extra]
    aliases = {}
    if prev is not None:
        in_specs.append(pl.BlockSpec(memory_space=pl.ANY))
        args.append(prev)
        aliases = {len(args) - 1: 0}
    return pl.pallas_call(
        body,
        grid=(n_batch, nqb),
        in_specs=in_specs,
        out_specs=pl.BlockSpec((tq, w), lambda b, i: (qb0 + b * nqb + i, 0)),
        out_shape=jax.ShapeDtypeStruct((t, w), F32),
        input_output_aliases=aliases,
        compiler_params=_cparams(("arbitrary", "arbitrary")),
        name=name,
    )(*args)


def _merge_body(h_ref, mod_ref, mg_ref, of_ref, ob_ref, ggate_ref, y1_ref, x2_ref, hgate_ref, oc_ref, cgate_ref,
                od_ref, dgate_ref, gnorm_ref, dnorm_ref, wbr_ref, wout_ref, lng_ref, lnb_ref, o_ref, *, diff_scale):
    def rms_heads(x, w):
        parts = []
        for h in range(HEADS):
            xh = x[:, h * HEAD_D:(h + 1) * HEAD_D]
            parts.append(xh * lax.rsqrt(jnp.mean(xh * xh, axis=-1, keepdims=True) + EPS) * w)
        return jnp.concatenate(parts, -1)

    ys = (rms_heads(of_ref[...] + ob_ref[...], gnorm_ref[...]) * _silu(ggate_ref[...].astype(F32)),
          x2_ref[...] * y1_ref[...] * _silu(hgate_ref[...].astype(F32)),
          oc_ref[...] * _silu(cgate_ref[...].astype(F32)),
          rms_heads(od_ref[...], dnorm_ref[...]) * diff_scale * _silu(dgate_ref[...].astype(F32)))
    acc = None
    for n in range(N_BRANCH):
        proj = jnp.dot(ys[n].astype(BF16), wbr_ref[n], preferred_element_type=F32)
        term = _sigmoid(mg_ref[:, n * D_MODEL:(n + 1) * D_MODEL].astype(F32)) * proj
        acc = term if acc is None else acc + term
    out = jnp.dot(acc.astype(BF16), wout_ref[...], preferred_element_type=F32)
    x = ALPHA * h_ref[...] + mod_ref[2:3, :] * out
    mu = jnp.mean(x, axis=-1, keepdims=True)
    xc = x - mu
    var = jnp.mean(xc * xc, axis=-1, keepdims=True)
    o_ref[...] = xc * lax.rsqrt(var + EPS) * lng_ref[...] + lnb_ref[...]


def merge_postnorm(h_all, mod3, p, o_f, o_b, y1, xv, oc, od, gnorm3, dnorm3, wbr, wout, lng3, lnb3, layer, lam_init,
                   n_lat, n_batch):
    t, d = h_all.shape
    r = 256 if n_lat % 256 == 0 else 64
    w = BRANCH_W
    lbb = n_lat // r
    row = lambda i: jnp.minimum(i // lbb, n_batch)
    tok = lambda cb: pl.BlockSpec((r, w), lambda i: (i, cb))
    vec = lambda width: pl.BlockSpec((None, 1, width), lambda i: (layer, 0, 0))
    return pl.pallas_call(
        functools.partial(_merge_body, diff_scale=1.0 - lam_init),
        grid=(t // r,),
        in_specs=[pl.BlockSpec((r, d), lambda i: (i, 0)),
                  pl.BlockSpec((None, 3, d), lambda i: (row(i), 0, 0)),
                  pl.BlockSpec((r, N_BRANCH * d), lambda i: (i, C_MERGE // (N_BRANCH * d))),
                  tok(0), tok(0), tok(C_GDN_GATE // w), tok(0), tok(1), tok(C_HY_GATE // w), tok(0),
                  tok(C_GQA_GATE // w), tok(0), tok(C_DIFF_GATE // w),
                  vec(LANE), vec(LANE),
                  pl.BlockSpec((None, N_BRANCH, w, d), lambda i: (layer, 0, 0, 0)),
                  pl.BlockSpec((None, d, d), lambda i: (layer, 0, 0)),
                  vec(d), vec(d)],
        out_specs=pl.BlockSpec((r, d), lambda i: (i, 0)),
        out_shape=jax.ShapeDtypeStruct((t, d), F32),
        compiler_params=_cparams(("arbitrary",)),
        name="merge_postnorm",
    )(h_all, mod3, p, o_f, o_b, p, y1, xv, p, oc, p, od, p, gnorm3, dnorm3, wbr, wout, lng3, lnb3)


def _rope_tables(n_lat, dim):
    rows = n_lat // GRID_W
    row = jnp.repeat(jnp.arange(rows, dtype=F32), GRID_W)
    col = jnp.tile(jnp.arange(GRID_W, dtype=F32), rows)
    half = dim // 2
    inv = ROPE_THETA ** (-jnp.arange(0, half, 2, dtype=F32) / half)
    ang = jnp.concatenate([row[:, None] * inv, col[:, None] * inv], -1)
    cos = jnp.repeat(jnp.cos(ang), 2, axis=-1)
    sin = jnp.repeat(jnp.sin(ang), 2, axis=-1)
    sign = jnp.tile(jnp.array([-1.0, 1.0], F32), dim // 2)
    reps = LANE // dim
    return jnp.tile(cos, (1, reps)), jnp.tile(sin * sign, (1, reps))


def kernel(x, c, ctx, c_ctx, w_ada, b_ada, w_in, gdn_conv, gdn_a_log, gdn_dt_bias, gdn_norm, hy_conv, hy_w1, hy_b1,
           hy_w2, hy_b2, hy_w3, hy_b3, hy_w4, hy_freq, hy_bias, gqa_qn, gqa_kn, diff_lam, diff_norm, w_br, w_out,
           ln_g, ln_b):
    nb, n_lat, d = x.shape
    n_ctx = ctx.shape[1]
    t_lat, t_ctx = nb * n_lat, nb * n_ctx
    depth = w_in.shape[0]
    w = BRANCH_W

    w_main = jnp.concatenate([w_in[:, :, O_MERGE:], w_in[:, :, :O_GDN_AB], w_in[:, :, O_GDN_AB + 4 * HEADS:O_MERGE]],
                             axis=2).astype(BF16)
    w_ab = jnp.pad(w_in[:, :, O_GDN_AB:O_GDN_AB + 4 * HEADS], ((0, 0), (0, 0), (0, LANE - 4 * HEADS)))
    wbr_bf = w_br.astype(BF16)
    wout_bf = w_out.astype(BF16)
    b_ada3 = b_ada[:, None, :]
    cvec = jnp.concatenate([c, c_ctx[None, :], jnp.zeros((SUB - nb - 1, d), F32)], 0)
    as3 = lambda a: a[:, None, :]
    gdn_par_r = jnp.pad(jnp.stack([gdn_a_log.reshape(depth, -1), gdn_dt_bias.reshape(depth, -1)], 1),
                        ((0, 0), (0, SUB - 2), (0, LANE - 2 * HEADS)))
    gdn_par_c = jnp.pad(jnp.stack([gdn_a_log.reshape(depth, -1), gdn_dt_bias.reshape(depth, -1)], 2),
                        ((0, 0), (0, 2 * HEADS), (0, LANE - 2)))
    hy_w1p = jnp.pad(hy_w1, ((0, 0), (0, LANE - HY_EMB), (0, 0)))
    hy_bias3 = hy_bias.reshape(depth * HY_ORDER, 1, w)
    ropes = _rope_tables(n_lat, HEAD_D) + _rope_tables(n_lat, DIFF_QK)

    tm = 1024 if (n_lat % 1024 == 0 and t_ctx % 1024 == 0) else n_ctx
    h_all = jnp.concatenate([x.reshape(t_lat, d), ctx.reshape(t_ctx, d)], 0)
    for l in range(depth):
        lam_init = 0.8 - 0.6 * math.exp(-0.3 * l)
        mod3 = ada_mod(cvec, w_ada, b_ada3, l).reshape(SUB, 3, d)
        p, ab = in_proj(h_all, mod3, w_main, w_ab, l, tm, n_lat // tm, nb)

        qkv = dwconv(p, gdn_conv, l, C_GDN_QKV, 3 * w, n_lat, n_ctx, nb, act=True)
        ab_rows = jnp.transpose(ab[:, :4 * HEADS].reshape(-1, GDN_CHUNK, 4 * HEADS), (0, 2, 1))
        o_f, o_b = gdn_scan(qkv, ab, ab_rows, gdn_par_r[l], gdn_par_c[l], n_lat, n_ctx, nb)

        xv = dwconv(p, hy_conv, l, C_HY_XV, 3 * w, n_lat, n_ctx, nb, act=False)
        filt = lambda n: hyena_filter(n, hy_w1p, as3(hy_b1), hy_w2, as3(hy_b2), hy_w3, as3(hy_b3), hy_w4,
                                      as3(hy_freq), l)
        spec_lat = hyena_spec_fft(filt(n_lat), n_lat)
        spec_ctx = hyena_spec_dense(filt(n_ctx), n_ctx)
        z1 = hyena_conv_fft(xv, 2 * w, n_lat, nb, spec_lat, hy_bias3, l, 0, t_lat + t_ctx, mult=(xv, 0))
        z1 = hyena_conv_dense(xv, 2 * w, t_lat, n_ctx, nb, spec_ctx, hy_bias3, l, 0, z1, mult=(xv, 0))
        y1 = hyena_conv_fft(z1, 0, n_lat, nb, spec_lat, hy_bias3, l, 1, t_lat + t_ctx)
        y1 = hyena_conv_dense(z1, 0, t_lat, n_ctx, nb, spec_ctx, hy_bias3, l, 1, y1)

        qg, kg, vg, qd, kd, vd = attn_prep(p, ropes, as3(gqa_qn), as3(gqa_kn), l, n_lat, n_ctx, nb)
        kv_all = n_lat + n_ctx
        tq = min(256, n_ctx)
        lam_spec = [pl.BlockSpec((None, 4, DIFF_QK), lambda b, i: (l, 0, 0))]
        diff_body = functools.partial(_diff_body, lam_init=lam_init)
        tq_lat = 2 * tq if n_lat % (2 * tq) == 0 else tq
        oc = attention(_gqa_body, qg, kg, vg, (), [], 0, n_lat, kv_all, kv_all, nb, tq_lat, "gqa_lat")
        oc = attention(_gqa_body, qg, kg, vg, (), [], t_lat, n_ctx, kv_all, n_ctx, nb, tq, "gqa_ctx", prev=oc)
        od = attention(diff_body, qd, kd, vd, (diff_lam,), lam_spec, 0, n_lat, kv_all, kv_all, nb, tq_lat, "diff_lat")
        od = attention(diff_body, qd, kd, vd, (diff_lam,), lam_spec, t_lat, n_ctx, kv_all, n_ctx, nb, tq, "diff_ctx",
                       prev=od)

        h_all = merge_postnorm(h_all, mod3, p, o_f, o_b, y1, xv, oc, od, as3(gdn_norm), as3(diff_norm), wbr_bf, wout_bf,
                               as3(ln_g), as3(ln_b), l, lam_init, n_lat, nb)
    return h_all[:t_lat].reshape(nb, n_lat, d)
```

```python
import functools
import math

import numpy as np
import jax
import jax.numpy as jnp
from jax import lax
from jax.experimental import pallas as pl
from jax.experimental.pallas import tpu as pltpu

F32 = jnp.float32
BF16 = jnp.bfloat16
HI = lax.Precision.HIGHEST

D_MODEL = 1024
DEPTH = 4
GRID_W = 64
BRANCH_W = D_MODEL // 2
N_BRANCH = 4
HEADS = 4
HEAD_D = BRANCH_W // HEADS
GDN_CONV = 4
GDN_CHUNK = 64
GDN_SUB = 4
HY_CONV = 3
HY_EMB = 33
HY_BANDS = (HY_EMB - 1) // 2
HY_FH = 64
HY_ORDER = 2
HY_MIN_DECAY = math.log(1e-2) / 1.5
HY_MAX_DECAY = math.log(1e-2) / 0.3
GQA_KV = 2
DIFF_QK = HEAD_D // 2
ROPE_THETA = 10000.0
EPS = 1e-6
ALPHA = (2.0 * DEPTH) ** 0.25

LANE = 128
SUB = 8
HALO = 16
FFT_N2 = 128
FFT_UNROLL = 8
VMEM_LIMIT = 60 * 1024 * 1024

C_MERGE = 0
C_GDN_QKV = 4096
C_GDN_GATE = 5632
C_HY_XV = 6144
C_HY_GATE = 7680
C_GQA_QKV = 8192
C_GQA_GATE = 9216
C_DIFF_Q = 9728
C_DIFF_K = 10240
C_DIFF_V = 10752
C_DIFF_GATE = 11264
N_MAIN = 11776
O_GDN_AB = 1536
O_MERGE = 7696


def _cparams(sem):
    return pltpu.CompilerParams(dimension_semantics=sem, vmem_limit_bytes=VMEM_LIMIT)


def _dot(a, b, hi=False):
    if hi:
        return jnp.dot(a, b, precision=HI, preferred_element_type=F32)
    return jnp.dot(a.astype(BF16), b.astype(BF16), preferred_element_type=F32)


def _dot_nt(a, b, hi=False):
    dn = (((1,), (1,)), ((), ()))
    if hi:
        return lax.dot_general(a, b, dn, precision=HI, preferred_element_type=F32)
    return lax.dot_general(a.astype(BF16), b.astype(BF16), dn, preferred_element_type=F32)


def _dot_tn(a, b):
    return lax.dot_general(a.astype(BF16), b.astype(BF16), (((0,), (0,)), ((), ())), preferred_element_type=F32)


def _sigmoid(x):
    return 1.0 / (1.0 + jnp.exp(-x))


def _silu(x):
    return x * _sigmoid(x)


def _softplus(x):
    return jnp.maximum(x, 0.0) + jnp.log1p(jnp.exp(-jnp.abs(x)))


def _ada_body(c_ref, w_ref, b_ref, o_ref):
    o_ref[...] = _dot(_silu(c_ref[...]), w_ref[...], hi=True) + b_ref[...]


def ada_mod(cvec, w_ada, b_ada3, layer):
    d = cvec.shape[1]
    tn = 512
    return pl.pallas_call(
        _ada_body,
        grid=(3 * d // tn,),
        in_specs=[pl.BlockSpec((SUB, d), lambda j: (0, 0)),
                  pl.BlockSpec((None, d, tn), lambda j: (layer, 0, j)),
                  pl.BlockSpec((None, 1, tn), lambda j: (layer, 0, j))],
        out_specs=pl.BlockSpec((SUB, tn), lambda j: (0, j)),
        out_shape=jax.ShapeDtypeStruct((SUB, 3 * d), F32),
        compiler_params=_cparams(("arbitrary",)),
        name="ada_mod",
    )(cvec, w_ada, b_ada3)


def _inproj_body(h_ref, mod_ref, w_ref, wab_ref, o_ref, ab_ref, u_ref):
    @pl.when(pl.program_id(1) == 0)
    def _():
        x = h_ref[...]
        mu = jnp.mean(x, axis=-1, keepdims=True)
        xc = x - mu
        var = jnp.mean(xc * xc, axis=-1, keepdims=True)
        u = xc * lax.rsqrt(var + EPS) * (1.0 + mod_ref[1:2, :]) + mod_ref[0:1, :]
        u_hi = u.astype(BF16)
        u_ref[...] = u_hi
        u_lo = (u - u_hi.astype(F32)).astype(BF16)
        w_ab = wab_ref[...]
        w_hi = w_ab.astype(BF16)
        w_lo = (w_ab - w_hi.astype(F32)).astype(BF16)
        ab_ref[...] = (jnp.dot(u_hi, w_hi, preferred_element_type=F32) + jnp.dot(u_hi, w_lo, preferred_element_type=F32)
                       + jnp.dot(u_lo, w_hi, preferred_element_type=F32))

    o_ref[...] = jnp.dot(u_ref[...], w_ref[...], preferred_element_type=F32).astype(BF16)


def in_proj(h_all, mod3, w_main, w_ab, layer, tm, lat_blocks_per_batch, n_batch):
    t, d = h_all.shape
    n_main = w_main.shape[2]
    tn = n_main // 4
    row = lambda i: jnp.minimum(i // lat_blocks_per_batch, n_batch)
    return pl.pallas_call(
        _inproj_body,
        grid=(t // tm, n_main // tn),
        in_specs=[pl.BlockSpec((tm, d), lambda i, j: (i, 0)),
                  pl.BlockSpec((None, 3, d), lambda i, j: (row(i), 0, 0)),
                  pl.BlockSpec((None, d, tn), lambda i, j: (layer, 0, j)),
                  pl.BlockSpec((None, d, LANE), lambda i, j: (layer, 0, 0))],
        out_specs=[pl.BlockSpec((tm, tn), lambda i, j: (i, j)),
                   pl.BlockSpec((tm, LANE), lambda i, j: (i, 0))],
        out_shape=[jax.ShapeDtypeStruct((t, n_main), BF16), jax.ShapeDtypeStruct((t, LANE), F32)],
        scratch_shapes=[pltpu.VMEM((tm, d), BF16)],
        compiler_params=_cparams(("arbitrary", "arbitrary")),
        name="in_proj",
    )(h_all, mod3, w_main, w_ab)


def _dwconv_body(xp_ref, x_ref, xn_ref, w_ref, o_ref, pad_ref, *, taps, pad_l, t_lat, n_lat, n_ctx, sb, act):
    i = pl.program_id(0)
    r = x_ref.shape[0]
    pad_ref[0:HALO, :] = xp_ref[...].astype(F32)
    pad_ref[HALO:HALO + r, :] = x_ref[...].astype(F32)
    pad_ref[HALO + r:2 * HALO + r, :] = xn_ref[...].astype(F32)
    row = lax.broadcasted_iota(jnp.int32, (sb, 1), 0)
    for k in range(r // sb):
        g0 = i * r + k * sb
        in_lat = g0 < t_lat
        starts = jnp.where(in_lat, g0 % n_lat == 0, (g0 - t_lat) % n_ctx == 0)
        ends = jnp.where(in_lat, (g0 + sb) % n_lat == 0, (g0 + sb - t_lat) % n_ctx == 0)
        acc = None
        for j in range(taps):
            d = j - pad_l
            off = HALO + k * sb + d
            xs = pad_ref[off:off + sb, :]
            if d < 0:
                xs = jnp.where(jnp.logical_and(starts, row < -d), 0.0, xs)
            elif d > 0:
                xs = jnp.where(jnp.logical_and(ends, row >= sb - d), 0.0, xs)
            term = w_ref[j:j + 1, :] * xs
            acc = term if acc is None else acc + term
        if act:
            acc = _silu(acc)
        o_ref[k * sb:(k + 1) * sb, :] = acc


def dwconv(p, w_conv, layer, col0, width, n_lat, n_ctx, n_batch, act):
    t = p.shape[0]
    taps = w_conv.shape[1]
    sb = min(256, n_ctx)
    r = 1024 if t % 1024 == 0 else sb
    lw = 512
    cb = col0 // lw
    rs = r // HALO
    body = functools.partial(_dwconv_body, taps=taps, pad_l=(taps - 1) // 2, t_lat=n_batch * n_lat, n_lat=n_lat,
                             n_ctx=n_ctx, sb=sb, act=act)
    return pl.pallas_call(
        body,
        grid=(t // r, width // lw),
        in_specs=[pl.BlockSpec((HALO, lw), lambda i, j: (jnp.maximum(i * rs - 1, 0), cb + j)),
                  pl.BlockSpec((r, lw), lambda i, j: (i, cb + j)),
                  pl.BlockSpec((HALO, lw), lambda i, j: (jnp.minimum((i + 1) * rs, t // HALO - 1), cb + j)),
                  pl.BlockSpec((None, taps, lw), lambda i, j: (layer, 0, j))],
        out_specs=pl.BlockSpec((r, lw), lambda i, j: (i, j)),
        out_shape=jax.ShapeDtypeStruct((t, width), F32),
        scratch_shapes=[pltpu.VMEM((r + 2 * HALO, lw), F32)],
        compiler_params=_cparams(("arbitrary", "arbitrary")),
        name="dwconv",
    )(p, p, p, w_conv)


def _gdn_body(qf_ref, qb_ref, abcf_ref, abcb_ref, abrf_ref, abrb_ref, pr_ref, pc_ref, of_ref, ob_ref, s_ref):
    c = GDN_CHUNK

    @pl.when(pl.program_id(1) == 0)
    def _():
        s_ref[...] = jnp.zeros_like(s_ref)

    ii = lax.broadcasted_iota(jnp.int32, (c, c), 0)
    jj = lax.broadcasted_iota(jnp.int32, (c, c), 1)
    lmat = (jj <= ii).astype(F32)
    eye = (jj == ii).astype(F32)
    alr, dtr = pr_ref[0:1, :], pr_ref[1:2, :]
    alc, dtc = pc_ref[:, 0:1], pc_ref[:, 1:2]
    chains = []
    for d in range(2):
        qkv_ref = (qf_ref, qb_ref)[d]
        abc_ref = (abcf_ref, abcb_ref)[d]
        abr_ref = (abrf_ref, abrb_ref)[d]
        incl = (jj <= ii) if d == 0 else (jj >= ii)
        strict = (jj < ii) if d == 0 else (jj > ii)
        for j in range(GDN_SUB):
            rows = slice(j * c, (j + 1) * c)
            abc = abc_ref[rows, :]
            abr = abr_ref[j]
            g_c = -jnp.exp(alr) * _softplus(abc + dtr)
            g_r = -jnp.exp(alc) * _softplus(abr + dtc)
            cum_c = _dot(lmat, g_c, hi=True)
            cum_r = _dot_nt(g_r, lmat, hi=True)
            if d == 1:
                cum_c = cum_c[c - 1:c, :] - cum_c + g_c
                cum_r = cum_r[:, c - 1:c] - cum_r + g_r
            beta_all = _sigmoid(abc)
            for h in range(HEADS):
                idx = HEADS * d + h
                q = qkv_ref[rows, h * HEAD_D:(h + 1) * HEAD_D]
                k = qkv_ref[rows, BRANCH_W + h * HEAD_D:BRANCH_W + (h + 1) * HEAD_D]
                v = qkv_ref[rows, 2 * BRANCH_W + h * HEAD_D:2 * BRANCH_W + (h + 1) * HEAD_D]
                q = q * lax.rsqrt(jnp.sum(q * q, axis=-1, keepdims=True) + EPS) * (HEAD_D ** -0.5)
                k = k * lax.rsqrt(jnp.sum(k * k, axis=-1, keepdims=True) + EPS)
                cc = cum_c[:, idx:idx + 1]
                cr = cum_r[idx:idx + 1, :]
                dec = jnp.exp(jnp.where(incl, cc - cr, -1e30))
                beta = beta_all[:, 2 * HEADS + idx:2 * HEADS + idx + 1]
                ecum = jnp.exp(cc)
                tot = cc[c - 1:c, :] if d == 0 else cc[0:1, :]
                chains.append(dict(d=d, h=h, j=j, rows=rows, q=q, k=k, dec=dec, strict=strict, beta=beta, ecum=ecum,
                                   tot=tot, rhs=jnp.concatenate([k * (beta * ecum), v * beta], 1),
                                   k_tail=k * jnp.exp(tot - cc)))
    for ch in chains:
        ch["kk"] = _dot_nt(ch["k"], ch["k"])
        ch["qk"] = _dot_nt(ch["q"], ch["k"])
    for ch in chains:
        ch["p"] = -jnp.where(ch["strict"], ch["beta"] * ch["kk"] * ch["dec"], 0.0)
        ch["inv"] = eye + ch["p"]
    for _ in range(int(math.log2(c)) - 1):
        for ch in chains:
            ch["p"] = _dot(ch["p"], ch["p"])
        for ch in chains:
            ch["inv"] = ch["inv"] + _dot(ch["inv"], ch["p"])
    for ch in chains:
        ch["wu"] = _dot(ch["inv"], ch["rhs"])
        ch["lhs"] = jnp.concatenate([ch["wu"][:, :HEAD_D], ch["q"] * ch["ecum"]], 0)
    state = {(d, h): s_ref[d, h] for d in range(2) for h in range(HEADS)}
    for step in range(GDN_SUB):
        cur = [ch for ch in chains if ch["j"] == (step if ch["d"] == 0 else GDN_SUB - 1 - step)]
        for ch in cur:
            ch["ws"] = _dot(ch["lhs"], state[ch["d"], ch["h"]])
        for ch in cur:
            ch["v_new"] = ch["wu"][:, HEAD_D:] - ch["ws"][:c]
        for ch in cur:
            out_ref = (of_ref, ob_ref)[ch["d"]]
            h = ch["h"]
            out_ref[ch["rows"], h * HEAD_D:(h + 1) * HEAD_D] = ch["ws"][c:] + _dot(ch["qk"] * ch["dec"], ch["v_new"])
            state[ch["d"], h] = state[ch["d"], h] * jnp.exp(ch["tot"]) + _dot_tn(ch["k_tail"], ch["v_new"])
    for (d, h), val in state.items():
        s_ref[d, h] = val


def gdn_scan(qkv, ab, ab_rows, par_r, par_c, n_lat, n_ctx, n_batch):
    t = qkv.shape[0]
    c = GDN_SUB * GDN_CHUNK
    nlc, ncc = n_lat // c, n_ctx // c
    base = n_batch * nlc

    def fwd(b, s):
        return jnp.where(s < ncc, base + b * ncc + s, b * nlc + (s - ncc))

    def bwd(b, s):
        return jnp.where(s < ncc, base + b * ncc + (ncc - 1 - s), b * nlc + (nlc - 1 - (s - ncc)))

    w3 = 3 * BRANCH_W
    return pl.pallas_call(
        _gdn_body,
        grid=(n_batch, ncc + nlc),
        in_specs=[pl.BlockSpec((c, w3), lambda b, s: (fwd(b, s), 0)),
                  pl.BlockSpec((c, w3), lambda b, s: (bwd(b, s), 0)),
                  pl.BlockSpec((c, LANE), lambda b, s: (fwd(b, s), 0)),
                  pl.BlockSpec((c, LANE), lambda b, s: (bwd(b, s), 0)),
                  pl.BlockSpec((GDN_SUB, 4 * HEADS, GDN_CHUNK), lambda b, s: (fwd(b, s), 0, 0)),
                  pl.BlockSpec((GDN_SUB, 4 * HEADS, GDN_CHUNK), lambda b, s: (bwd(b, s), 0, 0)),
                  pl.BlockSpec((SUB, LANE), lambda b, s: (0, 0)),
                  pl.BlockSpec((4 * HEADS, LANE), lambda b, s: (0, 0))],
        out_specs=[pl.BlockSpec((c, BRANCH_W), lambda b, s: (fwd(b, s), 0)),
                   pl.BlockSpec((c, BRANCH_W), lambda b, s: (bwd(b, s), 0))],
        out_shape=[jax.ShapeDtypeStruct((t, BRANCH_W), F32), jax.ShapeDtypeStruct((t, BRANCH_W), F32)],
        scratch_shapes=[pltpu.VMEM((2, HEADS, HEAD_D, HEAD_D), F32)],
        compiler_params=_cparams(("arbitrary", "arbitrary")),
        name="gdn_scan",
    )(qkv, qkv, ab, ab, ab_rows, ab_rows, par_r, par_c)


def _hyfilt_body(z_ref, aux_ref, w1_ref, b1_ref, w2_ref, b2_ref, w3_ref, b3_ref, w4_ref, fr_ref, dl_ref, o_ref):
    fr = fr_ref[...]
    h = jnp.sin(fr * (_dot(z_ref[...], w1_ref[...], hi=True) + b1_ref[...]))
    h = jnp.sin(fr * (_dot(h, w2_ref[...], hi=True) + b2_ref[...]))
    h = jnp.sin(fr * (_dot(h, w3_ref[...], hi=True) + b3_ref[...]))
    taps = _dot(h, w4_ref[...], hi=True) * jnp.exp(-aux_ref[:, 0:1] * dl_ref[...])
    w = BRANCH_W
    negative = aux_ref[:, 1:2] > 0.5
    keep = aux_ref[:, 2:3]
    for o in range(HY_ORDER):
        fwd = taps[:, o * 2 * w:o * 2 * w + w]
        bwd = taps[:, o * 2 * w + w:(o + 1) * 2 * w]
        o_ref[:, o * w:(o + 1) * w] = jnp.where(negative, bwd, fwd) * keep


def hyena_filter(n, w1p, b1, w2, b2, w3, b3, w4, fr, layer):
    row = jnp.arange(2 * n)
    src = jnp.where(row <= n, row, 2 * n - row)
    pos = jnp.where(row == n, 0, src).astype(F32)
    tt = pos / max(n - 1, 1)
    ang = (2.0 * math.pi / n) * pos[:, None] * jnp.linspace(1e-4, HY_BANDS - 1, HY_BANDS, dtype=F32)
    z = jnp.concatenate([tt[:, None], jnp.cos(ang), -jnp.sin(ang), jnp.zeros((2 * n, LANE - HY_EMB), F32)], -1)
    aux = jnp.stack([tt, (row > n).astype(F32), (row != n).astype(F32)], 1)
    aux = jnp.pad(aux, ((0, 0), (0, SUB - 3)))
    deltas = jnp.abs(jnp.linspace(HY_MIN_DECAY, HY_MAX_DECAY, BRANCH_W, dtype=F32))
    dl = jnp.tile(deltas, 2 * HY_ORDER)[None, :]
    r = 512
    wo = 2 * HY_ORDER * BRANCH_W
    full = lambda shape: pl.BlockSpec((None,) + shape, lambda i: (layer,) + (0,) * len(shape))
    return pl.pallas_call(
        _hyfilt_body,
        grid=(2 * n // r,),
        in_specs=[pl.BlockSpec((r, LANE), lambda i: (i, 0)),
                  pl.BlockSpec((r, SUB), lambda i: (i, 0)),
                  full((LANE, HY_FH)), full((1, HY_FH)), full((HY_FH, HY_FH)), full((1, HY_FH)),
                  full((HY_FH, HY_FH)), full((1, HY_FH)), full((HY_FH, wo)), full((1, HY_FH)),
                  pl.BlockSpec((1, wo), lambda i: (0, 0))],
        out_specs=pl.BlockSpec((r, HY_ORDER * BRANCH_W), lambda i: (i, 0)),
        out_shape=jax.ShapeDtypeStruct((2 * n, HY_ORDER * BRANCH_W), F32),
        compiler_params=_cparams(("arbitrary",)),
        name="hyena_filter",
    )(z, aux, w1p, b1, w2, b2, w3, b3, w4, fr, dl)


@functools.lru_cache(maxsize=None)
def _dense_dft_tables(n):
    nn = 2 * n
    k = np.arange(nn)[:, None].astype(np.float64)
    m = np.arange(nn)[None, :].astype(np.float64)
    ang = -2.0 * np.pi * k * m / nn
    wr, wi = np.cos(ang), np.sin(ang)
    f_real = np.concatenate([wr, wi], 0)
    wr_h, wi_h = wr[:, :n], wi[:, :n]
    f_fwd = np.block([[wr_h, -wi_h], [wi_h, wr_h]])
    cr, ci = wr.T[:n] / nn, -wi.T[:n] / nn
    f_inv = np.block([[cr, -ci], [ci, cr]])
    return (np.asarray(f_real, np.float32), np.asarray(f_fwd, np.float32), np.asarray(f_inv, np.float32))


@functools.lru_cache(maxsize=None)
def _two_stage_dft_tables(n):
    nn = 2 * n
    n2c = FFT_N2
    n1c = nn // n2c
    n1h = n1c // 2
    k1 = np.arange(n1c).astype(np.float64)
    n1 = np.arange(n1c).astype(np.float64)
    n2 = np.arange(n2c).astype(np.float64)
    ang = -2.0 * np.pi * (k1[None, :, None] * n1[None, None, :] / n1c + n2[:, None, None] * k1[None, :, None] / nn)
    mr, mi = np.cos(ang), np.sin(ang)
    f1_real = np.concatenate([mr, mi], 1)
    mrh, mih = mr[:, :, :n1h], mi[:, :, :n1h]
    f1_cplx = np.concatenate([np.concatenate([mrh, -mih], 2), np.concatenate([mih, mrh], 2)], 1)
    gr = np.transpose(mr, (0, 2, 1))[:, :n1h, :] / nn
    gi = -np.transpose(mi, (0, 2, 1))[:, :n1h, :] / nn
    g1 = np.concatenate([np.concatenate([gr, -gi], 2), np.concatenate([gi, gr], 2)], 1)
    k2 = np.arange(n2c).astype(np.float64)
    a2 = -2.0 * np.pi * k2[:, None] * n2[None, :] / n2c
    fr, fi = np.cos(a2), np.sin(a2)
    f2 = np.block([[fr, -fi], [fi, fr]])
    f2i = np.block([[fr.T, fi.T], [-fi.T, fr.T]])
    f32 = lambda a: np.asarray(a, np.float32)
    return f32(f1_real), f32(f1_cplx), f32(g1), f32(f2), f32(f2i)


def _spec_dense_body(f_ref, x_ref, o_ref):
    o_ref[...] = _dot(f_ref[...], x_ref[...], hi=True)


def hyena_spec_dense(full, n):
    f_real, _, _ = _dense_dft_tables(n)
    nn, cols = full.shape
    return pl.pallas_call(
        _spec_dense_body,
        grid=(cols // LANE,),
        in_specs=[pl.BlockSpec((2 * nn, nn), lambda j: (0, 0)),
                  pl.BlockSpec((nn, LANE), lambda j: (0, j))],
        out_specs=pl.BlockSpec((2 * nn, LANE), lambda j: (0, j)),
        out_shape=jax.ShapeDtypeStruct((2 * nn, cols), F32),
        compiler_params=_cparams(("arbitrary",)),
        name="hyena_spec_dense",
    )(jnp.asarray(f_real), full)


def _conv_dense_body(*refs, has_mult):
    z_ref, h_ref, ff_ref, fi_ref, bias_ref = refs[:5]
    m_ref = refs[5] if has_mult else None
    o_ref = refs[-1]
    z = z_ref[...]
    nn = z.shape[0]
    x = _dot(ff_ref[...], z, hi=True)
    xr, xi = x[:nn], x[nn:]
    hr, hi_ = h_ref[0:nn, :], h_ref[nn:2 * nn, :]
    y = _dot(fi_ref[...], jnp.concatenate([xr * hr - xi * hi_, xr * hi_ + xi * hr], 0), hi=True)
    out = y + z * bias_ref[...]
    if has_mult:
        out = out * m_ref[...]
    o_ref[...] = out


def hyena_conv_dense(zsrc, zcol, row0, n, n_batch, spec, bias3, layer, order, prev, mult=None):
    _, f_fwd, f_inv = _dense_dft_tables(n)
    nn = 2 * n
    rb, cb = row0 // nn, zcol // LANE
    wb = BRANCH_W // LANE
    in_specs = [pl.BlockSpec((nn, LANE), lambda p, j: (rb + p, cb + j)),
                pl.BlockSpec((2 * nn, LANE), lambda p, j: (0, order * wb + j)),
                pl.BlockSpec((2 * nn, nn), lambda p, j: (0, 0)),
                pl.BlockSpec((nn, 2 * nn), lambda p, j: (0, 0)),
                pl.BlockSpec((None, 1, LANE), lambda p, j: (layer * HY_ORDER + order, 0, j))]
    args = [zsrc, spec, jnp.asarray(f_fwd), jnp.asarray(f_inv), bias3]
    if mult is not None:
        mb = mult[1] // LANE
        in_specs.append(pl.BlockSpec((nn, LANE), lambda p, j: (rb + p, mb + j)))
        args.append(mult[0])
    in_specs.append(pl.BlockSpec(memory_space=pl.ANY))
    args.append(prev)
    return pl.pallas_call(
        functools.partial(_conv_dense_body, has_mult=mult is not None),
        grid=(n_batch // 2, wb),
        in_specs=in_specs,
        out_specs=pl.BlockSpec((nn, LANE), lambda p, j: (rb + p, j)),
        out_shape=jax.ShapeDtypeStruct(prev.shape, F32),
        input_output_aliases={len(args) - 1: 0},
        compiler_params=_cparams(("arbitrary", "arbitrary")),
        name="hyena_conv_dense",
    )(*args)


def _spec_fft_body(x_ref, f1_ref, f2_ref, o_ref, a_ref):
    n1c = o_ref.shape[0]

    def stage1(g, carry):
        n2s = [g * FFT_UNROLL + u for u in range(FFT_UNROLL)]
        xs = [x_ref[pl.ds(n2, n1c, stride=FFT_N2), :] for n2 in n2s]
        res = [_dot(f1_ref[n2], x) for n2, x in zip(n2s, xs)]
        for n2, r in zip(n2s, res):
            a_ref[pl.ds(pl.multiple_of(n2 * 2 * n1c, 2 * n1c), 2 * n1c), :] = r
        return carry

    lax.fori_loop(0, FFT_N2 // FFT_UNROLL, stage1, 0, unroll=2)
    g2 = FFT_UNROLL // 2

    def stage2(g, carry):
        k1s = [g * g2 + u for u in range(g2)]
        blks = [jnp.concatenate([a_ref[pl.ds(k1, FFT_N2, stride=2 * n1c), :],
                                 a_ref[pl.ds(n1c + k1, FFT_N2, stride=2 * n1c), :]], 0) for k1 in k1s]
        res = [_dot(f2_ref[...], blk) for blk in blks]
        for k1, r in zip(k1s, res):
            o_ref[k1] = r
        return carry

    lax.fori_loop(0, n1c // g2, stage2, 0, unroll=2)


def hyena_spec_fft(full, n):
    f1_real, _, _, f2, _ = _two_stage_dft_tables(n)
    nn, cols = full.shape
    n1c = nn // FFT_N2
    const = lambda shape: pl.BlockSpec(shape, lambda j: (0,) * len(shape), pipeline_mode=pl.Buffered(1))
    return pl.pallas_call(
        _spec_fft_body,
        grid=(cols // LANE,),
        in_specs=[pl.BlockSpec((nn, LANE), lambda j: (0, j)),
                  const((FFT_N2, 2 * n1c, n1c)), const((2 * FFT_N2, 2 * FFT_N2))],
        out_specs=pl.BlockSpec((n1c, 2 * FFT_N2, LANE), lambda j: (0, 0, j)),
        out_shape=jax.ShapeDtypeStruct((n1c, 2 * FFT_N2, cols), F32),
        scratch_shapes=[pltpu.VMEM((FFT_N2 * 2 * n1c, LANE), F32)],
        compiler_params=_cparams(("arbitrary",)),
        name="hyena_spec_fft",
    )(full, jnp.asarray(f1_real).astype(BF16), jnp.asarray(f2).astype(BF16))


def _conv_fft_body(*refs, has_mult):
    z_ref, h_ref, f1_ref, f2_ref, f2i_ref, g1_ref, bias_ref = refs[:7]
    m_ref = refs[7] if has_mult else None
    o_ref, a_ref, b_ref = refs[-3], refs[-2], refs[-1]
    n1c = h_ref.shape[0]
    n1h = n1c // 2
    n2c = FFT_N2
    n = n1h * n2c

    def slab(n2):
        return pl.ds(pl.multiple_of(n2 * 2 * n1c, 2 * n1c), 2 * n1c)

    def stage1(g, carry):
        n2s = [g * FFT_UNROLL + u for u in range(FFT_UNROLL)]
        xs = [jnp.concatenate([z_ref[pl.ds(n2, n1h, stride=n2c), :], z_ref[pl.ds(n + n2, n1h, stride=n2c), :]], 0)
              for n2 in n2s]
        res = [_dot(f1_ref[n2], x) for n2, x in zip(n2s, xs)]
        for n2, r in zip(n2s, res):
            a_ref[slab(n2), :] = r
        return carry

    lax.fori_loop(0, n2c // FFT_UNROLL, stage1, 0, unroll=2)
    g2 = FFT_UNROLL // 2

    def stage2(g, carry):
        k1s = [g * g2 + u for u in range(g2)]
        rows = [(pl.ds(k1, n2c, stride=2 * n1c), pl.ds(n1c + k1, n2c, stride=2 * n1c)) for k1 in k1s]
        blks = [jnp.concatenate([a_ref[re, :], a_ref[im, :]], 0) for re, im in rows]
        xs = [_dot(f2_ref[...], blk) for blk in blks]
        ys = []
        for k1, x in zip(k1s, xs):
            xr, xi = x[:n2c], x[n2c:]
            hr, hi_ = h_ref[k1, 0:n2c, :], h_ref[k1, n2c:2 * n2c, :]
            ys.append(jnp.concatenate([xr * hr - xi * hi_, xr * hi_ + xi * hr], 0))
        bs = [_dot(f2i_ref[...], y) for y in ys]
        for (re, im), b in zip(rows, bs):
            b_ref[re, :] = b[:n2c]
            b_ref[im, :] = b[n2c:]
        return carry

    lax.fori_loop(0, n1c // g2, stage2, 0, unroll=2)
    bias = bias_ref[...]

    def stage3(g, carry):
        n2s = [g * FFT_UNROLL + u for u in range(FFT_UNROLL)]
        blks = [b_ref[slab(n2), :] for n2 in n2s]
        ys = [_dot(g1_ref[n2], blk) for n2, blk in zip(n2s, blks)]
        outs = []
        for n2, y in zip(n2s, ys):
            for part, rows in ((y[:n1h], pl.ds(n2, n1h, stride=n2c)), (y[n1h:], pl.ds(n + n2, n1h, stride=n2c))):
                out = part + z_ref[rows, :] * bias
                if has_mult:
                    out = out * m_ref[rows, :]
                outs.append((rows, out))
        for rows, out in outs:
            o_ref[rows, :] = out
        return carry

    lax.fori_loop(0, n2c // FFT_UNROLL, stage3, 0, unroll=2)


def hyena_conv_fft(zsrc, zcol, n, n_batch, spec, bias3, layer, order, t_rows, mult=None):
    _, f1_cplx, g1, f2, f2i = _two_stage_dft_tables(n)
    n1c = 2 * n // FFT_N2
    cb = zcol // LANE
    wb = BRANCH_W // LANE
    const = lambda shape: pl.BlockSpec(shape, lambda j, p: (0,) * len(shape), pipeline_mode=pl.Buffered(1))
    in_specs = [pl.BlockSpec((2 * n, LANE), lambda j, p: (p, cb + j)),
                pl.BlockSpec((n1c, 2 * FFT_N2, LANE), lambda j, p: (0, 0, order * wb + j),
                             pipeline_mode=pl.Buffered(1)),
                const((FFT_N2, 2 * n1c, n1c)), const((2 * FFT_N2, 2 * FFT_N2)), const((2 * FFT_N2, 2 * FFT_N2)),
                const((FFT_N2, n1c, 2 * n1c)),
                pl.BlockSpec((None, 1, LANE), lambda j, p: (layer * HY_ORDER + order, 0, j))]
    args = [zsrc, spec, jnp.asarray(f1_cplx).astype(BF16), jnp.asarray(f2).astype(BF16),
            jnp.asarray(f2i).astype(BF16), jnp.asarray(g1).astype(BF16), bias3]
    if mult is not None:
        mb = mult[1] // LANE
        in_specs.append(pl.BlockSpec((2 * n, LANE), lambda j, p: (p, mb + j)))
        args.append(mult[0])
    return pl.pallas_call(
        functools.partial(_conv_fft_body, has_mult=mult is not None),
        grid=(wb, n_batch // 2),
        in_specs=in_specs,
        out_specs=pl.BlockSpec((2 * n, LANE), lambda j, p: (p, j)),
        out_shape=jax.ShapeDtypeStruct((t_rows, BRANCH_W), F32),
        scratch_shapes=[pltpu.VMEM((FFT_N2 * 2 * n1c, LANE), F32)] * 2,
        compiler_params=_cparams(("arbitrary", "arbitrary")),
        name="hyena_conv_fft",
    )(*args)


def _swap_pairs(x):
    w = x.shape[-1]
    lane = lax.broadcasted_iota(jnp.int32, x.shape, x.ndim - 1)
    return jnp.where(lane % 2 == 0, pltpu.roll(x, w - 1, x.ndim - 1), pltpu.roll(x, 1, x.ndim - 1))


def _attn_prep_body(g_ref, dq_ref, dk_ref, dv_ref, cg_ref, sg_ref, cd_ref, sd_ref, qn_ref, kn_ref,
                    qg_ref, kg_ref, vg_ref, qd_ref, kd_ref, vd_ref, *, lat_blocks):
    is_lat = pl.program_id(0) < lat_blocks
    cg = jnp.where(is_lat, cg_ref[...], 1.0)
    sg = jnp.where(is_lat, sg_ref[...], 0.0)
    cd = jnp.where(is_lat, cd_ref[...], 1.0)
    sd = jnp.where(is_lat, sd_ref[...], 0.0)

    def rope(x, cs, sn):
        return x * cs + _swap_pairs(x) * sn

    def rms(x, w):
        return x * lax.rsqrt(jnp.mean(x * x, axis=-1, keepdims=True) + EPS) * w

    for h in range(HEADS):
        sl = slice(h * HEAD_D, (h + 1) * HEAD_D)
        q = rope(rms(g_ref[:, sl].astype(F32), qn_ref[...]), cg, sg)
        qg_ref[:, sl] = (q * HEAD_D ** -0.5).astype(BF16)
        qd_ref[:, sl] = (rope(dq_ref[:, sl].astype(F32), cd, sd) * DIFF_QK ** -0.5).astype(BF16)
        kd_ref[:, sl] = rope(dk_ref[:, sl].astype(F32), cd, sd).astype(BF16)
    for h in range(GQA_KV):
        sl = slice(h * HEAD_D, (h + 1) * HEAD_D)
        kin = g_ref[:, BRANCH_W + h * HEAD_D:BRANCH_W + (h + 1) * HEAD_D].astype(F32)
        kg_ref[:, sl] = rope(rms(kin, kn_ref[...]), cg, sg).astype(BF16)
    ones = jnp.ones((g_ref.shape[0], HEAD_D), BF16)
    for h in range(GQA_KV):
        src = BRANCH_W + (GQA_KV + h) * HEAD_D
        vg_ref[:, 2 * h * HEAD_D:(2 * h + 1) * HEAD_D] = g_ref[:, src:src + HEAD_D].astype(BF16)
        vg_ref[:, (2 * h + 1) * HEAD_D:(2 * h + 2) * HEAD_D] = ones
    for h in range(HEADS):
        vd_ref[:, 2 * h * HEAD_D:(2 * h + 1) * HEAD_D] = dv_ref[:, h * HEAD_D:(h + 1) * HEAD_D].astype(BF16)
        vd_ref[:, (2 * h + 1) * HEAD_D:(2 * h + 2) * HEAD_D] = ones


def attn_prep(p, ropes, qn3, kn3, layer, n_lat, n_ctx, n_batch):
    t = p.shape[0]
    r = 256 if n_ctx % 256 == 0 else n_ctx
    nlb, ncb = n_lat // r, n_ctx // r
    lat_blocks = n_batch * nlb
    kvw = GQA_KV * HEAD_D
    w = BRANCH_W

    def kv_row(i):
        lat = (i // nlb) * (nlb + ncb) + ncb + i % nlb
        j = i - lat_blocks
        ctx = (j // ncb) * (nlb + ncb) + j % ncb
        return jnp.where(i < lat_blocks, lat, ctx)

    rope_spec = pl.BlockSpec((r, LANE), lambda i: (jnp.where(i < lat_blocks, i % nlb, 0), 0))
    nkv = n_batch * (n_lat + n_ctx)
    return pl.pallas_call(
        functools.partial(_attn_prep_body, lat_blocks=lat_blocks),
        grid=(t // r,),
        in_specs=[pl.BlockSpec((r, 2 * w), lambda i: (i, C_GQA_QKV // (2 * w))),
                  pl.BlockSpec((r, w), lambda i: (i, C_DIFF_Q // w)),
                  pl.BlockSpec((r, w), lambda i: (i, C_DIFF_K // w)),
                  pl.BlockSpec((r, w), lambda i: (i, C_DIFF_V // w)),
                  rope_spec, rope_spec, rope_spec, rope_spec,
                  pl.BlockSpec((None, 1, LANE), lambda i: (layer, 0, 0)),
                  pl.BlockSpec((None, 1, LANE), lambda i: (layer, 0, 0))],
        out_specs=[pl.BlockSpec((r, w), lambda i: (i, 0)),
                   pl.BlockSpec((r, kvw), lambda i: (kv_row(i), 0)),
                   pl.BlockSpec((r, 2 * kvw), lambda i: (kv_row(i), 0)),
                   pl.BlockSpec((r, w), lambda i: (i, 0)),
                   pl.BlockSpec((r, w), lambda i: (kv_row(i), 0)),
                   pl.BlockSpec((r, 2 * w), lambda i: (kv_row(i), 0))],
        out_shape=[jax.ShapeDtypeStruct((t, w), BF16), jax.ShapeDtypeStruct((nkv, kvw), BF16),
                   jax.ShapeDtypeStruct((nkv, 2 * kvw), BF16), jax.ShapeDtypeStruct((t, w), BF16),
                   jax.ShapeDtypeStruct((nkv, w), BF16), jax.ShapeDtypeStruct((nkv, 2 * w), BF16)],
        compiler_params=_cparams(("arbitrary",)),
        name="attn_prep",
    )(p, p, p, p, *ropes, qn3, kn3)


def _softmax_pv(s, v_ext):
    e = jnp.exp((s - jnp.max(s, axis=-1, keepdims=True)).astype(BF16))
    acc = jnp.dot(e, v_ext, preferred_element_type=F32)
    return acc[:, :HEAD_D] / acc[:, HEAD_D:HEAD_D + 1]


def _gqa_body(q_ref, k_ref, v_ref, *rest):
    o_ref = rest[-1]
    group = HEADS // GQA_KV
    for kvh in range(GQA_KV):
        k = k_ref[:, kvh * HEAD_D:(kvh + 1) * HEAD_D]
        v_ext = v_ref[:, 2 * kvh * HEAD_D:(2 * kvh + 2) * HEAD_D]
        for g in range(group):
            sl = slice((kvh * group + g) * HEAD_D, (kvh * group + g + 1) * HEAD_D)
            s = lax.dot_general(q_ref[:, sl], k, (((1,), (1,)), ((), ())), preferred_element_type=F32)
            o_ref[:, sl] = _softmax_pv(s, v_ext)


def _diff_body(q_ref, k_ref, v_ref, lam_ref, *rest, lam_init):
    o_ref = rest[-1]
    lam4 = lam_ref[...]
    lam = (jnp.exp(jnp.sum(lam4[0:1] * lam4[1:2], axis=-1, keepdims=True))
           - jnp.exp(jnp.sum(lam4[2:3] * lam4[3:4], axis=-1, keepdims=True)) + lam_init)
    dn = (((1,), (1,)), ((), ()))
    for h in range(HEADS):
        sl = slice(h * HEAD_D, (h + 1) * HEAD_D)
        q = q_ref[:, sl]
        k = k_ref[:, sl]
        v_ext = v_ref[:, 2 * h * HEAD_D:(2 * h + 2) * HEAD_D]
        first = lax.broadcasted_iota(jnp.int32, q.shape, 1) < DIFF_QK
        zero = jnp.zeros_like(q)
        o1 = _softmax_pv(lax.dot_general(jnp.where(first, q, zero), k, dn, preferred_element_type=F32), v_ext)
        o2 = _softmax_pv(lax.dot_general(jnp.where(first, zero, q), k, dn, preferred_element_type=F32), v_ext)
        o_ref[:, sl] = o1 - lam * o2


def attention(body, q, k, v, extra, extra_specs, q_row0, nq, kv_per_batch, kv_len, n_batch, tq, name, prev=None):
    t, w = q.shape
    qb0 = q_row0 // tq
    nqb = nq // tq
    kvb = kv_per_batch // kv_len
    in_specs = [pl.BlockSpec((tq, w), lambda b, i: (qb0 + b * nqb + i, 0)),
                pl.BlockSpec((kv_len, k.shape[1]), lambda b, i: (b * kvb, 0)),
                pl.BlockSpec((kv_len, v.shape[1]), lambda b, i: (b * kvb, 0))] + extra_specs
    args = [q, k, v, *extra]
    aliases = {}
    if prev is not None:
        in_specs.append(pl.BlockSpec(memory_space=pl.ANY))
        args.append(prev)
        aliases = {len(args) - 1: 0}
    return pl.pallas_call(
        body,
        grid=(n_batch, nqb),
        in_specs=in_specs,
        out_specs=pl.BlockSpec((tq, w), lambda b, i: (qb0 + b * nqb + i, 0)),
        out_shape=jax.ShapeDtypeStruct((t, w), F32),
        input_output_aliases=aliases,
        compiler_params=_cparams(("arbitrary", "arbitrary")),
        name=name,
    )(*args)


def _merge_body(h_ref, mod_ref, mg_ref, of_ref, ob_ref, ggate_ref, y1_ref, x2_ref, hgate_ref, oc_ref, cgate_ref,
                od_ref, dgate_ref, gnorm_ref, dnorm_ref, wbr_ref, wout_ref, lng_ref, lnb_ref, o_ref, *, diff_scale):
    def rms_heads(x, w):
        parts = []
        for h in range(HEADS):
            xh = x[:, h * HEAD_D:(h + 1) * HEAD_D]
            parts.append(xh * lax.rsqrt(jnp.mean(xh * xh, axis=-1, keepdims=True) + EPS) * w)
        return jnp.concatenate(parts, -1)

    ys = (rms_heads(of_ref[...] + ob_ref[...], gnorm_ref[...]) * _silu(ggate_ref[...].astype(F32)),
          x2_ref[...] * y1_ref[...] * _silu(hgate_ref[...].astype(F32)),
          oc_ref[...] * _silu(cgate_ref[...].astype(F32)),
          rms_heads(od_ref[...], dnorm_ref[...]) * diff_scale * _silu(dgate_ref[...].astype(F32)))
    acc = None
    for n in range(N_BRANCH):
        proj = jnp.dot(ys[n].astype(BF16), wbr_ref[n], preferred_element_type=F32)
        term = _sigmoid(mg_ref[:, n * D_MODEL:(n + 1) * D_MODEL].astype(F32)) * proj
        acc = term if acc is None else acc + term
    out = jnp.dot(acc.astype(BF16), wout_ref[...], preferred_element_type=F32)
    x = ALPHA * h_ref[...] + mod_ref[2:3, :] * out
    mu = jnp.mean(x, axis=-1, keepdims=True)
    xc = x - mu
    var = jnp.mean(xc * xc, axis=-1, keepdims=True)
    o_ref[...] = xc * lax.rsqrt(var + EPS) * lng_ref[...] + lnb_ref[...]


def merge_postnorm(h_all, mod3, p, o_f, o_b, y1, xv, oc, od, gnorm3, dnorm3, wbr, wout, lng3, lnb3, layer, lam_init,
                   n_lat, n_batch):
    t, d = h_all.shape
    r = 512 if (n_lat % 512 == 0 and t % 512 == 0) else 256
    w = BRANCH_W
    lbb = n_lat // r
    row = lambda i: jnp.minimum(i // lbb, n_batch)
    tok = lambda cb: pl.BlockSpec((r, w), lambda i: (i, cb))
    vec = lambda width: pl.BlockSpec((None, 1, width), lambda i: (layer, 0, 0))
    return pl.pallas_call(
        functools.partial(_merge_body, diff_scale=1.0 - lam_init),
        grid=(t // r,),
        in_specs=[pl.BlockSpec((r, d), lambda i: (i, 0)),
                  pl.BlockSpec((None, 3, d), lambda i: (row(i), 0, 0)),
                  pl.BlockSpec((r, N_BRANCH * d), lambda i: (i, C_MERGE // (N_BRANCH * d))),
                  tok(0), tok(0), tok(C_GDN_GATE // w), tok(0), tok(1), tok(C_HY_GATE // w), tok(0),
                  tok(C_GQA_GATE // w), tok(0), tok(C_DIFF_GATE // w),
                  vec(LANE), vec(LANE),
                  pl.BlockSpec((None, N_BRANCH, w, d), lambda i: (layer, 0, 0, 0), pipeline_mode=pl.Buffered(1)),
                  pl.BlockSpec((None, d, d), lambda i: (layer, 0, 0), pipeline_mode=pl.Buffered(1)),
                  vec(d), vec(d)],
        out_specs=pl.BlockSpec((r, d), lambda i: (i, 0)),
        out_shape=jax.ShapeDtypeStruct((t, d), F32),
        compiler_params=_cparams(("arbitrary",)),
        name="merge_postnorm",
    )(h_all, mod3, p, o_f, o_b, p, y1, xv, p, oc, p, od, p, gnorm3, dnorm3, wbr, wout, lng3, lnb3)


def _rope_tables(n_lat, dim):
    rows = n_lat // GRID_W
    row = jnp.repeat(jnp.arange(rows, dtype=F32), GRID_W)
    col = jnp.tile(jnp.arange(GRID_W, dtype=F32), rows)
    half = dim // 2
    inv = ROPE_THETA ** (-jnp.arange(0, half, 2, dtype=F32) / half)
    ang = jnp.concatenate([row[:, None] * inv, col[:, None] * inv], -1)
    cos = jnp.repeat(jnp.cos(ang), 2, axis=-1)
    sin = jnp.repeat(jnp.sin(ang), 2, axis=-1)
    sign = jnp.tile(jnp.array([-1.0, 1.0], F32), dim // 2)
    reps = LANE // dim
    return jnp.tile(cos, (1, reps)), jnp.tile(sin * sign, (1, reps))


def kernel(x, c, ctx, c_ctx, w_ada, b_ada, w_in, gdn_conv, gdn_a_log, gdn_dt_bias, gdn_norm, hy_conv, hy_w1, hy_b1,
           hy_w2, hy_b2, hy_w3, hy_b3, hy_w4, hy_freq, hy_bias, gqa_qn, gqa_kn, diff_lam, diff_norm, w_br, w_out,
           ln_g, ln_b):
    nb, n_lat, d = x.shape
    n_ctx = ctx.shape[1]
    t_lat, t_ctx = nb * n_lat, nb * n_ctx
    depth = w_in.shape[0]
    w = BRANCH_W

    w_main = jnp.concatenate([w_in[:, :, O_MERGE:], w_in[:, :, :O_GDN_AB], w_in[:, :, O_GDN_AB + 4 * HEADS:O_MERGE]],
                             axis=2).astype(BF16)
    w_ab = jnp.pad(w_in[:, :, O_GDN_AB:O_GDN_AB + 4 * HEADS], ((0, 0), (0, 0), (0, LANE - 4 * HEADS)))
    wbr_bf = w_br.astype(BF16)
    wout_bf = w_out.astype(BF16)
    b_ada3 = b_ada[:, None, :]
    cvec = jnp.concatenate([c, c_ctx[None, :], jnp.zeros((SUB - nb - 1, d), F32)], 0)
    as3 = lambda a: a[:, None, :]
    gdn_par_r = jnp.pad(jnp.stack([gdn_a_log.reshape(depth, -1), gdn_dt_bias.reshape(depth, -1)], 1),
                        ((0, 0), (0, SUB - 2), (0, LANE - 2 * HEADS)))
    gdn_par_c = jnp.pad(jnp.stack([gdn_a_log.reshape(depth, -1), gdn_dt_bias.reshape(depth, -1)], 2),
                        ((0, 0), (0, 2 * HEADS), (0, LANE - 2)))
    hy_w1p = jnp.pad(hy_w1, ((0, 0), (0, LANE - HY_EMB), (0, 0)))
    hy_bias3 = hy_bias.reshape(depth * HY_ORDER, 1, w)
    ropes = _rope_tables(n_lat, HEAD_D) + _rope_tables(n_lat, DIFF_QK)

    tm = 1024 if (n_lat % 1024 == 0 and t_ctx % 1024 == 0) else n_ctx
    h_all = jnp.concatenate([x.reshape(t_lat, d), ctx.reshape(t_ctx, d)], 0)
    for l in range(depth):
        lam_init = 0.8 - 0.6 * math.exp(-0.3 * l)
        mod3 = ada_mod(cvec, w_ada, b_ada3, l).reshape(SUB, 3, d)
        p, ab = in_proj(h_all, mod3, w_main, w_ab, l, tm, n_lat // tm, nb)

        qkv = dwconv(p, gdn_conv, l, C_GDN_QKV, 3 * w, n_lat, n_ctx, nb, act=True)
        ab_rows = jnp.transpose(ab[:, :4 * HEADS].reshape(-1, GDN_CHUNK, 4 * HEADS), (0, 2, 1))
        o_f, o_b = gdn_scan(qkv, ab, ab_rows, gdn_par_r[l], gdn_par_c[l], n_lat, n_ctx, nb)

        xv = dwconv(p, hy_conv, l, C_HY_XV, 3 * w, n_lat, n_ctx, nb, act=False)
        filt = lambda n: hyena_filter(n, hy_w1p, as3(hy_b1), hy_w2, as3(hy_b2), hy_w3, as3(hy_b3), hy_w4,
                                      as3(hy_freq), l)
        spec_lat = hyena_spec_fft(filt(n_lat), n_lat)
        spec_ctx = hyena_spec_dense(filt(n_ctx), n_ctx)
        z1 = hyena_conv_fft(xv, 2 * w, n_lat, nb, spec_lat, hy_bias3, l, 0, t_lat + t_ctx, mult=(xv, 0))
        z1 = hyena_conv_dense(xv, 2 * w, t_lat, n_ctx, nb, spec_ctx, hy_bias3, l, 0, z1, mult=(xv, 0))
        y1 = hyena_conv_fft(z1, 0, n_lat, nb, spec_lat, hy_bias3, l, 1, t_lat + t_ctx)
        y1 = hyena_conv_dense(z1, 0, t_lat, n_ctx, nb, spec_ctx, hy_bias3, l, 1, y1)

        qg, kg, vg, qd, kd, vd = attn_prep(p, ropes, as3(gqa_qn), as3(gqa_kn), l, n_lat, n_ctx, nb)
        kv_all = n_lat + n_ctx
        tq = min(256, n_ctx)
        lam_spec = [pl.BlockSpec((None, 4, DIFF_QK), lambda b, i: (l, 0, 0))]
        diff_body = functools.partial(_diff_body, lam_init=lam_init)
        tq_lat = 2 * tq if n_lat % (2 * tq) == 0 else tq
        oc = attention(_gqa_body, qg, kg, vg, (), [], 0, n_lat, kv_all, kv_all, nb, tq_lat, "gqa_lat")
        oc = attention(_gqa_body, qg, kg, vg, (), [], t_lat, n_ctx, kv_all, n_ctx, nb, tq, "gqa_ctx", prev=oc)
        od = attention(diff_body, qd, kd, vd, (diff_lam,), lam_spec, 0, n_lat, kv_all, kv_all, nb, tq_lat, "diff_lat")
        od = attention(diff_body, qd, kd, vd, (diff_lam,), lam_spec, t_lat, n_ctx, kv_all, n_ctx, nb, tq, "diff_ctx",
                       prev=od)

        h_all = merge_postnorm(h_all, mod3, p, o_f, o_b, y1, xv, oc, od, as3(gdn_norm), as3(diff_norm), wbr_bf, wout_bf,
                               as3(ln_g), as3(ln_b), l, lam_init, n_lat, nb)
    return h_all[:t_lat].reshape(nb, n_lat, d)
```

```python
import functools
import math

import numpy as np
import jax
import jax.numpy as jnp
from jax import lax
from jax.experimental import pallas as pl
from jax.experimental.pallas import tpu as pltpu

F32 = jnp.float32
BF16 = jnp.bfloat16
HI = lax.Precision.HIGHEST

D_MODEL = 1024
DEPTH = 4
GRID_W = 64
BRANCH_W = D_MODEL // 2
N_BRANCH = 4
HEADS = 4
HEAD_D = BRANCH_W // HEADS
GDN_CONV = 4
GDN_CHUNK = 64
GDN_SUB = 4
HY_CONV = 3
HY_EMB = 33
HY_BANDS = (HY_EMB - 1) // 2
HY_FH = 64
HY_ORDER = 2
HY_MIN_DECAY = math.log(1e-2) / 1.5
HY_MAX_DECAY = math.log(1e-2) / 0.3
GQA_KV = 2
DIFF_QK = HEAD_D // 2
KEY_CHUNK = 256
ROPE_THETA = 10000.0
EPS = 1e-6
ALPHA = (2.0 * DEPTH) ** 0.25

LANE = 128
SUB = 8
HALO = 16
FFT_N2 = 128
FFT_UNROLL = 8
VMEM_LIMIT = 60 * 1024 * 1024

C_MERGE = 0
C_GDN_QKV = 4096
C_GDN_GATE = 5632
C_HY_XV = 6144
C_HY_GATE = 7680
C_GQA_QKV = 8192
C_GQA_GATE = 9216
C_DIFF_Q = 9728
C_DIFF_K = 10240
C_DIFF_V = 10752
C_DIFF_GATE = 11264
N_MAIN = 11776
O_GDN_AB = 1536
O_MERGE = 7696


def _cparams(sem):
    return pltpu.CompilerParams(dimension_semantics=sem, vmem_limit_bytes=VMEM_LIMIT)


def _dot(a, b, hi=False):
    if hi:
        return jnp.dot(a, b, precision=HI, preferred_element_type=F32)
    return jnp.dot(a.astype(BF16), b.astype(BF16), preferred_element_type=F32)


def _dot_nt(a, b, hi=False):
    dn = (((1,), (1,)), ((), ()))
    if hi:
        return lax.dot_general(a, b, dn, precision=HI, preferred_element_type=F32)
    return lax.dot_general(a.astype(BF16), b.astype(BF16), dn, preferred_element_type=F32)


def _dot_tn(a, b):
    return lax.dot_general(a.astype(BF16), b.astype(BF16), (((0,), (0,)), ((), ())), preferred_element_type=F32)


def _sigmoid(x):
    return 1.0 / (1.0 + jnp.exp(-x))


def _silu(x):
    return x * _sigmoid(x)


def _softplus(x):
    return jnp.maximum(x, 0.0) + jnp.log1p(jnp.exp(-jnp.abs(x)))


def _ada_body(c_ref, w_ref, b_ref, o_ref):
    o_ref[...] = _dot(_silu(c_ref[...]), w_ref[...], hi=True) + b_ref[...]


def ada_mod(cvec, w_ada, b_ada3, layer):
    d = cvec.shape[1]
    tn = 512
    return pl.pallas_call(
        _ada_body,
        grid=(3 * d // tn,),
        in_specs=[pl.BlockSpec((SUB, d), lambda j: (0, 0)),
                  pl.BlockSpec((None, d, tn), lambda j: (layer, 0, j)),
                  pl.BlockSpec((None, 1, tn), lambda j: (layer, 0, j))],
        out_specs=pl.BlockSpec((SUB, tn), lambda j: (0, j)),
        out_shape=jax.ShapeDtypeStruct((SUB, 3 * d), F32),
        compiler_params=_cparams(("arbitrary",)),
        name="ada_mod",
    )(cvec, w_ada, b_ada3)


def _inproj_body(h_ref, mod_ref, w_ref, wab_ref, o_ref, ab_ref, u_ref):
    @pl.when(pl.program_id(1) == 0)
    def _():
        x = h_ref[...]
        mu = jnp.mean(x, axis=-1, keepdims=True)
        xc = x - mu
        var = jnp.mean(xc * xc, axis=-1, keepdims=True)
        u = xc * lax.rsqrt(var + EPS) * (1.0 + mod_ref[1:2, :]) + mod_ref[0:1, :]
        u_hi = u.astype(BF16)
        u_ref[...] = u_hi
        u_lo = (u - u_hi.astype(F32)).astype(BF16)
        w_ab = wab_ref[...]
        w_hi = w_ab.astype(BF16)
        w_lo = (w_ab - w_hi.astype(F32)).astype(BF16)
        ab_ref[...] = (jnp.dot(u_hi, w_hi, preferred_element_type=F32) + jnp.dot(u_hi, w_lo, preferred_element_type=F32)
                       + jnp.dot(u_lo, w_hi, preferred_element_type=F32))

    o_ref[...] = jnp.dot(u_ref[...], w_ref[...], preferred_element_type=F32).astype(BF16)


def in_proj(h_all, mod3, w_main, w_ab, layer, tm, lat_blocks_per_batch, n_batch):
    t, d = h_all.shape
    n_main = w_main.shape[2]
    tn = n_main // 4
    row = lambda i: jnp.minimum(i // lat_blocks_per_batch, n_batch)
    return pl.pallas_call(
        _inproj_body,
        grid=(t // tm, n_main // tn),
        in_specs=[pl.BlockSpec((tm, d), lambda i, j: (i, 0)),
                  pl.BlockSpec((None, 3, d), lambda i, j: (row(i), 0, 0)),
                  pl.BlockSpec((None, d, tn), lambda i, j: (layer, 0, j)),
                  pl.BlockSpec((None, d, LANE), lambda i, j: (layer, 0, 0))],
        out_specs=[pl.BlockSpec((tm, tn), lambda i, j: (i, j)),
                   pl.BlockSpec((tm, LANE), lambda i, j: (i, 0))],
        out_shape=[jax.ShapeDtypeStruct((t, n_main), BF16), jax.ShapeDtypeStruct((t, LANE), F32)],
        scratch_shapes=[pltpu.VMEM((tm, d), BF16)],
        compiler_params=_cparams(("arbitrary", "arbitrary")),
        name="in_proj",
    )(h_all, mod3, w_main, w_ab)


def _dwconv_body(xp_ref, x_ref, xn_ref, w_ref, o_ref, pad_ref, *, taps, pad_l, t_lat, n_lat, n_ctx, sb, act):
    i = pl.program_id(0)
    r = x_ref.shape[0]
    pad_ref[0:HALO, :] = xp_ref[...].astype(F32)
    pad_ref[HALO:HALO + r, :] = x_ref[...].astype(F32)
    pad_ref[HALO + r:2 * HALO + r, :] = xn_ref[...].astype(F32)
    row = lax.broadcasted_iota(jnp.int32, (sb, 1), 0)
    for k in range(r // sb):
        g0 = i * r + k * sb
        in_lat = g0 < t_lat
        starts = jnp.where(in_lat, g0 % n_lat == 0, (g0 - t_lat) % n_ctx == 0)
        ends = jnp.where(in_lat, (g0 + sb) % n_lat == 0, (g0 + sb - t_lat) % n_ctx == 0)
        acc = None
        for j in range(taps):
            d = j - pad_l
            off = HALO + k * sb + d
            xs = pad_ref[off:off + sb, :]
            if d < 0:
                xs = jnp.where(jnp.logical_and(starts, row < -d), 0.0, xs)
            elif d > 0:
                xs = jnp.where(jnp.logical_and(ends, row >= sb - d), 0.0, xs)
            term = w_ref[j:j + 1, :] * xs
            acc = term if acc is None else acc + term
        if act:
            acc = _silu(acc)
        o_ref[k * sb:(k + 1) * sb, :] = acc


def dwconv(p, w_conv, layer, col0, width, n_lat, n_ctx, n_batch, act):
    t = p.shape[0]
    taps = w_conv.shape[1]
    sb = min(256, n_ctx)
    r = 1024 if t % 1024 == 0 else sb
    lw = 512
    cb = col0 // lw
    rs = r // HALO
    body = functools.partial(_dwconv_body, taps=taps, pad_l=(taps - 1) // 2, t_lat=n_batch * n_lat, n_lat=n_lat,
                             n_ctx=n_ctx, sb=sb, act=act)
    return pl.pallas_call(
        body,
        grid=(t // r, width // lw),
        in_specs=[pl.BlockSpec((HALO, lw), lambda i, j: (jnp.maximum(i * rs - 1, 0), cb + j)),
                  pl.BlockSpec((r, lw), lambda i, j: (i, cb + j)),
                  pl.BlockSpec((HALO, lw), lambda i, j: (jnp.minimum((i + 1) * rs, t // HALO - 1), cb + j)),
                  pl.BlockSpec((None, taps, lw), lambda i, j: (layer, 0, j))],
        out_specs=pl.BlockSpec((r, lw), lambda i, j: (i, j)),
        out_shape=jax.ShapeDtypeStruct((t, width), F32),
        scratch_shapes=[pltpu.VMEM((r + 2 * HALO, lw), F32)],
        compiler_params=_cparams(("arbitrary", "arbitrary")),
        name="dwconv",
    )(p, p, p, w_conv)


def _gdn_body(qf_ref, qb_ref, abcf_ref, abcb_ref, abrf_ref, abrb_ref, pr_ref, pc_ref, of_ref, ob_ref, s_ref):
    c = GDN_CHUNK

    @pl.when(pl.program_id(1) == 0)
    def _():
        s_ref[...] = jnp.zeros_like(s_ref)

    ii = lax.broadcasted_iota(jnp.int32, (c, c), 0)
    jj = lax.broadcasted_iota(jnp.int32, (c, c), 1)
    lmat = (jj <= ii).astype(F32)
    eye = (jj == ii).astype(F32)
    alr, dtr = pr_ref[0:1, :], pr_ref[1:2, :]
    alc, dtc = pc_ref[:, 0:1], pc_ref[:, 1:2]
    chains = []
    for d in range(2):
        qkv_ref = (qf_ref, qb_ref)[d]
        abc_ref = (abcf_ref, abcb_ref)[d]
        abr_ref = (abrf_ref, abrb_ref)[d]
        incl = (jj <= ii) if d == 0 else (jj >= ii)
        strict = (jj < ii) if d == 0 else (jj > ii)
        for j in range(GDN_SUB):
            rows = slice(j * c, (j + 1) * c)
            abc = abc_ref[rows, :]
            abr = abr_ref[j]
            g_c = -jnp.exp(alr) * _softplus(abc + dtr)
            g_r = -jnp.exp(alc) * _softplus(abr + dtc)
            cum_c = _dot(lmat, g_c, hi=True)
            cum_r = _dot_nt(g_r, lmat, hi=True)
            if d == 1:
                cum_c = cum_c[c - 1:c, :] - cum_c + g_c
                cum_r = cum_r[:, c - 1:c] - cum_r + g_r
            beta_all = _sigmoid(abc)
            for h in range(HEADS):
                idx = HEADS * d + h
                q = qkv_ref[rows, h * HEAD_D:(h + 1) * HEAD_D]
                k = qkv_ref[rows, BRANCH_W + h * HEAD_D:BRANCH_W + (h + 1) * HEAD_D]
                v = qkv_ref[rows, 2 * BRANCH_W + h * HEAD_D:2 * BRANCH_W + (h + 1) * HEAD_D]
                q = q * lax.rsqrt(jnp.sum(q * q, axis=-1, keepdims=True) + EPS) * (HEAD_D ** -0.5)
                k = k * lax.rsqrt(jnp.sum(k * k, axis=-1, keepdims=True) + EPS)
                cc = cum_c[:, idx:idx + 1]
                cr = cum_r[idx:idx + 1, :]
                dec = jnp.exp(jnp.where(incl, cc - cr, -1e30))
                beta = beta_all[:, 2 * HEADS + idx:2 * HEADS + idx + 1]
                ecum = jnp.exp(cc)
                tot = cc[c - 1:c, :] if d == 0 else cc[0:1, :]
                chains.append(dict(d=d, h=h, j=j, rows=rows, q=q, k=k, dec=dec, strict=strict, beta=beta, ecum=ecum,
                                   tot=tot, rhs=jnp.concatenate([k * (beta * ecum), v * beta], 1),
                                   k_tail=k * jnp.exp(tot - cc)))
    for ch in chains:
        ch["kk"] = _dot_nt(ch["k"], ch["k"])
        ch["qk"] = _dot_nt(ch["q"], ch["k"])
    for ch in chains:
        ch["p"] = -jnp.where(ch["strict"], ch["beta"] * ch["kk"] * ch["dec"], 0.0)
        ch["inv"] = eye + ch["p"]
    for _ in range(int(math.log2(c)) - 1):
        for ch in chains:
            ch["p"] = _dot(ch["p"], ch["p"])
        for ch in chains:
            ch["inv"] = ch["inv"] + _dot(ch["inv"], ch["p"])
    for ch in chains:
        ch["wu"] = _dot(ch["inv"], ch["rhs"])
        ch["lhs"] = jnp.concatenate([ch["wu"][:, :HEAD_D], ch["q"] * ch["ecum"]], 0)
    state = {(d, h): s_ref[d, h] for d in range(2) for h in range(HEADS)}
    for step in range(GDN_SUB):
        cur = [ch for ch in chains if ch["j"] == (step if ch["d"] == 0 else GDN_SUB - 1 - step)]
        for ch in cur:
            ch["ws"] = _dot(ch["lhs"], state[ch["d"], ch["h"]])
        for ch in cur:
            ch["v_new"] = ch["wu"][:, HEAD_D:] - ch["ws"][:c]
        for ch in cur:
            out_ref = (of_ref, ob_ref)[ch["d"]]
            h = ch["h"]
            out_ref[ch["rows"], h * HEAD_D:(h + 1) * HEAD_D] = ch["ws"][c:] + _dot(ch["qk"] * ch["dec"], ch["v_new"])
            state[ch["d"], h] = state[ch["d"], h] * jnp.exp(ch["tot"]) + _dot_tn(ch["k_tail"], ch["v_new"])
    for (d, h), val in state.items():
        s_ref[d, h] = val


def gdn_scan(qkv, ab, ab_rows, par_r, par_c, n_lat, n_ctx, n_batch):
    t = qkv.shape[0]
    c = GDN_SUB * GDN_CHUNK
    nlc, ncc = n_lat // c, n_ctx // c
    base = n_batch * nlc

    def fwd(b, s):
        return jnp.where(s < ncc, base + b * ncc + s, b * nlc + (s - ncc))

    def bwd(b, s):
        return jnp.where(s < ncc, base + b * ncc + (ncc - 1 - s), b * nlc + (nlc - 1 - (s - ncc)))

    w3 = 3 * BRANCH_W
    return pl.pallas_call(
        _gdn_body,
        grid=(n_batch, ncc + nlc),
        in_specs=[pl.BlockSpec((c, w3), lambda b, s: (fwd(b, s), 0)),
                  pl.BlockSpec((c, w3), lambda b, s: (bwd(b, s), 0)),
                  pl.BlockSpec((c, LANE), lambda b, s: (fwd(b, s), 0)),
                  pl.BlockSpec((c, LANE), lambda b, s: (bwd(b, s), 0)),
                  pl.BlockSpec((GDN_SUB, 4 * HEADS, GDN_CHUNK), lambda b, s: (fwd(b, s), 0, 0)),
                  pl.BlockSpec((GDN_SUB, 4 * HEADS, GDN_CHUNK), lambda b, s: (bwd(b, s), 0, 0)),
                  pl.BlockSpec((SUB, LANE), lambda b, s: (0, 0)),
                  pl.BlockSpec((4 * HEADS, LANE), lambda b, s: (0, 0))],
        out_specs=[pl.BlockSpec((c, BRANCH_W), lambda b, s: (fwd(b, s), 0)),
                   pl.BlockSpec((c, BRANCH_W), lambda b, s: (bwd(b, s), 0))],
        out_shape=[jax.ShapeDtypeStruct((t, BRANCH_W), F32), jax.ShapeDtypeStruct((t, BRANCH_W), F32)],
        scratch_shapes=[pltpu.VMEM((2, HEADS, HEAD_D, HEAD_D), F32)],
        compiler_params=_cparams(("arbitrary", "arbitrary")),
        name="gdn_scan",
    )(qkv, qkv, ab, ab, ab_rows, ab_rows, par_r, par_c)


def _hyfilt_body(z_ref, aux_ref, w1_ref, b1_ref, w2_ref, b2_ref, w3_ref, b3_ref, w4_ref, fr_ref, dl_ref, o_ref):
    fr = fr_ref[...]
    h = jnp.sin(fr * (_dot(z_ref[...], w1_ref[...], hi=True) + b1_ref[...]))
    h = jnp.sin(fr * (_dot(h, w2_ref[...], hi=True) + b2_ref[...]))
    h = jnp.sin(fr * (_dot(h, w3_ref[...], hi=True) + b3_ref[...]))
    taps = _dot(h, w4_ref[...], hi=True) * jnp.exp(-aux_ref[:, 0:1] * dl_ref[...])
    w = BRANCH_W
    negative = aux_ref[:, 1:2] > 0.5
    keep = aux_ref[:, 2:3]
    for o in range(HY_ORDER):
        fwd = taps[:, o * 2 * w:o * 2 * w + w]
        bwd = taps[:, o * 2 * w + w:(o + 1) * 2 * w]
        o_ref[:, o * w:(o + 1) * w] = jnp.where(negative, bwd, fwd) * keep


def hyena_filter(n, w1p, b1, w2, b2, w3, b3, w4, fr, layer):
    row = jnp.arange(2 * n)
    src = jnp.where(row <= n, row, 2 * n - row)
    pos = jnp.where(row == n, 0, src).astype(F32)
    tt = pos / max(n - 1, 1)
    ang = (2.0 * math.pi / n) * pos[:, None] * jnp.linspace(1e-4, HY_BANDS - 1, HY_BANDS, dtype=F32)
    z = jnp.concatenate([tt[:, None], jnp.cos(ang), -jnp.sin(ang), jnp.zeros((2 * n, LANE - HY_EMB), F32)], -1)
    aux = jnp.stack([tt, (row > n).astype(F32), (row != n).astype(F32)], 1)
    aux = jnp.pad(aux, ((0, 0), (0, SUB - 3)))
    deltas = jnp.abs(jnp.linspace(HY_MIN_DECAY, HY_MAX_DECAY, BRANCH_W, dtype=F32))
    dl = jnp.tile(deltas, 2 * HY_ORDER)[None, :]
    r = 512
    wo = 2 * HY_ORDER * BRANCH_W
    full = lambda shape: pl.BlockSpec((None,) + shape, lambda i: (layer,) + (0,) * len(shape))
    return pl.pallas_call(
        _hyfilt_body,
        grid=(2 * n // r,),
        in_specs=[pl.BlockSpec((r, LANE), lambda i: (i, 0)),
                  pl.BlockSpec((r, SUB), lambda i: (i, 0)),
                  full((LANE, HY_FH)), full((1, HY_FH)), full((HY_FH, HY_FH)), full((1, HY_FH)),
                  full((HY_FH, HY_FH)), full((1, HY_FH)), full((HY_FH, wo)), full((1, HY_FH)),
                  pl.BlockSpec((1, wo), lambda i: (0, 0))],
        out_specs=pl.BlockSpec((r, HY_ORDER * BRANCH_W), lambda i: (i, 0)),
        out_shape=jax.ShapeDtypeStruct((2 * n, HY_ORDER * BRANCH_W), F32),
        compiler_params=_cparams(("arbitrary",)),
        name="hyena_filter",
    )(z, aux, w1p, b1, w2, b2, w3, b3, w4, fr, dl)


@functools.lru_cache(maxsize=None)
def _dense_dft_tables(n):
    nn = 2 * n
    k = np.arange(nn)[:, None].astype(np.float64)
    m = np.arange(nn)[None, :].astype(np.float64)
    ang = -2.0 * np.pi * k * m / nn
    wr, wi = np.cos(ang), np.sin(ang)
    f_real = np.concatenate([wr, wi], 0)
    wr_h, wi_h = wr[:, :n], wi[:, :n]
    f_fwd = np.block([[wr_h, -wi_h], [wi_h, wr_h]])
    cr, ci = wr.T[:n] / nn, -wi.T[:n] / nn
    f_inv = np.block([[cr, -ci], [ci, cr]])
    return (np.asarray(f_real, np.float32), np.asarray(f_fwd, np.float32), np.asarray(f_inv, np.float32))


@functools.lru_cache(maxsize=None)
def _two_stage_dft_tables(n):
    nn = 2 * n
    n2c = FFT_N2
    n1c = nn // n2c
    n1h = n1c // 2
    k1 = np.arange(n1c).astype(np.float64)
    n1 = np.arange(n1c).astype(np.float64)
    n2 = np.arange(n2c).astype(np.float64)
    ang = -2.0 * np.pi * (k1[None, :, None] * n1[None, None, :] / n1c + n2[:, None, None] * k1[None, :, None] / nn)
    mr, mi = np.cos(ang), np.sin(ang)
    f1_real = np.concatenate([mr, mi], 1)
    mrh, mih = mr[:, :, :n1h], mi[:, :, :n1h]
    f1_cplx = np.concatenate([np.concatenate([mrh, -mih], 2), np.concatenate([mih, mrh], 2)], 1)
    gr = np.transpose(mr, (0, 2, 1))[:, :n1h, :] / nn
    gi = -np.transpose(mi, (0, 2, 1))[:, :n1h, :] / nn
    g1 = np.concatenate([np.concatenate([gr, -gi], 2), np.concatenate([gi, gr], 2)], 1)
    k2 = np.arange(n2c).astype(np.float64)
    a2 = -2.0 * np.pi * k2[:, None] * n2[None, :] / n2c
    fr, fi = np.cos(a2), np.sin(a2)
    f2 = np.block([[fr, -fi], [fi, fr]])
    f2i = np.block([[fr.T, fi.T], [-fi.T, fr.T]])
    f32 = lambda a: np.asarray(a, np.float32)
    return f32(f1_real), f32(f1_cplx), f32(g1), f32(f2), f32(f2i)


def _spec_dense_body(f_ref, x_ref, o_ref):
    o_ref[...] = _dot(f_ref[...], x_ref[...], hi=True)


def hyena_spec_dense(full, n):
    f_real, _, _ = _dense_dft_tables(n)
    nn, cols = full.shape
    return pl.pallas_call(
        _spec_dense_body,
        grid=(cols // LANE,),
        in_specs=[pl.BlockSpec((2 * nn, nn), lambda j: (0, 0)),
                  pl.BlockSpec((nn, LANE), lambda j: (0, j))],
        out_specs=pl.BlockSpec((2 * nn, LANE), lambda j: (0, j)),
        out_shape=jax.ShapeDtypeStruct((2 * nn, cols), F32),
        compiler_params=_cparams(("arbitrary",)),
        name="hyena_spec_dense",
    )(jnp.asarray(f_real), full)


def _conv_dense_body(*refs, has_mult):
    z_ref, h_ref, ff_ref, fi_ref, bias_ref = refs[:5]
    m_ref = refs[5] if has_mult else None
    o_ref = refs[-1]
    z = z_ref[...]
    nn = z.shape[0]
    x = _dot(ff_ref[...], z, hi=True)
    xr, xi = x[:nn], x[nn:]
    hr, hi_ = h_ref[0:nn, :], h_ref[nn:2 * nn, :]
    y = _dot(fi_ref[...], jnp.concatenate([xr * hr - xi * hi_, xr * hi_ + xi * hr], 0), hi=True)
    out = y + z * bias_ref[...]
    if has_mult:
        out = out * m_ref[...]
    o_ref[...] = out


def hyena_conv_dense(zsrc, zcol, row0, n, n_batch, spec, bias3, layer, order, prev, mult=None):
    _, f_fwd, f_inv = _dense_dft_tables(n)
    nn = 2 * n
    rb, cb = row0 // nn, zcol // LANE
    wb = BRANCH_W // LANE
    in_specs = [pl.BlockSpec((nn, LANE), lambda p, j: (rb + p, cb + j)),
                pl.BlockSpec((2 * nn, LANE), lambda p, j: (0, order * wb + j)),
                pl.BlockSpec((2 * nn, nn), lambda p, j: (0, 0)),
                pl.BlockSpec((nn, 2 * nn), lambda p, j: (0, 0)),
                pl.BlockSpec((None, 1, LANE), lambda p, j: (layer * HY_ORDER + order, 0, j))]
    args = [zsrc, spec, jnp.asarray(f_fwd), jnp.asarray(f_inv), bias3]
    if mult is not None:
        mb = mult[1] // LANE
        in_specs.append(pl.BlockSpec((nn, LANE), lambda p, j: (rb + p, mb + j)))
        args.append(mult[0])
    in_specs.append(pl.BlockSpec(memory_space=pl.ANY))
    args.append(prev)
    return pl.pallas_call(
        functools.partial(_conv_dense_body, has_mult=mult is not None),
        grid=(n_batch // 2, wb),
        in_specs=in_specs,
        out_specs=pl.BlockSpec((nn, LANE), lambda p, j: (rb + p, j)),
        out_shape=jax.ShapeDtypeStruct(prev.shape, F32),
        input_output_aliases={len(args) - 1: 0},
        compiler_params=_cparams(("arbitrary", "arbitrary")),
        name="hyena_conv_dense",
    )(*args)


def _spec_fft_body(x_ref, f1_ref, f2_ref, o_ref, a_ref):
    n1c = o_ref.shape[0]

    def stage1(g, carry):
        n2s = [g * FFT_UNROLL + u for u in range(FFT_UNROLL)]
        xs = [x_ref[pl.ds(n2, n1c, stride=FFT_N2), :] for n2 in n2s]
        res = [_dot(f1_ref[n2], x) for n2, x in zip(n2s, xs)]
        for n2, r in zip(n2s, res):
            a_ref[pl.ds(pl.multiple_of(n2 * 2 * n1c, 2 * n1c), 2 * n1c), :] = r
        return carry

    lax.fori_loop(0, FFT_N2 // FFT_UNROLL, stage1, 0, unroll=2)
    g2 = FFT_UNROLL // 2

    def stage2(g, carry):
        k1s = [g * g2 + u for u in range(g2)]
        blks = [jnp.concatenate([a_ref[pl.ds(k1, FFT_N2, stride=2 * n1c), :],
                                 a_ref[pl.ds(n1c + k1, FFT_N2, stride=2 * n1c), :]], 0) for k1 in k1s]
        res = [_dot(f2_ref[...], blk) for blk in blks]
        for k1, r in zip(k1s, res):
            o_ref[k1] = r
        return carry

    lax.fori_loop(0, n1c // g2, stage2, 0, unroll=2)


def hyena_spec_fft(full, n):
    f1_real, _, _, f2, _ = _two_stage_dft_tables(n)
    nn, cols = full.shape
    n1c = nn // FFT_N2
    const = lambda shape: pl.BlockSpec(shape, lambda j: (0,) * len(shape), pipeline_mode=pl.Buffered(1))
    return pl.pallas_call(
        _spec_fft_body,
        grid=(cols // LANE,),
        in_specs=[pl.BlockSpec((nn, LANE), lambda j: (0, j)),
                  const((FFT_N2, 2 * n1c, n1c)), const((2 * FFT_N2, 2 * FFT_N2))],
        out_specs=pl.BlockSpec((n1c, 2 * FFT_N2, LANE), lambda j: (0, 0, j)),
        out_shape=jax.ShapeDtypeStruct((n1c, 2 * FFT_N2, cols), F32),
        scratch_shapes=[pltpu.VMEM((FFT_N2 * 2 * n1c, LANE), F32)],
        compiler_params=_cparams(("arbitrary",)),
        name="hyena_spec_fft",
    )(full, jnp.asarray(f1_real).astype(BF16), jnp.asarray(f2).astype(BF16))


def _conv_fft_body(*refs, has_mult):
    z_ref, h_ref, f1_ref, f2_ref, f2i_ref, g1_ref, bias_ref = refs[:7]
    m_ref = refs[7] if has_mult else None
    o_ref, a_ref, b_ref = refs[-3], refs[-2], refs[-1]
    n1c = h_ref.shape[0]
    n1h = n1c // 2
    n2c = FFT_N2
    n = n1h * n2c

    def slab(n2):
        return pl.ds(pl.multiple_of(n2 * 2 * n1c, 2 * n1c), 2 * n1c)

    def stage1(g, carry):
        n2s = [g * FFT_UNROLL + u for u in range(FFT_UNROLL)]
        xs = [jnp.concatenate([z_ref[pl.ds(n2, n1h, stride=n2c), :], z_ref[pl.ds(n + n2, n1h, stride=n2c), :]], 0)
              for n2 in n2s]
        res = [_dot(f1_ref[n2], x) for n2, x in zip(n2s, xs)]
        for n2, r in zip(n2s, res):
            a_ref[slab(n2), :] = r
        return carry

    lax.fori_loop(0, n2c // FFT_UNROLL, stage1, 0, unroll=2)
    g2 = FFT_UNROLL // 2

    def stage2(g, carry):
        k1s = [g * g2 + u for u in range(g2)]
        rows = [(pl.ds(k1, n2c, stride=2 * n1c), pl.ds(n1c + k1, n2c, stride=2 * n1c)) for k1 in k1s]
        blks = [jnp.concatenate([a_ref[re, :], a_ref[im, :]], 0) for re, im in rows]
        xs = [_dot(f2_ref[...], blk) for blk in blks]
        ys = []
        for k1, x in zip(k1s, xs):
            xr, xi = x[:n2c], x[n2c:]
            hr, hi_ = h_ref[k1, 0:n2c, :], h_ref[k1, n2c:2 * n2c, :]
            ys.append(jnp.concatenate([xr * hr - xi * hi_, xr * hi_ + xi * hr], 0))
        bs = [_dot(f2i_ref[...], y) for y in ys]
        for (re, im), b in zip(rows, bs):
            b_ref[re, :] = b[:n2c]
            b_ref[im, :] = b[n2c:]
        return carry

    lax.fori_loop(0, n1c // g2, stage2, 0, unroll=2)
    bias = bias_ref[...]

    def stage3(g, carry):
        n2s = [g * FFT_UNROLL + u for u in range(FFT_UNROLL)]
        blks = [b_ref[slab(n2), :] for n2 in n2s]
        ys = [_dot(g1_ref[n2], blk) for n2, blk in zip(n2s, blks)]
        outs = []
        for n2, y in zip(n2s, ys):
            for part, rows in ((y[:n1h], pl.ds(n2, n1h, stride=n2c)), (y[n1h:], pl.ds(n + n2, n1h, stride=n2c))):
                out = part + z_ref[rows, :] * bias
                if has_mult:
                    out = out * m_ref[rows, :]
                outs.append((rows, out))
        for rows, out in outs:
            o_ref[rows, :] = out
        return carry

    lax.fori_loop(0, n2c // FFT_UNROLL, stage3, 0, unroll=2)


def hyena_conv_fft(zsrc, zcol, n, n_batch, spec, bias3, layer, order, t_rows, mult=None):
    _, f1_cplx, g1, f2, f2i = _two_stage_dft_tables(n)
    n1c = 2 * n // FFT_N2
    cb = zcol // LANE
    wb = BRANCH_W // LANE
    const = lambda shape: pl.BlockSpec(shape, lambda j, p: (0,) * len(shape), pipeline_mode=pl.Buffered(1))
    in_specs = [pl.BlockSpec((2 * n, LANE), lambda j, p: (p, cb + j)),
                pl.BlockSpec((n1c, 2 * FFT_N2, LANE), lambda j, p: (0, 0, order * wb + j),
                             pipeline_mode=pl.Buffered(1)),
                const((FFT_N2, 2 * n1c, n1c)), const((2 * FFT_N2, 2 * FFT_N2)), const((2 * FFT_N2, 2 * FFT_N2)),
                const((FFT_N2, n1c, 2 * n1c)),
                pl.BlockSpec((None, 1, LANE), lambda j, p: (layer * HY_ORDER + order, 0, j))]
    args = [zsrc, spec, jnp.asarray(f1_cplx).astype(BF16), jnp.asarray(f2).astype(BF16),
            jnp.asarray(f2i).astype(BF16), jnp.asarray(g1).astype(BF16), bias3]
    if mult is not None:
        mb = mult[1] // LANE
        in_specs.append(pl.BlockSpec((2 * n, LANE), lambda j, p: (p, mb + j)))
        args.append(mult[0])
    return pl.pallas_call(
        functools.partial(_conv_fft_body, has_mult=mult is not None),
        grid=(wb, n_batch // 2),
        in_specs=in_specs,
        out_specs=pl.BlockSpec((2 * n, LANE), lambda j, p: (p, j)),
        out_shape=jax.ShapeDtypeStruct((t_rows, BRANCH_W), F32),
        scratch_shapes=[pltpu.VMEM((FFT_N2 * 2 * n1c, LANE), F32)] * 2,
        compiler_params=_cparams(("arbitrary", "arbitrary")),
        name="hyena_conv_fft",
    )(*args)


def _swap_pairs(x):
    w = x.shape[-1]
    lane = lax.broadcasted_iota(jnp.int32, x.shape, x.ndim - 1)
    return jnp.where(lane % 2 == 0, pltpu.roll(x, w - 1, x.ndim - 1), pltpu.roll(x, 1, x.ndim - 1))


def _attn_prep_body(g_ref, dq_ref, dk_ref, dv_ref, cg_ref, sg_ref, cd_ref, sd_ref, qn_ref, kn_ref,
                    qg_ref, kg_ref, vg_ref, qd_ref, kd_ref, vd_ref, *, lat_blocks):
    is_lat = pl.program_id(0) < lat_blocks
    cg = jnp.where(is_lat, cg_ref[...], 1.0)
    sg = jnp.where(is_lat, sg_ref[...], 0.0)
    cd = jnp.where(is_lat, cd_ref[...], 1.0)
    sd = jnp.where(is_lat, sd_ref[...], 0.0)

    def rope(x, cs, sn):
        return x * cs + _swap_pairs(x) * sn

    def rms(x, w):
        return x * lax.rsqrt(jnp.mean(x * x, axis=-1, keepdims=True) + EPS) * w

    for h in range(HEADS):
        sl = slice(h * HEAD_D, (h + 1) * HEAD_D)
        q = rope(rms(g_ref[:, sl].astype(F32), qn_ref[...]), cg, sg)
        qg_ref[:, sl] = (q * HEAD_D ** -0.5).astype(BF16)
        qd_ref[:, sl] = (rope(dq_ref[:, sl].astype(F32), cd, sd) * DIFF_QK ** -0.5).astype(BF16)
        kd_ref[:, sl] = rope(dk_ref[:, sl].astype(F32), cd, sd).astype(BF16)
    for h in range(GQA_KV):
        sl = slice(h * HEAD_D, (h + 1) * HEAD_D)
        kin = g_ref[:, BRANCH_W + h * HEAD_D:BRANCH_W + (h + 1) * HEAD_D].astype(F32)
        kg_ref[:, sl] = rope(rms(kin, kn_ref[...]), cg, sg).astype(BF16)
    ones = jnp.ones((g_ref.shape[0], HEAD_D), BF16)
    for h in range(GQA_KV):
        src = BRANCH_W + (GQA_KV + h) * HEAD_D
        vg_ref[:, 2 * h * HEAD_D:(2 * h + 1) * HEAD_D] = g_ref[:, src:src + HEAD_D].astype(BF16)
        vg_ref[:, (2 * h + 1) * HEAD_D:(2 * h + 2) * HEAD_D] = ones
    for h in range(HEADS):
        vd_ref[:, 2 * h * HEAD_D:(2 * h + 1) * HEAD_D] = dv_ref[:, h * HEAD_D:(h + 1) * HEAD_D].astype(BF16)
        vd_ref[:, (2 * h + 1) * HEAD_D:(2 * h + 2) * HEAD_D] = ones


def attn_prep(p, ropes, qn3, kn3, layer, n_lat, n_ctx, n_batch):
    t = p.shape[0]
    r = 256 if n_ctx % 256 == 0 else n_ctx
    nlb, ncb = n_lat // r, n_ctx // r
    lat_blocks = n_batch * nlb
    kvw = GQA_KV * HEAD_D
    w = BRANCH_W

    def kv_row(i):
        lat = (i // nlb) * (nlb + ncb) + ncb + i % nlb
        j = i - lat_blocks
        ctx = (j // ncb) * (nlb + ncb) + j % ncb
        return jnp.where(i < lat_blocks, lat, ctx)

    rope_spec = pl.BlockSpec((r, LANE), lambda i: (jnp.where(i < lat_blocks, i % nlb, 0), 0))
    nkv = n_batch * (n_lat + n_ctx)
    return pl.pallas_call(
        functools.partial(_attn_prep_body, lat_blocks=lat_blocks),
        grid=(t // r,),
        in_specs=[pl.BlockSpec((r, 2 * w), lambda i: (i, C_GQA_QKV // (2 * w))),
                  pl.BlockSpec((r, w), lambda i: (i, C_DIFF_Q // w)),
                  pl.BlockSpec((r, w), lambda i: (i, C_DIFF_K // w)),
                  pl.BlockSpec((r, w), lambda i: (i, C_DIFF_V // w)),
                  rope_spec, rope_spec, rope_spec, rope_spec,
                  pl.BlockSpec((None, 1, LANE), lambda i: (layer, 0, 0)),
                  pl.BlockSpec((None, 1, LANE), lambda i: (layer, 0, 0))],
        out_specs=[pl.BlockSpec((r, w), lambda i: (i, 0)),
                   pl.BlockSpec((r, kvw), lambda i: (kv_row(i), 0)),
                   pl.BlockSpec((r, 2 * kvw), lambda i: (kv_row(i), 0)),
                   pl.BlockSpec((r, w), lambda i: (i, 0)),
                   pl.BlockSpec((r, w), lambda i: (kv_row(i), 0)),
                   pl.BlockSpec((r, 2 * w), lambda i: (kv_row(i), 0))],
        out_shape=[jax.ShapeDtypeStruct((t, w), BF16), jax.ShapeDtypeStruct((nkv, kvw), BF16),
                   jax.ShapeDtypeStruct((nkv, 2 * kvw), BF16), jax.ShapeDtypeStruct((t, w), BF16),
                   jax.ShapeDtypeStruct((nkv, w), BF16), jax.ShapeDtypeStruct((nkv, 2 * w), BF16)],
        compiler_params=_cparams(("arbitrary",)),
        name="attn_prep",
    )(p, p, p, p, *ropes, qn3, kn3)


def _chunked_attend(q, k_ref, kcols, v_ref, vcols):
    tq = q.shape[0]
    dn = (((1,), (1,)), ((), ()))

    def body(c, carry):
        m, acc = carry
        rows = pl.ds(pl.multiple_of(c * KEY_CHUNK, KEY_CHUNK), KEY_CHUNK)
        s = lax.dot_general(q, k_ref[rows, kcols], dn, preferred_element_type=F32)
        m_new = jnp.maximum(m, jnp.max(s, axis=-1, keepdims=True))
        e = jnp.exp((s - m_new).astype(BF16))
        acc = acc * jnp.exp(m - m_new) + jnp.dot(e, v_ref[rows, vcols], preferred_element_type=F32)
        return m_new, acc

    init = (jnp.full((tq, 1), -1e30, F32), jnp.zeros((tq, 2 * HEAD_D), F32))
    _, acc = lax.fori_loop(0, k_ref.shape[0] // KEY_CHUNK, body, init)
    return acc[:, :HEAD_D] / acc[:, HEAD_D:HEAD_D + 1]


def _gqa_body(q_ref, k_ref, v_ref, *rest):
    o_ref = rest[-1]
    group = HEADS // GQA_KV
    for kvh in range(GQA_KV):
        kcols = slice(kvh * HEAD_D, (kvh + 1) * HEAD_D)
        vcols = slice(2 * kvh * HEAD_D, (2 * kvh + 2) * HEAD_D)
        for g in range(group):
            sl = slice((kvh * group + g) * HEAD_D, (kvh * group + g + 1) * HEAD_D)
            o_ref[:, sl] = _chunked_attend(q_ref[:, sl], k_ref, kcols, v_ref, vcols)


def _diff_body(q_ref, k_ref, v_ref, lam_ref, *rest, lam_init):
    o_ref = rest[-1]
    lam4 = lam_ref[...]
    lam = (jnp.exp(jnp.sum(lam4[0:1] * lam4[1:2], axis=-1, keepdims=True))
           - jnp.exp(jnp.sum(lam4[2:3] * lam4[3:4], axis=-1, keepdims=True)) + lam_init)
    for h in range(HEADS):
        sl = slice(h * HEAD_D, (h + 1) * HEAD_D)
        q = q_ref[:, sl]
        vcols = slice(2 * h * HEAD_D, (2 * h + 2) * HEAD_D)
        first = lax.broadcasted_iota(jnp.int32, q.shape, 1) < DIFF_QK
        zero = jnp.zeros_like(q)
        o1 = _chunked_attend(jnp.where(first, q, zero), k_ref, sl, v_ref, vcols)
        o2 = _chunked_attend(jnp.where(first, zero, q), k_ref, sl, v_ref, vcols)
        o_ref[:, sl] = o1 - lam * o2


def attention(body, q, k, v, extra, extra_specs, q_row0, nq, kv_per_batch, kv_len, n_batch, tq, name, prev=None):
    t, w = q.shape
    qb0 = q_row0 // tq
    nqb = nq // tq
    kvb = kv_per_batch // kv_len
    in_specs = [pl.BlockSpec((tq, w), lambda b, i: (qb0 + b * nqb + i, 0)),
                pl.BlockSpec((kv_len, k.shape[1]), lambda b, i: (b * kvb, 0)),
                pl.BlockSpec((kv_len, v.shape[1]), lambda b, i: (b * kvb, 0))] + extra_specs
    args = [q, k, v, *extra]
    aliases = {}
    if prev is not None:
        in_specs.append(pl.BlockSpec(memory_space=pl.ANY))
        args.append(prev)
        aliases = {len(args) - 1: 0}
    return pl.pallas_call(
        body,
        grid=(n_batch, nqb),
        in_specs=in_specs,
        out_specs=pl.BlockSpec((tq, w), lambda b, i: (qb0 + b * nqb + i, 0)),
        out_shape=jax.ShapeDtypeStruct((t, w), F32),
        input_output_aliases=aliases,
        compiler_params=_cparams(("arbitrary", "arbitrary")),
        name=name,
    )(*args)


def _merge_body(h_ref, mod_ref, mg_ref, of_ref, ob_ref, ggate_ref, y1_ref, x2_ref, hgate_ref, oc_ref, cgate_ref,
                od_ref, dgate_ref, gnorm_ref, dnorm_ref, wbr_ref, wout_ref, lng_ref, lnb_ref, o_ref, *, diff_scale):
    def rms_heads(x, w):
        parts = []
        for h in range(HEADS):
            xh = x[:, h * HEAD_D:(h + 1) * HEAD_D]
            parts.append(xh * lax.rsqrt(jnp.mean(xh * xh, axis=-1, keepdims=True) + EPS) * w)
        return jnp.concatenate(parts, -1)

    ys = (rms_heads(of_ref[...] + ob_ref[...], gnorm_ref[...]) * _silu(ggate_ref[...].astype(F32)),
          x2_ref[...] * y1_ref[...] * _silu(hgate_ref[...].astype(F32)),
          oc_ref[...] * _silu(cgate_ref[...].astype(F32)),
          rms_heads(od_ref[...], dnorm_ref[...]) * diff_scale * _silu(dgate_ref[...].astype(F32)))
    acc = None
    for n in range(N_BRANCH):
        proj = jnp.dot(ys[n].astype(BF16), wbr_ref[n], preferred_element_type=F32)
        term = _sigmoid(mg_ref[:, n * D_MODEL:(n + 1) * D_MODEL].astype(F32)) * proj
        acc = term if acc is None else acc + term
    out = jnp.dot(acc.astype(BF16), wout_ref[...], preferred_element_type=F32)
    x = ALPHA * h_ref[...] + mod_ref[2:3, :] * out
    mu = jnp.mean(x, axis=-1, keepdims=True)
    xc = x - mu
    var = jnp.mean(xc * xc, axis=-1, keepdims=True)
    o_ref[...] = xc * lax.rsqrt(var + EPS) * lng_ref[...] + lnb_ref[...]


def merge_postnorm(h_all, mod3, p, o_f, o_b, y1, xv, oc, od, gnorm3, dnorm3, wbr, wout, lng3, lnb3, layer, lam_init,
                   n_lat, n_batch):
    t, d = h_all.shape
    r = 512 if (n_lat % 512 == 0 and t % 512 == 0) else 256
    w = BRANCH_W
    lbb = n_lat // r
    row = lambda i: jnp.minimum(i // lbb, n_batch)
    tok = lambda cb: pl.BlockSpec((r, w), lambda i: (i, cb))
    vec = lambda width: pl.BlockSpec((None, 1, width), lambda i: (layer, 0, 0))
    return pl.pallas_call(
        functools.partial(_merge_body, diff_scale=1.0 - lam_init),
        grid=(t // r,),
        in_specs=[pl.BlockSpec((r, d), lambda i: (i, 0)),
                  pl.BlockSpec((None, 3, d), lambda i: (row(i), 0, 0)),
                  pl.BlockSpec((r, N_BRANCH * d), lambda i: (i, C_MERGE // (N_BRANCH * d))),
                  tok(0), tok(0), tok(C_GDN_GATE // w), tok(0), tok(1), tok(C_HY_GATE // w), tok(0),
                  tok(C_GQA_GATE // w), tok(0), tok(C_DIFF_GATE // w),
                  vec(LANE), vec(LANE),
                  pl.BlockSpec((None, N_BRANCH, w, d), lambda i: (layer, 0, 0, 0), pipeline_mode=pl.Buffered(1)),
                  pl.BlockSpec((None, d, d), lambda i: (layer, 0, 0), pipeline_mode=pl.Buffered(1)),
                  vec(d), vec(d)],
        out_specs=pl.BlockSpec((r, d), lambda i: (i, 0)),
        out_shape=jax.ShapeDtypeStruct((t, d), F32),
        compiler_params=_cparams(("arbitrary",)),
        name="merge_postnorm",
    )(h_all, mod3, p, o_f, o_b, p, y1, xv, p, oc, p, od, p, gnorm3, dnorm3, wbr, wout, lng3, lnb3)


def _rope_tables(n_lat, dim):
    rows = n_lat // GRID_W
    row = jnp.repeat(jnp.arange(rows, dtype=F32), GRID_W)
    col = jnp.tile(jnp.arange(GRID_W, dtype=F32), rows)
    half = dim // 2
    inv = ROPE_THETA ** (-jnp.arange(0, half, 2, dtype=F32) / half)
    ang = jnp.concatenate([row[:, None] * inv, col[:, None] * inv], -1)
    cos = jnp.repeat(jnp.cos(ang), 2, axis=-1)
    sin = jnp.repeat(jnp.sin(ang), 2, axis=-1)
    sign = jnp.tile(jnp.array([-1.0, 1.0], F32), dim // 2)
    reps = LANE // dim
    return jnp.tile(cos, (1, reps)), jnp.tile(sin * sign, (1, reps))


def kernel(x, c, ctx, c_ctx, w_ada, b_ada, w_in, gdn_conv, gdn_a_log, gdn_dt_bias, gdn_norm, hy_conv, hy_w1, hy_b1,
           hy_w2, hy_b2, hy_w3, hy_b3, hy_w4, hy_freq, hy_bias, gqa_qn, gqa_kn, diff_lam, diff_norm, w_br, w_out,
           ln_g, ln_b):
    nb, n_lat, d = x.shape
    n_ctx = ctx.shape[1]
    t_lat, t_ctx = nb * n_lat, nb * n_ctx
    depth = w_in.shape[0]
    w = BRANCH_W

    w_main = jnp.concatenate([w_in[:, :, O_MERGE:], w_in[:, :, :O_GDN_AB], w_in[:, :, O_GDN_AB + 4 * HEADS:O_MERGE]],
                             axis=2).astype(BF16)
    w_ab = jnp.pad(w_in[:, :, O_GDN_AB:O_GDN_AB + 4 * HEADS], ((0, 0), (0, 0), (0, LANE - 4 * HEADS)))
    wbr_bf = w_br.astype(BF16)
    wout_bf = w_out.astype(BF16)
    b_ada3 = b_ada[:, None, :]
    cvec = jnp.concatenate([c, c_ctx[None, :], jnp.zeros((SUB - nb - 1, d), F32)], 0)
    as3 = lambda a: a[:, None, :]
    gdn_par_r = jnp.pad(jnp.stack([gdn_a_log.reshape(depth, -1), gdn_dt_bias.reshape(depth, -1)], 1),
                        ((0, 0), (0, SUB - 2), (0, LANE - 2 * HEADS)))
    gdn_par_c = jnp.pad(jnp.stack([gdn_a_log.reshape(depth, -1), gdn_dt_bias.reshape(depth, -1)], 2),
                        ((0, 0), (0, 2 * HEADS), (0, LANE - 2)))
    hy_w1p = jnp.pad(hy_w1, ((0, 0), (0, LANE - HY_EMB), (0, 0)))
    hy_bias3 = hy_bias.reshape(depth * HY_ORDER, 1, w)
    ropes = _rope_tables(n_lat, HEAD_D) + _rope_tables(n_lat, DIFF_QK)

    tm = 1024 if (n_lat % 1024 == 0 and t_ctx % 1024 == 0) else n_ctx
    h_all = jnp.concatenate([x.reshape(t_lat, d), ctx.reshape(t_ctx, d)], 0)
    for l in range(depth):
        lam_init = 0.8 - 0.6 * math.exp(-0.3 * l)
        mod3 = ada_mod(cvec, w_ada, b_ada3, l).reshape(SUB, 3, d)
        p, ab = in_proj(h_all, mod3, w_main, w_ab, l, tm, n_lat // tm, nb)

        qkv = dwconv(p, gdn_conv, l, C_GDN_QKV, 3 * w, n_lat, n_ctx, nb, act=True)
        ab_rows = jnp.transpose(ab[:, :4 * HEADS].reshape(-1, GDN_CHUNK, 4 * HEADS), (0, 2, 1))
        o_f, o_b = gdn_scan(qkv, ab, ab_rows, gdn_par_r[l], gdn_par_c[l], n_lat, n_ctx, nb)

        xv = dwconv(p, hy_conv, l, C_HY_XV, 3 * w, n_lat, n_ctx, nb, act=False)
        filt = lambda n: hyena_filter(n, hy_w1p, as3(hy_b1), hy_w2, as3(hy_b2), hy_w3, as3(hy_b3), hy_w4,
                                      as3(hy_freq), l)
        spec_lat = hyena_spec_fft(filt(n_lat), n_lat)
        spec_ctx = hyena_spec_dense(filt(n_ctx), n_ctx)
        z1 = hyena_conv_fft(xv, 2 * w, n_lat, nb, spec_lat, hy_bias3, l, 0, t_lat + t_ctx, mult=(xv, 0))
        z1 = hyena_conv_dense(xv, 2 * w, t_lat, n_ctx, nb, spec_ctx, hy_bias3, l, 0, z1, mult=(xv, 0))
        y1 = hyena_conv_fft(z1, 0, n_lat, nb, spec_lat, hy_bias3, l, 1, t_lat + t_ctx)
        y1 = hyena_conv_dense(z1, 0, t_lat, n_ctx, nb, spec_ctx, hy_bias3, l, 1, y1)

        qg, kg, vg, qd, kd, vd = attn_prep(p, ropes, as3(gqa_qn), as3(gqa_kn), l, n_lat, n_ctx, nb)
        kv_all = n_lat + n_ctx
        tq = min(256, n_ctx)
        lam_spec = [pl.BlockSpec((None, 4, DIFF_QK), lambda b, i: (l, 0, 0))]
        diff_body = functools.partial(_diff_body, lam_init=lam_init)
        tq_lat = 2 * tq if n_lat % (2 * tq) == 0 else tq
        oc = attention(_gqa_body, qg, kg, vg, (), [], 0, n_lat, kv_all, kv_all, nb, tq_lat, "gqa_lat")
        oc = attention(_gqa_body, qg, kg, vg, (), [], t_lat, n_ctx, kv_all, n_ctx, nb, tq, "gqa_ctx", prev=oc)
        od = attention(diff_body, qd, kd, vd, (diff_lam,), lam_spec, 0, n_lat, kv_all, kv_all, nb, tq_lat, "diff_lat")
        od = attention(diff_body, qd, kd, vd, (diff_lam,), lam_spec, t_lat, n_ctx, kv_all, n_ctx, nb, tq, "diff_ctx",
                       prev=od)

        h_all = merge_postnorm(h_all, mod3, p, o_f, o_b, y1, xv, oc, od, as3(gdn_norm), as3(diff_norm), wbr_bf, wout_bf,
                               as3(ln_g), as3(ln_b), l, lam_init, n_lat, nb)
    return h_all[:t_lat].reshape(nb, n_lat, d)
```
